```python
import math
import jax, jax.numpy as jnp
from jax import lax
import numpy as np

D_MODEL = 1024
BATCH = 8
SEQ = 16384
DEPTH = 2

N_EVEN = (DEPTH + 1) // 2
N_ODD = DEPTH // 2
EPS = 1e-6

GM_HEADS = 4
GM_WIDTH = D_MODEL
GM_HEAD_DIM = GM_WIDTH // GM_HEADS
GM_CHUNK = 128

SSM_WIDTH = D_MODEL
SSM_HEAD_DIM = 64
SSM_HEADS = SSM_WIDTH // SSM_HEAD_DIM
SSM_GROUPS = 4
SSM_HEADS_PER_GROUP = SSM_HEADS // SSM_GROUPS
SSM_STATE = 128
SSM_CONV = 4
SSM_CHUNK = 128
SSM_CONV_DIM = SSM_WIDTH + 2 * SSM_GROUPS * SSM_STATE
DT_MIN = 0.001
DT_MAX = 0.1
DT_FLOOR = 1e-4

IN_PROJ_DIM = 2 * GM_WIDTH + SSM_WIDTH + SSM_CONV_DIM + SSM_HEADS
SPLIT_POINTS = (GM_WIDTH, 2 * GM_WIDTH, 2 * GM_WIDTH + SSM_WIDTH, 2 * GM_WIDTH + SSM_WIDTH + SSM_CONV_DIM)
MIX_WIDTH = GM_WIDTH + SSM_WIDTH

POOL_WINDOWS = (2, 4, 8, 16)
POOL_GROUPS = len(POOL_WINDOWS)
POOL_GROUP_DIM = D_MODEL // POOL_GROUPS

D_FF = ((8 * D_MODEL // 3 + 255) // 256) * 256

kernel_name = "hybrid_gmlp_ssd_pool_decoder"


def rms_norm(x, g):
    xf = x.astype(jnp.float32)
    y = xf * lax.rsqrt(jnp.mean(xf * xf, axis=-1, keepdims=True) + EPS)
    return (y * g.astype(jnp.float32)).astype(x.dtype)


def layer_norm(x, g, b):
    xf = x.astype(jnp.float32)
    mu = jnp.mean(xf, axis=-1, keepdims=True)
    xc = xf - mu
    y = xc * lax.rsqrt(jnp.mean(xc * xc, axis=-1, keepdims=True) + EPS)
    return (y * g.astype(jnp.float32) + b.astype(jnp.float32)).astype(x.dtype)


def gmlp_spatial_gating(u, v, ln_g, ln_b, w_s, b_s):
    bsz, seqlen, _ = v.shape
    n_chunks = seqlen // GM_CHUNK
    v = layer_norm(v, ln_g, ln_b).reshape(bsz, n_chunks, GM_CHUNK, GM_HEADS, GM_HEAD_DIM)
    causal = jnp.tril(jnp.ones((GM_CHUNK, GM_CHUNK), dtype=bool))
    w = jnp.where(causal[None], w_s, 0).astype(v.dtype)
    mixed = jnp.einsum("hts,bcshd->bcthd", w, v) + b_s.T.astype(v.dtype)[None, None, :, :, None]
    return u * mixed.reshape(bsz, seqlen, GM_WIDTH)


def causal_depthwise_conv(x, w, b):
    channels = x.shape[-1]
    y = lax.conv_general_dilated(
        x, w[:, None, :].astype(x.dtype), window_strides=(1,), padding=[(SSM_CONV - 1, 0)],
        dimension_numbers=("NWC", "WIO", "NWC"), feature_group_count=channels)
    return y + b.astype(x.dtype)


def segsum_exp(a_cum):
    n = a_cum.shape[-1]
    diff = a_cum[..., :, None] - a_cum[..., None, :]
    mask = jnp.tril(jnp.ones((n, n), dtype=bool))
    return jnp.exp(jnp.where(mask, diff, -jnp.inf))


def ssd_chunked(x, dt, a, b_mat, c_mat):
    bsz, seqlen = x.shape[:2]
    nc = seqlen // SSM_CHUNK

    def chunk(t):
        return t.reshape((bsz, nc, SSM_CHUNK) + t.shape[2:])

    xdt = chunk(x * dt[..., None])
    a_cum = jnp.cumsum(jnp.moveaxis(chunk(dt * a), 2, -1), axis=-1)
    b_c, c_c = chunk(b_mat), chunk(c_mat)
    decay = segsum_exp(a_cum)
    cb = jnp.einsum("bclgn,bcsgn->bcgls", c_c, b_c)
    y_diag = jnp.einsum("bcgls,bcgrls,bcsgrp->bclgrp", cb, decay, xdt)
    decay_to_end = jnp.exp(a_cum[..., -1:] - a_cum)
    chunk_states = jnp.einsum("bclgn,bcgrl,bclgrp->bcgrpn", b_c, decay_to_end, xdt)
    chunk_decay = jnp.exp(a_cum[..., -1])

    def step(state, inp):
        dec, new = inp
        return state * dec[..., None, None] + new, state

    init = jnp.zeros_like(chunk_states[:, 0])
    _, prev_states = lax.scan(step, init, (jnp.moveaxis(chunk_decay, 1, 0), jnp.moveaxis(chunk_states, 1, 0)))
    prev_states = jnp.moveaxis(prev_states, 0, 1)
    y_off = jnp.einsum("bclgn,bcgrpn,bcgrl->bclgrp", c_c, prev_states, jnp.exp(a_cum))
    return (y_diag + y_off).reshape((bsz, seqlen) + x.shape[2:])


def hybrid_gmlp_ssd_mixer(h, w_in, gm_ln_g, gm_ln_b, gm_ws, gm_bs, conv_w, conv_b,
                          dt_bias, a_log, d_skip, ssm_norm_g, w_out):
    f32 = jnp.float32
    bsz, seqlen, _ = h.shape
    proj = h @ w_in
    u, v, z, xbc, dt_raw = jnp.split(proj, SPLIT_POINTS, axis=-1)
    y_a = gmlp_spatial_gating(jax.nn.gelu(u), jax.nn.gelu(v), gm_ln_g, gm_ln_b, gm_ws, gm_bs)
    xbc = jax.nn.silu(causal_depthwise_conv(xbc, conv_w, conv_b))
    xs, b_mat, c_mat = jnp.split(xbc, (SSM_WIDTH, SSM_WIDTH + SSM_GROUPS * SSM_STATE), axis=-1)
    dt = jax.nn.softplus(dt_raw.astype(f32) + dt_bias.astype(f32))
    a = -jnp.exp(a_log.astype(f32))
    xs_h = xs.astype(f32).reshape(bsz, seqlen, SSM_GROUPS, SSM_HEADS_PER_GROUP, SSM_HEAD_DIM)
    y = ssd_chunked(
        xs_h,
        dt.reshape(bsz, seqlen, SSM_GROUPS, SSM_HEADS_PER_GROUP),
        a.reshape(SSM_GROUPS, SSM_HEADS_PER_GROUP),
        b_mat.astype(f32).reshape(bsz, seqlen, SSM_GROUPS, SSM_STATE),
        c_mat.astype(f32).reshape(bsz, seqlen, SSM_GROUPS, SSM_STATE))
    y = y + d_skip.astype(f32).reshape(SSM_GROUPS, SSM_HEADS_PER_GROUP)[:, :, None] * xs_h
    gated = (y.reshape(bsz, seqlen, SSM_WIDTH) * jax.nn.silu(z.astype(f32)))
    gated = gated.reshape(bsz, seqlen, SSM_GROUPS, SSM_WIDTH // SSM_GROUPS)
    gated = gated * lax.rsqrt(jnp.mean(gated * gated, axis=-1, keepdims=True) + EPS)
    y_b = (gated.reshape(bsz, seqlen, SSM_WIDTH) * ssm_norm_g.astype(f32)).astype(h.dtype)
    return jnp.concatenate([y_a, y_b], axis=-1) @ w_out


def multiscale_pool_mixer(h, pool_w, pool_b, pool_scale):
    f32 = jnp.float32
    bsz, seqlen, _ = h.shape
    hf = h.astype(f32).reshape(bsz, seqlen, POOL_GROUPS, POOL_GROUP_DIM)
    cs = jnp.cumsum(hf, axis=1)
    cs = jnp.concatenate([jnp.zeros_like(cs[:, :1]), cs], axis=1)
    pos = jnp.arange(1, seqlen + 1, dtype=f32)
    pooled = []
    for g, win in enumerate(POOL_WINDOWS):
        cs_g = cs[:, :, g]
        upper = cs_g[:, 1:]
        lower = jnp.pad(cs_g, ((0, 0), (win - 1, 0), (0, 0)))[:, :seqlen]
        count = jnp.minimum(pos, float(win))[None, :, None]
        pooled.append((upper - lower) / count)
    pooled = jnp.stack(pooled, axis=2)
    out = jnp.einsum("blgc,gcd->blgd", pooled - hf, pool_w.astype(f32)) + pool_b.astype(f32)
    return (out.reshape(bsz, seqlen, D_MODEL) * pool_scale.astype(f32)).astype(h.dtype)


def swiglu(h, w_gate, w_up, w_down):
    return (jax.nn.silu(h @ w_gate) * (h @ w_up)) @ w_down


def _fwd_setup_inputs(seed: int = 0) -> dict:
    key = jax.random.key(seed)
    ks = jax.random.split(key, 20)
    f32 = jnp.float32

    def nrm(k, shape, scale):
        return jax.random.normal(k, shape, f32) * scale

    x = nrm(ks[0], (BATCH, SEQ, D_MODEL), 1.0)
    norm_g = 1.0 + nrm(ks[1], (DEPTH, 4, D_MODEL), 0.02)
    w_in = nrm(ks[2], (N_EVEN, D_MODEL, IN_PROJ_DIM), D_MODEL ** -0.5)
    gm_ln_g = 1.0 + nrm(ks[3], (N_EVEN, GM_WIDTH), 0.02)
    gm_ln_b = nrm(ks[4], (N_EVEN, GM_WIDTH), 0.02)
    gm_ws = nrm(ks[5], (N_EVEN, GM_HEADS, GM_CHUNK, GM_CHUNK), GM_CHUNK ** -0.5)
    gm_bs = 1.0 + nrm(ks[6], (N_EVEN, GM_HEADS, GM_CHUNK), 0.02)
    conv_w = nrm(ks[7], (N_EVEN, SSM_CONV, SSM_CONV_DIM), SSM_CONV ** -0.5)
    conv_b = nrm(ks[8], (N_EVEN, SSM_CONV_DIM), 0.02)
    dt0 = jnp.exp(jax.random.uniform(ks[9], (N_EVEN, SSM_HEADS), f32, math.log(DT_MIN), math.log(DT_MAX)))
    dt0 = jnp.maximum(dt0, DT_FLOOR)
    dt_bias = dt0 + jnp.log(-jnp.expm1(-dt0))
    a_log = jnp.log(jax.random.uniform(ks[10], (N_EVEN, SSM_HEADS), f32, 1.0, 16.0))
    d_skip = 1.0 + nrm(ks[11], (N_EVEN, SSM_HEADS), 0.02)
    ssm_norm_g = 1.0 + nrm(ks[12], (N_EVEN, SSM_WIDTH), 0.02)
    w_out = nrm(ks[13], (N_EVEN, MIX_WIDTH, D_MODEL), MIX_WIDTH ** -0.5)
    pool_w = nrm(ks[14], (N_ODD, POOL_GROUPS, POOL_GROUP_DIM, POOL_GROUP_DIM), POOL_GROUP_DIM ** -0.5)
    pool_b = nrm(ks[15], (N_ODD, POOL_GROUPS, POOL_GROUP_DIM), 0.02)
    pool_scale = 1.0 + nrm(ks[16], (N_ODD, D_MODEL), 0.1)
    ffn_w_gate = nrm(ks[17], (DEPTH, D_MODEL, D_FF), D_MODEL ** -0.5)
    ffn_w_up = nrm(ks[18], (DEPTH, D_MODEL, D_FF), D_MODEL ** -0.5)
    ffn_w_down = nrm(ks[19], (DEPTH, D_FF, D_MODEL), D_FF ** -0.5)
    return {"x": x, "norm_g": norm_g, "w_in": w_in, "gm_ln_g": gm_ln_g, "gm_ln_b": gm_ln_b,
            "gm_ws": gm_ws, "gm_bs": gm_bs, "conv_w": conv_w, "conv_b": conv_b,
            "dt_bias": dt_bias, "a_log": a_log, "d_skip": d_skip, "ssm_norm_g": ssm_norm_g,
            "w_out": w_out, "pool_w": pool_w, "pool_b": pool_b, "pool_scale": pool_scale,
            "ffn_w_gate": ffn_w_gate, "ffn_w_up": ffn_w_up, "ffn_w_down": ffn_w_down}


def _fwd_reference(x, norm_g, w_in, gm_ln_g, gm_ln_b, gm_ws, gm_bs, conv_w, conv_b, dt_bias, a_log,
              d_skip, ssm_norm_g, w_out, pool_w, pool_b, pool_scale, ffn_w_gate, ffn_w_up, ffn_w_down):
    h = x
    for layer in range(DEPTH):
        i = layer // 2
        y = rms_norm(h, norm_g[layer, 0])
        if layer % 2 == 0:
            y = hybrid_gmlp_ssd_mixer(y, w_in[i], gm_ln_g[i], gm_ln_b[i], gm_ws[i], gm_bs[i],
                                      conv_w[i], conv_b[i], dt_bias[i], a_log[i], d_skip[i],
                                      ssm_norm_g[i], w_out[i])
        else:
            y = multiscale_pool_mixer(y, pool_w[i], pool_b[i], pool_scale[i])
        h = h + rms_norm(y, norm_g[layer, 1])
        y = swiglu(rms_norm(h, norm_g[layer, 2]), ffn_w_gate[layer], ffn_w_up[layer], ffn_w_down[layer])
        h = h + rms_norm(y, norm_g[layer, 3])
    return h


import jax as _jax
import jax.numpy as _jnp

TWIN_FORMAT = 'train_step'
FWD_PARAMS = ['x', 'norm_g', 'w_in', 'gm_ln_g', 'gm_ln_b', 'gm_ws', 'gm_bs', 'conv_w', 'conv_b', 'dt_bias', 'a_log', 'd_skip', 'ssm_norm_g', 'w_out', 'pool_w', 'pool_b', 'pool_scale', 'ffn_w_gate', 'ffn_w_up', 'ffn_w_down']
TWIN_WEIGHTS = ['norm_g', 'w_in', 'gm_ln_g', 'gm_ln_b', 'gm_ws', 'gm_bs', 'conv_w', 'conv_b', 'dt_bias', 'a_log', 'd_skip', 'ssm_norm_g', 'w_out', 'pool_w', 'pool_b', 'pool_scale', 'ffn_w_gate', 'ffn_w_up', 'ffn_w_down']
TWIN_DIFF_INPUT = 'x'
TWIN_INPUTS = ['x', 'norm_g', 'w_in', 'gm_ln_g', 'gm_ln_b', 'gm_ws', 'gm_bs', 'conv_w', 'conv_b', 'dt_bias', 'a_log', 'd_skip', 'ssm_norm_g', 'w_out', 'pool_w', 'pool_b', 'pool_scale', 'ffn_w_gate', 'ffn_w_up', 'ffn_w_down', 'loss_target', 'm_norm_g', 'm_w_in', 'm_gm_ln_g', 'm_gm_ln_b', 'm_gm_ws', 'm_gm_bs', 'm_conv_w', 'm_conv_b', 'm_dt_bias', 'm_a_log', 'm_d_skip', 'm_ssm_norm_g', 'm_w_out', 'm_pool_w', 'm_pool_b', 'm_pool_scale', 'm_ffn_w_gate', 'm_ffn_w_up', 'm_ffn_w_down', 'v_norm_g', 'v_w_in', 'v_gm_ln_g', 'v_gm_ln_b', 'v_gm_ws', 'v_gm_bs', 'v_conv_w', 'v_conv_b', 'v_dt_bias', 'v_a_log', 'v_d_skip', 'v_ssm_norm_g', 'v_w_out', 'v_pool_w', 'v_pool_b', 'v_pool_scale', 'v_ffn_w_gate', 'v_ffn_w_up', 'v_ffn_w_down']
TWIN_OUTPUTS = ['loss', 'grad_x', 'grad_norm_g', 'grad_w_in', 'grad_gm_ln_g', 'grad_gm_ln_b', 'grad_gm_ws', 'grad_gm_bs', 'grad_conv_w', 'grad_conv_b', 'grad_dt_bias', 'grad_a_log', 'grad_d_skip', 'grad_ssm_norm_g', 'grad_w_out', 'grad_pool_w', 'grad_pool_b', 'grad_pool_scale', 'grad_ffn_w_gate', 'grad_ffn_w_up', 'grad_ffn_w_down', 'delta_norm_g', 'delta_w_in', 'delta_gm_ln_g', 'delta_gm_ln_b', 'delta_gm_ws', 'delta_gm_bs', 'delta_conv_w', 'delta_conv_b', 'delta_dt_bias', 'delta_a_log', 'delta_d_skip', 'delta_ssm_norm_g', 'delta_w_out', 'delta_pool_w', 'delta_pool_b', 'delta_pool_scale', 'delta_ffn_w_gate', 'delta_ffn_w_up', 'delta_ffn_w_down', 'new_m_norm_g', 'new_m_w_in', 'new_m_gm_ln_g', 'new_m_gm_ln_b', 'new_m_gm_ws', 'new_m_gm_bs', 'new_m_conv_w', 'new_m_conv_b', 'new_m_dt_bias', 'new_m_a_log', 'new_m_d_skip', 'new_m_ssm_norm_g', 'new_m_w_out', 'new_m_pool_w', 'new_m_pool_b', 'new_m_pool_scale', 'new_m_ffn_w_gate', 'new_m_ffn_w_up', 'new_m_ffn_w_down', 'new_v_norm_g', 'new_v_w_in', 'new_v_gm_ln_g', 'new_v_gm_ln_b', 'new_v_gm_ws', 'new_v_gm_bs', 'new_v_conv_w', 'new_v_conv_b', 'new_v_dt_bias', 'new_v_a_log', 'new_v_d_skip', 'new_v_ssm_norm_g', 'new_v_w_out', 'new_v_pool_w', 'new_v_pool_b', 'new_v_pool_scale', 'new_v_ffn_w_gate', 'new_v_ffn_w_up', 'new_v_ffn_w_down']
TWIN_LEAF_KINDS = {'loss': 'loss', 'grad_x': 'grad_x', 'grad_norm_g': 'grad_w', 'grad_w_in': 'grad_w', 'grad_gm_ln_g': 'grad_w', 'grad_gm_ln_b': 'grad_w', 'grad_gm_ws': 'grad_w', 'grad_gm_bs': 'grad_w', 'grad_conv_w': 'grad_w', 'grad_conv_b': 'grad_w', 'grad_dt_bias': 'grad_w', 'grad_a_log': 'grad_w', 'grad_d_skip': 'grad_w', 'grad_ssm_norm_g': 'grad_w', 'grad_w_out': 'grad_w', 'grad_pool_w': 'grad_w', 'grad_pool_b': 'grad_w', 'grad_pool_scale': 'grad_w', 'grad_ffn_w_gate': 'grad_w', 'grad_ffn_w_up': 'grad_w', 'grad_ffn_w_down': 'grad_w', 'delta_norm_g': 'delta_w', 'delta_w_in': 'delta_w', 'delta_gm_ln_g': 'delta_w', 'delta_gm_ln_b': 'delta_w', 'delta_gm_ws': 'delta_w', 'delta_gm_bs': 'delta_w', 'delta_conv_w': 'delta_w', 'delta_conv_b': 'delta_w', 'delta_dt_bias': 'delta_w', 'delta_a_log': 'delta_w', 'delta_d_skip': 'delta_w', 'delta_ssm_norm_g': 'delta_w', 'delta_w_out': 'delta_w', 'delta_pool_w': 'delta_w', 'delta_pool_b': 'delta_w', 'delta_pool_scale': 'delta_w', 'delta_ffn_w_gate': 'delta_w', 'delta_ffn_w_up': 'delta_w', 'delta_ffn_w_down': 'delta_w', 'new_m_norm_g': 'new_m', 'new_m_w_in': 'new_m', 'new_m_gm_ln_g': 'new_m', 'new_m_gm_ln_b': 'new_m', 'new_m_gm_ws': 'new_m', 'new_m_gm_bs': 'new_m', 'new_m_conv_w': 'new_m', 'new_m_conv_b': 'new_m', 'new_m_dt_bias': 'new_m', 'new_m_a_log': 'new_m', 'new_m_d_skip': 'new_m', 'new_m_ssm_norm_g': 'new_m', 'new_m_w_out': 'new_m', 'new_m_pool_w': 'new_m', 'new_m_pool_b': 'new_m', 'new_m_pool_scale': 'new_m', 'new_m_ffn_w_gate': 'new_m', 'new_m_ffn_w_up': 'new_m', 'new_m_ffn_w_down': 'new_m', 'new_v_norm_g': 'new_v', 'new_v_w_in': 'new_v', 'new_v_gm_ln_g': 'new_v', 'new_v_gm_ln_b': 'new_v', 'new_v_gm_ws': 'new_v', 'new_v_gm_bs': 'new_v', 'new_v_conv_w': 'new_v', 'new_v_conv_b': 'new_v', 'new_v_dt_bias': 'new_v', 'new_v_a_log': 'new_v', 'new_v_d_skip': 'new_v', 'new_v_ssm_norm_g': 'new_v', 'new_v_w_out': 'new_v', 'new_v_pool_w': 'new_v', 'new_v_pool_b': 'new_v', 'new_v_pool_scale': 'new_v', 'new_v_ffn_w_gate': 'new_v', 'new_v_ffn_w_up': 'new_v', 'new_v_ffn_w_down': 'new_v'}


def _forward(args):
    return _fwd_reference(*[args[k] for k in FWD_PARAMS])


def _output_shape():
    def fwd():
        inp = _fwd_setup_inputs(0)
        return _fwd_reference(*[inp[k] for k in FWD_PARAMS])
    out = _jax.eval_shape(fwd)
    return out.shape, out.dtype

N_MICROBATCH = 1
ADAM_LR = 0.001
ADAM_B1 = 0.9
ADAM_B2 = 0.999
ADAM_EPS = 1e-08
ADAM_WD = 0.01
ADAM_STEP = 10
PER_EXAMPLE_BATCH_AXIS = {'x': 0, 'loss_target': 0}
SHARED_INPUTS = []
_WEIGHT_DTYPES = {'norm_g': _jnp.float32, 'w_in': _jnp.float32, 'gm_ln_g': _jnp.float32, 'gm_ln_b': _jnp.float32, 'gm_ws': _jnp.float32, 'gm_bs': _jnp.float32, 'conv_w': _jnp.float32, 'conv_b': _jnp.float32, 'dt_bias': _jnp.float32, 'a_log': _jnp.float32, 'd_skip': _jnp.float32, 'ssm_norm_g': _jnp.float32, 'w_out': _jnp.float32, 'pool_w': _jnp.float32, 'pool_b': _jnp.float32, 'pool_scale': _jnp.float32, 'ffn_w_gate': _jnp.float32, 'ffn_w_up': _jnp.float32, 'ffn_w_down': _jnp.float32}
MOMENT_SCALE = {'norm_g': 9.135687e+01, 'w_in': 1.181347e+00, 'gm_ln_g': 6.000894e-01, 'gm_ln_b': 5.411875e-01, 'gm_ws': 8.070004e-01, 'gm_bs': 1.170857e+00, 'conv_w': 1.433917e+00, 'conv_b': 3.916405e+00, 'dt_bias': 7.751413e+00, 'a_log': 1.154649e+01, 'd_skip': 8.002516e+00, 'ssm_norm_g': 3.878471e+00, 'w_out': 4.225607e+00, 'pool_w': 1.091375e+01, 'pool_b': 2.882290e+01, 'pool_scale': 1.195689e+01, 'ffn_w_gate': 8.815780e-01, 'ffn_w_up': 1.566153e+00, 'ffn_w_down': 2.766469e+00}


def _to_microbatches(a, axis):
    t = _jnp.moveaxis(a, axis, 0)
    t = t.reshape((N_MICROBATCH, t.shape[0] // N_MICROBATCH) + t.shape[1:])
    return _jnp.moveaxis(t, 1, axis + 1)


def setup_inputs(seed: int = 0) -> dict:
    inp = _fwd_setup_inputs(seed)
    key = _jax.random.fold_in(_jax.random.key(seed), 7919)
    shape, _ = _output_shape()
    out = dict(inp)
    out["loss_target"] = _jax.random.normal(_jax.random.fold_in(key, 0), shape, _jnp.float32)
    for i, name in enumerate(TWIN_WEIGHTS):
        w = inp[name].astype(_jnp.float32)
        if MOMENT_SCALE is None:
            s = _jnp.sqrt(_jnp.mean(_jnp.square(w)) + 1e-30)
        else:
            s = MOMENT_SCALE[name]
        km, kv = _jax.random.split(_jax.random.fold_in(key, i + 1))
        out[name] = w
        out["m_" + name] = s * _jax.random.normal(km, w.shape, _jnp.float32)
        out["v_" + name] = (s * s) * _jax.random.uniform(kv, w.shape, _jnp.float32, 0.5, 1.5)
    if N_MICROBATCH > 1:
        for name, axis in PER_EXAMPLE_BATCH_AXIS.items():
            out[name] = _to_microbatches(out[name], axis)
    return {'x': out['x'], 'norm_g': out['norm_g'], 'w_in': out['w_in'], 'gm_ln_g': out['gm_ln_g'], 'gm_ln_b': out['gm_ln_b'], 'gm_ws': out['gm_ws'], 'gm_bs': out['gm_bs'], 'conv_w': out['conv_w'], 'conv_b': out['conv_b'], 'dt_bias': out['dt_bias'], 'a_log': out['a_log'], 'd_skip': out['d_skip'], 'ssm_norm_g': out['ssm_norm_g'], 'w_out': out['w_out'], 'pool_w': out['pool_w'], 'pool_b': out['pool_b'], 'pool_scale': out['pool_scale'], 'ffn_w_gate': out['ffn_w_gate'], 'ffn_w_up': out['ffn_w_up'], 'ffn_w_down': out['ffn_w_down'], 'loss_target': out['loss_target'], 'm_norm_g': out['m_norm_g'], 'm_w_in': out['m_w_in'], 'm_gm_ln_g': out['m_gm_ln_g'], 'm_gm_ln_b': out['m_gm_ln_b'], 'm_gm_ws': out['m_gm_ws'], 'm_gm_bs': out['m_gm_bs'], 'm_conv_w': out['m_conv_w'], 'm_conv_b': out['m_conv_b'], 'm_dt_bias': out['m_dt_bias'], 'm_a_log': out['m_a_log'], 'm_d_skip': out['m_d_skip'], 'm_ssm_norm_g': out['m_ssm_norm_g'], 'm_w_out': out['m_w_out'], 'm_pool_w': out['m_pool_w'], 'm_pool_b': out['m_pool_b'], 'm_pool_scale': out['m_pool_scale'], 'm_ffn_w_gate': out['m_ffn_w_gate'], 'm_ffn_w_up': out['m_ffn_w_up'], 'm_ffn_w_down': out['m_ffn_w_down'], 'v_norm_g': out['v_norm_g'], 'v_w_in': out['v_w_in'], 'v_gm_ln_g': out['v_gm_ln_g'], 'v_gm_ln_b': out['v_gm_ln_b'], 'v_gm_ws': out['v_gm_ws'], 'v_gm_bs': out['v_gm_bs'], 'v_conv_w': out['v_conv_w'], 'v_conv_b': out['v_conv_b'], 'v_dt_bias': out['v_dt_bias'], 'v_a_log': out['v_a_log'], 'v_d_skip': out['v_d_skip'], 'v_ssm_norm_g': out['v_ssm_norm_g'], 'v_w_out': out['v_w_out'], 'v_pool_w': out['v_pool_w'], 'v_pool_b': out['v_pool_b'], 'v_pool_scale': out['v_pool_scale'], 'v_ffn_w_gate': out['v_ffn_w_gate'], 'v_ffn_w_up': out['v_ffn_w_up'], 'v_ffn_w_down': out['v_ffn_w_down']}


def _loss(weights, diff, rest, loss_target):
    with _jax.named_scope("forward"):
        args = {**rest, TWIN_DIFF_INPUT: diff, **{k: w.astype(_WEIGHT_DTYPES[k]) for k, w in weights.items()}}
        y = _forward(args)
    with _jax.named_scope("loss_head"):
        err = _jnp.square(y.astype(_jnp.float32) - loss_target)
        return 0.5 * _jnp.sum(_jnp.mean(err, axis=-1)) if err.ndim else 0.5 * err


def _adamw(w, g, m, v):
    m = ADAM_B1 * m + (1.0 - ADAM_B1) * g
    v = ADAM_B2 * v + (1.0 - ADAM_B2) * _jnp.square(g)
    m_hat = m / (1.0 - ADAM_B1 ** ADAM_STEP)
    v_hat = v / (1.0 - ADAM_B2 ** ADAM_STEP)
    delta = -ADAM_LR * (m_hat / (_jnp.sqrt(v_hat) + ADAM_EPS) + ADAM_WD * w)
    return delta, m, v


def reference(x, norm_g, w_in, gm_ln_g, gm_ln_b, gm_ws, gm_bs, conv_w, conv_b, dt_bias, a_log, d_skip, ssm_norm_g, w_out, pool_w, pool_b, pool_scale, ffn_w_gate, ffn_w_up, ffn_w_down, loss_target, m_norm_g, m_w_in, m_gm_ln_g, m_gm_ln_b, m_gm_ws, m_gm_bs, m_conv_w, m_conv_b, m_dt_bias, m_a_log, m_d_skip, m_ssm_norm_g, m_w_out, m_pool_w, m_pool_b, m_pool_scale, m_ffn_w_gate, m_ffn_w_up, m_ffn_w_down, v_norm_g, v_w_in, v_gm_ln_g, v_gm_ln_b, v_gm_ws, v_gm_bs, v_conv_w, v_conv_b, v_dt_bias, v_a_log, v_d_skip, v_ssm_norm_g, v_w_out, v_pool_w, v_pool_b, v_pool_scale, v_ffn_w_gate, v_ffn_w_up, v_ffn_w_down):
    given = dict(x=x, norm_g=norm_g, w_in=w_in, gm_ln_g=gm_ln_g, gm_ln_b=gm_ln_b, gm_ws=gm_ws, gm_bs=gm_bs, conv_w=conv_w, conv_b=conv_b, dt_bias=dt_bias, a_log=a_log, d_skip=d_skip, ssm_norm_g=ssm_norm_g, w_out=w_out, pool_w=pool_w, pool_b=pool_b, pool_scale=pool_scale, ffn_w_gate=ffn_w_gate, ffn_w_up=ffn_w_up, ffn_w_down=ffn_w_down, loss_target=loss_target, m_norm_g=m_norm_g, m_w_in=m_w_in, m_gm_ln_g=m_gm_ln_g, m_gm_ln_b=m_gm_ln_b, m_gm_ws=m_gm_ws, m_gm_bs=m_gm_bs, m_conv_w=m_conv_w, m_conv_b=m_conv_b, m_dt_bias=m_dt_bias, m_a_log=m_a_log, m_d_skip=m_d_skip, m_ssm_norm_g=m_ssm_norm_g, m_w_out=m_w_out, m_pool_w=m_pool_w, m_pool_b=m_pool_b, m_pool_scale=m_pool_scale, m_ffn_w_gate=m_ffn_w_gate, m_ffn_w_up=m_ffn_w_up, m_ffn_w_down=m_ffn_w_down, v_norm_g=v_norm_g, v_w_in=v_w_in, v_gm_ln_g=v_gm_ln_g, v_gm_ln_b=v_gm_ln_b, v_gm_ws=v_gm_ws, v_gm_bs=v_gm_bs, v_conv_w=v_conv_w, v_conv_b=v_conv_b, v_dt_bias=v_dt_bias, v_a_log=v_a_log, v_d_skip=v_d_skip, v_ssm_norm_g=v_ssm_norm_g, v_w_out=v_w_out, v_pool_w=v_pool_w, v_pool_b=v_pool_b, v_pool_scale=v_pool_scale, v_ffn_w_gate=v_ffn_w_gate, v_ffn_w_up=v_ffn_w_up, v_ffn_w_down=v_ffn_w_down)
    weights = {n: given[n] for n in TWIN_WEIGHTS}
    shared = {n: given[n] for n in SHARED_INPUTS}
    per_example = {n: given[n] for n in ['x']}
    grad_fn = _jax.value_and_grad(_loss, argnums=(0, 1))

    def one_microbatch(ex, loss_target):
        ex = dict(ex)
        diff = ex.pop(TWIN_DIFF_INPUT)
        return grad_fn(weights, diff, {**shared, **ex}, loss_target)

    if N_MICROBATCH == 1:
        loss, (grad_w, grad_x) = one_microbatch(per_example, given["loss_target"])
    else:
        def body(carry, xs):
            loss_sum, grad_sum = carry
            l_k, (gw_k, gx_k) = one_microbatch(xs[0], xs[1])
            with _jax.named_scope("update"):
                return (loss_sum + l_k, _jax.tree.map(_jnp.add, grad_sum, gw_k)), gx_k

        init = (_jnp.zeros((), _jnp.float32), _jax.tree.map(_jnp.zeros_like, weights))
        (loss, grad_w), grad_x = _jax.lax.scan(body, init, (per_example, given["loss_target"]))
    with _jax.named_scope("update"):
        delta_w, new_m, new_v = {}, {}, {}
        for n in TWIN_WEIGHTS:
            delta_w[n], new_m[n], new_v[n] = _adamw(weights[n], grad_w[n], given["m_" + n], given["v_" + n])
    return (loss, grad_x, *[grad_w[n] for n in TWIN_WEIGHTS], *[delta_w[n] for n in TWIN_WEIGHTS],
            *[new_m[n] for n in TWIN_WEIGHTS], *[new_v[n] for n in TWIN_WEIGHTS])
```

```python
import functools
import math

import jax
import jax.numpy as jnp
from jax import lax
from jax.experimental import pallas as pl
from jax.experimental.pallas import tpu as pltpu

f32, bf16 = jnp.float32, jnp.bfloat16
SDS = jax.ShapeDtypeStruct

D = 1024
EPS = 1e-6
CHUNK = 128
GM_HEADS, GM_HD = 4, 256
SSM_GROUPS, SSM_HPG, SSM_P, SSM_N = 4, 4, 64, 128
N_HEADS = SSM_GROUPS * SSM_HPG
CONV_K = 4
CONV_DIM = 2048
POOL_WINDOWS = (2, 4, 8, 16)
POOL_GD = 256
POOL_HALO = 16
CONV_HALO = 8
D_FF = 2816
DT_PAD = 128
IN_DIM = 5136

ADAM_LR, ADAM_B1, ADAM_B2, ADAM_EPS, ADAM_WD, ADAM_STEP = 0.001, 0.9, 0.999, 1e-08, 0.01, 10

NT = (((1,), (1,)), ((), ()))
TN = (((0,), (0,)), ((), ()))
NN = (((1,), (0,)), ((), ()))
HI = lax.Precision.HIGHEST


def _silu(x):
    return x * jax.nn.sigmoid(x)


def _softplus(x):
    return jnp.maximum(x, 0.0) + jnp.log1p(jnp.exp(-jnp.abs(x)))


def _rms(x, g):
    return x * lax.rsqrt(jnp.mean(x * x, axis=-1, keepdims=True) + EPS) * g


def _rms_bwd(x, g, dy):
    r = lax.rsqrt(jnp.mean(x * x, axis=-1, keepdims=True) + EPS)
    xh = x * r
    dxh = dy * g
    dx = r * (dxh - xh * jnp.mean(dxh * xh, axis=-1, keepdims=True))
    return dx, jnp.sum(dy * xh, axis=0, keepdims=True)


def _bdot(a, b, dims=NN):
    return lax.dot_general(a.astype(bf16), b.astype(bf16), dims, preferred_element_type=f32)


def matmul(name, pairs, mode, out_dtype, tm, tn, tk=None):
    a0, b0 = pairs[0]
    if mode == "tn":
        M, N, K = a0.shape[1], b0.shape[1], a0.shape[0]
    else:
        M, K = a0.shape
        N = b0.shape[1] if mode == "nn" else b0.shape[0]
    tm, tn = min(tm, M), min(tn, N)
    assert M % tm == 0 and N % tn == 0, (name, M, N, tm, tn)
    if tk is None:
        nk = 1
    else:
        assert len(pairs) == 1 and K % tk == 0
        nk = K // tk
    dims = {"nn": NN, "nt": NT, "tn": TN}[mode]
    in_specs, args = [], []
    for a, b in pairs:
        kk = (a.shape[0] if mode == "tn" else a.shape[1]) if tk is None else tk
        if mode == "tn":
            in_specs.append(pl.BlockSpec((kk, tm), lambda j, i, k: (k, i)))
            in_specs.append(pl.BlockSpec((kk, tn), lambda j, i, k: (k, j)))
        elif mode == "nn":
            in_specs.append(pl.BlockSpec((tm, kk), lambda j, i, k: (i, k)))
            in_specs.append(pl.BlockSpec((kk, tn), lambda j, i, k: (k, j)))
        else:
            in_specs.append(pl.BlockSpec((tm, kk), lambda j, i, k: (i, k)))
            in_specs.append(pl.BlockSpec((tn, kk), lambda j, i, k: (j, k)))
        args += [a, b]
    npairs = len(pairs)

    def kern(*refs):
        o = refs[2 * npairs]
        part = None
        for p in range(npairs):
            d = _bdot(refs[2 * p][...], refs[2 * p + 1][...], dims)
            part = d if part is None else part + d
        if nk == 1:
            o[...] = part.astype(out_dtype)
        else:
            acc = refs[2 * npairs + 1]
            k = pl.program_id(2)

            @pl.when(k == 0)
            def _():
                acc[...] = part

            @pl.when(k > 0)
            def _():
                acc[...] += part

            @pl.when(k == nk - 1)
            def _():
                o[...] = acc[...].astype(out_dtype)

    return pl.pallas_call(
        kern, name=name, grid=(N // tn, M // tm, nk),
        in_specs=in_specs, out_specs=pl.BlockSpec((tm, tn), lambda j, i, k: (i, j)),
        out_shape=SDS((M, N), out_dtype),
        scratch_shapes=[pltpu.VMEM((tm, tn), f32)] if nk > 1 else [],
        compiler_params=pltpu.CompilerParams(dimension_semantics=("parallel", "parallel", "arbitrary")),
    )(*args)


def rowcall(name, body, T, tm, ins, outs, accs=(), scratch=(), reverse=False):
    n = T // tm
    assert T % tm == 0

    def blk(i):
        return (n - 1 - i) if reverse else i

    in_specs, args = [], []
    for spec in ins:
        kind, arr = spec[0], spec[1]
        if kind == "row":
            _, _, w, cb = spec
            in_specs.append(pl.BlockSpec((tm, w), lambda i, cb=cb: (blk(i), cb)))
        elif kind == "prev":
            _, _, w, cb, h = spec
            r = tm // h
            in_specs.append(pl.BlockSpec((h, w), lambda i, cb=cb, r=r: (jnp.maximum(blk(i) * r - 1, 0), cb)))
        elif kind == "next":
            _, _, w, cb, h = spec
            r = tm // h
            in_specs.append(pl.BlockSpec((h, w), lambda i, cb=cb, r=r, h=h: (jnp.minimum((blk(i) + 1) * r, T // h - 1), cb)))
        else:
            nd = arr.ndim
            in_specs.append(pl.BlockSpec(arr.shape, lambda i, nd=nd: (0,) * nd))
        args.append(arr)
    out_shape = [SDS((T, w), dt) for w, dt in outs] + [SDS(tuple(s), f32) for s in accs]
    out_specs = [pl.BlockSpec((tm, w), lambda i: (blk(i), 0)) for w, _ in outs]
    out_specs += [pl.BlockSpec(tuple(s), lambda i, nd=len(s): (0,) * nd) for s in accs]
    ni, no, na = len(ins), len(outs), len(accs)

    def kern(*refs):
        i = pl.program_id(0)
        in_refs, out_refs = refs[:ni], refs[ni:ni + no]
        acc_refs, scr = refs[ni + no:ni + no + na], refs[ni + no + na:]
        if na:
            @pl.when(i == 0)
            def _():
                for a in acc_refs:
                    a[...] = jnp.zeros(a.shape, f32)
        body(blk(i), n, in_refs, out_refs, acc_refs, scr)

    res = pl.pallas_call(
        kern, name=name, grid=(n,), in_specs=in_specs, out_specs=out_specs, out_shape=out_shape,
        scratch_shapes=list(scratch),
        compiler_params=pltpu.CompilerParams(dimension_semantics=("arbitrary",)),
    )(*args)
    return res


def rms_to_bf16(name, T, tm, x, g):
    def body(i, n, ins, outs, accs, scr):
        outs[0][...] = _rms(ins[0][...], ins[1][...]).astype(bf16)
    return rowcall(name, body, T, tm, [("row", x, D, 0), ("const", g)], [(D, bf16)])[0]


def resid_norm(name, T, tm, h_in, f, g_post, g_pre):
    def body(i, n, ins, outs, accs, scr):
        h = ins[0][...] + _rms(ins[1][...], ins[2][...])
        outs[0][...] = h
        if g_pre is not None:
            outs[1][...] = _rms(h, ins[3][...]).astype(bf16)
    ins = [("row", h_in, D, 0), ("row", f, D, 0), ("const", g_post)] + ([("const", g_pre)] if g_pre is not None else [])
    outs = [(D, f32)] + ([(D, bf16)] if g_pre is not None else [])
    return rowcall(name, body, T, tm, ins, outs)


def swiglu_act(name, T, tm, gate, up):
    def body(i, n, ins, outs, accs, scr):
        outs[0][...] = (_silu(ins[0][...]) * ins[1][...]).astype(bf16)
    return rowcall(name, body, T, tm, [("row", gate, D_FF, 0), ("row", up, D_FF, 0)], [(D_FF, bf16)])[0]


def swiglu_bwd(name, T, tm, gate, up, d_act):
    def body(i, n, ins, outs, accs, scr):
        _, vjp = jax.vjp(lambda a, b: _silu(a) * b, ins[0][...], ins[1][...])
        dg, du = vjp(ins[2][...])
        outs[0][...] = dg.astype(bf16)
        outs[1][...] = du.astype(bf16)
    return rowcall(name, body, T, tm, [("row", gate, D_FF, 0), ("row", up, D_FF, 0), ("row", d_act, D_FF, 0)],
                   [(D_FF, bf16), (D_FF, bf16)])


def final_loss_bwd(name, T, tm, h3, f2, tgt, g_post):
    def body(i, n, ins, outs, accs, scr):
        f, g = ins[1][...], ins[3][...]
        e = ins[0][...] + _rms(f, g) - ins[2][...]
        accs[0][...] += jnp.sum(jnp.sum(e * e, axis=-1, keepdims=True) * (0.5 / D), axis=0, keepdims=True)
        dh = e * (1.0 / D)
        df, dg = _rms_bwd(f, g, dh)
        outs[0][...] = dh
        outs[1][...] = df.astype(bf16)
        accs[1][...] += dg
    return rowcall(name, body, T, tm, [("row", h3, D, 0), ("row", f2, D, 0), ("row", tgt, D, 0), ("const", g_post)],
                   [(D, f32), (D, bf16)], accs=[(1, 1), (1, D)])


def bwd_pre_post(name, T, tm, h_out, f, d_res, d_n, g_pre, g_post, df_dtype):
    def body(i, n, ins, outs, accs, scr):
        dx, dgp = _rms_bwd(ins[0][...], ins[4][...], ins[3][...])
        dh = ins[2][...] + dx
        df, dgq = _rms_bwd(ins[1][...], ins[5][...], dh)
        outs[0][...] = dh
        outs[1][...] = df.astype(df_dtype)
        accs[0][...] += dgp
        accs[1][...] += dgq
    return rowcall(name, body, T, tm,
                   [("row", h_out, D, 0), ("row", f, D, 0), ("row", d_res, D, 0), ("row", d_n, D, 0), ("const", g_pre), ("const", g_post)],
                   [(D, f32), (D, df_dtype)], accs=[(1, D), (1, D)])


def bwd_post(name, T, tm, f, d_h, g_post):
    def body(i, n, ins, outs, accs, scr):
        df, dg = _rms_bwd(ins[0][...], ins[2][...], ins[1][...])
        outs[0][...] = df.astype(bf16)
        accs[0][...] += dg
    return rowcall(name, body, T, tm, [("row", f, D, 0), ("row", d_h, D, 0), ("const", g_post)], [(D, bf16)], accs=[(1, D)])


def bwd_pre(name, T, tm, h, d_res, d_n, g_pre):
    def body(i, n, ins, outs, accs, scr):
        dx, dg = _rms_bwd(ins[0][...], ins[3][...], ins[2][...])
        outs[0][...] = ins[1][...] + dx
        accs[0][...] += dg
    return rowcall(name, body, T, tm, [("row", h, D, 0), ("row", d_res, D, 0), ("row", d_n, D, 0), ("const", g_pre)],
                   [(D, f32)], accs=[(1, D)])


def _layer_norm_parts(x):
    mu = jnp.mean(x, axis=-1, keepdims=True)
    xc = x - mu
    r = lax.rsqrt(jnp.mean(xc * xc, axis=-1, keepdims=True) + EPS)
    return xc * r, r


def gmlp_fwd(name, T, tm, uvz, ln_g, ln_b, wm, bs):
    def body(i, n, ins, outs, accs, scr):
        gu = jax.nn.gelu(ins[0][...])
        xh, _ = _layer_norm_parts(jax.nn.gelu(ins[1][...]))
        vln = (xh * ins[2][...] + ins[3][...]).astype(bf16)
        for c in range(tm // CHUNK):
            rows = slice(c * CHUNK, (c + 1) * CHUNK)
            for h in range(GM_HEADS):
                cols = slice(h * GM_HD, (h + 1) * GM_HD)
                mixed = jnp.dot(ins[4][h], vln[rows, cols], preferred_element_type=f32) + ins[5][h]
                outs[0][rows, cols] = (gu[rows, cols] * mixed).astype(bf16)
    return rowcall(name, body, T, tm, [("row", uvz, D, 0), ("row", uvz, D, 1), ("const", ln_g), ("const", ln_b), ("const", wm), ("const", bs)],
                   [(D, bf16)])[0]


def gmlp_bwd(name, T, tm, uvz, d_ya, ln_g, ln_b, wm, bs):
    def body(i, n, ins, outs, accs, scr):
        u, v, dya = ins[0][...], ins[1][...], ins[2][...]
        gu, gelu_u_vjp = jax.vjp(jax.nn.gelu, u)
        gv, gelu_v_vjp = jax.vjp(jax.nn.gelu, v)
        xh, r = _layer_norm_parts(gv)
        lng = ins[3][...]
        vln = (xh * lng + ins[4][...]).astype(bf16)
        rr = lax.broadcasted_iota(jnp.int32, (CHUNK, CHUNK), 0)
        cc = lax.broadcasted_iota(jnp.int32, (CHUNK, CHUNK), 1)
        causal = (rr >= cc).astype(f32)
        dvln_ref = scr[0]
        dgu_ref = scr[1]
        for c in range(tm // CHUNK):
            rows = slice(c * CHUNK, (c + 1) * CHUNK)
            for h in range(GM_HEADS):
                cols = slice(h * GM_HD, (h + 1) * GM_HD)
                w = ins[5][h]
                blk = vln[rows, cols]
                mixed = jnp.dot(w, blk, preferred_element_type=f32) + ins[6][h]
                dy = dya[rows, cols]
                dgu_ref[rows, cols] = dy * mixed
                dm = dy * gu[rows, cols]
                accs[3][h] += jnp.sum(dm, axis=1, keepdims=True)
                accs[2][h] += _bdot(dm, blk, NT) * causal
                dvln_ref[rows, cols] = _bdot(w, dm, TN)
        dvln = dvln_ref[...]
        accs[0][...] += jnp.sum(dvln * xh, axis=0, keepdims=True)
        accs[1][...] += jnp.sum(dvln, axis=0, keepdims=True)
        dxh = dvln * lng
        dgv = r * (dxh - jnp.mean(dxh, axis=-1, keepdims=True) - xh * jnp.mean(dxh * xh, axis=-1, keepdims=True))
        outs[0][...] = gelu_u_vjp(dgu_ref[...])[0].astype(bf16)
        outs[1][...] = gelu_v_vjp(dgv)[0].astype(bf16)
    return rowcall(name, body, T, tm,
                   [("row", uvz, D, 0), ("row", uvz, D, 1), ("row", d_ya, D, 0), ("const", ln_g), ("const", ln_b), ("const", wm), ("const", bs)],
                   [(D, bf16), (D, bf16)], accs=[(1, D), (1, D), (GM_HEADS, CHUNK, CHUNK), (GM_HEADS, CHUNK, 1)],
                   scratch=[pltpu.VMEM((tm, D), f32), pltpu.VMEM((tm, D), f32)])


def _conv_pre(i, x_ref, halo_ref, w_ref, b_ref, scr, tm):
    scr[pl.ds(0, CONV_HALO), :] = jnp.where(i > 0, halo_ref[...], 0.0)
    scr[pl.ds(CONV_HALO, tm), :] = x_ref[...]
    pre = b_ref[...]
    for k in range(CONV_K):
        pre = pre + w_ref[pl.ds(k, 1), :] * scr[pl.ds(CONV_HALO - (CONV_K - 1) + k, tm), :]
    return pre


def conv_fwd(name, T, tm, xbc, conv_w, conv_b):
    def body(i, n, ins, outs, accs, scr):
        outs[0][...] = _silu(_conv_pre(i, ins[0], ins[1], ins[2], ins[3], scr[0], tm))
    return rowcall(name, body, T, tm, [("row", xbc, CONV_DIM, 0), ("prev", xbc, CONV_DIM, 0, CONV_HALO), ("const", conv_w), ("const", conv_b)],
                   [(CONV_DIM, f32)], scratch=[pltpu.VMEM((tm + CONV_HALO, CONV_DIM), f32)])[0]


def conv_bwd_pre(name, T, tm, xbc, d_xc, conv_w, conv_b):
    def body(i, n, ins, outs, accs, scr):
        pre = _conv_pre(i, ins[0], ins[1], ins[3], ins[4], scr[0], tm)
        _, vjp = jax.vjp(_silu, pre)
        dpre = vjp(ins[2][...])[0]
        outs[0][...] = dpre
        accs[1][...] += jnp.sum(dpre, axis=0, keepdims=True)
        for k in range(CONV_K):
            accs[0][pl.ds(k, 1), :] += jnp.sum(dpre * scr[0][pl.ds(CONV_HALO - (CONV_K - 1) + k, tm), :], axis=0, keepdims=True)
    return rowcall(name, body, T, tm,
                   [("row", xbc, CONV_DIM, 0), ("prev", xbc, CONV_DIM, 0, CONV_HALO), ("row", d_xc, CONV_DIM, 0), ("const", conv_w), ("const", conv_b)],
                   [(CONV_DIM, f32)], accs=[(CONV_K, CONV_DIM), (1, CONV_DIM)], scratch=[pltpu.VMEM((tm + CONV_HALO, CONV_DIM), f32)])


def conv_bwd_x(name, T, tm, d_pre, conv_w):
    def body(i, n, ins, outs, accs, scr):
        s = scr[0]
        s[pl.ds(0, tm), :] = ins[0][...]
        s[pl.ds(tm, CONV_HALO), :] = jnp.where(i < n - 1, ins[1][...], 0.0)
        dx = jnp.zeros((tm, CONV_DIM), f32)
        for k in range(CONV_K):
            dx = dx + ins[2][pl.ds(k, 1), :] * s[pl.ds(CONV_K - 1 - k, tm), :]
        outs[0][...] = dx.astype(bf16)
    return rowcall(name, body, T, tm, [("row", d_pre, CONV_DIM, 0), ("next", d_pre, CONV_DIM, 0, CONV_HALO), ("const", conv_w)],
                   [(CONV_DIM, bf16)], scratch=[pltpu.VMEM((tm + CONV_HALO, CONV_DIM), f32)])[0]


def _ssd_chunk(X4, dtr, B4, C4, S4, dtb, alog, dsk):
    L = CHUNK
    rr = lax.broadcasted_iota(jnp.int32, (L, L), 0)
    cc = lax.broadcasted_iota(jnp.int32, (L, L), 1)
    tril = rr >= cc
    lane = lax.broadcasted_iota(jnp.int32, (1, DT_PAD), 1)
    sub = lax.broadcasted_iota(jnp.int32, (DT_PAD, 1), 0)
    glane = lax.broadcasted_iota(jnp.int32, (1, SSM_HPG * SSM_P), 1) // SSM_P
    dt = _softplus(dtr + dtb)
    a = -jnp.exp(alog)
    dA = dt * a
    acum = jnp.dot(tril.astype(f32), dA, precision=HI, preferred_element_type=f32)
    acumT = acum.T
    tot = jnp.sum(dA, axis=0, keepdims=True)
    ys, Sn = [], []
    for g in range(SSM_GROUPS):
        hm = [(glane == r).astype(f32) for r in range(SSM_HPG)]
        cols = [jnp.sum(acum * (lane == SSM_HPG * g + r).astype(f32), axis=1, keepdims=True) for r in range(SSM_HPG)]
        dtc = [jnp.sum(dt * (lane == SSM_HPG * g + r).astype(f32), axis=1, keepdims=True) for r in range(SSM_HPG)]
        tots = [jnp.sum(tot * (lane == SSM_HPG * g + r).astype(f32), axis=1, keepdims=True) for r in range(SSM_HPG)]
        dsc = [jnp.sum(dsk * (lane == SSM_HPG * g + r).astype(f32), axis=1, keepdims=True) for r in range(SSM_HPG)]
        x = X4[g]
        xdt = x * sum(dtc[r] * hm[r] for r in range(SSM_HPG))
        cb = _bdot(C4[g], B4[g], NT)
        y = x * sum(dsc[r] * hm[r] for r in range(SSM_HPG))
        for r in range(SSM_HPG):
            row = jnp.sum(acumT * (sub == SSM_HPG * g + r).astype(f32), axis=0, keepdims=True)
            dec = jnp.exp(jnp.where(tril, cols[r] - row, -jnp.inf))
            y = y + _bdot(cb * dec, xdt * hm[r])
        y = y + _bdot(C4[g], S4[g]) * sum(jnp.exp(cols[r]) * hm[r] for r in range(SSM_HPG))
        dte = sum(jnp.exp(tots[r] - cols[r]) * hm[r] for r in range(SSM_HPG))
        s_new = S4[g] * sum(jnp.exp(tots[r]) * hm[r] for r in range(SSM_HPG)) + _bdot(B4[g], xdt * dte, TN)
        ys.append(y)
        Sn.append(s_new)
    return tuple(ys), tuple(Sn)


def _ssd_ins(xc, dtr):
    gw = SSM_HPG * SSM_P
    ins = [("row", xc, gw, g) for g in range(SSM_GROUPS)]
    ins += [("row", xc, SSM_N, D // SSM_N + g) for g in range(SSM_GROUPS)]
    ins += [("row", xc, SSM_N, D // SSM_N + SSM_GROUPS + g) for g in range(SSM_GROUPS)]
    ins += [("row", dtr, DT_PAD, 0)]
    return ins


def ssd_fwd(name, T, xc, dtr, dtb, alog, dsk):
    gw = SSM_HPG * SSM_P

    def body(i, n, ins, outs, accs, scr):
        S = scr[0]

        @pl.when(i == 0)
        def _():
            S[...] = jnp.zeros(S.shape, f32)
        X4 = tuple(ins[g][...] for g in range(4))
        B4 = tuple(ins[4 + g][...] for g in range(4))
        C4 = tuple(ins[8 + g][...] for g in range(4))
        S4 = tuple(S[:, g * gw:(g + 1) * gw] for g in range(4))
        outs[1][...] = S[...]
        ys, Sn = _ssd_chunk(X4, ins[12][...], B4, C4, S4, ins[13][...], ins[14][...], ins[15][...])
        for g in range(4):
            outs[0][:, g * gw:(g + 1) * gw] = ys[g]
            S[:, g * gw:(g + 1) * gw] = Sn[g]
    ins = _ssd_ins(xc, dtr) + [("const", dtb), ("const", alog), ("const", dsk)]
    return rowcall(name, body, T, CHUNK, ins, [(D, f32), (D, f32)], scratch=[pltpu.VMEM((SSM_N, D), f32)])


def ssd_bwd(name, T, xc, dtr, sprev, d_y, dtb, alog, dsk):
    gw = SSM_HPG * SSM_P

    def body(i, n, ins, outs, accs, scr):
        dS = scr[0]

        @pl.when(i == n - 1)
        def _():
            dS[...] = jnp.zeros(dS.shape, f32)
        X4 = tuple(ins[g][...] for g in range(4))
        B4 = tuple(ins[4 + g][...] for g in range(4))
        C4 = tuple(ins[8 + g][...] for g in range(4))
        S4 = tuple(ins[13 + g][...] for g in range(4))
        dY4 = tuple(ins[17 + g][...] for g in range(4))
        dS4 = tuple(dS[:, g * gw:(g + 1) * gw] for g in range(4))
        _, vjp = jax.vjp(_ssd_chunk, X4, ins[12][...], B4, C4, S4, ins[21][...], ins[22][...], ins[23][...])
        dX4, ddtr, dB4, dC4, dSp, ddtb, dalog, ddsk = vjp((dY4, dS4))
        for g in range(4):
            outs[0][:, g * gw:(g + 1) * gw] = dX4[g]
            outs[0][:, D + g * SSM_N:D + (g + 1) * SSM_N] = dB4[g]
            outs[0][:, D + (SSM_GROUPS + g) * SSM_N:D + (SSM_GROUPS + g + 1) * SSM_N] = dC4[g]
            dS[:, g * gw:(g + 1) * gw] = dSp[g]
        outs[1][...] = ddtr.astype(bf16)
        accs[0][...] += ddtb
        accs[1][...] += dalog
        accs[2][...] += ddsk
    ins = _ssd_ins(xc, dtr) + [("row", sprev, gw, g) for g in range(4)] + [("row", d_y, gw, g) for g in range(4)]
    ins += [("const", dtb), ("const", alog), ("const", dsk)]
    return rowcall(name, body, T, CHUNK, ins, [(CONV_DIM, f32), (DT_PAD, bf16)], accs=[(1, DT_PAD)] * 3,
                   scratch=[pltpu.VMEM((SSM_N, D), f32)], reverse=True)


def _gate_group(y, z, g):
    return _rms(y * _silu(z), g)


def gate_fwd(name, T, tm, y, uvz, gn):
    def body(i, n, ins, outs, accs, scr):
        for g in range(SSM_GROUPS):
            cols = slice(g * 256, (g + 1) * 256)
            outs[0][:, cols] = _gate_group(ins[0][:, cols], ins[1][:, cols], ins[2][:, cols]).astype(bf16)
    return rowcall(name, body, T, tm, [("row", y, D, 0), ("row", uvz, D, 2), ("const", gn)], [(D, bf16)])[0]


def gate_bwd(name, T, tm, y, uvz, d_yb, gn):
    def body(i, n, ins, outs, accs, scr):
        for g in range(SSM_GROUPS):
            cols = slice(g * 256, (g + 1) * 256)
            _, vjp = jax.vjp(_gate_group, ins[0][:, cols], ins[1][:, cols], ins[3][:, cols])
            dy, dz, dg = vjp(ins[2][:, cols])
            outs[0][:, cols] = dy
            outs[1][:, cols] = dz.astype(bf16)
            accs[0][:, cols] += dg
    return rowcall(name, body, T, tm, [("row", y, D, 0), ("row", uvz, D, 2), ("row", d_yb, D, 0), ("const", gn)],
                   [(D, f32), (D, bf16)], accs=[(1, D)])


def _pool_diff(i, tm, h_ref, halo_ref, g_ref, scr):
    g = g_ref[...]
    yn = _rms(h_ref[...], g)
    scr[pl.ds(0, POOL_HALO), :] = jnp.where(i > 0, _rms(halo_ref[...], g), 0.0)
    scr[pl.ds(POOL_HALO, tm), :] = yn
    pos = (i * tm + lax.broadcasted_iota(jnp.int32, (tm, 1), 0) + 1).astype(f32)
    parts = []
    for gi, win in enumerate(POOL_WINDOWS):
        cols = slice(gi * POOL_GD, (gi + 1) * POOL_GD)
        s = scr[pl.ds(POOL_HALO, tm), cols]
        for j in range(1, win):
            s = s + scr[pl.ds(POOL_HALO - j, tm), cols]
        parts.append(s / jnp.minimum(pos, float(win)) - yn[:, cols])
    return parts


def pool_fwd(name, T, tm, h2, g_pre, pw, pb, psc):
    def body(i, n, ins, outs, accs, scr):
        parts = _pool_diff(i, tm, ins[0], ins[1], ins[2], scr[0])
        for gi in range(len(POOL_WINDOWS)):
            cols = slice(gi * POOL_GD, (gi + 1) * POOL_GD)
            o = _bdot(parts[gi], ins[3][gi]) + ins[4][:, cols]
            outs[0][:, cols] = o * ins[5][:, cols]
    return rowcall(name, body, T, tm, [("row", h2, D, 0), ("prev", h2, D, 0, POOL_HALO), ("const", g_pre), ("const", pw), ("const", pb), ("const", psc)],
                   [(D, f32)], scratch=[pltpu.VMEM((tm + POOL_HALO, D), f32)])[0]


def pool_bwd(name, T, tm, h2, d_pm, d_res, g_pre, pw, pb, psc):
    def body(i, n, ins, outs, accs, scr):
        parts = _pool_diff(i, tm, ins[0], ins[1], ins[5], scr[0])
        dpm = ins[2][...]
        psc_v = ins[8][...]
        dps = dpm * psc_v
        dps_halo = jnp.where(i < n - 1, ins[3][...] * psc_v, 0.0)
        accs[1][...] += jnp.sum(dps, axis=0, keepdims=True)
        pos = (i * tm + lax.broadcasted_iota(jnp.int32, (tm, 1), 0) + 1).astype(f32)
        pos_h = ((i + 1) * tm + lax.broadcasted_iota(jnp.int32, (POOL_HALO, 1), 0) + 1).astype(f32)
        r_scr = scr[1]
        dyn_scr = scr[2]
        for gi, win in enumerate(POOL_WINDOWS):
            cols = slice(gi * POOL_GD, (gi + 1) * POOL_GD)
            w = ins[6][gi]
            o = _bdot(parts[gi], w) + ins[7][:, cols]
            accs[2][:, cols] += jnp.sum(dpm[:, cols] * o, axis=0, keepdims=True)
            accs[0][gi] += _bdot(parts[gi], dps[:, cols], TN)
            q = _bdot(dps[:, cols], w, NT)
            qh = _bdot(dps_halo[:, cols], w, NT)
            r_scr[pl.ds(0, tm), cols] = q / jnp.minimum(pos, float(win))
            r_scr[pl.ds(tm, POOL_HALO), cols] = qh / jnp.minimum(pos_h, float(win))
            s = r_scr[pl.ds(0, tm), cols]
            for j in range(1, win):
                s = s + r_scr[pl.ds(j, tm), cols]
            dyn_scr[:, cols] = s - q
        dx, dg = _rms_bwd(ins[0][...], ins[5][...], dyn_scr[...])
        outs[0][...] = ins[4][...] + dx
        accs[3][...] += dg
    ins = [("row", h2, D, 0), ("prev", h2, D, 0, POOL_HALO), ("row", d_pm, D, 0), ("next", d_pm, D, 0, POOL_HALO), ("row", d_res, D, 0),
           ("const", g_pre), ("const", pw), ("const", pb), ("const", psc)]
    return rowcall(name, body, T, tm, ins, [(D, f32)], accs=[(4, POOL_GD, POOL_GD), (1, D), (1, D), (1, D)],
                   scratch=[pltpu.VMEM((tm + POOL_HALO, D), f32), pltpu.VMEM((tm + POOL_HALO, D), f32), pltpu.VMEM((tm, D), f32)])


def local_step(T, x, tgt, W):
    tm = 512 if T >= 1024 else T // 2
    TKW = 1024 if T >= 1024 else T
    ng = W["norm_g"]
    g = lambda l, j: ng[l, j][None, :]
    G = {}

    def ffn_fwd(tag, n_bf, l):
        gate = matmul(f"ffn{tag}_gate", [(n_bf, W["wg"][l])], "nn", f32, tm, 1408)
        up = matmul(f"ffn{tag}_up", [(n_bf, W["wu"][l])], "nn", f32, tm, 1408)
        act = swiglu_act(f"ffn{tag}_act", T, tm, gate, up)
        f = matmul(f"ffn{tag}_down", [(act, W["wd"][l])], "nn", f32, tm, 1024)
        return gate, up, act, f

    def ffn_bwd(tag, l, n_bf, gate, up, act, d_f):
        d_act = matmul(f"ffn{tag}_dact", [(d_f, W["wd"][l])], "nt", f32, tm, 1408)
        dwd = matmul(f"ffn{tag}_dwd", [(act, d_f)], "tn", f32, 1408, 1024, TKW)
        d_gate, d_up = swiglu_bwd(f"ffn{tag}_dgu", T, tm, gate, up, d_act)
        d_n = matmul(f"ffn{tag}_dn", [(d_gate, W["wg"][l]), (d_up, W["wu"][l])], "nt", f32, tm, 1024)
        dwg = matmul(f"ffn{tag}_dwg", [(n_bf, d_gate)], "tn", f32, 1024, 1408, TKW)
        dwu = matmul(f"ffn{tag}_dwu", [(n_bf, d_up)], "tn", f32, 1024, 1408, TKW)
        return d_n, dwg, dwu, dwd

    y0 = rms_to_bf16("l0_prenorm", T, tm, x, g(0, 0))
    uvz = matmul("in_uvz", [(y0, W["w_uvz"])], "nn", f32, tm, 1024)
    xbc = matmul("in_xbc", [(y0, W["w_xbc"])], "nn", f32, tm, 1024)
    dtr = matmul("in_dt", [(y0, W["w_dt"])], "nn", f32, tm, DT_PAD)
    y_a = gmlp_fwd("gmlp_fwd", T, tm, uvz, W["ln_g"], W["ln_b"], W["wm"], W["bs"])
    xc = conv_fwd("conv_fwd", T, tm, xbc, W["conv_w"], W["conv_b"])
    y_ssd, sprev = ssd_fwd("ssd_fwd", T, xc, dtr, W["dtb"], W["alog"], W["dsk"])
    y_b = gate_fwd("gate_fwd", T, tm, y_ssd, uvz, W["gn"])
    mixo = matmul("out_proj", [(y_a, W["wo_a"]), (y_b, W["wo_b"])], "nn", f32, tm, 1024)
    h1, n1 = resid_norm("l0_mix_resid", T, tm, x, mixo, g(0, 1), g(0, 2))
    gate0, up0, act0, f1 = ffn_fwd("0", n1, 0)
    (h2,) = resid_norm("l0_ffn_resid", T, tm, h1, f1, g(0, 3), None)
    pm = pool_fwd("pool_fwd", T, tm, h2, g(1, 0), W["pool_w"], W["pool_b"], W["pool_scale"])
    h3, n3 = resid_norm("l1_mix_resid", T, tm, h2, pm, g(1, 1), g(1, 2))
    gate1, up1, act1, f2 = ffn_fwd("1", n3, 1)
    dh4, d_f2, loss_acc, dg13 = final_loss_bwd("loss_bwd", T, tm, h3, f2, tgt, g(1, 3))
    d_n3, dwg1, dwu1, dwd1 = ffn_bwd("1", 1, n3, gate1, up1, act1, d_f2)
    d_h3, d_pm, dg12, dg11 = bwd_pre_post("l1_mix_bwd", T, tm, h3, pm, dh4, d_n3, g(1, 2), g(1, 1), f32)
    d_h2, G["pool_w"], G["pool_b"], G["pool_scale"], dg10 = pool_bwd("pool_bwd", T, tm, h2, d_pm, d_h3, g(1, 0), W["pool_w"], W["pool_b"], W["pool_scale"])
    d_f1, dg03 = bwd_post("l0_ffn_bwd", T, tm, f1, d_h2, g(0, 3))
    d_n1, dwg0, dwu0, dwd0 = ffn_bwd("0", 0, n1, gate0, up0, act0, d_f1)
    d_h1, d_mixo, dg02, dg01 = bwd_pre_post("l0_mix_bwd", T, tm, h1, mixo, d_h2, d_n1, g(0, 2), g(0, 1), bf16)
    d_ya = matmul("out_proj_dya", [(d_mixo, W["wo_a"])], "nt", f32, tm, 1024)
    d_yb = matmul("out_proj_dyb", [(d_mixo, W["wo_b"])], "nt", f32, tm, 1024)
    dwo_a = matmul("out_proj_dwa", [(y_a, d_mixo)], "tn", f32, 1024, 1024, TKW)
    dwo_b = matmul("out_proj_dwb", [(y_b, d_mixo)], "tn", f32, 1024, 1024, TKW)
    d_yssd, d_z, G["gn"] = gate_bwd("gate_bwd", T, tm, y_ssd, uvz, d_yb, W["gn"])
    d_xc, d_dtr, G["dtb"], G["alog"], G["dsk"] = ssd_bwd("ssd_bwd", T, xc, dtr, sprev, d_yssd, W["dtb"], W["alog"], W["dsk"])
    d_pre, G["conv_w"], G["conv_b"] = conv_bwd_pre("conv_bwd_pre", T, tm, xbc, d_xc, W["conv_w"], W["conv_b"])
    d_xbc = conv_bwd_x("conv_bwd_x", T, tm, d_pre, W["conv_w"])
    d_u, d_v, G["ln_g"], G["ln_b"], G["wm"], G["bs"] = gmlp_bwd("gmlp_bwd", T, tm, uvz, d_ya, W["ln_g"], W["ln_b"], W["wm"], W["bs"])
    w_u, w_v, w_z = W["w_uvz"][:, :D], W["w_uvz"][:, D:2 * D], W["w_uvz"][:, 2 * D:]
    d_y0 = matmul("in_dy0", [(d_u, w_u), (d_v, w_v), (d_z, w_z), (d_xbc, W["w_xbc"]), (d_dtr, W["w_dt"])], "nt", f32, tm, 1024)
    G["w_u"] = matmul("in_dwu", [(y0, d_u)], "tn", f32, 1024, 1024, TKW)
    G["w_v"] = matmul("in_dwv", [(y0, d_v)], "tn", f32, 1024, 1024, TKW)
    G["w_z"] = matmul("in_dwz", [(y0, d_z)], "tn", f32, 1024, 1024, TKW)
    G["w_xbc"] = matmul("in_dwxbc", [(y0, d_xbc)], "tn", f32, 1024, 1024, TKW)
    G["w_dt"] = matmul("in_dwdt", [(y0, d_dtr)], "tn", f32, 1024, DT_PAD, TKW)
    grad_x, dg00 = bwd_pre("l0_pre_bwd", T, tm, x, d_h1, d_y0, g(0, 0))
    G["norm_g"] = jnp.stack([jnp.concatenate([dg00, dg01, dg02, dg03], 0), jnp.concatenate([dg10, dg11, dg12, dg13], 0)])
    G["wo_a"], G["wo_b"] = dwo_a, dwo_b
    G["wg"], G["wu"], G["wd"] = [dwg0, dwg1], [dwu0, dwu1], [dwd0, dwd1]
    return loss_acc, grad_x, G


def build_weights(Wf):
    causal = jnp.tril(jnp.ones((CHUNK, CHUNK), bool))
    w_in = Wf["w_in"].astype(bf16)
    pad16 = lambda v: jnp.pad(v.reshape(1, N_HEADS).astype(f32), ((0, 0), (0, DT_PAD - N_HEADS)))
    return {
        "norm_g": Wf["norm_g"],
        "w_uvz": w_in[:, :3 * D], "w_xbc": w_in[:, 3 * D:3 * D + CONV_DIM],
        "w_dt": jnp.pad(w_in[:, 3 * D + CONV_DIM:], ((0, 0), (0, DT_PAD - N_HEADS))),
        "ln_g": Wf["gm_ln_g"].reshape(1, D), "ln_b": Wf["gm_ln_b"].reshape(1, D),
        "wm": jnp.where(causal[None], Wf["gm_ws"], 0).astype(bf16), "bs": Wf["gm_bs"].reshape(GM_HEADS, CHUNK, 1),
        "conv_w": Wf["conv_w"], "conv_b": Wf["conv_b"].reshape(1, CONV_DIM),
        "dtb": pad16(Wf["dt_bias"]), "alog": pad16(Wf["a_log"]), "dsk": pad16(Wf["d_skip"]),
        "gn": Wf["ssm_norm_g"].reshape(1, D),
        "wo_a": Wf["w_out"][:D].astype(bf16), "wo_b": Wf["w_out"][D:].astype(bf16),
        "pool_w": Wf["pool_w"].astype(bf16), "pool_b": Wf["pool_b"].reshape(1, D), "pool_scale": Wf["pool_scale"].reshape(1, D),
        "wg": [Wf["ffn_w_gate"][l].astype(bf16) for l in range(2)],
        "wu": [Wf["ffn_w_up"][l].astype(bf16) for l in range(2)],
        "wd": [Wf["ffn_w_down"][l].astype(bf16) for l in range(2)],
    }


def grads_full(G):
    return {
        "norm_g": G["norm_g"],
        "w_in": jnp.concatenate([G["w_u"], G["w_v"], G["w_z"], G["w_xbc"], G["w_dt"][:, :N_HEADS]], axis=1),
        "gm_ln_g": G["ln_g"].reshape(D), "gm_ln_b": G["ln_b"].reshape(D),
        "gm_ws": G["wm"], "gm_bs": G["bs"].reshape(GM_HEADS, CHUNK),
        "conv_w": G["conv_w"], "conv_b": G["conv_b"].reshape(CONV_DIM),
        "dt_bias": G["dtb"][0, :N_HEADS], "a_log": G["alog"][0, :N_HEADS], "d_skip": G["dsk"][0, :N_HEADS],
        "ssm_norm_g": G["gn"].reshape(D),
        "w_out": jnp.concatenate([G["wo_a"], G["wo_b"]], axis=0),
        "pool_w": G["pool_w"], "pool_b": G["pool_b"].reshape(4, POOL_GD), "pool_scale": G["pool_scale"].reshape(D),
        "ffn_w_gate": jnp.stack(G["wg"]), "ffn_w_up": jnp.stack(G["wu"]), "ffn_w_down": jnp.stack(G["wd"]),
    }


MESH_ID = pl.DeviceIdType.MESH
ANY = pl.BlockSpec(memory_space=pl.ANY)


def comm_call(name, operands, out_shapes, plan):
    n_in = len(operands)
    n_out = len(out_shapes)
    probe = plan((0, 0, 0), [None] * n_in, [None] * n_out, True)
    n_remote, n_local = probe

    def body(*refs):
        in_refs, out_refs = refs[:n_in], refs[n_in:n_in + n_out]
        send_sems, recv_sems, local_sems = refs[n_in + n_out:]
        me = (lax.axis_index("x"), lax.axis_index("y"), lax.axis_index("c"))
        remote, local = plan(me, in_refs, out_refs, False)
        locals_ = [pltpu.make_async_copy(s, d, local_sems.at[j]) for j, (s, d) in enumerate(local)]
        for cp in locals_:
            cp.start()
        sends = []
        for k, (flip, src, dst, _) in enumerate(remote):
            peer = tuple((1 - m) if f else m for m, f in zip(me, flip))
            cp = pltpu.make_async_remote_copy(src_ref=src, dst_ref=dst, send_sem=send_sems.at[k], recv_sem=recv_sems.at[k],
                                              device_id=peer, device_id_type=MESH_ID)
            cp.start()
            sends.append(cp)
        for k, (flip, src, dst, landing) in enumerate(remote):
            peer = tuple((1 - m) if f else m for m, f in zip(me, flip))
            pltpu.make_async_remote_copy(src_ref=landing, dst_ref=landing, send_sem=send_sems.at[k], recv_sem=recv_sems.at[k],
                                         device_id=peer, device_id_type=MESH_ID).wait_recv()
        for cp in sends:
            cp.wait_send()
        for cp in locals_:
            cp.wait()

    return pl.pallas_call(
        body, name=name, out_shape=list(out_shapes), in_specs=[ANY] * n_in, out_specs=[ANY] * n_out,
        scratch_shapes=[pltpu.SemaphoreType.DMA((n_remote,)), pltpu.SemaphoreType.DMA((n_remote,)), pltpu.SemaphoreType.DMA((max(n_local, 1),))],
    )(*operands)


CHIP_FLIPS = ((1, 0, 0), (0, 1, 0), (1, 1, 0))
PAIR_FLIP = (0, 0, 1)


def gather_over_chips(name, arrs):
    def plan(me, ins, outs, count):
        if count:
            return len(CHIP_FLIPS) * len(arrs), len(arrs)
        k = 2 * me[0] + me[1]
        remote, local = [], []
        for a in range(len(arrs)):
            for flip in CHIP_FLIPS:
                kp = 2 * ((1 - me[0]) if flip[0] else me[0]) + ((1 - me[1]) if flip[1] else me[1])
                remote.append((flip, ins[a], outs[a].at[k], outs[a].at[kp]))
            local.append((ins[a], outs[a].at[k]))
        return remote, local
    return comm_call(name, arrs, [SDS((4,) + a.shape, a.dtype) for a in arrs], plan)


def pair_split_exchange(name, p, rh):
    def plan(me, ins, outs, count):
        if count:
            return 1, 1
        c = me[2]
        mine = ins[0].at[:, pl.ds(pl.multiple_of(c * rh, 8), rh), :]
        theirs = ins[0].at[:, pl.ds(pl.multiple_of((1 - c) * rh, 8), rh), :]
        return [(PAIR_FLIP, theirs, outs[1], outs[1])], [(mine, outs[0])]
    s = SDS((4, rh, p.shape[2]), p.dtype)
    return comm_call(name, [p], [s, s], plan)


def scatter_over_chips(name, cs):
    def plan(me, ins, outs, count):
        if count:
            return len(CHIP_FLIPS), 1
        k = 2 * me[0] + me[1]
        remote = []
        for flip in CHIP_FLIPS:
            kp = 2 * ((1 - me[0]) if flip[0] else me[0]) + ((1 - me[1]) if flip[1] else me[1])
            remote.append((flip, ins[0].at[kp], outs[0].at[k], outs[0].at[kp]))
        return remote, [(ins[0].at[k], outs[0].at[k])]
    return comm_call(name, [cs], [SDS(cs.shape, cs.dtype)], plan)[0]


def pair_gather(name, half):
    def plan(me, ins, outs, count):
        if count:
            return 1, 1
        c = me[2]
        return [(PAIR_FLIP, ins[0], outs[0].at[c], outs[0].at[1 - c])], [(ins[0], outs[0].at[c])]
    return comm_call(name, [half], [SDS((2,) + half.shape, half.dtype)], plan)[0]


def _row_tile(rows, cap=512):
    if rows <= cap:
        return rows
    t = cap - cap % 8
    while rows % t:
        t -= 8
    return t


def add_rows(name, terms):
    R, C = terms[0].shape
    tr = _row_tile(R)

    def kern(*refs):
        s = refs[0][...]
        for r in refs[1:len(terms)]:
            s = s + r[...]
        refs[len(terms)][...] = s
    spec = pl.BlockSpec((tr, C), lambda i: (i, 0))
    return pl.pallas_call(kern, name=name, grid=(R // tr,), in_specs=[spec] * len(terms), out_specs=spec, out_shape=SDS((R, C), f32),
                          compiler_params=pltpu.CompilerParams(dimension_semantics=("parallel",)))(*terms)


def adamw(name, w, g, m, v):
    R, C = w.shape
    tr = _row_tile(R, 256)

    def kern(w_ref, g_ref, m_ref, v_ref, d_ref, mo_ref, vo_ref):
        gg = g_ref[...]
        mn = ADAM_B1 * m_ref[...] + (1.0 - ADAM_B1) * gg
        vn = ADAM_B2 * v_ref[...] + (1.0 - ADAM_B2) * jnp.square(gg)
        m_hat = mn / (1.0 - ADAM_B1 ** ADAM_STEP)
        v_hat = vn / (1.0 - ADAM_B2 ** ADAM_STEP)
        d_ref[...] = -ADAM_LR * (m_hat / (jnp.sqrt(v_hat) + ADAM_EPS) + ADAM_WD * w_ref[...])
        mo_ref[...] = mn
        vo_ref[...] = vn
    spec = pl.BlockSpec((tr, C), lambda i: (i, 0))
    s = SDS((R, C), f32)
    return pl.pallas_call(kern, name=name, grid=(R // tr,), in_specs=[spec] * 4, out_specs=[spec] * 3, out_shape=[s, s, s],
                          compiler_params=pltpu.CompilerParams(dimension_semantics=("parallel",)))(w, g, m, v)


WEIGHT_NAMES = ("norm_g", "w_in", "gm_ln_g", "gm_ln_b", "gm_ws", "gm_bs", "conv_w", "conv_b", "dt_bias", "a_log", "d_skip",
                "ssm_norm_g", "w_out", "pool_w", "pool_b", "pool_scale", "ffn_w_gate", "ffn_w_up", "ffn_w_down")
BIG = ("w_in", "w_out", "pool_w", "ffn_w_gate", "ffn_w_up", "ffn_w_down")
SMALL = ("norm_g", "conv_w", "pool_b", "pool_scale")
REPL = ("gm_ln_g", "gm_ln_b", "gm_ws", "gm_bs", "conv_b", "dt_bias", "a_log", "d_skip", "ssm_norm_g")
N_CHIPS = 4
BIG_ROWS = 6084
BIG_ROWS_BF16 = 6096
SMALL_ROWS = 8
REPL_ROWS = 72
SLOT_ROWS = 6272
HALF_ROWS = SLOT_ROWS // 2


def _flat_rows(pieces, rows):
    v = jnp.concatenate([p.reshape(-1) for p in pieces])
    return jnp.pad(v, (0, rows * D - v.shape[0])).reshape(rows, D)


def _shard_of(name, full, k):
    if name == "w_in":
        return full[:, k * 1284:(k + 1) * 1284]
    if name == "w_out":
        return full[k * 512:(k + 1) * 512]
    if name == "pool_w":
        return full[:, k * 64:(k + 1) * 64, :]
    if name in ("ffn_w_gate", "ffn_w_up"):
        return full[:, :, k * 704:(k + 1) * 704]
    if name == "ffn_w_down":
        return full[:, k * 704:(k + 1) * 704, :]
    if name == "norm_g":
        return full[:, :, k * 256:(k + 1) * 256]
    if name == "conv_w":
        return full[:, k * 512:(k + 1) * 512]
    if name == "pool_b":
        return full[:, k * 64:(k + 1) * 64]
    if name == "pool_scale":
        return full[k * 256:(k + 1) * 256]
    raise KeyError(name)


SHARD_AXIS = {"w_in": 1, "w_out": 0, "pool_w": 1, "ffn_w_gate": 2, "ffn_w_up": 2, "ffn_w_down": 1,
              "norm_g": 2, "conv_w": 1, "pool_b": 1, "pool_scale": 0}
FULL_LAYERED = ("norm_g", "ffn_w_gate", "ffn_w_up", "ffn_w_down")


def _drop1(name, a):
    return a if name in FULL_LAYERED else a[0]


def _split_rows(flat2d, shapes):
    v = flat2d.reshape(-1)
    out, off = [], 0
    for s in shapes:
        n = math.prod(s)
        out.append(v[off:off + n].reshape(s))
        off += n
    return out


def kernel(x, norm_g, w_in, gm_ln_g, gm_ln_b, gm_ws, gm_bs, conv_w, conv_b, dt_bias, a_log, d_skip, ssm_norm_g, w_out, pool_w, pool_b, pool_scale, ffn_w_gate, ffn_w_up, ffn_w_down, loss_target, m_norm_g, m_w_in, m_gm_ln_g, m_gm_ln_b, m_gm_ws, m_gm_bs, m_conv_w, m_conv_b, m_dt_bias, m_a_log, m_d_skip, m_ssm_norm_g, m_w_out, m_pool_w, m_pool_b, m_pool_scale, m_ffn_w_gate, m_ffn_w_up, m_ffn_w_down, v_norm_g, v_w_in, v_gm_ln_g, v_gm_ln_b, v_gm_ws, v_gm_bs, v_conv_w, v_conv_b, v_dt_bias, v_a_log, v_d_skip, v_ssm_norm_g, v_w_out, v_pool_w, v_pool_b, v_pool_scale, v_ffn_w_gate, v_ffn_w_up, v_ffn_w_down):
    T = x.shape[1]
    w_sh = dict(zip(WEIGHT_NAMES, (norm_g, w_in, gm_ln_g, gm_ln_b, gm_ws, gm_bs, conv_w, conv_b, dt_bias, a_log, d_skip, ssm_norm_g, w_out,
                                   pool_w, pool_b, pool_scale, ffn_w_gate, ffn_w_up, ffn_w_down)))
    m_sh = dict(zip(WEIGHT_NAMES, (m_norm_g, m_w_in, m_gm_ln_g, m_gm_ln_b, m_gm_ws, m_gm_bs, m_conv_w, m_conv_b, m_dt_bias, m_a_log, m_d_skip,
                                   m_ssm_norm_g, m_w_out, m_pool_w, m_pool_b, m_pool_scale, m_ffn_w_gate, m_ffn_w_up, m_ffn_w_down)))
    v_sh = dict(zip(WEIGHT_NAMES, (v_norm_g, v_w_in, v_gm_ln_g, v_gm_ln_b, v_gm_ws, v_gm_bs, v_conv_w, v_conv_b, v_dt_bias, v_a_log, v_d_skip,
                                   v_ssm_norm_g, v_w_out, v_pool_w, v_pool_b, v_pool_scale, v_ffn_w_gate, v_ffn_w_up, v_ffn_w_down)))

    big_pack = _flat_rows([w_sh[n].astype(bf16) for n in BIG], BIG_ROWS_BF16)
    small_pack = _flat_rows([w_sh[n] for n in SMALL], SMALL_ROWS)
    big_all, small_all = gather_over_chips("gather_weights", [big_pack, small_pack])
    big_shapes = [_drop1(n, w_sh[n]).shape for n in BIG]
    small_shapes = [_drop1(n, w_sh[n]).shape for n in SMALL]
    parts_big = [_split_rows(big_all[k], big_shapes) for k in range(N_CHIPS)]
    parts_small = [_split_rows(small_all[k], small_shapes) for k in range(N_CHIPS)]
    Wf = {n: w_sh[n][0] for n in REPL}
    for j, n in enumerate(BIG):
        Wf[n] = jnp.concatenate([parts_big[k][j] for k in range(N_CHIPS)], axis=SHARD_AXIS[n])
    for j, n in enumerate(SMALL):
        Wf[n] = jnp.concatenate([parts_small[k][j] for k in range(N_CHIPS)], axis=SHARD_AXIS[n])

    loss_acc, grad_x, G = local_step(T, x[0], loss_target[0], build_weights(Wf))
    Gf = grads_full(G)

    repl_pieces = [Gf[n] for n in REPL]
    slots = [_flat_rows([_shard_of(n, Gf[n], k) for n in BIG] + [_flat_rows([_shard_of(n, Gf[n], k) for n in SMALL], SMALL_ROWS)]
                        + repl_pieces, SLOT_ROWS) for k in range(N_CHIPS)]
    packs = jnp.stack(slots)
    own, got = pair_split_exchange("grads_pair_split", packs, HALF_ROWS)
    chip_sum = add_rows("grads_pair_sum", [own.reshape(-1, D), got.reshape(-1, D)]).reshape(N_CHIPS, HALF_ROWS, D)
    landed = scatter_over_chips("grads_scatter", chip_sum)
    half = add_rows("grads_chip_sum", [landed[k] for k in range(N_CHIPS)])
    total = pair_gather("grads_pair_gather", half).reshape(SLOT_ROWS, D)

    shard_shapes = [_drop1(n, w_sh[n]).shape for n in BIG]
    g_big = _split_rows(total[:BIG_ROWS], shard_shapes)
    g_small = _split_rows(total[BIG_ROWS:BIG_ROWS + SMALL_ROWS], small_shapes)
    g_repl = _split_rows(total[BIG_ROWS + SMALL_ROWS:BIG_ROWS + SMALL_ROWS + REPL_ROWS], [w_sh[n][0].shape for n in REPL])
    grads = {}
    for n, gsh in list(zip(BIG, g_big)) + list(zip(SMALL, g_small)) + list(zip(REPL, g_repl)):
        grads[n] = gsh.reshape(w_sh[n].shape)

    delta, new_m, new_v = {}, {}, {}
    for n in WEIGHT_NAMES:
        shp = w_sh[n].shape
        two_d = (-1, shp[-1])
        d_, m_, v_ = adamw("adamw_" + n, w_sh[n].reshape(two_d), grads[n].reshape(two_d), m_sh[n].reshape(two_d), v_sh[n].reshape(two_d))
        delta[n], new_m[n], new_v[n] = d_.reshape(shp), m_.reshape(shp), v_.reshape(shp)

    loss = lax.psum(loss_acc[0, 0], ("x", "y", "c"))
    return (loss, grad_x[None], *[grads[n] for n in WEIGHT_NAMES], *[delta[n] for n in WEIGHT_NAMES],
            *[new_m[n] for n in WEIGHT_NAMES], *[new_v[n] for n in WEIGHT_NAMES])
```

```python
import functools
import math

import jax
import jax.numpy as jnp
from jax import lax
from jax.experimental import pallas as pl
from jax.experimental.pallas import tpu as pltpu

f32, bf16 = jnp.float32, jnp.bfloat16
SDS = jax.ShapeDtypeStruct

D = 1024
EPS = 1e-6
CHUNK = 128
GM_HEADS, GM_HD = 4, 256
SSM_GROUPS, SSM_HPG, SSM_P, SSM_N = 4, 4, 64, 128
N_HEADS = SSM_GROUPS * SSM_HPG
CONV_K = 4
CONV_DIM = 2048
POOL_WINDOWS = (2, 4, 8, 16)
POOL_GD = 256
POOL_HALO = 16
CONV_HALO = 8
D_FF = 2816
DT_PAD = 128
IN_DIM = 5136

ADAM_LR, ADAM_B1, ADAM_B2, ADAM_EPS, ADAM_WD, ADAM_STEP = 0.001, 0.9, 0.999, 1e-08, 0.01, 10

NT = (((1,), (1,)), ((), ()))
TN = (((0,), (0,)), ((), ()))
NN = (((1,), (0,)), ((), ()))
HI = lax.Precision.HIGHEST


def _silu(x):
    return x * jax.nn.sigmoid(x)


def _softplus(x):
    return jnp.maximum(x, 0.0) + jnp.log1p(jnp.exp(-jnp.abs(x)))


def _rms(x, g):
    return x * lax.rsqrt(jnp.mean(x * x, axis=-1, keepdims=True) + EPS) * g


def _rms_bwd(x, g, dy):
    r = lax.rsqrt(jnp.mean(x * x, axis=-1, keepdims=True) + EPS)
    xh = x * r
    dxh = dy * g
    dx = r * (dxh - xh * jnp.mean(dxh * xh, axis=-1, keepdims=True))
    return dx, jnp.sum(dy * xh, axis=0, keepdims=True)


def _bdot(a, b, dims=NN):
    return lax.dot_general(a.astype(bf16), b.astype(bf16), dims, preferred_element_type=f32)


def matmul(name, pairs, mode, out_dtype, tm, tn, tk=None):
    a0, b0 = pairs[0]
    if mode == "tn":
        M, N, K = a0.shape[1], b0.shape[1], a0.shape[0]
    else:
        M, K = a0.shape
        N = b0.shape[1] if mode == "nn" else b0.shape[0]
    tm, tn = min(tm, M), min(tn, N)
    assert M % tm == 0 and N % tn == 0, (name, M, N, tm, tn)
    if tk is None:
        nk = 1
    else:
        assert len(pairs) == 1 and K % tk == 0
        nk = K // tk
    dims = {"nn": NN, "nt": NT, "tn": TN}[mode]
    in_specs, args = [], []
    for a, b in pairs:
        kk = (a.shape[0] if mode == "tn" else a.shape[1]) if tk is None else tk
        if mode == "tn":
            in_specs.append(pl.BlockSpec((kk, tm), lambda j, i, k: (k, i)))
            in_specs.append(pl.BlockSpec((kk, tn), lambda j, i, k: (k, j)))
        elif mode == "nn":
            in_specs.append(pl.BlockSpec((tm, kk), lambda j, i, k: (i, k)))
            in_specs.append(pl.BlockSpec((kk, tn), lambda j, i, k: (k, j)))
        else:
            in_specs.append(pl.BlockSpec((tm, kk), lambda j, i, k: (i, k)))
            in_specs.append(pl.BlockSpec((tn, kk), lambda j, i, k: (j, k)))
        args += [a, b]
    npairs = len(pairs)

    def kern(*refs):
        o = refs[2 * npairs]
        part = None
        for p in range(npairs):
            d = _bdot(refs[2 * p][...], refs[2 * p + 1][...], dims)
            part = d if part is None else part + d
        if nk == 1:
            o[...] = part.astype(out_dtype)
        else:
            acc = refs[2 * npairs + 1]
            k = pl.program_id(2)

            @pl.when(k == 0)
            def _():
                acc[...] = part

            @pl.when(k > 0)
            def _():
                acc[...] += part

            @pl.when(k == nk - 1)
            def _():
                o[...] = acc[...].astype(out_dtype)

    return pl.pallas_call(
        kern, name=name, grid=(N // tn, M // tm, nk),
        in_specs=in_specs, out_specs=pl.BlockSpec((tm, tn), lambda j, i, k: (i, j)),
        out_shape=SDS((M, N), out_dtype),
        scratch_shapes=[pltpu.VMEM((tm, tn), f32)] if nk > 1 else [],
        compiler_params=pltpu.CompilerParams(dimension_semantics=("parallel", "parallel", "arbitrary")),
    )(*args)


def mm(name, grid, pairs, dims, o_spec, out_shape):
    nk = grid[2]
    npairs = len(pairs)
    in_specs, args = [], []
    for a, a_spec, b, b_spec in pairs:
        in_specs += [a_spec, b_spec]
        args += [a, b]
    blk = tuple(d for d in o_spec.block_shape if d is not None)

    def kern(*refs):
        o = refs[2 * npairs]
        part = None
        for p in range(npairs):
            d = _bdot(refs[2 * p][...], refs[2 * p + 1][...], dims)
            part = d if part is None else part + d
        if nk == 1:
            o[...] = part.astype(o.dtype)
        else:
            acc = refs[2 * npairs + 1]
            k = pl.program_id(2)

            @pl.when(k == 0)
            def _():
                acc[...] = part

            @pl.when(k > 0)
            def _():
                acc[...] += part

            @pl.when(k == nk - 1)
            def _():
                o[...] = acc[...].astype(o.dtype)

    return pl.pallas_call(
        kern, name=name, grid=grid, in_specs=in_specs, out_specs=o_spec, out_shape=out_shape,
        scratch_shapes=[pltpu.VMEM(blk, f32)] if nk > 1 else [],
        compiler_params=pltpu.CompilerParams(dimension_semantics=("parallel", "parallel", "arbitrary")),
    )(*args)


FF_SH = D_FF // 4


def ffn_up(name, T, tm, n_bf, wg4, wu4, l):
    def kern(n_ref, wg_ref, wu_ref, g_ref, u_ref, a_ref):
        n = n_ref[...]
        g = jnp.dot(n, wg_ref[...], preferred_element_type=f32)
        u = jnp.dot(n, wu_ref[...], preferred_element_type=f32)
        g_ref[...] = g.astype(bf16)
        u_ref[...] = u.astype(bf16)
        a_ref[...] = (_silu(g) * u).astype(bf16)
    w_spec = pl.BlockSpec((None, D, FF_SH), lambda k, i: (k, l, 0))
    o_spec = pl.BlockSpec((None, tm, FF_SH), lambda k, i: (k, i, 0))
    s = SDS((4, T, FF_SH), bf16)
    return pl.pallas_call(kern, name=name, grid=(4, T // tm), in_specs=[pl.BlockSpec((tm, D), lambda k, i: (i, 0)), w_spec, w_spec],
                          out_specs=[o_spec] * 3, out_shape=[s, s, s],
                          compiler_params=pltpu.CompilerParams(dimension_semantics=("parallel", "parallel")))(n_bf, wg4, wu4)


def ffn_dgu(name, T, tm, d_f, wd4, gate4, up4, l):
    def kern(df_ref, wd_ref, g_ref, u_ref, dg_ref, du_ref):
        dact = _bdot(df_ref[...], wd_ref[...], NT)
        _, vjp = jax.vjp(lambda a, b: _silu(a) * b, g_ref[...].astype(f32), u_ref[...].astype(f32))
        dg, du = vjp(dact)
        dg_ref[...] = dg.astype(bf16)
        du_ref[...] = du.astype(bf16)
    a_spec = pl.BlockSpec((None, tm, FF_SH), lambda k, i: (k, i, 0))
    s = SDS((4, T, FF_SH), bf16)
    return pl.pallas_call(kern, name=name, grid=(4, T // tm),
                          in_specs=[pl.BlockSpec((tm, D), lambda k, i: (i, 0)), pl.BlockSpec((None, FF_SH, D), lambda k, i: (k, l, 0)), a_spec, a_spec],
                          out_specs=[a_spec] * 2, out_shape=[s, s],
                          compiler_params=pltpu.CompilerParams(dimension_semantics=("parallel", "parallel")))(d_f, wd4, gate4, up4)


def rowcall(name, body, T, tm, ins, outs, accs=(), scratch=(), reverse=False):
    n = T // tm
    assert T % tm == 0

    def blk(i):
        return (n - 1 - i) if reverse else i

    in_specs, args = [], []
    for spec in ins:
        kind, arr = spec[0], spec[1]
        if kind == "row":
            _, _, w, cb = spec
            in_specs.append(pl.BlockSpec((tm, w), lambda i, cb=cb: (blk(i), cb)))
        elif kind == "prev":
            _, _, w, cb, h = spec
            r = tm // h
            in_specs.append(pl.BlockSpec((h, w), lambda i, cb=cb, r=r: (jnp.maximum(blk(i) * r - 1, 0), cb)))
        elif kind == "next":
            _, _, w, cb, h = spec
            r = tm // h
            in_specs.append(pl.BlockSpec((h, w), lambda i, cb=cb, r=r, h=h: (jnp.minimum((blk(i) + 1) * r, T // h - 1), cb)))
        else:
            nd = arr.ndim
            in_specs.append(pl.BlockSpec(arr.shape, lambda i, nd=nd: (0,) * nd))
        args.append(arr)
    out_shape = [SDS((T, w), dt) for w, dt in outs] + [SDS(tuple(s), f32) for s in accs]
    out_specs = [pl.BlockSpec((tm, w), lambda i: (blk(i), 0)) for w, _ in outs]
    out_specs += [pl.BlockSpec(tuple(s), lambda i, nd=len(s): (0,) * nd) for s in accs]
    ni, no, na = len(ins), len(outs), len(accs)

    def kern(*refs):
        i = pl.program_id(0)
        in_refs, out_refs = refs[:ni], refs[ni:ni + no]
        acc_refs, scr = refs[ni + no:ni + no + na], refs[ni + no + na:]
        if na:
            @pl.when(i == 0)
            def _():
                for a in acc_refs:
                    a[...] = jnp.zeros(a.shape, f32)
        body(blk(i), n, in_refs, out_refs, acc_refs, scr)

    res = pl.pallas_call(
        kern, name=name, grid=(n,), in_specs=in_specs, out_specs=out_specs, out_shape=out_shape,
        scratch_shapes=list(scratch),
        compiler_params=pltpu.CompilerParams(dimension_semantics=("arbitrary",)),
    )(*args)
    return res


def rms_to_bf16(name, T, tm, x, g):
    def body(i, n, ins, outs, accs, scr):
        outs[0][...] = _rms(ins[0][...], ins[1][...]).astype(bf16)
    return rowcall(name, body, T, tm, [("row", x, D, 0), ("const", g)], [(D, bf16)])[0]


def resid_norm(name, T, tm, h_in, f, g_post, g_pre):
    def body(i, n, ins, outs, accs, scr):
        h = ins[0][...] + _rms(ins[1][...], ins[2][...])
        outs[0][...] = h
        if g_pre is not None:
            outs[1][...] = _rms(h, ins[3][...]).astype(bf16)
    ins = [("row", h_in, D, 0), ("row", f, D, 0), ("const", g_post)] + ([("const", g_pre)] if g_pre is not None else [])
    outs = [(D, f32)] + ([(D, bf16)] if g_pre is not None else [])
    return rowcall(name, body, T, tm, ins, outs)


def swiglu_act(name, T, tm, gate, up):
    def body(i, n, ins, outs, accs, scr):
        outs[0][...] = (_silu(ins[0][...]) * ins[1][...]).astype(bf16)
    return rowcall(name, body, T, tm, [("row", gate, D_FF, 0), ("row", up, D_FF, 0)], [(D_FF, bf16)])[0]


def swiglu_bwd(name, T, tm, gate, up, d_act):
    def body(i, n, ins, outs, accs, scr):
        _, vjp = jax.vjp(lambda a, b: _silu(a) * b, ins[0][...], ins[1][...])
        dg, du = vjp(ins[2][...])
        outs[0][...] = dg.astype(bf16)
        outs[1][...] = du.astype(bf16)
    return rowcall(name, body, T, tm, [("row", gate, D_FF, 0), ("row", up, D_FF, 0), ("row", d_act, D_FF, 0)],
                   [(D_FF, bf16), (D_FF, bf16)])


def final_loss_bwd(name, T, tm, h3, f2, tgt, g_post):
    def body(i, n, ins, outs, accs, scr):
        f, g = ins[1][...], ins[3][...]
        e = ins[0][...] + _rms(f, g) - ins[2][...]
        accs[0][...] += jnp.sum(jnp.sum(e * e, axis=-1, keepdims=True) * (0.5 / D), axis=0, keepdims=True)
        dh = e * (1.0 / D)
        df, dg = _rms_bwd(f, g, dh)
        outs[0][...] = dh
        outs[1][...] = df.astype(bf16)
        accs[1][...] += dg
    return rowcall(name, body, T, tm, [("row", h3, D, 0), ("row", f2, D, 0), ("row", tgt, D, 0), ("const", g_post)],
                   [(D, f32), (D, bf16)], accs=[(1, 1), (1, D)])


def bwd_pre_post(name, T, tm, h_out, f, d_res, d_n, g_pre, g_post, df_dtype):
    def body(i, n, ins, outs, accs, scr):
        dx, dgp = _rms_bwd(ins[0][...], ins[4][...], ins[3][...])
        dh = ins[2][...] + dx
        df, dgq = _rms_bwd(ins[1][...], ins[5][...], dh)
        outs[0][...] = dh
        outs[1][...] = df.astype(df_dtype)
        accs[0][...] += dgp
        accs[1][...] += dgq
    return rowcall(name, body, T, tm,
                   [("row", h_out, D, 0), ("row", f, D, 0), ("row", d_res, D, 0), ("row", d_n, D, 0), ("const", g_pre), ("const", g_post)],
                   [(D, f32), (D, df_dtype)], accs=[(1, D), (1, D)])


def bwd_post(name, T, tm, f, d_h, g_post):
    def body(i, n, ins, outs, accs, scr):
        df, dg = _rms_bwd(ins[0][...], ins[2][...], ins[1][...])
        outs[0][...] = df.astype(bf16)
        accs[0][...] += dg
    return rowcall(name, body, T, tm, [("row", f, D, 0), ("row", d_h, D, 0), ("const", g_post)], [(D, bf16)], accs=[(1, D)])


def bwd_pre(name, T, tm, h, d_res, d_n, g_pre):
    def body(i, n, ins, outs, accs, scr):
        dx, dg = _rms_bwd(ins[0][...], ins[3][...], ins[2][...])
        outs[0][...] = ins[1][...] + dx
        accs[0][...] += dg
    return rowcall(name, body, T, tm, [("row", h, D, 0), ("row", d_res, D, 0), ("row", d_n, D, 0), ("const", g_pre)],
                   [(D, f32)], accs=[(1, D)])


def _layer_norm_parts(x):
    mu = jnp.mean(x, axis=-1, keepdims=True)
    xc = x - mu
    r = lax.rsqrt(jnp.mean(xc * xc, axis=-1, keepdims=True) + EPS)
    return xc * r, r


def gmlp_fwd(name, T, tm, uvz, ln_g, ln_b, wm, bs):
    def body(i, n, ins, outs, accs, scr):
        gu = jax.nn.gelu(ins[0][...])
        xh, _ = _layer_norm_parts(jax.nn.gelu(ins[1][...]))
        vln = (xh * ins[2][...] + ins[3][...]).astype(bf16)
        for c in range(tm // CHUNK):
            rows = slice(c * CHUNK, (c + 1) * CHUNK)
            for h in range(GM_HEADS):
                cols = slice(h * GM_HD, (h + 1) * GM_HD)
                mixed = jnp.dot(ins[4][h], vln[rows, cols], preferred_element_type=f32) + ins[5][h]
                outs[0][rows, cols] = (gu[rows, cols] * mixed).astype(bf16)
    return rowcall(name, body, T, tm, [("row", uvz, D, 0), ("row", uvz, D, 1), ("const", ln_g), ("const", ln_b), ("const", wm), ("const", bs)],
                   [(D, bf16)])[0]


def gmlp_bwd(name, T, tm, uvz, d_ya, ln_g, ln_b, wm, bs):
    def body(i, n, ins, outs, accs, scr):
        u, v, dya = ins[0][...], ins[1][...], ins[2][...]
        gu, gelu_u_vjp = jax.vjp(jax.nn.gelu, u)
        gv, gelu_v_vjp = jax.vjp(jax.nn.gelu, v)
        xh, r = _layer_norm_parts(gv)
        lng = ins[3][...]
        vln = (xh * lng + ins[4][...]).astype(bf16)
        rr = lax.broadcasted_iota(jnp.int32, (CHUNK, CHUNK), 0)
        cc = lax.broadcasted_iota(jnp.int32, (CHUNK, CHUNK), 1)
        causal = (rr >= cc).astype(f32)
        dvln_ref = scr[0]
        dgu_ref = scr[1]
        for c in range(tm // CHUNK):
            rows = slice(c * CHUNK, (c + 1) * CHUNK)
            for h in range(GM_HEADS):
                cols = slice(h * GM_HD, (h + 1) * GM_HD)
                w = ins[5][h]
                blk = vln[rows, cols]
                mixed = jnp.dot(w, blk, preferred_element_type=f32) + ins[6][h]
                dy = dya[rows, cols]
                dgu_ref[rows, cols] = dy * mixed
                dm = dy * gu[rows, cols]
                accs[3][h] += jnp.sum(dm, axis=1, keepdims=True)
                accs[2][h] += _bdot(dm, blk, NT) * causal
                dvln_ref[rows, cols] = _bdot(w, dm, TN)
        dvln = dvln_ref[...]
        accs[0][...] += jnp.sum(dvln * xh, axis=0, keepdims=True)
        accs[1][...] += jnp.sum(dvln, axis=0, keepdims=True)
        dxh = dvln * lng
        dgv = r * (dxh - jnp.mean(dxh, axis=-1, keepdims=True) - xh * jnp.mean(dxh * xh, axis=-1, keepdims=True))
        outs[0][...] = gelu_u_vjp(dgu_ref[...])[0].astype(bf16)
        outs[1][...] = gelu_v_vjp(dgv)[0].astype(bf16)
    return rowcall(name, body, T, tm,
                   [("row", uvz, D, 0), ("row", uvz, D, 1), ("row", d_ya, D, 0), ("const", ln_g), ("const", ln_b), ("const", wm), ("const", bs)],
                   [(D, bf16), (D, bf16)], accs=[(1, D), (1, D), (GM_HEADS, CHUNK, CHUNK), (GM_HEADS, CHUNK, 1)],
                   scratch=[pltpu.VMEM((tm, D), f32), pltpu.VMEM((tm, D), f32)])


def _conv_pre(i, x_ref, halo_ref, w_ref, b_ref, scr, tm):
    scr[pl.ds(0, CONV_HALO), :] = jnp.where(i > 0, halo_ref[...], 0.0)
    scr[pl.ds(CONV_HALO, tm), :] = x_ref[...]
    pre = b_ref[...]
    for k in range(CONV_K):
        pre = pre + w_ref[pl.ds(k, 1), :] * scr[pl.ds(CONV_HALO - (CONV_K - 1) + k, tm), :]
    return pre


def conv_fwd(name, T, tm, xbc, conv_w, conv_b):
    def body(i, n, ins, outs, accs, scr):
        outs[0][...] = _silu(_conv_pre(i, ins[0], ins[1], ins[2], ins[3], scr[0], tm))
    return rowcall(name, body, T, tm, [("row", xbc, CONV_DIM, 0), ("prev", xbc, CONV_DIM, 0, CONV_HALO), ("const", conv_w), ("const", conv_b)],
                   [(CONV_DIM, f32)], scratch=[pltpu.VMEM((tm + CONV_HALO, CONV_DIM), f32)])[0]


def conv_bwd_pre(name, T, tm, xbc, d_xc, conv_w, conv_b):
    def body(i, n, ins, outs, accs, scr):
        pre = _conv_pre(i, ins[0], ins[1], ins[3], ins[4], scr[0], tm)
        _, vjp = jax.vjp(_silu, pre)
        dpre = vjp(ins[2][...])[0]
        outs[0][...] = dpre
        accs[1][...] += jnp.sum(dpre, axis=0, keepdims=True)
        for k in range(CONV_K):
            accs[0][pl.ds(k, 1), :] += jnp.sum(dpre * scr[0][pl.ds(CONV_HALO - (CONV_K - 1) + k, tm), :], axis=0, keepdims=True)
    return rowcall(name, body, T, tm,
                   [("row", xbc, CONV_DIM, 0), ("prev", xbc, CONV_DIM, 0, CONV_HALO), ("row", d_xc, CONV_DIM, 0), ("const", conv_w), ("const", conv_b)],
                   [(CONV_DIM, f32)], accs=[(CONV_K, CONV_DIM), (1, CONV_DIM)], scratch=[pltpu.VMEM((tm + CONV_HALO, CONV_DIM), f32)])


def conv_bwd_x(name, T, tm, d_pre, conv_w):
    def body(i, n, ins, outs, accs, scr):
        s = scr[0]
        s[pl.ds(0, tm), :] = ins[0][...]
        s[pl.ds(tm, CONV_HALO), :] = jnp.where(i < n - 1, ins[1][...], 0.0)
        dx = jnp.zeros((tm, CONV_DIM), f32)
        for k in range(CONV_K):
            dx = dx + ins[2][pl.ds(k, 1), :] * s[pl.ds(CONV_K - 1 - k, tm), :]
        outs[0][...] = dx.astype(bf16)
    return rowcall(name, body, T, tm, [("row", d_pre, CONV_DIM, 0), ("next", d_pre, CONV_DIM, 0, CONV_HALO), ("const", conv_w)],
                   [(CONV_DIM, bf16)], scratch=[pltpu.VMEM((tm + CONV_HALO, CONV_DIM), f32)])[0]


def _ssd_chunk(X4, dtr, B4, C4, S4, dtb, alog, dsk):
    L = CHUNK
    rr = lax.broadcasted_iota(jnp.int32, (L, L), 0)
    cc = lax.broadcasted_iota(jnp.int32, (L, L), 1)
    tril = rr >= cc
    lane = lax.broadcasted_iota(jnp.int32, (1, DT_PAD), 1)
    sub = lax.broadcasted_iota(jnp.int32, (DT_PAD, 1), 0)
    glane = lax.broadcasted_iota(jnp.int32, (1, SSM_HPG * SSM_P), 1) // SSM_P
    dt = _softplus(dtr + dtb)
    a = -jnp.exp(alog)
    dA = dt * a
    acum = jnp.dot(tril.astype(f32), dA, precision=HI, preferred_element_type=f32)
    acumT = acum.T
    tot = jnp.sum(dA, axis=0, keepdims=True)
    ys, Sn = [], []
    for g in range(SSM_GROUPS):
        hm = [(glane == r).astype(f32) for r in range(SSM_HPG)]
        cols = [jnp.sum(acum * (lane == SSM_HPG * g + r).astype(f32), axis=1, keepdims=True) for r in range(SSM_HPG)]
        dtc = [jnp.sum(dt * (lane == SSM_HPG * g + r).astype(f32), axis=1, keepdims=True) for r in range(SSM_HPG)]
        tots = [jnp.sum(tot * (lane == SSM_HPG * g + r).astype(f32), axis=1, keepdims=True) for r in range(SSM_HPG)]
        dsc = [jnp.sum(dsk * (lane == SSM_HPG * g + r).astype(f32), axis=1, keepdims=True) for r in range(SSM_HPG)]
        x = X4[g]
        xdt = x * sum(dtc[r] * hm[r] for r in range(SSM_HPG))
        cb = _bdot(C4[g], B4[g], NT)
        y = x * sum(dsc[r] * hm[r] for r in range(SSM_HPG))
        for r in range(SSM_HPG):
            row = jnp.sum(acumT * (sub == SSM_HPG * g + r).astype(f32), axis=0, keepdims=True)
            dec = jnp.exp(jnp.where(tril, cols[r] - row, -jnp.inf))
            y = y + _bdot(cb * dec, xdt * hm[r])
        y = y + _bdot(C4[g], S4[g]) * sum(jnp.exp(cols[r]) * hm[r] for r in range(SSM_HPG))
        dte = sum(jnp.exp(tots[r] - cols[r]) * hm[r] for r in range(SSM_HPG))
        s_new = S4[g] * sum(jnp.exp(tots[r]) * hm[r] for r in range(SSM_HPG)) + _bdot(B4[g], xdt * dte, TN)
        ys.append(y)
        Sn.append(s_new)
    return tuple(ys), tuple(Sn)


def _ssd_ins(xc, dtr):
    gw = SSM_HPG * SSM_P
    ins = [("row", xc, gw, g) for g in range(SSM_GROUPS)]
    ins += [("row", xc, SSM_N, D // SSM_N + g) for g in range(SSM_GROUPS)]
    ins += [("row", xc, SSM_N, D // SSM_N + SSM_GROUPS + g) for g in range(SSM_GROUPS)]
    ins += [("row", dtr, DT_PAD, 0)]
    return ins


def ssd_fwd(name, T, xc, dtr, dtb, alog, dsk):
    gw = SSM_HPG * SSM_P

    def body(i, n, ins, outs, accs, scr):
        S = scr[0]

        @pl.when(i == 0)
        def _():
            S[...] = jnp.zeros(S.shape, f32)
        X4 = tuple(ins[g][...] for g in range(4))
        B4 = tuple(ins[4 + g][...] for g in range(4))
        C4 = tuple(ins[8 + g][...] for g in range(4))
        S4 = tuple(S[:, g * gw:(g + 1) * gw] for g in range(4))
        outs[1][...] = S[...]
        ys, Sn = _ssd_chunk(X4, ins[12][...], B4, C4, S4, ins[13][...], ins[14][...], ins[15][...])
        for g in range(4):
            outs[0][:, g * gw:(g + 1) * gw] = ys[g]
            S[:, g * gw:(g + 1) * gw] = Sn[g]
    ins = _ssd_ins(xc, dtr) + [("const", dtb), ("const", alog), ("const", dsk)]
    return rowcall(name, body, T, CHUNK, ins, [(D, f32), (D, f32)], scratch=[pltpu.VMEM((SSM_N, D), f32)])


def ssd_bwd(name, T, xc, dtr, sprev, d_y, dtb, alog, dsk):
    gw = SSM_HPG * SSM_P

    def body(i, n, ins, outs, accs, scr):
        dS = scr[0]

        @pl.when(i == n - 1)
        def _():
            dS[...] = jnp.zeros(dS.shape, f32)
        X4 = tuple(ins[g][...] for g in range(4))
        B4 = tuple(ins[4 + g][...] for g in range(4))
        C4 = tuple(ins[8 + g][...] for g in range(4))
        S4 = tuple(ins[13 + g][...] for g in range(4))
        dY4 = tuple(ins[17 + g][...] for g in range(4))
        dS4 = tuple(dS[:, g * gw:(g + 1) * gw] for g in range(4))
        _, vjp = jax.vjp(_ssd_chunk, X4, ins[12][...], B4, C4, S4, ins[21][...], ins[22][...], ins[23][...])
        dX4, ddtr, dB4, dC4, dSp, ddtb, dalog, ddsk = vjp((dY4, dS4))
        for g in range(4):
            outs[0][:, g * gw:(g + 1) * gw] = dX4[g]
            outs[0][:, D + g * SSM_N:D + (g + 1) * SSM_N] = dB4[g]
            outs[0][:, D + (SSM_GROUPS + g) * SSM_N:D + (SSM_GROUPS + g + 1) * SSM_N] = dC4[g]
            dS[:, g * gw:(g + 1) * gw] = dSp[g]
        outs[1][...] = ddtr.astype(bf16)
        accs[0][...] += ddtb
        accs[1][...] += dalog
        accs[2][...] += ddsk
    ins = _ssd_ins(xc, dtr) + [("row", sprev, gw, g) for g in range(4)] + [("row", d_y, gw, g) for g in range(4)]
    ins += [("const", dtb), ("const", alog), ("const", dsk)]
    return rowcall(name, body, T, CHUNK, ins, [(CONV_DIM, f32), (DT_PAD, bf16)], accs=[(1, DT_PAD)] * 3,
                   scratch=[pltpu.VMEM((SSM_N, D), f32)], reverse=True)


def _gate_group(y, z, g):
    return _rms(y * _silu(z), g)


def gate_fwd(name, T, tm, y, uvz, gn):
    def body(i, n, ins, outs, accs, scr):
        for g in range(SSM_GROUPS):
            cols = slice(g * 256, (g + 1) * 256)
            outs[0][:, cols] = _gate_group(ins[0][:, cols], ins[1][:, cols], ins[2][:, cols]).astype(bf16)
    return rowcall(name, body, T, tm, [("row", y, D, 0), ("row", uvz, D, 2), ("const", gn)], [(D, bf16)])[0]


def gate_bwd(name, T, tm, y, uvz, d_yb, gn):
    def body(i, n, ins, outs, accs, scr):
        for g in range(SSM_GROUPS):
            cols = slice(g * 256, (g + 1) * 256)
            _, vjp = jax.vjp(_gate_group, ins[0][:, cols], ins[1][:, cols], ins[3][:, cols])
            dy, dz, dg = vjp(ins[2][:, cols])
            outs[0][:, cols] = dy
            outs[1][:, cols] = dz.astype(bf16)
            accs[0][:, cols] += dg
    return rowcall(name, body, T, tm, [("row", y, D, 0), ("row", uvz, D, 2), ("row", d_yb, D, 0), ("const", gn)],
                   [(D, f32), (D, bf16)], accs=[(1, D)])


def _pool_diff(i, tm, h_ref, halo_ref, g_ref, scr):
    g = g_ref[...]
    yn = _rms(h_ref[...], g)
    scr[pl.ds(0, POOL_HALO), :] = jnp.where(i > 0, _rms(halo_ref[...], g), 0.0)
    scr[pl.ds(POOL_HALO, tm), :] = yn
    pos = (i * tm + lax.broadcasted_iota(jnp.int32, (tm, 1), 0) + 1).astype(f32)
    parts = []
    for gi, win in enumerate(POOL_WINDOWS):
        cols = slice(gi * POOL_GD, (gi + 1) * POOL_GD)
        s = scr[pl.ds(POOL_HALO, tm), cols]
        for j in range(1, win):
            s = s + scr[pl.ds(POOL_HALO - j, tm), cols]
        parts.append(s / jnp.minimum(pos, float(win)) - yn[:, cols])
    return parts


def pool_fwd(name, T, tm, h2, g_pre, pw, pb, psc):
    def body(i, n, ins, outs, accs, scr):
        parts = _pool_diff(i, tm, ins[0], ins[1], ins[2], scr[0])
        for gi in range(len(POOL_WINDOWS)):
            cols = slice(gi * POOL_GD, (gi + 1) * POOL_GD)
            o = _bdot(parts[gi], ins[3][gi]) + ins[4][:, cols]
            outs[0][:, cols] = o * ins[5][:, cols]
    return rowcall(name, body, T, tm, [("row", h2, D, 0), ("prev", h2, D, 0, POOL_HALO), ("const", g_pre), ("const", pw), ("const", pb), ("const", psc)],
                   [(D, f32)], scratch=[pltpu.VMEM((tm + POOL_HALO, D), f32)])[0]


def pool_bwd(name, T, tm, h2, d_pm, d_res, g_pre, pw, pb, psc):
    def body(i, n, ins, outs, accs, scr):
        parts = _pool_diff(i, tm, ins[0], ins[1], ins[5], scr[0])
        dpm = ins[2][...]
        psc_v = ins[8][...]
        dps = dpm * psc_v
        dps_halo = jnp.where(i < n - 1, ins[3][...] * psc_v, 0.0)
        accs[1][...] += jnp.sum(dps, axis=0, keepdims=True)
        pos = (i * tm + lax.broadcasted_iota(jnp.int32, (tm, 1), 0) + 1).astype(f32)
        pos_h = ((i + 1) * tm + lax.broadcasted_iota(jnp.int32, (POOL_HALO, 1), 0) + 1).astype(f32)
        r_scr = scr[1]
        dyn_scr = scr[2]
        for gi, win in enumerate(POOL_WINDOWS):
            cols = slice(gi * POOL_GD, (gi + 1) * POOL_GD)
            w = ins[6][gi]
            o = _bdot(parts[gi], w) + ins[7][:, cols]
            accs[2][:, cols] += jnp.sum(dpm[:, cols] * o, axis=0, keepdims=True)
            accs[0][gi] += _bdot(parts[gi], dps[:, cols], TN)
            q = _bdot(dps[:, cols], w, NT)
            qh = _bdot(dps_halo[:, cols], w, NT)
            r_scr[pl.ds(0, tm), cols] = q / jnp.minimum(pos, float(win))
            r_scr[pl.ds(tm, POOL_HALO), cols] = qh / jnp.minimum(pos_h, float(win))
            s = r_scr[pl.ds(0, tm), cols]
            for j in range(1, win):
                s = s + r_scr[pl.ds(j, tm), cols]
            dyn_scr[:, cols] = s - q
        dx, dg = _rms_bwd(ins[0][...], ins[5][...], dyn_scr[...])
        outs[0][...] = ins[4][...] + dx
        accs[3][...] += dg
    ins = [("row", h2, D, 0), ("prev", h2, D, 0, POOL_HALO), ("row", d_pm, D, 0), ("next", d_pm, D, 0, POOL_HALO), ("row", d_res, D, 0),
           ("const", g_pre), ("const", pw), ("const", pb), ("const", psc)]
    return rowcall(name, body, T, tm, ins, [(D, f32)], accs=[(4, POOL_GD, POOL_GD), (1, D), (1, D), (1, D)],
                   scratch=[pltpu.VMEM((tm + POOL_HALO, D), f32), pltpu.VMEM((tm + POOL_HALO, D), f32), pltpu.VMEM((tm, D), f32)])


def local_step(T, x, tgt, W):
    tm = 512 if T >= 1024 else T // 2
    TKW = 1024 if T >= 1024 else T
    ng = W["norm_g"]
    g = lambda l, j: ng[l, j][None, :]
    G = {}

    row_spec = pl.BlockSpec((tm, D), lambda j, i, k: (i, 0))
    sh_spec = pl.BlockSpec((None, tm, FF_SH), lambda j, i, k: (k, i, 0))

    def ffn_fwd(tag, n_bf, l):
        gate4, up4, act4 = ffn_up(f"ffn{tag}_up", T, tm, n_bf, W["wg4"], W["wu4"], l)
        wd_spec = pl.BlockSpec((None, FF_SH, D), lambda j, i, k: (k, l, 0))
        f = mm(f"ffn{tag}_down", (1, T // tm, 4), [(act4, sh_spec, W["wd4"], wd_spec)], NN, row_spec, SDS((T, D), f32))
        return gate4, up4, act4, f

    def ffn_bwd(tag, l, n_bf, gate4, up4, act4, d_f):
        d_gate4, d_up4 = ffn_dgu(f"ffn{tag}_dgu", T, tm, d_f, W["wd4"], gate4, up4, l)
        w_spec = pl.BlockSpec((None, D, FF_SH), lambda j, i, k: (k, l, 0))
        d_n = mm(f"ffn{tag}_dn", (1, T // tm, 4), [(d_gate4, sh_spec, W["wg4"], w_spec), (d_up4, sh_spec, W["wu4"], w_spec)], NT,
                 row_spec, SDS((T, D), f32))

        def wgrad(nm, a4, b):
            return mm(nm, (4, 1, T // TKW),
                      [(a4, pl.BlockSpec((None, TKW, FF_SH), lambda s, j, k: (s, k, 0)), b, pl.BlockSpec((TKW, D), lambda s, j, k: (k, 0)))],
                      TN, pl.BlockSpec((None, FF_SH, D), lambda s, j, k: (s, 0, 0)), SDS((4, FF_SH, D), f32))
        return d_n, wgrad(f"ffn{tag}_dwg", d_gate4, n_bf), wgrad(f"ffn{tag}_dwu", d_up4, n_bf), wgrad(f"ffn{tag}_dwd", act4, d_f)

    y0 = rms_to_bf16("l0_prenorm", T, tm, x, g(0, 0))
    uvz = matmul("in_uvz", [(y0, W["w_uvz"])], "nn", f32, tm, 1024)
    xbc = matmul("in_xbc", [(y0, W["w_xbc"])], "nn", f32, tm, 1024)
    dtr = matmul("in_dt", [(y0, W["w_dt"])], "nn", f32, tm, DT_PAD)
    y_a = gmlp_fwd("gmlp_fwd", T, tm, uvz, W["ln_g"], W["ln_b"], W["wm"], W["bs"])
    xc = conv_fwd("conv_fwd", T, tm, xbc, W["conv_w"], W["conv_b"])
    y_ssd, sprev = ssd_fwd("ssd_fwd", T, xc, dtr, W["dtb"], W["alog"], W["dsk"])
    y_b = gate_fwd("gate_fwd", T, tm, y_ssd, uvz, W["gn"])
    half = D // 2
    wo4 = W["wo4"]
    ycol = [pl.BlockSpec((tm, half), lambda j, i, k, cb=cb: (i, cb)) for cb in range(2)]
    wo_s = [pl.BlockSpec((None, half, D), lambda j, i, k, s=s: (s, 0, 0)) for s in range(4)]
    mixo = mm("out_proj", (1, T // tm, 1), [(y_a, ycol[0], wo4, wo_s[0]), (y_a, ycol[1], wo4, wo_s[1]),
                                            (y_b, ycol[0], wo4, wo_s[2]), (y_b, ycol[1], wo4, wo_s[3])], NN, row_spec, SDS((T, D), f32))
    h1, n1 = resid_norm("l0_mix_resid", T, tm, x, mixo, g(0, 1), g(0, 2))
    gate0, up0, act0, f1 = ffn_fwd("0", n1, 0)
    (h2,) = resid_norm("l0_ffn_resid", T, tm, h1, f1, g(0, 3), None)
    pm = pool_fwd("pool_fwd", T, tm, h2, g(1, 0), W["pool_w"], W["pool_b"], W["pool_scale"])
    h3, n3 = resid_norm("l1_mix_resid", T, tm, h2, pm, g(1, 1), g(1, 2))
    gate1, up1, act1, f2 = ffn_fwd("1", n3, 1)
    dh4, d_f2, loss_acc, dg13 = final_loss_bwd("loss_bwd", T, tm, h3, f2, tgt, g(1, 3))
    d_n3, dwg1, dwu1, dwd1 = ffn_bwd("1", 1, n3, gate1, up1, act1, d_f2)
    d_h3, d_pm, dg12, dg11 = bwd_pre_post("l1_mix_bwd", T, tm, h3, pm, dh4, d_n3, g(1, 2), g(1, 1), f32)
    d_h2, G["pool_w"], G["pool_b"], G["pool_scale"], dg10 = pool_bwd("pool_bwd", T, tm, h2, d_pm, d_h3, g(1, 0), W["pool_w"], W["pool_b"], W["pool_scale"])
    d_f1, dg03 = bwd_post("l0_ffn_bwd", T, tm, f1, d_h2, g(0, 3))
    d_n1, dwg0, dwu0, dwd0 = ffn_bwd("0", 0, n1, gate0, up0, act0, d_f1)
    d_h1, d_mixo, dg02, dg01 = bwd_pre_post("l0_mix_bwd", T, tm, h1, mixo, d_h2, d_n1, g(0, 2), g(0, 1), bf16)
    def d_ycat(nm, s0):
        return mm(nm, (2, T // tm, 1), [(d_mixo, row_spec, wo4, pl.BlockSpec((None, half, D), lambda j, i, k: (s0 + j, 0, 0)))], NT,
                  pl.BlockSpec((tm, half), lambda j, i, k: (i, j)), SDS((T, D), f32))

    def d_wo(nm, y):
        return mm(nm, (2, 1, T // TKW), [(y, pl.BlockSpec((TKW, half), lambda s, j, k: (k, s)), d_mixo, pl.BlockSpec((TKW, D), lambda s, j, k: (k, 0)))],
                  TN, pl.BlockSpec((None, half, D), lambda s, j, k: (s, 0, 0)), SDS((2, half, D), f32))
    d_ya, d_yb = d_ycat("out_proj_dya", 0), d_ycat("out_proj_dyb", 2)
    dwo_a, dwo_b = d_wo("out_proj_dwa", y_a), d_wo("out_proj_dwb", y_b)
    d_yssd, d_z, G["gn"] = gate_bwd("gate_bwd", T, tm, y_ssd, uvz, d_yb, W["gn"])
    d_xc, d_dtr, G["dtb"], G["alog"], G["dsk"] = ssd_bwd("ssd_bwd", T, xc, dtr, sprev, d_yssd, W["dtb"], W["alog"], W["dsk"])
    d_pre, G["conv_w"], G["conv_b"] = conv_bwd_pre("conv_bwd_pre", T, tm, xbc, d_xc, W["conv_w"], W["conv_b"])
    d_xbc = conv_bwd_x("conv_bwd_x", T, tm, d_pre, W["conv_w"])
    d_u, d_v, G["ln_g"], G["ln_b"], G["wm"], G["bs"] = gmlp_bwd("gmlp_bwd", T, tm, uvz, d_ya, W["ln_g"], W["ln_b"], W["wm"], W["bs"])
    w_u, w_v, w_z = W["w_uvz"][:, :D], W["w_uvz"][:, D:2 * D], W["w_uvz"][:, 2 * D:]
    d_y0 = matmul("in_dy0", [(d_u, w_u), (d_v, w_v), (d_z, w_z), (d_xbc, W["w_xbc"]), (d_dtr, W["w_dt"])], "nt", f32, tm, 1024)
    G["w_inT"] = [matmul("in_dwu", [(d_u, y0)], "tn", f32, 1024, 1024, TKW), matmul("in_dwv", [(d_v, y0)], "tn", f32, 1024, 1024, TKW),
                  matmul("in_dwz", [(d_z, y0)], "tn", f32, 1024, 1024, TKW), matmul("in_dwxbc", [(d_xbc, y0)], "tn", f32, 1024, 1024, TKW),
                  matmul("in_dwdt", [(d_dtr, y0)], "tn", f32, DT_PAD, 1024, TKW)[:N_HEADS]]
    grad_x, dg00 = bwd_pre("l0_pre_bwd", T, tm, x, d_h1, d_y0, g(0, 0))
    G["norm_g"] = jnp.stack([jnp.concatenate([dg00, dg01, dg02, dg03], 0), jnp.concatenate([dg10, dg11, dg12, dg13], 0)])
    G["wo4"] = [dwo_a[0], dwo_a[1], dwo_b[0], dwo_b[1]]
    G["wgT4"], G["wuT4"], G["wd4"] = [dwg0, dwg1], [dwu0, dwu1], [dwd0, dwd1]
    return loss_acc, grad_x, G


def build_weights(Wf):
    causal = jnp.tril(jnp.ones((CHUNK, CHUNK), bool))
    w_in = Wf["w_in"].astype(bf16)
    pad16 = lambda v: jnp.pad(v.reshape(1, N_HEADS).astype(f32), ((0, 0), (0, DT_PAD - N_HEADS)))
    return {
        "norm_g": Wf["norm_g"],
        "w_uvz": w_in[:, :3 * D], "w_xbc": w_in[:, 3 * D:3 * D + CONV_DIM],
        "w_dt": jnp.pad(w_in[:, 3 * D + CONV_DIM:], ((0, 0), (0, DT_PAD - N_HEADS))),
        "ln_g": Wf["gm_ln_g"].reshape(1, D), "ln_b": Wf["gm_ln_b"].reshape(1, D),
        "wm": jnp.where(causal[None], Wf["gm_ws"], 0).astype(bf16), "bs": Wf["gm_bs"].reshape(GM_HEADS, CHUNK, 1),
        "conv_w": Wf["conv_w"], "conv_b": Wf["conv_b"].reshape(1, CONV_DIM),
        "dtb": pad16(Wf["dt_bias"]), "alog": pad16(Wf["a_log"]), "dsk": pad16(Wf["d_skip"]),
        "gn": Wf["ssm_norm_g"].reshape(1, D),
        "wo4": Wf["wo4"].astype(bf16), "wg4": Wf["wg4"].astype(bf16), "wu4": Wf["wu4"].astype(bf16), "wd4": Wf["wd4"].astype(bf16),
        "pool_w": Wf["pool_w"].astype(bf16), "pool_b": Wf["pool_b"].reshape(1, D), "pool_scale": Wf["pool_scale"].reshape(1, D),
    }


def small_grads(G):
    return {
        "norm_g": G["norm_g"],
        "gm_ln_g": G["ln_g"].reshape(D), "gm_ln_b": G["ln_b"].reshape(D),
        "gm_ws": G["wm"], "gm_bs": G["bs"].reshape(GM_HEADS, CHUNK),
        "conv_w": G["conv_w"], "conv_b": G["conv_b"].reshape(CONV_DIM),
        "dt_bias": G["dtb"][0, :N_HEADS], "a_log": G["alog"][0, :N_HEADS], "d_skip": G["dsk"][0, :N_HEADS],
        "ssm_norm_g": G["gn"].reshape(D),
        "pool_b": G["pool_b"].reshape(4, POOL_GD), "pool_scale": G["pool_scale"].reshape(D),
    }


MESH_ID = pl.DeviceIdType.MESH
ANY = pl.BlockSpec(memory_space=pl.ANY)


DMA_CHUNK_BYTES = 2 << 20
DMA_MAX_CHUNKS = 32


def _pieces(view, axis, align):
    shape = view.shape
    nbytes = math.prod(shape) * jnp.dtype(view.dtype).itemsize
    n = max(1, min(DMA_MAX_CHUNKS, -(-nbytes // DMA_CHUNK_BYTES)))
    rows = shape[axis]
    size = -(-rows // n)
    size = -(-size // align) * align
    out = []
    for s in range(0, rows, size):
        idx = [slice(None)] * len(shape)
        idx[axis] = pl.ds(s, min(size, rows - s))
        out.append(tuple(idx))
    return out


def comm_call(name, operands, out_shapes, plan):
    n_in = len(operands)
    n_out = len(out_shapes)
    n_remote, n_local = plan((0, 0, 0), [None] * n_in, [None] * n_out, True)

    def body(*refs):
        in_refs, out_refs = refs[:n_in], refs[n_in:n_in + n_out]
        send_sems, recv_sems, local_sems = refs[n_in + n_out:]
        me = (lax.axis_index("x"), lax.axis_index("y"), lax.axis_index("c"))
        remote, local = plan(me, in_refs, out_refs, False)
        align = lambda v: 16 if v.dtype == bf16 else 8
        for j, (s, d, axis) in enumerate(local):
            for ix in _pieces(s, axis, align(s)):
                pltpu.make_async_copy(s.at[ix], d.at[ix], local_sems.at[j]).start()
        peers = [tuple((1 - m) if f else m for m, f in zip(me, flip)) for flip, *_ in remote]
        for k, (flip, src, dst, _, axis) in enumerate(remote):
            for ix in _pieces(src, axis, align(src)):
                pltpu.make_async_remote_copy(src_ref=src.at[ix], dst_ref=dst.at[ix], send_sem=send_sems.at[k], recv_sem=recv_sems.at[k],
                                             device_id=peers[k], device_id_type=MESH_ID).start()
        for k, (flip, src, dst, landing, axis) in enumerate(remote):
            pltpu.make_async_remote_copy(src_ref=landing, dst_ref=landing, send_sem=send_sems.at[k], recv_sem=recv_sems.at[k],
                                         device_id=peers[k], device_id_type=MESH_ID).wait_recv()
        for k, (flip, src, dst, landing, axis) in enumerate(remote):
            pltpu.make_async_remote_copy(src_ref=src, dst_ref=dst, send_sem=send_sems.at[k], recv_sem=recv_sems.at[k],
                                         device_id=peers[k], device_id_type=MESH_ID).wait_send()
        for j, (s, d, axis) in enumerate(local):
            pltpu.make_async_copy(s, d, local_sems.at[j]).wait()

    return pl.pallas_call(
        body, name=name, out_shape=list(out_shapes), in_specs=[ANY] * n_in, out_specs=[ANY] * n_out,
        scratch_shapes=[pltpu.SemaphoreType.DMA((n_remote,)), pltpu.SemaphoreType.DMA((n_remote,)), pltpu.SemaphoreType.DMA((max(n_local, 1),))],
    )(*operands)


CHIP_FLIPS = ((1, 0, 0), (0, 1, 0), (1, 1, 0))
PAIR_FLIP = (0, 0, 1)


def gather_over_chips(name, arrs):
    def plan(me, ins, outs, count):
        if count:
            return len(CHIP_FLIPS) * len(arrs), len(arrs)
        k = 2 * me[0] + me[1]
        remote, local = [], []
        for a in range(len(arrs)):
            for flip in CHIP_FLIPS:
                kp = 2 * ((1 - me[0]) if flip[0] else me[0]) + ((1 - me[1]) if flip[1] else me[1])
                remote.append((flip, ins[a], outs[a].at[k], outs[a].at[kp], 0))
            local.append((ins[a], outs[a].at[k], 0))
        return remote, local
    return comm_call(name, arrs, [SDS((4,) + a.shape, a.dtype) for a in arrs], plan)


def pair_split_exchange(name, p, rh):
    def plan(me, ins, outs, count):
        if count:
            return 1, 1
        c = me[2]
        mine = ins[0].at[:, pl.ds(pl.multiple_of(c * rh, 8), rh), :]
        theirs = ins[0].at[:, pl.ds(pl.multiple_of((1 - c) * rh, 8), rh), :]
        return [(PAIR_FLIP, theirs, outs[1], outs[1], 1)], [(mine, outs[0], 1)]
    s = SDS((4, rh, p.shape[2]), p.dtype)
    return comm_call(name, [p], [s, s], plan)


def scatter_over_chips(name, cs):
    def plan(me, ins, outs, count):
        if count:
            return len(CHIP_FLIPS), 1
        k = 2 * me[0] + me[1]
        remote = []
        for flip in CHIP_FLIPS:
            kp = 2 * ((1 - me[0]) if flip[0] else me[0]) + ((1 - me[1]) if flip[1] else me[1])
            remote.append((flip, ins[0].at[kp], outs[0].at[k], outs[0].at[kp], 0))
        return remote, [(ins[0].at[k], outs[0].at[k], 0)]
    return comm_call(name, [cs], [SDS(cs.shape, cs.dtype)], plan)[0]


def pair_gather(name, half):
    def plan(me, ins, outs, count):
        if count:
            return 1, 1
        c = me[2]
        return [(PAIR_FLIP, ins[0], outs[0].at[c], outs[0].at[1 - c], 0)], [(ins[0], outs[0].at[c], 0)]
    return comm_call(name, [half], [SDS((2,) + half.shape, half.dtype)], plan)[0]


def _row_tile(rows, cap=512):
    if rows <= cap:
        return rows
    t = cap - cap % 8
    while rows % t:
        t -= 8
    return t


def add_rows(name, terms):
    R, C = terms[0].shape
    tr = _row_tile(R)

    def kern(*refs):
        s = refs[0][...]
        for r in refs[1:len(terms)]:
            s = s + r[...]
        refs[len(terms)][...] = s
    spec = pl.BlockSpec((tr, C), lambda i: (i, 0))
    return pl.pallas_call(kern, name=name, grid=(R // tr,), in_specs=[spec] * len(terms), out_specs=spec, out_shape=SDS((R, C), f32),
                          compiler_params=pltpu.CompilerParams(dimension_semantics=("parallel",)))(*terms)


def adamw(name, w, g, m, v):
    R, C = w.shape
    tr = _row_tile(R, 256)

    def kern(w_ref, g_ref, m_ref, v_ref, d_ref, mo_ref, vo_ref):
        gg = g_ref[...]
        mn = ADAM_B1 * m_ref[...] + (1.0 - ADAM_B1) * gg
        vn = ADAM_B2 * v_ref[...] + (1.0 - ADAM_B2) * jnp.square(gg)
        m_hat = mn / (1.0 - ADAM_B1 ** ADAM_STEP)
        v_hat = vn / (1.0 - ADAM_B2 ** ADAM_STEP)
        d_ref[...] = -ADAM_LR * (m_hat / (jnp.sqrt(v_hat) + ADAM_EPS) + ADAM_WD * w_ref[...])
        mo_ref[...] = mn
        vo_ref[...] = vn
    spec = pl.BlockSpec((tr, C), lambda i: (i, 0))
    s = SDS((R, C), f32)
    return pl.pallas_call(kern, name=name, grid=(R // tr,), in_specs=[spec] * 4, out_specs=[spec] * 3, out_shape=[s, s, s],
                          compiler_params=pltpu.CompilerParams(dimension_semantics=("parallel",)))(w, g, m, v)


WEIGHT_NAMES = ("norm_g", "w_in", "gm_ln_g", "gm_ln_b", "gm_ws", "gm_bs", "conv_w", "conv_b", "dt_bias", "a_log", "d_skip",
                "ssm_norm_g", "w_out", "pool_w", "pool_b", "pool_scale", "ffn_w_gate", "ffn_w_up", "ffn_w_down")
SMALL = ("norm_g", "conv_w", "pool_b", "pool_scale")
REPL = ("gm_ln_g", "gm_ln_b", "gm_ws", "gm_bs", "conv_b", "dt_bias", "a_log", "d_skip", "ssm_norm_g")
SMALL_AXIS = {"norm_g": 2, "conv_w": 1, "pool_b": 1, "pool_scale": 0}
N_CHIPS = 4
IN_SH = IN_DIM // N_CHIPS
SMALL_ROWS = 8
REPL_ROWS = 72
OFF_OUT, OFF_GATE, OFF_UP, OFF_DOWN = 0, 512, 512 + 2 * FF_SH, 512 + 4 * FF_SH
OFF_POOL = OFF_DOWN + 2 * FF_SH
OFF_SMALL = OFF_POOL + 64
OFF_REPL = OFF_SMALL + SMALL_ROWS
OFF_IN = OFF_REPL + REPL_ROWS
SLOT_END = OFF_IN + IN_SH
SLOT_ROWS = 6272
HALF_ROWS = SLOT_ROWS // 2


def _flat_rows(pieces, rows):
    v = jnp.concatenate([p.reshape(-1) for p in pieces])
    return jnp.pad(v, (0, rows * D - v.shape[0])).reshape(rows, D)


def _shard_small(name, full, k):
    ax = SMALL_AXIS[name]
    n = full.shape[ax] // N_CHIPS
    return lax.slice_in_dim(full, k * n, (k + 1) * n, axis=ax)


def _drop1(name, a):
    return a if name == "norm_g" else a[0]


def _row_range(blocks, lo, hi):
    out, off = [], 0
    for b in blocks:
        n = b.shape[0]
        a, e = max(lo, off), min(hi, off + n)
        if a < e:
            out.append(b[a - off:e - off])
        off += n
    return out


def gather_weights(w_sh):
    big = [w_sh["w_in"][0], w_sh["w_out"][0], w_sh["pool_w"][0].reshape(4 * 64, POOL_GD), w_sh["ffn_w_gate"].reshape(2 * D, FF_SH),
           w_sh["ffn_w_up"].reshape(2 * D, FF_SH), w_sh["ffn_w_down"].reshape(2 * FF_SH, D)]
    small_pack = _flat_rows([w_sh[n] for n in SMALL], SMALL_ROWS)
    s_in, s_out, s_pool, s_gate, s_up, s_down, s_small = gather_over_chips("gather_weights", [b.astype(bf16) for b in big] + [small_pack])
    Wf = {n: w_sh[n][0] for n in REPL}
    Wf["w_in"] = s_in.transpose(1, 0, 2).reshape(D, IN_DIM)
    Wf["pool_w"] = s_pool.reshape(N_CHIPS, 4, 64, POOL_GD).transpose(1, 0, 2, 3).reshape(4, POOL_GD, POOL_GD)
    Wf["wo4"], Wf["wg4"], Wf["wu4"], Wf["wd4"] = s_out, s_gate, s_up, s_down
    small_shapes = [_drop1(n, w_sh[n]).shape for n in SMALL]
    parts = [_split_rows(s_small[k], small_shapes) for k in range(N_CHIPS)]
    for j, n in enumerate(SMALL):
        Wf[n] = jnp.concatenate([parts[k][j] for k in range(N_CHIPS)], axis=SMALL_AXIS[n])
    return Wf


def pack_grads(G):
    sg = small_grads(G)
    repl = _flat_rows([sg[n] for n in REPL], REPL_ROWS)
    slots = []
    for k in range(N_CHIPS):
        rows = [G["wo4"][k], G["wgT4"][0][k], G["wgT4"][1][k], G["wuT4"][0][k], G["wuT4"][1][k], G["wd4"][0][k], G["wd4"][1][k],
                G["pool_w"][:, k * 64:(k + 1) * 64, :].reshape(64, D), _flat_rows([_shard_small(n, sg[n], k) for n in SMALL], SMALL_ROWS), repl]
        rows += _row_range(G["w_inT"], k * IN_SH, (k + 1) * IN_SH)
        rows.append(jnp.zeros((SLOT_ROWS - SLOT_END, D), f32))
        slots.append(jnp.concatenate(rows, axis=0))
    return jnp.stack(slots)


def unpack_grads(total, w_sh):
    g = {"w_out": total[OFF_OUT:OFF_GATE], "ffn_w_down": total[OFF_DOWN:OFF_POOL], "pool_w": total[OFF_POOL:OFF_SMALL],
         "ffn_w_gate": jnp.stack([total[OFF_GATE + l * FF_SH:OFF_GATE + (l + 1) * FF_SH].T for l in range(2)]),
         "ffn_w_up": jnp.stack([total[OFF_UP + l * FF_SH:OFF_UP + (l + 1) * FF_SH].T for l in range(2)]),
         "w_in": total[OFF_IN:SLOT_END].T}
    small = _split_rows(total[OFF_SMALL:OFF_REPL], [_drop1(n, w_sh[n]).shape for n in SMALL])
    repl = _split_rows(total[OFF_REPL:OFF_IN], [w_sh[n][0].shape for n in REPL])
    g.update(zip(SMALL, small))
    g.update(zip(REPL, repl))
    return {n: g[n].reshape(w_sh[n].shape) for n in WEIGHT_NAMES}


def _split_rows(flat2d, shapes):
    v = flat2d.reshape(-1)
    out, off = [], 0
    for s in shapes:
        n = math.prod(s)
        out.append(v[off:off + n].reshape(s))
        off += n
    return out


def kernel(x, norm_g, w_in, gm_ln_g, gm_ln_b, gm_ws, gm_bs, conv_w, conv_b, dt_bias, a_log, d_skip, ssm_norm_g, w_out, pool_w, pool_b, pool_scale, ffn_w_gate, ffn_w_up, ffn_w_down, loss_target, m_norm_g, m_w_in, m_gm_ln_g, m_gm_ln_b, m_gm_ws, m_gm_bs, m_conv_w, m_conv_b, m_dt_bias, m_a_log, m_d_skip, m_ssm_norm_g, m_w_out, m_pool_w, m_pool_b, m_pool_scale, m_ffn_w_gate, m_ffn_w_up, m_ffn_w_down, v_norm_g, v_w_in, v_gm_ln_g, v_gm_ln_b, v_gm_ws, v_gm_bs, v_conv_w, v_conv_b, v_dt_bias, v_a_log, v_d_skip, v_ssm_norm_g, v_w_out, v_pool_w, v_pool_b, v_pool_scale, v_ffn_w_gate, v_ffn_w_up, v_ffn_w_down):
    T = x.shape[1]
    w_sh = dict(zip(WEIGHT_NAMES, (norm_g, w_in, gm_ln_g, gm_ln_b, gm_ws, gm_bs, conv_w, conv_b, dt_bias, a_log, d_skip, ssm_norm_g, w_out,
                                   pool_w, pool_b, pool_scale, ffn_w_gate, ffn_w_up, ffn_w_down)))
    m_sh = dict(zip(WEIGHT_NAMES, (m_norm_g, m_w_in, m_gm_ln_g, m_gm_ln_b, m_gm_ws, m_gm_bs, m_conv_w, m_conv_b, m_dt_bias, m_a_log, m_d_skip,
                                   m_ssm_norm_g, m_w_out, m_pool_w, m_pool_b, m_pool_scale, m_ffn_w_gate, m_ffn_w_up, m_ffn_w_down)))
    v_sh = dict(zip(WEIGHT_NAMES, (v_norm_g, v_w_in, v_gm_ln_g, v_gm_ln_b, v_gm_ws, v_gm_bs, v_conv_w, v_conv_b, v_dt_bias, v_a_log, v_d_skip,
                                   v_ssm_norm_g, v_w_out, v_pool_w, v_pool_b, v_pool_scale, v_ffn_w_gate, v_ffn_w_up, v_ffn_w_down)))

    W = build_weights(gather_weights(w_sh))

    loss_acc, grad_x, G = local_step(T, x[0], loss_target[0], W)

    own, got = pair_split_exchange("grads_pair_split", pack_grads(G), HALF_ROWS)
    chip_sum = add_rows("grads_pair_sum", [own.reshape(-1, D), got.reshape(-1, D)]).reshape(N_CHIPS, HALF_ROWS, D)
    landed = scatter_over_chips("grads_scatter", chip_sum)
    half = add_rows("grads_chip_sum", [landed[k] for k in range(N_CHIPS)])
    total = pair_gather("grads_pair_gather", half).reshape(SLOT_ROWS, D)
    grads = unpack_grads(total, w_sh)

    delta, new_m, new_v = {}, {}, {}
    for n in WEIGHT_NAMES:
        shp = w_sh[n].shape
        two_d = (-1, shp[-1])
        d_, m_, v_ = adamw("adamw_" + n, w_sh[n].reshape(two_d), grads[n].reshape(two_d), m_sh[n].reshape(two_d), v_sh[n].reshape(two_d))
        delta[n], new_m[n], new_v[n] = d_.reshape(shp), m_.reshape(shp), v_.reshape(shp)

    loss = lax.psum(loss_acc[0, 0], ("x", "y", "c"))
    return (loss, grad_x[None], *[grads[n] for n in WEIGHT_NAMES], *[delta[n] for n in WEIGHT_NAMES],
            *[new_m[n] for n in WEIGHT_NAMES], *[new_v[n] for n in WEIGHT_NAMES])
```

```python
import functools
import math

import jax
import jax.numpy as jnp
from jax import lax
from jax.experimental import pallas as pl
from jax.experimental.pallas import tpu as pltpu

f32, bf16 = jnp.float32, jnp.bfloat16
SDS = jax.ShapeDtypeStruct

D = 1024
EPS = 1e-6
CHUNK = 128
GM_HEADS, GM_HD = 4, 256
SSM_GROUPS, SSM_HPG, SSM_P, SSM_N = 4, 4, 64, 128
N_HEADS = SSM_GROUPS * SSM_HPG
CONV_K = 4
CONV_DIM = 2048
POOL_WINDOWS = (2, 4, 8, 16)
POOL_GD = 256
POOL_HALO = 16
CONV_HALO = 8
D_FF = 2816
DT_PAD = 128
IN_DIM = 5136

ADAM_LR, ADAM_B1, ADAM_B2, ADAM_EPS, ADAM_WD, ADAM_STEP = 0.001, 0.9, 0.999, 1e-08, 0.01, 10

NT = (((1,), (1,)), ((), ()))
TN = (((0,), (0,)), ((), ()))
NN = (((1,), (0,)), ((), ()))
HI = lax.Precision.HIGHEST


def _silu(x):
    return x * jax.nn.sigmoid(x)


def _softplus(x):
    return jnp.maximum(x, 0.0) + jnp.log1p(jnp.exp(-jnp.abs(x)))


def _rms(x, g):
    return x * lax.rsqrt(jnp.mean(x * x, axis=-1, keepdims=True) + EPS) * g


def _rms_bwd(x, g, dy):
    r = lax.rsqrt(jnp.mean(x * x, axis=-1, keepdims=True) + EPS)
    xh = x * r
    dxh = dy * g
    dx = r * (dxh - xh * jnp.mean(dxh * xh, axis=-1, keepdims=True))
    return dx, jnp.sum(dy * xh, axis=0, keepdims=True)


def _bdot(a, b, dims=NN):
    return lax.dot_general(a.astype(bf16), b.astype(bf16), dims, preferred_element_type=f32)


def matmul(name, pairs, mode, out_dtype, tm, tn, tk=None):
    a0, b0 = pairs[0]
    if mode == "tn":
        M, N, K = a0.shape[1], b0.shape[1], a0.shape[0]
    else:
        M, K = a0.shape
        N = b0.shape[1] if mode == "nn" else b0.shape[0]
    tm, tn = min(tm, M), min(tn, N)
    assert M % tm == 0 and N % tn == 0, (name, M, N, tm, tn)
    if tk is None:
        nk = 1
    else:
        assert len(pairs) == 1 and K % tk == 0
        nk = K // tk
    dims = {"nn": NN, "nt": NT, "tn": TN}[mode]
    in_specs, args = [], []
    for a, b in pairs:
        kk = (a.shape[0] if mode == "tn" else a.shape[1]) if tk is None else tk
        if mode == "tn":
            in_specs.append(pl.BlockSpec((kk, tm), lambda j, i, k: (k, i)))
            in_specs.append(pl.BlockSpec((kk, tn), lambda j, i, k: (k, j)))
        elif mode == "nn":
            in_specs.append(pl.BlockSpec((tm, kk), lambda j, i, k: (i, k)))
            in_specs.append(pl.BlockSpec((kk, tn), lambda j, i, k: (k, j)))
        else:
            in_specs.append(pl.BlockSpec((tm, kk), lambda j, i, k: (i, k)))
            in_specs.append(pl.BlockSpec((tn, kk), lambda j, i, k: (j, k)))
        args += [a, b]
    npairs = len(pairs)

    def kern(*refs):
        o = refs[2 * npairs]
        part = None
        for p in range(npairs):
            d = _bdot(refs[2 * p][...], refs[2 * p + 1][...], dims)
            part = d if part is None else part + d
        if nk == 1:
            o[...] = part.astype(out_dtype)
        else:
            acc = refs[2 * npairs + 1]
            k = pl.program_id(2)

            @pl.when(k == 0)
            def _():
                acc[...] = part

            @pl.when(k > 0)
            def _():
                acc[...] += part

            @pl.when(k == nk - 1)
            def _():
                o[...] = acc[...].astype(out_dtype)

    return pl.pallas_call(
        kern, name=name, grid=(N // tn, M // tm, nk),
        in_specs=in_specs, out_specs=pl.BlockSpec((tm, tn), lambda j, i, k: (i, j)),
        out_shape=SDS((M, N), out_dtype),
        scratch_shapes=[pltpu.VMEM((tm, tn), f32)] if nk > 1 else [],
        compiler_params=pltpu.CompilerParams(dimension_semantics=("parallel", "parallel", "arbitrary")),
    )(*args)


def mm(name, grid, pairs, dims, o_spec, out_shape):
    nk = grid[2]
    npairs = len(pairs)
    in_specs, args = [], []
    for a, a_spec, b, b_spec in pairs:
        in_specs += [a_spec, b_spec]
        args += [a, b]
    blk = tuple(d for d in o_spec.block_shape if d is not None)

    def kern(*refs):
        o = refs[2 * npairs]
        part = None
        for p in range(npairs):
            d = _bdot(refs[2 * p][...], refs[2 * p + 1][...], dims)
            part = d if part is None else part + d
        if nk == 1:
            o[...] = part.astype(o.dtype)
        else:
            acc = refs[2 * npairs + 1]
            k = pl.program_id(2)

            @pl.when(k == 0)
            def _():
                acc[...] = part

            @pl.when(k > 0)
            def _():
                acc[...] += part

            @pl.when(k == nk - 1)
            def _():
                o[...] = acc[...].astype(o.dtype)

    return pl.pallas_call(
        kern, name=name, grid=grid, in_specs=in_specs, out_specs=o_spec, out_shape=out_shape,
        scratch_shapes=[pltpu.VMEM(blk, f32)] if nk > 1 else [],
        compiler_params=pltpu.CompilerParams(dimension_semantics=("parallel", "parallel", "arbitrary")),
    )(*args)


FF_SH = D_FF // 4


def ffn_up(name, T, tm, n_bf, wg4, wu4, l):
    def kern(n_ref, wg_ref, wu_ref, g_ref, u_ref, a_ref):
        n = n_ref[...]
        g = jnp.dot(n, wg_ref[...], preferred_element_type=f32)
        u = jnp.dot(n, wu_ref[...], preferred_element_type=f32)
        g_ref[...] = g.astype(bf16)
        u_ref[...] = u.astype(bf16)
        a_ref[...] = (_silu(g) * u).astype(bf16)
    w_spec = pl.BlockSpec((None, D, FF_SH), lambda k, i: (k, l, 0))
    o_spec = pl.BlockSpec((None, tm, FF_SH), lambda k, i: (k, i, 0))
    s = SDS((4, T, FF_SH), bf16)
    return pl.pallas_call(kern, name=name, grid=(4, T // tm), in_specs=[pl.BlockSpec((tm, D), lambda k, i: (i, 0)), w_spec, w_spec],
                          out_specs=[o_spec] * 3, out_shape=[s, s, s],
                          compiler_params=pltpu.CompilerParams(dimension_semantics=("parallel", "parallel")))(n_bf, wg4, wu4)


def ffn_dgu(name, T, tm, d_f, wd4, gate4, up4, l):
    def kern(df_ref, wd_ref, g_ref, u_ref, dg_ref, du_ref):
        dact = _bdot(df_ref[...], wd_ref[...], NT)
        _, vjp = jax.vjp(lambda a, b: _silu(a) * b, g_ref[...].astype(f32), u_ref[...].astype(f32))
        dg, du = vjp(dact)
        dg_ref[...] = dg.astype(bf16)
        du_ref[...] = du.astype(bf16)
    a_spec = pl.BlockSpec((None, tm, FF_SH), lambda k, i: (k, i, 0))
    s = SDS((4, T, FF_SH), bf16)
    return pl.pallas_call(kern, name=name, grid=(4, T // tm),
                          in_specs=[pl.BlockSpec((tm, D), lambda k, i: (i, 0)), pl.BlockSpec((None, FF_SH, D), lambda k, i: (k, l, 0)), a_spec, a_spec],
                          out_specs=[a_spec] * 2, out_shape=[s, s],
                          compiler_params=pltpu.CompilerParams(dimension_semantics=("parallel", "parallel")))(d_f, wd4, gate4, up4)


def rowcall(name, body, T, tm, ins, outs, accs=(), scratch=(), reverse=False):
    n = T // tm
    assert T % tm == 0

    def blk(i):
        return (n - 1 - i) if reverse else i

    in_specs, args = [], []
    for spec in ins:
        kind, arr = spec[0], spec[1]
        if kind == "row":
            _, _, w, cb = spec
            in_specs.append(pl.BlockSpec((tm, w), lambda i, cb=cb: (blk(i), cb)))
        elif kind == "prev":
            _, _, w, cb, h = spec
            r = tm // h
            in_specs.append(pl.BlockSpec((h, w), lambda i, cb=cb, r=r: (jnp.maximum(blk(i) * r - 1, 0), cb)))
        elif kind == "next":
            _, _, w, cb, h = spec
            r = tm // h
            in_specs.append(pl.BlockSpec((h, w), lambda i, cb=cb, r=r, h=h: (jnp.minimum((blk(i) + 1) * r, T // h - 1), cb)))
        else:
            nd = arr.ndim
            in_specs.append(pl.BlockSpec(arr.shape, lambda i, nd=nd: (0,) * nd))
        args.append(arr)
    out_shape = [SDS((T, w), dt) for w, dt in outs] + [SDS(tuple(s), f32) for s in accs]
    out_specs = [pl.BlockSpec((tm, w), lambda i: (blk(i), 0)) for w, _ in outs]
    out_specs += [pl.BlockSpec(tuple(s), lambda i, nd=len(s): (0,) * nd) for s in accs]
    ni, no, na = len(ins), len(outs), len(accs)

    def kern(*refs):
        i = pl.program_id(0)
        in_refs, out_refs = refs[:ni], refs[ni:ni + no]
        acc_refs, scr = refs[ni + no:ni + no + na], refs[ni + no + na:]
        if na:
            @pl.when(i == 0)
            def _():
                for a in acc_refs:
                    a[...] = jnp.zeros(a.shape, f32)
        body(blk(i), n, in_refs, out_refs, acc_refs, scr)

    res = pl.pallas_call(
        kern, name=name, grid=(n,), in_specs=in_specs, out_specs=out_specs, out_shape=out_shape,
        scratch_shapes=list(scratch),
        compiler_params=pltpu.CompilerParams(dimension_semantics=("arbitrary",)),
    )(*args)
    return res


def rms_to_bf16(name, T, tm, x, g):
    def body(i, n, ins, outs, accs, scr):
        outs[0][...] = _rms(ins[0][...], ins[1][...]).astype(bf16)
    return rowcall(name, body, T, tm, [("row", x, D, 0), ("const", g)], [(D, bf16)])[0]


def resid_norm(name, T, tm, h_in, f, g_post, g_pre):
    def body(i, n, ins, outs, accs, scr):
        h = ins[0][...] + _rms(ins[1][...], ins[2][...])
        outs[0][...] = h
        if g_pre is not None:
            outs[1][...] = _rms(h, ins[3][...]).astype(bf16)
    ins = [("row", h_in, D, 0), ("row", f, D, 0), ("const", g_post)] + ([("const", g_pre)] if g_pre is not None else [])
    outs = [(D, f32)] + ([(D, bf16)] if g_pre is not None else [])
    return rowcall(name, body, T, tm, ins, outs)


def swiglu_act(name, T, tm, gate, up):
    def body(i, n, ins, outs, accs, scr):
        outs[0][...] = (_silu(ins[0][...]) * ins[1][...]).astype(bf16)
    return rowcall(name, body, T, tm, [("row", gate, D_FF, 0), ("row", up, D_FF, 0)], [(D_FF, bf16)])[0]


def swiglu_bwd(name, T, tm, gate, up, d_act):
    def body(i, n, ins, outs, accs, scr):
        _, vjp = jax.vjp(lambda a, b: _silu(a) * b, ins[0][...], ins[1][...])
        dg, du = vjp(ins[2][...])
        outs[0][...] = dg.astype(bf16)
        outs[1][...] = du.astype(bf16)
    return rowcall(name, body, T, tm, [("row", gate, D_FF, 0), ("row", up, D_FF, 0), ("row", d_act, D_FF, 0)],
                   [(D_FF, bf16), (D_FF, bf16)])


def final_loss_bwd(name, T, tm, h3, f2, tgt, g_post):
    def body(i, n, ins, outs, accs, scr):
        f, g = ins[1][...], ins[3][...]
        e = ins[0][...] + _rms(f, g) - ins[2][...]
        accs[0][...] += jnp.sum(jnp.sum(e * e, axis=-1, keepdims=True) * (0.5 / D), axis=0, keepdims=True)
        dh = e * (1.0 / D)
        df, dg = _rms_bwd(f, g, dh)
        outs[0][...] = dh
        outs[1][...] = df.astype(bf16)
        accs[1][...] += dg
    return rowcall(name, body, T, tm, [("row", h3, D, 0), ("row", f2, D, 0), ("row", tgt, D, 0), ("const", g_post)],
                   [(D, f32), (D, bf16)], accs=[(1, 1), (1, D)])


def bwd_pre_post(name, T, tm, h_out, f, d_res, d_n, g_pre, g_post, df_dtype):
    def body(i, n, ins, outs, accs, scr):
        dx, dgp = _rms_bwd(ins[0][...], ins[4][...], ins[3][...])
        dh = ins[2][...] + dx
        df, dgq = _rms_bwd(ins[1][...], ins[5][...], dh)
        outs[0][...] = dh
        outs[1][...] = df.astype(df_dtype)
        accs[0][...] += dgp
        accs[1][...] += dgq
    return rowcall(name, body, T, tm,
                   [("row", h_out, D, 0), ("row", f, D, 0), ("row", d_res, D, 0), ("row", d_n, D, 0), ("const", g_pre), ("const", g_post)],
                   [(D, f32), (D, df_dtype)], accs=[(1, D), (1, D)])


def bwd_post(name, T, tm, f, d_h, g_post):
    def body(i, n, ins, outs, accs, scr):
        df, dg = _rms_bwd(ins[0][...], ins[2][...], ins[1][...])
        outs[0][...] = df.astype(bf16)
        accs[0][...] += dg
    return rowcall(name, body, T, tm, [("row", f, D, 0), ("row", d_h, D, 0), ("const", g_post)], [(D, bf16)], accs=[(1, D)])


def bwd_pre(name, T, tm, h, d_res, d_n, g_pre):
    def body(i, n, ins, outs, accs, scr):
        dx, dg = _rms_bwd(ins[0][...], ins[3][...], ins[2][...])
        outs[0][...] = ins[1][...] + dx
        accs[0][...] += dg
    return rowcall(name, body, T, tm, [("row", h, D, 0), ("row", d_res, D, 0), ("row", d_n, D, 0), ("const", g_pre)],
                   [(D, f32)], accs=[(1, D)])


def _layer_norm_parts(x):
    mu = jnp.mean(x, axis=-1, keepdims=True)
    xc = x - mu
    r = lax.rsqrt(jnp.mean(xc * xc, axis=-1, keepdims=True) + EPS)
    return xc * r, r


def gmlp_fwd(name, T, tm, uvz, ln_g, ln_b, wm, bs):
    def body(i, n, ins, outs, accs, scr):
        gu = jax.nn.gelu(ins[0][...])
        xh, _ = _layer_norm_parts(jax.nn.gelu(ins[1][...]))
        vln = (xh * ins[2][...] + ins[3][...]).astype(bf16)
        for c in range(tm // CHUNK):
            rows = slice(c * CHUNK, (c + 1) * CHUNK)
            for h in range(GM_HEADS):
                cols = slice(h * GM_HD, (h + 1) * GM_HD)
                mixed = jnp.dot(ins[4][h], vln[rows, cols], preferred_element_type=f32) + ins[5][h]
                outs[0][rows, cols] = (gu[rows, cols] * mixed).astype(bf16)
    return rowcall(name, body, T, tm, [("row", uvz, D, 0), ("row", uvz, D, 1), ("const", ln_g), ("const", ln_b), ("const", wm), ("const", bs)],
                   [(D, bf16)])[0]


def gmlp_bwd(name, T, tm, uvz, d_ya, ln_g, ln_b, wm, bs):
    def body(i, n, ins, outs, accs, scr):
        u, v, dya = ins[0][...], ins[1][...], ins[2][...]
        gu, gelu_u_vjp = jax.vjp(jax.nn.gelu, u)
        gv, gelu_v_vjp = jax.vjp(jax.nn.gelu, v)
        xh, r = _layer_norm_parts(gv)
        lng = ins[3][...]
        vln = (xh * lng + ins[4][...]).astype(bf16)
        rr = lax.broadcasted_iota(jnp.int32, (CHUNK, CHUNK), 0)
        cc = lax.broadcasted_iota(jnp.int32, (CHUNK, CHUNK), 1)
        causal = (rr >= cc).astype(f32)
        dvln_ref = scr[0]
        dgu_ref = scr[1]
        for c in range(tm // CHUNK):
            rows = slice(c * CHUNK, (c + 1) * CHUNK)
            for h in range(GM_HEADS):
                cols = slice(h * GM_HD, (h + 1) * GM_HD)
                w = ins[5][h]
                blk = vln[rows, cols]
                mixed = jnp.dot(w, blk, preferred_element_type=f32) + ins[6][h]
                dy = dya[rows, cols]
                dgu_ref[rows, cols] = dy * mixed
                dm = dy * gu[rows, cols]
                accs[3][h] += jnp.sum(dm, axis=1, keepdims=True)
                accs[2][h] += _bdot(dm, blk, NT) * causal
                dvln_ref[rows, cols] = _bdot(w, dm, TN)
        dvln = dvln_ref[...]
        accs[0][...] += jnp.sum(dvln * xh, axis=0, keepdims=True)
        accs[1][...] += jnp.sum(dvln, axis=0, keepdims=True)
        dxh = dvln * lng
        dgv = r * (dxh - jnp.mean(dxh, axis=-1, keepdims=True) - xh * jnp.mean(dxh * xh, axis=-1, keepdims=True))
        outs[0][...] = gelu_u_vjp(dgu_ref[...])[0].astype(bf16)
        outs[1][...] = gelu_v_vjp(dgv)[0].astype(bf16)
    return rowcall(name, body, T, tm,
                   [("row", uvz, D, 0), ("row", uvz, D, 1), ("row", d_ya, D, 0), ("const", ln_g), ("const", ln_b), ("const", wm), ("const", bs)],
                   [(D, bf16), (D, bf16)], accs=[(1, D), (1, D), (GM_HEADS, CHUNK, CHUNK), (GM_HEADS, CHUNK, 1)],
                   scratch=[pltpu.VMEM((tm, D), f32), pltpu.VMEM((tm, D), f32)])


def _conv_pre(i, x_ref, halo_ref, w_ref, b_ref, scr, tm):
    scr[pl.ds(0, CONV_HALO), :] = jnp.where(i > 0, halo_ref[...], 0.0)
    scr[pl.ds(CONV_HALO, tm), :] = x_ref[...]
    pre = b_ref[...]
    for k in range(CONV_K):
        pre = pre + w_ref[pl.ds(k, 1), :] * scr[pl.ds(CONV_HALO - (CONV_K - 1) + k, tm), :]
    return pre


def conv_fwd(name, T, tm, xbc, conv_w, conv_b):
    def body(i, n, ins, outs, accs, scr):
        outs[0][...] = _silu(_conv_pre(i, ins[0], ins[1], ins[2], ins[3], scr[0], tm))
    return rowcall(name, body, T, tm, [("row", xbc, CONV_DIM, 0), ("prev", xbc, CONV_DIM, 0, CONV_HALO), ("const", conv_w), ("const", conv_b)],
                   [(CONV_DIM, f32)], scratch=[pltpu.VMEM((tm + CONV_HALO, CONV_DIM), f32)])[0]


def conv_bwd_pre(name, T, tm, xbc, d_xc, conv_w, conv_b):
    def body(i, n, ins, outs, accs, scr):
        pre = _conv_pre(i, ins[0], ins[1], ins[3], ins[4], scr[0], tm)
        _, vjp = jax.vjp(_silu, pre)
        dpre = vjp(ins[2][...])[0]
        outs[0][...] = dpre
        accs[1][...] += jnp.sum(dpre, axis=0, keepdims=True)
        for k in range(CONV_K):
            accs[0][pl.ds(k, 1), :] += jnp.sum(dpre * scr[0][pl.ds(CONV_HALO - (CONV_K - 1) + k, tm), :], axis=0, keepdims=True)
    return rowcall(name, body, T, tm,
                   [("row", xbc, CONV_DIM, 0), ("prev", xbc, CONV_DIM, 0, CONV_HALO), ("row", d_xc, CONV_DIM, 0), ("const", conv_w), ("const", conv_b)],
                   [(CONV_DIM, f32)], accs=[(CONV_K, CONV_DIM), (1, CONV_DIM)], scratch=[pltpu.VMEM((tm + CONV_HALO, CONV_DIM), f32)])


def conv_bwd_x(name, T, tm, d_pre, conv_w):
    def body(i, n, ins, outs, accs, scr):
        s = scr[0]
        s[pl.ds(0, tm), :] = ins[0][...]
        s[pl.ds(tm, CONV_HALO), :] = jnp.where(i < n - 1, ins[1][...], 0.0)
        dx = jnp.zeros((tm, CONV_DIM), f32)
        for k in range(CONV_K):
            dx = dx + ins[2][pl.ds(k, 1), :] * s[pl.ds(CONV_K - 1 - k, tm), :]
        outs[0][...] = dx.astype(bf16)
    return rowcall(name, body, T, tm, [("row", d_pre, CONV_DIM, 0), ("next", d_pre, CONV_DIM, 0, CONV_HALO), ("const", conv_w)],
                   [(CONV_DIM, bf16)], scratch=[pltpu.VMEM((tm + CONV_HALO, CONV_DIM), f32)])[0]


def _ssd_chunk(X4, dtr, B4, C4, S4, dtb, alog, dsk):
    L = CHUNK
    rr = lax.broadcasted_iota(jnp.int32, (L, L), 0)
    cc = lax.broadcasted_iota(jnp.int32, (L, L), 1)
    tril = rr >= cc
    lane = lax.broadcasted_iota(jnp.int32, (1, DT_PAD), 1)
    sub = lax.broadcasted_iota(jnp.int32, (DT_PAD, 1), 0)
    glane = lax.broadcasted_iota(jnp.int32, (1, SSM_HPG * SSM_P), 1) // SSM_P
    dt = _softplus(dtr + dtb)
    a = -jnp.exp(alog)
    dA = dt * a
    acum = jnp.dot(tril.astype(f32), dA, precision=HI, preferred_element_type=f32)
    acumT = acum.T
    tot = jnp.sum(dA, axis=0, keepdims=True)
    ys, Sn = [], []
    for g in range(SSM_GROUPS):
        hm = [(glane == r).astype(f32) for r in range(SSM_HPG)]
        cols = [jnp.sum(acum * (lane == SSM_HPG * g + r).astype(f32), axis=1, keepdims=True) for r in range(SSM_HPG)]
        dtc = [jnp.sum(dt * (lane == SSM_HPG * g + r).astype(f32), axis=1, keepdims=True) for r in range(SSM_HPG)]
        tots = [jnp.sum(tot * (lane == SSM_HPG * g + r).astype(f32), axis=1, keepdims=True) for r in range(SSM_HPG)]
        dsc = [jnp.sum(dsk * (lane == SSM_HPG * g + r).astype(f32), axis=1, keepdims=True) for r in range(SSM_HPG)]
        x = X4[g]
        xdt = x * sum(dtc[r] * hm[r] for r in range(SSM_HPG))
        cb = _bdot(C4[g], B4[g], NT)
        y = x * sum(dsc[r] * hm[r] for r in range(SSM_HPG))
        for r in range(SSM_HPG):
            row = jnp.sum(acumT * (sub == SSM_HPG * g + r).astype(f32), axis=0, keepdims=True)
            dec = jnp.exp(jnp.where(tril, cols[r] - row, -jnp.inf))
            y = y + _bdot(cb * dec, xdt * hm[r])
        y = y + _bdot(C4[g], S4[g]) * sum(jnp.exp(cols[r]) * hm[r] for r in range(SSM_HPG))
        dte = sum(jnp.exp(tots[r] - cols[r]) * hm[r] for r in range(SSM_HPG))
        s_new = S4[g] * sum(jnp.exp(tots[r]) * hm[r] for r in range(SSM_HPG)) + _bdot(B4[g], xdt * dte, TN)
        ys.append(y)
        Sn.append(s_new)
    return tuple(ys), tuple(Sn)


def _ssd_ins(xc, dtr):
    gw = SSM_HPG * SSM_P
    ins = [("row", xc, gw, g) for g in range(SSM_GROUPS)]
    ins += [("row", xc, SSM_N, D // SSM_N + g) for g in range(SSM_GROUPS)]
    ins += [("row", xc, SSM_N, D // SSM_N + SSM_GROUPS + g) for g in range(SSM_GROUPS)]
    ins += [("row", dtr, DT_PAD, 0)]
    return ins


def ssd_fwd(name, T, xc, dtr, dtb, alog, dsk):
    gw = SSM_HPG * SSM_P

    def body(i, n, ins, outs, accs, scr):
        S = scr[0]

        @pl.when(i == 0)
        def _():
            S[...] = jnp.zeros(S.shape, f32)
        X4 = tuple(ins[g][...] for g in range(4))
        B4 = tuple(ins[4 + g][...] for g in range(4))
        C4 = tuple(ins[8 + g][...] for g in range(4))
        S4 = tuple(S[:, g * gw:(g + 1) * gw] for g in range(4))
        outs[1][...] = S[...]
        ys, Sn = _ssd_chunk(X4, ins[12][...], B4, C4, S4, ins[13][...], ins[14][...], ins[15][...])
        for g in range(4):
            outs[0][:, g * gw:(g + 1) * gw] = ys[g]
            S[:, g * gw:(g + 1) * gw] = Sn[g]
    ins = _ssd_ins(xc, dtr) + [("const", dtb), ("const", alog), ("const", dsk)]
    return rowcall(name, body, T, CHUNK, ins, [(D, f32), (D, f32)], scratch=[pltpu.VMEM((SSM_N, D), f32)])


def ssd_bwd(name, T, xc, dtr, sprev, d_y, dtb, alog, dsk):
    gw = SSM_HPG * SSM_P

    def body(i, n, ins, outs, accs, scr):
        dS = scr[0]

        @pl.when(i == n - 1)
        def _():
            dS[...] = jnp.zeros(dS.shape, f32)
        X4 = tuple(ins[g][...] for g in range(4))
        B4 = tuple(ins[4 + g][...] for g in range(4))
        C4 = tuple(ins[8 + g][...] for g in range(4))
        S4 = tuple(ins[13 + g][...] for g in range(4))
        dY4 = tuple(ins[17 + g][...] for g in range(4))
        dS4 = tuple(dS[:, g * gw:(g + 1) * gw] for g in range(4))
        _, vjp = jax.vjp(_ssd_chunk, X4, ins[12][...], B4, C4, S4, ins[21][...], ins[22][...], ins[23][...])
        dX4, ddtr, dB4, dC4, dSp, ddtb, dalog, ddsk = vjp((dY4, dS4))
        for g in range(4):
            outs[0][:, g * gw:(g + 1) * gw] = dX4[g]
            outs[0][:, D + g * SSM_N:D + (g + 1) * SSM_N] = dB4[g]
            outs[0][:, D + (SSM_GROUPS + g) * SSM_N:D + (SSM_GROUPS + g + 1) * SSM_N] = dC4[g]
            dS[:, g * gw:(g + 1) * gw] = dSp[g]
        outs[1][...] = ddtr.astype(bf16)
        accs[0][...] += ddtb
        accs[1][...] += dalog
        accs[2][...] += ddsk
    ins = _ssd_ins(xc, dtr) + [("row", sprev, gw, g) for g in range(4)] + [("row", d_y, gw, g) for g in range(4)]
    ins += [("const", dtb), ("const", alog), ("const", dsk)]
    return rowcall(name, body, T, CHUNK, ins, [(CONV_DIM, f32), (DT_PAD, bf16)], accs=[(1, DT_PAD)] * 3,
                   scratch=[pltpu.VMEM((SSM_N, D), f32)], reverse=True)


def _gate_group(y, z, g):
    return _rms(y * _silu(z), g)


def gate_fwd(name, T, tm, y, uvz, gn):
    def body(i, n, ins, outs, accs, scr):
        for g in range(SSM_GROUPS):
            cols = slice(g * 256, (g + 1) * 256)
            outs[0][:, cols] = _gate_group(ins[0][:, cols], ins[1][:, cols], ins[2][:, cols]).astype(bf16)
    return rowcall(name, body, T, tm, [("row", y, D, 0), ("row", uvz, D, 2), ("const", gn)], [(D, bf16)])[0]


def gate_bwd(name, T, tm, y, uvz, d_yb, gn):
    def body(i, n, ins, outs, accs, scr):
        for g in range(SSM_GROUPS):
            cols = slice(g * 256, (g + 1) * 256)
            _, vjp = jax.vjp(_gate_group, ins[0][:, cols], ins[1][:, cols], ins[3][:, cols])
            dy, dz, dg = vjp(ins[2][:, cols])
            outs[0][:, cols] = dy
            outs[1][:, cols] = dz.astype(bf16)
            accs[0][:, cols] += dg
    return rowcall(name, body, T, tm, [("row", y, D, 0), ("row", uvz, D, 2), ("row", d_yb, D, 0), ("const", gn)],
                   [(D, f32), (D, bf16)], accs=[(1, D)])


def _pool_diff(i, tm, h_ref, halo_ref, g_ref, scr):
    g = g_ref[...]
    yn = _rms(h_ref[...], g)
    scr[pl.ds(0, POOL_HALO), :] = jnp.where(i > 0, _rms(halo_ref[...], g), 0.0)
    scr[pl.ds(POOL_HALO, tm), :] = yn
    pos = (i * tm + lax.broadcasted_iota(jnp.int32, (tm, 1), 0) + 1).astype(f32)
    parts = []
    for gi, win in enumerate(POOL_WINDOWS):
        cols = slice(gi * POOL_GD, (gi + 1) * POOL_GD)
        s = scr[pl.ds(POOL_HALO, tm), cols]
        for j in range(1, win):
            s = s + scr[pl.ds(POOL_HALO - j, tm), cols]
        parts.append(s / jnp.minimum(pos, float(win)) - yn[:, cols])
    return parts


def pool_fwd(name, T, tm, h2, g_pre, pw, pb, psc):
    def body(i, n, ins, outs, accs, scr):
        parts = _pool_diff(i, tm, ins[0], ins[1], ins[2], scr[0])
        for gi in range(len(POOL_WINDOWS)):
            cols = slice(gi * POOL_GD, (gi + 1) * POOL_GD)
            o = _bdot(parts[gi], ins[3][gi]) + ins[4][:, cols]
            outs[0][:, cols] = o * ins[5][:, cols]
    return rowcall(name, body, T, tm, [("row", h2, D, 0), ("prev", h2, D, 0, POOL_HALO), ("const", g_pre), ("const", pw), ("const", pb), ("const", psc)],
                   [(D, f32)], scratch=[pltpu.VMEM((tm + POOL_HALO, D), f32)])[0]


def pool_bwd(name, T, tm, h2, d_pm, d_res, g_pre, pw, pb, psc):
    def body(i, n, ins, outs, accs, scr):
        parts = _pool_diff(i, tm, ins[0], ins[1], ins[5], scr[0])
        dpm = ins[2][...]
        psc_v = ins[8][...]
        dps = dpm * psc_v
        dps_halo = jnp.where(i < n - 1, ins[3][...] * psc_v, 0.0)
        accs[1][...] += jnp.sum(dps, axis=0, keepdims=True)
        pos = (i * tm + lax.broadcasted_iota(jnp.int32, (tm, 1), 0) + 1).astype(f32)
        pos_h = ((i + 1) * tm + lax.broadcasted_iota(jnp.int32, (POOL_HALO, 1), 0) + 1).astype(f32)
        r_scr = scr[1]
        dyn_scr = scr[2]
        for gi, win in enumerate(POOL_WINDOWS):
            cols = slice(gi * POOL_GD, (gi + 1) * POOL_GD)
            w = ins[6][gi]
            o = _bdot(parts[gi], w) + ins[7][:, cols]
            accs[2][:, cols] += jnp.sum(dpm[:, cols] * o, axis=0, keepdims=True)
            accs[0][gi] += _bdot(parts[gi], dps[:, cols], TN)
            q = _bdot(dps[:, cols], w, NT)
            qh = _bdot(dps_halo[:, cols], w, NT)
            r_scr[pl.ds(0, tm), cols] = q / jnp.minimum(pos, float(win))
            r_scr[pl.ds(tm, POOL_HALO), cols] = qh / jnp.minimum(pos_h, float(win))
            s = r_scr[pl.ds(0, tm), cols]
            for j in range(1, win):
                s = s + r_scr[pl.ds(j, tm), cols]
            dyn_scr[:, cols] = s - q
        dx, dg = _rms_bwd(ins[0][...], ins[5][...], dyn_scr[...])
        outs[0][...] = ins[4][...] + dx
        accs[3][...] += dg
    ins = [("row", h2, D, 0), ("prev", h2, D, 0, POOL_HALO), ("row", d_pm, D, 0), ("next", d_pm, D, 0, POOL_HALO), ("row", d_res, D, 0),
           ("const", g_pre), ("const", pw), ("const", pb), ("const", psc)]
    return rowcall(name, body, T, tm, ins, [(D, f32)], accs=[(4, POOL_GD, POOL_GD), (1, D), (1, D), (1, D)],
                   scratch=[pltpu.VMEM((tm + POOL_HALO, D), f32), pltpu.VMEM((tm + POOL_HALO, D), f32), pltpu.VMEM((tm, D), f32)])


def local_step(T, x, tgt, W):
    tm = 512 if T >= 1024 else T // 2
    TKW = 1024 if T >= 1024 else T
    ng = W["norm_g"]
    g = lambda l, j: ng[l, j][None, :]
    G = {}

    row_spec = pl.BlockSpec((tm, D), lambda j, i, k: (i, 0))
    sh_spec = [pl.BlockSpec((None, tm, FF_SH), lambda j, i, k, s=s: (s, i, 0)) for s in range(4)]

    def ffn_fwd(tag, n_bf, l):
        gate4, up4, act4 = ffn_up(f"ffn{tag}_up", T, tm, n_bf, W["wg4"], W["wu4"], l)
        wd_spec = [pl.BlockSpec((None, FF_SH, D), lambda j, i, k, s=s: (s, l, 0)) for s in range(4)]
        f = mm(f"ffn{tag}_down", (1, T // tm, 1), [(act4, sh_spec[s], W["wd4"], wd_spec[s]) for s in range(4)], NN, row_spec, SDS((T, D), f32))
        return gate4, up4, act4, f

    def ffn_bwd(tag, l, n_bf, gate4, up4, act4, d_f):
        d_gate4, d_up4 = ffn_dgu(f"ffn{tag}_dgu", T, tm, d_f, W["wd4"], gate4, up4, l)
        w_spec = [pl.BlockSpec((None, D, FF_SH), lambda j, i, k, s=s: (s, l, 0)) for s in range(4)]
        d_n = mm(f"ffn{tag}_dn", (1, T // tm, 1), [(d_gate4, sh_spec[s], W["wg4"], w_spec[s]) for s in range(4)]
                 + [(d_up4, sh_spec[s], W["wu4"], w_spec[s]) for s in range(4)], NT, row_spec, SDS((T, D), f32))

        def wgrad(nm, a4, b):
            return mm(nm, (4, 1, T // TKW),
                      [(a4, pl.BlockSpec((None, TKW, FF_SH), lambda s, j, k: (s, k, 0)), b, pl.BlockSpec((TKW, D), lambda s, j, k: (k, 0)))],
                      TN, pl.BlockSpec((None, FF_SH, D), lambda s, j, k: (s, 0, 0)), SDS((4, FF_SH, D), f32))
        return d_n, wgrad(f"ffn{tag}_dwg", d_gate4, n_bf), wgrad(f"ffn{tag}_dwu", d_up4, n_bf), wgrad(f"ffn{tag}_dwd", act4, d_f)

    y0 = rms_to_bf16("l0_prenorm", T, tm, x, g(0, 0))
    uvz = matmul("in_uvz", [(y0, W["w_uvz"])], "nn", f32, tm, 1024)
    xbc = matmul("in_xbc", [(y0, W["w_xbc"])], "nn", f32, tm, 1024)
    dtr = matmul("in_dt", [(y0, W["w_dt"])], "nn", f32, tm, DT_PAD)
    y_a = gmlp_fwd("gmlp_fwd", T, tm, uvz, W["ln_g"], W["ln_b"], W["wm"], W["bs"])
    xc = conv_fwd("conv_fwd", T, tm, xbc, W["conv_w"], W["conv_b"])
    y_ssd, sprev = ssd_fwd("ssd_fwd", T, xc, dtr, W["dtb"], W["alog"], W["dsk"])
    y_b = gate_fwd("gate_fwd", T, tm, y_ssd, uvz, W["gn"])
    half = D // 2
    wo4 = W["wo4"]
    ycol = [pl.BlockSpec((tm, half), lambda j, i, k, cb=cb: (i, cb)) for cb in range(2)]
    wo_s = [pl.BlockSpec((None, half, D), lambda j, i, k, s=s: (s, 0, 0)) for s in range(4)]
    mixo = mm("out_proj", (1, T // tm, 1), [(y_a, ycol[0], wo4, wo_s[0]), (y_a, ycol[1], wo4, wo_s[1]),
                                            (y_b, ycol[0], wo4, wo_s[2]), (y_b, ycol[1], wo4, wo_s[3])], NN, row_spec, SDS((T, D), f32))
    h1, n1 = resid_norm("l0_mix_resid", T, tm, x, mixo, g(0, 1), g(0, 2))
    gate0, up0, act0, f1 = ffn_fwd("0", n1, 0)
    (h2,) = resid_norm("l0_ffn_resid", T, tm, h1, f1, g(0, 3), None)
    pm = pool_fwd("pool_fwd", T, tm, h2, g(1, 0), W["pool_w"], W["pool_b"], W["pool_scale"])
    h3, n3 = resid_norm("l1_mix_resid", T, tm, h2, pm, g(1, 1), g(1, 2))
    gate1, up1, act1, f2 = ffn_fwd("1", n3, 1)
    dh4, d_f2, loss_acc, dg13 = final_loss_bwd("loss_bwd", T, tm, h3, f2, tgt, g(1, 3))
    d_n3, dwg1, dwu1, dwd1 = ffn_bwd("1", 1, n3, gate1, up1, act1, d_f2)
    d_h3, d_pm, dg12, dg11 = bwd_pre_post("l1_mix_bwd", T, tm, h3, pm, dh4, d_n3, g(1, 2), g(1, 1), f32)
    d_h2, G["pool_w"], G["pool_b"], G["pool_scale"], dg10 = pool_bwd("pool_bwd", T, tm, h2, d_pm, d_h3, g(1, 0), W["pool_w"], W["pool_b"], W["pool_scale"])
    d_f1, dg03 = bwd_post("l0_ffn_bwd", T, tm, f1, d_h2, g(0, 3))
    d_n1, dwg0, dwu0, dwd0 = ffn_bwd("0", 0, n1, gate0, up0, act0, d_f1)
    d_h1, d_mixo, dg02, dg01 = bwd_pre_post("l0_mix_bwd", T, tm, h1, mixo, d_h2, d_n1, g(0, 2), g(0, 1), bf16)
    def d_ycat(nm, s0):
        return mm(nm, (2, T // tm, 1), [(d_mixo, row_spec, wo4, pl.BlockSpec((None, half, D), lambda j, i, k: (s0 + j, 0, 0)))], NT,
                  pl.BlockSpec((tm, half), lambda j, i, k: (i, j)), SDS((T, D), f32))

    def d_wo(nm, y):
        return mm(nm, (2, 1, T // TKW), [(y, pl.BlockSpec((TKW, half), lambda s, j, k: (k, s)), d_mixo, pl.BlockSpec((TKW, D), lambda s, j, k: (k, 0)))],
                  TN, pl.BlockSpec((None, half, D), lambda s, j, k: (s, 0, 0)), SDS((2, half, D), f32))
    d_ya, d_yb = d_ycat("out_proj_dya", 0), d_ycat("out_proj_dyb", 2)
    dwo_a, dwo_b = d_wo("out_proj_dwa", y_a), d_wo("out_proj_dwb", y_b)
    d_yssd, d_z, G["gn"] = gate_bwd("gate_bwd", T, tm, y_ssd, uvz, d_yb, W["gn"])
    d_xc, d_dtr, G["dtb"], G["alog"], G["dsk"] = ssd_bwd("ssd_bwd", T, xc, dtr, sprev, d_yssd, W["dtb"], W["alog"], W["dsk"])
    d_pre, G["conv_w"], G["conv_b"] = conv_bwd_pre("conv_bwd_pre", T, tm, xbc, d_xc, W["conv_w"], W["conv_b"])
    d_xbc = conv_bwd_x("conv_bwd_x", T, tm, d_pre, W["conv_w"])
    d_u, d_v, G["ln_g"], G["ln_b"], G["wm"], G["bs"] = gmlp_bwd("gmlp_bwd", T, tm, uvz, d_ya, W["ln_g"], W["ln_b"], W["wm"], W["bs"])
    w_u, w_v, w_z = W["w_uvz"][:, :D], W["w_uvz"][:, D:2 * D], W["w_uvz"][:, 2 * D:]
    d_y0 = matmul("in_dy0", [(d_u, w_u), (d_v, w_v), (d_z, w_z), (d_xbc, W["w_xbc"]), (d_dtr, W["w_dt"])], "nt", f32, tm, 1024)
    G["w_inT"] = [matmul("in_dwu", [(d_u, y0)], "tn", f32, 1024, 1024, TKW), matmul("in_dwv", [(d_v, y0)], "tn", f32, 1024, 1024, TKW),
                  matmul("in_dwz", [(d_z, y0)], "tn", f32, 1024, 1024, TKW), matmul("in_dwxbc", [(d_xbc, y0)], "tn", f32, 1024, 1024, TKW),
                  matmul("in_dwdt", [(d_dtr, y0)], "tn", f32, DT_PAD, 1024, TKW)[:N_HEADS]]
    grad_x, dg00 = bwd_pre("l0_pre_bwd", T, tm, x, d_h1, d_y0, g(0, 0))
    G["norm_g"] = jnp.stack([jnp.concatenate([dg00, dg01, dg02, dg03], 0), jnp.concatenate([dg10, dg11, dg12, dg13], 0)])
    G["wo4"] = [dwo_a[0], dwo_a[1], dwo_b[0], dwo_b[1]]
    G["wgT4"], G["wuT4"], G["wd4"] = [dwg0, dwg1], [dwu0, dwu1], [dwd0, dwd1]
    return loss_acc, grad_x, G


def build_weights(Wf):
    causal = jnp.tril(jnp.ones((CHUNK, CHUNK), bool))
    w_in = Wf["w_in"].astype(bf16)
    pad16 = lambda v: jnp.pad(v.reshape(1, N_HEADS).astype(f32), ((0, 0), (0, DT_PAD - N_HEADS)))
    return {
        "norm_g": Wf["norm_g"],
        "w_uvz": w_in[:, :3 * D], "w_xbc": w_in[:, 3 * D:3 * D + CONV_DIM],
        "w_dt": jnp.pad(w_in[:, 3 * D + CONV_DIM:], ((0, 0), (0, DT_PAD - N_HEADS))),
        "ln_g": Wf["gm_ln_g"].reshape(1, D), "ln_b": Wf["gm_ln_b"].reshape(1, D),
        "wm": jnp.where(causal[None], Wf["gm_ws"], 0).astype(bf16), "bs": Wf["gm_bs"].reshape(GM_HEADS, CHUNK, 1),
        "conv_w": Wf["conv_w"], "conv_b": Wf["conv_b"].reshape(1, CONV_DIM),
        "dtb": pad16(Wf["dt_bias"]), "alog": pad16(Wf["a_log"]), "dsk": pad16(Wf["d_skip"]),
        "gn": Wf["ssm_norm_g"].reshape(1, D),
        "wo4": Wf["wo4"].astype(bf16), "wg4": Wf["wg4"].astype(bf16), "wu4": Wf["wu4"].astype(bf16), "wd4": Wf["wd4"].astype(bf16),
        "pool_w": Wf["pool_w"].astype(bf16), "pool_b": Wf["pool_b"].reshape(1, D), "pool_scale": Wf["pool_scale"].reshape(1, D),
    }


def small_grads(G):
    return {
        "norm_g": G["norm_g"],
        "gm_ln_g": G["ln_g"].reshape(D), "gm_ln_b": G["ln_b"].reshape(D),
        "gm_ws": G["wm"], "gm_bs": G["bs"].reshape(GM_HEADS, CHUNK),
        "conv_w": G["conv_w"], "conv_b": G["conv_b"].reshape(CONV_DIM),
        "dt_bias": G["dtb"][0, :N_HEADS], "a_log": G["alog"][0, :N_HEADS], "d_skip": G["dsk"][0, :N_HEADS],
        "ssm_norm_g": G["gn"].reshape(D),
        "pool_b": G["pool_b"].reshape(4, POOL_GD), "pool_scale": G["pool_scale"].reshape(D),
    }


MESH_ID = pl.DeviceIdType.MESH
ANY = pl.BlockSpec(memory_space=pl.ANY)


DMA_CHUNK_BYTES = 2 << 20
DMA_MAX_CHUNKS = 32


def _pieces(view, axis, align):
    shape = view.shape
    nbytes = math.prod(shape) * jnp.dtype(view.dtype).itemsize
    n = max(1, min(DMA_MAX_CHUNKS, -(-nbytes // DMA_CHUNK_BYTES)))
    rows = shape[axis]
    size = -(-rows // n)
    size = -(-size // align) * align
    out = []
    for s in range(0, rows, size):
        idx = [slice(None)] * len(shape)
        idx[axis] = pl.ds(s, min(size, rows - s))
        out.append(tuple(idx))
    return out


def comm_call(name, operands, out_shapes, plan):
    n_in = len(operands)
    n_out = len(out_shapes)
    n_remote, n_local = plan((0, 0, 0), [None] * n_in, [None] * n_out, True)

    def body(*refs):
        in_refs, out_refs = refs[:n_in], refs[n_in:n_in + n_out]
        send_sems, recv_sems, local_sems = refs[n_in + n_out:]
        me = (lax.axis_index("x"), lax.axis_index("y"), lax.axis_index("c"))
        remote, local = plan(me, in_refs, out_refs, False)
        align = lambda v: 16 if v.dtype == bf16 else 8
        for j, (s, d, axis) in enumerate(local):
            for ix in _pieces(s, axis, align(s)):
                pltpu.make_async_copy(s.at[ix], d.at[ix], local_sems.at[j]).start()
        peers = [tuple((1 - m) if f else m for m, f in zip(me, flip)) for flip, *_ in remote]
        for k, (flip, src, dst, _, axis) in enumerate(remote):
            for ix in _pieces(src, axis, align(src)):
                pltpu.make_async_remote_copy(src_ref=src.at[ix], dst_ref=dst.at[ix], send_sem=send_sems.at[k], recv_sem=recv_sems.at[k],
                                             device_id=peers[k], device_id_type=MESH_ID).start()
        for k, (flip, src, dst, landing, axis) in enumerate(remote):
            pltpu.make_async_remote_copy(src_ref=landing, dst_ref=landing, send_sem=send_sems.at[k], recv_sem=recv_sems.at[k],
                                         device_id=peers[k], device_id_type=MESH_ID).wait_recv()
        for k, (flip, src, dst, landing, axis) in enumerate(remote):
            pltpu.make_async_remote_copy(src_ref=src, dst_ref=dst, send_sem=send_sems.at[k], recv_sem=recv_sems.at[k],
                                         device_id=peers[k], device_id_type=MESH_ID).wait_send()
        for j, (s, d, axis) in enumerate(local):
            pltpu.make_async_copy(s, d, local_sems.at[j]).wait()

    return pl.pallas_call(
        body, name=name, out_shape=list(out_shapes), in_specs=[ANY] * n_in, out_specs=[ANY] * n_out,
        scratch_shapes=[pltpu.SemaphoreType.DMA((n_remote,)), pltpu.SemaphoreType.DMA((n_remote,)), pltpu.SemaphoreType.DMA((max(n_local, 1),))],
    )(*operands)


CHIP_FLIPS = ((1, 0, 0), (0, 1, 0), (1, 1, 0))
PAIR_FLIP = (0, 0, 1)


def gather_over_chips(name, arrs):
    def plan(me, ins, outs, count):
        if count:
            return len(CHIP_FLIPS) * len(arrs), len(arrs)
        k = 2 * me[0] + me[1]
        remote, local = [], []
        for a in range(len(arrs)):
            for flip in CHIP_FLIPS:
                kp = 2 * ((1 - me[0]) if flip[0] else me[0]) + ((1 - me[1]) if flip[1] else me[1])
                remote.append((flip, ins[a], outs[a].at[k], outs[a].at[kp], 0))
            local.append((ins[a], outs[a].at[k], 0))
        return remote, local
    return comm_call(name, arrs, [SDS((4,) + a.shape, a.dtype) for a in arrs], plan)


def pair_split_exchange(name, p, rh):
    def plan(me, ins, outs, count):
        if count:
            return 1, 0
        theirs = ins[0].at[:, pl.ds(pl.multiple_of((1 - me[2]) * rh, 8), rh), :]
        return [(PAIR_FLIP, theirs, outs[0], outs[0], 1)], []
    return comm_call(name, [p], [SDS((4, rh, p.shape[2]), p.dtype)], plan)[0]


def scatter_over_chips(name, cs):
    def plan(me, ins, outs, count):
        if count:
            return len(CHIP_FLIPS), 0
        k = 2 * me[0] + me[1]
        remote = []
        for flip in CHIP_FLIPS:
            kp = 2 * ((1 - me[0]) if flip[0] else me[0]) + ((1 - me[1]) if flip[1] else me[1])
            remote.append((flip, ins[0].at[kp], outs[0].at[k], outs[0].at[kp], 0))
        return remote, []
    return comm_call(name, [cs], [SDS(cs.shape, cs.dtype)], plan)[0]


def pair_swap(name, half):
    def plan(me, ins, outs, count):
        if count:
            return 1, 0
        return [(PAIR_FLIP, ins[0], outs[0], outs[0], 0)], []
    return comm_call(name, [half], [SDS(half.shape, half.dtype)], plan)[0]


def _row_tile(rows, cap=512):
    if rows <= cap:
        return rows
    t = cap - cap % 8
    while rows % t:
        t -= 8
    return t


SUM_ROWS = 448


def pair_sum(name, packs, got, c_arr):
    rh = got.shape[1]
    nb = rh // SUM_ROWS

    def kern(c_ref, a_ref, b_ref, o16_ref):
        o16_ref[...] = (a_ref[...] + b_ref[...]).astype(bf16)
    blk = (None, SUM_ROWS, D)
    grid_spec = pltpu.PrefetchScalarGridSpec(
        num_scalar_prefetch=1, grid=(4, nb),
        in_specs=[pl.BlockSpec(blk, lambda s, i, c: (s, c[0] * nb + i, 0)), pl.BlockSpec(blk, lambda s, i, c: (s, i, 0))],
        out_specs=pl.BlockSpec(blk, lambda s, i, c: (s, i, 0)))
    return pl.pallas_call(kern, name=name, grid_spec=grid_spec, out_shape=SDS(got.shape, bf16),
                          compiler_params=pltpu.CompilerParams(dimension_semantics=("parallel", "parallel")))(c_arr, packs, got)


def chip_sum(name, own16, landed16, k_arr):
    rh = own16.shape[1]
    nb = rh // SUM_ROWS

    def kern(k_ref, own_ref, l0, l1, l2, l3, o_ref):
        k = k_ref[0]
        s = None
        for j, lref in enumerate((l0, l1, l2, l3)):
            t = jnp.where(k == j, own_ref[...], lref[...]).astype(f32)
            s = t if s is None else s + t
        o_ref[...] = s
    blk = (None, SUM_ROWS, D)
    land = [pl.BlockSpec(blk, lambda i, k, j=j: (jnp.where(k[0] == j, (j + 1) % N_CHIPS, j), i, 0)) for j in range(N_CHIPS)]
    grid_spec = pltpu.PrefetchScalarGridSpec(
        num_scalar_prefetch=1, grid=(nb,),
        in_specs=[pl.BlockSpec(blk, lambda i, k: (k[0], i, 0))] + land,
        out_specs=pl.BlockSpec((SUM_ROWS, D), lambda i, k: (i, 0)))
    return pl.pallas_call(kern, name=name, grid_spec=grid_spec, out_shape=SDS((rh, D), f32),
                          compiler_params=pltpu.CompilerParams(dimension_semantics=("parallel",)))(k_arr, own16, landed16, landed16, landed16, landed16)


def adamw(name, w, g, m, v):
    R, C = w.shape
    tr = _row_tile(R, 256)

    def kern(w_ref, g_ref, m_ref, v_ref, d_ref, mo_ref, vo_ref):
        gg = g_ref[...]
        mn = ADAM_B1 * m_ref[...] + (1.0 - ADAM_B1) * gg
        vn = ADAM_B2 * v_ref[...] + (1.0 - ADAM_B2) * jnp.square(gg)
        m_hat = mn / (1.0 - ADAM_B1 ** ADAM_STEP)
        v_hat = vn / (1.0 - ADAM_B2 ** ADAM_STEP)
        d_ref[...] = -ADAM_LR * (m_hat / (jnp.sqrt(v_hat) + ADAM_EPS) + ADAM_WD * w_ref[...])
        mo_ref[...] = mn
        vo_ref[...] = vn
    spec = pl.BlockSpec((tr, C), lambda i: (i, 0))
    s = SDS((R, C), f32)
    return pl.pallas_call(kern, name=name, grid=(R // tr,), in_specs=[spec] * 4, out_specs=[spec] * 3, out_shape=[s, s, s],
                          compiler_params=pltpu.CompilerParams(dimension_semantics=("parallel",)))(w, g, m, v)


WEIGHT_NAMES = ("norm_g", "w_in", "gm_ln_g", "gm_ln_b", "gm_ws", "gm_bs", "conv_w", "conv_b", "dt_bias", "a_log", "d_skip",
                "ssm_norm_g", "w_out", "pool_w", "pool_b", "pool_scale", "ffn_w_gate", "ffn_w_up", "ffn_w_down")
SMALL = ("norm_g", "conv_w", "pool_b", "pool_scale")
REPL = ("gm_ln_g", "gm_ln_b", "gm_ws", "gm_bs", "conv_b", "dt_bias", "a_log", "d_skip", "ssm_norm_g")
SMALL_AXIS = {"norm_g": 2, "conv_w": 1, "pool_b": 1, "pool_scale": 0}
N_CHIPS = 4
IN_SH = IN_DIM // N_CHIPS
SMALL_ROWS = 8
REPL_ROWS = 72
OFF_OUT, OFF_GATE, OFF_UP, OFF_DOWN = 0, 512, 512 + 2 * FF_SH, 512 + 4 * FF_SH
OFF_POOL = OFF_DOWN + 2 * FF_SH
OFF_SMALL = OFF_POOL + 64
OFF_REPL = OFF_SMALL + SMALL_ROWS
OFF_IN = OFF_REPL + REPL_ROWS
SLOT_END = OFF_IN + IN_SH
SLOT_ROWS = 6272
HALF_ROWS = SLOT_ROWS // 2


def _flat_rows(pieces, rows):
    v = jnp.concatenate([p.reshape(-1) for p in pieces])
    return jnp.pad(v, (0, rows * D - v.shape[0])).reshape(rows, D)


def _shard_small(name, full, k):
    ax = SMALL_AXIS[name]
    n = full.shape[ax] // N_CHIPS
    return lax.slice_in_dim(full, k * n, (k + 1) * n, axis=ax)


def _drop1(name, a):
    return a if name == "norm_g" else a[0]


def _row_range(blocks, lo, hi):
    out, off = [], 0
    for b in blocks:
        n = b.shape[0]
        a, e = max(lo, off), min(hi, off + n)
        if a < e:
            out.append(b[a - off:e - off])
        off += n
    return out


def gather_weights(w_sh):
    big = [w_sh["w_in"][0], w_sh["w_out"][0], w_sh["pool_w"][0].reshape(4 * 64, POOL_GD), w_sh["ffn_w_gate"].reshape(2 * D, FF_SH),
           w_sh["ffn_w_up"].reshape(2 * D, FF_SH), w_sh["ffn_w_down"].reshape(2 * FF_SH, D)]
    small_pack = _flat_rows([w_sh[n] for n in SMALL], SMALL_ROWS)
    s_in, s_out, s_pool, s_gate, s_up, s_down, s_small = gather_over_chips("gather_weights", [b.astype(bf16) for b in big] + [small_pack])
    Wf = {n: w_sh[n][0] for n in REPL}
    Wf["w_in"] = s_in.transpose(1, 0, 2).reshape(D, IN_DIM)
    Wf["pool_w"] = s_pool.reshape(N_CHIPS, 4, 64, POOL_GD).transpose(1, 0, 2, 3).reshape(4, POOL_GD, POOL_GD)
    Wf["wo4"], Wf["wg4"], Wf["wu4"], Wf["wd4"] = s_out, s_gate, s_up, s_down
    small_shapes = [_drop1(n, w_sh[n]).shape for n in SMALL]
    parts = [_split_rows(s_small[k], small_shapes) for k in range(N_CHIPS)]
    for j, n in enumerate(SMALL):
        Wf[n] = jnp.concatenate([parts[k][j] for k in range(N_CHIPS)], axis=SMALL_AXIS[n])
    return Wf


def pack_grads(G):
    sg = small_grads(G)
    repl = _flat_rows([sg[n] for n in REPL], REPL_ROWS)
    slots = []
    for k in range(N_CHIPS):
        rows = [G["wo4"][k], G["wgT4"][0][k], G["wgT4"][1][k], G["wuT4"][0][k], G["wuT4"][1][k], G["wd4"][0][k], G["wd4"][1][k],
                G["pool_w"][:, k * 64:(k + 1) * 64, :].reshape(64, D), _flat_rows([_shard_small(n, sg[n], k) for n in SMALL], SMALL_ROWS), repl]
        rows += _row_range(G["w_inT"], k * IN_SH, (k + 1) * IN_SH)
        rows.append(jnp.zeros((SLOT_ROWS - SLOT_END, D), f32))
        slots.append(jnp.concatenate(rows, axis=0))
    return jnp.stack(slots)


def unpack_grads(total, w_sh):
    g = {"w_out": total[OFF_OUT:OFF_GATE], "ffn_w_down": total[OFF_DOWN:OFF_POOL], "pool_w": total[OFF_POOL:OFF_SMALL],
         "ffn_w_gate": jnp.stack([total[OFF_GATE + l * FF_SH:OFF_GATE + (l + 1) * FF_SH].T for l in range(2)]),
         "ffn_w_up": jnp.stack([total[OFF_UP + l * FF_SH:OFF_UP + (l + 1) * FF_SH].T for l in range(2)]),
         "w_in": total[OFF_IN:SLOT_END].T}
    small = _split_rows(total[OFF_SMALL:OFF_REPL], [_drop1(n, w_sh[n]).shape for n in SMALL])
    repl = _split_rows(total[OFF_REPL:OFF_IN], [w_sh[n][0].shape for n in REPL])
    g.update(zip(SMALL, small))
    g.update(zip(REPL, repl))
    return {n: g[n].reshape(w_sh[n].shape) for n in WEIGHT_NAMES}


def _split_rows(flat2d, shapes):
    v = flat2d.reshape(-1)
    out, off = [], 0
    for s in shapes:
        n = math.prod(s)
        out.append(v[off:off + n].reshape(s))
        off += n
    return out


def kernel(x, norm_g, w_in, gm_ln_g, gm_ln_b, gm_ws, gm_bs, conv_w, conv_b, dt_bias, a_log, d_skip, ssm_norm_g, w_out, pool_w, pool_b, pool_scale, ffn_w_gate, ffn_w_up, ffn_w_down, loss_target, m_norm_g, m_w_in, m_gm_ln_g, m_gm_ln_b, m_gm_ws, m_gm_bs, m_conv_w, m_conv_b, m_dt_bias, m_a_log, m_d_skip, m_ssm_norm_g, m_w_out, m_pool_w, m_pool_b, m_pool_scale, m_ffn_w_gate, m_ffn_w_up, m_ffn_w_down, v_norm_g, v_w_in, v_gm_ln_g, v_gm_ln_b, v_gm_ws, v_gm_bs, v_conv_w, v_conv_b, v_dt_bias, v_a_log, v_d_skip, v_ssm_norm_g, v_w_out, v_pool_w, v_pool_b, v_pool_scale, v_ffn_w_gate, v_ffn_w_up, v_ffn_w_down):
    T = x.shape[1]
    w_sh = dict(zip(WEIGHT_NAMES, (norm_g, w_in, gm_ln_g, gm_ln_b, gm_ws, gm_bs, conv_w, conv_b, dt_bias, a_log, d_skip, ssm_norm_g, w_out,
                                   pool_w, pool_b, pool_scale, ffn_w_gate, ffn_w_up, ffn_w_down)))
    m_sh = dict(zip(WEIGHT_NAMES, (m_norm_g, m_w_in, m_gm_ln_g, m_gm_ln_b, m_gm_ws, m_gm_bs, m_conv_w, m_conv_b, m_dt_bias, m_a_log, m_d_skip,
                                   m_ssm_norm_g, m_w_out, m_pool_w, m_pool_b, m_pool_scale, m_ffn_w_gate, m_ffn_w_up, m_ffn_w_down)))
    v_sh = dict(zip(WEIGHT_NAMES, (v_norm_g, v_w_in, v_gm_ln_g, v_gm_ln_b, v_gm_ws, v_gm_bs, v_conv_w, v_conv_b, v_dt_bias, v_a_log, v_d_skip,
                                   v_ssm_norm_g, v_w_out, v_pool_w, v_pool_b, v_pool_scale, v_ffn_w_gate, v_ffn_w_up, v_ffn_w_down)))

    W = build_weights(gather_weights(w_sh))

    loss_acc, grad_x, G = local_step(T, x[0], loss_target[0], W)

    my_c = lax.axis_index("c")
    c_arr = my_c.astype(jnp.int32).reshape(1)
    k_arr = (2 * lax.axis_index("x") + lax.axis_index("y")).astype(jnp.int32).reshape(1)
    packs = pack_grads(G)
    got = pair_split_exchange("grads_pair_split", packs, HALF_ROWS)
    pair16 = pair_sum("grads_pair_sum", packs, got, c_arr)
    landed = scatter_over_chips("grads_scatter", pair16)
    half = chip_sum("grads_chip_sum", pair16, landed, k_arr)
    other = pair_swap("grads_pair_swap", half)
    total = jnp.concatenate([jnp.where(my_c == 0, half, other), jnp.where(my_c == 0, other, half)], axis=0)
    grads = unpack_grads(total, w_sh)

    delta, new_m, new_v = {}, {}, {}
    for n in WEIGHT_NAMES:
        shp = w_sh[n].shape
        two_d = (-1, shp[-1])
        d_, m_, v_ = adamw("adamw_" + n, w_sh[n].reshape(two_d), grads[n].reshape(two_d), m_sh[n].reshape(two_d), v_sh[n].reshape(two_d))
        delta[n], new_m[n], new_v[n] = d_.reshape(shp), m_.reshape(shp), v_.reshape(shp)

    loss = lax.psum(loss_acc[0, 0], ("x", "y", "c"))
    return (loss, grad_x[None], *[grads[n] for n in WEIGHT_NAMES], *[delta[n] for n in WEIGHT_NAMES],
            *[new_m[n] for n in WEIGHT_NAMES], *[new_v[n] for n in WEIGHT_NAMES])
```

```python
import functools
import math

import jax
import jax.numpy as jnp
from jax import lax
from jax.experimental import pallas as pl
from jax.experimental.pallas import tpu as pltpu

f32, bf16 = jnp.float32, jnp.bfloat16
SDS = jax.ShapeDtypeStruct

D = 1024
EPS = 1e-6
CHUNK = 128
GM_HEADS, GM_HD = 4, 256
SSM_GROUPS, SSM_HPG, SSM_P, SSM_N = 4, 4, 64, 128
N_HEADS = SSM_GROUPS * SSM_HPG
CONV_K = 4
CONV_DIM = 2048
POOL_WINDOWS = (2, 4, 8, 16)
POOL_GD = 256
POOL_HALO = 16
CONV_HALO = 8
D_FF = 2816
DT_PAD = 128
IN_DIM = 5136

ADAM_LR, ADAM_B1, ADAM_B2, ADAM_EPS, ADAM_WD, ADAM_STEP = 0.001, 0.9, 0.999, 1e-08, 0.01, 10

NT = (((1,), (1,)), ((), ()))
TN = (((0,), (0,)), ((), ()))
NN = (((1,), (0,)), ((), ()))
HI = lax.Precision.HIGHEST


def _silu(x):
    return x * jax.nn.sigmoid(x)


def _softplus(x):
    return jnp.maximum(x, 0.0) + jnp.log1p(jnp.exp(-jnp.abs(x)))


def _rms(x, g):
    return x * lax.rsqrt(jnp.mean(x * x, axis=-1, keepdims=True) + EPS) * g


def _rms_bwd(x, g, dy):
    r = lax.rsqrt(jnp.mean(x * x, axis=-1, keepdims=True) + EPS)
    xh = x * r
    dxh = dy * g
    dx = r * (dxh - xh * jnp.mean(dxh * xh, axis=-1, keepdims=True))
    return dx, jnp.sum(dy * xh, axis=0, keepdims=True)


def _bdot(a, b, dims=NN):
    return lax.dot_general(a.astype(bf16), b.astype(bf16), dims, preferred_element_type=f32)


def matmul(name, pairs, mode, out_dtype, tm, tn, tk=None):
    a0, b0 = pairs[0]
    if mode == "tn":
        M, N, K = a0.shape[1], b0.shape[1], a0.shape[0]
    else:
        M, K = a0.shape
        N = b0.shape[1] if mode == "nn" else b0.shape[0]
    tm, tn = min(tm, M), min(tn, N)
    assert M % tm == 0 and N % tn == 0, (name, M, N, tm, tn)
    if tk is None:
        nk = 1
    else:
        assert len(pairs) == 1 and K % tk == 0
        nk = K // tk
    dims = {"nn": NN, "nt": NT, "tn": TN}[mode]
    in_specs, args = [], []
    for a, b in pairs:
        kk = (a.shape[0] if mode == "tn" else a.shape[1]) if tk is None else tk
        if mode == "tn":
            in_specs.append(pl.BlockSpec((kk, tm), lambda j, i, k: (k, i)))
            in_specs.append(pl.BlockSpec((kk, tn), lambda j, i, k: (k, j)))
        elif mode == "nn":
            in_specs.append(pl.BlockSpec((tm, kk), lambda j, i, k: (i, k)))
            in_specs.append(pl.BlockSpec((kk, tn), lambda j, i, k: (k, j)))
        else:
            in_specs.append(pl.BlockSpec((tm, kk), lambda j, i, k: (i, k)))
            in_specs.append(pl.BlockSpec((tn, kk), lambda j, i, k: (j, k)))
        args += [a, b]
    npairs = len(pairs)

    def kern(*refs):
        o = refs[2 * npairs]
        part = None
        for p in range(npairs):
            d = _bdot(refs[2 * p][...], refs[2 * p + 1][...], dims)
            part = d if part is None else part + d
        if nk == 1:
            o[...] = part.astype(out_dtype)
        else:
            acc = refs[2 * npairs + 1]
            k = pl.program_id(2)

            @pl.when(k == 0)
            def _():
                acc[...] = part

            @pl.when(k > 0)
            def _():
                acc[...] += part

            @pl.when(k == nk - 1)
            def _():
                o[...] = acc[...].astype(out_dtype)

    return pl.pallas_call(
        kern, name=name, grid=(N // tn, M // tm, nk),
        in_specs=in_specs, out_specs=pl.BlockSpec((tm, tn), lambda j, i, k: (i, j)),
        out_shape=SDS((M, N), out_dtype),
        scratch_shapes=[pltpu.VMEM((tm, tn), f32)] if nk > 1 else [],
        compiler_params=pltpu.CompilerParams(dimension_semantics=("parallel", "parallel", "arbitrary")),
    )(*args)


def mm(name, grid, pairs, dims, o_spec, out_shape):
    nk = grid[2]
    npairs = len(pairs)
    in_specs, args = [], []
    for a, a_spec, b, b_spec in pairs:
        in_specs += [a_spec, b_spec]
        args += [a, b]
    blk = tuple(d for d in o_spec.block_shape if d is not None)

    def kern(*refs):
        o = refs[2 * npairs]
        part = None
        for p in range(npairs):
            d = _bdot(refs[2 * p][...], refs[2 * p + 1][...], dims)
            part = d if part is None else part + d
        if nk == 1:
            o[...] = part.astype(o.dtype)
        else:
            acc = refs[2 * npairs + 1]
            k = pl.program_id(2)

            @pl.when(k == 0)
            def _():
                acc[...] = part

            @pl.when(k > 0)
            def _():
                acc[...] += part

            @pl.when(k == nk - 1)
            def _():
                o[...] = acc[...].astype(o.dtype)

    return pl.pallas_call(
        kern, name=name, grid=grid, in_specs=in_specs, out_specs=o_spec, out_shape=out_shape,
        scratch_shapes=[pltpu.VMEM(blk, f32)] if nk > 1 else [],
        compiler_params=pltpu.CompilerParams(dimension_semantics=("parallel", "parallel", "arbitrary")),
    )(*args)


FF_SH = D_FF // 4


def ffn_up(name, T, tm, n_bf, wg4, wu4, l):
    def kern(n_ref, wg_ref, wu_ref, g_ref, u_ref, a_ref):
        n = n_ref[...]
        g = jnp.dot(n, wg_ref[...], preferred_element_type=f32)
        u = jnp.dot(n, wu_ref[...], preferred_element_type=f32)
        g_ref[...] = g.astype(bf16)
        u_ref[...] = u.astype(bf16)
        a_ref[...] = (_silu(g) * u).astype(bf16)
    w_spec = pl.BlockSpec((None, D, FF_SH), lambda k, i: (k, l, 0))
    o_spec = pl.BlockSpec((None, tm, FF_SH), lambda k, i: (k, i, 0))
    s = SDS((4, T, FF_SH), bf16)
    return pl.pallas_call(kern, name=name, grid=(4, T // tm), in_specs=[pl.BlockSpec((tm, D), lambda k, i: (i, 0)), w_spec, w_spec],
                          out_specs=[o_spec] * 3, out_shape=[s, s, s],
                          compiler_params=pltpu.CompilerParams(dimension_semantics=("parallel", "parallel")))(n_bf, wg4, wu4)


def ffn_dgu(name, T, tm, d_f, wd4, gate4, up4, l):
    def kern(df_ref, wd_ref, g_ref, u_ref, dg_ref, du_ref):
        dact = _bdot(df_ref[...], wd_ref[...], NT)
        _, vjp = jax.vjp(lambda a, b: _silu(a) * b, g_ref[...].astype(f32), u_ref[...].astype(f32))
        dg, du = vjp(dact)
        dg_ref[...] = dg.astype(bf16)
        du_ref[...] = du.astype(bf16)
    a_spec = pl.BlockSpec((None, tm, FF_SH), lambda k, i: (k, i, 0))
    s = SDS((4, T, FF_SH), bf16)
    return pl.pallas_call(kern, name=name, grid=(4, T // tm),
                          in_specs=[pl.BlockSpec((tm, D), lambda k, i: (i, 0)), pl.BlockSpec((None, FF_SH, D), lambda k, i: (k, l, 0)), a_spec, a_spec],
                          out_specs=[a_spec] * 2, out_shape=[s, s],
                          compiler_params=pltpu.CompilerParams(dimension_semantics=("parallel", "parallel")))(d_f, wd4, gate4, up4)


def rowcall(name, body, T, tm, ins, outs, accs=(), scratch=(), reverse=False):
    n = T // tm
    assert T % tm == 0

    def blk(i):
        return (n - 1 - i) if reverse else i

    in_specs, args = [], []
    for spec in ins:
        kind, arr = spec[0], spec[1]
        if kind == "row":
            _, _, w, cb = spec
            in_specs.append(pl.BlockSpec((tm, w), lambda i, cb=cb: (blk(i), cb)))
        elif kind == "prev":
            _, _, w, cb, h = spec
            r = tm // h
            in_specs.append(pl.BlockSpec((h, w), lambda i, cb=cb, r=r: (jnp.maximum(blk(i) * r - 1, 0), cb)))
        elif kind == "next":
            _, _, w, cb, h = spec
            r = tm // h
            in_specs.append(pl.BlockSpec((h, w), lambda i, cb=cb, r=r, h=h: (jnp.minimum((blk(i) + 1) * r, T // h - 1), cb)))
        else:
            nd = arr.ndim
            in_specs.append(pl.BlockSpec(arr.shape, lambda i, nd=nd: (0,) * nd))
        args.append(arr)
    out_shape = [SDS((T, w), dt) for w, dt in outs] + [SDS(tuple(s), f32) for s in accs]
    out_specs = [pl.BlockSpec((tm, w), lambda i: (blk(i), 0)) for w, _ in outs]
    out_specs += [pl.BlockSpec(tuple(s), lambda i, nd=len(s): (0,) * nd) for s in accs]
    ni, no, na = len(ins), len(outs), len(accs)

    def kern(*refs):
        i = pl.program_id(0)
        in_refs, out_refs = refs[:ni], refs[ni:ni + no]
        acc_refs, scr = refs[ni + no:ni + no + na], refs[ni + no + na:]
        if na:
            @pl.when(i == 0)
            def _():
                for a in acc_refs:
                    a[...] = jnp.zeros(a.shape, f32)
        body(blk(i), n, in_refs, out_refs, acc_refs, scr)

    res = pl.pallas_call(
        kern, name=name, grid=(n,), in_specs=in_specs, out_specs=out_specs, out_shape=out_shape,
        scratch_shapes=list(scratch),
        compiler_params=pltpu.CompilerParams(dimension_semantics=("arbitrary",)),
    )(*args)
    return res


def rms_to_bf16(name, T, tm, x, g):
    def body(i, n, ins, outs, accs, scr):
        outs[0][...] = _rms(ins[0][...], ins[1][...]).astype(bf16)
    return rowcall(name, body, T, tm, [("row", x, D, 0), ("const", g)], [(D, bf16)])[0]


def resid_norm(name, T, tm, h_in, f, g_post, g_pre):
    def body(i, n, ins, outs, accs, scr):
        h = ins[0][...] + _rms(ins[1][...], ins[2][...])
        outs[0][...] = h
        if g_pre is not None:
            outs[1][...] = _rms(h, ins[3][...]).astype(bf16)
    ins = [("row", h_in, D, 0), ("row", f, D, 0), ("const", g_post)] + ([("const", g_pre)] if g_pre is not None else [])
    outs = [(D, f32)] + ([(D, bf16)] if g_pre is not None else [])
    return rowcall(name, body, T, tm, ins, outs)


def swiglu_act(name, T, tm, gate, up):
    def body(i, n, ins, outs, accs, scr):
        outs[0][...] = (_silu(ins[0][...]) * ins[1][...]).astype(bf16)
    return rowcall(name, body, T, tm, [("row", gate, D_FF, 0), ("row", up, D_FF, 0)], [(D_FF, bf16)])[0]


def swiglu_bwd(name, T, tm, gate, up, d_act):
    def body(i, n, ins, outs, accs, scr):
        _, vjp = jax.vjp(lambda a, b: _silu(a) * b, ins[0][...], ins[1][...])
        dg, du = vjp(ins[2][...])
        outs[0][...] = dg.astype(bf16)
        outs[1][...] = du.astype(bf16)
    return rowcall(name, body, T, tm, [("row", gate, D_FF, 0), ("row", up, D_FF, 0), ("row", d_act, D_FF, 0)],
                   [(D_FF, bf16), (D_FF, bf16)])


def final_loss_bwd(name, T, tm, h3, f2, tgt, g_post):
    def body(i, n, ins, outs, accs, scr):
        f, g = ins[1][...], ins[3][...]
        e = ins[0][...] + _rms(f, g) - ins[2][...]
        accs[0][...] += jnp.sum(jnp.sum(e * e, axis=-1, keepdims=True) * (0.5 / D), axis=0, keepdims=True)
        dh = e * (1.0 / D)
        df, dg = _rms_bwd(f, g, dh)
        outs[0][...] = dh
        outs[1][...] = df.astype(bf16)
        accs[1][...] += dg
    return rowcall(name, body, T, tm, [("row", h3, D, 0), ("row", f2, D, 0), ("row", tgt, D, 0), ("const", g_post)],
                   [(D, f32), (D, bf16)], accs=[(1, 1), (1, D)])


def bwd_pre_post(name, T, tm, h_out, f, d_res, d_n, g_pre, g_post, df_dtype):
    def body(i, n, ins, outs, accs, scr):
        dx, dgp = _rms_bwd(ins[0][...], ins[4][...], ins[3][...])
        dh = ins[2][...] + dx
        df, dgq = _rms_bwd(ins[1][...], ins[5][...], dh)
        outs[0][...] = dh
        outs[1][...] = df.astype(df_dtype)
        accs[0][...] += dgp
        accs[1][...] += dgq
    return rowcall(name, body, T, tm,
                   [("row", h_out, D, 0), ("row", f, D, 0), ("row", d_res, D, 0), ("row", d_n, D, 0), ("const", g_pre), ("const", g_post)],
                   [(D, f32), (D, df_dtype)], accs=[(1, D), (1, D)])


def bwd_post(name, T, tm, f, d_h, g_post):
    def body(i, n, ins, outs, accs, scr):
        df, dg = _rms_bwd(ins[0][...], ins[2][...], ins[1][...])
        outs[0][...] = df.astype(bf16)
        accs[0][...] += dg
    return rowcall(name, body, T, tm, [("row", f, D, 0), ("row", d_h, D, 0), ("const", g_post)], [(D, bf16)], accs=[(1, D)])


def bwd_pre(name, T, tm, h, d_res, d_n, g_pre):
    def body(i, n, ins, outs, accs, scr):
        dx, dg = _rms_bwd(ins[0][...], ins[3][...], ins[2][...])
        outs[0][...] = ins[1][...] + dx
        accs[0][...] += dg
    return rowcall(name, body, T, tm, [("row", h, D, 0), ("row", d_res, D, 0), ("row", d_n, D, 0), ("const", g_pre)],
                   [(D, f32)], accs=[(1, D)])


def _layer_norm_parts(x):
    mu = jnp.mean(x, axis=-1, keepdims=True)
    xc = x - mu
    r = lax.rsqrt(jnp.mean(xc * xc, axis=-1, keepdims=True) + EPS)
    return xc * r, r


def gmlp_fwd(name, T, tm, uvz, ln_g, ln_b, wm, bs):
    def body(i, n, ins, outs, accs, scr):
        gu = jax.nn.gelu(ins[0][...])
        xh, _ = _layer_norm_parts(jax.nn.gelu(ins[1][...]))
        vln = (xh * ins[2][...] + ins[3][...]).astype(bf16)
        for c in range(tm // CHUNK):
            rows = slice(c * CHUNK, (c + 1) * CHUNK)
            for h in range(GM_HEADS):
                cols = slice(h * GM_HD, (h + 1) * GM_HD)
                mixed = jnp.dot(ins[4][h], vln[rows, cols], preferred_element_type=f32) + ins[5][h]
                outs[0][rows, cols] = (gu[rows, cols] * mixed).astype(bf16)
    return rowcall(name, body, T, tm, [("row", uvz, D, 0), ("row", uvz, D, 1), ("const", ln_g), ("const", ln_b), ("const", wm), ("const", bs)],
                   [(D, bf16)])[0]


def gmlp_bwd(name, T, tm, uvz, d_ya, ln_g, ln_b, wm, bs):
    def body(i, n, ins, outs, accs, scr):
        u, v, dya = ins[0][...], ins[1][...], ins[2][...]
        gu, gelu_u_vjp = jax.vjp(jax.nn.gelu, u)
        gv, gelu_v_vjp = jax.vjp(jax.nn.gelu, v)
        xh, r = _layer_norm_parts(gv)
        lng = ins[3][...]
        vln = (xh * lng + ins[4][...]).astype(bf16)
        rr = lax.broadcasted_iota(jnp.int32, (CHUNK, CHUNK), 0)
        cc = lax.broadcasted_iota(jnp.int32, (CHUNK, CHUNK), 1)
        causal = (rr >= cc).astype(f32)
        dvln_ref = scr[0]
        dgu_ref = scr[1]
        for c in range(tm // CHUNK):
            rows = slice(c * CHUNK, (c + 1) * CHUNK)
            for h in range(GM_HEADS):
                cols = slice(h * GM_HD, (h + 1) * GM_HD)
                w = ins[5][h]
                blk = vln[rows, cols]
                mixed = jnp.dot(w, blk, preferred_element_type=f32) + ins[6][h]
                dy = dya[rows, cols]
                dgu_ref[rows, cols] = dy * mixed
                dm = dy * gu[rows, cols]
                accs[3][h] += jnp.sum(dm, axis=1, keepdims=True)
                accs[2][h] += _bdot(dm, blk, NT) * causal
                dvln_ref[rows, cols] = _bdot(w, dm, TN)
        dvln = dvln_ref[...]
        accs[0][...] += jnp.sum(dvln * xh, axis=0, keepdims=True)
        accs[1][...] += jnp.sum(dvln, axis=0, keepdims=True)
        dxh = dvln * lng
        dgv = r * (dxh - jnp.mean(dxh, axis=-1, keepdims=True) - xh * jnp.mean(dxh * xh, axis=-1, keepdims=True))
        outs[0][...] = gelu_u_vjp(dgu_ref[...])[0].astype(bf16)
        outs[1][...] = gelu_v_vjp(dgv)[0].astype(bf16)
    return rowcall(name, body, T, tm,
                   [("row", uvz, D, 0), ("row", uvz, D, 1), ("row", d_ya, D, 0), ("const", ln_g), ("const", ln_b), ("const", wm), ("const", bs)],
                   [(D, bf16), (D, bf16)], accs=[(1, D), (1, D), (GM_HEADS, CHUNK, CHUNK), (GM_HEADS, CHUNK, 1)],
                   scratch=[pltpu.VMEM((tm, D), f32), pltpu.VMEM((tm, D), f32)])


def _conv_pre(i, x_ref, halo_ref, w_ref, b_ref, scr, tm):
    scr[pl.ds(0, CONV_HALO), :] = jnp.where(i > 0, halo_ref[...], 0.0)
    scr[pl.ds(CONV_HALO, tm), :] = x_ref[...]
    pre = b_ref[...]
    for k in range(CONV_K):
        pre = pre + w_ref[pl.ds(k, 1), :] * scr[pl.ds(CONV_HALO - (CONV_K - 1) + k, tm), :]
    return pre


def conv_fwd(name, T, tm, xbc, conv_w, conv_b):
    def body(i, n, ins, outs, accs, scr):
        outs[0][...] = _silu(_conv_pre(i, ins[0], ins[1], ins[2], ins[3], scr[0], tm))
    return rowcall(name, body, T, tm, [("row", xbc, CONV_DIM, 0), ("prev", xbc, CONV_DIM, 0, CONV_HALO), ("const", conv_w), ("const", conv_b)],
                   [(CONV_DIM, f32)], scratch=[pltpu.VMEM((tm + CONV_HALO, CONV_DIM), f32)])[0]


def conv_bwd_pre(name, T, tm, xbc, d_xc, conv_w, conv_b):
    def body(i, n, ins, outs, accs, scr):
        pre = _conv_pre(i, ins[0], ins[1], ins[3], ins[4], scr[0], tm)
        _, vjp = jax.vjp(_silu, pre)
        dpre = vjp(ins[2][...])[0]
        outs[0][...] = dpre
        accs[1][...] += jnp.sum(dpre, axis=0, keepdims=True)
        for k in range(CONV_K):
            accs[0][pl.ds(k, 1), :] += jnp.sum(dpre * scr[0][pl.ds(CONV_HALO - (CONV_K - 1) + k, tm), :], axis=0, keepdims=True)
    return rowcall(name, body, T, tm,
                   [("row", xbc, CONV_DIM, 0), ("prev", xbc, CONV_DIM, 0, CONV_HALO), ("row", d_xc, CONV_DIM, 0), ("const", conv_w), ("const", conv_b)],
                   [(CONV_DIM, f32)], accs=[(CONV_K, CONV_DIM), (1, CONV_DIM)], scratch=[pltpu.VMEM((tm + CONV_HALO, CONV_DIM), f32)])


def conv_bwd_x(name, T, tm, d_pre, conv_w):
    def body(i, n, ins, outs, accs, scr):
        s = scr[0]
        s[pl.ds(0, tm), :] = ins[0][...]
        s[pl.ds(tm, CONV_HALO), :] = jnp.where(i < n - 1, ins[1][...], 0.0)
        dx = jnp.zeros((tm, CONV_DIM), f32)
        for k in range(CONV_K):
            dx = dx + ins[2][pl.ds(k, 1), :] * s[pl.ds(CONV_K - 1 - k, tm), :]
        outs[0][...] = dx.astype(bf16)
    return rowcall(name, body, T, tm, [("row", d_pre, CONV_DIM, 0), ("next", d_pre, CONV_DIM, 0, CONV_HALO), ("const", conv_w)],
                   [(CONV_DIM, bf16)], scratch=[pltpu.VMEM((tm + CONV_HALO, CONV_DIM), f32)])[0]


def _ssd_chunk(X4, dtr, B4, C4, S4, dtb, alog, dsk):
    L = CHUNK
    rr = lax.broadcasted_iota(jnp.int32, (L, L), 0)
    cc = lax.broadcasted_iota(jnp.int32, (L, L), 1)
    tril = rr >= cc
    lane = lax.broadcasted_iota(jnp.int32, (1, DT_PAD), 1)
    sub = lax.broadcasted_iota(jnp.int32, (DT_PAD, 1), 0)
    glane = lax.broadcasted_iota(jnp.int32, (1, SSM_HPG * SSM_P), 1) // SSM_P
    dt = _softplus(dtr + dtb)
    a = -jnp.exp(alog)
    dA = dt * a
    acum = jnp.dot(tril.astype(f32), dA, precision=HI, preferred_element_type=f32)
    acumT = acum.T
    tot = jnp.sum(dA, axis=0, keepdims=True)
    ys, Sn = [], []
    for g in range(SSM_GROUPS):
        hm = [(glane == r).astype(f32) for r in range(SSM_HPG)]
        cols = [jnp.sum(acum * (lane == SSM_HPG * g + r).astype(f32), axis=1, keepdims=True) for r in range(SSM_HPG)]
        dtc = [jnp.sum(dt * (lane == SSM_HPG * g + r).astype(f32), axis=1, keepdims=True) for r in range(SSM_HPG)]
        tots = [jnp.sum(tot * (lane == SSM_HPG * g + r).astype(f32), axis=1, keepdims=True) for r in range(SSM_HPG)]
        dsc = [jnp.sum(dsk * (lane == SSM_HPG * g + r).astype(f32), axis=1, keepdims=True) for r in range(SSM_HPG)]
        x = X4[g]
        xdt = x * sum(dtc[r] * hm[r] for r in range(SSM_HPG))
        cb = _bdot(C4[g], B4[g], NT)
        y = x * sum(dsc[r] * hm[r] for r in range(SSM_HPG))
        for r in range(SSM_HPG):
            row = jnp.sum(acumT * (sub == SSM_HPG * g + r).astype(f32), axis=0, keepdims=True)
            dec = jnp.exp(jnp.where(tril, cols[r] - row, -jnp.inf))
            y = y + _bdot(cb * dec, xdt * hm[r])
        y = y + _bdot(C4[g], S4[g]) * sum(jnp.exp(cols[r]) * hm[r] for r in range(SSM_HPG))
        dte = sum(jnp.exp(tots[r] - cols[r]) * hm[r] for r in range(SSM_HPG))
        s_new = S4[g] * sum(jnp.exp(tots[r]) * hm[r] for r in range(SSM_HPG)) + _bdot(B4[g], xdt * dte, TN)
        ys.append(y)
        Sn.append(s_new)
    return tuple(ys), tuple(Sn)


def _ssd_ins(xc, dtr):
    gw = SSM_HPG * SSM_P
    ins = [("row", xc, gw, g) for g in range(SSM_GROUPS)]
    ins += [("row", xc, SSM_N, D // SSM_N + g) for g in range(SSM_GROUPS)]
    ins += [("row", xc, SSM_N, D // SSM_N + SSM_GROUPS + g) for g in range(SSM_GROUPS)]
    ins += [("row", dtr, DT_PAD, 0)]
    return ins


def ssd_fwd(name, T, xc, dtr, dtb, alog, dsk):
    gw = SSM_HPG * SSM_P

    def body(i, n, ins, outs, accs, scr):
        S = scr[0]

        @pl.when(i == 0)
        def _():
            S[...] = jnp.zeros(S.shape, f32)
        X4 = tuple(ins[g][...] for g in range(4))
        B4 = tuple(ins[4 + g][...] for g in range(4))
        C4 = tuple(ins[8 + g][...] for g in range(4))
        S4 = tuple(S[:, g * gw:(g + 1) * gw] for g in range(4))
        outs[1][...] = S[...]
        ys, Sn = _ssd_chunk(X4, ins[12][...], B4, C4, S4, ins[13][...], ins[14][...], ins[15][...])
        for g in range(4):
            outs[0][:, g * gw:(g + 1) * gw] = ys[g]
            S[:, g * gw:(g + 1) * gw] = Sn[g]
    ins = _ssd_ins(xc, dtr) + [("const", dtb), ("const", alog), ("const", dsk)]
    return rowcall(name, body, T, CHUNK, ins, [(D, f32), (D, f32)], scratch=[pltpu.VMEM((SSM_N, D), f32)])


def ssd_bwd(name, T, xc, dtr, sprev, d_y, dtb, alog, dsk):
    gw = SSM_HPG * SSM_P

    def body(i, n, ins, outs, accs, scr):
        dS = scr[0]

        @pl.when(i == n - 1)
        def _():
            dS[...] = jnp.zeros(dS.shape, f32)
        X4 = tuple(ins[g][...] for g in range(4))
        B4 = tuple(ins[4 + g][...] for g in range(4))
        C4 = tuple(ins[8 + g][...] for g in range(4))
        S4 = tuple(ins[13 + g][...] for g in range(4))
        dY4 = tuple(ins[17 + g][...] for g in range(4))
        dS4 = tuple(dS[:, g * gw:(g + 1) * gw] for g in range(4))
        _, vjp = jax.vjp(_ssd_chunk, X4, ins[12][...], B4, C4, S4, ins[21][...], ins[22][...], ins[23][...])
        dX4, ddtr, dB4, dC4, dSp, ddtb, dalog, ddsk = vjp((dY4, dS4))
        for g in range(4):
            outs[0][:, g * gw:(g + 1) * gw] = dX4[g]
            outs[0][:, D + g * SSM_N:D + (g + 1) * SSM_N] = dB4[g]
            outs[0][:, D + (SSM_GROUPS + g) * SSM_N:D + (SSM_GROUPS + g + 1) * SSM_N] = dC4[g]
            dS[:, g * gw:(g + 1) * gw] = dSp[g]
        outs[1][...] = ddtr.astype(bf16)
        accs[0][...] += ddtb
        accs[1][...] += dalog
        accs[2][...] += ddsk
    ins = _ssd_ins(xc, dtr) + [("row", sprev, gw, g) for g in range(4)] + [("row", d_y, gw, g) for g in range(4)]
    ins += [("const", dtb), ("const", alog), ("const", dsk)]
    return rowcall(name, body, T, CHUNK, ins, [(CONV_DIM, f32), (DT_PAD, bf16)], accs=[(1, DT_PAD)] * 3,
                   scratch=[pltpu.VMEM((SSM_N, D), f32)], reverse=True)


def _gate_group(y, z, g):
    return _rms(y * _silu(z), g)


def gate_fwd(name, T, tm, y, uvz, gn):
    def body(i, n, ins, outs, accs, scr):
        for g in range(SSM_GROUPS):
            cols = slice(g * 256, (g + 1) * 256)
            outs[0][:, cols] = _gate_group(ins[0][:, cols], ins[1][:, cols], ins[2][:, cols]).astype(bf16)
    return rowcall(name, body, T, tm, [("row", y, D, 0), ("row", uvz, D, 2), ("const", gn)], [(D, bf16)])[0]


def gate_bwd(name, T, tm, y, uvz, d_yb, gn):
    def body(i, n, ins, outs, accs, scr):
        for g in range(SSM_GROUPS):
            cols = slice(g * 256, (g + 1) * 256)
            _, vjp = jax.vjp(_gate_group, ins[0][:, cols], ins[1][:, cols], ins[3][:, cols])
            dy, dz, dg = vjp(ins[2][:, cols])
            outs[0][:, cols] = dy
            outs[1][:, cols] = dz.astype(bf16)
            accs[0][:, cols] += dg
    return rowcall(name, body, T, tm, [("row", y, D, 0), ("row", uvz, D, 2), ("row", d_yb, D, 0), ("const", gn)],
                   [(D, f32), (D, bf16)], accs=[(1, D)])


def _pool_diff(i, tm, h_ref, halo_ref, g_ref, scr):
    g = g_ref[...]
    yn = _rms(h_ref[...], g)
    scr[pl.ds(0, POOL_HALO), :] = jnp.where(i > 0, _rms(halo_ref[...], g), 0.0)
    scr[pl.ds(POOL_HALO, tm), :] = yn
    pos = (i * tm + lax.broadcasted_iota(jnp.int32, (tm, 1), 0) + 1).astype(f32)
    parts = []
    for gi, win in enumerate(POOL_WINDOWS):
        cols = slice(gi * POOL_GD, (gi + 1) * POOL_GD)
        s = scr[pl.ds(POOL_HALO, tm), cols]
        for j in range(1, win):
            s = s + scr[pl.ds(POOL_HALO - j, tm), cols]
        parts.append(s / jnp.minimum(pos, float(win)) - yn[:, cols])
    return parts


def pool_fwd(name, T, tm, h2, g_pre, pw, pb, psc):
    def body(i, n, ins, outs, accs, scr):
        parts = _pool_diff(i, tm, ins[0], ins[1], ins[2], scr[0])
        for gi in range(len(POOL_WINDOWS)):
            cols = slice(gi * POOL_GD, (gi + 1) * POOL_GD)
            o = _bdot(parts[gi], ins[3][gi]) + ins[4][:, cols]
            outs[0][:, cols] = o * ins[5][:, cols]
    return rowcall(name, body, T, tm, [("row", h2, D, 0), ("prev", h2, D, 0, POOL_HALO), ("const", g_pre), ("const", pw), ("const", pb), ("const", psc)],
                   [(D, f32)], scratch=[pltpu.VMEM((tm + POOL_HALO, D), f32)])[0]


def pool_bwd(name, T, tm, h2, d_pm, d_res, g_pre, pw, pb, psc):
    def body(i, n, ins, outs, accs, scr):
        parts = _pool_diff(i, tm, ins[0], ins[1], ins[5], scr[0])
        dpm = ins[2][...]
        psc_v = ins[8][...]
        dps = dpm * psc_v
        dps_halo = jnp.where(i < n - 1, ins[3][...] * psc_v, 0.0)
        accs[1][...] += jnp.sum(dps, axis=0, keepdims=True)
        pos = (i * tm + lax.broadcasted_iota(jnp.int32, (tm, 1), 0) + 1).astype(f32)
        pos_h = ((i + 1) * tm + lax.broadcasted_iota(jnp.int32, (POOL_HALO, 1), 0) + 1).astype(f32)
        r_scr = scr[1]
        dyn_scr = scr[2]
        for gi, win in enumerate(POOL_WINDOWS):
            cols = slice(gi * POOL_GD, (gi + 1) * POOL_GD)
            w = ins[6][gi]
            o = _bdot(parts[gi], w) + ins[7][:, cols]
            accs[2][:, cols] += jnp.sum(dpm[:, cols] * o, axis=0, keepdims=True)
            accs[0][gi] += _bdot(parts[gi], dps[:, cols], TN)
            q = _bdot(dps[:, cols], w, NT)
            qh = _bdot(dps_halo[:, cols], w, NT)
            r_scr[pl.ds(0, tm), cols] = q / jnp.minimum(pos, float(win))
            r_scr[pl.ds(tm, POOL_HALO), cols] = qh / jnp.minimum(pos_h, float(win))
            s = r_scr[pl.ds(0, tm), cols]
            for j in range(1, win):
                s = s + r_scr[pl.ds(j, tm), cols]
            dyn_scr[:, cols] = s - q
        dx, dg = _rms_bwd(ins[0][...], ins[5][...], dyn_scr[...])
        outs[0][...] = ins[4][...] + dx
        accs[3][...] += dg
    ins = [("row", h2, D, 0), ("prev", h2, D, 0, POOL_HALO), ("row", d_pm, D, 0), ("next", d_pm, D, 0, POOL_HALO), ("row", d_res, D, 0),
           ("const", g_pre), ("const", pw), ("const", pb), ("const", psc)]
    return rowcall(name, body, T, tm, ins, [(D, f32)], accs=[(4, POOL_GD, POOL_GD), (1, D), (1, D), (1, D)],
                   scratch=[pltpu.VMEM((tm + POOL_HALO, D), f32), pltpu.VMEM((tm + POOL_HALO, D), f32), pltpu.VMEM((tm, D), f32)])


def local_step(T, x, tgt, W, ffn_weights):
    tm = 512 if T >= 1024 else T // 2
    TKW = 1024 if T >= 1024 else T
    ng = W["norm_g"]
    g = lambda l, j: ng[l, j][None, :]
    G = {}

    row_spec = pl.BlockSpec((tm, D), lambda j, i, k: (i, 0))
    sh_spec = [pl.BlockSpec((None, tm, FF_SH), lambda j, i, k, s=s: (s, i, 0)) for s in range(4)]

    def ffn_fwd(tag, n_bf, l):
        gate4, up4, act4 = ffn_up(f"ffn{tag}_up", T, tm, n_bf, W["wg4"], W["wu4"], l)
        wd_spec = [pl.BlockSpec((None, FF_SH, D), lambda j, i, k, s=s: (s, l, 0)) for s in range(4)]
        f = mm(f"ffn{tag}_down", (1, T // tm, 1), [(act4, sh_spec[s], W["wd4"], wd_spec[s]) for s in range(4)], NN, row_spec, SDS((T, D), f32))
        return gate4, up4, act4, f

    def ffn_bwd(tag, l, n_bf, gate4, up4, act4, d_f):
        d_gate4, d_up4 = ffn_dgu(f"ffn{tag}_dgu", T, tm, d_f, W["wd4"], gate4, up4, l)
        w_spec = [pl.BlockSpec((None, D, FF_SH), lambda j, i, k, s=s: (s, l, 0)) for s in range(4)]
        d_n = mm(f"ffn{tag}_dn", (1, T // tm, 1), [(d_gate4, sh_spec[s], W["wg4"], w_spec[s]) for s in range(4)]
                 + [(d_up4, sh_spec[s], W["wu4"], w_spec[s]) for s in range(4)], NT, row_spec, SDS((T, D), f32))

        def wgrad(nm, a4, b):
            return mm(nm, (4, 1, T // TKW),
                      [(a4, pl.BlockSpec((None, TKW, FF_SH), lambda s, j, k: (s, k, 0)), b, pl.BlockSpec((TKW, D), lambda s, j, k: (k, 0)))],
                      TN, pl.BlockSpec((None, FF_SH, D), lambda s, j, k: (s, 0, 0)), SDS((4, FF_SH, D), f32))
        return d_n, wgrad(f"ffn{tag}_dwg", d_gate4, n_bf), wgrad(f"ffn{tag}_dwu", d_up4, n_bf), wgrad(f"ffn{tag}_dwd", act4, d_f)

    y0 = rms_to_bf16("l0_prenorm", T, tm, x, g(0, 0))
    uvz = matmul("in_uvz", [(y0, W["w_uvz"])], "nn", f32, tm, 1024)
    xbc = matmul("in_xbc", [(y0, W["w_xbc"])], "nn", f32, tm, 1024)
    dtr = matmul("in_dt", [(y0, W["w_dt"])], "nn", f32, tm, DT_PAD)
    y_a = gmlp_fwd("gmlp_fwd", T, tm, uvz, W["ln_g"], W["ln_b"], W["wm"], W["bs"])
    xc = conv_fwd("conv_fwd", T, tm, xbc, W["conv_w"], W["conv_b"])
    y_ssd, sprev = ssd_fwd("ssd_fwd", T, xc, dtr, W["dtb"], W["alog"], W["dsk"])
    y_b = gate_fwd("gate_fwd", T, tm, y_ssd, uvz, W["gn"])
    half = D // 2
    wo4 = W["wo4"]
    ycol = [pl.BlockSpec((tm, half), lambda j, i, k, cb=cb: (i, cb)) for cb in range(2)]
    wo_s = [pl.BlockSpec((None, half, D), lambda j, i, k, s=s: (s, 0, 0)) for s in range(4)]
    mixo = mm("out_proj", (1, T // tm, 1), [(y_a, ycol[0], wo4, wo_s[0]), (y_a, ycol[1], wo4, wo_s[1]),
                                            (y_b, ycol[0], wo4, wo_s[2]), (y_b, ycol[1], wo4, wo_s[3])], NN, row_spec, SDS((T, D), f32))
    h1, n1 = resid_norm("l0_mix_resid", T, tm, x, mixo, g(0, 1), g(0, 2))
    W = dict(W)
    W["wg4"], W["wu4"], W["wd4"] = ffn_weights(h1)
    gate0, up0, act0, f1 = ffn_fwd("0", n1, 0)
    (h2,) = resid_norm("l0_ffn_resid", T, tm, h1, f1, g(0, 3), None)
    pm = pool_fwd("pool_fwd", T, tm, h2, g(1, 0), W["pool_w"], W["pool_b"], W["pool_scale"])
    h3, n3 = resid_norm("l1_mix_resid", T, tm, h2, pm, g(1, 1), g(1, 2))
    gate1, up1, act1, f2 = ffn_fwd("1", n3, 1)
    dh4, d_f2, loss_acc, dg13 = final_loss_bwd("loss_bwd", T, tm, h3, f2, tgt, g(1, 3))
    d_n3, dwg1, dwu1, dwd1 = ffn_bwd("1", 1, n3, gate1, up1, act1, d_f2)
    d_h3, d_pm, dg12, dg11 = bwd_pre_post("l1_mix_bwd", T, tm, h3, pm, dh4, d_n3, g(1, 2), g(1, 1), f32)
    d_h2, G["pool_w"], G["pool_b"], G["pool_scale"], dg10 = pool_bwd("pool_bwd", T, tm, h2, d_pm, d_h3, g(1, 0), W["pool_w"], W["pool_b"], W["pool_scale"])
    d_f1, dg03 = bwd_post("l0_ffn_bwd", T, tm, f1, d_h2, g(0, 3))
    d_n1, dwg0, dwu0, dwd0 = ffn_bwd("0", 0, n1, gate0, up0, act0, d_f1)
    d_h1, d_mixo, dg02, dg01 = bwd_pre_post("l0_mix_bwd", T, tm, h1, mixo, d_h2, d_n1, g(0, 2), g(0, 1), bf16)
    def d_ycat(nm, s0):
        return mm(nm, (2, T // tm, 1), [(d_mixo, row_spec, wo4, pl.BlockSpec((None, half, D), lambda j, i, k: (s0 + j, 0, 0)))], NT,
                  pl.BlockSpec((tm, half), lambda j, i, k: (i, j)), SDS((T, D), f32))

    def d_wo(nm, y):
        return mm(nm, (2, 1, T // TKW), [(y, pl.BlockSpec((TKW, half), lambda s, j, k: (k, s)), d_mixo, pl.BlockSpec((TKW, D), lambda s, j, k: (k, 0)))],
                  TN, pl.BlockSpec((None, half, D), lambda s, j, k: (s, 0, 0)), SDS((2, half, D), f32))
    d_ya, d_yb = d_ycat("out_proj_dya", 0), d_ycat("out_proj_dyb", 2)
    dwo_a, dwo_b = d_wo("out_proj_dwa", y_a), d_wo("out_proj_dwb", y_b)
    d_yssd, d_z, G["gn"] = gate_bwd("gate_bwd", T, tm, y_ssd, uvz, d_yb, W["gn"])
    d_xc, d_dtr, G["dtb"], G["alog"], G["dsk"] = ssd_bwd("ssd_bwd", T, xc, dtr, sprev, d_yssd, W["dtb"], W["alog"], W["dsk"])
    d_pre, G["conv_w"], G["conv_b"] = conv_bwd_pre("conv_bwd_pre", T, tm, xbc, d_xc, W["conv_w"], W["conv_b"])
    d_xbc = conv_bwd_x("conv_bwd_x", T, tm, d_pre, W["conv_w"])
    d_u, d_v, G["ln_g"], G["ln_b"], G["wm"], G["bs"] = gmlp_bwd("gmlp_bwd", T, tm, uvz, d_ya, W["ln_g"], W["ln_b"], W["wm"], W["bs"])
    w_u, w_v, w_z = W["w_uvz"][:, :D], W["w_uvz"][:, D:2 * D], W["w_uvz"][:, 2 * D:]
    d_y0 = matmul("in_dy0", [(d_u, w_u), (d_v, w_v), (d_z, w_z), (d_xbc, W["w_xbc"]), (d_dtr, W["w_dt"])], "nt", f32, tm, 1024)
    G["w_inT"] = [matmul("in_dwu", [(d_u, y0)], "tn", f32, 1024, 1024, TKW), matmul("in_dwv", [(d_v, y0)], "tn", f32, 1024, 1024, TKW),
                  matmul("in_dwz", [(d_z, y0)], "tn", f32, 1024, 1024, TKW), matmul("in_dwxbc", [(d_xbc, y0)], "tn", f32, 1024, 1024, TKW),
                  matmul("in_dwdt", [(d_dtr, y0)], "tn", f32, DT_PAD, 1024, TKW)[:N_HEADS]]
    grad_x, dg00 = bwd_pre("l0_pre_bwd", T, tm, x, d_h1, d_y0, g(0, 0))
    G["norm_g"] = jnp.stack([jnp.concatenate([dg00, dg01, dg02, dg03], 0), jnp.concatenate([dg10, dg11, dg12, dg13], 0)])
    G["wo4"] = [dwo_a[0], dwo_a[1], dwo_b[0], dwo_b[1]]
    G["wgT4"], G["wuT4"], G["wd4"] = [dwg0, dwg1], [dwu0, dwu1], [dwd0, dwd1]
    return loss_acc, grad_x, G


def build_weights(Wf):
    causal = jnp.tril(jnp.ones((CHUNK, CHUNK), bool))
    w_in = Wf["w_in"].astype(bf16)
    pad16 = lambda v: jnp.pad(v.reshape(1, N_HEADS).astype(f32), ((0, 0), (0, DT_PAD - N_HEADS)))
    return {
        "norm_g": Wf["norm_g"],
        "w_uvz": w_in[:, :3 * D], "w_xbc": w_in[:, 3 * D:3 * D + CONV_DIM],
        "w_dt": jnp.pad(w_in[:, 3 * D + CONV_DIM:], ((0, 0), (0, DT_PAD - N_HEADS))),
        "ln_g": Wf["gm_ln_g"].reshape(1, D), "ln_b": Wf["gm_ln_b"].reshape(1, D),
        "wm": jnp.where(causal[None], Wf["gm_ws"], 0).astype(bf16), "bs": Wf["gm_bs"].reshape(GM_HEADS, CHUNK, 1),
        "conv_w": Wf["conv_w"], "conv_b": Wf["conv_b"].reshape(1, CONV_DIM),
        "dtb": pad16(Wf["dt_bias"]), "alog": pad16(Wf["a_log"]), "dsk": pad16(Wf["d_skip"]),
        "gn": Wf["ssm_norm_g"].reshape(1, D),
        "wo4": Wf["wo4"].astype(bf16),
        "pool_w": Wf["pool_w"].astype(bf16), "pool_b": Wf["pool_b"].reshape(1, D), "pool_scale": Wf["pool_scale"].reshape(1, D),
    }


def small_grads(G):
    return {
        "norm_g": G["norm_g"],
        "gm_ln_g": G["ln_g"].reshape(D), "gm_ln_b": G["ln_b"].reshape(D),
        "gm_ws": G["wm"], "gm_bs": G["bs"].reshape(GM_HEADS, CHUNK),
        "conv_w": G["conv_w"], "conv_b": G["conv_b"].reshape(CONV_DIM),
        "dt_bias": G["dtb"][0, :N_HEADS], "a_log": G["alog"][0, :N_HEADS], "d_skip": G["dsk"][0, :N_HEADS],
        "ssm_norm_g": G["gn"].reshape(D),
        "pool_b": G["pool_b"].reshape(4, POOL_GD), "pool_scale": G["pool_scale"].reshape(D),
    }


MESH_ID = pl.DeviceIdType.MESH
ANY = pl.BlockSpec(memory_space=pl.ANY)


DMA_CHUNK_BYTES = 2 << 20
DMA_MAX_CHUNKS = 32


def _pieces(view, axis, align):
    shape = view.shape
    nbytes = math.prod(shape) * jnp.dtype(view.dtype).itemsize
    n = max(1, min(DMA_MAX_CHUNKS, -(-nbytes // DMA_CHUNK_BYTES)))
    rows = shape[axis]
    size = -(-rows // n)
    size = -(-size // align) * align
    out = []
    for s in range(0, rows, size):
        idx = [slice(None)] * len(shape)
        idx[axis] = pl.ds(s, min(size, rows - s))
        out.append(tuple(idx))
    return out


def comm_call(name, operands, out_shapes, plan):
    n_in = len(operands)
    n_out = len(out_shapes)
    n_remote, n_local = plan((0, 0, 0), [None] * n_in, [None] * n_out, True)

    def body(*refs):
        in_refs, out_refs = refs[:n_in], refs[n_in:n_in + n_out]
        send_sems, recv_sems, local_sems = refs[n_in + n_out:]
        me = (lax.axis_index("x"), lax.axis_index("y"), lax.axis_index("c"))
        remote, local = plan(me, in_refs, out_refs, False)
        align = lambda v: 16 if v.dtype == bf16 else 8
        for j, (s, d, axis) in enumerate(local):
            for ix in _pieces(s, axis, align(s)):
                pltpu.make_async_copy(s.at[ix], d.at[ix], local_sems.at[j]).start()
        peers = [tuple((1 - m) if f else m for m, f in zip(me, flip)) for flip, *_ in remote]
        for k, (flip, src, dst, _, axis) in enumerate(remote):
            for ix in _pieces(src, axis, align(src)):
                pltpu.make_async_remote_copy(src_ref=src.at[ix], dst_ref=dst.at[ix], send_sem=send_sems.at[k], recv_sem=recv_sems.at[k],
                                             device_id=peers[k], device_id_type=MESH_ID).start()
        for k, (flip, src, dst, landing, axis) in enumerate(remote):
            pltpu.make_async_remote_copy(src_ref=landing, dst_ref=landing, send_sem=send_sems.at[k], recv_sem=recv_sems.at[k],
                                         device_id=peers[k], device_id_type=MESH_ID).wait_recv()
        for k, (flip, src, dst, landing, axis) in enumerate(remote):
            pltpu.make_async_remote_copy(src_ref=src, dst_ref=dst, send_sem=send_sems.at[k], recv_sem=recv_sems.at[k],
                                         device_id=peers[k], device_id_type=MESH_ID).wait_send()
        for j, (s, d, axis) in enumerate(local):
            pltpu.make_async_copy(s, d, local_sems.at[j]).wait()

    return pl.pallas_call(
        body, name=name, out_shape=list(out_shapes), in_specs=[ANY] * n_in, out_specs=[ANY] * n_out,
        scratch_shapes=[pltpu.SemaphoreType.DMA((n_remote,)), pltpu.SemaphoreType.DMA((n_remote,)), pltpu.SemaphoreType.DMA((max(n_local, 1),))],
    )(*operands)


CHIP_FLIPS = ((1, 0, 0), (0, 1, 0), (1, 1, 0))
PAIR_FLIP = (0, 0, 1)


def gather_over_chips(name, arrs):
    def plan(me, ins, outs, count):
        if count:
            return len(CHIP_FLIPS) * len(arrs), len(arrs)
        k = 2 * me[0] + me[1]
        remote, local = [], []
        for a in range(len(arrs)):
            for flip in CHIP_FLIPS:
                kp = 2 * ((1 - me[0]) if flip[0] else me[0]) + ((1 - me[1]) if flip[1] else me[1])
                remote.append((flip, ins[a], outs[a].at[k], outs[a].at[kp], 0))
            local.append((ins[a], outs[a].at[k], 0))
        return remote, local
    return comm_call(name, arrs, [SDS((4,) + a.shape, a.dtype) for a in arrs], plan)


def pair_split_exchange(name, p, rh):
    def plan(me, ins, outs, count):
        if count:
            return 1, 0
        theirs = ins[0].at[:, pl.ds(pl.multiple_of((1 - me[2]) * rh, 8), rh), :]
        return [(PAIR_FLIP, theirs, outs[0], outs[0], 1)], []
    return comm_call(name, [p], [SDS((4, rh, p.shape[2]), p.dtype)], plan)[0]


def scatter_over_chips(name, cs):
    def plan(me, ins, outs, count):
        if count:
            return len(CHIP_FLIPS), 0
        k = 2 * me[0] + me[1]
        remote = []
        for flip in CHIP_FLIPS:
            kp = 2 * ((1 - me[0]) if flip[0] else me[0]) + ((1 - me[1]) if flip[1] else me[1])
            remote.append((flip, ins[0].at[kp], outs[0].at[k], outs[0].at[kp], 0))
        return remote, []
    return comm_call(name, [cs], [SDS(cs.shape, cs.dtype)], plan)[0]


def pair_swap(name, half):
    def plan(me, ins, outs, count):
        if count:
            return 1, 0
        return [(PAIR_FLIP, ins[0], outs[0], outs[0], 0)], []
    return comm_call(name, [half], [SDS(half.shape, half.dtype)], plan)[0]


def _row_tile(rows, cap=512):
    if rows <= cap:
        return rows
    t = cap - cap % 8
    while rows % t:
        t -= 8
    return t


SUM_ROWS = 448


def pair_sum(name, packs, got, c_arr):
    rh = got.shape[1]
    nb = rh // SUM_ROWS

    def kern(c_ref, a_ref, b_ref, o16_ref):
        o16_ref[...] = (a_ref[...] + b_ref[...]).astype(bf16)
    blk = (None, SUM_ROWS, D)
    grid_spec = pltpu.PrefetchScalarGridSpec(
        num_scalar_prefetch=1, grid=(4, nb),
        in_specs=[pl.BlockSpec(blk, lambda s, i, c: (s, c[0] * nb + i, 0)), pl.BlockSpec(blk, lambda s, i, c: (s, i, 0))],
        out_specs=pl.BlockSpec(blk, lambda s, i, c: (s, i, 0)))
    return pl.pallas_call(kern, name=name, grid_spec=grid_spec, out_shape=SDS(got.shape, bf16),
                          compiler_params=pltpu.CompilerParams(dimension_semantics=("parallel", "parallel")))(c_arr, packs, got)


def chip_sum(name, own16, landed16, k_arr):
    rh = own16.shape[1]
    nb = rh // SUM_ROWS

    def kern(k_ref, own_ref, l0, l1, l2, l3, o_ref):
        k = k_ref[0]
        s = None
        for j, lref in enumerate((l0, l1, l2, l3)):
            t = jnp.where(k == j, own_ref[...], lref[...]).astype(f32)
            s = t if s is None else s + t
        o_ref[...] = s
    blk = (None, SUM_ROWS, D)
    land = [pl.BlockSpec(blk, lambda i, k, j=j: (jnp.where(k[0] == j, (j + 1) % N_CHIPS, j), i, 0)) for j in range(N_CHIPS)]
    grid_spec = pltpu.PrefetchScalarGridSpec(
        num_scalar_prefetch=1, grid=(nb,),
        in_specs=[pl.BlockSpec(blk, lambda i, k: (k[0], i, 0))] + land,
        out_specs=pl.BlockSpec((SUM_ROWS, D), lambda i, k: (i, 0)))
    return pl.pallas_call(kern, name=name, grid_spec=grid_spec, out_shape=SDS((rh, D), f32),
                          compiler_params=pltpu.CompilerParams(dimension_semantics=("parallel",)))(k_arr, own16, landed16, landed16, landed16, landed16)


def adamw(name, w, g, m, v):
    R, C = w.shape
    tr = _row_tile(R, 256)

    def kern(w_ref, g_ref, m_ref, v_ref, d_ref, mo_ref, vo_ref):
        gg = g_ref[...]
        mn = ADAM_B1 * m_ref[...] + (1.0 - ADAM_B1) * gg
        vn = ADAM_B2 * v_ref[...] + (1.0 - ADAM_B2) * jnp.square(gg)
        m_hat = mn / (1.0 - ADAM_B1 ** ADAM_STEP)
        v_hat = vn / (1.0 - ADAM_B2 ** ADAM_STEP)
        d_ref[...] = -ADAM_LR * (m_hat / (jnp.sqrt(v_hat) + ADAM_EPS) + ADAM_WD * w_ref[...])
        mo_ref[...] = mn
        vo_ref[...] = vn
    spec = pl.BlockSpec((tr, C), lambda i: (i, 0))
    s = SDS((R, C), f32)
    return pl.pallas_call(kern, name=name, grid=(R // tr,), in_specs=[spec] * 4, out_specs=[spec] * 3, out_shape=[s, s, s],
                          compiler_params=pltpu.CompilerParams(dimension_semantics=("parallel",)))(w, g, m, v)


WEIGHT_NAMES = ("norm_g", "w_in", "gm_ln_g", "gm_ln_b", "gm_ws", "gm_bs", "conv_w", "conv_b", "dt_bias", "a_log", "d_skip",
                "ssm_norm_g", "w_out", "pool_w", "pool_b", "pool_scale", "ffn_w_gate", "ffn_w_up", "ffn_w_down")
SMALL = ("norm_g", "conv_w", "pool_b", "pool_scale")
REPL = ("gm_ln_g", "gm_ln_b", "gm_ws", "gm_bs", "conv_b", "dt_bias", "a_log", "d_skip", "ssm_norm_g")
SMALL_AXIS = {"norm_g": 2, "conv_w": 1, "pool_b": 1, "pool_scale": 0}
N_CHIPS = 4
IN_SH = IN_DIM // N_CHIPS
SMALL_ROWS = 8
REPL_ROWS = 72
OFF_OUT, OFF_GATE, OFF_UP, OFF_DOWN = 0, 512, 512 + 2 * FF_SH, 512 + 4 * FF_SH
OFF_POOL = OFF_DOWN + 2 * FF_SH
OFF_SMALL = OFF_POOL + 64
OFF_REPL = OFF_SMALL + SMALL_ROWS
OFF_IN = OFF_REPL + REPL_ROWS
SLOT_END = OFF_IN + IN_SH
SLOT_ROWS = 6272
HALF_ROWS = SLOT_ROWS // 2


def _flat_rows(pieces, rows):
    v = jnp.concatenate([p.reshape(-1) for p in pieces])
    return jnp.pad(v, (0, rows * D - v.shape[0])).reshape(rows, D)


def _shard_small(name, full, k):
    ax = SMALL_AXIS[name]
    n = full.shape[ax] // N_CHIPS
    return lax.slice_in_dim(full, k * n, (k + 1) * n, axis=ax)


def _drop1(name, a):
    return a if name == "norm_g" else a[0]


def _row_range(blocks, lo, hi):
    out, off = [], 0
    for b in blocks:
        n = b.shape[0]
        a, e = max(lo, off), min(hi, off + n)
        if a < e:
            out.append(b[a - off:e - off])
        off += n
    return out


HBM_SPEC = pl.BlockSpec(memory_space=pltpu.HBM)
SEM_SPEC = pl.BlockSpec(memory_space=pltpu.SEMAPHORE)
SPLIT_EFFECT = pltpu.SideEffectType.DATAFLOW_SIDE_EFFECTING


def _chip_of(me, flip):
    return 2 * ((1 - me[0]) if flip[0] else me[0]) + ((1 - me[1]) if flip[1] else me[1])


def gather_start(name, arrs):
    n = len(arrs)
    ncp = n * len(CHIP_FLIPS)

    def body(*refs):
        srcs, lands = refs[:n], refs[n:2 * n]
        send_sems, recv_sems, token = refs[2 * n], refs[2 * n + 1], refs[-1]
        me = (lax.axis_index("x"), lax.axis_index("y"), lax.axis_index("c"))
        k = 2 * me[0] + me[1]
        for a in range(n):
            for f, flip in enumerate(CHIP_FLIPS):
                peer = tuple((1 - m) if fl else m for m, fl in zip(me, flip))
                for ix in _pieces(srcs[a], 0, 16):
                    pltpu.make_async_remote_copy(src_ref=srcs[a].at[ix], dst_ref=lands[a].at[k].at[ix],
                                                 send_sem=send_sems.at[a * len(CHIP_FLIPS) + f], recv_sem=recv_sems.at[a * len(CHIP_FLIPS) + f],
                                                 device_id=peer, device_id_type=MESH_ID).start()
        token[...] = jnp.zeros_like(token)

    land_shapes = [(N_CHIPS,) + a.shape for a in arrs]
    operands = [pltpu.with_memory_space_constraint(a, pltpu.HBM) for a in arrs]
    operands += [pltpu.with_memory_space_constraint(lax.empty(s, a.dtype), pltpu.HBM) for s, a in zip(land_shapes, arrs)]
    out = pl.pallas_call(
        body, name=name,
        out_shape=(pltpu.SemaphoreType.DMA((ncp,)), pltpu.SemaphoreType.DMA((ncp,)), *[pltpu.HBM(a.shape, a.dtype) for a in arrs],
                   *[pltpu.HBM(s, a.dtype) for s, a in zip(land_shapes, arrs)], SDS((8, 128), f32)),
        in_specs=[HBM_SPEC] * (2 * n), out_specs=(SEM_SPEC, SEM_SPEC, *[HBM_SPEC] * (2 * n), pl.BlockSpec(memory_space=pltpu.VMEM)),
        input_output_aliases={i: 2 + i for i in range(2 * n)},
        compiler_params=pltpu.CompilerParams(has_side_effects=SPLIT_EFFECT),
    )(*operands)
    return out[0], out[1], out[2:2 + n], out[2 + n:2 + 2 * n], out[-1]


def gather_wait(name, send_sems, recv_sems, thru, lands, after):
    n = len(thru)

    def body(*refs):
        srcs, lands_r = refs[:n], refs[n:2 * n]
        s_sems, r_sems = refs[2 * n], refs[2 * n + 1]
        me = (lax.axis_index("x"), lax.axis_index("y"), lax.axis_index("c"))
        k = 2 * me[0] + me[1]
        for a in range(n):
            for f, flip in enumerate(CHIP_FLIPS):
                peer = tuple((1 - m) if fl else m for m, fl in zip(me, flip))
                idx = a * len(CHIP_FLIPS) + f
                pltpu.make_async_remote_copy(src_ref=srcs[a], dst_ref=lands_r[a].at[k], send_sem=s_sems.at[idx], recv_sem=r_sems.at[idx],
                                             device_id=peer, device_id_type=MESH_ID).wait_send()
                pltpu.make_async_remote_copy(src_ref=srcs[a], dst_ref=lands_r[a].at[_chip_of(me, flip)], send_sem=s_sems.at[idx],
                                             recv_sem=r_sems.at[idx], device_id=peer, device_id_type=MESH_ID).wait_recv()

    out = pl.pallas_call(
        body, name=name, out_shape=tuple(pltpu.HBM(t.shape, t.dtype) for t in (*thru, *lands)),
        in_specs=[HBM_SPEC] * (2 * n) + [SEM_SPEC, SEM_SPEC, ANY], out_specs=tuple([HBM_SPEC] * (2 * n)),
        input_output_aliases={i: i for i in range(2 * n)},
        compiler_params=pltpu.CompilerParams(has_side_effects=SPLIT_EFFECT),
    )(*thru, *lands, send_sems, recv_sems, after)
    return out[n:]


def gather_weights(w_sh, token):
    big = [w_sh["w_in"][0], w_sh["w_out"][0], w_sh["pool_w"][0].reshape(4 * 64, POOL_GD)]
    small_pack = _flat_rows([w_sh[n] for n in SMALL], SMALL_ROWS) + token[0, 0]
    s_in, s_out, s_pool, s_small = gather_over_chips("gather_weights", [b.astype(bf16) for b in big] + [small_pack])
    Wf = {n: w_sh[n][0] for n in REPL}
    Wf["w_in"] = s_in.transpose(1, 0, 2).reshape(D, IN_DIM)
    Wf["pool_w"] = s_pool.reshape(N_CHIPS, 4, 64, POOL_GD).transpose(1, 0, 2, 3).reshape(4, POOL_GD, POOL_GD)
    Wf["wo4"] = s_out
    small_shapes = [_drop1(n, w_sh[n]).shape for n in SMALL]
    parts = [_split_rows(s_small[k], small_shapes) for k in range(N_CHIPS)]
    for j, n in enumerate(SMALL):
        Wf[n] = jnp.concatenate([parts[k][j] for k in range(N_CHIPS)], axis=SMALL_AXIS[n])
    return Wf


def pack_grads(G):
    sg = small_grads(G)
    repl = _flat_rows([sg[n] for n in REPL], REPL_ROWS)
    w_in_t = jnp.concatenate(G["w_inT"], axis=0)
    slots = []
    for k in range(N_CHIPS):
        rows = [G["wo4"][k], G["wgT4"][0][k], G["wgT4"][1][k], G["wuT4"][0][k], G["wuT4"][1][k], G["wd4"][0][k], G["wd4"][1][k],
                G["pool_w"][:, k * 64:(k + 1) * 64, :].reshape(64, D), _flat_rows([_shard_small(n, sg[n], k) for n in SMALL], SMALL_ROWS), repl]
        rows.append(jnp.pad(w_in_t[k * IN_SH:(k + 1) * IN_SH], ((0, SLOT_ROWS - SLOT_END), (0, 0))))
        slots.append(jnp.concatenate(rows, axis=0))
    return jnp.stack(slots)


def unpack_grads(total, w_sh):
    g = {"w_out": total[OFF_OUT:OFF_GATE], "ffn_w_down": total[OFF_DOWN:OFF_POOL], "pool_w": total[OFF_POOL:OFF_SMALL],
         "ffn_w_gate": jnp.stack([total[OFF_GATE + l * FF_SH:OFF_GATE + (l + 1) * FF_SH].T for l in range(2)]),
         "ffn_w_up": jnp.stack([total[OFF_UP + l * FF_SH:OFF_UP + (l + 1) * FF_SH].T for l in range(2)]),
         "w_in": total[OFF_IN:SLOT_END].T}
    small = _split_rows(total[OFF_SMALL:OFF_REPL], [_drop1(n, w_sh[n]).shape for n in SMALL])
    repl = _split_rows(total[OFF_REPL:OFF_IN], [w_sh[n][0].shape for n in REPL])
    g.update(zip(SMALL, small))
    g.update(zip(REPL, repl))
    return {n: g[n].reshape(w_sh[n].shape) for n in WEIGHT_NAMES}


def _split_rows(flat2d, shapes):
    v = flat2d.reshape(-1)
    out, off = [], 0
    for s in shapes:
        n = math.prod(s)
        out.append(v[off:off + n].reshape(s))
        off += n
    return out


def kernel(x, norm_g, w_in, gm_ln_g, gm_ln_b, gm_ws, gm_bs, conv_w, conv_b, dt_bias, a_log, d_skip, ssm_norm_g, w_out, pool_w, pool_b, pool_scale, ffn_w_gate, ffn_w_up, ffn_w_down, loss_target, m_norm_g, m_w_in, m_gm_ln_g, m_gm_ln_b, m_gm_ws, m_gm_bs, m_conv_w, m_conv_b, m_dt_bias, m_a_log, m_d_skip, m_ssm_norm_g, m_w_out, m_pool_w, m_pool_b, m_pool_scale, m_ffn_w_gate, m_ffn_w_up, m_ffn_w_down, v_norm_g, v_w_in, v_gm_ln_g, v_gm_ln_b, v_gm_ws, v_gm_bs, v_conv_w, v_conv_b, v_dt_bias, v_a_log, v_d_skip, v_ssm_norm_g, v_w_out, v_pool_w, v_pool_b, v_pool_scale, v_ffn_w_gate, v_ffn_w_up, v_ffn_w_down):
    T = x.shape[1]
    w_sh = dict(zip(WEIGHT_NAMES, (norm_g, w_in, gm_ln_g, gm_ln_b, gm_ws, gm_bs, conv_w, conv_b, dt_bias, a_log, d_skip, ssm_norm_g, w_out,
                                   pool_w, pool_b, pool_scale, ffn_w_gate, ffn_w_up, ffn_w_down)))
    m_sh = dict(zip(WEIGHT_NAMES, (m_norm_g, m_w_in, m_gm_ln_g, m_gm_ln_b, m_gm_ws, m_gm_bs, m_conv_w, m_conv_b, m_dt_bias, m_a_log, m_d_skip,
                                   m_ssm_norm_g, m_w_out, m_pool_w, m_pool_b, m_pool_scale, m_ffn_w_gate, m_ffn_w_up, m_ffn_w_down)))
    v_sh = dict(zip(WEIGHT_NAMES, (v_norm_g, v_w_in, v_gm_ln_g, v_gm_ln_b, v_gm_ws, v_gm_bs, v_conv_w, v_conv_b, v_dt_bias, v_a_log, v_d_skip,
                                   v_ssm_norm_g, v_w_out, v_pool_w, v_pool_b, v_pool_scale, v_ffn_w_gate, v_ffn_w_up, v_ffn_w_down)))

    my_k = 2 * lax.axis_index("x") + lax.axis_index("y")
    ffn_own = [w_sh["ffn_w_gate"].reshape(2 * D, FF_SH).astype(bf16), w_sh["ffn_w_up"].reshape(2 * D, FF_SH).astype(bf16),
               w_sh["ffn_w_down"].reshape(2 * FF_SH, D).astype(bf16)]
    send_sems, recv_sems, thru, lands, token = gather_start("gather_ffn_start", ffn_own)
    W = build_weights(gather_weights(w_sh, token))

    def ffn_weights(after):
        landed = gather_wait("gather_ffn_wait", send_sems, recv_sems, thru, lands, after)
        return tuple(lax.dynamic_update_slice(l, o[None], (my_k, 0, 0)) for l, o in zip(landed, ffn_own))

    loss_acc, grad_x, G = local_step(T, x[0], loss_target[0], W, ffn_weights)

    my_c = lax.axis_index("c")
    c_arr = my_c.astype(jnp.int32).reshape(1)
    k_arr = (2 * lax.axis_index("x") + lax.axis_index("y")).astype(jnp.int32).reshape(1)
    packs = pack_grads(G)
    got = pair_split_exchange("grads_pair_split", packs, HALF_ROWS)
    pair16 = pair_sum("grads_pair_sum", packs, got, c_arr)
    landed = scatter_over_chips("grads_scatter", pair16)
    half = chip_sum("grads_chip_sum", pair16, landed, k_arr)
    other = pair_swap("grads_pair_swap", half)
    total = jnp.concatenate([jnp.where(my_c == 0, half, other), jnp.where(my_c == 0, other, half)], axis=0)
    grads = unpack_grads(total, w_sh)

    delta, new_m, new_v = {}, {}, {}
    for n in WEIGHT_NAMES:
        shp = w_sh[n].shape
        two_d = (-1, shp[-1])
        d_, m_, v_ = adamw("adamw_" + n, w_sh[n].reshape(two_d), grads[n].reshape(two_d), m_sh[n].reshape(two_d), v_sh[n].reshape(two_d))
        delta[n], new_m[n], new_v[n] = d_.reshape(shp), m_.reshape(shp), v_.reshape(shp)

    loss = lax.psum(loss_acc[0, 0], ("x", "y", "c"))
    return (loss, grad_x[None], *[grads[n] for n in WEIGHT_NAMES], *[delta[n] for n in WEIGHT_NAMES],
            *[new_m[n] for n in WEIGHT_NAMES], *[new_v[n] for n in WEIGHT_NAMES])
```

```python
import functools
import math

import jax
import jax.numpy as jnp
from jax import lax
from jax.experimental import pallas as pl
from jax.experimental.pallas import tpu as pltpu

f32, bf16 = jnp.float32, jnp.bfloat16
SDS = jax.ShapeDtypeStruct

D = 1024
EPS = 1e-6
CHUNK = 128
GM_HEADS, GM_HD = 4, 256
SSM_GROUPS, SSM_HPG, SSM_P, SSM_N = 4, 4, 64, 128
N_HEADS = SSM_GROUPS * SSM_HPG
CONV_K = 4
CONV_DIM = 2048
POOL_WINDOWS = (2, 4, 8, 16)
POOL_GD = 256
POOL_HALO = 16
CONV_HALO = 8
D_FF = 2816
DT_PAD = 128
IN_DIM = 5136

ADAM_LR, ADAM_B1, ADAM_B2, ADAM_EPS, ADAM_WD, ADAM_STEP = 0.001, 0.9, 0.999, 1e-08, 0.01, 10

NT = (((1,), (1,)), ((), ()))
TN = (((0,), (0,)), ((), ()))
NN = (((1,), (0,)), ((), ()))
HI = lax.Precision.HIGHEST


def _silu(x):
    return x * jax.nn.sigmoid(x)


def _softplus(x):
    return jnp.maximum(x, 0.0) + jnp.log1p(jnp.exp(-jnp.abs(x)))


def _rms(x, g):
    return x * lax.rsqrt(jnp.mean(x * x, axis=-1, keepdims=True) + EPS) * g


def _rms_bwd(x, g, dy):
    r = lax.rsqrt(jnp.mean(x * x, axis=-1, keepdims=True) + EPS)
    xh = x * r
    dxh = dy * g
    dx = r * (dxh - xh * jnp.mean(dxh * xh, axis=-1, keepdims=True))
    return dx, jnp.sum(dy * xh, axis=0, keepdims=True)


def _bdot(a, b, dims=NN):
    return lax.dot_general(a.astype(bf16), b.astype(bf16), dims, preferred_element_type=f32)


def matmul(name, pairs, mode, out_dtype, tm, tn, tk=None):
    a0, b0 = pairs[0]
    if mode == "tn":
        M, N, K = a0.shape[1], b0.shape[1], a0.shape[0]
    else:
        M, K = a0.shape
        N = b0.shape[1] if mode == "nn" else b0.shape[0]
    tm, tn = min(tm, M), min(tn, N)
    assert M % tm == 0 and N % tn == 0, (name, M, N, tm, tn)
    if tk is None:
        nk = 1
    else:
        assert len(pairs) == 1 and K % tk == 0
        nk = K // tk
    dims = {"nn": NN, "nt": NT, "tn": TN}[mode]
    in_specs, args = [], []
    for a, b in pairs:
        kk = (a.shape[0] if mode == "tn" else a.shape[1]) if tk is None else tk
        if mode == "tn":
            in_specs.append(pl.BlockSpec((kk, tm), lambda j, i, k: (k, i)))
            in_specs.append(pl.BlockSpec((kk, tn), lambda j, i, k: (k, j)))
        elif mode == "nn":
            in_specs.append(pl.BlockSpec((tm, kk), lambda j, i, k: (i, k)))
            in_specs.append(pl.BlockSpec((kk, tn), lambda j, i, k: (k, j)))
        else:
            in_specs.append(pl.BlockSpec((tm, kk), lambda j, i, k: (i, k)))
            in_specs.append(pl.BlockSpec((tn, kk), lambda j, i, k: (j, k)))
        args += [a, b]
    npairs = len(pairs)

    def kern(*refs):
        o = refs[2 * npairs]
        part = None
        for p in range(npairs):
            d = _bdot(refs[2 * p][...], refs[2 * p + 1][...], dims)
            part = d if part is None else part + d
        if nk == 1:
            o[...] = part.astype(out_dtype)
        else:
            acc = refs[2 * npairs + 1]
            k = pl.program_id(2)

            @pl.when(k == 0)
            def _():
                acc[...] = part

            @pl.when(k > 0)
            def _():
                acc[...] += part

            @pl.when(k == nk - 1)
            def _():
                o[...] = acc[...].astype(out_dtype)

    return pl.pallas_call(
        kern, name=name, grid=(N // tn, M // tm, nk),
        in_specs=in_specs, out_specs=pl.BlockSpec((tm, tn), lambda j, i, k: (i, j)),
        out_shape=SDS((M, N), out_dtype),
        scratch_shapes=[pltpu.VMEM((tm, tn), f32)] if nk > 1 else [],
        compiler_params=pltpu.CompilerParams(dimension_semantics=("parallel", "parallel", "arbitrary")),
    )(*args)


def mm(name, grid, pairs, dims, o_spec, out_shape):
    nk = grid[2]
    npairs = len(pairs)
    in_specs, args = [], []
    for a, a_spec, b, b_spec in pairs:
        in_specs += [a_spec, b_spec]
        args += [a, b]
    blk = tuple(d for d in o_spec.block_shape if d is not None)

    def kern(*refs):
        o = refs[2 * npairs]
        part = None
        for p in range(npairs):
            d = _bdot(refs[2 * p][...], refs[2 * p + 1][...], dims)
            part = d if part is None else part + d
        if nk == 1:
            o[...] = part.astype(o.dtype)
        else:
            acc = refs[2 * npairs + 1]
            k = pl.program_id(2)

            @pl.when(k == 0)
            def _():
                acc[...] = part

            @pl.when(k > 0)
            def _():
                acc[...] += part

            @pl.when(k == nk - 1)
            def _():
                o[...] = acc[...].astype(o.dtype)

    return pl.pallas_call(
        kern, name=name, grid=grid, in_specs=in_specs, out_specs=o_spec, out_shape=out_shape,
        scratch_shapes=[pltpu.VMEM(blk, f32)] if nk > 1 else [],
        compiler_params=pltpu.CompilerParams(dimension_semantics=("parallel", "parallel", "arbitrary")),
    )(*args)


FF_SH = D_FF // 4


def ffn_up(name, T, tm, n_bf, wg4, wu4, l):
    def kern(n_ref, wg_ref, wu_ref, g_ref, u_ref, a_ref):
        n = n_ref[...]
        g = jnp.dot(n, wg_ref[...], preferred_element_type=f32)
        u = jnp.dot(n, wu_ref[...], preferred_element_type=f32)
        g_ref[...] = g.astype(bf16)
        u_ref[...] = u.astype(bf16)
        a_ref[...] = (_silu(g) * u).astype(bf16)
    w_spec = pl.BlockSpec((None, D, FF_SH), lambda k, i: (k, l, 0))
    o_spec = pl.BlockSpec((None, tm, FF_SH), lambda k, i: (k, i, 0))
    s = SDS((4, T, FF_SH), bf16)
    return pl.pallas_call(kern, name=name, grid=(4, T // tm), in_specs=[pl.BlockSpec((tm, D), lambda k, i: (i, 0)), w_spec, w_spec],
                          out_specs=[o_spec] * 3, out_shape=[s, s, s],
                          compiler_params=pltpu.CompilerParams(dimension_semantics=("parallel", "parallel")))(n_bf, wg4, wu4)


def ffn_dgu(name, T, tm, d_f, wd4, gate4, up4, l):
    def kern(df_ref, wd_ref, g_ref, u_ref, dg_ref, du_ref):
        dact = _bdot(df_ref[...], wd_ref[...], NT)
        _, vjp = jax.vjp(lambda a, b: _silu(a) * b, g_ref[...].astype(f32), u_ref[...].astype(f32))
        dg, du = vjp(dact)
        dg_ref[...] = dg.astype(bf16)
        du_ref[...] = du.astype(bf16)
    a_spec = pl.BlockSpec((None, tm, FF_SH), lambda k, i: (k, i, 0))
    s = SDS((4, T, FF_SH), bf16)
    return pl.pallas_call(kern, name=name, grid=(4, T // tm),
                          in_specs=[pl.BlockSpec((tm, D), lambda k, i: (i, 0)), pl.BlockSpec((None, FF_SH, D), lambda k, i: (k, l, 0)), a_spec, a_spec],
                          out_specs=[a_spec] * 2, out_shape=[s, s],
                          compiler_params=pltpu.CompilerParams(dimension_semantics=("parallel", "parallel")))(d_f, wd4, gate4, up4)


def rowcall(name, body, T, tm, ins, outs, accs=(), scratch=(), reverse=False):
    n = T // tm
    assert T % tm == 0

    def blk(i):
        return (n - 1 - i) if reverse else i

    in_specs, args = [], []
    for spec in ins:
        kind, arr = spec[0], spec[1]
        if kind == "row":
            _, _, w, cb = spec
            in_specs.append(pl.BlockSpec((tm, w), lambda i, cb=cb: (blk(i), cb)))
        elif kind == "prev":
            _, _, w, cb, h = spec
            r = tm // h
            in_specs.append(pl.BlockSpec((h, w), lambda i, cb=cb, r=r: (jnp.maximum(blk(i) * r - 1, 0), cb)))
        elif kind == "next":
            _, _, w, cb, h = spec
            r = tm // h
            in_specs.append(pl.BlockSpec((h, w), lambda i, cb=cb, r=r, h=h: (jnp.minimum((blk(i) + 1) * r, T // h - 1), cb)))
        else:
            nd = arr.ndim
            in_specs.append(pl.BlockSpec(arr.shape, lambda i, nd=nd: (0,) * nd))
        args.append(arr)
    out_shape = [SDS((T, w), dt) for w, dt in outs] + [SDS(tuple(s), f32) for s in accs]
    out_specs = [pl.BlockSpec((tm, w), lambda i: (blk(i), 0)) for w, _ in outs]
    out_specs += [pl.BlockSpec(tuple(s), lambda i, nd=len(s): (0,) * nd) for s in accs]
    ni, no, na = len(ins), len(outs), len(accs)

    def kern(*refs):
        i = pl.program_id(0)
        in_refs, out_refs = refs[:ni], refs[ni:ni + no]
        acc_refs, scr = refs[ni + no:ni + no + na], refs[ni + no + na:]
        if na:
            @pl.when(i == 0)
            def _():
                for a in acc_refs:
                    a[...] = jnp.zeros(a.shape, f32)
        body(blk(i), n, in_refs, out_refs, acc_refs, scr)

    res = pl.pallas_call(
        kern, name=name, grid=(n,), in_specs=in_specs, out_specs=out_specs, out_shape=out_shape,
        scratch_shapes=list(scratch),
        compiler_params=pltpu.CompilerParams(dimension_semantics=("arbitrary",)),
    )(*args)
    return res


def rms_to_bf16(name, T, tm, x, g):
    def body(i, n, ins, outs, accs, scr):
        outs[0][...] = _rms(ins[0][...], ins[1][...]).astype(bf16)
    return rowcall(name, body, T, tm, [("row", x, D, 0), ("const", g)], [(D, bf16)])[0]


def resid_norm(name, T, tm, h_in, f, g_post, g_pre):
    def body(i, n, ins, outs, accs, scr):
        h = ins[0][...] + _rms(ins[1][...], ins[2][...])
        outs[0][...] = h
        if g_pre is not None:
            outs[1][...] = _rms(h, ins[3][...]).astype(bf16)
    ins = [("row", h_in, D, 0), ("row", f, D, 0), ("const", g_post)] + ([("const", g_pre)] if g_pre is not None else [])
    outs = [(D, f32)] + ([(D, bf16)] if g_pre is not None else [])
    return rowcall(name, body, T, tm, ins, outs)


def swiglu_act(name, T, tm, gate, up):
    def body(i, n, ins, outs, accs, scr):
        outs[0][...] = (_silu(ins[0][...]) * ins[1][...]).astype(bf16)
    return rowcall(name, body, T, tm, [("row", gate, D_FF, 0), ("row", up, D_FF, 0)], [(D_FF, bf16)])[0]


def swiglu_bwd(name, T, tm, gate, up, d_act):
    def body(i, n, ins, outs, accs, scr):
        _, vjp = jax.vjp(lambda a, b: _silu(a) * b, ins[0][...], ins[1][...])
        dg, du = vjp(ins[2][...])
        outs[0][...] = dg.astype(bf16)
        outs[1][...] = du.astype(bf16)
    return rowcall(name, body, T, tm, [("row", gate, D_FF, 0), ("row", up, D_FF, 0), ("row", d_act, D_FF, 0)],
                   [(D_FF, bf16), (D_FF, bf16)])


def final_loss_bwd(name, T, tm, h3, f2, tgt, g_post):
    def body(i, n, ins, outs, accs, scr):
        f, g = ins[1][...], ins[3][...]
        e = ins[0][...] + _rms(f, g) - ins[2][...]
        accs[0][...] += jnp.sum(jnp.sum(e * e, axis=-1, keepdims=True) * (0.5 / D), axis=0, keepdims=True)
        dh = e * (1.0 / D)
        df, dg = _rms_bwd(f, g, dh)
        outs[0][...] = dh
        outs[1][...] = df.astype(bf16)
        accs[1][...] += dg
    return rowcall(name, body, T, tm, [("row", h3, D, 0), ("row", f2, D, 0), ("row", tgt, D, 0), ("const", g_post)],
                   [(D, f32), (D, bf16)], accs=[(1, 1), (1, D)])


def bwd_pre_post(name, T, tm, h_out, f, d_res, d_n, g_pre, g_post, df_dtype):
    def body(i, n, ins, outs, accs, scr):
        dx, dgp = _rms_bwd(ins[0][...], ins[4][...], ins[3][...])
        dh = ins[2][...] + dx
        df, dgq = _rms_bwd(ins[1][...], ins[5][...], dh)
        outs[0][...] = dh
        outs[1][...] = df.astype(df_dtype)
        accs[0][...] += dgp
        accs[1][...] += dgq
    return rowcall(name, body, T, tm,
                   [("row", h_out, D, 0), ("row", f, D, 0), ("row", d_res, D, 0), ("row", d_n, D, 0), ("const", g_pre), ("const", g_post)],
                   [(D, f32), (D, df_dtype)], accs=[(1, D), (1, D)])


def bwd_post(name, T, tm, f, d_h, g_post):
    def body(i, n, ins, outs, accs, scr):
        df, dg = _rms_bwd(ins[0][...], ins[2][...], ins[1][...])
        outs[0][...] = df.astype(bf16)
        accs[0][...] += dg
    return rowcall(name, body, T, tm, [("row", f, D, 0), ("row", d_h, D, 0), ("const", g_post)], [(D, bf16)], accs=[(1, D)])


def bwd_pre(name, T, tm, h, d_res, d_n, g_pre):
    def body(i, n, ins, outs, accs, scr):
        dx, dg = _rms_bwd(ins[0][...], ins[3][...], ins[2][...])
        outs[0][...] = ins[1][...] + dx
        accs[0][...] += dg
    return rowcall(name, body, T, tm, [("row", h, D, 0), ("row", d_res, D, 0), ("row", d_n, D, 0), ("const", g_pre)],
                   [(D, f32)], accs=[(1, D)])


def _layer_norm_parts(x):
    mu = jnp.mean(x, axis=-1, keepdims=True)
    xc = x - mu
    r = lax.rsqrt(jnp.mean(xc * xc, axis=-1, keepdims=True) + EPS)
    return xc * r, r


def gmlp_fwd(name, T, tm, uvz, ln_g, ln_b, wm, bs):
    def body(i, n, ins, outs, accs, scr):
        gu = jax.nn.gelu(ins[0][...])
        xh, _ = _layer_norm_parts(jax.nn.gelu(ins[1][...]))
        vln = (xh * ins[2][...] + ins[3][...]).astype(bf16)
        for c in range(tm // CHUNK):
            rows = slice(c * CHUNK, (c + 1) * CHUNK)
            for h in range(GM_HEADS):
                cols = slice(h * GM_HD, (h + 1) * GM_HD)
                mixed = jnp.dot(ins[4][h], vln[rows, cols], preferred_element_type=f32) + ins[5][h]
                outs[0][rows, cols] = (gu[rows, cols] * mixed).astype(bf16)
    return rowcall(name, body, T, tm, [("row", uvz, D, 0), ("row", uvz, D, 1), ("const", ln_g), ("const", ln_b), ("const", wm), ("const", bs)],
                   [(D, bf16)])[0]


def gmlp_bwd(name, T, tm, uvz, d_ya, ln_g, ln_b, wm, bs):
    def body(i, n, ins, outs, accs, scr):
        u, v, dya = ins[0][...], ins[1][...], ins[2][...]
        gu, gelu_u_vjp = jax.vjp(jax.nn.gelu, u)
        gv, gelu_v_vjp = jax.vjp(jax.nn.gelu, v)
        xh, r = _layer_norm_parts(gv)
        lng = ins[3][...]
        vln = (xh * lng + ins[4][...]).astype(bf16)
        rr = lax.broadcasted_iota(jnp.int32, (CHUNK, CHUNK), 0)
        cc = lax.broadcasted_iota(jnp.int32, (CHUNK, CHUNK), 1)
        causal = (rr >= cc).astype(f32)
        dvln_ref = scr[0]
        dgu_ref = scr[1]
        for c in range(tm // CHUNK):
            rows = slice(c * CHUNK, (c + 1) * CHUNK)
            for h in range(GM_HEADS):
                cols = slice(h * GM_HD, (h + 1) * GM_HD)
                w = ins[5][h]
                blk = vln[rows, cols]
                mixed = jnp.dot(w, blk, preferred_element_type=f32) + ins[6][h]
                dy = dya[rows, cols]
                dgu_ref[rows, cols] = dy * mixed
                dm = dy * gu[rows, cols]
                accs[3][h] += jnp.sum(dm, axis=1, keepdims=True)
                accs[2][h] += _bdot(dm, blk, NT) * causal
                dvln_ref[rows, cols] = _bdot(w, dm, TN)
        dvln = dvln_ref[...]
        accs[0][...] += jnp.sum(dvln * xh, axis=0, keepdims=True)
        accs[1][...] += jnp.sum(dvln, axis=0, keepdims=True)
        dxh = dvln * lng
        dgv = r * (dxh - jnp.mean(dxh, axis=-1, keepdims=True) - xh * jnp.mean(dxh * xh, axis=-1, keepdims=True))
        outs[0][...] = gelu_u_vjp(dgu_ref[...])[0].astype(bf16)
        outs[1][...] = gelu_v_vjp(dgv)[0].astype(bf16)
    return rowcall(name, body, T, tm,
                   [("row", uvz, D, 0), ("row", uvz, D, 1), ("row", d_ya, D, 0), ("const", ln_g), ("const", ln_b), ("const", wm), ("const", bs)],
                   [(D, bf16), (D, bf16)], accs=[(1, D), (1, D), (GM_HEADS, CHUNK, CHUNK), (GM_HEADS, CHUNK, 1)],
                   scratch=[pltpu.VMEM((tm, D), f32), pltpu.VMEM((tm, D), f32)])


CONV_RC, CONV_LB = 32, 512


def _conv_fill(i, x_ref, halo_ref, scr, tm):
    scr[pl.ds(0, CONV_HALO), :] = jnp.where(i > 0, halo_ref[...], 0.0)
    scr[pl.ds(CONV_HALO, tm), :] = x_ref[...]


def _conv_taps(ext):
    return [ext[CONV_HALO - (CONV_K - 1) + k:CONV_HALO - (CONV_K - 1) + k + CONV_RC] for k in range(CONV_K)]


def conv_fwd(name, T, tm, xbc, conv_w, conv_b):
    def body(i, n, ins, outs, accs, scr):
        s = scr[0]
        _conv_fill(i, ins[0], ins[1], s, tm)
        for lb in range(CONV_DIM // CONV_LB):
            lanes = slice(lb * CONV_LB, (lb + 1) * CONV_LB)
            w, b = ins[2][:, lanes], ins[3][:, lanes]

            def step(j, carry):
                r0 = pl.multiple_of(j * CONV_RC, CONV_RC)
                taps = _conv_taps(s[pl.ds(r0, CONV_RC + CONV_HALO), lanes])
                pre = b + sum(w[k:k + 1] * taps[k] for k in range(CONV_K))
                outs[0][pl.ds(r0, CONV_RC), lanes] = _silu(pre)
                return carry
            lax.fori_loop(0, tm // CONV_RC, step, 0)
    return rowcall(name, body, T, tm, [("row", xbc, CONV_DIM, 0), ("prev", xbc, CONV_DIM, 0, CONV_HALO), ("const", conv_w), ("const", conv_b)],
                   [(CONV_DIM, f32)], scratch=[pltpu.VMEM((tm + CONV_HALO, CONV_DIM), f32)])[0]


def conv_bwd_pre(name, T, tm, xbc, d_xc, conv_w, conv_b):
    def body(i, n, ins, outs, accs, scr):
        s = scr[0]
        _conv_fill(i, ins[0], ins[1], s, tm)
        fold = lambda v: jnp.sum(v.reshape(CONV_RC // 8, 8, CONV_LB), axis=0)
        for lb in range(CONV_DIM // CONV_LB):
            lanes = slice(lb * CONV_LB, (lb + 1) * CONV_LB)
            w, b = ins[3][:, lanes], ins[4][:, lanes]

            def step(j, carry):
                r0 = pl.multiple_of(j * CONV_RC, CONV_RC)
                taps = _conv_taps(s[pl.ds(r0, CONV_RC + CONV_HALO), lanes])
                pre = b + sum(w[k:k + 1] * taps[k] for k in range(CONV_K))
                _, vjp = jax.vjp(_silu, pre)
                dpre = vjp(ins[2][pl.ds(r0, CONV_RC), lanes])[0]
                outs[0][pl.ds(r0, CONV_RC), lanes] = dpre
                return tuple(carry[k] + fold(dpre * taps[k]) for k in range(CONV_K)) + (carry[CONV_K] + fold(dpre),)
            zero = jnp.zeros((8, CONV_LB), f32)
            sums = lax.fori_loop(0, tm // CONV_RC, step, (zero,) * (CONV_K + 1))
            for k in range(CONV_K):
                accs[0][pl.ds(k, 1), lanes] += jnp.sum(sums[k], axis=0, keepdims=True)
            accs[1][:, lanes] += jnp.sum(sums[CONV_K], axis=0, keepdims=True)
    return rowcall(name, body, T, tm,
                   [("row", xbc, CONV_DIM, 0), ("prev", xbc, CONV_DIM, 0, CONV_HALO), ("row", d_xc, CONV_DIM, 0), ("const", conv_w), ("const", conv_b)],
                   [(CONV_DIM, f32)], accs=[(CONV_K, CONV_DIM), (1, CONV_DIM)], scratch=[pltpu.VMEM((tm + CONV_HALO, CONV_DIM), f32)])


def conv_bwd_x(name, T, tm, d_pre, conv_w):
    def body(i, n, ins, outs, accs, scr):
        s = scr[0]
        s[pl.ds(0, tm), :] = ins[0][...]
        s[pl.ds(tm, CONV_HALO), :] = jnp.where(i < n - 1, ins[1][...], 0.0)
        for lb in range(CONV_DIM // CONV_LB):
            lanes = slice(lb * CONV_LB, (lb + 1) * CONV_LB)
            w = ins[2][:, lanes]

            def step(j, carry):
                r0 = pl.multiple_of(j * CONV_RC, CONV_RC)
                ext = s[pl.ds(r0, CONV_RC + CONV_HALO), lanes]
                dx = sum(w[k:k + 1] * ext[CONV_K - 1 - k:CONV_K - 1 - k + CONV_RC] for k in range(CONV_K))
                outs[0][pl.ds(r0, CONV_RC), lanes] = dx.astype(bf16)
                return carry
            lax.fori_loop(0, tm // CONV_RC, step, 0)
    return rowcall(name, body, T, tm, [("row", d_pre, CONV_DIM, 0), ("next", d_pre, CONV_DIM, 0, CONV_HALO), ("const", conv_w)],
                   [(CONV_DIM, bf16)], scratch=[pltpu.VMEM((tm + CONV_HALO, CONV_DIM), f32)])[0]


def _ssd_chunk(X4, dtr, B4, C4, S4, dtb, alog, dsk):
    L = CHUNK
    rr = lax.broadcasted_iota(jnp.int32, (L, L), 0)
    cc = lax.broadcasted_iota(jnp.int32, (L, L), 1)
    tril = rr >= cc
    lane = lax.broadcasted_iota(jnp.int32, (1, DT_PAD), 1)
    sub = lax.broadcasted_iota(jnp.int32, (DT_PAD, 1), 0)
    glane = lax.broadcasted_iota(jnp.int32, (1, SSM_HPG * SSM_P), 1) // SSM_P
    dt = _softplus(dtr + dtb)
    a = -jnp.exp(alog)
    dA = dt * a
    acum = jnp.dot(tril.astype(f32), dA, precision=HI, preferred_element_type=f32)
    acumT = acum.T
    tot = jnp.sum(dA, axis=0, keepdims=True)
    ys, Sn = [], []
    for g in range(SSM_GROUPS):
        hm = [(glane == r).astype(f32) for r in range(SSM_HPG)]
        cols = [jnp.sum(acum * (lane == SSM_HPG * g + r).astype(f32), axis=1, keepdims=True) for r in range(SSM_HPG)]
        dtc = [jnp.sum(dt * (lane == SSM_HPG * g + r).astype(f32), axis=1, keepdims=True) for r in range(SSM_HPG)]
        tots = [jnp.sum(tot * (lane == SSM_HPG * g + r).astype(f32), axis=1, keepdims=True) for r in range(SSM_HPG)]
        dsc = [jnp.sum(dsk * (lane == SSM_HPG * g + r).astype(f32), axis=1, keepdims=True) for r in range(SSM_HPG)]
        x = X4[g]
        xdt = x * sum(dtc[r] * hm[r] for r in range(SSM_HPG))
        cb = _bdot(C4[g], B4[g], NT)
        y = x * sum(dsc[r] * hm[r] for r in range(SSM_HPG))
        for r in range(SSM_HPG):
            row = jnp.sum(acumT * (sub == SSM_HPG * g + r).astype(f32), axis=0, keepdims=True)
            dec = jnp.exp(jnp.where(tril, cols[r] - row, -jnp.inf))
            y = y + _bdot(cb * dec, xdt * hm[r])
        y = y + _bdot(C4[g], S4[g]) * sum(jnp.exp(cols[r]) * hm[r] for r in range(SSM_HPG))
        dte = sum(jnp.exp(tots[r] - cols[r]) * hm[r] for r in range(SSM_HPG))
        s_new = S4[g] * sum(jnp.exp(tots[r]) * hm[r] for r in range(SSM_HPG)) + _bdot(B4[g], xdt * dte, TN)
        ys.append(y)
        Sn.append(s_new)
    return tuple(ys), tuple(Sn)


def _ssd_ins(xc, dtr):
    gw = SSM_HPG * SSM_P
    ins = [("row", xc, gw, g) for g in range(SSM_GROUPS)]
    ins += [("row", xc, SSM_N, D // SSM_N + g) for g in range(SSM_GROUPS)]
    ins += [("row", xc, SSM_N, D // SSM_N + SSM_GROUPS + g) for g in range(SSM_GROUPS)]
    ins += [("row", dtr, DT_PAD, 0)]
    return ins


def ssd_fwd(name, T, xc, dtr, dtb, alog, dsk):
    gw = SSM_HPG * SSM_P

    def body(i, n, ins, outs, accs, scr):
        S = scr[0]

        @pl.when(i == 0)
        def _():
            S[...] = jnp.zeros(S.shape, f32)
        X4 = tuple(ins[g][...] for g in range(4))
        B4 = tuple(ins[4 + g][...] for g in range(4))
        C4 = tuple(ins[8 + g][...] for g in range(4))
        S4 = tuple(S[:, g * gw:(g + 1) * gw] for g in range(4))
        outs[1][...] = S[...]
        ys, Sn = _ssd_chunk(X4, ins[12][...], B4, C4, S4, ins[13][...], ins[14][...], ins[15][...])
        for g in range(4):
            outs[0][:, g * gw:(g + 1) * gw] = ys[g]
            S[:, g * gw:(g + 1) * gw] = Sn[g]
    ins = _ssd_ins(xc, dtr) + [("const", dtb), ("const", alog), ("const", dsk)]
    return rowcall(name, body, T, CHUNK, ins, [(D, f32), (D, f32)], scratch=[pltpu.VMEM((SSM_N, D), f32)])


def ssd_bwd(name, T, xc, dtr, sprev, d_y, dtb, alog, dsk):
    gw = SSM_HPG * SSM_P

    def body(i, n, ins, outs, accs, scr):
        dS = scr[0]

        @pl.when(i == n - 1)
        def _():
            dS[...] = jnp.zeros(dS.shape, f32)
        X4 = tuple(ins[g][...] for g in range(4))
        B4 = tuple(ins[4 + g][...] for g in range(4))
        C4 = tuple(ins[8 + g][...] for g in range(4))
        S4 = tuple(ins[13 + g][...] for g in range(4))
        dY4 = tuple(ins[17 + g][...] for g in range(4))
        dS4 = tuple(dS[:, g * gw:(g + 1) * gw] for g in range(4))
        _, vjp = jax.vjp(_ssd_chunk, X4, ins[12][...], B4, C4, S4, ins[21][...], ins[22][...], ins[23][...])
        dX4, ddtr, dB4, dC4, dSp, ddtb, dalog, ddsk = vjp((dY4, dS4))
        for g in range(4):
            outs[0][:, g * gw:(g + 1) * gw] = dX4[g]
            outs[0][:, D + g * SSM_N:D + (g + 1) * SSM_N] = dB4[g]
            outs[0][:, D + (SSM_GROUPS + g) * SSM_N:D + (SSM_GROUPS + g + 1) * SSM_N] = dC4[g]
            dS[:, g * gw:(g + 1) * gw] = dSp[g]
        outs[1][...] = ddtr.astype(bf16)
        accs[0][...] += ddtb
        accs[1][...] += dalog
        accs[2][...] += ddsk
    ins = _ssd_ins(xc, dtr) + [("row", sprev, gw, g) for g in range(4)] + [("row", d_y, gw, g) for g in range(4)]
    ins += [("const", dtb), ("const", alog), ("const", dsk)]
    return rowcall(name, body, T, CHUNK, ins, [(CONV_DIM, f32), (DT_PAD, bf16)], accs=[(1, DT_PAD)] * 3,
                   scratch=[pltpu.VMEM((SSM_N, D), f32)], reverse=True)


def _gate_group(y, z, g):
    return _rms(y * _silu(z), g)


def gate_fwd(name, T, tm, y, uvz, gn):
    def body(i, n, ins, outs, accs, scr):
        for g in range(SSM_GROUPS):
            cols = slice(g * 256, (g + 1) * 256)
            outs[0][:, cols] = _gate_group(ins[0][:, cols], ins[1][:, cols], ins[2][:, cols]).astype(bf16)
    return rowcall(name, body, T, tm, [("row", y, D, 0), ("row", uvz, D, 2), ("const", gn)], [(D, bf16)])[0]


def gate_bwd(name, T, tm, y, uvz, d_yb, gn):
    def body(i, n, ins, outs, accs, scr):
        for g in range(SSM_GROUPS):
            cols = slice(g * 256, (g + 1) * 256)
            _, vjp = jax.vjp(_gate_group, ins[0][:, cols], ins[1][:, cols], ins[3][:, cols])
            dy, dz, dg = vjp(ins[2][:, cols])
            outs[0][:, cols] = dy
            outs[1][:, cols] = dz.astype(bf16)
            accs[0][:, cols] += dg
    return rowcall(name, body, T, tm, [("row", y, D, 0), ("row", uvz, D, 2), ("row", d_yb, D, 0), ("const", gn)],
                   [(D, f32), (D, bf16)], accs=[(1, D)])


def _pool_diff(i, tm, h_ref, halo_ref, g_ref, scr):
    g = g_ref[...]
    yn = _rms(h_ref[...], g)
    scr[pl.ds(0, POOL_HALO), :] = jnp.where(i > 0, _rms(halo_ref[...], g), 0.0)
    scr[pl.ds(POOL_HALO, tm), :] = yn
    pos = (i * tm + lax.broadcasted_iota(jnp.int32, (tm, 1), 0) + 1).astype(f32)
    parts = []
    for gi, win in enumerate(POOL_WINDOWS):
        cols = slice(gi * POOL_GD, (gi + 1) * POOL_GD)
        s = scr[pl.ds(POOL_HALO, tm), cols]
        for j in range(1, win):
            s = s + scr[pl.ds(POOL_HALO - j, tm), cols]
        parts.append(s / jnp.minimum(pos, float(win)) - yn[:, cols])
    return parts


def pool_fwd(name, T, tm, h2, g_pre, pw, pb, psc):
    def body(i, n, ins, outs, accs, scr):
        parts = _pool_diff(i, tm, ins[0], ins[1], ins[2], scr[0])
        for gi in range(len(POOL_WINDOWS)):
            cols = slice(gi * POOL_GD, (gi + 1) * POOL_GD)
            o = _bdot(parts[gi], ins[3][gi]) + ins[4][:, cols]
            outs[0][:, cols] = o * ins[5][:, cols]
    return rowcall(name, body, T, tm, [("row", h2, D, 0), ("prev", h2, D, 0, POOL_HALO), ("const", g_pre), ("const", pw), ("const", pb), ("const", psc)],
                   [(D, f32)], scratch=[pltpu.VMEM((tm + POOL_HALO, D), f32)])[0]


def pool_bwd(name, T, tm, h2, d_pm, d_res, g_pre, pw, pb, psc):
    def body(i, n, ins, outs, accs, scr):
        parts = _pool_diff(i, tm, ins[0], ins[1], ins[5], scr[0])
        dpm = ins[2][...]
        psc_v = ins[8][...]
        dps = dpm * psc_v
        dps_halo = jnp.where(i < n - 1, ins[3][...] * psc_v, 0.0)
        accs[1][...] += jnp.sum(dps, axis=0, keepdims=True)
        pos = (i * tm + lax.broadcasted_iota(jnp.int32, (tm, 1), 0) + 1).astype(f32)
        pos_h = ((i + 1) * tm + lax.broadcasted_iota(jnp.int32, (POOL_HALO, 1), 0) + 1).astype(f32)
        r_scr = scr[1]
        dyn_scr = scr[2]
        for gi, win in enumerate(POOL_WINDOWS):
            cols = slice(gi * POOL_GD, (gi + 1) * POOL_GD)
            w = ins[6][gi]
            o = _bdot(parts[gi], w) + ins[7][:, cols]
            accs[2][:, cols] += jnp.sum(dpm[:, cols] * o, axis=0, keepdims=True)
            accs[0][gi] += _bdot(parts[gi], dps[:, cols], TN)
            q = _bdot(dps[:, cols], w, NT)
            qh = _bdot(dps_halo[:, cols], w, NT)
            r_scr[pl.ds(0, tm), cols] = q / jnp.minimum(pos, float(win))
            r_scr[pl.ds(tm, POOL_HALO), cols] = qh / jnp.minimum(pos_h, float(win))
            s = r_scr[pl.ds(0, tm), cols]
            for j in range(1, win):
                s = s + r_scr[pl.ds(j, tm), cols]
            dyn_scr[:, cols] = s - q
        dx, dg = _rms_bwd(ins[0][...], ins[5][...], dyn_scr[...])
        outs[0][...] = ins[4][...] + dx
        accs[3][...] += dg
    ins = [("row", h2, D, 0), ("prev", h2, D, 0, POOL_HALO), ("row", d_pm, D, 0), ("next", d_pm, D, 0, POOL_HALO), ("row", d_res, D, 0),
           ("const", g_pre), ("const", pw), ("const", pb), ("const", psc)]
    return rowcall(name, body, T, tm, ins, [(D, f32)], accs=[(4, POOL_GD, POOL_GD), (1, D), (1, D), (1, D)],
                   scratch=[pltpu.VMEM((tm + POOL_HALO, D), f32), pltpu.VMEM((tm + POOL_HALO, D), f32), pltpu.VMEM((tm, D), f32)])


def local_step(T, x, tgt, W, ffn_weights):
    tm = 512 if T >= 1024 else T // 2
    TKW = 1024 if T >= 1024 else T
    ng = W["norm_g"]
    g = lambda l, j: ng[l, j][None, :]
    G = {}

    row_spec = pl.BlockSpec((tm, D), lambda j, i, k: (i, 0))
    sh_spec = [pl.BlockSpec((None, tm, FF_SH), lambda j, i, k, s=s: (s, i, 0)) for s in range(4)]

    def ffn_fwd(tag, n_bf, l):
        gate4, up4, act4 = ffn_up(f"ffn{tag}_up", T, tm, n_bf, W["wg4"], W["wu4"], l)
        wd_spec = [pl.BlockSpec((None, FF_SH, D), lambda j, i, k, s=s: (s, l, 0)) for s in range(4)]
        f = mm(f"ffn{tag}_down", (1, T // tm, 1), [(act4, sh_spec[s], W["wd4"], wd_spec[s]) for s in range(4)], NN, row_spec, SDS((T, D), f32))
        return gate4, up4, act4, f

    def ffn_bwd(tag, l, n_bf, gate4, up4, act4, d_f):
        d_gate4, d_up4 = ffn_dgu(f"ffn{tag}_dgu", T, tm, d_f, W["wd4"], gate4, up4, l)
        w_spec = [pl.BlockSpec((None, D, FF_SH), lambda j, i, k, s=s: (s, l, 0)) for s in range(4)]
        d_n = mm(f"ffn{tag}_dn", (1, T // tm, 1), [(d_gate4, sh_spec[s], W["wg4"], w_spec[s]) for s in range(4)]
                 + [(d_up4, sh_spec[s], W["wu4"], w_spec[s]) for s in range(4)], NT, row_spec, SDS((T, D), f32))

        def wgrad(nm, a4, b):
            return mm(nm, (4, 1, T // TKW),
                      [(a4, pl.BlockSpec((None, TKW, FF_SH), lambda s, j, k: (s, k, 0)), b, pl.BlockSpec((TKW, D), lambda s, j, k: (k, 0)))],
                      TN, pl.BlockSpec((None, FF_SH, D), lambda s, j, k: (s, 0, 0)), SDS((4, FF_SH, D), f32))
        return d_n, wgrad(f"ffn{tag}_dwg", d_gate4, n_bf), wgrad(f"ffn{tag}_dwu", d_up4, n_bf), wgrad(f"ffn{tag}_dwd", act4, d_f)

    y0 = rms_to_bf16("l0_prenorm", T, tm, x, g(0, 0))
    uvz = matmul("in_uvz", [(y0, W["w_uvz"])], "nn", f32, tm, 1024)
    xbc = matmul("in_xbc", [(y0, W["w_xbc"])], "nn", f32, tm, 1024)
    dtr = matmul("in_dt", [(y0, W["w_dt"])], "nn", f32, tm, DT_PAD)
    y_a = gmlp_fwd("gmlp_fwd", T, tm, uvz, W["ln_g"], W["ln_b"], W["wm"], W["bs"])
    xc = conv_fwd("conv_fwd", T, tm, xbc, W["conv_w"], W["conv_b"])
    y_ssd, sprev = ssd_fwd("ssd_fwd", T, xc, dtr, W["dtb"], W["alog"], W["dsk"])
    y_b = gate_fwd("gate_fwd", T, tm, y_ssd, uvz, W["gn"])
    half = D // 2
    wo4 = W["wo4"]
    ycol = [pl.BlockSpec((tm, half), lambda j, i, k, cb=cb: (i, cb)) for cb in range(2)]
    wo_s = [pl.BlockSpec((None, half, D), lambda j, i, k, s=s: (s, 0, 0)) for s in range(4)]
    mixo = mm("out_proj", (1, T // tm, 1), [(y_a, ycol[0], wo4, wo_s[0]), (y_a, ycol[1], wo4, wo_s[1]),
                                            (y_b, ycol[0], wo4, wo_s[2]), (y_b, ycol[1], wo4, wo_s[3])], NN, row_spec, SDS((T, D), f32))
    h1, n1 = resid_norm("l0_mix_resid", T, tm, x, mixo, g(0, 1), g(0, 2))
    W = dict(W)
    W["wg4"], W["wu4"], W["wd4"] = ffn_weights(h1)
    gate0, up0, act0, f1 = ffn_fwd("0", n1, 0)
    (h2,) = resid_norm("l0_ffn_resid", T, tm, h1, f1, g(0, 3), None)
    pm = pool_fwd("pool_fwd", T, tm, h2, g(1, 0), W["pool_w"], W["pool_b"], W["pool_scale"])
    h3, n3 = resid_norm("l1_mix_resid", T, tm, h2, pm, g(1, 1), g(1, 2))
    gate1, up1, act1, f2 = ffn_fwd("1", n3, 1)
    dh4, d_f2, loss_acc, dg13 = final_loss_bwd("loss_bwd", T, tm, h3, f2, tgt, g(1, 3))
    d_n3, dwg1, dwu1, dwd1 = ffn_bwd("1", 1, n3, gate1, up1, act1, d_f2)
    d_h3, d_pm, dg12, dg11 = bwd_pre_post("l1_mix_bwd", T, tm, h3, pm, dh4, d_n3, g(1, 2), g(1, 1), f32)
    d_h2, G["pool_w"], G["pool_b"], G["pool_scale"], dg10 = pool_bwd("pool_bwd", T, tm, h2, d_pm, d_h3, g(1, 0), W["pool_w"], W["pool_b"], W["pool_scale"])
    d_f1, dg03 = bwd_post("l0_ffn_bwd", T, tm, f1, d_h2, g(0, 3))
    d_n1, dwg0, dwu0, dwd0 = ffn_bwd("0", 0, n1, gate0, up0, act0, d_f1)
    d_h1, d_mixo, dg02, dg01 = bwd_pre_post("l0_mix_bwd", T, tm, h1, mixo, d_h2, d_n1, g(0, 2), g(0, 1), bf16)
    def d_ycat(nm, s0):
        return mm(nm, (2, T // tm, 1), [(d_mixo, row_spec, wo4, pl.BlockSpec((None, half, D), lambda j, i, k: (s0 + j, 0, 0)))], NT,
                  pl.BlockSpec((tm, half), lambda j, i, k: (i, j)), SDS((T, D), f32))

    def d_wo(nm, y):
        return mm(nm, (2, 1, T // TKW), [(y, pl.BlockSpec((TKW, half), lambda s, j, k: (k, s)), d_mixo, pl.BlockSpec((TKW, D), lambda s, j, k: (k, 0)))],
                  TN, pl.BlockSpec((None, half, D), lambda s, j, k: (s, 0, 0)), SDS((2, half, D), f32))
    d_ya, d_yb = d_ycat("out_proj_dya", 0), d_ycat("out_proj_dyb", 2)
    dwo_a, dwo_b = d_wo("out_proj_dwa", y_a), d_wo("out_proj_dwb", y_b)
    d_yssd, d_z, G["gn"] = gate_bwd("gate_bwd", T, tm, y_ssd, uvz, d_yb, W["gn"])
    d_xc, d_dtr, G["dtb"], G["alog"], G["dsk"] = ssd_bwd("ssd_bwd", T, xc, dtr, sprev, d_yssd, W["dtb"], W["alog"], W["dsk"])
    d_pre, G["conv_w"], G["conv_b"] = conv_bwd_pre("conv_bwd_pre", T, tm, xbc, d_xc, W["conv_w"], W["conv_b"])
    d_xbc = conv_bwd_x("conv_bwd_x", T, tm, d_pre, W["conv_w"])
    d_u, d_v, G["ln_g"], G["ln_b"], G["wm"], G["bs"] = gmlp_bwd("gmlp_bwd", T, tm, uvz, d_ya, W["ln_g"], W["ln_b"], W["wm"], W["bs"])
    w_u, w_v, w_z = W["w_uvz"][:, :D], W["w_uvz"][:, D:2 * D], W["w_uvz"][:, 2 * D:]
    d_y0 = matmul("in_dy0", [(d_u, w_u), (d_v, w_v), (d_z, w_z), (d_xbc, W["w_xbc"]), (d_dtr, W["w_dt"])], "nt", f32, tm, 1024)
    G["w_inT"] = [matmul("in_dwu", [(d_u, y0)], "tn", f32, 1024, 1024, TKW), matmul("in_dwv", [(d_v, y0)], "tn", f32, 1024, 1024, TKW),
                  matmul("in_dwz", [(d_z, y0)], "tn", f32, 1024, 1024, TKW), matmul("in_dwxbc", [(d_xbc, y0)], "tn", f32, 1024, 1024, TKW),
                  matmul("in_dwdt", [(d_dtr, y0)], "tn", f32, DT_PAD, 1024, TKW)[:N_HEADS]]
    grad_x, dg00 = bwd_pre("l0_pre_bwd", T, tm, x, d_h1, d_y0, g(0, 0))
    G["norm_g"] = jnp.stack([jnp.concatenate([dg00, dg01, dg02, dg03], 0), jnp.concatenate([dg10, dg11, dg12, dg13], 0)])
    G["wo4"] = [dwo_a[0], dwo_a[1], dwo_b[0], dwo_b[1]]
    G["wgT4"], G["wuT4"], G["wd4"] = [dwg0, dwg1], [dwu0, dwu1], [dwd0, dwd1]
    return loss_acc, grad_x, G


def build_weights(Wf):
    causal = jnp.tril(jnp.ones((CHUNK, CHUNK), bool))
    w_in = Wf["w_in"].astype(bf16)
    pad16 = lambda v: jnp.pad(v.reshape(1, N_HEADS).astype(f32), ((0, 0), (0, DT_PAD - N_HEADS)))
    return {
        "norm_g": Wf["norm_g"],
        "w_uvz": w_in[:, :3 * D], "w_xbc": w_in[:, 3 * D:3 * D + CONV_DIM],
        "w_dt": jnp.pad(w_in[:, 3 * D + CONV_DIM:], ((0, 0), (0, DT_PAD - N_HEADS))),
        "ln_g": Wf["gm_ln_g"].reshape(1, D), "ln_b": Wf["gm_ln_b"].reshape(1, D),
        "wm": jnp.where(causal[None], Wf["gm_ws"], 0).astype(bf16), "bs": Wf["gm_bs"].reshape(GM_HEADS, CHUNK, 1),
        "conv_w": Wf["conv_w"], "conv_b": Wf["conv_b"].reshape(1, CONV_DIM),
        "dtb": pad16(Wf["dt_bias"]), "alog": pad16(Wf["a_log"]), "dsk": pad16(Wf["d_skip"]),
        "gn": Wf["ssm_norm_g"].reshape(1, D),
        "wo4": Wf["wo4"].astype(bf16),
        "pool_w": Wf["pool_w"].astype(bf16), "pool_b": Wf["pool_b"].reshape(1, D), "pool_scale": Wf["pool_scale"].reshape(1, D),
    }


def small_grads(G):
    return {
        "norm_g": G["norm_g"],
        "gm_ln_g": G["ln_g"].reshape(D), "gm_ln_b": G["ln_b"].reshape(D),
        "gm_ws": G["wm"], "gm_bs": G["bs"].reshape(GM_HEADS, CHUNK),
        "conv_w": G["conv_w"], "conv_b": G["conv_b"].reshape(CONV_DIM),
        "dt_bias": G["dtb"][0, :N_HEADS], "a_log": G["alog"][0, :N_HEADS], "d_skip": G["dsk"][0, :N_HEADS],
        "ssm_norm_g": G["gn"].reshape(D),
        "pool_b": G["pool_b"].reshape(4, POOL_GD), "pool_scale": G["pool_scale"].reshape(D),
    }


MESH_ID = pl.DeviceIdType.MESH
ANY = pl.BlockSpec(memory_space=pl.ANY)


DMA_CHUNK_BYTES = 2 << 20
DMA_MAX_CHUNKS = 32


def _pieces(view, axis, align):
    shape = view.shape
    nbytes = math.prod(shape) * jnp.dtype(view.dtype).itemsize
    n = max(1, min(DMA_MAX_CHUNKS, -(-nbytes // DMA_CHUNK_BYTES)))
    rows = shape[axis]
    size = -(-rows // n)
    size = -(-size // align) * align
    out = []
    for s in range(0, rows, size):
        idx = [slice(None)] * len(shape)
        idx[axis] = pl.ds(s, min(size, rows - s))
        out.append(tuple(idx))
    return out


def comm_call(name, operands, out_shapes, plan):
    n_in = len(operands)
    n_out = len(out_shapes)
    n_remote, n_local = plan((0, 0, 0), [None] * n_in, [None] * n_out, True)

    def body(*refs):
        in_refs, out_refs = refs[:n_in], refs[n_in:n_in + n_out]
        send_sems, recv_sems, local_sems = refs[n_in + n_out:]
        me = (lax.axis_index("x"), lax.axis_index("y"), lax.axis_index("c"))
        remote, local = plan(me, in_refs, out_refs, False)
        align = lambda v: 16 if v.dtype == bf16 else 8
        for j, (s, d, axis) in enumerate(local):
            for ix in _pieces(s, axis, align(s)):
                pltpu.make_async_copy(s.at[ix], d.at[ix], local_sems.at[j]).start()
        peers = [tuple((1 - m) if f else m for m, f in zip(me, flip)) for flip, *_ in remote]
        for k, (flip, src, dst, _, axis) in enumerate(remote):
            for ix in _pieces(src, axis, align(src)):
                pltpu.make_async_remote_copy(src_ref=src.at[ix], dst_ref=dst.at[ix], send_sem=send_sems.at[k], recv_sem=recv_sems.at[k],
                                             device_id=peers[k], device_id_type=MESH_ID).start()
        for k, (flip, src, dst, landing, axis) in enumerate(remote):
            pltpu.make_async_remote_copy(src_ref=landing, dst_ref=landing, send_sem=send_sems.at[k], recv_sem=recv_sems.at[k],
                                         device_id=peers[k], device_id_type=MESH_ID).wait_recv()
        for k, (flip, src, dst, landing, axis) in enumerate(remote):
            pltpu.make_async_remote_copy(src_ref=src, dst_ref=dst, send_sem=send_sems.at[k], recv_sem=recv_sems.at[k],
                                         device_id=peers[k], device_id_type=MESH_ID).wait_send()
        for j, (s, d, axis) in enumerate(local):
            pltpu.make_async_copy(s, d, local_sems.at[j]).wait()

    return pl.pallas_call(
        body, name=name, out_shape=list(out_shapes), in_specs=[ANY] * n_in, out_specs=[ANY] * n_out,
        scratch_shapes=[pltpu.SemaphoreType.DMA((n_remote,)), pltpu.SemaphoreType.DMA((n_remote,)), pltpu.SemaphoreType.DMA((max(n_local, 1),))],
    )(*operands)


CHIP_FLIPS = ((1, 0, 0), (0, 1, 0), (1, 1, 0))
PAIR_FLIP = (0, 0, 1)


def gather_over_chips(name, arrs):
    def plan(me, ins, outs, count):
        if count:
            return len(CHIP_FLIPS) * len(arrs), len(arrs)
        k = 2 * me[0] + me[1]
        remote, local = [], []
        for a in range(len(arrs)):
            for flip in CHIP_FLIPS:
                kp = 2 * ((1 - me[0]) if flip[0] else me[0]) + ((1 - me[1]) if flip[1] else me[1])
                remote.append((flip, ins[a], outs[a].at[k], outs[a].at[kp], 0))
            local.append((ins[a], outs[a].at[k], 0))
        return remote, local
    return comm_call(name, arrs, [SDS((4,) + a.shape, a.dtype) for a in arrs], plan)


def pair_split_exchange(name, p, rh):
    def plan(me, ins, outs, count):
        if count:
            return 1, 0
        theirs = ins[0].at[:, pl.ds(pl.multiple_of((1 - me[2]) * rh, 8), rh), :]
        return [(PAIR_FLIP, theirs, outs[0], outs[0], 1)], []
    return comm_call(name, [p], [SDS((4, rh, p.shape[2]), p.dtype)], plan)[0]


def scatter_over_chips(name, cs):
    def plan(me, ins, outs, count):
        if count:
            return len(CHIP_FLIPS), 0
        k = 2 * me[0] + me[1]
        remote = []
        for flip in CHIP_FLIPS:
            kp = 2 * ((1 - me[0]) if flip[0] else me[0]) + ((1 - me[1]) if flip[1] else me[1])
            remote.append((flip, ins[0].at[kp], outs[0].at[k], outs[0].at[kp], 0))
        return remote, []
    return comm_call(name, [cs], [SDS(cs.shape, cs.dtype)], plan)[0]


def pair_swap(name, half):
    def plan(me, ins, outs, count):
        if count:
            return 1, 0
        return [(PAIR_FLIP, ins[0], outs[0], outs[0], 0)], []
    return comm_call(name, [half], [SDS(half.shape, half.dtype)], plan)[0]


def _row_tile(rows, cap=512):
    if rows <= cap:
        return rows
    t = cap - cap % 8
    while rows % t:
        t -= 8
    return t


SUM_ROWS = 448


def pair_sum(name, packs, got, c_arr):
    rh = got.shape[1]
    nb = rh // SUM_ROWS

    def kern(c_ref, a_ref, b_ref, o16_ref):
        o16_ref[...] = (a_ref[...] + b_ref[...]).astype(bf16)
    blk = (None, SUM_ROWS, D)
    grid_spec = pltpu.PrefetchScalarGridSpec(
        num_scalar_prefetch=1, grid=(4, nb),
        in_specs=[pl.BlockSpec(blk, lambda s, i, c: (s, c[0] * nb + i, 0)), pl.BlockSpec(blk, lambda s, i, c: (s, i, 0))],
        out_specs=pl.BlockSpec(blk, lambda s, i, c: (s, i, 0)))
    return pl.pallas_call(kern, name=name, grid_spec=grid_spec, out_shape=SDS(got.shape, bf16),
                          compiler_params=pltpu.CompilerParams(dimension_semantics=("parallel", "parallel")))(c_arr, packs, got)


def chip_sum(name, own16, landed16, k_arr):
    rh = own16.shape[1]
    nb = rh // SUM_ROWS

    def kern(k_ref, own_ref, l0, l1, l2, l3, o_ref):
        k = k_ref[0]
        s = None
        for j, lref in enumerate((l0, l1, l2, l3)):
            t = jnp.where(k == j, own_ref[...], lref[...]).astype(f32)
            s = t if s is None else s + t
        o_ref[...] = s
    blk = (None, SUM_ROWS, D)
    land = [pl.BlockSpec(blk, lambda i, k, j=j: (jnp.where(k[0] == j, (j + 1) % N_CHIPS, j), i, 0)) for j in range(N_CHIPS)]
    grid_spec = pltpu.PrefetchScalarGridSpec(
        num_scalar_prefetch=1, grid=(nb,),
        in_specs=[pl.BlockSpec(blk, lambda i, k: (k[0], i, 0))] + land,
        out_specs=pl.BlockSpec((SUM_ROWS, D), lambda i, k: (i, 0)))
    return pl.pallas_call(kern, name=name, grid_spec=grid_spec, out_shape=SDS((rh, D), f32),
                          compiler_params=pltpu.CompilerParams(dimension_semantics=("parallel",)))(k_arr, own16, landed16, landed16, landed16, landed16)


def adamw(name, w, g, m, v):
    R, C = w.shape
    tr = _row_tile(R, 256)

    def kern(w_ref, g_ref, m_ref, v_ref, d_ref, mo_ref, vo_ref):
        gg = g_ref[...]
        mn = ADAM_B1 * m_ref[...] + (1.0 - ADAM_B1) * gg
        vn = ADAM_B2 * v_ref[...] + (1.0 - ADAM_B2) * jnp.square(gg)
        m_hat = mn / (1.0 - ADAM_B1 ** ADAM_STEP)
        v_hat = vn / (1.0 - ADAM_B2 ** ADAM_STEP)
        d_ref[...] = -ADAM_LR * (m_hat / (jnp.sqrt(v_hat) + ADAM_EPS) + ADAM_WD * w_ref[...])
        mo_ref[...] = mn
        vo_ref[...] = vn
    spec = pl.BlockSpec((tr, C), lambda i: (i, 0))
    s = SDS((R, C), f32)
    return pl.pallas_call(kern, name=name, grid=(R // tr,), in_specs=[spec] * 4, out_specs=[spec] * 3, out_shape=[s, s, s],
                          compiler_params=pltpu.CompilerParams(dimension_semantics=("parallel",)))(w, g, m, v)


WEIGHT_NAMES = ("norm_g", "w_in", "gm_ln_g", "gm_ln_b", "gm_ws", "gm_bs", "conv_w", "conv_b", "dt_bias", "a_log", "d_skip",
                "ssm_norm_g", "w_out", "pool_w", "pool_b", "pool_scale", "ffn_w_gate", "ffn_w_up", "ffn_w_down")
SMALL = ("norm_g", "conv_w", "pool_b", "pool_scale")
REPL = ("gm_ln_g", "gm_ln_b", "gm_ws", "gm_bs", "conv_b", "dt_bias", "a_log", "d_skip", "ssm_norm_g")
SMALL_AXIS = {"norm_g": 2, "conv_w": 1, "pool_b": 1, "pool_scale": 0}
N_CHIPS = 4
IN_SH = IN_DIM // N_CHIPS
SMALL_ROWS = 8
REPL_ROWS = 72
OFF_OUT, OFF_GATE, OFF_UP, OFF_DOWN = 0, 512, 512 + 2 * FF_SH, 512 + 4 * FF_SH
OFF_POOL = OFF_DOWN + 2 * FF_SH
OFF_SMALL = OFF_POOL + 64
OFF_REPL = OFF_SMALL + SMALL_ROWS
OFF_IN = OFF_REPL + REPL_ROWS
SLOT_END = OFF_IN + IN_SH
SLOT_ROWS = 6272
HALF_ROWS = SLOT_ROWS // 2


def _flat_rows(pieces, rows):
    v = jnp.concatenate([p.reshape(-1) for p in pieces])
    return jnp.pad(v, (0, rows * D - v.shape[0])).reshape(rows, D)


def _shard_small(name, full, k):
    ax = SMALL_AXIS[name]
    n = full.shape[ax] // N_CHIPS
    return lax.slice_in_dim(full, k * n, (k + 1) * n, axis=ax)


def _drop1(name, a):
    return a if name == "norm_g" else a[0]


def _row_range(blocks, lo, hi):
    out, off = [], 0
    for b in blocks:
        n = b.shape[0]
        a, e = max(lo, off), min(hi, off + n)
        if a < e:
            out.append(b[a - off:e - off])
        off += n
    return out


HBM_SPEC = pl.BlockSpec(memory_space=pltpu.HBM)
SEM_SPEC = pl.BlockSpec(memory_space=pltpu.SEMAPHORE)
SPLIT_EFFECT = pltpu.SideEffectType.DATAFLOW_SIDE_EFFECTING


def _chip_of(me, flip):
    return 2 * ((1 - me[0]) if flip[0] else me[0]) + ((1 - me[1]) if flip[1] else me[1])


def gather_start(name, arrs, after):
    n = len(arrs)
    ncp = n * len(CHIP_FLIPS)

    def body(*refs):
        srcs, lands = refs[:n], refs[n:2 * n]
        send_sems, recv_sems, token = refs[2 * n + 1], refs[2 * n + 2], refs[-1]
        me = (lax.axis_index("x"), lax.axis_index("y"), lax.axis_index("c"))
        k = 2 * me[0] + me[1]
        for a in range(n):
            for f, flip in enumerate(CHIP_FLIPS):
                peer = tuple((1 - m) if fl else m for m, fl in zip(me, flip))
                for ix in _pieces(srcs[a], 0, 16):
                    pltpu.make_async_remote_copy(src_ref=srcs[a].at[ix], dst_ref=lands[a].at[k].at[ix],
                                                 send_sem=send_sems.at[a * len(CHIP_FLIPS) + f], recv_sem=recv_sems.at[a * len(CHIP_FLIPS) + f],
                                                 device_id=peer, device_id_type=MESH_ID).start()
        token[...] = jnp.zeros_like(token)

    land_shapes = [(N_CHIPS,) + a.shape for a in arrs]
    operands = [pltpu.with_memory_space_constraint(a, pltpu.HBM) for a in arrs]
    operands += [pltpu.with_memory_space_constraint(lax.empty(s, a.dtype), pltpu.HBM) for s, a in zip(land_shapes, arrs)]
    out = pl.pallas_call(
        body, name=name,
        out_shape=(pltpu.SemaphoreType.DMA((ncp,)), pltpu.SemaphoreType.DMA((ncp,)), *[pltpu.HBM(a.shape, a.dtype) for a in arrs],
                   *[pltpu.HBM(s, a.dtype) for s, a in zip(land_shapes, arrs)], SDS((8, 128), f32)),
        in_specs=[HBM_SPEC] * (2 * n) + [ANY], out_specs=(SEM_SPEC, SEM_SPEC, *[HBM_SPEC] * (2 * n), pl.BlockSpec(memory_space=pltpu.VMEM)),
        input_output_aliases={i: 2 + i for i in range(2 * n)},
        compiler_params=pltpu.CompilerParams(has_side_effects=SPLIT_EFFECT),
    )(*operands, after)
    return out[0], out[1], out[2:2 + n], out[2 + n:2 + 2 * n], out[-1]


def gather_wait(name, send_sems, recv_sems, thru, lands, after):
    n = len(thru)

    def body(*refs):
        srcs, lands_r = refs[:n], refs[n:2 * n]
        s_sems, r_sems = refs[2 * n], refs[2 * n + 1]
        me = (lax.axis_index("x"), lax.axis_index("y"), lax.axis_index("c"))
        k = 2 * me[0] + me[1]
        for a in range(n):
            for f, flip in enumerate(CHIP_FLIPS):
                peer = tuple((1 - m) if fl else m for m, fl in zip(me, flip))
                idx = a * len(CHIP_FLIPS) + f
                pltpu.make_async_remote_copy(src_ref=srcs[a], dst_ref=lands_r[a].at[k], send_sem=s_sems.at[idx], recv_sem=r_sems.at[idx],
                                             device_id=peer, device_id_type=MESH_ID).wait_send()
                pltpu.make_async_remote_copy(src_ref=srcs[a], dst_ref=lands_r[a].at[_chip_of(me, flip)], send_sem=s_sems.at[idx],
                                             recv_sem=r_sems.at[idx], device_id=peer, device_id_type=MESH_ID).wait_recv()

    out = pl.pallas_call(
        body, name=name, out_shape=tuple(pltpu.HBM(t.shape, t.dtype) for t in (*thru, *lands)),
        in_specs=[HBM_SPEC] * (2 * n) + [SEM_SPEC, SEM_SPEC, ANY], out_specs=tuple([HBM_SPEC] * (2 * n)),
        input_output_aliases={i: i for i in range(2 * n)},
        compiler_params=pltpu.CompilerParams(has_side_effects=SPLIT_EFFECT),
    )(*thru, *lands, send_sems, recv_sems, after)
    return out[n:]


def gather_weights(w_sh):
    big = [w_sh["w_in"][0], w_sh["w_out"][0], w_sh["pool_w"][0].reshape(4 * 64, POOL_GD)]
    small_pack = _flat_rows([w_sh[n] for n in SMALL], SMALL_ROWS)
    s_in, s_out, s_pool, s_small = gather_over_chips("gather_weights", [b.astype(bf16) for b in big] + [small_pack])
    Wf = {n: w_sh[n][0] for n in REPL}
    Wf["w_in"] = s_in.transpose(1, 0, 2).reshape(D, IN_DIM)
    Wf["pool_w"] = s_pool.reshape(N_CHIPS, 4, 64, POOL_GD).transpose(1, 0, 2, 3).reshape(4, POOL_GD, POOL_GD)
    Wf["wo4"] = s_out
    small_shapes = [_drop1(n, w_sh[n]).shape for n in SMALL]
    parts = [_split_rows(s_small[k], small_shapes) for k in range(N_CHIPS)]
    for j, n in enumerate(SMALL):
        Wf[n] = jnp.concatenate([parts[k][j] for k in range(N_CHIPS)], axis=SMALL_AXIS[n])
    return Wf


def pack_grads(G):
    sg = small_grads(G)
    repl = _flat_rows([sg[n] for n in REPL], REPL_ROWS)
    w_in_t = jnp.concatenate(G["w_inT"], axis=0)
    slots = []
    for k in range(N_CHIPS):
        rows = [G["wo4"][k], G["wgT4"][0][k], G["wgT4"][1][k], G["wuT4"][0][k], G["wuT4"][1][k], G["wd4"][0][k], G["wd4"][1][k],
                G["pool_w"][:, k * 64:(k + 1) * 64, :].reshape(64, D), _flat_rows([_shard_small(n, sg[n], k) for n in SMALL], SMALL_ROWS), repl]
        rows.append(jnp.pad(w_in_t[k * IN_SH:(k + 1) * IN_SH], ((0, SLOT_ROWS - SLOT_END), (0, 0))))
        slots.append(jnp.concatenate(rows, axis=0))
    return jnp.stack(slots)


def unpack_grads(total, w_sh):
    g = {"w_out": total[OFF_OUT:OFF_GATE], "ffn_w_down": total[OFF_DOWN:OFF_POOL], "pool_w": total[OFF_POOL:OFF_SMALL],
         "ffn_w_gate": jnp.stack([total[OFF_GATE + l * FF_SH:OFF_GATE + (l + 1) * FF_SH].T for l in range(2)]),
         "ffn_w_up": jnp.stack([total[OFF_UP + l * FF_SH:OFF_UP + (l + 1) * FF_SH].T for l in range(2)]),
         "w_in": total[OFF_IN:SLOT_END].T}
    small = _split_rows(total[OFF_SMALL:OFF_REPL], [_drop1(n, w_sh[n]).shape for n in SMALL])
    repl = _split_rows(total[OFF_REPL:OFF_IN], [w_sh[n][0].shape for n in REPL])
    g.update(zip(SMALL, small))
    g.update(zip(REPL, repl))
    return {n: g[n].reshape(w_sh[n].shape) for n in WEIGHT_NAMES}


def _split_rows(flat2d, shapes):
    v = flat2d.reshape(-1)
    out, off = [], 0
    for s in shapes:
        n = math.prod(s)
        out.append(v[off:off + n].reshape(s))
        off += n
    return out


def kernel(x, norm_g, w_in, gm_ln_g, gm_ln_b, gm_ws, gm_bs, conv_w, conv_b, dt_bias, a_log, d_skip, ssm_norm_g, w_out, pool_w, pool_b, pool_scale, ffn_w_gate, ffn_w_up, ffn_w_down, loss_target, m_norm_g, m_w_in, m_gm_ln_g, m_gm_ln_b, m_gm_ws, m_gm_bs, m_conv_w, m_conv_b, m_dt_bias, m_a_log, m_d_skip, m_ssm_norm_g, m_w_out, m_pool_w, m_pool_b, m_pool_scale, m_ffn_w_gate, m_ffn_w_up, m_ffn_w_down, v_norm_g, v_w_in, v_gm_ln_g, v_gm_ln_b, v_gm_ws, v_gm_bs, v_conv_w, v_conv_b, v_dt_bias, v_a_log, v_d_skip, v_ssm_norm_g, v_w_out, v_pool_w, v_pool_b, v_pool_scale, v_ffn_w_gate, v_ffn_w_up, v_ffn_w_down):
    T = x.shape[1]
    w_sh = dict(zip(WEIGHT_NAMES, (norm_g, w_in, gm_ln_g, gm_ln_b, gm_ws, gm_bs, conv_w, conv_b, dt_bias, a_log, d_skip, ssm_norm_g, w_out,
                                   pool_w, pool_b, pool_scale, ffn_w_gate, ffn_w_up, ffn_w_down)))
    m_sh = dict(zip(WEIGHT_NAMES, (m_norm_g, m_w_in, m_gm_ln_g, m_gm_ln_b, m_gm_ws, m_gm_bs, m_conv_w, m_conv_b, m_dt_bias, m_a_log, m_d_skip,
                                   m_ssm_norm_g, m_w_out, m_pool_w, m_pool_b, m_pool_scale, m_ffn_w_gate, m_ffn_w_up, m_ffn_w_down)))
    v_sh = dict(zip(WEIGHT_NAMES, (v_norm_g, v_w_in, v_gm_ln_g, v_gm_ln_b, v_gm_ws, v_gm_bs, v_conv_w, v_conv_b, v_dt_bias, v_a_log, v_d_skip,
                                   v_ssm_norm_g, v_w_out, v_pool_w, v_pool_b, v_pool_scale, v_ffn_w_gate, v_ffn_w_up, v_ffn_w_down)))

    my_k = 2 * lax.axis_index("x") + lax.axis_index("y")
    ffn_own = [w_sh["ffn_w_gate"].reshape(2 * D, FF_SH).astype(bf16), w_sh["ffn_w_up"].reshape(2 * D, FF_SH).astype(bf16),
               w_sh["ffn_w_down"].reshape(2 * FF_SH, D).astype(bf16)]
    Wf = gather_weights(w_sh)
    send_sems, recv_sems, thru, lands, token = gather_start("gather_ffn_start", ffn_own, Wf["wo4"])
    Wf["norm_g"] = Wf["norm_g"] + token[0, 0]
    W = build_weights(Wf)

    def ffn_weights(after):
        landed = gather_wait("gather_ffn_wait", send_sems, recv_sems, thru, lands, after)
        return tuple(lax.dynamic_update_slice(l, o[None], (my_k, 0, 0)) for l, o in zip(landed, ffn_own))

    loss_acc, grad_x, G = local_step(T, x[0], loss_target[0], W, ffn_weights)

    my_c = lax.axis_index("c")
    c_arr = my_c.astype(jnp.int32).reshape(1)
    k_arr = (2 * lax.axis_index("x") + lax.axis_index("y")).astype(jnp.int32).reshape(1)
    packs = pack_grads(G)
    got = pair_split_exchange("grads_pair_split", packs, HALF_ROWS)
    pair16 = pair_sum("grads_pair_sum", packs, got, c_arr)
    landed = scatter_over_chips("grads_scatter", pair16)
    half = chip_sum("grads_chip_sum", pair16, landed, k_arr)
    other = pair_swap("grads_pair_swap", half)
    total = jnp.concatenate([jnp.where(my_c == 0, half, other), jnp.where(my_c == 0, other, half)], axis=0)
    grads = unpack_grads(total, w_sh)

    delta, new_m, new_v = {}, {}, {}
    for n in WEIGHT_NAMES:
        shp = w_sh[n].shape
        two_d = (-1, shp[-1])
        d_, m_, v_ = adamw("adamw_" + n, w_sh[n].reshape(two_d), grads[n].reshape(two_d), m_sh[n].reshape(two_d), v_sh[n].reshape(two_d))
        delta[n], new_m[n], new_v[n] = d_.reshape(shp), m_.reshape(shp), v_.reshape(shp)

    loss = lax.psum(loss_acc[0, 0], ("x", "y", "c"))
    return (loss, grad_x[None], *[grads[n] for n in WEIGHT_NAMES], *[delta[n] for n in WEIGHT_NAMES],
            *[new_m[n] for n in WEIGHT_NAMES], *[new_v[n] for n in WEIGHT_NAMES])
```

```python
import functools
import math

import jax
import jax.numpy as jnp
from jax import lax
from jax.experimental import pallas as pl
from jax.experimental.pallas import tpu as pltpu

f32, bf16 = jnp.float32, jnp.bfloat16
SDS = jax.ShapeDtypeStruct

D = 1024
EPS = 1e-6
CHUNK = 128
GM_HEADS, GM_HD = 4, 256
SSM_GROUPS, SSM_HPG, SSM_P, SSM_N = 4, 4, 64, 128
N_HEADS = SSM_GROUPS * SSM_HPG
CONV_K = 4
CONV_DIM = 2048
POOL_WINDOWS = (2, 4, 8, 16)
POOL_GD = 256
POOL_HALO = 16
CONV_HALO = 8
D_FF = 2816
DT_PAD = 128
IN_DIM = 5136

ADAM_LR, ADAM_B1, ADAM_B2, ADAM_EPS, ADAM_WD, ADAM_STEP = 0.001, 0.9, 0.999, 1e-08, 0.01, 10

NT = (((1,), (1,)), ((), ()))
TN = (((0,), (0,)), ((), ()))
NN = (((1,), (0,)), ((), ()))
HI = lax.Precision.HIGHEST


def _silu(x):
    return x * jax.nn.sigmoid(x)


def _softplus(x):
    return jnp.maximum(x, 0.0) + jnp.log1p(jnp.exp(-jnp.abs(x)))


def _rms(x, g):
    return x * lax.rsqrt(jnp.mean(x * x, axis=-1, keepdims=True) + EPS) * g


def _rms_bwd(x, g, dy):
    r = lax.rsqrt(jnp.mean(x * x, axis=-1, keepdims=True) + EPS)
    xh = x * r
    dxh = dy * g
    dx = r * (dxh - xh * jnp.mean(dxh * xh, axis=-1, keepdims=True))
    return dx, jnp.sum(dy * xh, axis=0, keepdims=True)


def _bdot(a, b, dims=NN):
    return lax.dot_general(a.astype(bf16), b.astype(bf16), dims, preferred_element_type=f32)


def matmul(name, pairs, mode, out_dtype, tm, tn, tk=None):
    a0, b0 = pairs[0]
    if mode == "tn":
        M, N, K = a0.shape[1], b0.shape[1], a0.shape[0]
    else:
        M, K = a0.shape
        N = b0.shape[1] if mode == "nn" else b0.shape[0]
    tm, tn = min(tm, M), min(tn, N)
    assert M % tm == 0 and N % tn == 0, (name, M, N, tm, tn)
    if tk is None:
        nk = 1
    else:
        assert len(pairs) == 1 and K % tk == 0
        nk = K // tk
    dims = {"nn": NN, "nt": NT, "tn": TN}[mode]
    in_specs, args = [], []
    for a, b in pairs:
        kk = (a.shape[0] if mode == "tn" else a.shape[1]) if tk is None else tk
        if mode == "tn":
            in_specs.append(pl.BlockSpec((kk, tm), lambda j, i, k: (k, i)))
            in_specs.append(pl.BlockSpec((kk, tn), lambda j, i, k: (k, j)))
        elif mode == "nn":
            in_specs.append(pl.BlockSpec((tm, kk), lambda j, i, k: (i, k)))
            in_specs.append(pl.BlockSpec((kk, tn), lambda j, i, k: (k, j)))
        else:
            in_specs.append(pl.BlockSpec((tm, kk), lambda j, i, k: (i, k)))
            in_specs.append(pl.BlockSpec((tn, kk), lambda j, i, k: (j, k)))
        args += [a, b]
    npairs = len(pairs)

    def kern(*refs):
        o = refs[2 * npairs]
        part = None
        for p in range(npairs):
            d = _bdot(refs[2 * p][...], refs[2 * p + 1][...], dims)
            part = d if part is None else part + d
        if nk == 1:
            o[...] = part.astype(out_dtype)
        else:
            acc = refs[2 * npairs + 1]
            k = pl.program_id(2)

            @pl.when(k == 0)
            def _():
                acc[...] = part

            @pl.when(k > 0)
            def _():
                acc[...] += part

            @pl.when(k == nk - 1)
            def _():
                o[...] = acc[...].astype(out_dtype)

    return pl.pallas_call(
        kern, name=name, grid=(N // tn, M // tm, nk),
        in_specs=in_specs, out_specs=pl.BlockSpec((tm, tn), lambda j, i, k: (i, j)),
        out_shape=SDS((M, N), out_dtype),
        scratch_shapes=[pltpu.VMEM((tm, tn), f32)] if nk > 1 else [],
        compiler_params=pltpu.CompilerParams(dimension_semantics=("parallel", "parallel", "arbitrary")),
    )(*args)


def mm(name, grid, pairs, dims, o_spec, out_shape):
    nk = grid[2]
    npairs = len(pairs)
    in_specs, args = [], []
    for a, a_spec, b, b_spec in pairs:
        in_specs += [a_spec, b_spec]
        args += [a, b]
    blk = tuple(d for d in o_spec.block_shape if d is not None)

    def kern(*refs):
        o = refs[2 * npairs]
        part = None
        for p in range(npairs):
            d = _bdot(refs[2 * p][...], refs[2 * p + 1][...], dims)
            part = d if part is None else part + d
        if nk == 1:
            o[...] = part.astype(o.dtype)
        else:
            acc = refs[2 * npairs + 1]
            k = pl.program_id(2)

            @pl.when(k == 0)
            def _():
                acc[...] = part

            @pl.when(k > 0)
            def _():
                acc[...] += part

            @pl.when(k == nk - 1)
            def _():
                o[...] = acc[...].astype(o.dtype)

    return pl.pallas_call(
        kern, name=name, grid=grid, in_specs=in_specs, out_specs=o_spec, out_shape=out_shape,
        scratch_shapes=[pltpu.VMEM(blk, f32)] if nk > 1 else [],
        compiler_params=pltpu.CompilerParams(dimension_semantics=("parallel", "parallel", "arbitrary")),
    )(*args)


def mm_fused(name, n_row_blocks, pairs, dims, extra_ins, outs, accs, epilogue):
    npairs, nx, no, na = len(pairs), len(extra_ins), len(outs), len(accs)
    in_specs, args = [], []
    for a, a_spec, b, b_spec in pairs:
        in_specs += [a_spec, b_spec]
        args += [a, b]
    for arr, spec in extra_ins:
        in_specs.append(spec)
        args.append(arr)

    def kern(*refs):
        part = None
        for p in range(npairs):
            d = _bdot(refs[2 * p][...], refs[2 * p + 1][...], dims)
            part = d if part is None else part + d
        x_refs = refs[2 * npairs:2 * npairs + nx]
        o_refs = refs[2 * npairs + nx:2 * npairs + nx + no]
        a_refs = refs[2 * npairs + nx + no:]
        if na:
            @pl.when(pl.program_id(0) == 0)
            def _():
                for a in a_refs:
                    a[...] = jnp.zeros(a.shape, f32)
        epilogue(part, x_refs, o_refs, a_refs)

    return pl.pallas_call(
        kern, name=name, grid=(n_row_blocks,), in_specs=in_specs,
        out_specs=[spec for _, spec in outs] + [pl.BlockSpec(tuple(s), lambda i, nd=len(s): (0,) * nd) for s in accs],
        out_shape=[s for s, _ in outs] + [SDS(tuple(s), f32) for s in accs],
        compiler_params=pltpu.CompilerParams(dimension_semantics=("arbitrary",)),
    )(*args)


FF_SH = D_FF // 4


def ffn_up(name, T, tm, n_bf, wg4, wu4, l):
    def kern(n_ref, wg_ref, wu_ref, g_ref, u_ref, a_ref):
        n = n_ref[...]
        g = jnp.dot(n, wg_ref[...], preferred_element_type=f32)
        u = jnp.dot(n, wu_ref[...], preferred_element_type=f32)
        g_ref[...] = g.astype(bf16)
        u_ref[...] = u.astype(bf16)
        a_ref[...] = (_silu(g) * u).astype(bf16)
    w_spec = pl.BlockSpec((None, D, FF_SH), lambda k, i: (k, l, 0))
    o_spec = pl.BlockSpec((None, tm, FF_SH), lambda k, i: (k, i, 0))
    s = SDS((4, T, FF_SH), bf16)
    return pl.pallas_call(kern, name=name, grid=(4, T // tm), in_specs=[pl.BlockSpec((tm, D), lambda k, i: (i, 0)), w_spec, w_spec],
                          out_specs=[o_spec] * 3, out_shape=[s, s, s],
                          compiler_params=pltpu.CompilerParams(dimension_semantics=("parallel", "parallel")))(n_bf, wg4, wu4)


def ffn_dgu(name, T, tm, d_f, wd4, gate4, up4, l):
    def kern(df_ref, wd_ref, g_ref, u_ref, dg_ref, du_ref):
        dact = _bdot(df_ref[...], wd_ref[...], NT)
        _, vjp = jax.vjp(lambda a, b: _silu(a) * b, g_ref[...].astype(f32), u_ref[...].astype(f32))
        dg, du = vjp(dact)
        dg_ref[...] = dg.astype(bf16)
        du_ref[...] = du.astype(bf16)
    a_spec = pl.BlockSpec((None, tm, FF_SH), lambda k, i: (k, i, 0))
    s = SDS((4, T, FF_SH), bf16)
    return pl.pallas_call(kern, name=name, grid=(4, T // tm),
                          in_specs=[pl.BlockSpec((tm, D), lambda k, i: (i, 0)), pl.BlockSpec((None, FF_SH, D), lambda k, i: (k, l, 0)), a_spec, a_spec],
                          out_specs=[a_spec] * 2, out_shape=[s, s],
                          compiler_params=pltpu.CompilerParams(dimension_semantics=("parallel", "parallel")))(d_f, wd4, gate4, up4)


def rowcall(name, body, T, tm, ins, outs, accs=(), scratch=(), reverse=False):
    n = T // tm
    assert T % tm == 0

    def blk(i):
        return (n - 1 - i) if reverse else i

    in_specs, args = [], []
    for spec in ins:
        kind, arr = spec[0], spec[1]
        if kind == "row":
            _, _, w, cb = spec
            in_specs.append(pl.BlockSpec((tm, w), lambda i, cb=cb: (blk(i), cb)))
        elif kind == "prev":
            _, _, w, cb, h = spec
            r = tm // h
            in_specs.append(pl.BlockSpec((h, w), lambda i, cb=cb, r=r: (jnp.maximum(blk(i) * r - 1, 0), cb)))
        elif kind == "next":
            _, _, w, cb, h = spec
            r = tm // h
            in_specs.append(pl.BlockSpec((h, w), lambda i, cb=cb, r=r, h=h: (jnp.minimum((blk(i) + 1) * r, T // h - 1), cb)))
        else:
            nd = arr.ndim
            in_specs.append(pl.BlockSpec(arr.shape, lambda i, nd=nd: (0,) * nd))
        args.append(arr)
    out_shape = [SDS((T, w), dt) for w, dt in outs] + [SDS(tuple(s), f32) for s in accs]
    out_specs = [pl.BlockSpec((tm, w), lambda i: (blk(i), 0)) for w, _ in outs]
    out_specs += [pl.BlockSpec(tuple(s), lambda i, nd=len(s): (0,) * nd) for s in accs]
    ni, no, na = len(ins), len(outs), len(accs)

    def kern(*refs):
        i = pl.program_id(0)
        in_refs, out_refs = refs[:ni], refs[ni:ni + no]
        acc_refs, scr = refs[ni + no:ni + no + na], refs[ni + no + na:]
        if na:
            @pl.when(i == 0)
            def _():
                for a in acc_refs:
                    a[...] = jnp.zeros(a.shape, f32)
        body(blk(i), n, in_refs, out_refs, acc_refs, scr)

    res = pl.pallas_call(
        kern, name=name, grid=(n,), in_specs=in_specs, out_specs=out_specs, out_shape=out_shape,
        scratch_shapes=list(scratch),
        compiler_params=pltpu.CompilerParams(dimension_semantics=("arbitrary",)),
    )(*args)
    return res


def rms_to_bf16(name, T, tm, x, g):
    def body(i, n, ins, outs, accs, scr):
        outs[0][...] = _rms(ins[0][...], ins[1][...]).astype(bf16)
    return rowcall(name, body, T, tm, [("row", x, D, 0), ("const", g)], [(D, bf16)])[0]


def resid_norm(name, T, tm, h_in, f, g_post, g_pre):
    def body(i, n, ins, outs, accs, scr):
        h = ins[0][...] + _rms(ins[1][...], ins[2][...])
        outs[0][...] = h
        if g_pre is not None:
            outs[1][...] = _rms(h, ins[3][...]).astype(bf16)
    ins = [("row", h_in, D, 0), ("row", f, D, 0), ("const", g_post)] + ([("const", g_pre)] if g_pre is not None else [])
    outs = [(D, f32)] + ([(D, bf16)] if g_pre is not None else [])
    return rowcall(name, body, T, tm, ins, outs)


def swiglu_act(name, T, tm, gate, up):
    def body(i, n, ins, outs, accs, scr):
        outs[0][...] = (_silu(ins[0][...]) * ins[1][...]).astype(bf16)
    return rowcall(name, body, T, tm, [("row", gate, D_FF, 0), ("row", up, D_FF, 0)], [(D_FF, bf16)])[0]


def swiglu_bwd(name, T, tm, gate, up, d_act):
    def body(i, n, ins, outs, accs, scr):
        _, vjp = jax.vjp(lambda a, b: _silu(a) * b, ins[0][...], ins[1][...])
        dg, du = vjp(ins[2][...])
        outs[0][...] = dg.astype(bf16)
        outs[1][...] = du.astype(bf16)
    return rowcall(name, body, T, tm, [("row", gate, D_FF, 0), ("row", up, D_FF, 0), ("row", d_act, D_FF, 0)],
                   [(D_FF, bf16), (D_FF, bf16)])


def final_loss_bwd(name, T, tm, h3, f2, tgt, g_post):
    def body(i, n, ins, outs, accs, scr):
        f, g = ins[1][...], ins[3][...]
        e = ins[0][...] + _rms(f, g) - ins[2][...]
        accs[0][...] += jnp.sum(jnp.sum(e * e, axis=-1, keepdims=True) * (0.5 / D), axis=0, keepdims=True)
        dh = e * (1.0 / D)
        df, dg = _rms_bwd(f, g, dh)
        outs[0][...] = dh
        outs[1][...] = df.astype(bf16)
        accs[1][...] += dg
    return rowcall(name, body, T, tm, [("row", h3, D, 0), ("row", f2, D, 0), ("row", tgt, D, 0), ("const", g_post)],
                   [(D, f32), (D, bf16)], accs=[(1, 1), (1, D)])


def bwd_pre_post(name, T, tm, h_out, f, d_res, d_n, g_pre, g_post, df_dtype):
    def body(i, n, ins, outs, accs, scr):
        dx, dgp = _rms_bwd(ins[0][...], ins[4][...], ins[3][...])
        dh = ins[2][...] + dx
        df, dgq = _rms_bwd(ins[1][...], ins[5][...], dh)
        outs[0][...] = dh
        outs[1][...] = df.astype(df_dtype)
        accs[0][...] += dgp
        accs[1][...] += dgq
    return rowcall(name, body, T, tm,
                   [("row", h_out, D, 0), ("row", f, D, 0), ("row", d_res, D, 0), ("row", d_n, D, 0), ("const", g_pre), ("const", g_post)],
                   [(D, f32), (D, df_dtype)], accs=[(1, D), (1, D)])


def bwd_post(name, T, tm, f, d_h, g_post):
    def body(i, n, ins, outs, accs, scr):
        df, dg = _rms_bwd(ins[0][...], ins[2][...], ins[1][...])
        outs[0][...] = df.astype(bf16)
        accs[0][...] += dg
    return rowcall(name, body, T, tm, [("row", f, D, 0), ("row", d_h, D, 0), ("const", g_post)], [(D, bf16)], accs=[(1, D)])


def bwd_pre(name, T, tm, h, d_res, d_n, g_pre):
    def body(i, n, ins, outs, accs, scr):
        dx, dg = _rms_bwd(ins[0][...], ins[3][...], ins[2][...])
        outs[0][...] = ins[1][...] + dx
        accs[0][...] += dg
    return rowcall(name, body, T, tm, [("row", h, D, 0), ("row", d_res, D, 0), ("row", d_n, D, 0), ("const", g_pre)],
                   [(D, f32)], accs=[(1, D)])


def _layer_norm_parts(x):
    mu = jnp.mean(x, axis=-1, keepdims=True)
    xc = x - mu
    r = lax.rsqrt(jnp.mean(xc * xc, axis=-1, keepdims=True) + EPS)
    return xc * r, r


def gmlp_fwd(name, T, tm, uvz, ln_g, ln_b, wm, bs):
    def body(i, n, ins, outs, accs, scr):
        gu = jax.nn.gelu(ins[0][...])
        xh, _ = _layer_norm_parts(jax.nn.gelu(ins[1][...]))
        vln = (xh * ins[2][...] + ins[3][...]).astype(bf16)
        for c in range(tm // CHUNK):
            rows = slice(c * CHUNK, (c + 1) * CHUNK)
            for h in range(GM_HEADS):
                cols = slice(h * GM_HD, (h + 1) * GM_HD)
                mixed = jnp.dot(ins[4][h], vln[rows, cols], preferred_element_type=f32) + ins[5][h]
                outs[0][rows, cols] = (gu[rows, cols] * mixed).astype(bf16)
    return rowcall(name, body, T, tm, [("row", uvz, D, 0), ("row", uvz, D, 1), ("const", ln_g), ("const", ln_b), ("const", wm), ("const", bs)],
                   [(D, bf16)])[0]


def gmlp_bwd(name, T, tm, uvz, d_ya, ln_g, ln_b, wm, bs):
    def body(i, n, ins, outs, accs, scr):
        u, v, dya = ins[0][...], ins[1][...], ins[2][...]
        gu, gelu_u_vjp = jax.vjp(jax.nn.gelu, u)
        gv, gelu_v_vjp = jax.vjp(jax.nn.gelu, v)
        xh, r = _layer_norm_parts(gv)
        lng = ins[3][...]
        vln = (xh * lng + ins[4][...]).astype(bf16)
        rr = lax.broadcasted_iota(jnp.int32, (CHUNK, CHUNK), 0)
        cc = lax.broadcasted_iota(jnp.int32, (CHUNK, CHUNK), 1)
        causal = (rr >= cc).astype(f32)
        dvln_ref = scr[0]
        dgu_ref = scr[1]
        for c in range(tm // CHUNK):
            rows = slice(c * CHUNK, (c + 1) * CHUNK)
            for h in range(GM_HEADS):
                cols = slice(h * GM_HD, (h + 1) * GM_HD)
                w = ins[5][h]
                blk = vln[rows, cols]
                mixed = jnp.dot(w, blk, preferred_element_type=f32) + ins[6][h]
                dy = dya[rows, cols]
                dgu_ref[rows, cols] = dy * mixed
                dm = dy * gu[rows, cols]
                accs[3][h] += jnp.sum(dm, axis=1, keepdims=True)
                accs[2][h] += _bdot(dm, blk, NT) * causal
                dvln_ref[rows, cols] = _bdot(w, dm, TN)
        dvln = dvln_ref[...]
        accs[0][...] += jnp.sum(dvln * xh, axis=0, keepdims=True)
        accs[1][...] += jnp.sum(dvln, axis=0, keepdims=True)
        dxh = dvln * lng
        dgv = r * (dxh - jnp.mean(dxh, axis=-1, keepdims=True) - xh * jnp.mean(dxh * xh, axis=-1, keepdims=True))
        outs[0][...] = gelu_u_vjp(dgu_ref[...])[0].astype(bf16)
        outs[1][...] = gelu_v_vjp(dgv)[0].astype(bf16)
    return rowcall(name, body, T, tm,
                   [("row", uvz, D, 0), ("row", uvz, D, 1), ("row", d_ya, D, 0), ("const", ln_g), ("const", ln_b), ("const", wm), ("const", bs)],
                   [(D, bf16), (D, bf16)], accs=[(1, D), (1, D), (GM_HEADS, CHUNK, CHUNK), (GM_HEADS, CHUNK, 1)],
                   scratch=[pltpu.VMEM((tm, D), f32), pltpu.VMEM((tm, D), f32)])


CONV_RC, CONV_LB = 32, 512


def _conv_fill(i, x_ref, halo_ref, scr, tm):
    scr[pl.ds(0, CONV_HALO), :] = jnp.where(i > 0, halo_ref[...], 0.0)
    scr[pl.ds(CONV_HALO, tm), :] = x_ref[...]


def _conv_taps(ext):
    return [ext[CONV_HALO - (CONV_K - 1) + k:CONV_HALO - (CONV_K - 1) + k + CONV_RC] for k in range(CONV_K)]


def conv_fwd(name, T, tm, xbc, conv_w, conv_b):
    def body(i, n, ins, outs, accs, scr):
        s = scr[0]
        _conv_fill(i, ins[0], ins[1], s, tm)
        for lb in range(CONV_DIM // CONV_LB):
            lanes = slice(lb * CONV_LB, (lb + 1) * CONV_LB)
            w, b = ins[2][:, lanes], ins[3][:, lanes]

            def step(j, carry):
                r0 = pl.multiple_of(j * CONV_RC, CONV_RC)
                taps = _conv_taps(s[pl.ds(r0, CONV_RC + CONV_HALO), lanes])
                pre = b + sum(w[k:k + 1] * taps[k] for k in range(CONV_K))
                outs[0][pl.ds(r0, CONV_RC), lanes] = _silu(pre)
                return carry
            lax.fori_loop(0, tm // CONV_RC, step, 0)
    return rowcall(name, body, T, tm, [("row", xbc, CONV_DIM, 0), ("prev", xbc, CONV_DIM, 0, CONV_HALO), ("const", conv_w), ("const", conv_b)],
                   [(CONV_DIM, f32)], scratch=[pltpu.VMEM((tm + CONV_HALO, CONV_DIM), f32)])[0]


def conv_bwd_pre(name, T, tm, xbc, d_xc, conv_w, conv_b):
    def body(i, n, ins, outs, accs, scr):
        s = scr[0]
        _conv_fill(i, ins[0], ins[1], s, tm)
        fold = lambda v: jnp.sum(v.reshape(CONV_RC // 8, 8, CONV_LB), axis=0)
        for lb in range(CONV_DIM // CONV_LB):
            lanes = slice(lb * CONV_LB, (lb + 1) * CONV_LB)
            w, b = ins[3][:, lanes], ins[4][:, lanes]

            def step(j, carry):
                r0 = pl.multiple_of(j * CONV_RC, CONV_RC)
                taps = _conv_taps(s[pl.ds(r0, CONV_RC + CONV_HALO), lanes])
                pre = b + sum(w[k:k + 1] * taps[k] for k in range(CONV_K))
                _, vjp = jax.vjp(_silu, pre)
                dpre = vjp(ins[2][pl.ds(r0, CONV_RC), lanes])[0]
                outs[0][pl.ds(r0, CONV_RC), lanes] = dpre
                return tuple(carry[k] + fold(dpre * taps[k]) for k in range(CONV_K)) + (carry[CONV_K] + fold(dpre),)
            zero = jnp.zeros((8, CONV_LB), f32)
            sums = lax.fori_loop(0, tm // CONV_RC, step, (zero,) * (CONV_K + 1))
            for k in range(CONV_K):
                accs[0][pl.ds(k, 1), lanes] += jnp.sum(sums[k], axis=0, keepdims=True)
            accs[1][:, lanes] += jnp.sum(sums[CONV_K], axis=0, keepdims=True)
    return rowcall(name, body, T, tm,
                   [("row", xbc, CONV_DIM, 0), ("prev", xbc, CONV_DIM, 0, CONV_HALO), ("row", d_xc, CONV_DIM, 0), ("const", conv_w), ("const", conv_b)],
                   [(CONV_DIM, f32)], accs=[(CONV_K, CONV_DIM), (1, CONV_DIM)], scratch=[pltpu.VMEM((tm + CONV_HALO, CONV_DIM), f32)])


def conv_bwd_x(name, T, tm, d_pre, conv_w):
    def body(i, n, ins, outs, accs, scr):
        s = scr[0]
        s[pl.ds(0, tm), :] = ins[0][...]
        s[pl.ds(tm, CONV_HALO), :] = jnp.where(i < n - 1, ins[1][...], 0.0)
        for lb in range(CONV_DIM // CONV_LB):
            lanes = slice(lb * CONV_LB, (lb + 1) * CONV_LB)
            w = ins[2][:, lanes]

            def step(j, carry):
                r0 = pl.multiple_of(j * CONV_RC, CONV_RC)
                ext = s[pl.ds(r0, CONV_RC + CONV_HALO), lanes]
                dx = sum(w[k:k + 1] * ext[CONV_K - 1 - k:CONV_K - 1 - k + CONV_RC] for k in range(CONV_K))
                outs[0][pl.ds(r0, CONV_RC), lanes] = dx.astype(bf16)
                return carry
            lax.fori_loop(0, tm // CONV_RC, step, 0)
    return rowcall(name, body, T, tm, [("row", d_pre, CONV_DIM, 0), ("next", d_pre, CONV_DIM, 0, CONV_HALO), ("const", conv_w)],
                   [(CONV_DIM, bf16)], scratch=[pltpu.VMEM((tm + CONV_HALO, CONV_DIM), f32)])[0]


def _ssd_prep(dtr, dtb, alog):
    rr = lax.broadcasted_iota(jnp.int32, (CHUNK, CHUNK), 0)
    cc = lax.broadcasted_iota(jnp.int32, (CHUNK, CHUNK), 1)
    dt = _softplus(dtr + dtb)
    dA = dt * -jnp.exp(alog)
    acum = jnp.dot((rr >= cc).astype(f32), dA, precision=HI, preferred_element_type=f32)
    return dt, acum, acum.T, jnp.sum(dA, axis=0, keepdims=True)


def _ssd_group(g, x, Bm, Cm, S, dt, acum, acumT, tot, dsk):
    rr = lax.broadcasted_iota(jnp.int32, (CHUNK, CHUNK), 0)
    cc = lax.broadcasted_iota(jnp.int32, (CHUNK, CHUNK), 1)
    tril = rr >= cc
    lane = lax.broadcasted_iota(jnp.int32, (1, DT_PAD), 1)
    sub = lax.broadcasted_iota(jnp.int32, (DT_PAD, 1), 0)
    glane = lax.broadcasted_iota(jnp.int32, (1, SSM_HPG * SSM_P), 1) // SSM_P
    hm = [(glane == r).astype(f32) for r in range(SSM_HPG)]
    pick = lambda v, r: jnp.sum(v * (lane == SSM_HPG * g + r).astype(f32), axis=1, keepdims=True)
    cols = [pick(acum, r) for r in range(SSM_HPG)]
    tots = [pick(tot, r) for r in range(SSM_HPG)]
    spread = lambda vals: sum(vals[r] * hm[r] for r in range(SSM_HPG))
    xdt = x * spread([pick(dt, r) for r in range(SSM_HPG)])
    cb = _bdot(Cm, Bm, NT)
    y = x * spread([pick(dsk, r) for r in range(SSM_HPG)])
    for r in range(SSM_HPG):
        row = jnp.sum(acumT * (sub == SSM_HPG * g + r).astype(f32), axis=0, keepdims=True)
        dec = jnp.exp(jnp.where(tril, cols[r] - row, -jnp.inf))
        y = y + _bdot(cb * dec, xdt * hm[r])
    y = y + _bdot(Cm, S) * spread([jnp.exp(c) for c in cols])
    dte = spread([jnp.exp(tots[r] - cols[r]) for r in range(SSM_HPG)])
    s_new = S * spread([jnp.exp(t) for t in tots]) + _bdot(Bm, xdt * dte, TN)
    return y, s_new


def _ssd_ins(xc, dtr):
    gw = SSM_HPG * SSM_P
    ins = [("row", xc, gw, g) for g in range(SSM_GROUPS)]
    ins += [("row", xc, SSM_N, D // SSM_N + g) for g in range(SSM_GROUPS)]
    ins += [("row", xc, SSM_N, D // SSM_N + SSM_GROUPS + g) for g in range(SSM_GROUPS)]
    ins += [("row", dtr, DT_PAD, 0)]
    return ins


SSD_CPS = 2


def ssd_fwd(name, T, xc, dtr, dtb, alog, dsk):
    gw = SSM_HPG * SSM_P

    def body(i, n, ins, outs, accs, scr):
        S = scr[0]

        @pl.when(i == 0)
        def _():
            S[...] = jnp.zeros(S.shape, f32)
        S4 = tuple(S[:, g * gw:(g + 1) * gw] for g in range(4))
        for c in range(SSD_CPS):
            rows = pl.ds(c * CHUNK, CHUNK)
            X4 = tuple(ins[g][rows, :] for g in range(4))
            B4 = tuple(ins[4 + g][rows, :] for g in range(4))
            C4 = tuple(ins[8 + g][rows, :] for g in range(4))
            prep = _ssd_prep(ins[12][rows, :], ins[13][...], ins[14][...])
            nxt = []
            for g in range(4):
                outs[1][rows, g * gw:(g + 1) * gw] = S4[g]
                y, s_new = _ssd_group(g, X4[g], B4[g], C4[g], S4[g], *prep, ins[15][...])
                outs[0][rows, g * gw:(g + 1) * gw] = y
                nxt.append(s_new)
            S4 = tuple(nxt)
        for g in range(4):
            S[:, g * gw:(g + 1) * gw] = S4[g]
    ins = _ssd_ins(xc, dtr) + [("const", dtb), ("const", alog), ("const", dsk)]
    return rowcall(name, body, T, SSD_CPS * CHUNK, ins, [(D, f32), (D, f32)], scratch=[pltpu.VMEM((SSM_N, D), f32)])


def ssd_bwd(name, T, xc, dtr, sprev, d_y, dtb, alog, dsk):
    gw = SSM_HPG * SSM_P

    def body(i, n, ins, outs, accs, scr):
        dS = scr[0]

        @pl.when(i == n - 1)
        def _():
            dS[...] = jnp.zeros(dS.shape, f32)
        dS4 = tuple(dS[:, g * gw:(g + 1) * gw] for g in range(4))
        for c in reversed(range(SSD_CPS)):
            rows = pl.ds(c * CHUNK, CHUNK)
            X4 = tuple(ins[g][rows, :] for g in range(4))
            B4 = tuple(ins[4 + g][rows, :] for g in range(4))
            C4 = tuple(ins[8 + g][rows, :] for g in range(4))
            S4 = tuple(ins[13 + g][rows, :] for g in range(4))
            dY4 = tuple(ins[17 + g][rows, :] for g in range(4))
            dsk = ins[23][...]
            prep, prep_vjp = jax.vjp(_ssd_prep, ins[12][rows, :], ins[21][...], ins[22][...])
            d_prep, ddsk, nxt = None, None, []
            for g in range(4):
                _, gvjp = jax.vjp(functools.partial(_ssd_group, g), X4[g], B4[g], C4[g], S4[g], *prep, dsk)
                dx, dB, dC, dSp, *dp, dk = gvjp((dY4[g], dS4[g]))
                outs[0][rows, g * gw:(g + 1) * gw] = dx
                outs[0][rows, D + g * SSM_N:D + (g + 1) * SSM_N] = dB
                outs[0][rows, D + (SSM_GROUPS + g) * SSM_N:D + (SSM_GROUPS + g + 1) * SSM_N] = dC
                nxt.append(dSp)
                d_prep = dp if d_prep is None else [a + b for a, b in zip(d_prep, dp)]
                ddsk = dk if ddsk is None else ddsk + dk
            dS4 = tuple(nxt)
            ddtr, ddtb, dalog = prep_vjp(tuple(d_prep))
            outs[1][rows, :] = ddtr.astype(bf16)
            accs[0][...] += ddtb
            accs[1][...] += dalog
            accs[2][...] += ddsk
        for g in range(4):
            dS[:, g * gw:(g + 1) * gw] = dS4[g]
    ins = _ssd_ins(xc, dtr) + [("row", sprev, gw, g) for g in range(4)] + [("row", d_y, gw, g) for g in range(4)]
    ins += [("const", dtb), ("const", alog), ("const", dsk)]
    return rowcall(name, body, T, SSD_CPS * CHUNK, ins, [(CONV_DIM, f32), (DT_PAD, bf16)], accs=[(1, DT_PAD)] * 3,
                   scratch=[pltpu.VMEM((SSM_N, D), f32)], reverse=True)


def _gate_group(y, z, g):
    return _rms(y * _silu(z), g)


def gate_fwd(name, T, tm, y, uvz, gn):
    def body(i, n, ins, outs, accs, scr):
        for g in range(SSM_GROUPS):
            cols = slice(g * 256, (g + 1) * 256)
            outs[0][:, cols] = _gate_group(ins[0][:, cols], ins[1][:, cols], ins[2][:, cols]).astype(bf16)
    return rowcall(name, body, T, tm, [("row", y, D, 0), ("row", uvz, D, 2), ("const", gn)], [(D, bf16)])[0]


def gate_bwd(name, T, tm, y, uvz, d_yb, gn):
    def body(i, n, ins, outs, accs, scr):
        for g in range(SSM_GROUPS):
            cols = slice(g * 256, (g + 1) * 256)
            _, vjp = jax.vjp(_gate_group, ins[0][:, cols], ins[1][:, cols], ins[3][:, cols])
            dy, dz, dg = vjp(ins[2][:, cols])
            outs[0][:, cols] = dy
            outs[1][:, cols] = dz.astype(bf16)
            accs[0][:, cols] += dg
    return rowcall(name, body, T, tm, [("row", y, D, 0), ("row", uvz, D, 2), ("row", d_yb, D, 0), ("const", gn)],
                   [(D, f32), (D, bf16)], accs=[(1, D)])


def _pool_diff(i, tm, h_ref, halo_ref, g_ref, scr):
    g = g_ref[...]
    yn = _rms(h_ref[...], g)
    scr[pl.ds(0, POOL_HALO), :] = jnp.where(i > 0, _rms(halo_ref[...], g), 0.0)
    scr[pl.ds(POOL_HALO, tm), :] = yn
    pos = (i * tm + lax.broadcasted_iota(jnp.int32, (tm, 1), 0) + 1).astype(f32)
    parts = []
    for gi, win in enumerate(POOL_WINDOWS):
        cols = slice(gi * POOL_GD, (gi + 1) * POOL_GD)
        s = scr[pl.ds(POOL_HALO, tm), cols]
        for j in range(1, win):
            s = s + scr[pl.ds(POOL_HALO - j, tm), cols]
        parts.append(s / jnp.minimum(pos, float(win)) - yn[:, cols])
    return parts


def pool_fwd(name, T, tm, h2, g_pre, pw, pb, psc, g_post, g_next):
    def body(i, n, ins, outs, accs, scr):
        parts = _pool_diff(i, tm, ins[0], ins[1], ins[2], scr[0])
        for gi in range(len(POOL_WINDOWS)):
            cols = slice(gi * POOL_GD, (gi + 1) * POOL_GD)
            o = _bdot(parts[gi], ins[3][gi]) + ins[4][:, cols]
            outs[0][:, cols] = o * ins[5][:, cols]
        h = ins[0][...] + _rms(outs[0][...], ins[6][...])
        outs[1][...] = h
        outs[2][...] = _rms(h, ins[7][...]).astype(bf16)
    return rowcall(name, body, T, tm, [("row", h2, D, 0), ("prev", h2, D, 0, POOL_HALO), ("const", g_pre), ("const", pw), ("const", pb), ("const", psc),
                                       ("const", g_post), ("const", g_next)],
                   [(D, f32), (D, f32), (D, bf16)], scratch=[pltpu.VMEM((tm + POOL_HALO, D), f32)])


def pool_bwd(name, T, tm, h2, d_pm, d_res, g_pre, pw, pb, psc, f_prev, g_prev):
    def body(i, n, ins, outs, accs, scr):
        parts = _pool_diff(i, tm, ins[0], ins[1], ins[5], scr[0])
        dpm = ins[2][...]
        psc_v = ins[8][...]
        dps = dpm * psc_v
        dps_halo = jnp.where(i < n - 1, ins[3][...] * psc_v, 0.0)
        accs[1][...] += jnp.sum(dps, axis=0, keepdims=True)
        pos = (i * tm + lax.broadcasted_iota(jnp.int32, (tm, 1), 0) + 1).astype(f32)
        pos_h = ((i + 1) * tm + lax.broadcasted_iota(jnp.int32, (POOL_HALO, 1), 0) + 1).astype(f32)
        r_scr = scr[1]
        dyn_scr = scr[2]
        for gi, win in enumerate(POOL_WINDOWS):
            cols = slice(gi * POOL_GD, (gi + 1) * POOL_GD)
            w = ins[6][gi]
            o = _bdot(parts[gi], w) + ins[7][:, cols]
            accs[2][:, cols] += jnp.sum(dpm[:, cols] * o, axis=0, keepdims=True)
            accs[0][gi] += _bdot(parts[gi], dps[:, cols], TN)
            q = _bdot(dps[:, cols], w, NT)
            qh = _bdot(dps_halo[:, cols], w, NT)
            r_scr[pl.ds(0, tm), cols] = q / jnp.minimum(pos, float(win))
            r_scr[pl.ds(tm, POOL_HALO), cols] = qh / jnp.minimum(pos_h, float(win))
            s = r_scr[pl.ds(0, tm), cols]
            for j in range(1, win):
                s = s + r_scr[pl.ds(j, tm), cols]
            dyn_scr[:, cols] = s - q
        dx, dg = _rms_bwd(ins[0][...], ins[5][...], dyn_scr[...])
        dh = ins[4][...] + dx
        outs[0][...] = dh
        accs[3][...] += dg
        df, dgp = _rms_bwd(ins[9][...], ins[10][...], dh)
        outs[1][...] = df.astype(bf16)
        accs[4][...] += dgp
    ins = [("row", h2, D, 0), ("prev", h2, D, 0, POOL_HALO), ("row", d_pm, D, 0), ("next", d_pm, D, 0, POOL_HALO), ("row", d_res, D, 0),
           ("const", g_pre), ("const", pw), ("const", pb), ("const", psc), ("row", f_prev, D, 0), ("const", g_prev)]
    return rowcall(name, body, T, tm, ins, [(D, f32), (D, bf16)], accs=[(4, POOL_GD, POOL_GD), (1, D), (1, D), (1, D), (1, D)],
                   scratch=[pltpu.VMEM((tm + POOL_HALO, D), f32), pltpu.VMEM((tm + POOL_HALO, D), f32), pltpu.VMEM((tm, D), f32)])


def local_step(T, x, tgt, W, ffn_weights):
    tm = 512 if T >= 1024 else T // 2
    TKW = 2048 if T >= 2048 else T
    ng = W["norm_g"]
    g = lambda l, j: ng[l, j][None, :]
    G = {}

    row_spec = pl.BlockSpec((tm, D), lambda j, i, k: (i, 0))
    tf = tm // 2 if T >= 1024 else tm
    rows_f = pl.BlockSpec((tf, D), lambda i: (i, 0))
    vec_f = pl.BlockSpec((1, D), lambda i: (0, 0))
    sh_f = [pl.BlockSpec((None, tf, FF_SH), lambda i, s=s: (s, i, 0)) for s in range(4)]
    sh_spec = [pl.BlockSpec((None, tm, FF_SH), lambda j, i, k, s=s: (s, i, 0)) for s in range(4)]
    out_f32, out_bf16 = (SDS((T, D), f32), rows_f), (SDS((T, D), bf16), rows_f)

    def resid_epilogue(with_pre):
        def ep(part, xs, os, accs):
            h = xs[0][...] + _rms(part, xs[1][...])
            os[0][...] = part
            os[1][...] = h
            if with_pre:
                os[2][...] = _rms(h, xs[2][...]).astype(bf16)
        return ep

    def bwd_epilogue(df_dtype):
        def ep(part, xs, os, accs):
            dx, dgp = _rms_bwd(xs[0][...], xs[3][...], part)
            dh = xs[2][...] + dx
            df, dgq = _rms_bwd(xs[1][...], xs[4][...], dh)
            os[0][...] = dh
            os[1][...] = df.astype(df_dtype)
            accs[0][...] += dgp
            accs[1][...] += dgq
        return ep

    def ffn_fwd(tag, n_bf, l, resid=None):
        gate4, up4, act4 = ffn_up(f"ffn{tag}_up", T, tm, n_bf, W["wg4"], W["wu4"], l)
        if resid is None:
            wd_spec = [pl.BlockSpec((None, FF_SH, D), lambda j, i, k, s=s: (s, l, 0)) for s in range(4)]
            f = mm(f"ffn{tag}_down", (1, T // tm, 1), [(act4, sh_spec[s], W["wd4"], wd_spec[s]) for s in range(4)], NN, row_spec, SDS((T, D), f32))
            return gate4, up4, act4, f, None
        wd_f = [pl.BlockSpec((None, FF_SH, D), lambda i, s=s: (s, l, 0)) for s in range(4)]
        f, h_out = mm_fused(f"ffn{tag}_down", T // tf, [(act4, sh_f[s], W["wd4"], wd_f[s]) for s in range(4)], NN,
                            [(resid[0], rows_f), (resid[1], vec_f)], [out_f32, out_f32], [], resid_epilogue(False))
        return gate4, up4, act4, f, h_out

    def ffn_bwd(tag, l, n_bf, gate4, up4, act4, d_f, h_out, f_pre, d_res, g_pre, g_post, df_dtype):
        d_gate4, d_up4 = ffn_dgu(f"ffn{tag}_dgu", T, tm, d_f, W["wd4"], gate4, up4, l)
        w_f = [pl.BlockSpec((None, D, FF_SH), lambda i, s=s: (s, l, 0)) for s in range(4)]
        d_h, d_fp, dgp, dgq = mm_fused(
            f"ffn{tag}_dn", T // tf, [(d_gate4, sh_f[s], W["wg4"], w_f[s]) for s in range(4)] + [(d_up4, sh_f[s], W["wu4"], w_f[s]) for s in range(4)],
            NT, [(h_out, rows_f), (f_pre, rows_f), (d_res, rows_f), (g_pre, vec_f), (g_post, vec_f)],
            [out_f32, (SDS((T, D), df_dtype), rows_f)], [(1, D), (1, D)], bwd_epilogue(df_dtype))

        def wgrad(nm, a4, b):
            return mm(nm, (4, 1, T // TKW),
                      [(a4, pl.BlockSpec((None, TKW, FF_SH), lambda s, j, k: (s, k, 0)), b, pl.BlockSpec((TKW, D), lambda s, j, k: (k, 0)))],
                      TN, pl.BlockSpec((None, FF_SH, D), lambda s, j, k: (s, 0, 0)), SDS((4, FF_SH, D), f32))
        return d_h, d_fp, dgp, dgq, wgrad(f"ffn{tag}_dwg", d_gate4, n_bf), wgrad(f"ffn{tag}_dwu", d_up4, n_bf), wgrad(f"ffn{tag}_dwd", act4, d_f)

    y0 = rms_to_bf16("l0_prenorm", T, tm, x, g(0, 0))
    uvz = matmul("in_uvz", [(y0, W["w_uvz"])], "nn", f32, tm, 1024)
    xbc = matmul("in_xbc", [(y0, W["w_xbc"])], "nn", f32, tm, 1024)
    dtr = matmul("in_dt", [(y0, W["w_dt"])], "nn", f32, tm, DT_PAD)
    y_a = gmlp_fwd("gmlp_fwd", T, tm, uvz, W["ln_g"], W["ln_b"], W["wm"], W["bs"])
    xc = conv_fwd("conv_fwd", T, tm, xbc, W["conv_w"], W["conv_b"])
    y_ssd, sprev = ssd_fwd("ssd_fwd", T, xc, dtr, W["dtb"], W["alog"], W["dsk"])
    y_b = gate_fwd("gate_fwd", T, tm, y_ssd, uvz, W["gn"])
    half = D // 2
    wo4 = W["wo4"]
    ycol = [pl.BlockSpec((tf, half), lambda i, cb=cb: (i, cb)) for cb in range(2)]
    wo_s = [pl.BlockSpec((None, half, D), lambda i, s=s: (s, 0, 0)) for s in range(4)]
    mixo, h1, n1 = mm_fused("out_proj", T // tf, [(y_a, ycol[0], wo4, wo_s[0]), (y_a, ycol[1], wo4, wo_s[1]),
                                                  (y_b, ycol[0], wo4, wo_s[2]), (y_b, ycol[1], wo4, wo_s[3])], NN,
                            [(x, rows_f), (g(0, 1), vec_f), (g(0, 2), vec_f)], [out_f32, out_f32, out_bf16], [], resid_epilogue(True))
    W = dict(W)
    W["wg4"], W["wu4"], W["wd4"] = ffn_weights(h1)
    gate0, up0, act0, f1, h2 = ffn_fwd("0", n1, 0, (h1, g(0, 3)))
    pm, h3, n3 = pool_fwd("pool_fwd", T, tm, h2, g(1, 0), W["pool_w"], W["pool_b"], W["pool_scale"], g(1, 1), g(1, 2))
    gate1, up1, act1, f2, _ = ffn_fwd("1", n3, 1)
    dh4, d_f2, loss_acc, dg13 = final_loss_bwd("loss_bwd", T, tm, h3, f2, tgt, g(1, 3))
    d_h3, d_pm, dg12, dg11, dwg1, dwu1, dwd1 = ffn_bwd("1", 1, n3, gate1, up1, act1, d_f2, h3, pm, dh4, g(1, 2), g(1, 1), f32)
    d_h2, d_f1, G["pool_w"], G["pool_b"], G["pool_scale"], dg10, dg03 = pool_bwd("pool_bwd", T, tm, h2, d_pm, d_h3, g(1, 0), W["pool_w"], W["pool_b"],
                                                                                 W["pool_scale"], f1, g(0, 3))
    d_h1, d_mixo, dg02, dg01, dwg0, dwu0, dwd0 = ffn_bwd("0", 0, n1, gate0, up0, act0, d_f1, h1, mixo, d_h2, g(0, 2), g(0, 1), bf16)
    def d_ycat(nm, s0):
        return mm(nm, (2, T // tm, 1), [(d_mixo, row_spec, wo4, pl.BlockSpec((None, half, D), lambda j, i, k: (s0 + j, 0, 0)))], NT,
                  pl.BlockSpec((tm, half), lambda j, i, k: (i, j)), SDS((T, D), f32))

    def d_wo(nm, y):
        return mm(nm, (2, 1, T // TKW), [(y, pl.BlockSpec((TKW, half), lambda s, j, k: (k, s)), d_mixo, pl.BlockSpec((TKW, D), lambda s, j, k: (k, 0)))],
                  TN, pl.BlockSpec((None, half, D), lambda s, j, k: (s, 0, 0)), SDS((2, half, D), f32))
    d_ya, d_yb = d_ycat("out_proj_dya", 0), d_ycat("out_proj_dyb", 2)
    dwo_a, dwo_b = d_wo("out_proj_dwa", y_a), d_wo("out_proj_dwb", y_b)
    d_yssd, d_z, G["gn"] = gate_bwd("gate_bwd", T, tm, y_ssd, uvz, d_yb, W["gn"])
    d_xc, d_dtr, G["dtb"], G["alog"], G["dsk"] = ssd_bwd("ssd_bwd", T, xc, dtr, sprev, d_yssd, W["dtb"], W["alog"], W["dsk"])
    d_pre, G["conv_w"], G["conv_b"] = conv_bwd_pre("conv_bwd_pre", T, tm, xbc, d_xc, W["conv_w"], W["conv_b"])
    d_xbc = conv_bwd_x("conv_bwd_x", T, tm, d_pre, W["conv_w"])
    d_u, d_v, G["ln_g"], G["ln_b"], G["wm"], G["bs"] = gmlp_bwd("gmlp_bwd", T, tm, uvz, d_ya, W["ln_g"], W["ln_b"], W["wm"], W["bs"])
    w_u, w_v, w_z = W["w_uvz"][:, :D], W["w_uvz"][:, D:2 * D], W["w_uvz"][:, 2 * D:]
    def pre_epilogue(part, xs, os, accs):
        dx, dg = _rms_bwd(xs[0][...], xs[2][...], part)
        os[0][...] = xs[1][...] + dx
        accs[0][...] += dg
    blk = lambda w: pl.BlockSpec((tf, w), lambda i: (i, 0))
    whole = lambda a: pl.BlockSpec(a.shape, lambda i: (0, 0))
    grad_x, dg00 = mm_fused("in_dy0", T // tf, [(d_u, blk(D), w_u, whole(w_u)), (d_v, blk(D), w_v, whole(w_v)), (d_z, blk(D), w_z, whole(w_z)),
                                                (d_xbc, blk(CONV_DIM), W["w_xbc"], whole(W["w_xbc"])), (d_dtr, blk(DT_PAD), W["w_dt"], whole(W["w_dt"]))],
                            NT, [(x, rows_f), (d_h1, rows_f), (g(0, 0), vec_f)], [out_f32], [(1, D)], pre_epilogue)
    G["w_inT"] = [matmul("in_dwu", [(d_u, y0)], "tn", f32, 1024, 1024, TKW), matmul("in_dwv", [(d_v, y0)], "tn", f32, 1024, 1024, TKW),
                  matmul("in_dwz", [(d_z, y0)], "tn", f32, 1024, 1024, TKW), matmul("in_dwxbc", [(d_xbc, y0)], "tn", f32, 1024, 1024, TKW),
                  matmul("in_dwdt", [(d_dtr, y0)], "tn", f32, DT_PAD, 1024, TKW)[:N_HEADS]]
    G["norm_g"] = jnp.stack([jnp.concatenate([dg00, dg01, dg02, dg03], 0), jnp.concatenate([dg10, dg11, dg12, dg13], 0)])
    G["wo4"] = [dwo_a[0], dwo_a[1], dwo_b[0], dwo_b[1]]
    G["wgT4"], G["wuT4"], G["wd4"] = [dwg0, dwg1], [dwu0, dwu1], [dwd0, dwd1]
    return loss_acc, grad_x, G


def build_weights(Wf):
    causal = jnp.tril(jnp.ones((CHUNK, CHUNK), bool))
    w_in = Wf["w_in"].astype(bf16)
    pad16 = lambda v: jnp.pad(v.reshape(1, N_HEADS).astype(f32), ((0, 0), (0, DT_PAD - N_HEADS)))
    return {
        "norm_g": Wf["norm_g"],
        "w_uvz": w_in[:, :3 * D], "w_xbc": w_in[:, 3 * D:3 * D + CONV_DIM],
        "w_dt": jnp.pad(w_in[:, 3 * D + CONV_DIM:], ((0, 0), (0, DT_PAD - N_HEADS))),
        "ln_g": Wf["gm_ln_g"].reshape(1, D), "ln_b": Wf["gm_ln_b"].reshape(1, D),
        "wm": jnp.where(causal[None], Wf["gm_ws"], 0).astype(bf16), "bs": Wf["gm_bs"].reshape(GM_HEADS, CHUNK, 1),
        "conv_w": Wf["conv_w"], "conv_b": Wf["conv_b"].reshape(1, CONV_DIM),
        "dtb": pad16(Wf["dt_bias"]), "alog": pad16(Wf["a_log"]), "dsk": pad16(Wf["d_skip"]),
        "gn": Wf["ssm_norm_g"].reshape(1, D),
        "wo4": Wf["wo4"].astype(bf16),
        "pool_w": Wf["pool_w"].astype(bf16), "pool_b": Wf["pool_b"].reshape(1, D), "pool_scale": Wf["pool_scale"].reshape(1, D),
    }


def small_grads(G):
    return {
        "norm_g": G["norm_g"],
        "gm_ln_g": G["ln_g"].reshape(D), "gm_ln_b": G["ln_b"].reshape(D),
        "gm_ws": G["wm"], "gm_bs": G["bs"].reshape(GM_HEADS, CHUNK),
        "conv_w": G["conv_w"], "conv_b": G["conv_b"].reshape(CONV_DIM),
        "dt_bias": G["dtb"][0, :N_HEADS], "a_log": G["alog"][0, :N_HEADS], "d_skip": G["dsk"][0, :N_HEADS],
        "ssm_norm_g": G["gn"].reshape(D),
        "pool_b": G["pool_b"].reshape(4, POOL_GD), "pool_scale": G["pool_scale"].reshape(D),
    }


MESH_ID = pl.DeviceIdType.MESH
ANY = pl.BlockSpec(memory_space=pl.ANY)


DMA_CHUNK_BYTES = 2 << 20
DMA_MAX_CHUNKS = 32


def _pieces(view, axis, align):
    shape = view.shape
    nbytes = math.prod(shape) * jnp.dtype(view.dtype).itemsize
    n = max(1, min(DMA_MAX_CHUNKS, -(-nbytes // DMA_CHUNK_BYTES)))
    rows = shape[axis]
    size = -(-rows // n)
    size = -(-size // align) * align
    out = []
    for s in range(0, rows, size):
        idx = [slice(None)] * len(shape)
        idx[axis] = pl.ds(s, min(size, rows - s))
        out.append(tuple(idx))
    return out


def comm_call(name, operands, out_shapes, plan):
    n_in = len(operands)
    n_out = len(out_shapes)
    n_remote, n_local = plan((0, 0, 0), [None] * n_in, [None] * n_out, True)

    def body(*refs):
        in_refs, out_refs = refs[:n_in], refs[n_in:n_in + n_out]
        send_sems, recv_sems, local_sems = refs[n_in + n_out:]
        me = (lax.axis_index("x"), lax.axis_index("y"), lax.axis_index("c"))
        remote, local = plan(me, in_refs, out_refs, False)
        align = lambda v: 16 if v.dtype == bf16 else 8
        for j, (s, d, axis) in enumerate(local):
            for ix in _pieces(s, axis, align(s)):
                pltpu.make_async_copy(s.at[ix], d.at[ix], local_sems.at[j]).start()
        peers = [tuple((1 - m) if f else m for m, f in zip(me, flip)) for flip, *_ in remote]
        for k, (flip, src, dst, _, axis) in enumerate(remote):
            for ix in _pieces(src, axis, align(src)):
                pltpu.make_async_remote_copy(src_ref=src.at[ix], dst_ref=dst.at[ix], send_sem=send_sems.at[k], recv_sem=recv_sems.at[k],
                                             device_id=peers[k], device_id_type=MESH_ID).start()
        for k, (flip, src, dst, landing, axis) in enumerate(remote):
            pltpu.make_async_remote_copy(src_ref=landing, dst_ref=landing, send_sem=send_sems.at[k], recv_sem=recv_sems.at[k],
                                         device_id=peers[k], device_id_type=MESH_ID).wait_recv()
        for k, (flip, src, dst, landing, axis) in enumerate(remote):
            pltpu.make_async_remote_copy(src_ref=src, dst_ref=dst, send_sem=send_sems.at[k], recv_sem=recv_sems.at[k],
                                         device_id=peers[k], device_id_type=MESH_ID).wait_send()
        for j, (s, d, axis) in enumerate(local):
            pltpu.make_async_copy(s, d, local_sems.at[j]).wait()

    return pl.pallas_call(
        body, name=name, out_shape=list(out_shapes), in_specs=[ANY] * n_in, out_specs=[ANY] * n_out,
        scratch_shapes=[pltpu.SemaphoreType.DMA((n_remote,)), pltpu.SemaphoreType.DMA((n_remote,)), pltpu.SemaphoreType.DMA((max(n_local, 1),))],
    )(*operands)


CHIP_FLIPS = ((1, 0, 0), (0, 1, 0), (1, 1, 0))
PAIR_FLIP = (0, 0, 1)


def gather_over_chips(name, arrs):
    def plan(me, ins, outs, count):
        if count:
            return len(CHIP_FLIPS) * len(arrs), len(arrs)
        k = 2 * me[0] + me[1]
        remote, local = [], []
        for a in range(len(arrs)):
            for flip in CHIP_FLIPS:
                kp = 2 * ((1 - me[0]) if flip[0] else me[0]) + ((1 - me[1]) if flip[1] else me[1])
                remote.append((flip, ins[a], outs[a].at[k], outs[a].at[kp], 0))
            local.append((ins[a], outs[a].at[k], 0))
        return remote, local
    return comm_call(name, arrs, [SDS((4,) + a.shape, a.dtype) for a in arrs], plan)


def pair_split_exchange(name, p, rh):
    def plan(me, ins, outs, count):
        if count:
            return 1, 0
        theirs = ins[0].at[:, pl.ds(pl.multiple_of((1 - me[2]) * rh, 8), rh), :]
        return [(PAIR_FLIP, theirs, outs[0], outs[0], 1)], []
    return comm_call(name, [p], [SDS((4, rh, p.shape[2]), p.dtype)], plan)[0]


def scatter_over_chips(name, cs):
    def plan(me, ins, outs, count):
        if count:
            return len(CHIP_FLIPS), 0
        k = 2 * me[0] + me[1]
        remote = []
        for flip in CHIP_FLIPS:
            kp = 2 * ((1 - me[0]) if flip[0] else me[0]) + ((1 - me[1]) if flip[1] else me[1])
            remote.append((flip, ins[0].at[kp], outs[0].at[k], outs[0].at[kp], 0))
        return remote, []
    return comm_call(name, [cs], [SDS(cs.shape, cs.dtype)], plan)[0]


def pair_swap(name, half):
    def plan(me, ins, outs, count):
        if count:
            return 1, 0
        return [(PAIR_FLIP, ins[0], outs[0], outs[0], 0)], []
    return comm_call(name, [half], [SDS(half.shape, half.dtype)], plan)[0]


def _row_tile(rows, cap=512):
    if rows <= cap:
        return rows
    t = cap - cap % 8
    while rows % t:
        t -= 8
    return t


SUM_ROWS = 448


def pair_sum(name, packs, got, c_arr):
    rh = got.shape[1]
    nb = rh // SUM_ROWS

    def kern(c_ref, a_ref, b_ref, o16_ref):
        o16_ref[...] = (a_ref[...] + b_ref[...]).astype(bf16)
    blk = (None, SUM_ROWS, D)
    grid_spec = pltpu.PrefetchScalarGridSpec(
        num_scalar_prefetch=1, grid=(4, nb),
        in_specs=[pl.BlockSpec(blk, lambda s, i, c: (s, c[0] * nb + i, 0)), pl.BlockSpec(blk, lambda s, i, c: (s, i, 0))],
        out_specs=pl.BlockSpec(blk, lambda s, i, c: (s, i, 0)))
    return pl.pallas_call(kern, name=name, grid_spec=grid_spec, out_shape=SDS(got.shape, bf16),
                          compiler_params=pltpu.CompilerParams(dimension_semantics=("parallel", "parallel")))(c_arr, packs, got)


def chip_sum(name, own16, landed16, k_arr):
    rh = own16.shape[1]
    nb = rh // SUM_ROWS

    def kern(k_ref, own_ref, l0, l1, l2, l3, o_ref):
        k = k_ref[0]
        s = None
        for j, lref in enumerate((l0, l1, l2, l3)):
            t = jnp.where(k == j, own_ref[...], lref[...]).astype(f32)
            s = t if s is None else s + t
        o_ref[...] = s
    blk = (None, SUM_ROWS, D)
    land = [pl.BlockSpec(blk, lambda i, k, j=j: (jnp.where(k[0] == j, (j + 1) % N_CHIPS, j), i, 0)) for j in range(N_CHIPS)]
    grid_spec = pltpu.PrefetchScalarGridSpec(
        num_scalar_prefetch=1, grid=(nb,),
        in_specs=[pl.BlockSpec(blk, lambda i, k: (k[0], i, 0))] + land,
        out_specs=pl.BlockSpec((SUM_ROWS, D), lambda i, k: (i, 0)))
    return pl.pallas_call(kern, name=name, grid_spec=grid_spec, out_shape=SDS((rh, D), f32),
                          compiler_params=pltpu.CompilerParams(dimension_semantics=("parallel",)))(k_arr, own16, landed16, landed16, landed16, landed16)


def adamw(name, w, g, m, v):
    R, C = w.shape
    tr = _row_tile(R, 256)

    def kern(w_ref, g_ref, m_ref, v_ref, d_ref, mo_ref, vo_ref):
        gg = g_ref[...]
        mn = ADAM_B1 * m_ref[...] + (1.0 - ADAM_B1) * gg
        vn = ADAM_B2 * v_ref[...] + (1.0 - ADAM_B2) * jnp.square(gg)
        m_hat = mn / (1.0 - ADAM_B1 ** ADAM_STEP)
        v_hat = vn / (1.0 - ADAM_B2 ** ADAM_STEP)
        d_ref[...] = -ADAM_LR * (m_hat / (jnp.sqrt(v_hat) + ADAM_EPS) + ADAM_WD * w_ref[...])
        mo_ref[...] = mn
        vo_ref[...] = vn
    spec = pl.BlockSpec((tr, C), lambda i: (i, 0))
    s = SDS((R, C), f32)
    return pl.pallas_call(kern, name=name, grid=(R // tr,), in_specs=[spec] * 4, out_specs=[spec] * 3, out_shape=[s, s, s],
                          compiler_params=pltpu.CompilerParams(dimension_semantics=("parallel",)))(w, g, m, v)


WEIGHT_NAMES = ("norm_g", "w_in", "gm_ln_g", "gm_ln_b", "gm_ws", "gm_bs", "conv_w", "conv_b", "dt_bias", "a_log", "d_skip",
                "ssm_norm_g", "w_out", "pool_w", "pool_b", "pool_scale", "ffn_w_gate", "ffn_w_up", "ffn_w_down")
SMALL = ("norm_g", "conv_w", "pool_b", "pool_scale")
REPL = ("gm_ln_g", "gm_ln_b", "gm_ws", "gm_bs", "conv_b", "dt_bias", "a_log", "d_skip", "ssm_norm_g")
SMALL_AXIS = {"norm_g": 2, "conv_w": 1, "pool_b": 1, "pool_scale": 0}
N_CHIPS = 4
IN_SH = IN_DIM // N_CHIPS
SMALL_ROWS = 8
REPL_ROWS = 72
OFF_OUT, OFF_GATE, OFF_UP, OFF_DOWN = 0, 512, 512 + 2 * FF_SH, 512 + 4 * FF_SH
OFF_POOL = OFF_DOWN + 2 * FF_SH
OFF_SMALL = OFF_POOL + 64
OFF_REPL = OFF_SMALL + SMALL_ROWS
OFF_IN = OFF_REPL + REPL_ROWS
SLOT_END = OFF_IN + IN_SH
SLOT_ROWS = 6272
HALF_ROWS = SLOT_ROWS // 2


def _flat_rows(pieces, rows):
    v = jnp.concatenate([p.reshape(-1) for p in pieces])
    return jnp.pad(v, (0, rows * D - v.shape[0])).reshape(rows, D)


def _shard_small(name, full, k):
    ax = SMALL_AXIS[name]
    n = full.shape[ax] // N_CHIPS
    return lax.slice_in_dim(full, k * n, (k + 1) * n, axis=ax)


def _drop1(name, a):
    return a if name == "norm_g" else a[0]


def _row_range(blocks, lo, hi):
    out, off = [], 0
    for b in blocks:
        n = b.shape[0]
        a, e = max(lo, off), min(hi, off + n)
        if a < e:
            out.append(b[a - off:e - off])
        off += n
    return out


HBM_SPEC = pl.BlockSpec(memory_space=pltpu.HBM)
SEM_SPEC = pl.BlockSpec(memory_space=pltpu.SEMAPHORE)
SPLIT_EFFECT = pltpu.SideEffectType.DATAFLOW_SIDE_EFFECTING


def _chip_of(me, flip):
    return 2 * ((1 - me[0]) if flip[0] else me[0]) + ((1 - me[1]) if flip[1] else me[1])


def gather_start(name, arrs, after):
    n = len(arrs)
    ncp = n * len(CHIP_FLIPS)

    def body(*refs):
        srcs, lands = refs[:n], refs[n:2 * n]
        send_sems, recv_sems, token = refs[2 * n + 1], refs[2 * n + 2], refs[-1]
        me = (lax.axis_index("x"), lax.axis_index("y"), lax.axis_index("c"))
        k = 2 * me[0] + me[1]
        for a in range(n):
            for f, flip in enumerate(CHIP_FLIPS):
                peer = tuple((1 - m) if fl else m for m, fl in zip(me, flip))
                for ix in _pieces(srcs[a], 0, 16):
                    pltpu.make_async_remote_copy(src_ref=srcs[a].at[ix], dst_ref=lands[a].at[k].at[ix],
                                                 send_sem=send_sems.at[a * len(CHIP_FLIPS) + f], recv_sem=recv_sems.at[a * len(CHIP_FLIPS) + f],
                                                 device_id=peer, device_id_type=MESH_ID).start()
        token[...] = jnp.zeros_like(token)

    land_shapes = [(N_CHIPS,) + a.shape for a in arrs]
    operands = [pltpu.with_memory_space_constraint(a, pltpu.HBM) for a in arrs]
    operands += [pltpu.with_memory_space_constraint(lax.empty(s, a.dtype), pltpu.HBM) for s, a in zip(land_shapes, arrs)]
    out = pl.pallas_call(
        body, name=name,
        out_shape=(pltpu.SemaphoreType.DMA((ncp,)), pltpu.SemaphoreType.DMA((ncp,)), *[pltpu.HBM(a.shape, a.dtype) for a in arrs],
                   *[pltpu.HBM(s, a.dtype) for s, a in zip(land_shapes, arrs)], SDS((8, 128), f32)),
        in_specs=[HBM_SPEC] * (2 * n) + [ANY], out_specs=(SEM_SPEC, SEM_SPEC, *[HBM_SPEC] * (2 * n), pl.BlockSpec(memory_space=pltpu.VMEM)),
        input_output_aliases={i: 2 + i for i in range(2 * n)},
        compiler_params=pltpu.CompilerParams(has_side_effects=SPLIT_EFFECT),
    )(*operands, after)
    return out[0], out[1], out[2:2 + n], out[2 + n:2 + 2 * n], out[-1]


def gather_wait(name, send_sems, recv_sems, thru, lands, after):
    n = len(thru)

    def body(*refs):
        srcs, lands_r = refs[:n], refs[n:2 * n]
        s_sems, r_sems = refs[2 * n], refs[2 * n + 1]
        me = (lax.axis_index("x"), lax.axis_index("y"), lax.axis_index("c"))
        k = 2 * me[0] + me[1]
        for a in range(n):
            for f, flip in enumerate(CHIP_FLIPS):
                peer = tuple((1 - m) if fl else m for m, fl in zip(me, flip))
                idx = a * len(CHIP_FLIPS) + f
                pltpu.make_async_remote_copy(src_ref=srcs[a], dst_ref=lands_r[a].at[k], send_sem=s_sems.at[idx], recv_sem=r_sems.at[idx],
                                             device_id=peer, device_id_type=MESH_ID).wait_send()
                pltpu.make_async_remote_copy(src_ref=srcs[a], dst_ref=lands_r[a].at[_chip_of(me, flip)], send_sem=s_sems.at[idx],
                                             recv_sem=r_sems.at[idx], device_id=peer, device_id_type=MESH_ID).wait_recv()

    out = pl.pallas_call(
        body, name=name, out_shape=tuple(pltpu.HBM(t.shape, t.dtype) for t in (*thru, *lands)),
        in_specs=[HBM_SPEC] * (2 * n) + [SEM_SPEC, SEM_SPEC, ANY], out_specs=tuple([HBM_SPEC] * (2 * n)),
        input_output_aliases={i: i for i in range(2 * n)},
        compiler_params=pltpu.CompilerParams(has_side_effects=SPLIT_EFFECT),
    )(*thru, *lands, send_sems, recv_sems, after)
    return out[n:]


def gather_weights(w_sh):
    big = [w_sh["w_in"][0], w_sh["w_out"][0], w_sh["pool_w"][0].reshape(4 * 64, POOL_GD)]
    small_pack = _flat_rows([w_sh[n] for n in SMALL], SMALL_ROWS)
    s_in, s_out, s_pool, s_small = gather_over_chips("gather_weights", [b.astype(bf16) for b in big] + [small_pack])
    Wf = {n: w_sh[n][0] for n in REPL}
    Wf["w_in"] = s_in.transpose(1, 0, 2).reshape(D, IN_DIM)
    Wf["pool_w"] = s_pool.reshape(N_CHIPS, 4, 64, POOL_GD).transpose(1, 0, 2, 3).reshape(4, POOL_GD, POOL_GD)
    Wf["wo4"] = s_out
    small_shapes = [_drop1(n, w_sh[n]).shape for n in SMALL]
    parts = [_split_rows(s_small[k], small_shapes) for k in range(N_CHIPS)]
    for j, n in enumerate(SMALL):
        Wf[n] = jnp.concatenate([parts[k][j] for k in range(N_CHIPS)], axis=SMALL_AXIS[n])
    return Wf


def pack_grads(G):
    sg = small_grads(G)
    repl = _flat_rows([sg[n] for n in REPL], REPL_ROWS)
    w_in_t = jnp.concatenate(G["w_inT"], axis=0)
    slots = []
    for k in range(N_CHIPS):
        rows = [G["wo4"][k], G["wgT4"][0][k], G["wgT4"][1][k], G["wuT4"][0][k], G["wuT4"][1][k], G["wd4"][0][k], G["wd4"][1][k],
                G["pool_w"][:, k * 64:(k + 1) * 64, :].reshape(64, D), _flat_rows([_shard_small(n, sg[n], k) for n in SMALL], SMALL_ROWS), repl]
        rows.append(jnp.pad(w_in_t[k * IN_SH:(k + 1) * IN_SH], ((0, SLOT_ROWS - SLOT_END), (0, 0))))
        slots.append(jnp.concatenate(rows, axis=0))
    return jnp.stack(slots)


def unpack_grads(total, w_sh):
    g = {"w_out": total[OFF_OUT:OFF_GATE], "ffn_w_down": total[OFF_DOWN:OFF_POOL], "pool_w": total[OFF_POOL:OFF_SMALL],
         "ffn_w_gate": jnp.stack([total[OFF_GATE + l * FF_SH:OFF_GATE + (l + 1) * FF_SH].T for l in range(2)]),
         "ffn_w_up": jnp.stack([total[OFF_UP + l * FF_SH:OFF_UP + (l + 1) * FF_SH].T for l in range(2)]),
         "w_in": total[OFF_IN:SLOT_END].T}
    small = _split_rows(total[OFF_SMALL:OFF_REPL], [_drop1(n, w_sh[n]).shape for n in SMALL])
    repl = _split_rows(total[OFF_REPL:OFF_IN], [w_sh[n][0].shape for n in REPL])
    g.update(zip(SMALL, small))
    g.update(zip(REPL, repl))
    return {n: g[n].reshape(w_sh[n].shape) for n in WEIGHT_NAMES}


def _split_rows(flat2d, shapes):
    v = flat2d.reshape(-1)
    out, off = [], 0
    for s in shapes:
        n = math.prod(s)
        out.append(v[off:off + n].reshape(s))
        off += n
    return out


def kernel(x, norm_g, w_in, gm_ln_g, gm_ln_b, gm_ws, gm_bs, conv_w, conv_b, dt_bias, a_log, d_skip, ssm_norm_g, w_out, pool_w, pool_b, pool_scale, ffn_w_gate, ffn_w_up, ffn_w_down, loss_target, m_norm_g, m_w_in, m_gm_ln_g, m_gm_ln_b, m_gm_ws, m_gm_bs, m_conv_w, m_conv_b, m_dt_bias, m_a_log, m_d_skip, m_ssm_norm_g, m_w_out, m_pool_w, m_pool_b, m_pool_scale, m_ffn_w_gate, m_ffn_w_up, m_ffn_w_down, v_norm_g, v_w_in, v_gm_ln_g, v_gm_ln_b, v_gm_ws, v_gm_bs, v_conv_w, v_conv_b, v_dt_bias, v_a_log, v_d_skip, v_ssm_norm_g, v_w_out, v_pool_w, v_pool_b, v_pool_scale, v_ffn_w_gate, v_ffn_w_up, v_ffn_w_down):
    T = x.shape[1]
    w_sh = dict(zip(WEIGHT_NAMES, (norm_g, w_in, gm_ln_g, gm_ln_b, gm_ws, gm_bs, conv_w, conv_b, dt_bias, a_log, d_skip, ssm_norm_g, w_out,
                                   pool_w, pool_b, pool_scale, ffn_w_gate, ffn_w_up, ffn_w_down)))
    m_sh = dict(zip(WEIGHT_NAMES, (m_norm_g, m_w_in, m_gm_ln_g, m_gm_ln_b, m_gm_ws, m_gm_bs, m_conv_w, m_conv_b, m_dt_bias, m_a_log, m_d_skip,
                                   m_ssm_norm_g, m_w_out, m_pool_w, m_pool_b, m_pool_scale, m_ffn_w_gate, m_ffn_w_up, m_ffn_w_down)))
    v_sh = dict(zip(WEIGHT_NAMES, (v_norm_g, v_w_in, v_gm_ln_g, v_gm_ln_b, v_gm_ws, v_gm_bs, v_conv_w, v_conv_b, v_dt_bias, v_a_log, v_d_skip,
                                   v_ssm_norm_g, v_w_out, v_pool_w, v_pool_b, v_pool_scale, v_ffn_w_gate, v_ffn_w_up, v_ffn_w_down)))

    my_k = 2 * lax.axis_index("x") + lax.axis_index("y")
    ffn_own = [w_sh["ffn_w_gate"].reshape(2 * D, FF_SH).astype(bf16), w_sh["ffn_w_up"].reshape(2 * D, FF_SH).astype(bf16),
               w_sh["ffn_w_down"].reshape(2 * FF_SH, D).astype(bf16)]
    Wf = gather_weights(w_sh)
    send_sems, recv_sems, thru, lands, token = gather_start("gather_ffn_start", ffn_own, Wf["wo4"])
    Wf["norm_g"] = Wf["norm_g"] + token[0, 0]
    W = build_weights(Wf)

    def ffn_weights(after):
        landed = gather_wait("gather_ffn_wait", send_sems, recv_sems, thru, lands, after)
        return tuple(lax.dynamic_update_slice(l, o[None], (my_k, 0, 0)) for l, o in zip(landed, ffn_own))

    loss_acc, grad_x, G = local_step(T, x[0], loss_target[0], W, ffn_weights)

    my_c = lax.axis_index("c")
    c_arr = my_c.astype(jnp.int32).reshape(1)
    k_arr = (2 * lax.axis_index("x") + lax.axis_index("y")).astype(jnp.int32).reshape(1)
    packs = pack_grads(G)
    got = pair_split_exchange("grads_pair_split", packs, HALF_ROWS)
    pair16 = pair_sum("grads_pair_sum", packs, got, c_arr)
    landed = scatter_over_chips("grads_scatter", pair16)
    half = chip_sum("grads_chip_sum", pair16, landed, k_arr)
    other = pair_swap("grads_pair_swap", half)
    total = jnp.concatenate([jnp.where(my_c == 0, half, other), jnp.where(my_c == 0, other, half)], axis=0)
    grads = unpack_grads(total, w_sh)

    delta, new_m, new_v = {}, {}, {}
    for n in WEIGHT_NAMES:
        shp = w_sh[n].shape
        two_d = (-1, shp[-1])
        d_, m_, v_ = adamw("adamw_" + n, w_sh[n].reshape(two_d), grads[n].reshape(two_d), m_sh[n].reshape(two_d), v_sh[n].reshape(two_d))
        delta[n], new_m[n], new_v[n] = d_.reshape(shp), m_.reshape(shp), v_.reshape(shp)

    loss = lax.psum(loss_acc[0, 0], ("x", "y", "c"))
    return (loss, grad_x[None], *[grads[n] for n in WEIGHT_NAMES], *[delta[n] for n in WEIGHT_NAMES],
            *[new_m[n] for n in WEIGHT_NAMES], *[new_v[n] for n in WEIGHT_NAMES])
```

```python
import functools
import math

import jax
import jax.numpy as jnp
from jax import lax
from jax.experimental import pallas as pl
from jax.experimental.pallas import tpu as pltpu

f32, bf16 = jnp.float32, jnp.bfloat16
SDS = jax.ShapeDtypeStruct

D = 1024
EPS = 1e-6
CHUNK = 128
GM_HEADS, GM_HD = 4, 256
SSM_GROUPS, SSM_HPG, SSM_P, SSM_N = 4, 4, 64, 128
N_HEADS = SSM_GROUPS * SSM_HPG
CONV_K = 4
CONV_DIM = 2048
POOL_WINDOWS = (2, 4, 8, 16)
POOL_GD = 256
POOL_HALO = 16
CONV_HALO = 8
D_FF = 2816
DT_PAD = 128
IN_DIM = 5136

ADAM_LR, ADAM_B1, ADAM_B2, ADAM_EPS, ADAM_WD, ADAM_STEP = 0.001, 0.9, 0.999, 1e-08, 0.01, 10

NT = (((1,), (1,)), ((), ()))
TN = (((0,), (0,)), ((), ()))
NN = (((1,), (0,)), ((), ()))
HI = lax.Precision.HIGHEST


def _silu(x):
    return x * jax.nn.sigmoid(x)


def _softplus(x):
    return jnp.maximum(x, 0.0) + jnp.log1p(jnp.exp(-jnp.abs(x)))


def _rms(x, g):
    return x * lax.rsqrt(jnp.mean(x * x, axis=-1, keepdims=True) + EPS) * g


def _rms_bwd(x, g, dy):
    r = lax.rsqrt(jnp.mean(x * x, axis=-1, keepdims=True) + EPS)
    xh = x * r
    dxh = dy * g
    dx = r * (dxh - xh * jnp.mean(dxh * xh, axis=-1, keepdims=True))
    return dx, jnp.sum(dy * xh, axis=0, keepdims=True)


def _bdot(a, b, dims=NN):
    return lax.dot_general(a.astype(bf16), b.astype(bf16), dims, preferred_element_type=f32)


def matmul(name, pairs, mode, out_dtype, tm, tn, tk=None):
    a0, b0 = pairs[0]
    if mode == "tn":
        M, N, K = a0.shape[1], b0.shape[1], a0.shape[0]
    else:
        M, K = a0.shape
        N = b0.shape[1] if mode == "nn" else b0.shape[0]
    tm, tn = min(tm, M), min(tn, N)
    assert M % tm == 0 and N % tn == 0, (name, M, N, tm, tn)
    if tk is None:
        nk = 1
    else:
        assert len(pairs) == 1 and K % tk == 0
        nk = K // tk
    dims = {"nn": NN, "nt": NT, "tn": TN}[mode]
    in_specs, args = [], []
    for a, b in pairs:
        kk = (a.shape[0] if mode == "tn" else a.shape[1]) if tk is None else tk
        if mode == "tn":
            in_specs.append(pl.BlockSpec((kk, tm), lambda j, i, k: (k, i)))
            in_specs.append(pl.BlockSpec((kk, tn), lambda j, i, k: (k, j)))
        elif mode == "nn":
            in_specs.append(pl.BlockSpec((tm, kk), lambda j, i, k: (i, k)))
            in_specs.append(pl.BlockSpec((kk, tn), lambda j, i, k: (k, j)))
        else:
            in_specs.append(pl.BlockSpec((tm, kk), lambda j, i, k: (i, k)))
            in_specs.append(pl.BlockSpec((tn, kk), lambda j, i, k: (j, k)))
        args += [a, b]
    npairs = len(pairs)

    def kern(*refs):
        o = refs[2 * npairs]
        part = None
        for p in range(npairs):
            d = _bdot(refs[2 * p][...], refs[2 * p + 1][...], dims)
            part = d if part is None else part + d
        if nk == 1:
            o[...] = part.astype(out_dtype)
        else:
            acc = refs[2 * npairs + 1]
            k = pl.program_id(2)

            @pl.when(k == 0)
            def _():
                acc[...] = part

            @pl.when(k > 0)
            def _():
                acc[...] += part

            @pl.when(k == nk - 1)
            def _():
                o[...] = acc[...].astype(out_dtype)

    return pl.pallas_call(
        kern, name=name, grid=(N // tn, M // tm, nk),
        in_specs=in_specs, out_specs=pl.BlockSpec((tm, tn), lambda j, i, k: (i, j)),
        out_shape=SDS((M, N), out_dtype),
        scratch_shapes=[pltpu.VMEM((tm, tn), f32)] if nk > 1 else [],
        compiler_params=pltpu.CompilerParams(dimension_semantics=("parallel", "parallel", "arbitrary")),
    )(*args)


def mm(name, grid, pairs, dims, o_spec, out_shape):
    nk = grid[2]
    npairs = len(pairs)
    in_specs, args = [], []
    for a, a_spec, b, b_spec in pairs:
        in_specs += [a_spec, b_spec]
        args += [a, b]
    blk = tuple(d for d in o_spec.block_shape if d is not None)

    def kern(*refs):
        o = refs[2 * npairs]
        part = None
        for p in range(npairs):
            d = _bdot(refs[2 * p][...], refs[2 * p + 1][...], dims)
            part = d if part is None else part + d
        if nk == 1:
            o[...] = part.astype(o.dtype)
        else:
            acc = refs[2 * npairs + 1]
            k = pl.program_id(2)

            @pl.when(k == 0)
            def _():
                acc[...] = part

            @pl.when(k > 0)
            def _():
                acc[...] += part

            @pl.when(k == nk - 1)
            def _():
                o[...] = acc[...].astype(o.dtype)

    return pl.pallas_call(
        kern, name=name, grid=grid, in_specs=in_specs, out_specs=o_spec, out_shape=out_shape,
        scratch_shapes=[pltpu.VMEM(blk, f32)] if nk > 1 else [],
        compiler_params=pltpu.CompilerParams(dimension_semantics=("parallel", "parallel", "arbitrary")),
    )(*args)


def mm_fused(name, n_row_blocks, pairs, dims, extra_ins, outs, accs, epilogue):
    npairs, nx, no, na = len(pairs), len(extra_ins), len(outs), len(accs)
    in_specs, args = [], []
    for a, a_spec, b, b_spec in pairs:
        in_specs += [a_spec, b_spec]
        args += [a, b]
    for arr, spec in extra_ins:
        in_specs.append(spec)
        args.append(arr)

    def kern(*refs):
        part = None
        for p in range(npairs):
            d = _bdot(refs[2 * p][...], refs[2 * p + 1][...], dims)
            part = d if part is None else part + d
        x_refs = refs[2 * npairs:2 * npairs + nx]
        o_refs = refs[2 * npairs + nx:2 * npairs + nx + no]
        a_refs = refs[2 * npairs + nx + no:]
        if na:
            @pl.when(pl.program_id(0) == 0)
            def _():
                for a in a_refs:
                    a[...] = jnp.zeros(a.shape, f32)
        epilogue(part, x_refs, o_refs, a_refs)

    return pl.pallas_call(
        kern, name=name, grid=(n_row_blocks,), in_specs=in_specs,
        out_specs=[spec for _, spec in outs] + [pl.BlockSpec(tuple(s), lambda i, nd=len(s): (0,) * nd) for s in accs],
        out_shape=[s for s, _ in outs] + [SDS(tuple(s), f32) for s in accs],
        compiler_params=pltpu.CompilerParams(dimension_semantics=("arbitrary",)),
    )(*args)


FF_SH = D_FF // 4


def ffn_up(name, T, tm, n_bf, wg4, wu4, l):
    def kern(n_ref, wg_ref, wu_ref, g_ref, u_ref, a_ref):
        n = n_ref[...]
        g = jnp.dot(n, wg_ref[...], preferred_element_type=f32)
        u = jnp.dot(n, wu_ref[...], preferred_element_type=f32)
        g_ref[...] = g.astype(bf16)
        u_ref[...] = u.astype(bf16)
        a_ref[...] = (_silu(g) * u).astype(bf16)
    w_spec = pl.BlockSpec((None, D, FF_SH), lambda k, i: (k, l, 0))
    o_spec = pl.BlockSpec((None, tm, FF_SH), lambda k, i: (k, i, 0))
    s = SDS((4, T, FF_SH), bf16)
    return pl.pallas_call(kern, name=name, grid=(4, T // tm), in_specs=[pl.BlockSpec((tm, D), lambda k, i: (i, 0)), w_spec, w_spec],
                          out_specs=[o_spec] * 3, out_shape=[s, s, s],
                          compiler_params=pltpu.CompilerParams(dimension_semantics=("parallel", "parallel")))(n_bf, wg4, wu4)


def ffn_dgu(name, T, tm, d_f, wd4, gate4, up4, l):
    rc = 16

    def kern(df_ref, wd_ref, g_ref, u_ref, dg_ref, du_ref, dact_ref):
        dact_ref[...] = _bdot(df_ref[...], wd_ref[...], NT)
        for r0 in range(0, tm, rc):
            rows = pl.ds(r0, rc)
            _, vjp = jax.vjp(lambda a, b: _silu(a) * b, g_ref[rows, :].astype(f32), u_ref[rows, :].astype(f32))
            dg, du = vjp(dact_ref[rows, :])
            dg_ref[rows, :] = dg.astype(bf16)
            du_ref[rows, :] = du.astype(bf16)
    a_spec = pl.BlockSpec((None, tm, FF_SH), lambda k, i: (k, i, 0))
    s = SDS((4, T, FF_SH), bf16)
    return pl.pallas_call(kern, name=name, grid=(4, T // tm),
                          in_specs=[pl.BlockSpec((tm, D), lambda k, i: (i, 0)), pl.BlockSpec((None, FF_SH, D), lambda k, i: (k, l, 0)), a_spec, a_spec],
                          out_specs=[a_spec] * 2, out_shape=[s, s], scratch_shapes=[pltpu.VMEM((tm, FF_SH), f32)],
                          compiler_params=pltpu.CompilerParams(dimension_semantics=("parallel", "parallel")))(d_f, wd4, gate4, up4)


def rowcall(name, body, T, tm, ins, outs, accs=(), scratch=(), reverse=False):
    n = T // tm
    assert T % tm == 0

    def blk(i):
        return (n - 1 - i) if reverse else i

    in_specs, args = [], []
    for spec in ins:
        kind, arr = spec[0], spec[1]
        if kind == "row":
            _, _, w, cb = spec
            in_specs.append(pl.BlockSpec((tm, w), lambda i, cb=cb: (blk(i), cb)))
        elif kind == "prev":
            _, _, w, cb, h = spec
            r = tm // h
            in_specs.append(pl.BlockSpec((h, w), lambda i, cb=cb, r=r: (jnp.maximum(blk(i) * r - 1, 0), cb)))
        elif kind == "next":
            _, _, w, cb, h = spec
            r = tm // h
            in_specs.append(pl.BlockSpec((h, w), lambda i, cb=cb, r=r, h=h: (jnp.minimum((blk(i) + 1) * r, T // h - 1), cb)))
        else:
            nd = arr.ndim
            in_specs.append(pl.BlockSpec(arr.shape, lambda i, nd=nd: (0,) * nd))
        args.append(arr)
    out_shape = [SDS((T, w), dt) for w, dt in outs] + [SDS(tuple(s), f32) for s in accs]
    out_specs = [pl.BlockSpec((tm, w), lambda i: (blk(i), 0)) for w, _ in outs]
    out_specs += [pl.BlockSpec(tuple(s), lambda i, nd=len(s): (0,) * nd) for s in accs]
    ni, no, na = len(ins), len(outs), len(accs)

    def kern(*refs):
        i = pl.program_id(0)
        in_refs, out_refs = refs[:ni], refs[ni:ni + no]
        acc_refs, scr = refs[ni + no:ni + no + na], refs[ni + no + na:]
        if na:
            @pl.when(i == 0)
            def _():
                for a in acc_refs:
                    a[...] = jnp.zeros(a.shape, f32)
        body(blk(i), n, in_refs, out_refs, acc_refs, scr)

    res = pl.pallas_call(
        kern, name=name, grid=(n,), in_specs=in_specs, out_specs=out_specs, out_shape=out_shape,
        scratch_shapes=list(scratch),
        compiler_params=pltpu.CompilerParams(dimension_semantics=("arbitrary",)),
    )(*args)
    return res


def rms_to_bf16(name, T, tm, x, g):
    def body(i, n, ins, outs, accs, scr):
        outs[0][...] = _rms(ins[0][...], ins[1][...]).astype(bf16)
    return rowcall(name, body, T, tm, [("row", x, D, 0), ("const", g)], [(D, bf16)])[0]


def resid_norm(name, T, tm, h_in, f, g_post, g_pre):
    def body(i, n, ins, outs, accs, scr):
        h = ins[0][...] + _rms(ins[1][...], ins[2][...])
        outs[0][...] = h
        if g_pre is not None:
            outs[1][...] = _rms(h, ins[3][...]).astype(bf16)
    ins = [("row", h_in, D, 0), ("row", f, D, 0), ("const", g_post)] + ([("const", g_pre)] if g_pre is not None else [])
    outs = [(D, f32)] + ([(D, bf16)] if g_pre is not None else [])
    return rowcall(name, body, T, tm, ins, outs)


def swiglu_act(name, T, tm, gate, up):
    def body(i, n, ins, outs, accs, scr):
        outs[0][...] = (_silu(ins[0][...]) * ins[1][...]).astype(bf16)
    return rowcall(name, body, T, tm, [("row", gate, D_FF, 0), ("row", up, D_FF, 0)], [(D_FF, bf16)])[0]


def swiglu_bwd(name, T, tm, gate, up, d_act):
    def body(i, n, ins, outs, accs, scr):
        _, vjp = jax.vjp(lambda a, b: _silu(a) * b, ins[0][...], ins[1][...])
        dg, du = vjp(ins[2][...])
        outs[0][...] = dg.astype(bf16)
        outs[1][...] = du.astype(bf16)
    return rowcall(name, body, T, tm, [("row", gate, D_FF, 0), ("row", up, D_FF, 0), ("row", d_act, D_FF, 0)],
                   [(D_FF, bf16), (D_FF, bf16)])


def final_loss_bwd(name, T, tm, h3, f2, tgt, g_post):
    def body(i, n, ins, outs, accs, scr):
        f, g = ins[1][...], ins[3][...]
        e = ins[0][...] + _rms(f, g) - ins[2][...]
        accs[0][...] += jnp.sum(jnp.sum(e * e, axis=-1, keepdims=True) * (0.5 / D), axis=0, keepdims=True)
        dh = e * (1.0 / D)
        df, dg = _rms_bwd(f, g, dh)
        outs[0][...] = dh
        outs[1][...] = df.astype(bf16)
        accs[1][...] += dg
    return rowcall(name, body, T, tm, [("row", h3, D, 0), ("row", f2, D, 0), ("row", tgt, D, 0), ("const", g_post)],
                   [(D, f32), (D, bf16)], accs=[(1, 1), (1, D)])


def bwd_pre_post(name, T, tm, h_out, f, d_res, d_n, g_pre, g_post, df_dtype):
    def body(i, n, ins, outs, accs, scr):
        dx, dgp = _rms_bwd(ins[0][...], ins[4][...], ins[3][...])
        dh = ins[2][...] + dx
        df, dgq = _rms_bwd(ins[1][...], ins[5][...], dh)
        outs[0][...] = dh
        outs[1][...] = df.astype(df_dtype)
        accs[0][...] += dgp
        accs[1][...] += dgq
    return rowcall(name, body, T, tm,
                   [("row", h_out, D, 0), ("row", f, D, 0), ("row", d_res, D, 0), ("row", d_n, D, 0), ("const", g_pre), ("const", g_post)],
                   [(D, f32), (D, df_dtype)], accs=[(1, D), (1, D)])


def bwd_post(name, T, tm, f, d_h, g_post):
    def body(i, n, ins, outs, accs, scr):
        df, dg = _rms_bwd(ins[0][...], ins[2][...], ins[1][...])
        outs[0][...] = df.astype(bf16)
        accs[0][...] += dg
    return rowcall(name, body, T, tm, [("row", f, D, 0), ("row", d_h, D, 0), ("const", g_post)], [(D, bf16)], accs=[(1, D)])


def bwd_pre(name, T, tm, h, d_res, d_n, g_pre):
    def body(i, n, ins, outs, accs, scr):
        dx, dg = _rms_bwd(ins[0][...], ins[3][...], ins[2][...])
        outs[0][...] = ins[1][...] + dx
        accs[0][...] += dg
    return rowcall(name, body, T, tm, [("row", h, D, 0), ("row", d_res, D, 0), ("row", d_n, D, 0), ("const", g_pre)],
                   [(D, f32)], accs=[(1, D)])


def _layer_norm_parts(x):
    mu = jnp.mean(x, axis=-1, keepdims=True)
    xc = x - mu
    r = lax.rsqrt(jnp.mean(xc * xc, axis=-1, keepdims=True) + EPS)
    return xc * r, r


def gmlp_fwd(name, T, tm, uvz, ln_g, ln_b, wm, bs):
    def body(i, n, ins, outs, accs, scr):
        gu = jax.nn.gelu(ins[0][...])
        xh, _ = _layer_norm_parts(jax.nn.gelu(ins[1][...]))
        vln = (xh * ins[2][...] + ins[3][...]).astype(bf16)
        for c in range(tm // CHUNK):
            rows = slice(c * CHUNK, (c + 1) * CHUNK)
            for h in range(GM_HEADS):
                cols = slice(h * GM_HD, (h + 1) * GM_HD)
                mixed = jnp.dot(ins[4][h], vln[rows, cols], preferred_element_type=f32) + ins[5][h]
                outs[0][rows, cols] = (gu[rows, cols] * mixed).astype(bf16)
    return rowcall(name, body, T, tm, [("row", uvz, D, 0), ("row", uvz, D, 1), ("const", ln_g), ("const", ln_b), ("const", wm), ("const", bs)],
                   [(D, bf16)])[0]


def gmlp_bwd(name, T, tm, uvz, d_ya, ln_g, ln_b, wm, bs):
    def body(i, n, ins, outs, accs, scr):
        u, v, dya = ins[0][...], ins[1][...], ins[2][...]
        gu, gelu_u_vjp = jax.vjp(jax.nn.gelu, u)
        gv, gelu_v_vjp = jax.vjp(jax.nn.gelu, v)
        xh, r = _layer_norm_parts(gv)
        lng = ins[3][...]
        vln = (xh * lng + ins[4][...]).astype(bf16)
        rr = lax.broadcasted_iota(jnp.int32, (CHUNK, CHUNK), 0)
        cc = lax.broadcasted_iota(jnp.int32, (CHUNK, CHUNK), 1)
        causal = (rr >= cc).astype(f32)
        dvln_ref = scr[0]
        dgu_ref = scr[1]
        for c in range(tm // CHUNK):
            rows = slice(c * CHUNK, (c + 1) * CHUNK)
            for h in range(GM_HEADS):
                cols = slice(h * GM_HD, (h + 1) * GM_HD)
                w = ins[5][h]
                blk = vln[rows, cols]
                mixed = jnp.dot(w, blk, preferred_element_type=f32) + ins[6][h]
                dy = dya[rows, cols]
                dgu_ref[rows, cols] = dy * mixed
                dm = dy * gu[rows, cols]
                accs[3][h] += jnp.sum(dm, axis=1, keepdims=True)
                accs[2][h] += _bdot(dm, blk, NT) * causal
                dvln_ref[rows, cols] = _bdot(w, dm, TN)
        dvln = dvln_ref[...]
        accs[0][...] += jnp.sum(dvln * xh, axis=0, keepdims=True)
        accs[1][...] += jnp.sum(dvln, axis=0, keepdims=True)
        dxh = dvln * lng
        dgv = r * (dxh - jnp.mean(dxh, axis=-1, keepdims=True) - xh * jnp.mean(dxh * xh, axis=-1, keepdims=True))
        outs[0][...] = gelu_u_vjp(dgu_ref[...])[0].astype(bf16)
        outs[1][...] = gelu_v_vjp(dgv)[0].astype(bf16)
    return rowcall(name, body, T, tm,
                   [("row", uvz, D, 0), ("row", uvz, D, 1), ("row", d_ya, D, 0), ("const", ln_g), ("const", ln_b), ("const", wm), ("const", bs)],
                   [(D, bf16), (D, bf16)], accs=[(1, D), (1, D), (GM_HEADS, CHUNK, CHUNK), (GM_HEADS, CHUNK, 1)],
                   scratch=[pltpu.VMEM((tm, D), f32), pltpu.VMEM((tm, D), f32)])


CONV_RC, CONV_LB = 32, 512


def _conv_fill(i, x_ref, halo_ref, scr, tm):
    scr[pl.ds(0, CONV_HALO), :] = jnp.where(i > 0, halo_ref[...], 0.0)
    scr[pl.ds(CONV_HALO, tm), :] = x_ref[...]


def _conv_taps(scr, r0, lanes):
    return [scr[pl.ds(r0 + CONV_HALO - (CONV_K - 1) + k, CONV_RC), lanes] for k in range(CONV_K)]


def conv_fwd(name, T, tm, xbc, conv_w, conv_b):
    def body(i, n, ins, outs, accs, scr):
        s = scr[0]
        _conv_fill(i, ins[0], ins[1], s, tm)
        for lb in range(CONV_DIM // CONV_LB):
            lanes = slice(lb * CONV_LB, (lb + 1) * CONV_LB)
            w, b = ins[2][:, lanes], ins[3][:, lanes]

            for r0 in range(0, tm, CONV_RC):
                taps = _conv_taps(s, r0, lanes)
                pre = b + sum(w[k:k + 1] * taps[k] for k in range(CONV_K))
                outs[0][pl.ds(r0, CONV_RC), lanes] = _silu(pre)
    return rowcall(name, body, T, tm, [("row", xbc, CONV_DIM, 0), ("prev", xbc, CONV_DIM, 0, CONV_HALO), ("const", conv_w), ("const", conv_b)],
                   [(CONV_DIM, f32)], scratch=[pltpu.VMEM((tm + CONV_HALO, CONV_DIM), f32)])[0]


def conv_bwd_pre(name, T, tm, xbc, d_xc, conv_w, conv_b):
    def body(i, n, ins, outs, accs, scr):
        s = scr[0]
        _conv_fill(i, ins[0], ins[1], s, tm)
        fold = lambda v: jnp.sum(v.reshape(CONV_RC // 8, 8, CONV_LB), axis=0)
        for lb in range(CONV_DIM // CONV_LB):
            lanes = slice(lb * CONV_LB, (lb + 1) * CONV_LB)
            w, b = ins[3][:, lanes], ins[4][:, lanes]

            sums = [jnp.zeros((8, CONV_LB), f32)] * (CONV_K + 1)
            for r0 in range(0, tm, CONV_RC):
                taps = _conv_taps(s, r0, lanes)
                pre = b + sum(w[k:k + 1] * taps[k] for k in range(CONV_K))
                _, vjp = jax.vjp(_silu, pre)
                dpre = vjp(ins[2][pl.ds(r0, CONV_RC), lanes])[0]
                outs[0][pl.ds(r0, CONV_RC), lanes] = dpre
                sums = [sums[k] + fold(dpre * taps[k]) for k in range(CONV_K)] + [sums[CONV_K] + fold(dpre)]
            for k in range(CONV_K):
                accs[0][pl.ds(k, 1), lanes] += jnp.sum(sums[k], axis=0, keepdims=True)
            accs[1][:, lanes] += jnp.sum(sums[CONV_K], axis=0, keepdims=True)
    return rowcall(name, body, T, tm,
                   [("row", xbc, CONV_DIM, 0), ("prev", xbc, CONV_DIM, 0, CONV_HALO), ("row", d_xc, CONV_DIM, 0), ("const", conv_w), ("const", conv_b)],
                   [(CONV_DIM, f32)], accs=[(CONV_K, CONV_DIM), (1, CONV_DIM)], scratch=[pltpu.VMEM((tm + CONV_HALO, CONV_DIM), f32)])


def conv_bwd_x(name, T, tm, d_pre, conv_w):
    def body(i, n, ins, outs, accs, scr):
        s = scr[0]
        s[pl.ds(0, tm), :] = ins[0][...]
        s[pl.ds(tm, CONV_HALO), :] = jnp.where(i < n - 1, ins[1][...], 0.0)
        for lb in range(CONV_DIM // CONV_LB):
            lanes = slice(lb * CONV_LB, (lb + 1) * CONV_LB)
            w = ins[2][:, lanes]

            for r0 in range(0, tm, CONV_RC):
                dx = sum(w[k:k + 1] * s[pl.ds(r0 + CONV_K - 1 - k, CONV_RC), lanes] for k in range(CONV_K))
                outs[0][pl.ds(r0, CONV_RC), lanes] = dx.astype(bf16)
    return rowcall(name, body, T, tm, [("row", d_pre, CONV_DIM, 0), ("next", d_pre, CONV_DIM, 0, CONV_HALO), ("const", conv_w)],
                   [(CONV_DIM, bf16)], scratch=[pltpu.VMEM((tm + CONV_HALO, CONV_DIM), f32)])[0]


def _ssd_prep(dtr, dtb, alog):
    rr = lax.broadcasted_iota(jnp.int32, (CHUNK, CHUNK), 0)
    cc = lax.broadcasted_iota(jnp.int32, (CHUNK, CHUNK), 1)
    dt = _softplus(dtr + dtb)
    dA = dt * -jnp.exp(alog)
    acum = jnp.dot((rr >= cc).astype(f32), dA, precision=HI, preferred_element_type=f32)
    return dt, acum, acum.T, jnp.sum(dA, axis=0, keepdims=True)


def _ssd_group(g, x, Bm, Cm, S, dt, acum, acumT, tot, dsk):
    rr = lax.broadcasted_iota(jnp.int32, (CHUNK, CHUNK), 0)
    cc = lax.broadcasted_iota(jnp.int32, (CHUNK, CHUNK), 1)
    tril = rr >= cc
    lane = lax.broadcasted_iota(jnp.int32, (1, DT_PAD), 1)
    sub = lax.broadcasted_iota(jnp.int32, (DT_PAD, 1), 0)
    glane = lax.broadcasted_iota(jnp.int32, (1, SSM_HPG * SSM_P), 1) // SSM_P
    hm = [(glane == r).astype(f32) for r in range(SSM_HPG)]
    pick = lambda v, r: jnp.sum(v * (lane == SSM_HPG * g + r).astype(f32), axis=1, keepdims=True)
    cols = [pick(acum, r) for r in range(SSM_HPG)]
    tots = [pick(tot, r) for r in range(SSM_HPG)]
    spread = lambda vals: sum(vals[r] * hm[r] for r in range(SSM_HPG))
    xdt = x * spread([pick(dt, r) for r in range(SSM_HPG)])
    cb = _bdot(Cm, Bm, NT)
    y = x * spread([pick(dsk, r) for r in range(SSM_HPG)])
    for r in range(SSM_HPG):
        row = jnp.sum(acumT * (sub == SSM_HPG * g + r).astype(f32), axis=0, keepdims=True)
        dec = jnp.exp(jnp.where(tril, cols[r] - row, -jnp.inf))
        y = y + _bdot(cb * dec, xdt * hm[r])
    y = y + _bdot(Cm, S) * spread([jnp.exp(c) for c in cols])
    dte = spread([jnp.exp(tots[r] - cols[r]) for r in range(SSM_HPG)])
    s_new = S * spread([jnp.exp(t) for t in tots]) + _bdot(Bm, xdt * dte, TN)
    return y, s_new


def _ssd_ins(xc, dtr):
    gw = SSM_HPG * SSM_P
    ins = [("row", xc, gw, g) for g in range(SSM_GROUPS)]
    ins += [("row", xc, SSM_N, D // SSM_N + g) for g in range(SSM_GROUPS)]
    ins += [("row", xc, SSM_N, D // SSM_N + SSM_GROUPS + g) for g in range(SSM_GROUPS)]
    ins += [("row", dtr, DT_PAD, 0)]
    return ins


SSD_CPS = 2


def ssd_fwd(name, T, xc, dtr, dtb, alog, dsk):
    gw = SSM_HPG * SSM_P

    def body(i, n, ins, outs, accs, scr):
        S = scr[0]

        @pl.when(i == 0)
        def _():
            S[...] = jnp.zeros(S.shape, f32)
        S4 = tuple(S[:, g * gw:(g + 1) * gw] for g in range(4))
        for c in range(SSD_CPS):
            rows = pl.ds(c * CHUNK, CHUNK)
            X4 = tuple(ins[g][rows, :] for g in range(4))
            B4 = tuple(ins[4 + g][rows, :] for g in range(4))
            C4 = tuple(ins[8 + g][rows, :] for g in range(4))
            prep = _ssd_prep(ins[12][rows, :], ins[13][...], ins[14][...])
            nxt = []
            for g in range(4):
                outs[1][rows, g * gw:(g + 1) * gw] = S4[g]
                y, s_new = _ssd_group(g, X4[g], B4[g], C4[g], S4[g], *prep, ins[15][...])
                outs[0][rows, g * gw:(g + 1) * gw] = y
                nxt.append(s_new)
            S4 = tuple(nxt)
        for g in range(4):
            S[:, g * gw:(g + 1) * gw] = S4[g]
    ins = _ssd_ins(xc, dtr) + [("const", dtb), ("const", alog), ("const", dsk)]
    return rowcall(name, body, T, SSD_CPS * CHUNK, ins, [(D, f32), (D, f32)], scratch=[pltpu.VMEM((SSM_N, D), f32)])


def ssd_bwd(name, T, xc, dtr, sprev, d_y, dtb, alog, dsk):
    gw = SSM_HPG * SSM_P

    def body(i, n, ins, outs, accs, scr):
        dS = scr[0]

        @pl.when(i == n - 1)
        def _():
            dS[...] = jnp.zeros(dS.shape, f32)
        dS4 = tuple(dS[:, g * gw:(g + 1) * gw] for g in range(4))
        def chunk(X4, dtr_c, B4, C4, S4, dtb_c, alog_c, dsk_c):
            prep = _ssd_prep(dtr_c, dtb_c, alog_c)
            res = [_ssd_group(g, X4[g], B4[g], C4[g], S4[g], *prep, dsk_c) for g in range(4)]
            return tuple(r[0] for r in res), tuple(r[1] for r in res)
        X4 = tuple(ins[g][...] for g in range(4))
        B4 = tuple(ins[4 + g][...] for g in range(4))
        C4 = tuple(ins[8 + g][...] for g in range(4))
        S4 = tuple(ins[13 + g][...] for g in range(4))
        dY4 = tuple(ins[17 + g][...] for g in range(4))
        _, vjp = jax.vjp(chunk, X4, ins[12][...], B4, C4, S4, ins[21][...], ins[22][...], ins[23][...])
        dX4, ddtr, dB4, dC4, dS4, ddtb, dalog, ddsk = vjp((dY4, dS4))
        for g in range(4):
            outs[0][:, g * gw:(g + 1) * gw] = dX4[g]
            outs[0][:, D + g * SSM_N:D + (g + 1) * SSM_N] = dB4[g]
            outs[0][:, D + (SSM_GROUPS + g) * SSM_N:D + (SSM_GROUPS + g + 1) * SSM_N] = dC4[g]
            dS[:, g * gw:(g + 1) * gw] = dS4[g]
        outs[1][...] = ddtr.astype(bf16)
        accs[0][...] += ddtb
        accs[1][...] += dalog
        accs[2][...] += ddsk
    ins = _ssd_ins(xc, dtr) + [("row", sprev, gw, g) for g in range(4)] + [("row", d_y, gw, g) for g in range(4)]
    ins += [("const", dtb), ("const", alog), ("const", dsk)]
    return rowcall(name, body, T, CHUNK, ins, [(CONV_DIM, f32), (DT_PAD, bf16)], accs=[(1, DT_PAD)] * 3,
                   scratch=[pltpu.VMEM((SSM_N, D), f32)], reverse=True)


def _gate_group(y, z, g):
    return _rms(y * _silu(z), g)


def gate_fwd(name, T, tm, y, uvz, gn):
    def body(i, n, ins, outs, accs, scr):
        for g in range(SSM_GROUPS):
            cols = slice(g * 256, (g + 1) * 256)
            outs[0][:, cols] = _gate_group(ins[0][:, cols], ins[1][:, cols], ins[2][:, cols]).astype(bf16)
    return rowcall(name, body, T, tm, [("row", y, D, 0), ("row", uvz, D, 2), ("const", gn)], [(D, bf16)])[0]


def gate_bwd(name, T, tm, y, uvz, d_yb, gn):
    def body(i, n, ins, outs, accs, scr):
        for g in range(SSM_GROUPS):
            cols = slice(g * 256, (g + 1) * 256)
            _, vjp = jax.vjp(_gate_group, ins[0][:, cols], ins[1][:, cols], ins[3][:, cols])
            dy, dz, dg = vjp(ins[2][:, cols])
            outs[0][:, cols] = dy
            outs[1][:, cols] = dz.astype(bf16)
            accs[0][:, cols] += dg
    return rowcall(name, body, T, tm, [("row", y, D, 0), ("row", uvz, D, 2), ("row", d_yb, D, 0), ("const", gn)],
                   [(D, f32), (D, bf16)], accs=[(1, D)])


def _pool_diff(i, tm, h_ref, halo_ref, g_ref, scr):
    g = g_ref[...]
    yn = _rms(h_ref[...], g)
    scr[pl.ds(0, POOL_HALO), :] = jnp.where(i > 0, _rms(halo_ref[...], g), 0.0)
    scr[pl.ds(POOL_HALO, tm), :] = yn
    pos = (i * tm + lax.broadcasted_iota(jnp.int32, (tm, 1), 0) + 1).astype(f32)
    parts = []
    for gi, win in enumerate(POOL_WINDOWS):
        cols = slice(gi * POOL_GD, (gi + 1) * POOL_GD)
        s = scr[pl.ds(POOL_HALO, tm), cols]
        for j in range(1, win):
            s = s + scr[pl.ds(POOL_HALO - j, tm), cols]
        parts.append(s / jnp.minimum(pos, float(win)) - yn[:, cols])
    return parts


def pool_fwd(name, T, tm, h2, g_pre, pw, pb, psc, g_post, g_next):
    def body(i, n, ins, outs, accs, scr):
        parts = _pool_diff(i, tm, ins[0], ins[1], ins[2], scr[0])
        for gi in range(len(POOL_WINDOWS)):
            cols = slice(gi * POOL_GD, (gi + 1) * POOL_GD)
            o = _bdot(parts[gi], ins[3][gi]) + ins[4][:, cols]
            outs[0][:, cols] = o * ins[5][:, cols]
        h = ins[0][...] + _rms(outs[0][...], ins[6][...])
        outs[1][...] = h
        outs[2][...] = _rms(h, ins[7][...]).astype(bf16)
    return rowcall(name, body, T, tm, [("row", h2, D, 0), ("prev", h2, D, 0, POOL_HALO), ("const", g_pre), ("const", pw), ("const", pb), ("const", psc),
                                       ("const", g_post), ("const", g_next)],
                   [(D, f32), (D, f32), (D, bf16)], scratch=[pltpu.VMEM((tm + POOL_HALO, D), f32)])


def pool_bwd(name, T, tm, h2, d_pm, d_res, g_pre, pw, pb, psc, f_prev, g_prev):
    def body(i, n, ins, outs, accs, scr):
        parts = _pool_diff(i, tm, ins[0], ins[1], ins[5], scr[0])
        dpm = ins[2][...]
        psc_v = ins[8][...]
        dps = dpm * psc_v
        dps_halo = jnp.where(i < n - 1, ins[3][...] * psc_v, 0.0)
        accs[1][...] += jnp.sum(dps, axis=0, keepdims=True)
        pos = (i * tm + lax.broadcasted_iota(jnp.int32, (tm, 1), 0) + 1).astype(f32)
        pos_h = ((i + 1) * tm + lax.broadcasted_iota(jnp.int32, (POOL_HALO, 1), 0) + 1).astype(f32)
        r_scr = scr[1]
        dyn_scr = scr[2]
        for gi, win in enumerate(POOL_WINDOWS):
            cols = slice(gi * POOL_GD, (gi + 1) * POOL_GD)
            w = ins[6][gi]
            o = _bdot(parts[gi], w) + ins[7][:, cols]
            accs[2][:, cols] += jnp.sum(dpm[:, cols] * o, axis=0, keepdims=True)
            accs[0][gi] += _bdot(parts[gi], dps[:, cols], TN)
            q = _bdot(dps[:, cols], w, NT)
            qh = _bdot(dps_halo[:, cols], w, NT)
            r_scr[pl.ds(0, tm), cols] = q / jnp.minimum(pos, float(win))
            r_scr[pl.ds(tm, POOL_HALO), cols] = qh / jnp.minimum(pos_h, float(win))
            s = r_scr[pl.ds(0, tm), cols]
            for j in range(1, win):
                s = s + r_scr[pl.ds(j, tm), cols]
            dyn_scr[:, cols] = s - q
        dx, dg = _rms_bwd(ins[0][...], ins[5][...], dyn_scr[...])
        dh = ins[4][...] + dx
        outs[0][...] = dh
        accs[3][...] += dg
        df, dgp = _rms_bwd(ins[9][...], ins[10][...], dh)
        outs[1][...] = df.astype(bf16)
        accs[4][...] += dgp
    ins = [("row", h2, D, 0), ("prev", h2, D, 0, POOL_HALO), ("row", d_pm, D, 0), ("next", d_pm, D, 0, POOL_HALO), ("row", d_res, D, 0),
           ("const", g_pre), ("const", pw), ("const", pb), ("const", psc), ("row", f_prev, D, 0), ("const", g_prev)]
    return rowcall(name, body, T, tm, ins, [(D, f32), (D, bf16)], accs=[(4, POOL_GD, POOL_GD), (1, D), (1, D), (1, D), (1, D)],
                   scratch=[pltpu.VMEM((tm + POOL_HALO, D), f32), pltpu.VMEM((tm + POOL_HALO, D), f32), pltpu.VMEM((tm, D), f32)])


def local_step(T, x, tgt, W, ffn_weights):
    tm = 512 if T >= 1024 else T // 2
    TKW = 2048 if T >= 2048 else T
    ng = W["norm_g"]
    g = lambda l, j: ng[l, j][None, :]
    G = {}

    row_spec = pl.BlockSpec((tm, D), lambda j, i, k: (i, 0))
    tf = tm // 2 if T >= 1024 else tm
    rows_f = pl.BlockSpec((tf, D), lambda i: (i, 0))
    vec_f = pl.BlockSpec((1, D), lambda i: (0, 0))
    sh_f = [pl.BlockSpec((None, tf, FF_SH), lambda i, s=s: (s, i, 0)) for s in range(4)]
    sh_spec = [pl.BlockSpec((None, tm, FF_SH), lambda j, i, k, s=s: (s, i, 0)) for s in range(4)]
    out_f32, out_bf16 = (SDS((T, D), f32), rows_f), (SDS((T, D), bf16), rows_f)

    def resid_epilogue(with_pre):
        def ep(part, xs, os, accs):
            h = xs[0][...] + _rms(part, xs[1][...])
            os[0][...] = part
            os[1][...] = h
            if with_pre:
                os[2][...] = _rms(h, xs[2][...]).astype(bf16)
        return ep

    def bwd_epilogue(df_dtype):
        def ep(part, xs, os, accs):
            dx, dgp = _rms_bwd(xs[0][...], xs[3][...], part)
            dh = xs[2][...] + dx
            df, dgq = _rms_bwd(xs[1][...], xs[4][...], dh)
            os[0][...] = dh
            os[1][...] = df.astype(df_dtype)
            accs[0][...] += dgp
            accs[1][...] += dgq
        return ep

    def ffn_fwd(tag, n_bf, l, resid=None):
        gate4, up4, act4 = ffn_up(f"ffn{tag}_up", T, tm, n_bf, W["wg4"], W["wu4"], l)
        if resid is None:
            wd_spec = [pl.BlockSpec((None, FF_SH, D), lambda j, i, k, s=s: (s, l, 0)) for s in range(4)]
            f = mm(f"ffn{tag}_down", (1, T // tm, 1), [(act4, sh_spec[s], W["wd4"], wd_spec[s]) for s in range(4)], NN, row_spec, SDS((T, D), f32))
            return gate4, up4, act4, f, None
        wd_f = [pl.BlockSpec((None, FF_SH, D), lambda i, s=s: (s, l, 0)) for s in range(4)]
        f, h_out = mm_fused(f"ffn{tag}_down", T // tf, [(act4, sh_f[s], W["wd4"], wd_f[s]) for s in range(4)], NN,
                            [(resid[0], rows_f), (resid[1], vec_f)], [out_f32, out_f32], [], resid_epilogue(False))
        return gate4, up4, act4, f, h_out

    def ffn_bwd(tag, l, n_bf, gate4, up4, act4, d_f, h_out, f_pre, d_res, g_pre, g_post, df_dtype):
        d_gate4, d_up4 = ffn_dgu(f"ffn{tag}_dgu", T, tm, d_f, W["wd4"], gate4, up4, l)
        w_f = [pl.BlockSpec((None, D, FF_SH), lambda i, s=s: (s, l, 0)) for s in range(4)]
        d_h, d_fp, dgp, dgq = mm_fused(
            f"ffn{tag}_dn", T // tf, [(d_gate4, sh_f[s], W["wg4"], w_f[s]) for s in range(4)] + [(d_up4, sh_f[s], W["wu4"], w_f[s]) for s in range(4)],
            NT, [(h_out, rows_f), (f_pre, rows_f), (d_res, rows_f), (g_pre, vec_f), (g_post, vec_f)],
            [out_f32, (SDS((T, D), df_dtype), rows_f)], [(1, D), (1, D)], bwd_epilogue(df_dtype))

        def wgrad(nm, a4, b):
            return mm(nm, (4, 1, T // TKW),
                      [(a4, pl.BlockSpec((None, TKW, FF_SH), lambda s, j, k: (s, k, 0)), b, pl.BlockSpec((TKW, D), lambda s, j, k: (k, 0)))],
                      TN, pl.BlockSpec((None, FF_SH, D), lambda s, j, k: (s, 0, 0)), SDS((4, FF_SH, D), f32))
        return d_h, d_fp, dgp, dgq, wgrad(f"ffn{tag}_dwg", d_gate4, n_bf), wgrad(f"ffn{tag}_dwu", d_up4, n_bf), wgrad(f"ffn{tag}_dwd", act4, d_f)

    y0 = rms_to_bf16("l0_prenorm", T, tm, x, g(0, 0))
    uvz = matmul("in_uvz", [(y0, W["w_uvz"])], "nn", f32, tm, 1024)
    xbc = matmul("in_xbc", [(y0, W["w_xbc"])], "nn", f32, tm, 1024)
    dtr = matmul("in_dt", [(y0, W["w_dt"])], "nn", f32, tm, DT_PAD)
    y_a = gmlp_fwd("gmlp_fwd", T, tm, uvz, W["ln_g"], W["ln_b"], W["wm"], W["bs"])
    xc = conv_fwd("conv_fwd", T, tm, xbc, W["conv_w"], W["conv_b"])
    y_ssd, sprev = ssd_fwd("ssd_fwd", T, xc, dtr, W["dtb"], W["alog"], W["dsk"])
    y_b = gate_fwd("gate_fwd", T, tm, y_ssd, uvz, W["gn"])
    half = D // 2
    wo4 = W["wo4"]
    ycol = [pl.BlockSpec((tf, half), lambda i, cb=cb: (i, cb)) for cb in range(2)]
    wo_s = [pl.BlockSpec((None, half, D), lambda i, s=s: (s, 0, 0)) for s in range(4)]
    mixo, h1, n1 = mm_fused("out_proj", T // tf, [(y_a, ycol[0], wo4, wo_s[0]), (y_a, ycol[1], wo4, wo_s[1]),
                                                  (y_b, ycol[0], wo4, wo_s[2]), (y_b, ycol[1], wo4, wo_s[3])], NN,
                            [(x, rows_f), (g(0, 1), vec_f), (g(0, 2), vec_f)], [out_f32, out_f32, out_bf16], [], resid_epilogue(True))
    W = dict(W)
    W["wg4"], W["wu4"], W["wd4"] = ffn_weights(h1)
    gate0, up0, act0, f1, h2 = ffn_fwd("0", n1, 0, (h1, g(0, 3)))
    pm, h3, n3 = pool_fwd("pool_fwd", T, tm, h2, g(1, 0), W["pool_w"], W["pool_b"], W["pool_scale"], g(1, 1), g(1, 2))
    gate1, up1, act1, f2, _ = ffn_fwd("1", n3, 1)
    dh4, d_f2, loss_acc, dg13 = final_loss_bwd("loss_bwd", T, tm, h3, f2, tgt, g(1, 3))
    d_h3, d_pm, dg12, dg11, dwg1, dwu1, dwd1 = ffn_bwd("1", 1, n3, gate1, up1, act1, d_f2, h3, pm, dh4, g(1, 2), g(1, 1), f32)
    d_h2, d_f1, G["pool_w"], G["pool_b"], G["pool_scale"], dg10, dg03 = pool_bwd("pool_bwd", T, tm, h2, d_pm, d_h3, g(1, 0), W["pool_w"], W["pool_b"],
                                                                                 W["pool_scale"], f1, g(0, 3))
    d_h1, d_mixo, dg02, dg01, dwg0, dwu0, dwd0 = ffn_bwd("0", 0, n1, gate0, up0, act0, d_f1, h1, mixo, d_h2, g(0, 2), g(0, 1), bf16)
    def d_ycat(nm, s0):
        return mm(nm, (2, T // tm, 1), [(d_mixo, row_spec, wo4, pl.BlockSpec((None, half, D), lambda j, i, k: (s0 + j, 0, 0)))], NT,
                  pl.BlockSpec((tm, half), lambda j, i, k: (i, j)), SDS((T, D), f32))

    def d_wo(nm, y):
        return mm(nm, (2, 1, T // TKW), [(y, pl.BlockSpec((TKW, half), lambda s, j, k: (k, s)), d_mixo, pl.BlockSpec((TKW, D), lambda s, j, k: (k, 0)))],
                  TN, pl.BlockSpec((None, half, D), lambda s, j, k: (s, 0, 0)), SDS((2, half, D), f32))
    d_ya, d_yb = d_ycat("out_proj_dya", 0), d_ycat("out_proj_dyb", 2)
    dwo_a, dwo_b = d_wo("out_proj_dwa", y_a), d_wo("out_proj_dwb", y_b)
    d_yssd, d_z, G["gn"] = gate_bwd("gate_bwd", T, tm, y_ssd, uvz, d_yb, W["gn"])
    d_xc, d_dtr, G["dtb"], G["alog"], G["dsk"] = ssd_bwd("ssd_bwd", T, xc, dtr, sprev, d_yssd, W["dtb"], W["alog"], W["dsk"])
    d_pre, G["conv_w"], G["conv_b"] = conv_bwd_pre("conv_bwd_pre", T, tm, xbc, d_xc, W["conv_w"], W["conv_b"])
    d_xbc = conv_bwd_x("conv_bwd_x", T, tm, d_pre, W["conv_w"])
    d_u, d_v, G["ln_g"], G["ln_b"], G["wm"], G["bs"] = gmlp_bwd("gmlp_bwd", T, tm, uvz, d_ya, W["ln_g"], W["ln_b"], W["wm"], W["bs"])
    w_u, w_v, w_z = W["w_uvz"][:, :D], W["w_uvz"][:, D:2 * D], W["w_uvz"][:, 2 * D:]
    def pre_epilogue(part, xs, os, accs):
        dx, dg = _rms_bwd(xs[0][...], xs[2][...], part)
        os[0][...] = xs[1][...] + dx
        accs[0][...] += dg
    blk = lambda w: pl.BlockSpec((tf, w), lambda i: (i, 0))
    whole = lambda a: pl.BlockSpec(a.shape, lambda i: (0, 0))
    grad_x, dg00 = mm_fused("in_dy0", T // tf, [(d_u, blk(D), w_u, whole(w_u)), (d_v, blk(D), w_v, whole(w_v)), (d_z, blk(D), w_z, whole(w_z)),
                                                (d_xbc, blk(CONV_DIM), W["w_xbc"], whole(W["w_xbc"])), (d_dtr, blk(DT_PAD), W["w_dt"], whole(W["w_dt"]))],
                            NT, [(x, rows_f), (d_h1, rows_f), (g(0, 0), vec_f)], [out_f32], [(1, D)], pre_epilogue)
    G["w_inT"] = [matmul("in_dwu", [(d_u, y0)], "tn", f32, 1024, 1024, TKW), matmul("in_dwv", [(d_v, y0)], "tn", f32, 1024, 1024, TKW),
                  matmul("in_dwz", [(d_z, y0)], "tn", f32, 1024, 1024, TKW), matmul("in_dwxbc", [(d_xbc, y0)], "tn", f32, 1024, 1024, TKW),
                  matmul("in_dwdt", [(d_dtr, y0)], "tn", f32, DT_PAD, 1024, TKW)[:N_HEADS]]
    G["norm_g"] = jnp.stack([jnp.concatenate([dg00, dg01, dg02, dg03], 0), jnp.concatenate([dg10, dg11, dg12, dg13], 0)])
    G["wo4"] = [dwo_a[0], dwo_a[1], dwo_b[0], dwo_b[1]]
    G["wgT4"], G["wuT4"], G["wd4"] = [dwg0, dwg1], [dwu0, dwu1], [dwd0, dwd1]
    return loss_acc, grad_x, G


def build_weights(Wf):
    causal = jnp.tril(jnp.ones((CHUNK, CHUNK), bool))
    w_in = Wf["w_in"].astype(bf16)
    pad16 = lambda v: jnp.pad(v.reshape(1, N_HEADS).astype(f32), ((0, 0), (0, DT_PAD - N_HEADS)))
    return {
        "norm_g": Wf["norm_g"],
        "w_uvz": w_in[:, :3 * D], "w_xbc": w_in[:, 3 * D:3 * D + CONV_DIM],
        "w_dt": jnp.pad(w_in[:, 3 * D + CONV_DIM:], ((0, 0), (0, DT_PAD - N_HEADS))),
        "ln_g": Wf["gm_ln_g"].reshape(1, D), "ln_b": Wf["gm_ln_b"].reshape(1, D),
        "wm": jnp.where(causal[None], Wf["gm_ws"], 0).astype(bf16), "bs": Wf["gm_bs"].reshape(GM_HEADS, CHUNK, 1),
        "conv_w": Wf["conv_w"], "conv_b": Wf["conv_b"].reshape(1, CONV_DIM),
        "dtb": pad16(Wf["dt_bias"]), "alog": pad16(Wf["a_log"]), "dsk": pad16(Wf["d_skip"]),
        "gn": Wf["ssm_norm_g"].reshape(1, D),
        "wo4": Wf["wo4"].astype(bf16),
        "pool_w": Wf["pool_w"].astype(bf16), "pool_b": Wf["pool_b"].reshape(1, D), "pool_scale": Wf["pool_scale"].reshape(1, D),
    }


def small_grads(G):
    return {
        "norm_g": G["norm_g"],
        "gm_ln_g": G["ln_g"].reshape(D), "gm_ln_b": G["ln_b"].reshape(D),
        "gm_ws": G["wm"], "gm_bs": G["bs"].reshape(GM_HEADS, CHUNK),
        "conv_w": G["conv_w"], "conv_b": G["conv_b"].reshape(CONV_DIM),
        "dt_bias": G["dtb"][0, :N_HEADS], "a_log": G["alog"][0, :N_HEADS], "d_skip": G["dsk"][0, :N_HEADS],
        "ssm_norm_g": G["gn"].reshape(D),
        "pool_b": G["pool_b"].reshape(4, POOL_GD), "pool_scale": G["pool_scale"].reshape(D),
    }


MESH_ID = pl.DeviceIdType.MESH
ANY = pl.BlockSpec(memory_space=pl.ANY)


DMA_CHUNK_BYTES = 2 << 20
DMA_MAX_CHUNKS = 32


def _pieces(view, axis, align):
    shape = view.shape
    nbytes = math.prod(shape) * jnp.dtype(view.dtype).itemsize
    n = max(1, min(DMA_MAX_CHUNKS, -(-nbytes // DMA_CHUNK_BYTES)))
    rows = shape[axis]
    size = -(-rows // n)
    size = -(-size // align) * align
    out = []
    for s in range(0, rows, size):
        idx = [slice(None)] * len(shape)
        idx[axis] = pl.ds(s, min(size, rows - s))
        out.append(tuple(idx))
    return out


def comm_call(name, operands, out_shapes, plan):
    n_in = len(operands)
    n_out = len(out_shapes)
    n_remote, n_local = plan((0, 0, 0), [None] * n_in, [None] * n_out, True)

    def body(*refs):
        in_refs, out_refs = refs[:n_in], refs[n_in:n_in + n_out]
        send_sems, recv_sems, local_sems = refs[n_in + n_out:]
        me = (lax.axis_index("x"), lax.axis_index("y"), lax.axis_index("c"))
        remote, local = plan(me, in_refs, out_refs, False)
        align = lambda v: 16 if v.dtype == bf16 else 8
        for j, (s, d, axis) in enumerate(local):
            for ix in _pieces(s, axis, align(s)):
                pltpu.make_async_copy(s.at[ix], d.at[ix], local_sems.at[j]).start()
        peers = [tuple((1 - m) if f else m for m, f in zip(me, flip)) for flip, *_ in remote]
        for k, (flip, src, dst, _, axis) in enumerate(remote):
            for ix in _pieces(src, axis, align(src)):
                pltpu.make_async_remote_copy(src_ref=src.at[ix], dst_ref=dst.at[ix], send_sem=send_sems.at[k], recv_sem=recv_sems.at[k],
                                             device_id=peers[k], device_id_type=MESH_ID).start()
        for k, (flip, src, dst, landing, axis) in enumerate(remote):
            pltpu.make_async_remote_copy(src_ref=landing, dst_ref=landing, send_sem=send_sems.at[k], recv_sem=recv_sems.at[k],
                                         device_id=peers[k], device_id_type=MESH_ID).wait_recv()
        for k, (flip, src, dst, landing, axis) in enumerate(remote):
            pltpu.make_async_remote_copy(src_ref=src, dst_ref=dst, send_sem=send_sems.at[k], recv_sem=recv_sems.at[k],
                                         device_id=peers[k], device_id_type=MESH_ID).wait_send()
        for j, (s, d, axis) in enumerate(local):
            pltpu.make_async_copy(s, d, local_sems.at[j]).wait()

    return pl.pallas_call(
        body, name=name, out_shape=list(out_shapes), in_specs=[ANY] * n_in, out_specs=[ANY] * n_out,
        scratch_shapes=[pltpu.SemaphoreType.DMA((n_remote,)), pltpu.SemaphoreType.DMA((n_remote,)), pltpu.SemaphoreType.DMA((max(n_local, 1),))],
    )(*operands)


CHIP_FLIPS = ((1, 0, 0), (0, 1, 0), (1, 1, 0))
PAIR_FLIP = (0, 0, 1)


def gather_over_chips(name, arrs):
    def plan(me, ins, outs, count):
        if count:
            return len(CHIP_FLIPS) * len(arrs), len(arrs)
        k = 2 * me[0] + me[1]
        remote, local = [], []
        for a in range(len(arrs)):
            for flip in CHIP_FLIPS:
                kp = 2 * ((1 - me[0]) if flip[0] else me[0]) + ((1 - me[1]) if flip[1] else me[1])
                remote.append((flip, ins[a], outs[a].at[k], outs[a].at[kp], 0))
            local.append((ins[a], outs[a].at[k], 0))
        return remote, local
    return comm_call(name, arrs, [SDS((4,) + a.shape, a.dtype) for a in arrs], plan)


def pair_split_exchange(name, p, rh):
    def plan(me, ins, outs, count):
        if count:
            return 1, 0
        theirs = ins[0].at[:, pl.ds(pl.multiple_of((1 - me[2]) * rh, 8), rh), :]
        return [(PAIR_FLIP, theirs, outs[0], outs[0], 1)], []
    return comm_call(name, [p], [SDS((4, rh, p.shape[2]), p.dtype)], plan)[0]


def scatter_over_chips(name, cs):
    def plan(me, ins, outs, count):
        if count:
            return len(CHIP_FLIPS), 0
        k = 2 * me[0] + me[1]
        remote = []
        for flip in CHIP_FLIPS:
            kp = 2 * ((1 - me[0]) if flip[0] else me[0]) + ((1 - me[1]) if flip[1] else me[1])
            remote.append((flip, ins[0].at[kp], outs[0].at[k], outs[0].at[kp], 0))
        return remote, []
    return comm_call(name, [cs], [SDS(cs.shape, cs.dtype)], plan)[0]


def pair_swap(name, half):
    def plan(me, ins, outs, count):
        if count:
            return 1, 0
        return [(PAIR_FLIP, ins[0], outs[0], outs[0], 0)], []
    return comm_call(name, [half], [SDS(half.shape, half.dtype)], plan)[0]


def _row_tile(rows, cap=512):
    if rows <= cap:
        return rows
    t = cap - cap % 8
    while rows % t:
        t -= 8
    return t


SUM_ROWS = 448


def pair_sum(name, packs, got, c_arr):
    rh = got.shape[1]
    nb = rh // SUM_ROWS

    def kern(c_ref, a_ref, b_ref, o16_ref):
        o16_ref[...] = (a_ref[...] + b_ref[...]).astype(bf16)
    blk = (None, SUM_ROWS, D)
    grid_spec = pltpu.PrefetchScalarGridSpec(
        num_scalar_prefetch=1, grid=(4, nb),
        in_specs=[pl.BlockSpec(blk, lambda s, i, c: (s, c[0] * nb + i, 0)), pl.BlockSpec(blk, lambda s, i, c: (s, i, 0))],
        out_specs=pl.BlockSpec(blk, lambda s, i, c: (s, i, 0)))
    return pl.pallas_call(kern, name=name, grid_spec=grid_spec, out_shape=SDS(got.shape, bf16),
                          compiler_params=pltpu.CompilerParams(dimension_semantics=("parallel", "parallel")))(c_arr, packs, got)


def chip_sum(name, own16, landed16, k_arr):
    rh = own16.shape[1]
    nb = rh // SUM_ROWS

    def kern(k_ref, own_ref, l0, l1, l2, l3, o_ref):
        k = k_ref[0]
        s = None
        for j, lref in enumerate((l0, l1, l2, l3)):
            t = jnp.where(k == j, own_ref[...], lref[...]).astype(f32)
            s = t if s is None else s + t
        o_ref[...] = s
    blk = (None, SUM_ROWS, D)
    land = [pl.BlockSpec(blk, lambda i, k, j=j: (jnp.where(k[0] == j, (j + 1) % N_CHIPS, j), i, 0)) for j in range(N_CHIPS)]
    grid_spec = pltpu.PrefetchScalarGridSpec(
        num_scalar_prefetch=1, grid=(nb,),
        in_specs=[pl.BlockSpec(blk, lambda i, k: (k[0], i, 0))] + land,
        out_specs=pl.BlockSpec((SUM_ROWS, D), lambda i, k: (i, 0)))
    return pl.pallas_call(kern, name=name, grid_spec=grid_spec, out_shape=SDS((rh, D), f32),
                          compiler_params=pltpu.CompilerParams(dimension_semantics=("parallel",)))(k_arr, own16, landed16, landed16, landed16, landed16)


def adamw(name, w, g, m, v):
    R, C = w.shape
    tr = _row_tile(R, 256)

    def kern(w_ref, g_ref, m_ref, v_ref, d_ref, mo_ref, vo_ref):
        gg = g_ref[...]
        mn = ADAM_B1 * m_ref[...] + (1.0 - ADAM_B1) * gg
        vn = ADAM_B2 * v_ref[...] + (1.0 - ADAM_B2) * jnp.square(gg)
        m_hat = mn / (1.0 - ADAM_B1 ** ADAM_STEP)
        v_hat = vn / (1.0 - ADAM_B2 ** ADAM_STEP)
        d_ref[...] = -ADAM_LR * (m_hat / (jnp.sqrt(v_hat) + ADAM_EPS) + ADAM_WD * w_ref[...])
        mo_ref[...] = mn
        vo_ref[...] = vn
    spec = pl.BlockSpec((tr, C), lambda i: (i, 0))
    s = SDS((R, C), f32)
    return pl.pallas_call(kern, name=name, grid=(R // tr,), in_specs=[spec] * 4, out_specs=[spec] * 3, out_shape=[s, s, s],
                          compiler_params=pltpu.CompilerParams(dimension_semantics=("parallel",)))(w, g, m, v)


WEIGHT_NAMES = ("norm_g", "w_in", "gm_ln_g", "gm_ln_b", "gm_ws", "gm_bs", "conv_w", "conv_b", "dt_bias", "a_log", "d_skip",
                "ssm_norm_g", "w_out", "pool_w", "pool_b", "pool_scale", "ffn_w_gate", "ffn_w_up", "ffn_w_down")
SMALL = ("norm_g", "conv_w", "pool_b", "pool_scale")
REPL = ("gm_ln_g", "gm_ln_b", "gm_ws", "gm_bs", "conv_b", "dt_bias", "a_log", "d_skip", "ssm_norm_g")
SMALL_AXIS = {"norm_g": 2, "conv_w": 1, "pool_b": 1, "pool_scale": 0}
N_CHIPS = 4
IN_SH = IN_DIM // N_CHIPS
SMALL_ROWS = 8
REPL_ROWS = 72
OFF_OUT, OFF_GATE, OFF_UP, OFF_DOWN = 0, 512, 512 + 2 * FF_SH, 512 + 4 * FF_SH
OFF_POOL = OFF_DOWN + 2 * FF_SH
OFF_SMALL = OFF_POOL + 64
OFF_REPL = OFF_SMALL + SMALL_ROWS
OFF_IN = OFF_REPL + REPL_ROWS
SLOT_END = OFF_IN + IN_SH
SLOT_ROWS = 6272
HALF_ROWS = SLOT_ROWS // 2


def _flat_rows(pieces, rows):
    v = jnp.concatenate([p.reshape(-1) for p in pieces])
    return jnp.pad(v, (0, rows * D - v.shape[0])).reshape(rows, D)


def _shard_small(name, full, k):
    ax = SMALL_AXIS[name]
    n = full.shape[ax] // N_CHIPS
    return lax.slice_in_dim(full, k * n, (k + 1) * n, axis=ax)


def _drop1(name, a):
    return a if name == "norm_g" else a[0]


def _row_range(blocks, lo, hi):
    out, off = [], 0
    for b in blocks:
        n = b.shape[0]
        a, e = max(lo, off), min(hi, off + n)
        if a < e:
            out.append(b[a - off:e - off])
        off += n
    return out


HBM_SPEC = pl.BlockSpec(memory_space=pltpu.HBM)
SEM_SPEC = pl.BlockSpec(memory_space=pltpu.SEMAPHORE)
SPLIT_EFFECT = pltpu.SideEffectType.DATAFLOW_SIDE_EFFECTING


def _chip_of(me, flip):
    return 2 * ((1 - me[0]) if flip[0] else me[0]) + ((1 - me[1]) if flip[1] else me[1])


def gather_start(name, arrs, after):
    n = len(arrs)
    ncp = n * len(CHIP_FLIPS)

    def body(*refs):
        srcs, lands = refs[:n], refs[n:2 * n]
        send_sems, recv_sems, token = refs[2 * n + 1], refs[2 * n + 2], refs[-1]
        me = (lax.axis_index("x"), lax.axis_index("y"), lax.axis_index("c"))
        k = 2 * me[0] + me[1]
        for a in range(n):
            for f, flip in enumerate(CHIP_FLIPS):
                peer = tuple((1 - m) if fl else m for m, fl in zip(me, flip))
                for ix in _pieces(srcs[a], 0, 16):
                    pltpu.make_async_remote_copy(src_ref=srcs[a].at[ix], dst_ref=lands[a].at[k].at[ix],
                                                 send_sem=send_sems.at[a * len(CHIP_FLIPS) + f], recv_sem=recv_sems.at[a * len(CHIP_FLIPS) + f],
                                                 device_id=peer, device_id_type=MESH_ID).start()
        token[...] = jnp.zeros_like(token)

    land_shapes = [(N_CHIPS,) + a.shape for a in arrs]
    operands = [pltpu.with_memory_space_constraint(a, pltpu.HBM) for a in arrs]
    operands += [pltpu.with_memory_space_constraint(lax.empty(s, a.dtype), pltpu.HBM) for s, a in zip(land_shapes, arrs)]
    out = pl.pallas_call(
        body, name=name,
        out_shape=(pltpu.SemaphoreType.DMA((ncp,)), pltpu.SemaphoreType.DMA((ncp,)), *[pltpu.HBM(a.shape, a.dtype) for a in arrs],
                   *[pltpu.HBM(s, a.dtype) for s, a in zip(land_shapes, arrs)], SDS((8, 128), f32)),
        in_specs=[HBM_SPEC] * (2 * n) + [ANY], out_specs=(SEM_SPEC, SEM_SPEC, *[HBM_SPEC] * (2 * n), pl.BlockSpec(memory_space=pltpu.VMEM)),
        input_output_aliases={i: 2 + i for i in range(2 * n)},
        compiler_params=pltpu.CompilerParams(has_side_effects=SPLIT_EFFECT),
    )(*operands, after)
    return out[0], out[1], out[2:2 + n], out[2 + n:2 + 2 * n], out[-1]


def gather_wait(name, send_sems, recv_sems, thru, lands, after):
    n = len(thru)

    def body(*refs):
        srcs, lands_r = refs[:n], refs[n:2 * n]
        s_sems, r_sems = refs[2 * n], refs[2 * n + 1]
        me = (lax.axis_index("x"), lax.axis_index("y"), lax.axis_index("c"))
        k = 2 * me[0] + me[1]
        for a in range(n):
            for f, flip in enumerate(CHIP_FLIPS):
                peer = tuple((1 - m) if fl else m for m, fl in zip(me, flip))
                idx = a * len(CHIP_FLIPS) + f
                pltpu.make_async_remote_copy(src_ref=srcs[a], dst_ref=lands_r[a].at[k], send_sem=s_sems.at[idx], recv_sem=r_sems.at[idx],
                                             device_id=peer, device_id_type=MESH_ID).wait_send()
                pltpu.make_async_remote_copy(src_ref=srcs[a], dst_ref=lands_r[a].at[_chip_of(me, flip)], send_sem=s_sems.at[idx],
                                             recv_sem=r_sems.at[idx], device_id=peer, device_id_type=MESH_ID).wait_recv()

    out = pl.pallas_call(
        body, name=name, out_shape=tuple(pltpu.HBM(t.shape, t.dtype) for t in (*thru, *lands)),
        in_specs=[HBM_SPEC] * (2 * n) + [SEM_SPEC, SEM_SPEC, ANY], out_specs=tuple([HBM_SPEC] * (2 * n)),
        input_output_aliases={i: i for i in range(2 * n)},
        compiler_params=pltpu.CompilerParams(has_side_effects=SPLIT_EFFECT),
    )(*thru, *lands, send_sems, recv_sems, after)
    return out[n:]


def gather_weights(w_sh):
    big = [w_sh["w_in"][0], w_sh["w_out"][0], w_sh["pool_w"][0].reshape(4 * 64, POOL_GD)]
    small_pack = _flat_rows([w_sh[n] for n in SMALL], SMALL_ROWS)
    s_in, s_out, s_pool, s_small = gather_over_chips("gather_weights", [b.astype(bf16) for b in big] + [small_pack])
    Wf = {n: w_sh[n][0] for n in REPL}
    Wf["w_in"] = s_in.transpose(1, 0, 2).reshape(D, IN_DIM)
    Wf["pool_w"] = s_pool.reshape(N_CHIPS, 4, 64, POOL_GD).transpose(1, 0, 2, 3).reshape(4, POOL_GD, POOL_GD)
    Wf["wo4"] = s_out
    small_shapes = [_drop1(n, w_sh[n]).shape for n in SMALL]
    parts = [_split_rows(s_small[k], small_shapes) for k in range(N_CHIPS)]
    for j, n in enumerate(SMALL):
        Wf[n] = jnp.concatenate([parts[k][j] for k in range(N_CHIPS)], axis=SMALL_AXIS[n])
    return Wf


def pack_grads(G):
    sg = small_grads(G)
    repl = _flat_rows([sg[n] for n in REPL], REPL_ROWS)
    w_in_t = jnp.concatenate(G["w_inT"], axis=0)
    slots = []
    for k in range(N_CHIPS):
        rows = [G["wo4"][k], G["wgT4"][0][k], G["wgT4"][1][k], G["wuT4"][0][k], G["wuT4"][1][k], G["wd4"][0][k], G["wd4"][1][k],
                G["pool_w"][:, k * 64:(k + 1) * 64, :].reshape(64, D), _flat_rows([_shard_small(n, sg[n], k) for n in SMALL], SMALL_ROWS), repl]
        rows.append(jnp.pad(w_in_t[k * IN_SH:(k + 1) * IN_SH], ((0, SLOT_ROWS - SLOT_END), (0, 0))))
        slots.append(jnp.concatenate(rows, axis=0))
    return jnp.stack(slots)


def unpack_grads(total, w_sh):
    g = {"w_out": total[OFF_OUT:OFF_GATE], "ffn_w_down": total[OFF_DOWN:OFF_POOL], "pool_w": total[OFF_POOL:OFF_SMALL],
         "ffn_w_gate": jnp.stack([total[OFF_GATE + l * FF_SH:OFF_GATE + (l + 1) * FF_SH].T for l in range(2)]),
         "ffn_w_up": jnp.stack([total[OFF_UP + l * FF_SH:OFF_UP + (l + 1) * FF_SH].T for l in range(2)]),
         "w_in": total[OFF_IN:SLOT_END].T}
    small = _split_rows(total[OFF_SMALL:OFF_REPL], [_drop1(n, w_sh[n]).shape for n in SMALL])
    repl = _split_rows(total[OFF_REPL:OFF_IN], [w_sh[n][0].shape for n in REPL])
    g.update(zip(SMALL, small))
    g.update(zip(REPL, repl))
    return {n: g[n].reshape(w_sh[n].shape) for n in WEIGHT_NAMES}


def _split_rows(flat2d, shapes):
    v = flat2d.reshape(-1)
    out, off = [], 0
    for s in shapes:
        n = math.prod(s)
        out.append(v[off:off + n].reshape(s))
        off += n
    return out


def kernel(x, norm_g, w_in, gm_ln_g, gm_ln_b, gm_ws, gm_bs, conv_w, conv_b, dt_bias, a_log, d_skip, ssm_norm_g, w_out, pool_w, pool_b, pool_scale, ffn_w_gate, ffn_w_up, ffn_w_down, loss_target, m_norm_g, m_w_in, m_gm_ln_g, m_gm_ln_b, m_gm_ws, m_gm_bs, m_conv_w, m_conv_b, m_dt_bias, m_a_log, m_d_skip, m_ssm_norm_g, m_w_out, m_pool_w, m_pool_b, m_pool_scale, m_ffn_w_gate, m_ffn_w_up, m_ffn_w_down, v_norm_g, v_w_in, v_gm_ln_g, v_gm_ln_b, v_gm_ws, v_gm_bs, v_conv_w, v_conv_b, v_dt_bias, v_a_log, v_d_skip, v_ssm_norm_g, v_w_out, v_pool_w, v_pool_b, v_pool_scale, v_ffn_w_gate, v_ffn_w_up, v_ffn_w_down):
    T = x.shape[1]
    w_sh = dict(zip(WEIGHT_NAMES, (norm_g, w_in, gm_ln_g, gm_ln_b, gm_ws, gm_bs, conv_w, conv_b, dt_bias, a_log, d_skip, ssm_norm_g, w_out,
                                   pool_w, pool_b, pool_scale, ffn_w_gate, ffn_w_up, ffn_w_down)))
    m_sh = dict(zip(WEIGHT_NAMES, (m_norm_g, m_w_in, m_gm_ln_g, m_gm_ln_b, m_gm_ws, m_gm_bs, m_conv_w, m_conv_b, m_dt_bias, m_a_log, m_d_skip,
                                   m_ssm_norm_g, m_w_out, m_pool_w, m_pool_b, m_pool_scale, m_ffn_w_gate, m_ffn_w_up, m_ffn_w_down)))
    v_sh = dict(zip(WEIGHT_NAMES, (v_norm_g, v_w_in, v_gm_ln_g, v_gm_ln_b, v_gm_ws, v_gm_bs, v_conv_w, v_conv_b, v_dt_bias, v_a_log, v_d_skip,
                                   v_ssm_norm_g, v_w_out, v_pool_w, v_pool_b, v_pool_scale, v_ffn_w_gate, v_ffn_w_up, v_ffn_w_down)))

    my_k = 2 * lax.axis_index("x") + lax.axis_index("y")
    ffn_own = [w_sh["ffn_w_gate"].reshape(2 * D, FF_SH).astype(bf16), w_sh["ffn_w_up"].reshape(2 * D, FF_SH).astype(bf16),
               w_sh["ffn_w_down"].reshape(2 * FF_SH, D).astype(bf16)]
    Wf = gather_weights(w_sh)
    send_sems, recv_sems, thru, lands, token = gather_start("gather_ffn_start", ffn_own, Wf["wo4"])
    Wf["norm_g"] = Wf["norm_g"] + token[0, 0]
    W = build_weights(Wf)

    def ffn_weights(after):
        landed = gather_wait("gather_ffn_wait", send_sems, recv_sems, thru, lands, after)
        return tuple(lax.dynamic_update_slice(l, o[None], (my_k, 0, 0)) for l, o in zip(landed, ffn_own))

    loss_acc, grad_x, G = local_step(T, x[0], loss_target[0], W, ffn_weights)

    my_c = lax.axis_index("c")
    c_arr = my_c.astype(jnp.int32).reshape(1)
    k_arr = (2 * lax.axis_index("x") + lax.axis_index("y")).astype(jnp.int32).reshape(1)
    packs = pack_grads(G)
    got = pair_split_exchange("grads_pair_split", packs, HALF_ROWS)
    pair16 = pair_sum("grads_pair_sum", packs, got, c_arr)
    landed = scatter_over_chips("grads_scatter", pair16)
    half = chip_sum("grads_chip_sum", pair16, landed, k_arr)
    other = pair_swap("grads_pair_swap", half)
    total = jnp.concatenate([jnp.where(my_c == 0, half, other), jnp.where(my_c == 0, other, half)], axis=0)
    grads = unpack_grads(total, w_sh)

    delta, new_m, new_v = {}, {}, {}
    for n in WEIGHT_NAMES:
        shp = w_sh[n].shape
        two_d = (-1, shp[-1])
        d_, m_, v_ = adamw("adamw_" + n, w_sh[n].reshape(two_d), grads[n].reshape(two_d), m_sh[n].reshape(two_d), v_sh[n].reshape(two_d))
        delta[n], new_m[n], new_v[n] = d_.reshape(shp), m_.reshape(shp), v_.reshape(shp)

    loss = lax.psum(loss_acc[0, 0], ("x", "y", "c"))
    return (loss, grad_x[None], *[grads[n] for n in WEIGHT_NAMES], *[delta[n] for n in WEIGHT_NAMES],
            *[new_m[n] for n in WEIGHT_NAMES], *[new_v[n] for n in WEIGHT_NAMES])
```

```python
import functools
import math

import jax
import jax.numpy as jnp
from jax import lax
from jax.experimental import pallas as pl
from jax.experimental.pallas import tpu as pltpu

f32, bf16 = jnp.float32, jnp.bfloat16
SDS = jax.ShapeDtypeStruct

D = 1024
EPS = 1e-6
CHUNK = 128
GM_HEADS, GM_HD = 4, 256
SSM_GROUPS, SSM_HPG, SSM_P, SSM_N = 4, 4, 64, 128
N_HEADS = SSM_GROUPS * SSM_HPG
CONV_K = 4
CONV_DIM = 2048
POOL_WINDOWS = (2, 4, 8, 16)
POOL_GD = 256
POOL_HALO = 16
CONV_HALO = 8
D_FF = 2816
DT_PAD = 128
IN_DIM = 5136

ADAM_LR, ADAM_B1, ADAM_B2, ADAM_EPS, ADAM_WD, ADAM_STEP = 0.001, 0.9, 0.999, 1e-08, 0.01, 10

NT = (((1,), (1,)), ((), ()))
TN = (((0,), (0,)), ((), ()))
NN = (((1,), (0,)), ((), ()))
HI = lax.Precision.HIGHEST


def _silu(x):
    return x * jax.nn.sigmoid(x)


def _softplus(x):
    return jnp.maximum(x, 0.0) + jnp.log1p(jnp.exp(-jnp.abs(x)))


def _rms(x, g):
    return x * lax.rsqrt(jnp.mean(x * x, axis=-1, keepdims=True) + EPS) * g


def _rms_bwd(x, g, dy):
    r = lax.rsqrt(jnp.mean(x * x, axis=-1, keepdims=True) + EPS)
    xh = x * r
    dxh = dy * g
    dx = r * (dxh - xh * jnp.mean(dxh * xh, axis=-1, keepdims=True))
    return dx, jnp.sum(dy * xh, axis=0, keepdims=True)


def _bdot(a, b, dims=NN):
    return lax.dot_general(a.astype(bf16), b.astype(bf16), dims, preferred_element_type=f32)


def matmul(name, pairs, mode, out_dtype, tm, tn, tk=None):
    a0, b0 = pairs[0]
    if mode == "tn":
        M, N, K = a0.shape[1], b0.shape[1], a0.shape[0]
    else:
        M, K = a0.shape
        N = b0.shape[1] if mode == "nn" else b0.shape[0]
    tm, tn = min(tm, M), min(tn, N)
    assert M % tm == 0 and N % tn == 0, (name, M, N, tm, tn)
    if tk is None:
        nk = 1
    else:
        assert len(pairs) == 1 and K % tk == 0
        nk = K // tk
    dims = {"nn": NN, "nt": NT, "tn": TN}[mode]
    in_specs, args = [], []
    for a, b in pairs:
        kk = (a.shape[0] if mode == "tn" else a.shape[1]) if tk is None else tk
        if mode == "tn":
            in_specs.append(pl.BlockSpec((kk, tm), lambda j, i, k: (k, i)))
            in_specs.append(pl.BlockSpec((kk, tn), lambda j, i, k: (k, j)))
        elif mode == "nn":
            in_specs.append(pl.BlockSpec((tm, kk), lambda j, i, k: (i, k)))
            in_specs.append(pl.BlockSpec((kk, tn), lambda j, i, k: (k, j)))
        else:
            in_specs.append(pl.BlockSpec((tm, kk), lambda j, i, k: (i, k)))
            in_specs.append(pl.BlockSpec((tn, kk), lambda j, i, k: (j, k)))
        args += [a, b]
    npairs = len(pairs)

    def kern(*refs):
        o = refs[2 * npairs]
        part = None
        for p in range(npairs):
            d = _bdot(refs[2 * p][...], refs[2 * p + 1][...], dims)
            part = d if part is None else part + d
        if nk == 1:
            o[...] = part.astype(out_dtype)
        else:
            acc = refs[2 * npairs + 1]
            k = pl.program_id(2)

            @pl.when(k == 0)
            def _():
                acc[...] = part

            @pl.when(k > 0)
            def _():
                acc[...] += part

            @pl.when(k == nk - 1)
            def _():
                o[...] = acc[...].astype(out_dtype)

    return pl.pallas_call(
        kern, name=name, grid=(N // tn, M // tm, nk),
        in_specs=in_specs, out_specs=pl.BlockSpec((tm, tn), lambda j, i, k: (i, j)),
        out_shape=SDS((M, N), out_dtype),
        scratch_shapes=[pltpu.VMEM((tm, tn), f32)] if nk > 1 else [],
        compiler_params=pltpu.CompilerParams(dimension_semantics=("parallel", "parallel", "arbitrary")),
    )(*args)


def mm(name, grid, pairs, dims, o_spec, out_shape):
    nk = grid[2]
    npairs = len(pairs)
    in_specs, args = [], []
    for a, a_spec, b, b_spec in pairs:
        in_specs += [a_spec, b_spec]
        args += [a, b]
    blk = tuple(d for d in o_spec.block_shape if d is not None)

    def kern(*refs):
        o = refs[2 * npairs]
        part = None
        for p in range(npairs):
            d = _bdot(refs[2 * p][...], refs[2 * p + 1][...], dims)
            part = d if part is None else part + d
        if nk == 1:
            o[...] = part.astype(o.dtype)
        else:
            acc = refs[2 * npairs + 1]
            k = pl.program_id(2)

            @pl.when(k == 0)
            def _():
                acc[...] = part

            @pl.when(k > 0)
            def _():
                acc[...] += part

            @pl.when(k == nk - 1)
            def _():
                o[...] = acc[...].astype(o.dtype)

    return pl.pallas_call(
        kern, name=name, grid=grid, in_specs=in_specs, out_specs=o_spec, out_shape=out_shape,
        scratch_shapes=[pltpu.VMEM(blk, f32)] if nk > 1 else [],
        compiler_params=pltpu.CompilerParams(dimension_semantics=("parallel", "parallel", "arbitrary")),
    )(*args)


def mm_fused(name, n_row_blocks, pairs, dims, extra_ins, outs, accs, epilogue):
    npairs, nx, no, na = len(pairs), len(extra_ins), len(outs), len(accs)
    in_specs, args = [], []
    for a, a_spec, b, b_spec in pairs:
        in_specs += [a_spec, b_spec]
        args += [a, b]
    for arr, spec in extra_ins:
        in_specs.append(spec)
        args.append(arr)

    def kern(*refs):
        part = None
        for p in range(npairs):
            d = _bdot(refs[2 * p][...], refs[2 * p + 1][...], dims)
            part = d if part is None else part + d
        x_refs = refs[2 * npairs:2 * npairs + nx]
        o_refs = refs[2 * npairs + nx:2 * npairs + nx + no]
        a_refs = refs[2 * npairs + nx + no:]
        if na:
            @pl.when(pl.program_id(0) == 0)
            def _():
                for a in a_refs:
                    a[...] = jnp.zeros(a.shape, f32)
        epilogue(part, x_refs, o_refs, a_refs)

    return pl.pallas_call(
        kern, name=name, grid=(n_row_blocks,), in_specs=in_specs,
        out_specs=[spec for _, spec in outs] + [pl.BlockSpec(tuple(s), lambda i, nd=len(s): (0,) * nd) for s in accs],
        out_shape=[s for s, _ in outs] + [SDS(tuple(s), f32) for s in accs],
        compiler_params=pltpu.CompilerParams(dimension_semantics=("arbitrary",)),
    )(*args)


FF_SH = D_FF // 4


def ffn_up(name, T, tm, n_bf, wg4, wu4, l):
    def kern(n_ref, wg_ref, wu_ref, g_ref, u_ref, a_ref):
        n = n_ref[...]
        g = jnp.dot(n, wg_ref[...], preferred_element_type=f32)
        u = jnp.dot(n, wu_ref[...], preferred_element_type=f32)
        g_ref[...] = g.astype(bf16)
        u_ref[...] = u.astype(bf16)
        a_ref[...] = (_silu(g) * u).astype(bf16)
    w_spec = pl.BlockSpec((None, D, FF_SH), lambda k, i: (k, l, 0))
    o_spec = pl.BlockSpec((None, tm, FF_SH), lambda k, i: (k, i, 0))
    s = SDS((4, T, FF_SH), bf16)
    return pl.pallas_call(kern, name=name, grid=(4, T // tm), in_specs=[pl.BlockSpec((tm, D), lambda k, i: (i, 0)), w_spec, w_spec],
                          out_specs=[o_spec] * 3, out_shape=[s, s, s],
                          compiler_params=pltpu.CompilerParams(dimension_semantics=("parallel", "parallel")))(n_bf, wg4, wu4)


def ffn_dgu(name, T, tm, d_f, wd4, gate4, up4, l):
    rc = 16

    def kern(df_ref, wd_ref, g_ref, u_ref, dg_ref, du_ref, dact_ref):
        dact_ref[...] = _bdot(df_ref[...], wd_ref[...], NT)
        for r0 in range(0, tm, rc):
            rows = pl.ds(r0, rc)
            _, vjp = jax.vjp(lambda a, b: _silu(a) * b, g_ref[rows, :].astype(f32), u_ref[rows, :].astype(f32))
            dg, du = vjp(dact_ref[rows, :])
            dg_ref[rows, :] = dg.astype(bf16)
            du_ref[rows, :] = du.astype(bf16)
    a_spec = pl.BlockSpec((None, tm, FF_SH), lambda k, i: (k, i, 0))
    s = SDS((4, T, FF_SH), bf16)
    return pl.pallas_call(kern, name=name, grid=(4, T // tm),
                          in_specs=[pl.BlockSpec((tm, D), lambda k, i: (i, 0)), pl.BlockSpec((None, FF_SH, D), lambda k, i: (k, l, 0)), a_spec, a_spec],
                          out_specs=[a_spec] * 2, out_shape=[s, s], scratch_shapes=[pltpu.VMEM((tm, FF_SH), f32)],
                          compiler_params=pltpu.CompilerParams(dimension_semantics=("parallel", "parallel")))(d_f, wd4, gate4, up4)


def rowcall(name, body, T, tm, ins, outs, accs=(), scratch=(), reverse=False, sub=None):
    n = T // tm
    assert T % tm == 0

    def blk(i):
        return (n - 1 - i) if reverse else i

    in_specs, args = [], []
    for spec in ins:
        kind, arr = spec[0], spec[1]
        if kind == "row":
            _, _, w, cb = spec
            in_specs.append(pl.BlockSpec((tm, w), lambda i, cb=cb: (blk(i), cb)))
        elif kind == "prev":
            _, _, w, cb, h = spec
            r = tm // h
            in_specs.append(pl.BlockSpec((h, w), lambda i, cb=cb, r=r: (jnp.maximum(blk(i) * r - 1, 0), cb)))
        elif kind == "next":
            _, _, w, cb, h = spec
            r = tm // h
            in_specs.append(pl.BlockSpec((h, w), lambda i, cb=cb, r=r, h=h: (jnp.minimum((blk(i) + 1) * r, T // h - 1), cb)))
        else:
            nd = arr.ndim
            in_specs.append(pl.BlockSpec(arr.shape, lambda i, nd=nd: (0,) * nd))
        args.append(arr)
    out_shape = [SDS((T, w), dt) for w, dt in outs] + [SDS(tuple(s), f32) for s in accs]
    out_specs = [pl.BlockSpec((tm, w), lambda i: (blk(i), 0)) for w, _ in outs]
    out_specs += [pl.BlockSpec(tuple(s), lambda i, nd=len(s): (0,) * nd) for s in accs]
    ni, no, na = len(ins), len(outs), len(accs)

    def kern(*refs):
        i = pl.program_id(0)
        in_refs, out_refs = refs[:ni], refs[ni:ni + no]
        acc_refs, scr = refs[ni + no:ni + no + na], refs[ni + no + na:]
        if na:
            @pl.when(i == 0)
            def _():
                for a in acc_refs:
                    a[...] = jnp.zeros(a.shape, f32)
        if sub is None or sub >= tm:
            body(blk(i), n, in_refs, out_refs, acc_refs, scr)
        else:
            for r0 in range(0, tm, sub):
                rows = pl.ds(r0, sub)
                body(blk(i), n, [r.at[rows, :] if spec[0] == "row" else r for r, spec in zip(in_refs, ins)],
                     [o.at[rows, :] for o in out_refs], acc_refs, [s.at[rows, :] for s in scr])

    res = pl.pallas_call(
        kern, name=name, grid=(n,), in_specs=in_specs, out_specs=out_specs, out_shape=out_shape,
        scratch_shapes=list(scratch),
        compiler_params=pltpu.CompilerParams(dimension_semantics=("arbitrary",)),
    )(*args)
    return res


def rms_to_bf16(name, T, tm, x, g):
    def body(i, n, ins, outs, accs, scr):
        outs[0][...] = _rms(ins[0][...], ins[1][...]).astype(bf16)
    return rowcall(name, body, T, tm, [("row", x, D, 0), ("const", g)], [(D, bf16)], sub=64)[0]


def resid_norm(name, T, tm, h_in, f, g_post, g_pre):
    def body(i, n, ins, outs, accs, scr):
        h = ins[0][...] + _rms(ins[1][...], ins[2][...])
        outs[0][...] = h
        if g_pre is not None:
            outs[1][...] = _rms(h, ins[3][...]).astype(bf16)
    ins = [("row", h_in, D, 0), ("row", f, D, 0), ("const", g_post)] + ([("const", g_pre)] if g_pre is not None else [])
    outs = [(D, f32)] + ([(D, bf16)] if g_pre is not None else [])
    return rowcall(name, body, T, tm, ins, outs)


def swiglu_act(name, T, tm, gate, up):
    def body(i, n, ins, outs, accs, scr):
        outs[0][...] = (_silu(ins[0][...]) * ins[1][...]).astype(bf16)
    return rowcall(name, body, T, tm, [("row", gate, D_FF, 0), ("row", up, D_FF, 0)], [(D_FF, bf16)])[0]


def swiglu_bwd(name, T, tm, gate, up, d_act):
    def body(i, n, ins, outs, accs, scr):
        _, vjp = jax.vjp(lambda a, b: _silu(a) * b, ins[0][...], ins[1][...])
        dg, du = vjp(ins[2][...])
        outs[0][...] = dg.astype(bf16)
        outs[1][...] = du.astype(bf16)
    return rowcall(name, body, T, tm, [("row", gate, D_FF, 0), ("row", up, D_FF, 0), ("row", d_act, D_FF, 0)],
                   [(D_FF, bf16), (D_FF, bf16)])


def final_loss_bwd(name, T, tm, h3, f2, tgt, g_post):
    def body(i, n, ins, outs, accs, scr):
        f, g = ins[1][...], ins[3][...]
        e = ins[0][...] + _rms(f, g) - ins[2][...]
        accs[0][...] += jnp.sum(jnp.sum(e * e, axis=-1, keepdims=True) * (0.5 / D), axis=0, keepdims=True)
        dh = e * (1.0 / D)
        df, dg = _rms_bwd(f, g, dh)
        outs[0][...] = dh
        outs[1][...] = df.astype(bf16)
        accs[1][...] += dg
    return rowcall(name, body, T, tm, [("row", h3, D, 0), ("row", f2, D, 0), ("row", tgt, D, 0), ("const", g_post)],
                   [(D, f32), (D, bf16)], accs=[(1, 1), (1, D)], sub=32)


def bwd_pre_post(name, T, tm, h_out, f, d_res, d_n, g_pre, g_post, df_dtype):
    def body(i, n, ins, outs, accs, scr):
        dx, dgp = _rms_bwd(ins[0][...], ins[4][...], ins[3][...])
        dh = ins[2][...] + dx
        df, dgq = _rms_bwd(ins[1][...], ins[5][...], dh)
        outs[0][...] = dh
        outs[1][...] = df.astype(df_dtype)
        accs[0][...] += dgp
        accs[1][...] += dgq
    return rowcall(name, body, T, tm,
                   [("row", h_out, D, 0), ("row", f, D, 0), ("row", d_res, D, 0), ("row", d_n, D, 0), ("const", g_pre), ("const", g_post)],
                   [(D, f32), (D, df_dtype)], accs=[(1, D), (1, D)])


def bwd_post(name, T, tm, f, d_h, g_post):
    def body(i, n, ins, outs, accs, scr):
        df, dg = _rms_bwd(ins[0][...], ins[2][...], ins[1][...])
        outs[0][...] = df.astype(bf16)
        accs[0][...] += dg
    return rowcall(name, body, T, tm, [("row", f, D, 0), ("row", d_h, D, 0), ("const", g_post)], [(D, bf16)], accs=[(1, D)])


def bwd_pre(name, T, tm, h, d_res, d_n, g_pre):
    def body(i, n, ins, outs, accs, scr):
        dx, dg = _rms_bwd(ins[0][...], ins[3][...], ins[2][...])
        outs[0][...] = ins[1][...] + dx
        accs[0][...] += dg
    return rowcall(name, body, T, tm, [("row", h, D, 0), ("row", d_res, D, 0), ("row", d_n, D, 0), ("const", g_pre)],
                   [(D, f32)], accs=[(1, D)])


def _layer_norm_parts(x):
    mu = jnp.mean(x, axis=-1, keepdims=True)
    xc = x - mu
    r = lax.rsqrt(jnp.mean(xc * xc, axis=-1, keepdims=True) + EPS)
    return xc * r, r


def gmlp_fwd(name, T, tm, uvz, ln_g, ln_b, wm, bs):
    def body(i, n, ins, outs, accs, scr):
        gu = jax.nn.gelu(ins[0][...])
        xh, _ = _layer_norm_parts(jax.nn.gelu(ins[1][...]))
        vln = (xh * ins[2][...] + ins[3][...]).astype(bf16)
        for c in range(ins[0].shape[0] // CHUNK):
            rows = slice(c * CHUNK, (c + 1) * CHUNK)
            for h in range(GM_HEADS):
                cols = slice(h * GM_HD, (h + 1) * GM_HD)
                mixed = jnp.dot(ins[4][h], vln[rows, cols], preferred_element_type=f32) + ins[5][h]
                outs[0][rows, cols] = (gu[rows, cols] * mixed).astype(bf16)
    return rowcall(name, body, T, tm, [("row", uvz, D, 0), ("row", uvz, D, 1), ("const", ln_g), ("const", ln_b), ("const", wm), ("const", bs)],
                   [(D, bf16)], sub=CHUNK)[0]


def gmlp_bwd(name, T, tm, uvz, d_ya, ln_g, ln_b, wm, bs):
    def body(i, n, ins, outs, accs, scr):
        u, v, dya = ins[0][...], ins[1][...], ins[2][...]
        gu, gelu_u_vjp = jax.vjp(jax.nn.gelu, u)
        gv, gelu_v_vjp = jax.vjp(jax.nn.gelu, v)
        xh, r = _layer_norm_parts(gv)
        lng = ins[3][...]
        vln = (xh * lng + ins[4][...]).astype(bf16)
        rr = lax.broadcasted_iota(jnp.int32, (CHUNK, CHUNK), 0)
        cc = lax.broadcasted_iota(jnp.int32, (CHUNK, CHUNK), 1)
        causal = (rr >= cc).astype(f32)
        dvln_ref = scr[0]
        dgu_ref = scr[1]
        for c in range(ins[0].shape[0] // CHUNK):
            rows = slice(c * CHUNK, (c + 1) * CHUNK)
            for h in range(GM_HEADS):
                cols = slice(h * GM_HD, (h + 1) * GM_HD)
                w = ins[5][h]
                blk = vln[rows, cols]
                mixed = jnp.dot(w, blk, preferred_element_type=f32) + ins[6][h]
                dy = dya[rows, cols]
                dgu_ref[rows, cols] = dy * mixed
                dm = dy * gu[rows, cols]
                accs[3][h] += jnp.sum(dm, axis=1, keepdims=True)
                accs[2][h] += _bdot(dm, blk, NT) * causal
                dvln_ref[rows, cols] = _bdot(w, dm, TN)
        dvln = dvln_ref[...]
        accs[0][...] += jnp.sum(dvln * xh, axis=0, keepdims=True)
        accs[1][...] += jnp.sum(dvln, axis=0, keepdims=True)
        dxh = dvln * lng
        dgv = r * (dxh - jnp.mean(dxh, axis=-1, keepdims=True) - xh * jnp.mean(dxh * xh, axis=-1, keepdims=True))
        outs[0][...] = gelu_u_vjp(dgu_ref[...])[0].astype(bf16)
        outs[1][...] = gelu_v_vjp(dgv)[0].astype(bf16)
    return rowcall(name, body, T, tm,
                   [("row", uvz, D, 0), ("row", uvz, D, 1), ("row", d_ya, D, 0), ("const", ln_g), ("const", ln_b), ("const", wm), ("const", bs)],
                   [(D, bf16), (D, bf16)], accs=[(1, D), (1, D), (GM_HEADS, CHUNK, CHUNK), (GM_HEADS, CHUNK, 1)],
                   scratch=[pltpu.VMEM((tm, D), f32), pltpu.VMEM((tm, D), f32)], sub=CHUNK)


CONV_RC, CONV_LB = 32, 512


def _conv_fill(i, x_ref, halo_ref, scr, tm):
    scr[pl.ds(0, CONV_HALO), :] = jnp.where(i > 0, halo_ref[...], 0.0)
    scr[pl.ds(CONV_HALO, tm), :] = x_ref[...]


def _conv_taps(scr, r0, lanes):
    return [scr[pl.ds(r0 + CONV_HALO - (CONV_K - 1) + k, CONV_RC), lanes] for k in range(CONV_K)]


def conv_fwd(name, T, tm, xbc, conv_w, conv_b):
    def body(i, n, ins, outs, accs, scr):
        s = scr[0]
        _conv_fill(i, ins[0], ins[1], s, tm)
        for lb in range(CONV_DIM // CONV_LB):
            lanes = slice(lb * CONV_LB, (lb + 1) * CONV_LB)
            w, b = ins[2][:, lanes], ins[3][:, lanes]

            for r0 in range(0, tm, CONV_RC):
                taps = _conv_taps(s, r0, lanes)
                pre = b + sum(w[k:k + 1] * taps[k] for k in range(CONV_K))
                outs[0][pl.ds(r0, CONV_RC), lanes] = _silu(pre)
    return rowcall(name, body, T, tm, [("row", xbc, CONV_DIM, 0), ("prev", xbc, CONV_DIM, 0, CONV_HALO), ("const", conv_w), ("const", conv_b)],
                   [(CONV_DIM, f32)], scratch=[pltpu.VMEM((tm + CONV_HALO, CONV_DIM), f32)])[0]


def conv_bwd_pre(name, T, tm, xbc, d_xc, conv_w, conv_b):
    def body(i, n, ins, outs, accs, scr):
        s = scr[0]
        _conv_fill(i, ins[0], ins[1], s, tm)
        fold = lambda v: jnp.sum(v.reshape(CONV_RC // 8, 8, CONV_LB), axis=0)
        for lb in range(CONV_DIM // CONV_LB):
            lanes = slice(lb * CONV_LB, (lb + 1) * CONV_LB)
            w, b = ins[3][:, lanes], ins[4][:, lanes]

            sums = [jnp.zeros((8, CONV_LB), f32)] * (CONV_K + 1)
            for r0 in range(0, tm, CONV_RC):
                taps = _conv_taps(s, r0, lanes)
                pre = b + sum(w[k:k + 1] * taps[k] for k in range(CONV_K))
                _, vjp = jax.vjp(_silu, pre)
                dpre = vjp(ins[2][pl.ds(r0, CONV_RC), lanes])[0]
                outs[0][pl.ds(r0, CONV_RC), lanes] = dpre
                sums = [sums[k] + fold(dpre * taps[k]) for k in range(CONV_K)] + [sums[CONV_K] + fold(dpre)]
            for k in range(CONV_K):
                accs[0][pl.ds(k, 1), lanes] += jnp.sum(sums[k], axis=0, keepdims=True)
            accs[1][:, lanes] += jnp.sum(sums[CONV_K], axis=0, keepdims=True)
    return rowcall(name, body, T, tm,
                   [("row", xbc, CONV_DIM, 0), ("prev", xbc, CONV_DIM, 0, CONV_HALO), ("row", d_xc, CONV_DIM, 0), ("const", conv_w), ("const", conv_b)],
                   [(CONV_DIM, f32)], accs=[(CONV_K, CONV_DIM), (1, CONV_DIM)], scratch=[pltpu.VMEM((tm + CONV_HALO, CONV_DIM), f32)])


def conv_bwd_x(name, T, tm, d_pre, conv_w):
    def body(i, n, ins, outs, accs, scr):
        s = scr[0]
        s[pl.ds(0, tm), :] = ins[0][...]
        s[pl.ds(tm, CONV_HALO), :] = jnp.where(i < n - 1, ins[1][...], 0.0)
        for lb in range(CONV_DIM // CONV_LB):
            lanes = slice(lb * CONV_LB, (lb + 1) * CONV_LB)
            w = ins[2][:, lanes]

            for r0 in range(0, tm, CONV_RC):
                dx = sum(w[k:k + 1] * s[pl.ds(r0 + CONV_K - 1 - k, CONV_RC), lanes] for k in range(CONV_K))
                outs[0][pl.ds(r0, CONV_RC), lanes] = dx.astype(bf16)
    return rowcall(name, body, T, tm, [("row", d_pre, CONV_DIM, 0), ("next", d_pre, CONV_DIM, 0, CONV_HALO), ("const", conv_w)],
                   [(CONV_DIM, bf16)], scratch=[pltpu.VMEM((tm + CONV_HALO, CONV_DIM), f32)])[0]


def _ssd_prep(dtr, dtb, alog):
    rr = lax.broadcasted_iota(jnp.int32, (CHUNK, CHUNK), 0)
    cc = lax.broadcasted_iota(jnp.int32, (CHUNK, CHUNK), 1)
    dt = _softplus(dtr + dtb)
    dA = dt * -jnp.exp(alog)
    acum = jnp.dot((rr >= cc).astype(f32), dA, precision=HI, preferred_element_type=f32)
    return dt, acum, acum.T, jnp.sum(dA, axis=0, keepdims=True)


def _ssd_group(g, x, Bm, Cm, S, dt, acum, acumT, tot, dsk):
    rr = lax.broadcasted_iota(jnp.int32, (CHUNK, CHUNK), 0)
    cc = lax.broadcasted_iota(jnp.int32, (CHUNK, CHUNK), 1)
    tril = rr >= cc
    lane = lax.broadcasted_iota(jnp.int32, (1, DT_PAD), 1)
    sub = lax.broadcasted_iota(jnp.int32, (DT_PAD, 1), 0)
    glane = lax.broadcasted_iota(jnp.int32, (1, SSM_HPG * SSM_P), 1) // SSM_P
    hm = [(glane == r).astype(f32) for r in range(SSM_HPG)]
    pick = lambda v, r: jnp.sum(v * (lane == SSM_HPG * g + r).astype(f32), axis=1, keepdims=True)
    cols = [pick(acum, r) for r in range(SSM_HPG)]
    tots = [pick(tot, r) for r in range(SSM_HPG)]
    spread = lambda vals: sum(vals[r] * hm[r] for r in range(SSM_HPG))
    xdt = x * spread([pick(dt, r) for r in range(SSM_HPG)])
    cb = _bdot(Cm, Bm, NT)
    y = x * spread([pick(dsk, r) for r in range(SSM_HPG)])
    for r in range(SSM_HPG):
        row = jnp.sum(acumT * (sub == SSM_HPG * g + r).astype(f32), axis=0, keepdims=True)
        dec = jnp.exp(jnp.where(tril, cols[r] - row, -jnp.inf))
        y = y + _bdot(cb * dec, xdt * hm[r])
    y = y + _bdot(Cm, S) * spread([jnp.exp(c) for c in cols])
    dte = spread([jnp.exp(tots[r] - cols[r]) for r in range(SSM_HPG)])
    s_new = S * spread([jnp.exp(t) for t in tots]) + _bdot(Bm, xdt * dte, TN)
    return y, s_new


def _ssd_ins(xc, dtr):
    gw = SSM_HPG * SSM_P
    ins = [("row", xc, gw, g) for g in range(SSM_GROUPS)]
    ins += [("row", xc, SSM_N, D // SSM_N + g) for g in range(SSM_GROUPS)]
    ins += [("row", xc, SSM_N, D // SSM_N + SSM_GROUPS + g) for g in range(SSM_GROUPS)]
    ins += [("row", dtr, DT_PAD, 0)]
    return ins


SSD_CPS = 2


def ssd_fwd(name, T, xc, dtr, dtb, alog, dsk):
    gw = SSM_HPG * SSM_P

    def body(i, n, ins, outs, accs, scr):
        S = scr[0]

        @pl.when(i == 0)
        def _():
            S[...] = jnp.zeros(S.shape, f32)
        S4 = tuple(S[:, g * gw:(g + 1) * gw] for g in range(4))
        for c in range(SSD_CPS):
            rows = pl.ds(c * CHUNK, CHUNK)
            X4 = tuple(ins[g][rows, :] for g in range(4))
            B4 = tuple(ins[4 + g][rows, :] for g in range(4))
            C4 = tuple(ins[8 + g][rows, :] for g in range(4))
            prep = _ssd_prep(ins[12][rows, :], ins[13][...], ins[14][...])
            nxt = []
            for g in range(4):
                outs[1][rows, g * gw:(g + 1) * gw] = S4[g]
                y, s_new = _ssd_group(g, X4[g], B4[g], C4[g], S4[g], *prep, ins[15][...])
                outs[0][rows, g * gw:(g + 1) * gw] = y
                nxt.append(s_new)
            S4 = tuple(nxt)
        for g in range(4):
            S[:, g * gw:(g + 1) * gw] = S4[g]
    ins = _ssd_ins(xc, dtr) + [("const", dtb), ("const", alog), ("const", dsk)]
    return rowcall(name, body, T, SSD_CPS * CHUNK, ins, [(D, f32), (D, f32)], scratch=[pltpu.VMEM((SSM_N, D), f32)])


def ssd_bwd(name, T, xc, dtr, sprev, d_y, dtb, alog, dsk):
    gw = SSM_HPG * SSM_P

    def body(i, n, ins, outs, accs, scr):
        dS = scr[0]

        @pl.when(i == n - 1)
        def _():
            dS[...] = jnp.zeros(dS.shape, f32)
        dS4 = tuple(dS[:, g * gw:(g + 1) * gw] for g in range(4))
        def chunk(X4, dtr_c, B4, C4, S4, dtb_c, alog_c, dsk_c):
            prep = _ssd_prep(dtr_c, dtb_c, alog_c)
            res = [_ssd_group(g, X4[g], B4[g], C4[g], S4[g], *prep, dsk_c) for g in range(4)]
            return tuple(r[0] for r in res), tuple(r[1] for r in res)
        X4 = tuple(ins[g][...] for g in range(4))
        B4 = tuple(ins[4 + g][...] for g in range(4))
        C4 = tuple(ins[8 + g][...] for g in range(4))
        S4 = tuple(ins[13 + g][...] for g in range(4))
        dY4 = tuple(ins[17 + g][...] for g in range(4))
        _, vjp = jax.vjp(chunk, X4, ins[12][...], B4, C4, S4, ins[21][...], ins[22][...], ins[23][...])
        dX4, ddtr, dB4, dC4, dS4, ddtb, dalog, ddsk = vjp((dY4, dS4))
        for g in range(4):
            outs[0][:, g * gw:(g + 1) * gw] = dX4[g]
            outs[0][:, D + g * SSM_N:D + (g + 1) * SSM_N] = dB4[g]
            outs[0][:, D + (SSM_GROUPS + g) * SSM_N:D + (SSM_GROUPS + g + 1) * SSM_N] = dC4[g]
            dS[:, g * gw:(g + 1) * gw] = dS4[g]
        outs[1][...] = ddtr.astype(bf16)
        accs[0][...] += ddtb
        accs[1][...] += dalog
        accs[2][...] += ddsk
    ins = _ssd_ins(xc, dtr) + [("row", sprev, gw, g) for g in range(4)] + [("row", d_y, gw, g) for g in range(4)]
    ins += [("const", dtb), ("const", alog), ("const", dsk)]
    return rowcall(name, body, T, CHUNK, ins, [(CONV_DIM, f32), (DT_PAD, bf16)], accs=[(1, DT_PAD)] * 3,
                   scratch=[pltpu.VMEM((SSM_N, D), f32)], reverse=True)


def _gate_group(y, z, g):
    return _rms(y * _silu(z), g)


def gate_fwd(name, T, tm, y, uvz, gn):
    def body(i, n, ins, outs, accs, scr):
        for g in range(SSM_GROUPS):
            cols = slice(g * 256, (g + 1) * 256)
            outs[0][:, cols] = _gate_group(ins[0][:, cols], ins[1][:, cols], ins[2][:, cols]).astype(bf16)
    return rowcall(name, body, T, tm, [("row", y, D, 0), ("row", uvz, D, 2), ("const", gn)], [(D, bf16)], sub=64)[0]


def gate_bwd(name, T, tm, y, uvz, d_yb, gn):
    def body(i, n, ins, outs, accs, scr):
        for g in range(SSM_GROUPS):
            cols = slice(g * 256, (g + 1) * 256)
            _, vjp = jax.vjp(_gate_group, ins[0][:, cols], ins[1][:, cols], ins[3][:, cols])
            dy, dz, dg = vjp(ins[2][:, cols])
            outs[0][:, cols] = dy
            outs[1][:, cols] = dz.astype(bf16)
            accs[0][:, cols] += dg
    return rowcall(name, body, T, tm, [("row", y, D, 0), ("row", uvz, D, 2), ("row", d_yb, D, 0), ("const", gn)],
                   [(D, f32), (D, bf16)], accs=[(1, D)], sub=64)


def _pool_diff(i, tm, h_ref, halo_ref, g_ref, scr):
    g = g_ref[...]
    yn = _rms(h_ref[...], g)
    scr[pl.ds(0, POOL_HALO), :] = jnp.where(i > 0, _rms(halo_ref[...], g), 0.0)
    scr[pl.ds(POOL_HALO, tm), :] = yn
    pos = (i * tm + lax.broadcasted_iota(jnp.int32, (tm, 1), 0) + 1).astype(f32)
    parts = []
    for gi, win in enumerate(POOL_WINDOWS):
        cols = slice(gi * POOL_GD, (gi + 1) * POOL_GD)
        s = scr[pl.ds(POOL_HALO, tm), cols]
        for j in range(1, win):
            s = s + scr[pl.ds(POOL_HALO - j, tm), cols]
        parts.append(s / jnp.minimum(pos, float(win)) - yn[:, cols])
    return parts


def pool_fwd(name, T, tm, h2, g_pre, pw, pb, psc, g_post, g_next):
    def body(i, n, ins, outs, accs, scr):
        parts = _pool_diff(i, tm, ins[0], ins[1], ins[2], scr[0])
        for gi in range(len(POOL_WINDOWS)):
            cols = slice(gi * POOL_GD, (gi + 1) * POOL_GD)
            o = _bdot(parts[gi], ins[3][gi]) + ins[4][:, cols]
            outs[0][:, cols] = o * ins[5][:, cols]
        h = ins[0][...] + _rms(outs[0][...], ins[6][...])
        outs[1][...] = h
        outs[2][...] = _rms(h, ins[7][...]).astype(bf16)
    return rowcall(name, body, T, tm, [("row", h2, D, 0), ("prev", h2, D, 0, POOL_HALO), ("const", g_pre), ("const", pw), ("const", pb), ("const", psc),
                                       ("const", g_post), ("const", g_next)],
                   [(D, f32), (D, f32), (D, bf16)], scratch=[pltpu.VMEM((tm + POOL_HALO, D), f32)])


def pool_bwd(name, T, tm, h2, d_pm, d_res, g_pre, pw, pb, psc, f_prev, g_prev):
    def body(i, n, ins, outs, accs, scr):
        parts = _pool_diff(i, tm, ins[0], ins[1], ins[5], scr[0])
        dpm = ins[2][...]
        psc_v = ins[8][...]
        dps = dpm * psc_v
        dps_halo = jnp.where(i < n - 1, ins[3][...] * psc_v, 0.0)
        accs[1][...] += jnp.sum(dps, axis=0, keepdims=True)
        pos = (i * tm + lax.broadcasted_iota(jnp.int32, (tm, 1), 0) + 1).astype(f32)
        pos_h = ((i + 1) * tm + lax.broadcasted_iota(jnp.int32, (POOL_HALO, 1), 0) + 1).astype(f32)
        r_scr = scr[1]
        dyn_scr = scr[2]
        for gi, win in enumerate(POOL_WINDOWS):
            cols = slice(gi * POOL_GD, (gi + 1) * POOL_GD)
            w = ins[6][gi]
            o = _bdot(parts[gi], w) + ins[7][:, cols]
            accs[2][:, cols] += jnp.sum(dpm[:, cols] * o, axis=0, keepdims=True)
            accs[0][gi] += _bdot(parts[gi], dps[:, cols], TN)
            q = _bdot(dps[:, cols], w, NT)
            qh = _bdot(dps_halo[:, cols], w, NT)
            r_scr[pl.ds(0, tm), cols] = q / jnp.minimum(pos, float(win))
            r_scr[pl.ds(tm, POOL_HALO), cols] = qh / jnp.minimum(pos_h, float(win))
            s = r_scr[pl.ds(0, tm), cols]
            for j in range(1, win):
                s = s + r_scr[pl.ds(j, tm), cols]
            dyn_scr[:, cols] = s - q
        dx, dg = _rms_bwd(ins[0][...], ins[5][...], dyn_scr[...])
        dh = ins[4][...] + dx
        outs[0][...] = dh
        accs[3][...] += dg
        df, dgp = _rms_bwd(ins[9][...], ins[10][...], dh)
        outs[1][...] = df.astype(bf16)
        accs[4][...] += dgp
    ins = [("row", h2, D, 0), ("prev", h2, D, 0, POOL_HALO), ("row", d_pm, D, 0), ("next", d_pm, D, 0, POOL_HALO), ("row", d_res, D, 0),
           ("const", g_pre), ("const", pw), ("const", pb), ("const", psc), ("row", f_prev, D, 0), ("const", g_prev)]
    return rowcall(name, body, T, tm, ins, [(D, f32), (D, bf16)], accs=[(4, POOL_GD, POOL_GD), (1, D), (1, D), (1, D), (1, D)],
                   scratch=[pltpu.VMEM((tm + POOL_HALO, D), f32), pltpu.VMEM((tm + POOL_HALO, D), f32), pltpu.VMEM((tm, D), f32)])


def local_step(T, x, tgt, W, ffn_weights, early_grads):
    tm = 512 if T >= 1024 else T // 2
    TKW = 2048 if T >= 2048 else T
    ng = W["norm_g"]
    g = lambda l, j: ng[l, j][None, :]
    G = {}

    row_spec = pl.BlockSpec((tm, D), lambda j, i, k: (i, 0))
    tf = tm // 2 if T >= 1024 else tm
    rows_f = pl.BlockSpec((tf, D), lambda i: (i, 0))
    vec_f = pl.BlockSpec((1, D), lambda i: (0, 0))
    sh_f = [pl.BlockSpec((None, tf, FF_SH), lambda i, s=s: (s, i, 0)) for s in range(4)]
    sh_spec = [pl.BlockSpec((None, tm, FF_SH), lambda j, i, k, s=s: (s, i, 0)) for s in range(4)]
    out_f32, out_bf16 = (SDS((T, D), f32), rows_f), (SDS((T, D), bf16), rows_f)

    def resid_epilogue(with_pre):
        def ep(part, xs, os, accs):
            h = xs[0][...] + _rms(part, xs[1][...])
            os[0][...] = part
            os[1][...] = h
            if with_pre:
                os[2][...] = _rms(h, xs[2][...]).astype(bf16)
        return ep

    def bwd_epilogue(df_dtype):
        def ep(part, xs, os, accs):
            dx, dgp = _rms_bwd(xs[0][...], xs[3][...], part)
            dh = xs[2][...] + dx
            df, dgq = _rms_bwd(xs[1][...], xs[4][...], dh)
            os[0][...] = dh
            os[1][...] = df.astype(df_dtype)
            accs[0][...] += dgp
            accs[1][...] += dgq
        return ep

    def ffn_fwd(tag, n_bf, l, resid=None):
        gate4, up4, act4 = ffn_up(f"ffn{tag}_up", T, tm, n_bf, W["wg4"], W["wu4"], l)
        if resid is None:
            wd_spec = [pl.BlockSpec((None, FF_SH, D), lambda j, i, k, s=s: (s, l, 0)) for s in range(4)]
            f = mm(f"ffn{tag}_down", (1, T // tm, 1), [(act4, sh_spec[s], W["wd4"], wd_spec[s]) for s in range(4)], NN, row_spec, SDS((T, D), f32))
            return gate4, up4, act4, f, None
        wd_f = [pl.BlockSpec((None, FF_SH, D), lambda i, s=s: (s, l, 0)) for s in range(4)]
        f, h_out = mm_fused(f"ffn{tag}_down", T // tf, [(act4, sh_f[s], W["wd4"], wd_f[s]) for s in range(4)], NN,
                            [(resid[0], rows_f), (resid[1], vec_f)], [out_f32, out_f32], [], resid_epilogue(False))
        return gate4, up4, act4, f, h_out

    def ffn_bwd(tag, l, n_bf, gate4, up4, act4, d_f, h_out, f_pre, d_res, g_pre, g_post, df_dtype):
        d_gate4, d_up4 = ffn_dgu(f"ffn{tag}_dgu", T, tm, d_f, W["wd4"], gate4, up4, l)
        w_f = [pl.BlockSpec((None, D, FF_SH), lambda i, s=s: (s, l, 0)) for s in range(4)]
        d_h, d_fp, dgp, dgq = mm_fused(
            f"ffn{tag}_dn", T // tf, [(d_gate4, sh_f[s], W["wg4"], w_f[s]) for s in range(4)] + [(d_up4, sh_f[s], W["wu4"], w_f[s]) for s in range(4)],
            NT, [(h_out, rows_f), (f_pre, rows_f), (d_res, rows_f), (g_pre, vec_f), (g_post, vec_f)],
            [out_f32, (SDS((T, D), df_dtype), rows_f)], [(1, D), (1, D)], bwd_epilogue(df_dtype))

        def wgrad(nm, a4, b):
            return mm(nm, (4, 1, T // TKW),
                      [(a4, pl.BlockSpec((None, TKW, FF_SH), lambda s, j, k: (s, k, 0)), b, pl.BlockSpec((TKW, D), lambda s, j, k: (k, 0)))],
                      TN, pl.BlockSpec((None, FF_SH, D), lambda s, j, k: (s, 0, 0)), SDS((4, FF_SH, D), f32))
        return d_h, d_fp, dgp, dgq, wgrad(f"ffn{tag}_dwg", d_gate4, n_bf), wgrad(f"ffn{tag}_dwu", d_up4, n_bf), wgrad(f"ffn{tag}_dwd", act4, d_f)

    y0 = rms_to_bf16("l0_prenorm", T, tm, x, g(0, 0))
    uvz = matmul("in_uvz", [(y0, W["w_uvz"])], "nn", f32, tm, 1024)
    xbc = matmul("in_xbc", [(y0, W["w_xbc"])], "nn", f32, tm, 1024)
    dtr = matmul("in_dt", [(y0, W["w_dt"])], "nn", f32, tm, DT_PAD)
    y_a = gmlp_fwd("gmlp_fwd", T, tm, uvz, W["ln_g"], W["ln_b"], W["wm"], W["bs"])
    xc = conv_fwd("conv_fwd", T, tm, xbc, W["conv_w"], W["conv_b"])
    y_ssd, sprev = ssd_fwd("ssd_fwd", T, xc, dtr, W["dtb"], W["alog"], W["dsk"])
    y_b = gate_fwd("gate_fwd", T, tm, y_ssd, uvz, W["gn"])
    half = D // 2
    wo4 = W["wo4"]
    ycol = [pl.BlockSpec((tf, half), lambda i, cb=cb: (i, cb)) for cb in range(2)]
    wo_s = [pl.BlockSpec((None, half, D), lambda i, s=s: (s, 0, 0)) for s in range(4)]
    mixo, h1, n1 = mm_fused("out_proj", T // tf, [(y_a, ycol[0], wo4, wo_s[0]), (y_a, ycol[1], wo4, wo_s[1]),
                                                  (y_b, ycol[0], wo4, wo_s[2]), (y_b, ycol[1], wo4, wo_s[3])], NN,
                            [(x, rows_f), (g(0, 1), vec_f), (g(0, 2), vec_f)], [out_f32, out_f32, out_bf16], [], resid_epilogue(True))
    W = dict(W)
    W["wg4"], W["wu4"], W["wd4"] = ffn_weights(h1)
    gate0, up0, act0, f1, h2 = ffn_fwd("0", n1, 0, (h1, g(0, 3)))
    pm, h3, n3 = pool_fwd("pool_fwd", T, tm, h2, g(1, 0), W["pool_w"], W["pool_b"], W["pool_scale"], g(1, 1), g(1, 2))
    gate1, up1, act1, f2, _ = ffn_fwd("1", n3, 1)
    dh4, d_f2, loss_acc, dg13 = final_loss_bwd("loss_bwd", T, tm, h3, f2, tgt, g(1, 3))
    d_h3, d_pm, dg12, dg11, dwg1, dwu1, dwd1 = ffn_bwd("1", 1, n3, gate1, up1, act1, d_f2, h3, pm, dh4, g(1, 2), g(1, 1), f32)
    d_h2, d_f1, G["pool_w"], G["pool_b"], G["pool_scale"], dg10, dg03 = pool_bwd("pool_bwd", T, tm, h2, d_pm, d_h3, g(1, 0), W["pool_w"], W["pool_b"],
                                                                                 W["pool_scale"], f1, g(0, 3))
    d_h1, d_mixo, dg02, dg01, dwg0, dwu0, dwd0 = ffn_bwd("0", 0, n1, gate0, up0, act0, d_f1, h1, mixo, d_h2, g(0, 2), g(0, 1), bf16)
    def d_ycat(nm, s0):
        return mm(nm, (2, T // tm, 1), [(d_mixo, row_spec, wo4, pl.BlockSpec((None, half, D), lambda j, i, k: (s0 + j, 0, 0)))], NT,
                  pl.BlockSpec((tm, half), lambda j, i, k: (i, j)), SDS((T, D), f32))

    def d_wo(nm, y):
        return mm(nm, (2, 1, T // TKW), [(y, pl.BlockSpec((TKW, half), lambda s, j, k: (k, s)), d_mixo, pl.BlockSpec((TKW, D), lambda s, j, k: (k, 0)))],
                  TN, pl.BlockSpec((None, half, D), lambda s, j, k: (s, 0, 0)), SDS((2, half, D), f32))
    d_ya, d_yb = d_ycat("out_proj_dya", 0), d_ycat("out_proj_dyb", 2)
    dwo_a, dwo_b = d_wo("out_proj_dwa", y_a), d_wo("out_proj_dwb", y_b)
    G["wo4"] = [dwo_a[0], dwo_a[1], dwo_b[0], dwo_b[1]]
    G["wgT4"], G["wuT4"], G["wd4"] = [dwg0, dwg1], [dwu0, dwu1], [dwd0, dwd1]
    token = early_grads(G)
    d_yssd, d_z, G["gn"] = gate_bwd("gate_bwd", T, tm, y_ssd, uvz, d_yb, W["gn"] + token[0, 0])
    d_xc, d_dtr, G["dtb"], G["alog"], G["dsk"] = ssd_bwd("ssd_bwd", T, xc, dtr, sprev, d_yssd, W["dtb"], W["alog"], W["dsk"])
    d_pre, G["conv_w"], G["conv_b"] = conv_bwd_pre("conv_bwd_pre", T, tm, xbc, d_xc, W["conv_w"], W["conv_b"])
    d_xbc = conv_bwd_x("conv_bwd_x", T, tm, d_pre, W["conv_w"])
    d_u, d_v, G["ln_g"], G["ln_b"], G["wm"], G["bs"] = gmlp_bwd("gmlp_bwd", T, tm, uvz, d_ya, W["ln_g"], W["ln_b"], W["wm"], W["bs"])
    w_u, w_v, w_z = W["w_uvz"][:, :D], W["w_uvz"][:, D:2 * D], W["w_uvz"][:, 2 * D:]
    def pre_epilogue(part, xs, os, accs):
        dx, dg = _rms_bwd(xs[0][...], xs[2][...], part)
        os[0][...] = xs[1][...] + dx
        accs[0][...] += dg
    blk = lambda w: pl.BlockSpec((tf, w), lambda i: (i, 0))
    whole = lambda a: pl.BlockSpec(a.shape, lambda i: (0, 0))
    grad_x, dg00 = mm_fused("in_dy0", T // tf, [(d_u, blk(D), w_u, whole(w_u)), (d_v, blk(D), w_v, whole(w_v)), (d_z, blk(D), w_z, whole(w_z)),
                                                (d_xbc, blk(CONV_DIM), W["w_xbc"], whole(W["w_xbc"])), (d_dtr, blk(DT_PAD), W["w_dt"], whole(W["w_dt"]))],
                            NT, [(x, rows_f), (d_h1, rows_f), (g(0, 0), vec_f)], [out_f32], [(1, D)], pre_epilogue)
    G["w_inT"] = [matmul("in_dwu", [(d_u, y0)], "tn", f32, 1024, 1024, TKW), matmul("in_dwv", [(d_v, y0)], "tn", f32, 1024, 1024, TKW),
                  matmul("in_dwz", [(d_z, y0)], "tn", f32, 1024, 1024, TKW), matmul("in_dwxbc", [(d_xbc, y0)], "tn", f32, 1024, 1024, TKW),
                  matmul("in_dwdt", [(d_dtr, y0)], "tn", f32, DT_PAD, 1024, TKW)[:N_HEADS]]
    G["norm_g"] = jnp.stack([jnp.concatenate([dg00, dg01, dg02, dg03], 0), jnp.concatenate([dg10, dg11, dg12, dg13], 0)])
    return loss_acc, grad_x, G


def build_weights(Wf):
    causal = jnp.tril(jnp.ones((CHUNK, CHUNK), bool))
    w_in = Wf["w_in"].astype(bf16)
    pad16 = lambda v: jnp.pad(v.reshape(1, N_HEADS).astype(f32), ((0, 0), (0, DT_PAD - N_HEADS)))
    return {
        "norm_g": Wf["norm_g"],
        "w_uvz": w_in[:, :3 * D], "w_xbc": w_in[:, 3 * D:3 * D + CONV_DIM],
        "w_dt": jnp.pad(w_in[:, 3 * D + CONV_DIM:], ((0, 0), (0, DT_PAD - N_HEADS))),
        "ln_g": Wf["gm_ln_g"].reshape(1, D), "ln_b": Wf["gm_ln_b"].reshape(1, D),
        "wm": jnp.where(causal[None], Wf["gm_ws"], 0).astype(bf16), "bs": Wf["gm_bs"].reshape(GM_HEADS, CHUNK, 1),
        "conv_w": Wf["conv_w"], "conv_b": Wf["conv_b"].reshape(1, CONV_DIM),
        "dtb": pad16(Wf["dt_bias"]), "alog": pad16(Wf["a_log"]), "dsk": pad16(Wf["d_skip"]),
        "gn": Wf["ssm_norm_g"].reshape(1, D),
        "wo4": Wf["wo4"].astype(bf16),
        "pool_w": Wf["pool_w"].astype(bf16), "pool_b": Wf["pool_b"].reshape(1, D), "pool_scale": Wf["pool_scale"].reshape(1, D),
    }


def small_grads(G):
    return {
        "norm_g": G["norm_g"],
        "gm_ln_g": G["ln_g"].reshape(D), "gm_ln_b": G["ln_b"].reshape(D),
        "gm_ws": G["wm"], "gm_bs": G["bs"].reshape(GM_HEADS, CHUNK),
        "conv_w": G["conv_w"], "conv_b": G["conv_b"].reshape(CONV_DIM),
        "dt_bias": G["dtb"][0, :N_HEADS], "a_log": G["alog"][0, :N_HEADS], "d_skip": G["dsk"][0, :N_HEADS],
        "ssm_norm_g": G["gn"].reshape(D),
        "pool_b": G["pool_b"].reshape(4, POOL_GD), "pool_scale": G["pool_scale"].reshape(D),
    }


MESH_ID = pl.DeviceIdType.MESH
ANY = pl.BlockSpec(memory_space=pl.ANY)


DMA_CHUNK_BYTES = 2 << 20
DMA_MAX_CHUNKS = 32


def _pieces(view, axis, align):
    shape = view.shape
    nbytes = math.prod(shape) * jnp.dtype(view.dtype).itemsize
    n = max(1, min(DMA_MAX_CHUNKS, -(-nbytes // DMA_CHUNK_BYTES)))
    rows = shape[axis]
    size = -(-rows // n)
    size = -(-size // align) * align
    out = []
    for s in range(0, rows, size):
        idx = [slice(None)] * len(shape)
        idx[axis] = pl.ds(s, min(size, rows - s))
        out.append(tuple(idx))
    return out


def comm_call(name, operands, out_shapes, plan):
    n_in = len(operands)
    n_out = len(out_shapes)
    n_remote, n_local = plan((0, 0, 0), [None] * n_in, [None] * n_out, True)

    def body(*refs):
        in_refs, out_refs = refs[:n_in], refs[n_in:n_in + n_out]
        send_sems, recv_sems, local_sems = refs[n_in + n_out:]
        me = (lax.axis_index("x"), lax.axis_index("y"), lax.axis_index("c"))
        remote, local = plan(me, in_refs, out_refs, False)
        align = lambda v: 16 if v.dtype == bf16 else 8
        for j, (s, d, axis) in enumerate(local):
            for ix in _pieces(s, axis, align(s)):
                pltpu.make_async_copy(s.at[ix], d.at[ix], local_sems.at[j]).start()
        peers = [tuple((1 - m) if f else m for m, f in zip(me, flip)) for flip, *_ in remote]
        for k, (flip, src, dst, _, axis) in enumerate(remote):
            for ix in _pieces(src, axis, align(src)):
                pltpu.make_async_remote_copy(src_ref=src.at[ix], dst_ref=dst.at[ix], send_sem=send_sems.at[k], recv_sem=recv_sems.at[k],
                                             device_id=peers[k], device_id_type=MESH_ID).start()
        for k, (flip, src, dst, landing, axis) in enumerate(remote):
            pltpu.make_async_remote_copy(src_ref=landing, dst_ref=landing, send_sem=send_sems.at[k], recv_sem=recv_sems.at[k],
                                         device_id=peers[k], device_id_type=MESH_ID).wait_recv()
        for k, (flip, src, dst, landing, axis) in enumerate(remote):
            pltpu.make_async_remote_copy(src_ref=src, dst_ref=dst, send_sem=send_sems.at[k], recv_sem=recv_sems.at[k],
                                         device_id=peers[k], device_id_type=MESH_ID).wait_send()
        for j, (s, d, axis) in enumerate(local):
            pltpu.make_async_copy(s, d, local_sems.at[j]).wait()

    return pl.pallas_call(
        body, name=name, out_shape=list(out_shapes), in_specs=[ANY] * n_in, out_specs=[ANY] * n_out,
        scratch_shapes=[pltpu.SemaphoreType.DMA((n_remote,)), pltpu.SemaphoreType.DMA((n_remote,)), pltpu.SemaphoreType.DMA((max(n_local, 1),))],
    )(*operands)


CHIP_FLIPS = ((1, 0, 0), (0, 1, 0), (1, 1, 0))
PAIR_FLIP = (0, 0, 1)


def gather_over_chips(name, arrs):
    def plan(me, ins, outs, count):
        if count:
            return len(CHIP_FLIPS) * len(arrs), len(arrs)
        k = 2 * me[0] + me[1]
        remote, local = [], []
        for a in range(len(arrs)):
            for flip in CHIP_FLIPS:
                kp = 2 * ((1 - me[0]) if flip[0] else me[0]) + ((1 - me[1]) if flip[1] else me[1])
                remote.append((flip, ins[a], outs[a].at[k], outs[a].at[kp], 0))
            local.append((ins[a], outs[a].at[k], 0))
        return remote, local
    return comm_call(name, arrs, [SDS((4,) + a.shape, a.dtype) for a in arrs], plan)


def pair_split_exchange(name, p, rh):
    def plan(me, ins, outs, count):
        if count:
            return 1, 0
        theirs = ins[0].at[:, pl.ds(pl.multiple_of((1 - me[2]) * rh, 8), rh), :]
        return [(PAIR_FLIP, theirs, outs[0], outs[0], 1)], []
    return comm_call(name, [p], [SDS((4, rh, p.shape[2]), p.dtype)], plan)[0]


def scatter_over_chips(name, cs):
    def plan(me, ins, outs, count):
        if count:
            return len(CHIP_FLIPS), 0
        k = 2 * me[0] + me[1]
        remote = []
        for flip in CHIP_FLIPS:
            kp = 2 * ((1 - me[0]) if flip[0] else me[0]) + ((1 - me[1]) if flip[1] else me[1])
            remote.append((flip, ins[0].at[kp], outs[0].at[k], outs[0].at[kp], 0))
        return remote, []
    return comm_call(name, [cs], [SDS(cs.shape, cs.dtype)], plan)[0]


def pair_swap(name, half):
    def plan(me, ins, outs, count):
        if count:
            return 1, 0
        return [(PAIR_FLIP, ins[0], outs[0], outs[0], 0)], []
    return comm_call(name, [half], [SDS(half.shape, half.dtype)], plan)[0]


def _row_tile(rows, cap=512):
    if rows <= cap:
        return rows
    t = cap - cap % 8
    while rows % t:
        t -= 8
    return t


def pair_sum(name, packs, got, c_arr, tile):
    rh = got.shape[1]
    nb = rh // tile

    def kern(c_ref, a_ref, b_ref, o16_ref):
        o16_ref[...] = (a_ref[...] + b_ref[...]).astype(bf16)
    blk = (None, tile, D)
    grid_spec = pltpu.PrefetchScalarGridSpec(
        num_scalar_prefetch=1, grid=(4, nb),
        in_specs=[pl.BlockSpec(blk, lambda s, i, c: (s, c[0] * nb + i, 0)), pl.BlockSpec(blk, lambda s, i, c: (s, i, 0))],
        out_specs=pl.BlockSpec(blk, lambda s, i, c: (s, i, 0)))
    return pl.pallas_call(kern, name=name, grid_spec=grid_spec, out_shape=SDS(got.shape, bf16),
                          compiler_params=pltpu.CompilerParams(dimension_semantics=("parallel", "parallel")))(c_arr, packs, got)


def chip_sum(name, own16, landed16, k_arr, tile):
    rh = own16.shape[1]
    nb = rh // tile

    def kern(k_ref, own_ref, l0, l1, l2, l3, o_ref):
        k = k_ref[0]
        s = None
        for j, lref in enumerate((l0, l1, l2, l3)):
            t = jnp.where(k == j, own_ref[...], lref[...]).astype(f32)
            s = t if s is None else s + t
        o_ref[...] = s
    blk = (None, tile, D)
    land = [pl.BlockSpec(blk, lambda i, k, j=j: (jnp.where(k[0] == j, (j + 1) % N_CHIPS, j), i, 0)) for j in range(N_CHIPS)]
    grid_spec = pltpu.PrefetchScalarGridSpec(
        num_scalar_prefetch=1, grid=(nb,),
        in_specs=[pl.BlockSpec(blk, lambda i, k: (k[0], i, 0))] + land,
        out_specs=pl.BlockSpec((tile, D), lambda i, k: (i, 0)))
    return pl.pallas_call(kern, name=name, grid_spec=grid_spec, out_shape=SDS((rh, D), f32),
                          compiler_params=pltpu.CompilerParams(dimension_semantics=("parallel",)))(k_arr, own16, landed16, landed16, landed16, landed16)


def adamw(name, w, g, m, v):
    R, C = w.shape
    tr = _row_tile(R, 256)

    def kern(w_ref, g_ref, m_ref, v_ref, d_ref, mo_ref, vo_ref):
        gg = g_ref[...]
        mn = ADAM_B1 * m_ref[...] + (1.0 - ADAM_B1) * gg
        vn = ADAM_B2 * v_ref[...] + (1.0 - ADAM_B2) * jnp.square(gg)
        m_hat = mn / (1.0 - ADAM_B1 ** ADAM_STEP)
        v_hat = vn / (1.0 - ADAM_B2 ** ADAM_STEP)
        d_ref[...] = -ADAM_LR * (m_hat / (jnp.sqrt(v_hat) + ADAM_EPS) + ADAM_WD * w_ref[...])
        mo_ref[...] = mn
        vo_ref[...] = vn
    spec = pl.BlockSpec((tr, C), lambda i: (i, 0))
    s = SDS((R, C), f32)
    return pl.pallas_call(kern, name=name, grid=(R // tr,), in_specs=[spec] * 4, out_specs=[spec] * 3, out_shape=[s, s, s],
                          compiler_params=pltpu.CompilerParams(dimension_semantics=("parallel",)))(w, g, m, v)


WEIGHT_NAMES = ("norm_g", "w_in", "gm_ln_g", "gm_ln_b", "gm_ws", "gm_bs", "conv_w", "conv_b", "dt_bias", "a_log", "d_skip",
                "ssm_norm_g", "w_out", "pool_w", "pool_b", "pool_scale", "ffn_w_gate", "ffn_w_up", "ffn_w_down")
SMALL = ("norm_g", "conv_w", "pool_b", "pool_scale")
REPL = ("gm_ln_g", "gm_ln_b", "gm_ws", "gm_bs", "conv_b", "dt_bias", "a_log", "d_skip", "ssm_norm_g")
SMALL_AXIS = {"norm_g": 2, "conv_w": 1, "pool_b": 1, "pool_scale": 0}
N_CHIPS = 4
IN_SH = IN_DIM // N_CHIPS
SMALL_ROWS = 8
REPL_ROWS = 72
E_OUT, E_GATE, E_UP, E_DOWN = 0, 512, 512 + 2 * FF_SH, 512 + 4 * FF_SH
E_POOL = E_DOWN + 2 * FF_SH
E_ROWS, E_TILE = E_POOL + 64, 400
L_SMALL, L_REPL, L_IN = 0, SMALL_ROWS, SMALL_ROWS + REPL_ROWS
L_END = L_IN + IN_SH
L_ROWS, L_TILE = 1408, 352


def _flat_rows(pieces, rows):
    v = jnp.concatenate([p.reshape(-1) for p in pieces])
    return jnp.pad(v, (0, rows * D - v.shape[0])).reshape(rows, D)


def _shard_small(name, full, k):
    ax = SMALL_AXIS[name]
    n = full.shape[ax] // N_CHIPS
    return lax.slice_in_dim(full, k * n, (k + 1) * n, axis=ax)


def _drop1(name, a):
    return a if name == "norm_g" else a[0]


def _row_range(blocks, lo, hi):
    out, off = [], 0
    for b in blocks:
        n = b.shape[0]
        a, e = max(lo, off), min(hi, off + n)
        if a < e:
            out.append(b[a - off:e - off])
        off += n
    return out


HBM_SPEC = pl.BlockSpec(memory_space=pltpu.HBM)
SEM_SPEC = pl.BlockSpec(memory_space=pltpu.SEMAPHORE)
SPLIT_EFFECT = pltpu.SideEffectType.DATAFLOW_SIDE_EFFECTING


def _chip_of(me, flip):
    return 2 * ((1 - me[0]) if flip[0] else me[0]) + ((1 - me[1]) if flip[1] else me[1])


def gather_start(name, arrs, after, slotted=False):
    n = len(arrs)
    ncp = n * len(CHIP_FLIPS)

    def body(*refs):
        srcs, lands = refs[:n], refs[n:2 * n]
        send_sems, recv_sems, token = refs[2 * n + 1], refs[2 * n + 2], refs[-1]
        me = (lax.axis_index("x"), lax.axis_index("y"), lax.axis_index("c"))
        k = 2 * me[0] + me[1]
        for a in range(n):
            for f, flip in enumerate(CHIP_FLIPS):
                peer = tuple((1 - m) if fl else m for m, fl in zip(me, flip))
                src = srcs[a].at[_chip_of(me, flip)] if slotted else srcs[a]
                for ix in _pieces(src, 0, 16):
                    pltpu.make_async_remote_copy(src_ref=src.at[ix], dst_ref=lands[a].at[k].at[ix],
                                                 send_sem=send_sems.at[a * len(CHIP_FLIPS) + f], recv_sem=recv_sems.at[a * len(CHIP_FLIPS) + f],
                                                 device_id=peer, device_id_type=MESH_ID).start()
        token[...] = jnp.zeros_like(token)

    land_shapes = [a.shape if slotted else (N_CHIPS,) + a.shape for a in arrs]
    operands = [pltpu.with_memory_space_constraint(a, pltpu.HBM) for a in arrs]
    operands += [pltpu.with_memory_space_constraint(lax.empty(s, a.dtype), pltpu.HBM) for s, a in zip(land_shapes, arrs)]
    out = pl.pallas_call(
        body, name=name,
        out_shape=(pltpu.SemaphoreType.DMA((ncp,)), pltpu.SemaphoreType.DMA((ncp,)), *[pltpu.HBM(a.shape, a.dtype) for a in arrs],
                   *[pltpu.HBM(s, a.dtype) for s, a in zip(land_shapes, arrs)], SDS((8, 128), f32)),
        in_specs=[HBM_SPEC] * (2 * n) + [ANY], out_specs=(SEM_SPEC, SEM_SPEC, *[HBM_SPEC] * (2 * n), pl.BlockSpec(memory_space=pltpu.VMEM)),
        input_output_aliases={i: 2 + i for i in range(2 * n)},
        compiler_params=pltpu.CompilerParams(has_side_effects=SPLIT_EFFECT),
    )(*operands, after)
    return out[0], out[1], out[2:2 + n], out[2 + n:2 + 2 * n], out[-1]


def gather_wait(name, send_sems, recv_sems, thru, lands, after, slotted=False):
    n = len(thru)

    def body(*refs):
        srcs, lands_r = refs[:n], refs[n:2 * n]
        s_sems, r_sems = refs[2 * n], refs[2 * n + 1]
        me = (lax.axis_index("x"), lax.axis_index("y"), lax.axis_index("c"))
        k = 2 * me[0] + me[1]
        for a in range(n):
            for f, flip in enumerate(CHIP_FLIPS):
                peer = tuple((1 - m) if fl else m for m, fl in zip(me, flip))
                idx = a * len(CHIP_FLIPS) + f
                src = srcs[a].at[_chip_of(me, flip)] if slotted else srcs[a]
                pltpu.make_async_remote_copy(src_ref=src, dst_ref=lands_r[a].at[k], send_sem=s_sems.at[idx], recv_sem=r_sems.at[idx],
                                             device_id=peer, device_id_type=MESH_ID).wait_send()
                pltpu.make_async_remote_copy(src_ref=src, dst_ref=lands_r[a].at[_chip_of(me, flip)], send_sem=s_sems.at[idx],
                                             recv_sem=r_sems.at[idx], device_id=peer, device_id_type=MESH_ID).wait_recv()

    out = pl.pallas_call(
        body, name=name, out_shape=tuple(pltpu.HBM(t.shape, t.dtype) for t in (*thru, *lands)),
        in_specs=[HBM_SPEC] * (2 * n) + [SEM_SPEC, SEM_SPEC, ANY], out_specs=tuple([HBM_SPEC] * (2 * n)),
        input_output_aliases={i: i for i in range(2 * n)},
        compiler_params=pltpu.CompilerParams(has_side_effects=SPLIT_EFFECT),
    )(*thru, *lands, send_sems, recv_sems, after)
    return out[:n], out[n:]


def gather_weights(w_sh):
    big = [w_sh["w_in"][0], w_sh["w_out"][0], w_sh["pool_w"][0].reshape(4 * 64, POOL_GD)]
    small_pack = _flat_rows([w_sh[n] for n in SMALL], SMALL_ROWS)
    s_in, s_out, s_pool, s_small = gather_over_chips("gather_weights", [b.astype(bf16) for b in big] + [small_pack])
    Wf = {n: w_sh[n][0] for n in REPL}
    Wf["w_in"] = s_in.transpose(1, 0, 2).reshape(D, IN_DIM)
    Wf["pool_w"] = s_pool.reshape(N_CHIPS, 4, 64, POOL_GD).transpose(1, 0, 2, 3).reshape(4, POOL_GD, POOL_GD)
    Wf["wo4"] = s_out
    small_shapes = [_drop1(n, w_sh[n]).shape for n in SMALL]
    parts = [_split_rows(s_small[k], small_shapes) for k in range(N_CHIPS)]
    for j, n in enumerate(SMALL):
        Wf[n] = jnp.concatenate([parts[k][j] for k in range(N_CHIPS)], axis=SMALL_AXIS[n])
    return Wf


def pack_early(G):
    slots = [jnp.concatenate([G["wo4"][k], G["wgT4"][0][k], G["wgT4"][1][k], G["wuT4"][0][k], G["wuT4"][1][k], G["wd4"][0][k], G["wd4"][1][k],
                              G["pool_w"][:, k * 64:(k + 1) * 64, :].reshape(64, D)], axis=0) for k in range(N_CHIPS)]
    return jnp.stack(slots)


def pack_late(G):
    sg = small_grads(G)
    repl = _flat_rows([sg[n] for n in REPL], REPL_ROWS)
    w_in_t = jnp.concatenate(G["w_inT"], axis=0)
    slots = [jnp.concatenate([_flat_rows([_shard_small(n, sg[n], k) for n in SMALL], SMALL_ROWS), repl,
                              jnp.pad(w_in_t[k * IN_SH:(k + 1) * IN_SH], ((0, L_ROWS - L_END), (0, 0)))], axis=0)
             for k in range(N_CHIPS)]
    return jnp.stack(slots)


def unpack_grads(early, late, w_sh):
    g = {"w_out": early[E_OUT:E_GATE], "ffn_w_down": early[E_DOWN:E_POOL], "pool_w": early[E_POOL:E_ROWS],
         "ffn_w_gate": jnp.stack([early[E_GATE + l * FF_SH:E_GATE + (l + 1) * FF_SH].T for l in range(2)]),
         "ffn_w_up": jnp.stack([early[E_UP + l * FF_SH:E_UP + (l + 1) * FF_SH].T for l in range(2)]),
         "w_in": late[L_IN:L_END].T}
    small = _split_rows(late[L_SMALL:L_REPL], [_drop1(n, w_sh[n]).shape for n in SMALL])
    repl = _split_rows(late[L_REPL:L_IN], [w_sh[n][0].shape for n in REPL])
    g.update(zip(SMALL, small))
    g.update(zip(REPL, repl))
    return {n: g[n].reshape(w_sh[n].shape) for n in WEIGHT_NAMES}


def _split_rows(flat2d, shapes):
    v = flat2d.reshape(-1)
    out, off = [], 0
    for s in shapes:
        n = math.prod(s)
        out.append(v[off:off + n].reshape(s))
        off += n
    return out


def kernel(x, norm_g, w_in, gm_ln_g, gm_ln_b, gm_ws, gm_bs, conv_w, conv_b, dt_bias, a_log, d_skip, ssm_norm_g, w_out, pool_w, pool_b, pool_scale, ffn_w_gate, ffn_w_up, ffn_w_down, loss_target, m_norm_g, m_w_in, m_gm_ln_g, m_gm_ln_b, m_gm_ws, m_gm_bs, m_conv_w, m_conv_b, m_dt_bias, m_a_log, m_d_skip, m_ssm_norm_g, m_w_out, m_pool_w, m_pool_b, m_pool_scale, m_ffn_w_gate, m_ffn_w_up, m_ffn_w_down, v_norm_g, v_w_in, v_gm_ln_g, v_gm_ln_b, v_gm_ws, v_gm_bs, v_conv_w, v_conv_b, v_dt_bias, v_a_log, v_d_skip, v_ssm_norm_g, v_w_out, v_pool_w, v_pool_b, v_pool_scale, v_ffn_w_gate, v_ffn_w_up, v_ffn_w_down):
    T = x.shape[1]
    w_sh = dict(zip(WEIGHT_NAMES, (norm_g, w_in, gm_ln_g, gm_ln_b, gm_ws, gm_bs, conv_w, conv_b, dt_bias, a_log, d_skip, ssm_norm_g, w_out,
                                   pool_w, pool_b, pool_scale, ffn_w_gate, ffn_w_up, ffn_w_down)))
    m_sh = dict(zip(WEIGHT_NAMES, (m_norm_g, m_w_in, m_gm_ln_g, m_gm_ln_b, m_gm_ws, m_gm_bs, m_conv_w, m_conv_b, m_dt_bias, m_a_log, m_d_skip,
                                   m_ssm_norm_g, m_w_out, m_pool_w, m_pool_b, m_pool_scale, m_ffn_w_gate, m_ffn_w_up, m_ffn_w_down)))
    v_sh = dict(zip(WEIGHT_NAMES, (v_norm_g, v_w_in, v_gm_ln_g, v_gm_ln_b, v_gm_ws, v_gm_bs, v_conv_w, v_conv_b, v_dt_bias, v_a_log, v_d_skip,
                                   v_ssm_norm_g, v_w_out, v_pool_w, v_pool_b, v_pool_scale, v_ffn_w_gate, v_ffn_w_up, v_ffn_w_down)))

    my_k = 2 * lax.axis_index("x") + lax.axis_index("y")
    ffn_own = [w_sh["ffn_w_gate"].reshape(2 * D, FF_SH).astype(bf16), w_sh["ffn_w_up"].reshape(2 * D, FF_SH).astype(bf16),
               w_sh["ffn_w_down"].reshape(2 * FF_SH, D).astype(bf16)]
    Wf = gather_weights(w_sh)
    send_sems, recv_sems, thru, lands, token = gather_start("gather_ffn_start", ffn_own, Wf["wo4"])
    Wf["norm_g"] = Wf["norm_g"] + token[0, 0]
    W = build_weights(Wf)

    def ffn_weights(after):
        _, landed = gather_wait("gather_ffn_wait", send_sems, recv_sems, thru, lands, after)
        return tuple(lax.dynamic_update_slice(l, o[None], (my_k, 0, 0)) for l, o in zip(landed, ffn_own))

    my_c = lax.axis_index("c")
    c_arr = my_c.astype(jnp.int32).reshape(1)
    k_arr = my_k.astype(jnp.int32).reshape(1)

    def pair_stage(tag, packs, tile):
        got = pair_split_exchange(f"grads{tag}_pair_split", packs, packs.shape[1] // 2)
        return pair_sum(f"grads{tag}_pair_sum", packs, got, c_arr, tile)

    def chip_stage(tag, pair16, landed, tile):
        half = chip_sum(f"grads{tag}_chip_sum", pair16, landed, k_arr, tile)
        other = pair_swap(f"grads{tag}_pair_swap", half)
        return jnp.concatenate([jnp.where(my_c == 0, half, other), jnp.where(my_c == 0, other, half)], axis=0)

    early = {}

    def early_grads(Ge):
        pair16 = pair_stage("E", pack_early(Ge), E_TILE)
        s_sems, r_sems, thru, lands, tok = gather_start("gradsE_scatter_start", [pair16], jnp.zeros((8, 128), f32), slotted=True)
        early.update(s_sems=s_sems, r_sems=r_sems, thru=thru, lands=lands)
        return tok

    loss_acc, grad_x, G = local_step(T, x[0], loss_target[0], W, ffn_weights, early_grads)
    pair_l = pair_stage("L", pack_late(G), L_TILE)
    total_l = chip_stage("L", pair_l, scatter_over_chips("gradsL_scatter", pair_l), L_TILE)
    (pair_e,), (landed_e,) = gather_wait("gradsE_scatter_wait", early["s_sems"], early["r_sems"], early["thru"], early["lands"], total_l,
                                         slotted=True)
    total_e = chip_stage("E", pair_e, landed_e, E_TILE)
    grads = unpack_grads(total_e, total_l, w_sh)

    delta, new_m, new_v = {}, {}, {}
    for n in WEIGHT_NAMES:
        shp = w_sh[n].shape
        two_d = (-1, shp[-1])
        d_, m_, v_ = adamw("adamw_" + n, w_sh[n].reshape(two_d), grads[n].reshape(two_d), m_sh[n].reshape(two_d), v_sh[n].reshape(two_d))
        delta[n], new_m[n], new_v[n] = d_.reshape(shp), m_.reshape(shp), v_.reshape(shp)

    loss = lax.psum(loss_acc[0, 0], ("x", "y", "c"))
    return (loss, grad_x[None], *[grads[n] for n in WEIGHT_NAMES], *[delta[n] for n in WEIGHT_NAMES],
            *[new_m[n] for n in WEIGHT_NAMES], *[new_v[n] for n in WEIGHT_NAMES])
```

```python
import functools
import math

import jax
import jax.numpy as jnp
from jax import lax
from jax.experimental import pallas as pl
from jax.experimental.pallas import tpu as pltpu

f32, bf16 = jnp.float32, jnp.bfloat16
SDS = jax.ShapeDtypeStruct

D = 1024
EPS = 1e-6
CHUNK = 128
GM_HEADS, GM_HD = 4, 256
SSM_GROUPS, SSM_HPG, SSM_P, SSM_N = 4, 4, 64, 128
N_HEADS = SSM_GROUPS * SSM_HPG
CONV_K = 4
CONV_DIM = 2048
POOL_WINDOWS = (2, 4, 8, 16)
POOL_GD = 256
POOL_HALO = 16
CONV_HALO = 8
D_FF = 2816
DT_PAD = 128
IN_DIM = 5136

ADAM_LR, ADAM_B1, ADAM_B2, ADAM_EPS, ADAM_WD, ADAM_STEP = 0.001, 0.9, 0.999, 1e-08, 0.01, 10

NT = (((1,), (1,)), ((), ()))
TN = (((0,), (0,)), ((), ()))
NN = (((1,), (0,)), ((), ()))
HI = lax.Precision.HIGHEST
MM_SUB = 256


def _silu(x):
    return x * jax.nn.sigmoid(x)


def _softplus(x):
    return jnp.maximum(x, 0.0) + jnp.log1p(jnp.exp(-jnp.abs(x)))


def _rms(x, g):
    return x * lax.rsqrt(jnp.mean(x * x, axis=-1, keepdims=True) + EPS) * g


def _rms_bwd(x, g, dy):
    r = lax.rsqrt(jnp.mean(x * x, axis=-1, keepdims=True) + EPS)
    xh = x * r
    dxh = dy * g
    dx = r * (dxh - xh * jnp.mean(dxh * xh, axis=-1, keepdims=True))
    return dx, jnp.sum(dy * xh, axis=0, keepdims=True)


def _bdot(a, b, dims=NN):
    return lax.dot_general(a.astype(bf16), b.astype(bf16), dims, preferred_element_type=f32)


def matmul(name, pairs, mode, out_dtype, tm, tn, tk=None):
    a0, b0 = pairs[0]
    if mode == "tn":
        M, N, K = a0.shape[1], b0.shape[1], a0.shape[0]
    else:
        M, K = a0.shape
        N = b0.shape[1] if mode == "nn" else b0.shape[0]
    tm, tn = min(tm, M), min(tn, N)
    assert M % tm == 0 and N % tn == 0, (name, M, N, tm, tn)
    if tk is None:
        nk = 1
    else:
        assert len(pairs) == 1 and K % tk == 0
        nk = K // tk
    dims = {"nn": NN, "nt": NT, "tn": TN}[mode]
    in_specs, args = [], []
    for a, b in pairs:
        kk = (a.shape[0] if mode == "tn" else a.shape[1]) if tk is None else tk
        if mode == "tn":
            in_specs.append(pl.BlockSpec((kk, tm), lambda j, i, k: (k, i)))
            in_specs.append(pl.BlockSpec((kk, tn), lambda j, i, k: (k, j)))
        elif mode == "nn":
            in_specs.append(pl.BlockSpec((tm, kk), lambda j, i, k: (i, k)))
            in_specs.append(pl.BlockSpec((kk, tn), lambda j, i, k: (k, j)))
        else:
            in_specs.append(pl.BlockSpec((tm, kk), lambda j, i, k: (i, k)))
            in_specs.append(pl.BlockSpec((tn, kk), lambda j, i, k: (j, k)))
        args += [a, b]
    npairs = len(pairs)

    def kern(*refs):
        o = refs[2 * npairs]
        part = None
        for p in range(npairs):
            d = _bdot(refs[2 * p][...], refs[2 * p + 1][...], dims)
            part = d if part is None else part + d
        if nk == 1:
            o[...] = part.astype(out_dtype)
        else:
            acc = refs[2 * npairs + 1]
            k = pl.program_id(2)

            @pl.when(k == 0)
            def _():
                acc[...] = part

            @pl.when(k > 0)
            def _():
                acc[...] += part

            @pl.when(k == nk - 1)
            def _():
                o[...] = acc[...].astype(out_dtype)

    return pl.pallas_call(
        kern, name=name, grid=(N // tn, M // tm, nk),
        in_specs=in_specs, out_specs=pl.BlockSpec((tm, tn), lambda j, i, k: (i, j)),
        out_shape=SDS((M, N), out_dtype),
        scratch_shapes=[pltpu.VMEM((tm, tn), f32)] if nk > 1 else [],
        compiler_params=pltpu.CompilerParams(dimension_semantics=("parallel", "parallel", "arbitrary")),
    )(*args)


def mm(name, grid, pairs, dims, o_spec, out_shape):
    nk = grid[2]
    npairs = len(pairs)
    in_specs, args = [], []
    for a, a_spec, b, b_spec in pairs:
        in_specs += [a_spec, b_spec]
        args += [a, b]
    blk = tuple(d for d in o_spec.block_shape if d is not None)

    def kern(*refs):
        o = refs[2 * npairs]
        part = None
        for p in range(npairs):
            d = _bdot(refs[2 * p][...], refs[2 * p + 1][...], dims)
            part = d if part is None else part + d
        if nk == 1:
            o[...] = part.astype(o.dtype)
        else:
            acc = refs[2 * npairs + 1]
            k = pl.program_id(2)

            @pl.when(k == 0)
            def _():
                acc[...] = part

            @pl.when(k > 0)
            def _():
                acc[...] += part

            @pl.when(k == nk - 1)
            def _():
                o[...] = acc[...].astype(o.dtype)

    return pl.pallas_call(
        kern, name=name, grid=grid, in_specs=in_specs, out_specs=o_spec, out_shape=out_shape,
        scratch_shapes=[pltpu.VMEM(blk, f32)] if nk > 1 else [],
        compiler_params=pltpu.CompilerParams(dimension_semantics=("parallel", "parallel", "arbitrary")),
    )(*args)


def mm_fused(name, n_row_blocks, pairs, dims, extra_ins, outs, accs, epilogue):
    npairs, nx, no, na = len(pairs), len(extra_ins), len(outs), len(accs)
    in_specs, args = [], []
    for a, a_spec, b, b_spec in pairs:
        in_specs += [a_spec, b_spec]
        args += [a, b]
    for arr, spec in extra_ins:
        in_specs.append(spec)
        args.append(arr)

    rows_blk = outs[0][1].block_shape[0]
    sub = min(rows_blk, MM_SUB)

    def kern(*refs):
        x_refs = refs[2 * npairs:2 * npairs + nx]
        o_refs = refs[2 * npairs + nx:2 * npairs + nx + no]
        a_refs = refs[2 * npairs + nx + no:]
        if na:
            @pl.when(pl.program_id(0) == 0)
            def _():
                for a in a_refs:
                    a[...] = jnp.zeros(a.shape, f32)
        for r0 in range(0, rows_blk, sub):
            rows = pl.ds(r0, sub)
            part = None
            for p in range(npairs):
                d = _bdot(refs[2 * p][rows, :], refs[2 * p + 1][...], dims)
                part = d if part is None else part + d
            epilogue(part, [x.at[rows, :] if x.shape[0] == rows_blk else x for x in x_refs], [o.at[rows, :] for o in o_refs], a_refs)

    return pl.pallas_call(
        kern, name=name, grid=(n_row_blocks,), in_specs=in_specs,
        out_specs=[spec for _, spec in outs] + [pl.BlockSpec(tuple(s), lambda i, nd=len(s): (0,) * nd) for s in accs],
        out_shape=[s for s, _ in outs] + [SDS(tuple(s), f32) for s in accs],
        compiler_params=pltpu.CompilerParams(dimension_semantics=("arbitrary",)),
    )(*args)


FF_SH = D_FF // 4


def ffn_up(name, T, tm, n_bf, wg4, wu4, l):
    sub = min(tm, MM_SUB)

    def kern(n_ref, wg_ref, wu_ref, g_ref, u_ref, a_ref):
        for r0 in range(0, tm, sub):
            rows = pl.ds(r0, sub)
            n = n_ref[rows, :]
            g = jnp.dot(n, wg_ref[...], preferred_element_type=f32)
            u = jnp.dot(n, wu_ref[...], preferred_element_type=f32)
            g_ref[rows, :] = g.astype(bf16)
            u_ref[rows, :] = u.astype(bf16)
            a_ref[rows, :] = (_silu(g) * u).astype(bf16)
    w_spec = pl.BlockSpec((None, D, FF_SH), lambda k, i: (k, l, 0))
    o_spec = pl.BlockSpec((None, tm, FF_SH), lambda k, i: (k, i, 0))
    s = SDS((4, T, FF_SH), bf16)
    return pl.pallas_call(kern, name=name, grid=(4, T // tm), in_specs=[pl.BlockSpec((tm, D), lambda k, i: (i, 0)), w_spec, w_spec],
                          out_specs=[o_spec] * 3, out_shape=[s, s, s],
                          compiler_params=pltpu.CompilerParams(dimension_semantics=("parallel", "parallel")))(n_bf, wg4, wu4)


def ffn_dgu(name, T, tm, d_f, wd4, gate4, up4, l):
    rc = 16

    sub = min(tm, MM_SUB)

    def kern(df_ref, wd_ref, g_ref, u_ref, dg_ref, du_ref, dact_ref):
        for s0 in range(0, tm, sub):
            dact_ref[pl.ds(s0, sub), :] = _bdot(df_ref[pl.ds(s0, sub), :], wd_ref[...], NT)
            for r0 in range(s0, s0 + sub, rc):
                rows = pl.ds(r0, rc)
                _, vjp = jax.vjp(lambda a, b: _silu(a) * b, g_ref[rows, :].astype(f32), u_ref[rows, :].astype(f32))
                dg, du = vjp(dact_ref[rows, :])
                dg_ref[rows, :] = dg.astype(bf16)
                du_ref[rows, :] = du.astype(bf16)
    a_spec = pl.BlockSpec((None, tm, FF_SH), lambda k, i: (k, i, 0))
    s = SDS((4, T, FF_SH), bf16)
    return pl.pallas_call(kern, name=name, grid=(4, T // tm),
                          in_specs=[pl.BlockSpec((tm, D), lambda k, i: (i, 0)), pl.BlockSpec((None, FF_SH, D), lambda k, i: (k, l, 0)), a_spec, a_spec],
                          out_specs=[a_spec] * 2, out_shape=[s, s], scratch_shapes=[pltpu.VMEM((tm, FF_SH), f32)],
                          compiler_params=pltpu.CompilerParams(dimension_semantics=("parallel", "parallel")))(d_f, wd4, gate4, up4)


def rowcall(name, body, T, tm, ins, outs, accs=(), scratch=(), reverse=False, sub=None):
    n = T // tm
    assert T % tm == 0

    def blk(i):
        return (n - 1 - i) if reverse else i

    in_specs, args = [], []
    for spec in ins:
        kind, arr = spec[0], spec[1]
        if kind == "row":
            _, _, w, cb = spec
            in_specs.append(pl.BlockSpec((tm, w), lambda i, cb=cb: (blk(i), cb)))
        elif kind == "prev":
            _, _, w, cb, h = spec
            r = tm // h
            in_specs.append(pl.BlockSpec((h, w), lambda i, cb=cb, r=r: (jnp.maximum(blk(i) * r - 1, 0), cb)))
        elif kind == "next":
            _, _, w, cb, h = spec
            r = tm // h
            in_specs.append(pl.BlockSpec((h, w), lambda i, cb=cb, r=r, h=h: (jnp.minimum((blk(i) + 1) * r, T // h - 1), cb)))
        else:
            nd = arr.ndim
            in_specs.append(pl.BlockSpec(arr.shape, lambda i, nd=nd: (0,) * nd))
        args.append(arr)
    out_shape = [SDS((T, w), dt) for w, dt in outs] + [SDS(tuple(s), f32) for s in accs]
    out_specs = [pl.BlockSpec((tm, w), lambda i: (blk(i), 0)) for w, _ in outs]
    out_specs += [pl.BlockSpec(tuple(s), lambda i, nd=len(s): (0,) * nd) for s in accs]
    ni, no, na = len(ins), len(outs), len(accs)

    def kern(*refs):
        i = pl.program_id(0)
        in_refs, out_refs = refs[:ni], refs[ni:ni + no]
        acc_refs, scr = refs[ni + no:ni + no + na], refs[ni + no + na:]
        if na:
            @pl.when(i == 0)
            def _():
                for a in acc_refs:
                    a[...] = jnp.zeros(a.shape, f32)
        if sub is None or sub >= tm:
            body(blk(i), n, in_refs, out_refs, acc_refs, scr)
        else:
            for r0 in range(0, tm, sub):
                rows = pl.ds(r0, sub)
                body(blk(i), n, [r.at[rows, :] if spec[0] == "row" else r for r, spec in zip(in_refs, ins)],
                     [o.at[rows, :] for o in out_refs], acc_refs, [s.at[rows, :] for s in scr])

    res = pl.pallas_call(
        kern, name=name, grid=(n,), in_specs=in_specs, out_specs=out_specs, out_shape=out_shape,
        scratch_shapes=list(scratch),
        compiler_params=pltpu.CompilerParams(dimension_semantics=("arbitrary",)),
    )(*args)
    return res


def rms_to_bf16(name, T, tm, x, g):
    def body(i, n, ins, outs, accs, scr):
        outs[0][...] = _rms(ins[0][...], ins[1][...]).astype(bf16)
    return rowcall(name, body, T, tm, [("row", x, D, 0), ("const", g)], [(D, bf16)], sub=64)[0]


def resid_norm(name, T, tm, h_in, f, g_post, g_pre):
    def body(i, n, ins, outs, accs, scr):
        h = ins[0][...] + _rms(ins[1][...], ins[2][...])
        outs[0][...] = h
        if g_pre is not None:
            outs[1][...] = _rms(h, ins[3][...]).astype(bf16)
    ins = [("row", h_in, D, 0), ("row", f, D, 0), ("const", g_post)] + ([("const", g_pre)] if g_pre is not None else [])
    outs = [(D, f32)] + ([(D, bf16)] if g_pre is not None else [])
    return rowcall(name, body, T, tm, ins, outs)


def swiglu_act(name, T, tm, gate, up):
    def body(i, n, ins, outs, accs, scr):
        outs[0][...] = (_silu(ins[0][...]) * ins[1][...]).astype(bf16)
    return rowcall(name, body, T, tm, [("row", gate, D_FF, 0), ("row", up, D_FF, 0)], [(D_FF, bf16)])[0]


def swiglu_bwd(name, T, tm, gate, up, d_act):
    def body(i, n, ins, outs, accs, scr):
        _, vjp = jax.vjp(lambda a, b: _silu(a) * b, ins[0][...], ins[1][...])
        dg, du = vjp(ins[2][...])
        outs[0][...] = dg.astype(bf16)
        outs[1][...] = du.astype(bf16)
    return rowcall(name, body, T, tm, [("row", gate, D_FF, 0), ("row", up, D_FF, 0), ("row", d_act, D_FF, 0)],
                   [(D_FF, bf16), (D_FF, bf16)])


def final_loss_bwd(name, T, tm, h3, f2, tgt, g_post):
    def body(i, n, ins, outs, accs, scr):
        f, g = ins[1][...], ins[3][...]
        e = ins[0][...] + _rms(f, g) - ins[2][...]
        accs[0][...] += jnp.sum(jnp.sum(e * e, axis=-1, keepdims=True) * (0.5 / D), axis=0, keepdims=True)
        dh = e * (1.0 / D)
        df, dg = _rms_bwd(f, g, dh)
        outs[0][...] = dh
        outs[1][...] = df.astype(bf16)
        accs[1][...] += dg
    return rowcall(name, body, T, tm, [("row", h3, D, 0), ("row", f2, D, 0), ("row", tgt, D, 0), ("const", g_post)],
                   [(D, f32), (D, bf16)], accs=[(1, 1), (1, D)], sub=32)


def bwd_pre_post(name, T, tm, h_out, f, d_res, d_n, g_pre, g_post, df_dtype):
    def body(i, n, ins, outs, accs, scr):
        dx, dgp = _rms_bwd(ins[0][...], ins[4][...], ins[3][...])
        dh = ins[2][...] + dx
        df, dgq = _rms_bwd(ins[1][...], ins[5][...], dh)
        outs[0][...] = dh
        outs[1][...] = df.astype(df_dtype)
        accs[0][...] += dgp
        accs[1][...] += dgq
    return rowcall(name, body, T, tm,
                   [("row", h_out, D, 0), ("row", f, D, 0), ("row", d_res, D, 0), ("row", d_n, D, 0), ("const", g_pre), ("const", g_post)],
                   [(D, f32), (D, df_dtype)], accs=[(1, D), (1, D)])


def bwd_post(name, T, tm, f, d_h, g_post):
    def body(i, n, ins, outs, accs, scr):
        df, dg = _rms_bwd(ins[0][...], ins[2][...], ins[1][...])
        outs[0][...] = df.astype(bf16)
        accs[0][...] += dg
    return rowcall(name, body, T, tm, [("row", f, D, 0), ("row", d_h, D, 0), ("const", g_post)], [(D, bf16)], accs=[(1, D)])


def bwd_pre(name, T, tm, h, d_res, d_n, g_pre):
    def body(i, n, ins, outs, accs, scr):
        dx, dg = _rms_bwd(ins[0][...], ins[3][...], ins[2][...])
        outs[0][...] = ins[1][...] + dx
        accs[0][...] += dg
    return rowcall(name, body, T, tm, [("row", h, D, 0), ("row", d_res, D, 0), ("row", d_n, D, 0), ("const", g_pre)],
                   [(D, f32)], accs=[(1, D)])


def _layer_norm_parts(x):
    mu = jnp.mean(x, axis=-1, keepdims=True)
    xc = x - mu
    r = lax.rsqrt(jnp.mean(xc * xc, axis=-1, keepdims=True) + EPS)
    return xc * r, r


def gmlp_fwd(name, T, tm, uvz, ln_g, ln_b, wm, bs):
    def body(i, n, ins, outs, accs, scr):
        gu = jax.nn.gelu(ins[0][...])
        xh, _ = _layer_norm_parts(jax.nn.gelu(ins[1][...]))
        vln = (xh * ins[2][...] + ins[3][...]).astype(bf16)
        for c in range(ins[0].shape[0] // CHUNK):
            rows = slice(c * CHUNK, (c + 1) * CHUNK)
            for h in range(GM_HEADS):
                cols = slice(h * GM_HD, (h + 1) * GM_HD)
                mixed = jnp.dot(ins[4][h], vln[rows, cols], preferred_element_type=f32) + ins[5][h]
                outs[0][rows, cols] = (gu[rows, cols] * mixed).astype(bf16)
    return rowcall(name, body, T, tm, [("row", uvz, D, 0), ("row", uvz, D, 1), ("const", ln_g), ("const", ln_b), ("const", wm), ("const", bs)],
                   [(D, bf16)], sub=CHUNK)[0]


def gmlp_bwd(name, T, tm, uvz, d_ya, ln_g, ln_b, wm, bs):
    def body(i, n, ins, outs, accs, scr):
        u, v, dya = ins[0][...], ins[1][...], ins[2][...]
        gu, gelu_u_vjp = jax.vjp(jax.nn.gelu, u)
        gv, gelu_v_vjp = jax.vjp(jax.nn.gelu, v)
        xh, r = _layer_norm_parts(gv)
        lng = ins[3][...]
        vln = (xh * lng + ins[4][...]).astype(bf16)
        rr = lax.broadcasted_iota(jnp.int32, (CHUNK, CHUNK), 0)
        cc = lax.broadcasted_iota(jnp.int32, (CHUNK, CHUNK), 1)
        causal = (rr >= cc).astype(f32)
        dvln_ref = scr[0]
        dgu_ref = scr[1]
        for c in range(ins[0].shape[0] // CHUNK):
            rows = slice(c * CHUNK, (c + 1) * CHUNK)
            for h in range(GM_HEADS):
                cols = slice(h * GM_HD, (h + 1) * GM_HD)
                w = ins[5][h]
                blk = vln[rows, cols]
                mixed = jnp.dot(w, blk, preferred_element_type=f32) + ins[6][h]
                dy = dya[rows, cols]
                dgu_ref[rows, cols] = dy * mixed
                dm = dy * gu[rows, cols]
                accs[3][h] += jnp.sum(dm, axis=1, keepdims=True)
                accs[2][h] += _bdot(dm, blk, NT) * causal
                dvln_ref[rows, cols] = _bdot(w, dm, TN)
        dvln = dvln_ref[...]
        accs[0][...] += jnp.sum(dvln * xh, axis=0, keepdims=True)
        accs[1][...] += jnp.sum(dvln, axis=0, keepdims=True)
        dxh = dvln * lng
        dgv = r * (dxh - jnp.mean(dxh, axis=-1, keepdims=True) - xh * jnp.mean(dxh * xh, axis=-1, keepdims=True))
        outs[0][...] = gelu_u_vjp(dgu_ref[...])[0].astype(bf16)
        outs[1][...] = gelu_v_vjp(dgv)[0].astype(bf16)
    return rowcall(name, body, T, tm,
                   [("row", uvz, D, 0), ("row", uvz, D, 1), ("row", d_ya, D, 0), ("const", ln_g), ("const", ln_b), ("const", wm), ("const", bs)],
                   [(D, bf16), (D, bf16)], accs=[(1, D), (1, D), (GM_HEADS, CHUNK, CHUNK), (GM_HEADS, CHUNK, 1)],
                   scratch=[pltpu.VMEM((tm, D), f32), pltpu.VMEM((tm, D), f32)], sub=CHUNK)


CONV_RC, CONV_LB = 32, 512


def _conv_fill(i, x_ref, halo_ref, scr, tm):
    scr[pl.ds(0, CONV_HALO), :] = jnp.where(i > 0, halo_ref[...], 0.0)
    scr[pl.ds(CONV_HALO, tm), :] = x_ref[...]


def _conv_taps(scr, r0, lanes):
    return [scr[pl.ds(r0 + CONV_HALO - (CONV_K - 1) + k, CONV_RC), lanes] for k in range(CONV_K)]


def conv_fwd(name, T, tm, xbc, conv_w, conv_b):
    def body(i, n, ins, outs, accs, scr):
        s = scr[0]
        _conv_fill(i, ins[0], ins[1], s, tm)
        for lb in range(CONV_DIM // CONV_LB):
            lanes = slice(lb * CONV_LB, (lb + 1) * CONV_LB)
            w, b = ins[2][:, lanes], ins[3][:, lanes]

            for r0 in range(0, tm, CONV_RC):
                taps = _conv_taps(s, r0, lanes)
                pre = b + sum(w[k:k + 1] * taps[k] for k in range(CONV_K))
                outs[0][pl.ds(r0, CONV_RC), lanes] = _silu(pre)
    return rowcall(name, body, T, tm, [("row", xbc, CONV_DIM, 0), ("prev", xbc, CONV_DIM, 0, CONV_HALO), ("const", conv_w), ("const", conv_b)],
                   [(CONV_DIM, f32)], scratch=[pltpu.VMEM((tm + CONV_HALO, CONV_DIM), f32)])[0]


def conv_bwd_pre(name, T, tm, xbc, d_xc, conv_w, conv_b):
    def body(i, n, ins, outs, accs, scr):
        s = scr[0]
        _conv_fill(i, ins[0], ins[1], s, tm)
        fold = lambda v: jnp.sum(v.reshape(CONV_RC // 8, 8, CONV_LB), axis=0)
        for lb in range(CONV_DIM // CONV_LB):
            lanes = slice(lb * CONV_LB, (lb + 1) * CONV_LB)
            w, b = ins[3][:, lanes], ins[4][:, lanes]

            sums = [jnp.zeros((8, CONV_LB), f32)] * (CONV_K + 1)
            for r0 in range(0, tm, CONV_RC):
                taps = _conv_taps(s, r0, lanes)
                pre = b + sum(w[k:k + 1] * taps[k] for k in range(CONV_K))
                _, vjp = jax.vjp(_silu, pre)
                dpre = vjp(ins[2][pl.ds(r0, CONV_RC), lanes])[0]
                outs[0][pl.ds(r0, CONV_RC), lanes] = dpre
                sums = [sums[k] + fold(dpre * taps[k]) for k in range(CONV_K)] + [sums[CONV_K] + fold(dpre)]
            for k in range(CONV_K):
                accs[0][pl.ds(k, 1), lanes] += jnp.sum(sums[k], axis=0, keepdims=True)
            accs[1][:, lanes] += jnp.sum(sums[CONV_K], axis=0, keepdims=True)
    return rowcall(name, body, T, tm,
                   [("row", xbc, CONV_DIM, 0), ("prev", xbc, CONV_DIM, 0, CONV_HALO), ("row", d_xc, CONV_DIM, 0), ("const", conv_w), ("const", conv_b)],
                   [(CONV_DIM, f32)], accs=[(CONV_K, CONV_DIM), (1, CONV_DIM)], scratch=[pltpu.VMEM((tm + CONV_HALO, CONV_DIM), f32)])


def conv_bwd_x(name, T, tm, d_pre, conv_w):
    def body(i, n, ins, outs, accs, scr):
        s = scr[0]
        s[pl.ds(0, tm), :] = ins[0][...]
        s[pl.ds(tm, CONV_HALO), :] = jnp.where(i < n - 1, ins[1][...], 0.0)
        for lb in range(CONV_DIM // CONV_LB):
            lanes = slice(lb * CONV_LB, (lb + 1) * CONV_LB)
            w = ins[2][:, lanes]

            for r0 in range(0, tm, CONV_RC):
                dx = sum(w[k:k + 1] * s[pl.ds(r0 + CONV_K - 1 - k, CONV_RC), lanes] for k in range(CONV_K))
                outs[0][pl.ds(r0, CONV_RC), lanes] = dx.astype(bf16)
    return rowcall(name, body, T, tm, [("row", d_pre, CONV_DIM, 0), ("next", d_pre, CONV_DIM, 0, CONV_HALO), ("const", conv_w)],
                   [(CONV_DIM, bf16)], scratch=[pltpu.VMEM((tm + CONV_HALO, CONV_DIM), f32)])[0]


def _ssd_prep(dtr, dtb, alog):
    rr = lax.broadcasted_iota(jnp.int32, (CHUNK, CHUNK), 0)
    cc = lax.broadcasted_iota(jnp.int32, (CHUNK, CHUNK), 1)
    dt = _softplus(dtr + dtb)
    dA = dt * -jnp.exp(alog)
    acum = jnp.dot((rr >= cc).astype(f32), dA, precision=HI, preferred_element_type=f32)
    return dt, acum, acum.T, jnp.sum(dA, axis=0, keepdims=True)


def _ssd_group(g, x, Bm, Cm, S, dt, acum, acumT, tot, dsk):
    rr = lax.broadcasted_iota(jnp.int32, (CHUNK, CHUNK), 0)
    cc = lax.broadcasted_iota(jnp.int32, (CHUNK, CHUNK), 1)
    tril = rr >= cc
    lane = lax.broadcasted_iota(jnp.int32, (1, DT_PAD), 1)
    sub = lax.broadcasted_iota(jnp.int32, (DT_PAD, 1), 0)
    glane = lax.broadcasted_iota(jnp.int32, (1, SSM_HPG * SSM_P), 1) // SSM_P
    hm = [(glane == r).astype(f32) for r in range(SSM_HPG)]
    pick = lambda v, r: jnp.sum(v * (lane == SSM_HPG * g + r).astype(f32), axis=1, keepdims=True)
    cols = [pick(acum, r) for r in range(SSM_HPG)]
    tots = [pick(tot, r) for r in range(SSM_HPG)]
    spread = lambda vals: sum(vals[r] * hm[r] for r in range(SSM_HPG))
    xdt = x * spread([pick(dt, r) for r in range(SSM_HPG)])
    cb = _bdot(Cm, Bm, NT)
    y = x * spread([pick(dsk, r) for r in range(SSM_HPG)])
    for r in range(SSM_HPG):
        row = jnp.sum(acumT * (sub == SSM_HPG * g + r).astype(f32), axis=0, keepdims=True)
        dec = jnp.exp(jnp.where(tril, cols[r] - row, -jnp.inf))
        y = y + _bdot(cb * dec, xdt * hm[r])
    y = y + _bdot(Cm, S) * spread([jnp.exp(c) for c in cols])
    dte = spread([jnp.exp(tots[r] - cols[r]) for r in range(SSM_HPG)])
    s_new = S * spread([jnp.exp(t) for t in tots]) + _bdot(Bm, xdt * dte, TN)
    return y, s_new


def _ssd_ins(xc, dtr):
    gw = SSM_HPG * SSM_P
    ins = [("row", xc, gw, g) for g in range(SSM_GROUPS)]
    ins += [("row", xc, SSM_N, D // SSM_N + g) for g in range(SSM_GROUPS)]
    ins += [("row", xc, SSM_N, D // SSM_N + SSM_GROUPS + g) for g in range(SSM_GROUPS)]
    ins += [("row", dtr, DT_PAD, 0)]
    return ins


SSD_CPS = 2


def ssd_fwd(name, T, xc, dtr, dtb, alog, dsk):
    gw = SSM_HPG * SSM_P

    def body(i, n, ins, outs, accs, scr):
        S = scr[0]

        @pl.when(i == 0)
        def _():
            S[...] = jnp.zeros(S.shape, f32)
        S4 = tuple(S[:, g * gw:(g + 1) * gw] for g in range(4))
        for c in range(SSD_CPS):
            rows = pl.ds(c * CHUNK, CHUNK)
            X4 = tuple(ins[g][rows, :] for g in range(4))
            B4 = tuple(ins[4 + g][rows, :] for g in range(4))
            C4 = tuple(ins[8 + g][rows, :] for g in range(4))
            prep = _ssd_prep(ins[12][rows, :], ins[13][...], ins[14][...])
            nxt = []
            for g in range(4):
                outs[1][rows, g * gw:(g + 1) * gw] = S4[g]
                y, s_new = _ssd_group(g, X4[g], B4[g], C4[g], S4[g], *prep, ins[15][...])
                outs[0][rows, g * gw:(g + 1) * gw] = y
                nxt.append(s_new)
            S4 = tuple(nxt)
        for g in range(4):
            S[:, g * gw:(g + 1) * gw] = S4[g]
    ins = _ssd_ins(xc, dtr) + [("const", dtb), ("const", alog), ("const", dsk)]
    return rowcall(name, body, T, SSD_CPS * CHUNK, ins, [(D, f32), (D, f32)], scratch=[pltpu.VMEM((SSM_N, D), f32)])


def ssd_bwd(name, T, xc, dtr, sprev, d_y, dtb, alog, dsk):
    gw = SSM_HPG * SSM_P

    def body(i, n, ins, outs, accs, scr):
        dS = scr[0]

        @pl.when(i == n - 1)
        def _():
            dS[...] = jnp.zeros(dS.shape, f32)
        dS4 = tuple(dS[:, g * gw:(g + 1) * gw] for g in range(4))
        def chunk(X4, dtr_c, B4, C4, S4, dtb_c, alog_c, dsk_c):
            prep = _ssd_prep(dtr_c, dtb_c, alog_c)
            res = [_ssd_group(g, X4[g], B4[g], C4[g], S4[g], *prep, dsk_c) for g in range(4)]
            return tuple(r[0] for r in res), tuple(r[1] for r in res)
        X4 = tuple(ins[g][...] for g in range(4))
        B4 = tuple(ins[4 + g][...] for g in range(4))
        C4 = tuple(ins[8 + g][...] for g in range(4))
        S4 = tuple(ins[13 + g][...] for g in range(4))
        dY4 = tuple(ins[17 + g][...] for g in range(4))
        _, vjp = jax.vjp(chunk, X4, ins[12][...], B4, C4, S4, ins[21][...], ins[22][...], ins[23][...])
        dX4, ddtr, dB4, dC4, dS4, ddtb, dalog, ddsk = vjp((dY4, dS4))
        for g in range(4):
            outs[0][:, g * gw:(g + 1) * gw] = dX4[g]
            outs[0][:, D + g * SSM_N:D + (g + 1) * SSM_N] = dB4[g]
            outs[0][:, D + (SSM_GROUPS + g) * SSM_N:D + (SSM_GROUPS + g + 1) * SSM_N] = dC4[g]
            dS[:, g * gw:(g + 1) * gw] = dS4[g]
        outs[1][...] = ddtr.astype(bf16)
        accs[0][...] += ddtb
        accs[1][...] += dalog
        accs[2][...] += ddsk
    ins = _ssd_ins(xc, dtr) + [("row", sprev, gw, g) for g in range(4)] + [("row", d_y, gw, g) for g in range(4)]
    ins += [("const", dtb), ("const", alog), ("const", dsk)]
    return rowcall(name, body, T, CHUNK, ins, [(CONV_DIM, f32), (DT_PAD, bf16)], accs=[(1, DT_PAD)] * 3,
                   scratch=[pltpu.VMEM((SSM_N, D), f32)], reverse=True)


def _gate_group(y, z, g):
    return _rms(y * _silu(z), g)


def gate_fwd(name, T, tm, y, uvz, gn):
    def body(i, n, ins, outs, accs, scr):
        for g in range(SSM_GROUPS):
            cols = slice(g * 256, (g + 1) * 256)
            outs[0][:, cols] = _gate_group(ins[0][:, cols], ins[1][:, cols], ins[2][:, cols]).astype(bf16)
    return rowcall(name, body, T, tm, [("row", y, D, 0), ("row", uvz, D, 2), ("const", gn)], [(D, bf16)], sub=64)[0]


def gate_bwd(name, T, tm, y, uvz, d_yb, gn):
    def body(i, n, ins, outs, accs, scr):
        for g in range(SSM_GROUPS):
            cols = slice(g * 256, (g + 1) * 256)
            _, vjp = jax.vjp(_gate_group, ins[0][:, cols], ins[1][:, cols], ins[3][:, cols])
            dy, dz, dg = vjp(ins[2][:, cols])
            outs[0][:, cols] = dy
            outs[1][:, cols] = dz.astype(bf16)
            accs[0][:, cols] += dg
    return rowcall(name, body, T, tm, [("row", y, D, 0), ("row", uvz, D, 2), ("row", d_yb, D, 0), ("const", gn)],
                   [(D, f32), (D, bf16)], accs=[(1, D)], sub=64)


def _pool_diff(i, tm, h_ref, halo_ref, g_ref, scr):
    g = g_ref[...]
    yn = _rms(h_ref[...], g)
    scr[pl.ds(0, POOL_HALO), :] = jnp.where(i > 0, _rms(halo_ref[...], g), 0.0)
    scr[pl.ds(POOL_HALO, tm), :] = yn
    pos = (i * tm + lax.broadcasted_iota(jnp.int32, (tm, 1), 0) + 1).astype(f32)
    parts = []
    for gi, win in enumerate(POOL_WINDOWS):
        cols = slice(gi * POOL_GD, (gi + 1) * POOL_GD)
        s = scr[pl.ds(POOL_HALO, tm), cols]
        for j in range(1, win):
            s = s + scr[pl.ds(POOL_HALO - j, tm), cols]
        parts.append(s / jnp.minimum(pos, float(win)) - yn[:, cols])
    return parts


def pool_fwd(name, T, tm, h2, g_pre, pw, pb, psc, g_post, g_next):
    def body(i, n, ins, outs, accs, scr):
        parts = _pool_diff(i, tm, ins[0], ins[1], ins[2], scr[0])
        for gi in range(len(POOL_WINDOWS)):
            cols = slice(gi * POOL_GD, (gi + 1) * POOL_GD)
            o = _bdot(parts[gi], ins[3][gi]) + ins[4][:, cols]
            outs[0][:, cols] = o * ins[5][:, cols]
        h = ins[0][...] + _rms(outs[0][...], ins[6][...])
        outs[1][...] = h
        outs[2][...] = _rms(h, ins[7][...]).astype(bf16)
    return rowcall(name, body, T, tm, [("row", h2, D, 0), ("prev", h2, D, 0, POOL_HALO), ("const", g_pre), ("const", pw), ("const", pb), ("const", psc),
                                       ("const", g_post), ("const", g_next)],
                   [(D, f32), (D, f32), (D, bf16)], scratch=[pltpu.VMEM((tm + POOL_HALO, D), f32)])


def pool_bwd(name, T, tm, h2, d_pm, d_res, g_pre, pw, pb, psc, f_prev, g_prev):
    def body(i, n, ins, outs, accs, scr):
        parts = _pool_diff(i, tm, ins[0], ins[1], ins[5], scr[0])
        dpm = ins[2][...]
        psc_v = ins[8][...]
        dps = dpm * psc_v
        dps_halo = jnp.where(i < n - 1, ins[3][...] * psc_v, 0.0)
        accs[1][...] += jnp.sum(dps, axis=0, keepdims=True)
        pos = (i * tm + lax.broadcasted_iota(jnp.int32, (tm, 1), 0) + 1).astype(f32)
        pos_h = ((i + 1) * tm + lax.broadcasted_iota(jnp.int32, (POOL_HALO, 1), 0) + 1).astype(f32)
        r_scr = scr[1]
        dyn_scr = scr[2]
        for gi, win in enumerate(POOL_WINDOWS):
            cols = slice(gi * POOL_GD, (gi + 1) * POOL_GD)
            w = ins[6][gi]
            o = _bdot(parts[gi], w) + ins[7][:, cols]
            accs[2][:, cols] += jnp.sum(dpm[:, cols] * o, axis=0, keepdims=True)
            accs[0][gi] += _bdot(parts[gi], dps[:, cols], TN)
            q = _bdot(dps[:, cols], w, NT)
            qh = _bdot(dps_halo[:, cols], w, NT)
            r_scr[pl.ds(0, tm), cols] = q / jnp.minimum(pos, float(win))
            r_scr[pl.ds(tm, POOL_HALO), cols] = qh / jnp.minimum(pos_h, float(win))
            s = r_scr[pl.ds(0, tm), cols]
            for j in range(1, win):
                s = s + r_scr[pl.ds(j, tm), cols]
            dyn_scr[:, cols] = s - q
        dx, dg = _rms_bwd(ins[0][...], ins[5][...], dyn_scr[...])
        dh = ins[4][...] + dx
        outs[0][...] = dh
        accs[3][...] += dg
        df, dgp = _rms_bwd(ins[9][...], ins[10][...], dh)
        outs[1][...] = df.astype(bf16)
        accs[4][...] += dgp
    ins = [("row", h2, D, 0), ("prev", h2, D, 0, POOL_HALO), ("row", d_pm, D, 0), ("next", d_pm, D, 0, POOL_HALO), ("row", d_res, D, 0),
           ("const", g_pre), ("const", pw), ("const", pb), ("const", psc), ("row", f_prev, D, 0), ("const", g_prev)]
    return rowcall(name, body, T, tm, ins, [(D, f32), (D, bf16)], accs=[(4, POOL_GD, POOL_GD), (1, D), (1, D), (1, D), (1, D)],
                   scratch=[pltpu.VMEM((tm + POOL_HALO, D), f32), pltpu.VMEM((tm + POOL_HALO, D), f32), pltpu.VMEM((tm, D), f32)])


def local_step(T, x, tgt, W, ffn_weights, early_grads):
    tm = 512 if T >= 1024 else T // 2
    TKW = 4096 if T >= 4096 else T
    ng = W["norm_g"]
    g = lambda l, j: ng[l, j][None, :]
    G = {}

    row_spec = pl.BlockSpec((tm, D), lambda j, i, k: (i, 0))
    tf = tm // 2 if T >= 1024 else tm
    rows_f = pl.BlockSpec((tf, D), lambda i: (i, 0))
    vec_f = pl.BlockSpec((1, D), lambda i: (0, 0))
    sh_f = [pl.BlockSpec((None, tf, FF_SH), lambda i, s=s: (s, i, 0)) for s in range(4)]
    sh_spec = [pl.BlockSpec((None, tm, FF_SH), lambda j, i, k, s=s: (s, i, 0)) for s in range(4)]
    out_f32, out_bf16 = (SDS((T, D), f32), rows_f), (SDS((T, D), bf16), rows_f)

    def resid_epilogue(with_pre):
        def ep(part, xs, os, accs):
            h = xs[0][...] + _rms(part, xs[1][...])
            os[0][...] = part
            os[1][...] = h
            if with_pre:
                os[2][...] = _rms(h, xs[2][...]).astype(bf16)
        return ep

    def bwd_epilogue(df_dtype):
        def ep(part, xs, os, accs):
            dx, dgp = _rms_bwd(xs[0][...], xs[3][...], part)
            dh = xs[2][...] + dx
            df, dgq = _rms_bwd(xs[1][...], xs[4][...], dh)
            os[0][...] = dh
            os[1][...] = df.astype(df_dtype)
            accs[0][...] += dgp
            accs[1][...] += dgq
        return ep

    def loss_epilogue(part, xs, os, accs):
        g_post = xs[2][...]
        e = xs[0][...] + _rms(part, g_post) - xs[1][...]
        accs[0][...] += jnp.sum(jnp.sum(e * e, axis=-1, keepdims=True) * (0.5 / D), axis=0, keepdims=True)
        dh = e * (1.0 / D)
        df, dg = _rms_bwd(part, g_post, dh)
        os[0][...] = dh
        os[1][...] = df.astype(bf16)
        accs[1][...] += dg

    def ffn_fwd(tag, n_bf, l, resid=None, loss=None):
        gate4, up4, act4 = ffn_up(f"ffn{tag}_up", T, tm, n_bf, W["wg4"], W["wu4"], l)
        wd_f = [pl.BlockSpec((None, FF_SH, D), lambda i, s=s: (s, l, 0)) for s in range(4)]
        pairs = [(act4, sh_f[s], W["wd4"], wd_f[s]) for s in range(4)]
        if loss is not None:
            return (gate4, up4, act4) + tuple(mm_fused(f"ffn{tag}_down", T // tf, pairs, NN, [(loss[0], rows_f), (loss[1], rows_f), (loss[2], vec_f)],
                                                       [out_f32, out_bf16], [(1, 1), (1, D)], loss_epilogue))
        f, h_out = mm_fused(f"ffn{tag}_down", T // tf, pairs, NN, [(resid[0], rows_f), (resid[1], vec_f)], [out_f32, out_f32], [],
                            resid_epilogue(False))
        return gate4, up4, act4, f, h_out

    def ffn_bwd(tag, l, n_bf, gate4, up4, act4, d_f, h_out, f_pre, d_res, g_pre, g_post, df_dtype):
        d_gate4, d_up4 = ffn_dgu(f"ffn{tag}_dgu", T, tm, d_f, W["wd4"], gate4, up4, l)
        w_f = [pl.BlockSpec((None, D, FF_SH), lambda i, s=s: (s, l, 0)) for s in range(4)]
        d_h, d_fp, dgp, dgq = mm_fused(
            f"ffn{tag}_dn", T // tf, [(d_gate4, sh_f[s], W["wg4"], w_f[s]) for s in range(4)] + [(d_up4, sh_f[s], W["wu4"], w_f[s]) for s in range(4)],
            NT, [(h_out, rows_f), (f_pre, rows_f), (d_res, rows_f), (g_pre, vec_f), (g_post, vec_f)],
            [out_f32, (SDS((T, D), df_dtype), rows_f)], [(1, D), (1, D)], bwd_epilogue(df_dtype))

        def wgrad(nm, a4, b):
            return mm(nm, (4, 1, T // TKW),
                      [(a4, pl.BlockSpec((None, TKW, FF_SH), lambda s, j, k: (s, k, 0)), b, pl.BlockSpec((TKW, D), lambda s, j, k: (k, 0)))],
                      TN, pl.BlockSpec((None, FF_SH, D), lambda s, j, k: (s, 0, 0)), SDS((4, FF_SH, D), f32))
        return d_h, d_fp, dgp, dgq, wgrad(f"ffn{tag}_dwg", d_gate4, n_bf), wgrad(f"ffn{tag}_dwu", d_up4, n_bf), wgrad(f"ffn{tag}_dwd", act4, d_f)

    y0 = rms_to_bf16("l0_prenorm", T, tm, x, g(0, 0))
    uvz = matmul("in_uvz", [(y0, W["w_uvz"])], "nn", f32, tm, 1024)
    xbc = matmul("in_xbc", [(y0, W["w_xbc"])], "nn", f32, tm, 1024)
    dtr = matmul("in_dt", [(y0, W["w_dt"])], "nn", f32, tm, DT_PAD)
    y_a = gmlp_fwd("gmlp_fwd", T, tm, uvz, W["ln_g"], W["ln_b"], W["wm"], W["bs"])
    xc = conv_fwd("conv_fwd", T, tm, xbc, W["conv_w"], W["conv_b"])
    y_ssd, sprev = ssd_fwd("ssd_fwd", T, xc, dtr, W["dtb"], W["alog"], W["dsk"])
    y_b = gate_fwd("gate_fwd", T, tm, y_ssd, uvz, W["gn"])
    half = D // 2
    wo4 = W["wo4"]
    ycol = [pl.BlockSpec((tf, half), lambda i, cb=cb: (i, cb)) for cb in range(2)]
    wo_s = [pl.BlockSpec((None, half, D), lambda i, s=s: (s, 0, 0)) for s in range(4)]
    mixo, h1, n1 = mm_fused("out_proj", T // tf, [(y_a, ycol[0], wo4, wo_s[0]), (y_a, ycol[1], wo4, wo_s[1]),
                                                  (y_b, ycol[0], wo4, wo_s[2]), (y_b, ycol[1], wo4, wo_s[3])], NN,
                            [(x, rows_f), (g(0, 1), vec_f), (g(0, 2), vec_f)], [out_f32, out_f32, out_bf16], [], resid_epilogue(True))
    W = dict(W)
    W["wg4"], W["wu4"], W["wd4"] = ffn_weights(h1)
    gate0, up0, act0, f1, h2 = ffn_fwd("0", n1, 0, resid=(h1, g(0, 3)))
    pm, h3, n3 = pool_fwd("pool_fwd", T, tm, h2, g(1, 0), W["pool_w"], W["pool_b"], W["pool_scale"], g(1, 1), g(1, 2))
    gate1, up1, act1, dh4, d_f2, loss_acc, dg13 = ffn_fwd("1", n3, 1, loss=(h3, tgt, g(1, 3)))
    d_h3, d_pm, dg12, dg11, dwg1, dwu1, dwd1 = ffn_bwd("1", 1, n3, gate1, up1, act1, d_f2, h3, pm, dh4, g(1, 2), g(1, 1), f32)
    d_h2, d_f1, G["pool_w"], G["pool_b"], G["pool_scale"], dg10, dg03 = pool_bwd("pool_bwd", T, tm, h2, d_pm, d_h3, g(1, 0), W["pool_w"], W["pool_b"],
                                                                                 W["pool_scale"], f1, g(0, 3))
    d_h1, d_mixo, dg02, dg01, dwg0, dwu0, dwd0 = ffn_bwd("0", 0, n1, gate0, up0, act0, d_f1, h1, mixo, d_h2, g(0, 2), g(0, 1), bf16)
    def d_ycat(nm, s0):
        return mm(nm, (2, T // tm, 1), [(d_mixo, row_spec, wo4, pl.BlockSpec((None, half, D), lambda j, i, k: (s0 + j, 0, 0)))], NT,
                  pl.BlockSpec((tm, half), lambda j, i, k: (i, j)), SDS((T, D), f32))

    def d_wo(nm, y):
        return mm(nm, (2, 1, T // TKW), [(y, pl.BlockSpec((TKW, half), lambda s, j, k: (k, s)), d_mixo, pl.BlockSpec((TKW, D), lambda s, j, k: (k, 0)))],
                  TN, pl.BlockSpec((None, half, D), lambda s, j, k: (s, 0, 0)), SDS((2, half, D), f32))
    d_ya, d_yb = d_ycat("out_proj_dya", 0), d_ycat("out_proj_dyb", 2)
    dwo_a, dwo_b = d_wo("out_proj_dwa", y_a), d_wo("out_proj_dwb", y_b)
    G["wo4"] = [dwo_a[0], dwo_a[1], dwo_b[0], dwo_b[1]]
    G["wgT4"], G["wuT4"], G["wd4"] = [dwg0, dwg1], [dwu0, dwu1], [dwd0, dwd1]
    token = early_grads(G)
    d_yssd, d_z, G["gn"] = gate_bwd("gate_bwd", T, tm, y_ssd, uvz, d_yb, W["gn"] + token[0, 0])
    d_xc, d_dtr, G["dtb"], G["alog"], G["dsk"] = ssd_bwd("ssd_bwd", T, xc, dtr, sprev, d_yssd, W["dtb"], W["alog"], W["dsk"])
    d_pre, G["conv_w"], G["conv_b"] = conv_bwd_pre("conv_bwd_pre", T, tm, xbc, d_xc, W["conv_w"], W["conv_b"])
    d_xbc = conv_bwd_x("conv_bwd_x", T, tm, d_pre, W["conv_w"])
    d_u, d_v, G["ln_g"], G["ln_b"], G["wm"], G["bs"] = gmlp_bwd("gmlp_bwd", T, tm, uvz, d_ya, W["ln_g"], W["ln_b"], W["wm"], W["bs"])
    w_u, w_v, w_z = W["w_uvz"][:, :D], W["w_uvz"][:, D:2 * D], W["w_uvz"][:, 2 * D:]
    def pre_epilogue(part, xs, os, accs):
        dx, dg = _rms_bwd(xs[0][...], xs[2][...], part)
        os[0][...] = xs[1][...] + dx
        accs[0][...] += dg
    blk = lambda w: pl.BlockSpec((tf, w), lambda i: (i, 0))
    whole = lambda a: pl.BlockSpec(a.shape, lambda i: (0, 0))
    grad_x, dg00 = mm_fused("in_dy0", T // tf, [(d_u, blk(D), w_u, whole(w_u)), (d_v, blk(D), w_v, whole(w_v)), (d_z, blk(D), w_z, whole(w_z)),
                                                (d_xbc, blk(CONV_DIM), W["w_xbc"], whole(W["w_xbc"])), (d_dtr, blk(DT_PAD), W["w_dt"], whole(W["w_dt"]))],
                            NT, [(x, rows_f), (d_h1, rows_f), (g(0, 0), vec_f)], [out_f32], [(1, D)], pre_epilogue)
    G["w_inT"] = [matmul("in_dwu", [(d_u, y0)], "tn", f32, 1024, 1024, TKW), matmul("in_dwv", [(d_v, y0)], "tn", f32, 1024, 1024, TKW),
                  matmul("in_dwz", [(d_z, y0)], "tn", f32, 1024, 1024, TKW), matmul("in_dwxbc", [(d_xbc, y0)], "tn", f32, 1024, 1024, TKW),
                  matmul("in_dwdt", [(d_dtr, y0)], "tn", f32, DT_PAD, 1024, TKW)[:N_HEADS]]
    G["norm_g"] = jnp.stack([jnp.concatenate([dg00, dg01, dg02, dg03], 0), jnp.concatenate([dg10, dg11, dg12, dg13], 0)])
    return loss_acc, grad_x, G


def build_weights(Wf):
    causal = jnp.tril(jnp.ones((CHUNK, CHUNK), bool))
    w_in = Wf["w_in"].astype(bf16)
    pad16 = lambda v: jnp.pad(v.reshape(1, N_HEADS).astype(f32), ((0, 0), (0, DT_PAD - N_HEADS)))
    return {
        "norm_g": Wf["norm_g"],
        "w_uvz": w_in[:, :3 * D], "w_xbc": w_in[:, 3 * D:3 * D + CONV_DIM],
        "w_dt": jnp.pad(w_in[:, 3 * D + CONV_DIM:], ((0, 0), (0, DT_PAD - N_HEADS))),
        "ln_g": Wf["gm_ln_g"].reshape(1, D), "ln_b": Wf["gm_ln_b"].reshape(1, D),
        "wm": jnp.where(causal[None], Wf["gm_ws"], 0).astype(bf16), "bs": Wf["gm_bs"].reshape(GM_HEADS, CHUNK, 1),
        "conv_w": Wf["conv_w"], "conv_b": Wf["conv_b"].reshape(1, CONV_DIM),
        "dtb": pad16(Wf["dt_bias"]), "alog": pad16(Wf["a_log"]), "dsk": pad16(Wf["d_skip"]),
        "gn": Wf["ssm_norm_g"].reshape(1, D),
        "wo4": Wf["wo4"].astype(bf16),
        "pool_w": Wf["pool_w"].astype(bf16), "pool_b": Wf["pool_b"].reshape(1, D), "pool_scale": Wf["pool_scale"].reshape(1, D),
    }


def small_grads(G):
    return {
        "norm_g": G["norm_g"],
        "gm_ln_g": G["ln_g"].reshape(D), "gm_ln_b": G["ln_b"].reshape(D),
        "gm_ws": G["wm"], "gm_bs": G["bs"].reshape(GM_HEADS, CHUNK),
        "conv_w": G["conv_w"], "conv_b": G["conv_b"].reshape(CONV_DIM),
        "dt_bias": G["dtb"][0, :N_HEADS], "a_log": G["alog"][0, :N_HEADS], "d_skip": G["dsk"][0, :N_HEADS],
        "ssm_norm_g": G["gn"].reshape(D),
        "pool_b": G["pool_b"].reshape(4, POOL_GD), "pool_scale": G["pool_scale"].reshape(D),
    }


MESH_ID = pl.DeviceIdType.MESH
ANY = pl.BlockSpec(memory_space=pl.ANY)


DMA_CHUNK_BYTES = 2 << 20
DMA_MAX_CHUNKS = 32


def _pieces(view, axis, align):
    shape = view.shape
    nbytes = math.prod(shape) * jnp.dtype(view.dtype).itemsize
    n = max(1, min(DMA_MAX_CHUNKS, -(-nbytes // DMA_CHUNK_BYTES)))
    rows = shape[axis]
    size = -(-rows // n)
    size = -(-size // align) * align
    out = []
    for s in range(0, rows, size):
        idx = [slice(None)] * len(shape)
        idx[axis] = pl.ds(s, min(size, rows - s))
        out.append(tuple(idx))
    return out


def comm_call(name, operands, out_shapes, plan):
    n_in = len(operands)
    n_out = len(out_shapes)
    n_remote, n_local = plan((0, 0, 0), [None] * n_in, [None] * n_out, True)

    def body(*refs):
        in_refs, out_refs = refs[:n_in], refs[n_in:n_in + n_out]
        send_sems, recv_sems, local_sems = refs[n_in + n_out:]
        me = (lax.axis_index("x"), lax.axis_index("y"), lax.axis_index("c"))
        remote, local = plan(me, in_refs, out_refs, False)
        align = lambda v: 16 if v.dtype == bf16 else 8
        for j, (s, d, axis) in enumerate(local):
            for ix in _pieces(s, axis, align(s)):
                pltpu.make_async_copy(s.at[ix], d.at[ix], local_sems.at[j]).start()
        peers = [tuple((1 - m) if f else m for m, f in zip(me, flip)) for flip, *_ in remote]
        for k, (flip, src, dst, _, axis) in enumerate(remote):
            for ix in _pieces(src, axis, align(src)):
                pltpu.make_async_remote_copy(src_ref=src.at[ix], dst_ref=dst.at[ix], send_sem=send_sems.at[k], recv_sem=recv_sems.at[k],
                                             device_id=peers[k], device_id_type=MESH_ID).start()
        for k, (flip, src, dst, landing, axis) in enumerate(remote):
            pltpu.make_async_remote_copy(src_ref=landing, dst_ref=landing, send_sem=send_sems.at[k], recv_sem=recv_sems.at[k],
                                         device_id=peers[k], device_id_type=MESH_ID).wait_recv()
        for k, (flip, src, dst, landing, axis) in enumerate(remote):
            pltpu.make_async_remote_copy(src_ref=src, dst_ref=dst, send_sem=send_sems.at[k], recv_sem=recv_sems.at[k],
                                         device_id=peers[k], device_id_type=MESH_ID).wait_send()
        for j, (s, d, axis) in enumerate(local):
            pltpu.make_async_copy(s, d, local_sems.at[j]).wait()

    return pl.pallas_call(
        body, name=name, out_shape=list(out_shapes), in_specs=[ANY] * n_in, out_specs=[ANY] * n_out,
        scratch_shapes=[pltpu.SemaphoreType.DMA((n_remote,)), pltpu.SemaphoreType.DMA((n_remote,)), pltpu.SemaphoreType.DMA((max(n_local, 1),))],
    )(*operands)


CHIP_FLIPS = ((1, 0, 0), (0, 1, 0), (1, 1, 0))
PAIR_FLIP = (0, 0, 1)


def gather_over_chips(name, arrs):
    def plan(me, ins, outs, count):
        if count:
            return len(CHIP_FLIPS) * len(arrs), len(arrs)
        k = 2 * me[0] + me[1]
        remote, local = [], []
        for a in range(len(arrs)):
            for flip in CHIP_FLIPS:
                kp = 2 * ((1 - me[0]) if flip[0] else me[0]) + ((1 - me[1]) if flip[1] else me[1])
                remote.append((flip, ins[a], outs[a].at[k], outs[a].at[kp], 0))
            local.append((ins[a], outs[a].at[k], 0))
        return remote, local
    return comm_call(name, arrs, [SDS((4,) + a.shape, a.dtype) for a in arrs], plan)


def gather_two_level(name, halved, whole):
    nh, nw = len(halved), len(whole)
    nf = len(CHIP_FLIPS)

    def body(*refs):
        srcs, outs = refs[:nh + nw], refs[nh + nw:2 * (nh + nw)]
        send_sems, recv_sems, fwd_send, fwd_recv = refs[2 * (nh + nw):]
        me = (lax.axis_index("x"), lax.axis_index("y"), lax.axis_index("c"))
        k, c = 2 * me[0] + me[1], me[2]
        sibling = (me[0], me[1], 1 - c)
        peers = [tuple((1 - m) if fl else m for m, fl in zip(me, flip)) for flip in CHIP_FLIPS]

        def half(ref, which):
            rh = ref.shape[0] // 2
            return ref.at[pl.ds(pl.multiple_of(which * rh, 16), rh), :]

        def ici(a, f):
            src = half(srcs[a], c) if a < nh else srcs[a]
            dst = half(outs[a].at[k], c) if a < nh else outs[a].at[k]
            return pltpu.make_async_remote_copy(src_ref=src, dst_ref=dst, send_sem=send_sems.at[a * nf + f], recv_sem=recv_sems.at[a * nf + f],
                                                device_id=peers[f], device_id_type=MESH_ID)

        def landed(a, f):
            slot = outs[a].at[_chip_of(me, CHIP_FLIPS[f])]
            return half(slot, c) if a < nh else slot

        def forward(a, f, which):
            v = half(outs[a].at[_chip_of(me, CHIP_FLIPS[f])], which)
            return pltpu.make_async_remote_copy(src_ref=v, dst_ref=v, send_sem=fwd_send.at[a * nf + f], recv_sem=fwd_recv.at[a * nf + f],
                                                device_id=sibling, device_id_type=MESH_ID)

        copies = [ici(a, f) for a in range(nh + nw) for f in range(nf)]
        for cp in copies:
            cp.start()
        fwds = []
        for a in range(nh):
            for f in range(nf):
                lv = landed(a, f)
                pltpu.make_async_remote_copy(src_ref=lv, dst_ref=lv, send_sem=send_sems.at[a * nf + f], recv_sem=recv_sems.at[a * nf + f],
                                             device_id=peers[f], device_id_type=MESH_ID).wait_recv()
                fw = forward(a, f, c)
                fw.start()
                fwds.append(fw)
        for a in range(nh, nh + nw):
            for f in range(nf):
                lv = landed(a, f)
                pltpu.make_async_remote_copy(src_ref=lv, dst_ref=lv, send_sem=send_sems.at[a * nf + f], recv_sem=recv_sems.at[a * nf + f],
                                             device_id=peers[f], device_id_type=MESH_ID).wait_recv()
        for a in range(nh):
            for f in range(nf):
                forward(a, f, 1 - c).wait_recv()
        for fw in fwds:
            fw.wait_send()
        for cp in copies:
            cp.wait_send()

    arrs = list(halved) + list(whole)
    n_ici = (nh + nw) * nf
    return pl.pallas_call(
        body, name=name, out_shape=[SDS((N_CHIPS,) + a.shape, a.dtype) for a in arrs], in_specs=[ANY] * len(arrs), out_specs=[ANY] * len(arrs),
        scratch_shapes=[pltpu.SemaphoreType.DMA((n_ici,)), pltpu.SemaphoreType.DMA((n_ici,)),
                        pltpu.SemaphoreType.DMA((nh * nf,)), pltpu.SemaphoreType.DMA((nh * nf,))],
    )(*arrs)


def pair_split_exchange(name, p, rh):
    def plan(me, ins, outs, count):
        if count:
            return 1, 0
        theirs = ins[0].at[:, pl.ds(pl.multiple_of((1 - me[2]) * rh, 8), rh), :]
        return [(PAIR_FLIP, theirs, outs[0], outs[0], 1)], []
    return comm_call(name, [p], [SDS((4, rh, p.shape[2]), p.dtype)], plan)[0]


def scatter_over_chips(name, cs):
    def plan(me, ins, outs, count):
        if count:
            return len(CHIP_FLIPS), 0
        k = 2 * me[0] + me[1]
        remote = []
        for flip in CHIP_FLIPS:
            kp = 2 * ((1 - me[0]) if flip[0] else me[0]) + ((1 - me[1]) if flip[1] else me[1])
            remote.append((flip, ins[0].at[kp], outs[0].at[k], outs[0].at[kp], 0))
        return remote, []
    return comm_call(name, [cs], [SDS(cs.shape, cs.dtype)], plan)[0]


def pair_swap(name, half):
    def plan(me, ins, outs, count):
        if count:
            return 1, 0
        return [(PAIR_FLIP, ins[0], outs[0], outs[0], 0)], []
    return comm_call(name, [half], [SDS(half.shape, half.dtype)], plan)[0]


def _row_tile(rows, cap=512):
    if rows <= cap:
        return rows
    t = cap - cap % 8
    while rows % t:
        t -= 8
    return t


def pair_sum(name, packs, got, c_arr, tile):
    rh = got.shape[1]
    nb = rh // tile

    def kern(c_ref, a_ref, b_ref, o16_ref):
        o16_ref[...] = (a_ref[...] + b_ref[...]).astype(bf16)
    blk = (None, tile, D)
    grid_spec = pltpu.PrefetchScalarGridSpec(
        num_scalar_prefetch=1, grid=(4, nb),
        in_specs=[pl.BlockSpec(blk, lambda s, i, c: (s, c[0] * nb + i, 0)), pl.BlockSpec(blk, lambda s, i, c: (s, i, 0))],
        out_specs=pl.BlockSpec(blk, lambda s, i, c: (s, i, 0)))
    return pl.pallas_call(kern, name=name, grid_spec=grid_spec, out_shape=SDS(got.shape, bf16),
                          compiler_params=pltpu.CompilerParams(dimension_semantics=("parallel", "parallel")))(c_arr, packs, got)


def chip_sum(name, own16, landed16, k_arr, tile):
    rh = own16.shape[1]
    nb = rh // tile

    def kern(k_ref, own_ref, l0, l1, l2, l3, o_ref):
        k = k_ref[0]
        s = None
        for j, lref in enumerate((l0, l1, l2, l3)):
            t = jnp.where(k == j, own_ref[...], lref[...]).astype(f32)
            s = t if s is None else s + t
        o_ref[...] = s
    blk = (None, tile, D)
    land = [pl.BlockSpec(blk, lambda i, k, j=j: (jnp.where(k[0] == j, (j + 1) % N_CHIPS, j), i, 0)) for j in range(N_CHIPS)]
    grid_spec = pltpu.PrefetchScalarGridSpec(
        num_scalar_prefetch=1, grid=(nb,),
        in_specs=[pl.BlockSpec(blk, lambda i, k: (k[0], i, 0))] + land,
        out_specs=pl.BlockSpec((tile, D), lambda i, k: (i, 0)))
    return pl.pallas_call(kern, name=name, grid_spec=grid_spec, out_shape=SDS((rh, D), f32),
                          compiler_params=pltpu.CompilerParams(dimension_semantics=("parallel",)))(k_arr, own16, landed16, landed16, landed16, landed16)


def adamw(name, w, g, m, v):
    R, C = w.shape
    tr = _row_tile(R, 256)

    def kern(w_ref, g_ref, m_ref, v_ref, d_ref, mo_ref, vo_ref):
        gg = g_ref[...]
        mn = ADAM_B1 * m_ref[...] + (1.0 - ADAM_B1) * gg
        vn = ADAM_B2 * v_ref[...] + (1.0 - ADAM_B2) * jnp.square(gg)
        m_hat = mn / (1.0 - ADAM_B1 ** ADAM_STEP)
        v_hat = vn / (1.0 - ADAM_B2 ** ADAM_STEP)
        d_ref[...] = -ADAM_LR * (m_hat / (jnp.sqrt(v_hat) + ADAM_EPS) + ADAM_WD * w_ref[...])
        mo_ref[...] = mn
        vo_ref[...] = vn
    spec = pl.BlockSpec((tr, C), lambda i: (i, 0))
    s = SDS((R, C), f32)
    return pl.pallas_call(kern, name=name, grid=(R // tr,), in_specs=[spec] * 4, out_specs=[spec] * 3, out_shape=[s, s, s],
                          compiler_params=pltpu.CompilerParams(dimension_semantics=("parallel",)))(w, g, m, v)


WEIGHT_NAMES = ("norm_g", "w_in", "gm_ln_g", "gm_ln_b", "gm_ws", "gm_bs", "conv_w", "conv_b", "dt_bias", "a_log", "d_skip",
                "ssm_norm_g", "w_out", "pool_w", "pool_b", "pool_scale", "ffn_w_gate", "ffn_w_up", "ffn_w_down")
SMALL = ("norm_g", "conv_w", "pool_b", "pool_scale")
REPL = ("gm_ln_g", "gm_ln_b", "gm_ws", "gm_bs", "conv_b", "dt_bias", "a_log", "d_skip", "ssm_norm_g")
SMALL_AXIS = {"norm_g": 2, "conv_w": 1, "pool_b": 1, "pool_scale": 0}
N_CHIPS = 4
IN_SH = IN_DIM // N_CHIPS
SMALL_ROWS = 8
REPL_ROWS = 72
E_OUT, E_GATE, E_UP, E_DOWN = 0, 512, 512 + 2 * FF_SH, 512 + 4 * FF_SH
E_POOL = E_DOWN + 2 * FF_SH
E_ROWS, E_TILE = E_POOL + 64, 400
L_SMALL, L_REPL, L_IN = 0, SMALL_ROWS, SMALL_ROWS + REPL_ROWS
L_END = L_IN + IN_SH
L_ROWS, L_TILE = 1408, 352


def _flat_rows(pieces, rows):
    v = jnp.concatenate([p.reshape(-1) for p in pieces])
    return jnp.pad(v, (0, rows * D - v.shape[0])).reshape(rows, D)


def _shard_small(name, full, k):
    ax = SMALL_AXIS[name]
    n = full.shape[ax] // N_CHIPS
    return lax.slice_in_dim(full, k * n, (k + 1) * n, axis=ax)


def _drop1(name, a):
    return a if name == "norm_g" else a[0]


def _row_range(blocks, lo, hi):
    out, off = [], 0
    for b in blocks:
        n = b.shape[0]
        a, e = max(lo, off), min(hi, off + n)
        if a < e:
            out.append(b[a - off:e - off])
        off += n
    return out


HBM_SPEC = pl.BlockSpec(memory_space=pltpu.HBM)
SEM_SPEC = pl.BlockSpec(memory_space=pltpu.SEMAPHORE)
SPLIT_EFFECT = pltpu.SideEffectType.DATAFLOW_SIDE_EFFECTING


def _chip_of(me, flip):
    return 2 * ((1 - me[0]) if flip[0] else me[0]) + ((1 - me[1]) if flip[1] else me[1])


def gather_start(name, arrs, after, slotted=False):
    n = len(arrs)
    ncp = n * len(CHIP_FLIPS)

    def body(*refs):
        srcs, lands = refs[:n], refs[n:2 * n]
        send_sems, recv_sems, token = refs[2 * n + 1], refs[2 * n + 2], refs[-1]
        me = (lax.axis_index("x"), lax.axis_index("y"), lax.axis_index("c"))
        k = 2 * me[0] + me[1]
        for a in range(n):
            for f, flip in enumerate(CHIP_FLIPS):
                peer = tuple((1 - m) if fl else m for m, fl in zip(me, flip))
                src = srcs[a].at[_chip_of(me, flip)] if slotted else srcs[a]
                for ix in _pieces(src, 0, 16):
                    pltpu.make_async_remote_copy(src_ref=src.at[ix], dst_ref=lands[a].at[k].at[ix],
                                                 send_sem=send_sems.at[a * len(CHIP_FLIPS) + f], recv_sem=recv_sems.at[a * len(CHIP_FLIPS) + f],
                                                 device_id=peer, device_id_type=MESH_ID).start()
        token[...] = jnp.zeros_like(token)

    land_shapes = [a.shape if slotted else (N_CHIPS,) + a.shape for a in arrs]
    operands = [pltpu.with_memory_space_constraint(a, pltpu.HBM) for a in arrs]
    operands += [pltpu.with_memory_space_constraint(lax.empty(s, a.dtype), pltpu.HBM) for s, a in zip(land_shapes, arrs)]
    out = pl.pallas_call(
        body, name=name,
        out_shape=(pltpu.SemaphoreType.DMA((ncp,)), pltpu.SemaphoreType.DMA((ncp,)), *[pltpu.HBM(a.shape, a.dtype) for a in arrs],
                   *[pltpu.HBM(s, a.dtype) for s, a in zip(land_shapes, arrs)], SDS((8, 128), f32)),
        in_specs=[HBM_SPEC] * (2 * n) + [ANY], out_specs=(SEM_SPEC, SEM_SPEC, *[HBM_SPEC] * (2 * n), pl.BlockSpec(memory_space=pltpu.VMEM)),
        input_output_aliases={i: 2 + i for i in range(2 * n)},
        compiler_params=pltpu.CompilerParams(has_side_effects=SPLIT_EFFECT),
    )(*operands, after)
    return out[0], out[1], out[2:2 + n], out[2 + n:2 + 2 * n], out[-1]


def gather_wait(name, send_sems, recv_sems, thru, lands, after, slotted=False):
    n = len(thru)

    def body(*refs):
        srcs, lands_r = refs[:n], refs[n:2 * n]
        s_sems, r_sems = refs[2 * n], refs[2 * n + 1]
        me = (lax.axis_index("x"), lax.axis_index("y"), lax.axis_index("c"))
        k = 2 * me[0] + me[1]
        for a in range(n):
            for f, flip in enumerate(CHIP_FLIPS):
                peer = tuple((1 - m) if fl else m for m, fl in zip(me, flip))
                idx = a * len(CHIP_FLIPS) + f
                src = srcs[a].at[_chip_of(me, flip)] if slotted else srcs[a]
                pltpu.make_async_remote_copy(src_ref=src, dst_ref=lands_r[a].at[k], send_sem=s_sems.at[idx], recv_sem=r_sems.at[idx],
                                             device_id=peer, device_id_type=MESH_ID).wait_send()
                pltpu.make_async_remote_copy(src_ref=src, dst_ref=lands_r[a].at[_chip_of(me, flip)], send_sem=s_sems.at[idx],
                                             recv_sem=r_sems.at[idx], device_id=peer, device_id_type=MESH_ID).wait_recv()

    out = pl.pallas_call(
        body, name=name, out_shape=tuple(pltpu.HBM(t.shape, t.dtype) for t in (*thru, *lands)),
        in_specs=[HBM_SPEC] * (2 * n) + [SEM_SPEC, SEM_SPEC, ANY], out_specs=tuple([HBM_SPEC] * (2 * n)),
        input_output_aliases={i: i for i in range(2 * n)},
        compiler_params=pltpu.CompilerParams(has_side_effects=SPLIT_EFFECT),
    )(*thru, *lands, send_sems, recv_sems, after)
    return out[:n], out[n:]


def gather_weights(w_sh):
    big = [w_sh["w_in"][0], w_sh["w_out"][0], w_sh["pool_w"][0].reshape(4 * 64, POOL_GD)]
    small_pack = _flat_rows([w_sh[n] for n in SMALL], SMALL_ROWS)
    own = [b.astype(bf16) for b in big] + [small_pack]
    my_k = 2 * lax.axis_index("x") + lax.axis_index("y")
    s_in, s_out, s_pool, s_small = [lax.dynamic_update_slice(s, o[None], (my_k, 0, 0))
                                    for s, o in zip(gather_two_level("gather_weights", own[:3], own[3:]), own)]
    Wf = {n: w_sh[n][0] for n in REPL}
    Wf["w_in"] = s_in.transpose(1, 0, 2).reshape(D, IN_DIM)
    Wf["pool_w"] = s_pool.reshape(N_CHIPS, 4, 64, POOL_GD).transpose(1, 0, 2, 3).reshape(4, POOL_GD, POOL_GD)
    Wf["wo4"] = s_out
    small_shapes = [_drop1(n, w_sh[n]).shape for n in SMALL]
    parts = [_split_rows(s_small[k], small_shapes) for k in range(N_CHIPS)]
    for j, n in enumerate(SMALL):
        Wf[n] = jnp.concatenate([parts[k][j] for k in range(N_CHIPS)], axis=SMALL_AXIS[n])
    return Wf


def pack_early(G):
    slots = [jnp.concatenate([G["wo4"][k], G["wgT4"][0][k], G["wgT4"][1][k], G["wuT4"][0][k], G["wuT4"][1][k], G["wd4"][0][k], G["wd4"][1][k],
                              G["pool_w"][:, k * 64:(k + 1) * 64, :].reshape(64, D)], axis=0) for k in range(N_CHIPS)]
    return jnp.stack(slots)


def pack_late(G):
    sg = small_grads(G)
    repl = _flat_rows([sg[n] for n in REPL], REPL_ROWS)
    w_in_t = jnp.concatenate(G["w_inT"], axis=0)
    slots = [jnp.concatenate([_flat_rows([_shard_small(n, sg[n], k) for n in SMALL], SMALL_ROWS), repl,
                              jnp.pad(w_in_t[k * IN_SH:(k + 1) * IN_SH], ((0, L_ROWS - L_END), (0, 0)))], axis=0)
             for k in range(N_CHIPS)]
    return jnp.stack(slots)


def unpack_grads(early, late, w_sh):
    g = {"w_out": early[E_OUT:E_GATE], "ffn_w_down": early[E_DOWN:E_POOL], "pool_w": early[E_POOL:E_ROWS],
         "ffn_w_gate": jnp.stack([early[E_GATE + l * FF_SH:E_GATE + (l + 1) * FF_SH].T for l in range(2)]),
         "ffn_w_up": jnp.stack([early[E_UP + l * FF_SH:E_UP + (l + 1) * FF_SH].T for l in range(2)]),
         "w_in": late[L_IN:L_END].T}
    small = _split_rows(late[L_SMALL:L_REPL], [_drop1(n, w_sh[n]).shape for n in SMALL])
    repl = _split_rows(late[L_REPL:L_IN], [w_sh[n][0].shape for n in REPL])
    g.update(zip(SMALL, small))
    g.update(zip(REPL, repl))
    return {n: g[n].reshape(w_sh[n].shape) for n in WEIGHT_NAMES}


def _split_rows(flat2d, shapes):
    v = flat2d.reshape(-1)
    out, off = [], 0
    for s in shapes:
        n = math.prod(s)
        out.append(v[off:off + n].reshape(s))
        off += n
    return out


def kernel(x, norm_g, w_in, gm_ln_g, gm_ln_b, gm_ws, gm_bs, conv_w, conv_b, dt_bias, a_log, d_skip, ssm_norm_g, w_out, pool_w, pool_b, pool_scale, ffn_w_gate, ffn_w_up, ffn_w_down, loss_target, m_norm_g, m_w_in, m_gm_ln_g, m_gm_ln_b, m_gm_ws, m_gm_bs, m_conv_w, m_conv_b, m_dt_bias, m_a_log, m_d_skip, m_ssm_norm_g, m_w_out, m_pool_w, m_pool_b, m_pool_scale, m_ffn_w_gate, m_ffn_w_up, m_ffn_w_down, v_norm_g, v_w_in, v_gm_ln_g, v_gm_ln_b, v_gm_ws, v_gm_bs, v_conv_w, v_conv_b, v_dt_bias, v_a_log, v_d_skip, v_ssm_norm_g, v_w_out, v_pool_w, v_pool_b, v_pool_scale, v_ffn_w_gate, v_ffn_w_up, v_ffn_w_down):
    T = x.shape[1]
    w_sh = dict(zip(WEIGHT_NAMES, (norm_g, w_in, gm_ln_g, gm_ln_b, gm_ws, gm_bs, conv_w, conv_b, dt_bias, a_log, d_skip, ssm_norm_g, w_out,
                                   pool_w, pool_b, pool_scale, ffn_w_gate, ffn_w_up, ffn_w_down)))
    m_sh = dict(zip(WEIGHT_NAMES, (m_norm_g, m_w_in, m_gm_ln_g, m_gm_ln_b, m_gm_ws, m_gm_bs, m_conv_w, m_conv_b, m_dt_bias, m_a_log, m_d_skip,
                                   m_ssm_norm_g, m_w_out, m_pool_w, m_pool_b, m_pool_scale, m_ffn_w_gate, m_ffn_w_up, m_ffn_w_down)))
    v_sh = dict(zip(WEIGHT_NAMES, (v_norm_g, v_w_in, v_gm_ln_g, v_gm_ln_b, v_gm_ws, v_gm_bs, v_conv_w, v_conv_b, v_dt_bias, v_a_log, v_d_skip,
                                   v_ssm_norm_g, v_w_out, v_pool_w, v_pool_b, v_pool_scale, v_ffn_w_gate, v_ffn_w_up, v_ffn_w_down)))

    my_k = 2 * lax.axis_index("x") + lax.axis_index("y")
    ffn_own = [w_sh["ffn_w_gate"].reshape(2 * D, FF_SH).astype(bf16), w_sh["ffn_w_up"].reshape(2 * D, FF_SH).astype(bf16),
               w_sh["ffn_w_down"].reshape(2 * FF_SH, D).astype(bf16)]
    Wf = gather_weights(w_sh)
    send_sems, recv_sems, thru, lands, token = gather_start("gather_ffn_start", ffn_own, Wf["wo4"])
    Wf["norm_g"] = Wf["norm_g"] + token[0, 0]
    W = build_weights(Wf)

    def ffn_weights(after):
        _, landed = gather_wait("gather_ffn_wait", send_sems, recv_sems, thru, lands, after)
        return tuple(lax.dynamic_update_slice(l, o[None], (my_k, 0, 0)) for l, o in zip(landed, ffn_own))

    my_c = lax.axis_index("c")
    c_arr = my_c.astype(jnp.int32).reshape(1)
    k_arr = my_k.astype(jnp.int32).reshape(1)

    def pair_stage(tag, packs, tile):
        got = pair_split_exchange(f"grads{tag}_pair_split", packs, packs.shape[1] // 2)
        return pair_sum(f"grads{tag}_pair_sum", packs, got, c_arr, tile)

    def chip_stage(tag, pair16, landed, tile):
        half = chip_sum(f"grads{tag}_chip_sum", pair16, landed, k_arr, tile)
        other = pair_swap(f"grads{tag}_pair_swap", half)
        return jnp.concatenate([jnp.where(my_c == 0, half, other), jnp.where(my_c == 0, other, half)], axis=0)

    early = {}

    def early_grads(Ge):
        pair16 = pair_stage("E", pack_early(Ge), E_TILE)
        s_sems, r_sems, thru, lands, tok = gather_start("gradsE_scatter_start", [pair16], jnp.zeros((8, 128), f32), slotted=True)
        early.update(s_sems=s_sems, r_sems=r_sems, thru=thru, lands=lands)
        return tok

    loss_acc, grad_x, G = local_step(T, x[0], loss_target[0], W, ffn_weights, early_grads)
    pair_l = pair_stage("L", pack_late(G), L_TILE)
    total_l = chip_stage("L", pair_l, scatter_over_chips("gradsL_scatter", pair_l), L_TILE)
    (pair_e,), (landed_e,) = gather_wait("gradsE_scatter_wait", early["s_sems"], early["r_sems"], early["thru"], early["lands"], total_l,
                                         slotted=True)
    total_e = chip_stage("E", pair_e, landed_e, E_TILE)
    grads = unpack_grads(total_e, total_l, w_sh)

    delta, new_m, new_v = {}, {}, {}
    for n in WEIGHT_NAMES:
        shp = w_sh[n].shape
        two_d = (-1, shp[-1])
        d_, m_, v_ = adamw("adamw_" + n, w_sh[n].reshape(two_d), grads[n].reshape(two_d), m_sh[n].reshape(two_d), v_sh[n].reshape(two_d))
        delta[n], new_m[n], new_v[n] = d_.reshape(shp), m_.reshape(shp), v_.reshape(shp)

    loss = lax.psum(loss_acc[0, 0], ("x", "y", "c"))
    return (loss, grad_x[None], *[grads[n] for n in WEIGHT_NAMES], *[delta[n] for n in WEIGHT_NAMES],
            *[new_m[n] for n in WEIGHT_NAMES], *[new_v[n] for n in WEIGHT_NAMES])
```

```python
import math

import jax
import jax.numpy as jnp
from jax import lax
from jax.experimental import pallas as pl
from jax.experimental.pallas import tpu as pltpu

f32, bf16 = jnp.float32, jnp.bfloat16
SDS = jax.ShapeDtypeStruct

D = 1024
EPS = 1e-6
CHUNK = 128
GM_HEADS, GM_HD = 4, 256
SSM_GROUPS, SSM_HPG, SSM_P, SSM_N = 4, 4, 64, 128
N_HEADS = SSM_GROUPS * SSM_HPG
CONV_K = 4
CONV_DIM = 2048
POOL_WINDOWS = (2, 4, 8, 16)
POOL_GD = 256
POOL_HALO = 32
CONV_HALO = 8
D_FF = 2816
DT_PAD = 128
IN_DIM = 5136

ADAM_LR, ADAM_B1, ADAM_B2, ADAM_EPS, ADAM_WD, ADAM_STEP = 0.001, 0.9, 0.999, 1e-08, 0.01, 10

NT = (((1,), (1,)), ((), ()))
TN = (((0,), (0,)), ((), ()))
NN = (((1,), (0,)), ((), ()))
HI = lax.Precision.HIGHEST
MM_SUB = 256


def _silu(x):
    return x * jax.nn.sigmoid(x)


def _softplus(x):
    return jnp.maximum(x, 0.0) + jnp.log1p(jnp.exp(-jnp.abs(x)))


def _rms(x, g):
    return x * lax.rsqrt(jnp.mean(x * x, axis=-1, keepdims=True) + EPS) * g


def _rms_bwd(x, g, dy):
    r = lax.rsqrt(jnp.mean(x * x, axis=-1, keepdims=True) + EPS)
    xh = x * r
    dxh = dy * g
    dx = r * (dxh - xh * jnp.mean(dxh * xh, axis=-1, keepdims=True))
    return dx, jnp.sum(dy * xh, axis=0, keepdims=True)


def _bdot(a, b, dims=NN):
    return lax.dot_general(a.astype(bf16), b.astype(bf16), dims, preferred_element_type=f32)


def matmul(name, pairs, mode, out_dtype, tm, tn, tk=None):
    a0, b0 = pairs[0]
    if mode == "tn":
        M, N, K = a0.shape[1], b0.shape[1], a0.shape[0]
    else:
        M, K = a0.shape
        N = b0.shape[1] if mode == "nn" else b0.shape[0]
    tm, tn = min(tm, M), min(tn, N)
    assert M % tm == 0 and N % tn == 0, (name, M, N, tm, tn)
    if tk is None:
        nk = 1
    else:
        assert len(pairs) == 1 and K % tk == 0
        nk = K // tk
    dims = {"nn": NN, "nt": NT, "tn": TN}[mode]
    in_specs, args = [], []
    for a, b in pairs:
        kk = (a.shape[0] if mode == "tn" else a.shape[1]) if tk is None else tk
        if mode == "tn":
            in_specs.append(pl.BlockSpec((kk, tm), lambda j, i, k: (k, i)))
            in_specs.append(pl.BlockSpec((kk, tn), lambda j, i, k: (k, j)))
        elif mode == "nn":
            in_specs.append(pl.BlockSpec((tm, kk), lambda j, i, k: (i, k)))
            in_specs.append(pl.BlockSpec((kk, tn), lambda j, i, k: (k, j)))
        else:
            in_specs.append(pl.BlockSpec((tm, kk), lambda j, i, k: (i, k)))
            in_specs.append(pl.BlockSpec((tn, kk), lambda j, i, k: (j, k)))
        args += [a, b]
    npairs = len(pairs)

    def kern(*refs):
        o = refs[2 * npairs]
        part = None
        for p in range(npairs):
            d = _bdot(refs[2 * p][...], refs[2 * p + 1][...], dims)
            part = d if part is None else part + d
        if nk == 1:
            o[...] = part.astype(out_dtype)
        else:
            acc = refs[2 * npairs + 1]
            k = pl.program_id(2)

            @pl.when(k == 0)
            def _():
                acc[...] = part

            @pl.when(k > 0)
            def _():
                acc[...] += part

            @pl.when(k == nk - 1)
            def _():
                o[...] = acc[...].astype(out_dtype)

    return pl.pallas_call(
        kern, name=name, grid=(N // tn, M // tm, nk),
        in_specs=in_specs, out_specs=pl.BlockSpec((tm, tn), lambda j, i, k: (i, j)),
        out_shape=SDS((M, N), out_dtype),
        scratch_shapes=[pltpu.VMEM((tm, tn), f32)] if nk > 1 else [],
        compiler_params=pltpu.CompilerParams(dimension_semantics=("parallel", "parallel", "arbitrary")),
    )(*args)


def mm(name, grid, pairs, dims, o_spec, out_shape):
    nk = grid[2]
    npairs = len(pairs)
    in_specs, args = [], []
    for a, a_spec, b, b_spec in pairs:
        in_specs += [a_spec, b_spec]
        args += [a, b]
    blk = tuple(d for d in o_spec.block_shape if d is not None)

    def kern(*refs):
        o = refs[2 * npairs]
        part = None
        for p in range(npairs):
            d = _bdot(refs[2 * p][...], refs[2 * p + 1][...], dims)
            part = d if part is None else part + d
        if nk == 1:
            o[...] = part.astype(o.dtype)
        else:
            acc = refs[2 * npairs + 1]
            k = pl.program_id(2)

            @pl.when(k == 0)
            def _():
                acc[...] = part

            @pl.when(k > 0)
            def _():
                acc[...] += part

            @pl.when(k == nk - 1)
            def _():
                o[...] = acc[...].astype(o.dtype)

    return pl.pallas_call(
        kern, name=name, grid=grid, in_specs=in_specs, out_specs=o_spec, out_shape=out_shape,
        scratch_shapes=[pltpu.VMEM(blk, f32)] if nk > 1 else [],
        compiler_params=pltpu.CompilerParams(dimension_semantics=("parallel", "parallel", "arbitrary")),
    )(*args)


def mm_fused(name, n_row_blocks, pairs, dims, extra_ins, outs, accs, epilogue):
    npairs, nx, no, na = len(pairs), len(extra_ins), len(outs), len(accs)
    in_specs, args = [], []
    for a, a_spec, b, b_spec in pairs:
        in_specs += [a_spec, b_spec]
        args += [a, b]
    for arr, spec in extra_ins:
        in_specs.append(spec)
        args.append(arr)

    rows_blk = outs[0][1].block_shape[0]
    sub = min(rows_blk, MM_SUB)

    def kern(*refs):
        x_refs = refs[2 * npairs:2 * npairs + nx]
        o_refs = refs[2 * npairs + nx:2 * npairs + nx + no]
        a_refs = refs[2 * npairs + nx + no:]
        if na:
            @pl.when(pl.program_id(0) == 0)
            def _():
                for a in a_refs:
                    a[...] = jnp.zeros(a.shape, f32)
        for r0 in range(0, rows_blk, sub):
            rows = pl.ds(r0, sub)
            part = None
            for p in range(npairs):
                d = _bdot(refs[2 * p][rows, :], refs[2 * p + 1][...], dims)
                part = d if part is None else part + d
            epilogue(part, [x.at[rows, :] if x.shape[0] == rows_blk else x for x in x_refs], [o.at[rows, :] for o in o_refs], a_refs)

    return pl.pallas_call(
        kern, name=name, grid=(n_row_blocks,), in_specs=in_specs,
        out_specs=[spec for _, spec in outs] + [pl.BlockSpec(tuple(s), lambda i, nd=len(s): (0,) * nd) for s in accs],
        out_shape=[s for s, _ in outs] + [SDS(tuple(s), f32) for s in accs],
        compiler_params=pltpu.CompilerParams(dimension_semantics=("arbitrary",)),
    )(*args)


FF_SH = D_FF // 4


def ffn_up(name, T, tm, n_bf, wg4, wu4, l):
    sub = min(tm, MM_SUB)

    def kern(n_ref, wg_ref, wu_ref, g_ref, u_ref, a_ref):
        for r0 in range(0, tm, sub):
            rows = pl.ds(r0, sub)
            n = n_ref[rows, :]
            g = jnp.dot(n, wg_ref[...], preferred_element_type=f32)
            u = jnp.dot(n, wu_ref[...], preferred_element_type=f32)
            g_ref[rows, :] = g.astype(bf16)
            u_ref[rows, :] = u.astype(bf16)
            a_ref[rows, :] = (_silu(g) * u).astype(bf16)
    w_spec = pl.BlockSpec((None, D, FF_SH), lambda k, i: (k, l, 0))
    o_spec = pl.BlockSpec((None, tm, FF_SH), lambda k, i: (k, i, 0))
    s = SDS((4, T, FF_SH), bf16)
    return pl.pallas_call(kern, name=name, grid=(4, T // tm), in_specs=[pl.BlockSpec((tm, D), lambda k, i: (i, 0)), w_spec, w_spec],
                          out_specs=[o_spec] * 3, out_shape=[s, s, s],
                          compiler_params=pltpu.CompilerParams(dimension_semantics=("parallel", "parallel")))(n_bf, wg4, wu4)


def ffn_dgu(name, T, tm, d_f, wd4, gate4, up4, l):
    rc = 16

    sub = min(tm, MM_SUB)

    def kern(df_ref, wd_ref, g_ref, u_ref, dg_ref, du_ref, dact_ref):
        for s0 in range(0, tm, sub):
            dact_ref[pl.ds(s0, sub), :] = _bdot(df_ref[pl.ds(s0, sub), :], wd_ref[...], NT)
            for r0 in range(s0, s0 + sub, rc):
                rows = pl.ds(r0, rc)
                _, vjp = jax.vjp(lambda a, b: _silu(a) * b, g_ref[rows, :].astype(f32), u_ref[rows, :].astype(f32))
                dg, du = vjp(dact_ref[rows, :])
                dg_ref[rows, :] = dg.astype(bf16)
                du_ref[rows, :] = du.astype(bf16)
    a_spec = pl.BlockSpec((None, tm, FF_SH), lambda k, i: (k, i, 0))
    s = SDS((4, T, FF_SH), bf16)
    return pl.pallas_call(kern, name=name, grid=(4, T // tm),
                          in_specs=[pl.BlockSpec((tm, D), lambda k, i: (i, 0)), pl.BlockSpec((None, FF_SH, D), lambda k, i: (k, l, 0)), a_spec, a_spec],
                          out_specs=[a_spec] * 2, out_shape=[s, s], scratch_shapes=[pltpu.VMEM((tm, FF_SH), f32)],
                          compiler_params=pltpu.CompilerParams(dimension_semantics=("parallel", "parallel")))(d_f, wd4, gate4, up4)


def rowcall(name, body, T, tm, ins, outs, accs=(), scratch=(), reverse=False, sub=None):
    n = T // tm
    assert T % tm == 0

    def blk(i):
        return (n - 1 - i) if reverse else i

    in_specs, args = [], []
    for spec in ins:
        kind, arr = spec[0], spec[1]
        if kind == "row":
            _, _, w, cb = spec
            in_specs.append(pl.BlockSpec((tm, w), lambda i, cb=cb: (blk(i), cb)))
        elif kind == "prev":
            _, _, w, cb, h = spec
            r = tm // h
            in_specs.append(pl.BlockSpec((h, w), lambda i, cb=cb, r=r: (jnp.maximum(blk(i) * r - 1, 0), cb)))
        elif kind == "next":
            _, _, w, cb, h = spec
            r = tm // h
            in_specs.append(pl.BlockSpec((h, w), lambda i, cb=cb, r=r, h=h: (jnp.minimum((blk(i) + 1) * r, T // h - 1), cb)))
        else:
            nd = arr.ndim
            in_specs.append(pl.BlockSpec(arr.shape, lambda i, nd=nd: (0,) * nd))
        args.append(arr)
    out_shape = [SDS((T, w), dt) for w, dt in outs] + [SDS(tuple(s), f32) for s in accs]
    out_specs = [pl.BlockSpec((tm, w), lambda i: (blk(i), 0)) for w, _ in outs]
    out_specs += [pl.BlockSpec(tuple(s), lambda i, nd=len(s): (0,) * nd) for s in accs]
    ni, no, na = len(ins), len(outs), len(accs)

    def kern(*refs):
        i = pl.program_id(0)
        in_refs, out_refs = refs[:ni], refs[ni:ni + no]
        acc_refs, scr = refs[ni + no:ni + no + na], refs[ni + no + na:]
        if na:
            @pl.when(i == 0)
            def _():
                for a in acc_refs:
                    a[...] = jnp.zeros(a.shape, f32)
        if sub is None or sub >= tm:
            body(blk(i), n, in_refs, out_refs, acc_refs, scr)
        else:
            for r0 in range(0, tm, sub):
                rows = pl.ds(r0, sub)
                body(blk(i), n, [r.at[rows, :] if spec[0] == "row" else r for r, spec in zip(in_refs, ins)],
                     [o.at[rows, :] for o in out_refs], acc_refs, [s.at[rows, :] for s in scr])

    res = pl.pallas_call(
        kern, name=name, grid=(n,), in_specs=in_specs, out_specs=out_specs, out_shape=out_shape,
        scratch_shapes=list(scratch),
        compiler_params=pltpu.CompilerParams(dimension_semantics=("arbitrary",)),
    )(*args)
    return res


def rms_to_bf16(name, T, tm, x, g):
    def body(i, n, ins, outs, accs, scr):
        outs[0][...] = _rms(ins[0][...], ins[1][...]).astype(bf16)
    return rowcall(name, body, T, tm, [("row", x, D, 0), ("const", g)], [(D, bf16)], sub=64)[0]


def _layer_norm_parts(x):
    mu = jnp.mean(x, axis=-1, keepdims=True)
    xc = x - mu
    r = lax.rsqrt(jnp.mean(xc * xc, axis=-1, keepdims=True) + EPS)
    return xc * r, r


def gmlp_fwd(name, T, tm, uvz, ln_g, ln_b, wm, bs):
    def body(i, n, ins, outs, accs, scr):
        gu = jax.nn.gelu(ins[0][...])
        xh, _ = _layer_norm_parts(jax.nn.gelu(ins[1][...]))
        vln = (xh * ins[2][...] + ins[3][...]).astype(bf16)
        for c in range(ins[0].shape[0] // CHUNK):
            rows = slice(c * CHUNK, (c + 1) * CHUNK)
            for h in range(GM_HEADS):
                cols = slice(h * GM_HD, (h + 1) * GM_HD)
                mixed = jnp.dot(ins[4][h], vln[rows, cols], preferred_element_type=f32) + ins[5][h]
                outs[0][rows, cols] = (gu[rows, cols] * mixed).astype(bf16)
    return rowcall(name, body, T, tm, [("row", uvz, D, 0), ("row", uvz, D, 1), ("const", ln_g), ("const", ln_b), ("const", wm), ("const", bs)],
                   [(D, bf16)], sub=CHUNK)[0]


def gmlp_bwd(name, T, tm, uvz, d_ya, ln_g, ln_b, wm, bs):
    def body(i, n, ins, outs, accs, scr):
        u, v, dya = ins[0][...], ins[1][...], ins[2][...]
        gu, gelu_u_vjp = jax.vjp(jax.nn.gelu, u)
        gv, gelu_v_vjp = jax.vjp(jax.nn.gelu, v)
        xh, r = _layer_norm_parts(gv)
        lng = ins[3][...]
        vln = (xh * lng + ins[4][...]).astype(bf16)
        rr = lax.broadcasted_iota(jnp.int32, (CHUNK, CHUNK), 0)
        cc = lax.broadcasted_iota(jnp.int32, (CHUNK, CHUNK), 1)
        causal = (rr >= cc).astype(f32)
        dvln_ref = scr[0]
        dgu_ref = scr[1]
        for c in range(ins[0].shape[0] // CHUNK):
            rows = slice(c * CHUNK, (c + 1) * CHUNK)
            for h in range(GM_HEADS):
                cols = slice(h * GM_HD, (h + 1) * GM_HD)
                w = ins[5][h]
                blk = vln[rows, cols]
                mixed = jnp.dot(w, blk, preferred_element_type=f32) + ins[6][h]
                dy = dya[rows, cols]
                dgu_ref[rows, cols] = dy * mixed
                dm = dy * gu[rows, cols]
                accs[3][h] += jnp.sum(dm, axis=1, keepdims=True)
                accs[2][h] += _bdot(dm, blk, NT) * causal
                dvln_ref[rows, cols] = _bdot(w, dm, TN)
        dvln = dvln_ref[...]
        accs[0][...] += jnp.sum(dvln * xh, axis=0, keepdims=True)
        accs[1][...] += jnp.sum(dvln, axis=0, keepdims=True)
        dxh = dvln * lng
        dgv = r * (dxh - jnp.mean(dxh, axis=-1, keepdims=True) - xh * jnp.mean(dxh * xh, axis=-1, keepdims=True))
        outs[0][...] = gelu_u_vjp(dgu_ref[...])[0].astype(bf16)
        outs[1][...] = gelu_v_vjp(dgv)[0].astype(bf16)
    return rowcall(name, body, T, tm,
                   [("row", uvz, D, 0), ("row", uvz, D, 1), ("row", d_ya, D, 0), ("const", ln_g), ("const", ln_b), ("const", wm), ("const", bs)],
                   [(D, bf16), (D, bf16)], accs=[(1, D), (1, D), (GM_HEADS, CHUNK, CHUNK), (GM_HEADS, CHUNK, 1)],
                   scratch=[pltpu.VMEM((tm, D), f32), pltpu.VMEM((tm, D), f32)], sub=CHUNK)


CONV_RC, CONV_LB = 32, 512


def _conv_fill(i, x_ref, halo_ref, scr, tm):
    scr[pl.ds(0, CONV_HALO), :] = jnp.where(i > 0, halo_ref[...], 0.0)
    scr[pl.ds(CONV_HALO, tm), :] = x_ref[...]


def _conv_taps(scr, r0, lanes):
    return [scr[pl.ds(r0 + CONV_HALO - (CONV_K - 1) + k, CONV_RC), lanes] for k in range(CONV_K)]


def conv_fwd(name, T, tm, xbc, conv_w, conv_b):
    def body(i, n, ins, outs, accs, scr):
        s = scr[0]
        _conv_fill(i, ins[0], ins[1], s, tm)
        for lb in range(CONV_DIM // CONV_LB):
            lanes = slice(lb * CONV_LB, (lb + 1) * CONV_LB)
            w, b = ins[2][:, lanes], ins[3][:, lanes]

            for r0 in range(0, tm, CONV_RC):
                taps = _conv_taps(s, r0, lanes)
                pre = b + sum(w[k:k + 1] * taps[k] for k in range(CONV_K))
                outs[0][pl.ds(r0, CONV_RC), lanes] = _silu(pre)
    return rowcall(name, body, T, tm, [("row", xbc, CONV_DIM, 0), ("prev", xbc, CONV_DIM, 0, CONV_HALO), ("const", conv_w), ("const", conv_b)],
                   [(CONV_DIM, f32)], scratch=[pltpu.VMEM((tm + CONV_HALO, CONV_DIM), f32)])[0]


def conv_bwd_pre(name, T, tm, xbc, d_xc, conv_w, conv_b):
    def body(i, n, ins, outs, accs, scr):
        s = scr[0]
        _conv_fill(i, ins[0], ins[1], s, tm)
        fold = lambda v: jnp.sum(v.reshape(CONV_RC // 8, 8, CONV_LB), axis=0)
        for lb in range(CONV_DIM // CONV_LB):
            lanes = slice(lb * CONV_LB, (lb + 1) * CONV_LB)
            w, b = ins[3][:, lanes], ins[4][:, lanes]

            sums = [jnp.zeros((8, CONV_LB), f32)] * (CONV_K + 1)
            for r0 in range(0, tm, CONV_RC):
                taps = _conv_taps(s, r0, lanes)
                pre = b + sum(w[k:k + 1] * taps[k] for k in range(CONV_K))
                _, vjp = jax.vjp(_silu, pre)
                dpre = vjp(ins[2][pl.ds(r0, CONV_RC), lanes])[0]
                outs[0][pl.ds(r0, CONV_RC), lanes] = dpre
                sums = [sums[k] + fold(dpre * taps[k]) for k in range(CONV_K)] + [sums[CONV_K] + fold(dpre)]
            for k in range(CONV_K):
                accs[0][pl.ds(k, 1), lanes] += jnp.sum(sums[k], axis=0, keepdims=True)
            accs[1][:, lanes] += jnp.sum(sums[CONV_K], axis=0, keepdims=True)
    return rowcall(name, body, T, tm,
                   [("row", xbc, CONV_DIM, 0), ("prev", xbc, CONV_DIM, 0, CONV_HALO), ("row", d_xc, CONV_DIM, 0), ("const", conv_w), ("const", conv_b)],
                   [(CONV_DIM, f32)], accs=[(CONV_K, CONV_DIM), (1, CONV_DIM)], scratch=[pltpu.VMEM((tm + CONV_HALO, CONV_DIM), f32)])


def conv_bwd_x(name, T, tm, d_pre, conv_w):
    def body(i, n, ins, outs, accs, scr):
        s = scr[0]
        s[pl.ds(0, tm), :] = ins[0][...]
        s[pl.ds(tm, CONV_HALO), :] = jnp.where(i < n - 1, ins[1][...], 0.0)
        for lb in range(CONV_DIM // CONV_LB):
            lanes = slice(lb * CONV_LB, (lb + 1) * CONV_LB)
            w = ins[2][:, lanes]

            for r0 in range(0, tm, CONV_RC):
                dx = sum(w[k:k + 1] * s[pl.ds(r0 + CONV_K - 1 - k, CONV_RC), lanes] for k in range(CONV_K))
                outs[0][pl.ds(r0, CONV_RC), lanes] = dx.astype(bf16)
    return rowcall(name, body, T, tm, [("row", d_pre, CONV_DIM, 0), ("next", d_pre, CONV_DIM, 0, CONV_HALO), ("const", conv_w)],
                   [(CONV_DIM, bf16)], scratch=[pltpu.VMEM((tm + CONV_HALO, CONV_DIM), f32)])[0]


def _ssd_prep(dtr, dtb, alog):
    rr = lax.broadcasted_iota(jnp.int32, (CHUNK, CHUNK), 0)
    cc = lax.broadcasted_iota(jnp.int32, (CHUNK, CHUNK), 1)
    dt = _softplus(dtr + dtb)
    dA = dt * -jnp.exp(alog)
    acum = jnp.dot((rr >= cc).astype(f32), dA, precision=HI, preferred_element_type=f32)
    return dt, acum, acum.T, jnp.sum(dA, axis=0, keepdims=True)


def _ssd_group(g, x, Bm, Cm, S, dt, acum, acumT, tot, dsk):
    rr = lax.broadcasted_iota(jnp.int32, (CHUNK, CHUNK), 0)
    cc = lax.broadcasted_iota(jnp.int32, (CHUNK, CHUNK), 1)
    tril = rr >= cc
    lane = lax.broadcasted_iota(jnp.int32, (1, DT_PAD), 1)
    sub = lax.broadcasted_iota(jnp.int32, (DT_PAD, 1), 0)
    glane = lax.broadcasted_iota(jnp.int32, (1, SSM_HPG * SSM_P), 1) // SSM_P
    hm = [(glane == r).astype(f32) for r in range(SSM_HPG)]
    pick = lambda v, r: jnp.sum(v * (lane == SSM_HPG * g + r).astype(f32), axis=1, keepdims=True)
    cols = [pick(acum, r) for r in range(SSM_HPG)]
    tots = [pick(tot, r) for r in range(SSM_HPG)]
    spread = lambda vals: sum(vals[r] * hm[r] for r in range(SSM_HPG))
    xdt = x * spread([pick(dt, r) for r in range(SSM_HPG)])
    cb = _bdot(Cm, Bm, NT)
    y = x * spread([pick(dsk, r) for r in range(SSM_HPG)])
    for r in range(SSM_HPG):
        row = jnp.sum(acumT * (sub == SSM_HPG * g + r).astype(f32), axis=0, keepdims=True)
        dec = jnp.exp(jnp.where(tril, cols[r] - row, -jnp.inf))
        y = y + _bdot(cb * dec, xdt * hm[r])
    y = y + _bdot(Cm, S) * spread([jnp.exp(c) for c in cols])
    dte = spread([jnp.exp(tots[r] - cols[r]) for r in range(SSM_HPG)])
    s_new = S * spread([jnp.exp(t) for t in tots]) + _bdot(Bm, xdt * dte, TN)
    return y, s_new


def _ssd_ins(xc, dtr):
    gw = SSM_HPG * SSM_P
    ins = [("row", xc, gw, g) for g in range(SSM_GROUPS)]
    ins += [("row", xc, SSM_N, D // SSM_N + g) for g in range(SSM_GROUPS)]
    ins += [("row", xc, SSM_N, D // SSM_N + SSM_GROUPS + g) for g in range(SSM_GROUPS)]
    ins += [("row", dtr, DT_PAD, 0)]
    return ins


SSD_CPS = 2


def ssd_fwd(name, T, xc, dtr, dtb, alog, dsk):
    gw = SSM_HPG * SSM_P

    def body(i, n, ins, outs, accs, scr):
        S = scr[0]

        @pl.when(i == 0)
        def _():
            S[...] = jnp.zeros(S.shape, f32)
        S4 = tuple(S[:, g * gw:(g + 1) * gw] for g in range(4))
        for c in range(SSD_CPS):
            rows = pl.ds(c * CHUNK, CHUNK)
            X4 = tuple(ins[g][rows, :] for g in range(4))
            B4 = tuple(ins[4 + g][rows, :] for g in range(4))
            C4 = tuple(ins[8 + g][rows, :] for g in range(4))
            prep = _ssd_prep(ins[12][rows, :], ins[13][...], ins[14][...])
            nxt = []
            for g in range(4):
                outs[1][rows, g * gw:(g + 1) * gw] = S4[g]
                y, s_new = _ssd_group(g, X4[g], B4[g], C4[g], S4[g], *prep, ins[15][...])
                outs[0][rows, g * gw:(g + 1) * gw] = y
                nxt.append(s_new)
            S4 = tuple(nxt)
        for g in range(4):
            S[:, g * gw:(g + 1) * gw] = S4[g]
    ins = _ssd_ins(xc, dtr) + [("const", dtb), ("const", alog), ("const", dsk)]
    return rowcall(name, body, T, SSD_CPS * CHUNK, ins, [(D, f32), (D, f32)], scratch=[pltpu.VMEM((SSM_N, D), f32)])


def ssd_bwd(name, T, xc, dtr, sprev, d_y, dtb, alog, dsk):
    gw = SSM_HPG * SSM_P

    def body(i, n, ins, outs, accs, scr):
        dS = scr[0]

        @pl.when(i == n - 1)
        def _():
            dS[...] = jnp.zeros(dS.shape, f32)
        dS4 = tuple(dS[:, g * gw:(g + 1) * gw] for g in range(4))
        def chunk(X4, dtr_c, B4, C4, S4, dtb_c, alog_c, dsk_c):
            prep = _ssd_prep(dtr_c, dtb_c, alog_c)
            res = [_ssd_group(g, X4[g], B4[g], C4[g], S4[g], *prep, dsk_c) for g in range(4)]
            return tuple(r[0] for r in res), tuple(r[1] for r in res)
        X4 = tuple(ins[g][...] for g in range(4))
        B4 = tuple(ins[4 + g][...] for g in range(4))
        C4 = tuple(ins[8 + g][...] for g in range(4))
        S4 = tuple(ins[13 + g][...] for g in range(4))
        dY4 = tuple(ins[17 + g][...] for g in range(4))
        _, vjp = jax.vjp(chunk, X4, ins[12][...], B4, C4, S4, ins[21][...], ins[22][...], ins[23][...])
        dX4, ddtr, dB4, dC4, dS4, ddtb, dalog, ddsk = vjp((dY4, dS4))
        for g in range(4):
            outs[0][:, g * gw:(g + 1) * gw] = dX4[g]
            outs[0][:, D + g * SSM_N:D + (g + 1) * SSM_N] = dB4[g]
            outs[0][:, D + (SSM_GROUPS + g) * SSM_N:D + (SSM_GROUPS + g + 1) * SSM_N] = dC4[g]
            dS[:, g * gw:(g + 1) * gw] = dS4[g]
        outs[1][...] = ddtr.astype(bf16)
        accs[0][...] += ddtb
        accs[1][...] += dalog
        accs[2][...] += ddsk
    ins = _ssd_ins(xc, dtr) + [("row", sprev, gw, g) for g in range(4)] + [("row", d_y, gw, g) for g in range(4)]
    ins += [("const", dtb), ("const", alog), ("const", dsk)]
    return rowcall(name, body, T, CHUNK, ins, [(CONV_DIM, f32), (DT_PAD, bf16)], accs=[(1, DT_PAD)] * 3,
                   scratch=[pltpu.VMEM((SSM_N, D), f32)], reverse=True)


def _gate_group(y, z, g):
    return _rms(y * _silu(z), g)


def gate_fwd(name, T, tm, y, uvz, gn):
    def body(i, n, ins, outs, accs, scr):
        for g in range(SSM_GROUPS):
            cols = slice(g * 256, (g + 1) * 256)
            outs[0][:, cols] = _gate_group(ins[0][:, cols], ins[1][:, cols], ins[2][:, cols]).astype(bf16)
    return rowcall(name, body, T, tm, [("row", y, D, 0), ("row", uvz, D, 2), ("const", gn)], [(D, bf16)], sub=64)[0]


def gate_bwd(name, T, tm, y, uvz, d_yb, gn):
    def body(i, n, ins, outs, accs, scr):
        for g in range(SSM_GROUPS):
            cols = slice(g * 256, (g + 1) * 256)
            _, vjp = jax.vjp(_gate_group, ins[0][:, cols], ins[1][:, cols], ins[3][:, cols])
            dy, dz, dg = vjp(ins[2][:, cols])
            outs[0][:, cols] = dy
            outs[1][:, cols] = dz.astype(bf16)
            accs[0][:, cols] += dg
    return rowcall(name, body, T, tm, [("row", y, D, 0), ("row", uvz, D, 2), ("row", d_yb, D, 0), ("const", gn)],
                   [(D, f32), (D, bf16)], accs=[(1, D)], sub=64)


def _window_sum(src, cols, levels, tm, lv, trailing):
    cur, cur_cols = src, cols
    for l in range(1, levels + 1):
        shift = 2 ** (l - 1)
        last = l == levels
        if trailing:
            start = POOL_HALO if last else 8 * l
            rows = tm if last else tm + POOL_HALO - start
            new = cur[pl.ds(start, rows), cur_cols] + cur[pl.ds(start - shift, rows), cur_cols]
        else:
            start = 0
            rows = tm if last else tm + POOL_HALO - 8 * l
            new = cur[pl.ds(0, rows), cur_cols] + cur[pl.ds(shift, rows), cur_cols]
        if last:
            return new
        nxt = lv[l % 2]
        nxt[pl.ds(start, rows), :] = new
        cur, cur_cols = nxt, slice(None)


def _pool_diff(i, tm, h_ref, halo_ref, g_ref, scr, lv):
    g = g_ref[...]
    yn = _rms(h_ref[...], g)
    scr[pl.ds(0, POOL_HALO), :] = jnp.where(i > 0, _rms(halo_ref[...], g), 0.0)
    scr[pl.ds(POOL_HALO, tm), :] = yn
    pos = (i * tm + lax.broadcasted_iota(jnp.int32, (tm, 1), 0) + 1).astype(f32)
    parts = []
    for gi, win in enumerate(POOL_WINDOWS):
        cols = slice(gi * POOL_GD, (gi + 1) * POOL_GD)
        s = _window_sum(scr, cols, gi + 1, tm, lv, True)
        parts.append(s * (1.0 / jnp.minimum(pos, float(win))) - yn[:, cols])
    return parts


def pool_fwd(name, T, tm, h2, g_pre, pw, pb, psc, g_post, g_next):
    def body(i, n, ins, outs, accs, scr):
        parts = _pool_diff(i, tm, ins[0], ins[1], ins[2], scr[0], scr[1:3])
        for gi in range(len(POOL_WINDOWS)):
            cols = slice(gi * POOL_GD, (gi + 1) * POOL_GD)
            o = _bdot(parts[gi], ins[3][gi]) + ins[4][:, cols]
            outs[0][:, cols] = o * ins[5][:, cols]
        h = ins[0][...] + _rms(outs[0][...], ins[6][...])
        outs[1][...] = h
        outs[2][...] = _rms(h, ins[7][...]).astype(bf16)
    return rowcall(name, body, T, tm, [("row", h2, D, 0), ("prev", h2, D, 0, POOL_HALO), ("const", g_pre), ("const", pw), ("const", pb), ("const", psc),
                                       ("const", g_post), ("const", g_next)],
                   [(D, f32), (D, f32), (D, bf16)], scratch=[pltpu.VMEM((tm + POOL_HALO, D), f32)] + [pltpu.VMEM((tm + POOL_HALO, POOL_GD), f32)] * 2)


def pool_bwd(name, T, tm, h2, d_pm, d_res, g_pre, pw, pb, psc, f_prev, g_prev):
    def body(i, n, ins, outs, accs, scr):
        parts = _pool_diff(i, tm, ins[0], ins[1], ins[5], scr[0], scr[3:5])
        dpm = ins[2][...]
        psc_v = ins[8][...]
        dps = dpm * psc_v
        dps_halo = jnp.where(i < n - 1, ins[3][...] * psc_v, 0.0)
        accs[1][...] += jnp.sum(dps, axis=0, keepdims=True)
        pos = (i * tm + lax.broadcasted_iota(jnp.int32, (tm, 1), 0) + 1).astype(f32)
        pos_h = ((i + 1) * tm + lax.broadcasted_iota(jnp.int32, (POOL_HALO, 1), 0) + 1).astype(f32)
        r_scr = scr[1]
        dyn_scr = scr[2]
        for gi, win in enumerate(POOL_WINDOWS):
            cols = slice(gi * POOL_GD, (gi + 1) * POOL_GD)
            w = ins[6][gi]
            o = _bdot(parts[gi], w) + ins[7][:, cols]
            accs[2][:, cols] += jnp.sum(dpm[:, cols] * o, axis=0, keepdims=True)
            accs[0][gi] += _bdot(parts[gi], dps[:, cols], TN)
            q = _bdot(dps[:, cols], w, NT)
            qh = _bdot(dps_halo[:, cols], w, NT)
            r_scr[pl.ds(0, tm), cols] = q * (1.0 / jnp.minimum(pos, float(win)))
            r_scr[pl.ds(tm, POOL_HALO), cols] = qh * (1.0 / jnp.minimum(pos_h, float(win)))
            dyn_scr[:, cols] = _window_sum(r_scr, cols, gi + 1, tm, scr[3:5], False) - q
        dx, dg = _rms_bwd(ins[0][...], ins[5][...], dyn_scr[...])
        dh = ins[4][...] + dx
        outs[0][...] = dh
        accs[3][...] += dg
        df, dgp = _rms_bwd(ins[9][...], ins[10][...], dh)
        outs[1][...] = df.astype(bf16)
        accs[4][...] += dgp
    ins = [("row", h2, D, 0), ("prev", h2, D, 0, POOL_HALO), ("row", d_pm, D, 0), ("next", d_pm, D, 0, POOL_HALO), ("row", d_res, D, 0),
           ("const", g_pre), ("const", pw), ("const", pb), ("const", psc), ("row", f_prev, D, 0), ("const", g_prev)]
    return rowcall(name, body, T, tm, ins, [(D, f32), (D, bf16)], accs=[(4, POOL_GD, POOL_GD), (1, D), (1, D), (1, D), (1, D)],
                   scratch=[pltpu.VMEM((tm + POOL_HALO, D), f32), pltpu.VMEM((tm + POOL_HALO, D), f32), pltpu.VMEM((tm, D), f32)]
                   + [pltpu.VMEM((tm + POOL_HALO, POOL_GD), f32)] * 2)


def local_step(T, x, tgt, W, ffn_weights, early_grads):
    tm = 512 if T >= 1024 else T // 2
    TKW = 4096 if T >= 4096 else T
    ng = W["norm_g"]
    g = lambda l, j: ng[l, j][None, :]
    G = {}

    row_spec = pl.BlockSpec((tm, D), lambda j, i, k: (i, 0))
    tf = tm // 2 if T >= 1024 else tm
    rows_f = pl.BlockSpec((tf, D), lambda i: (i, 0))
    vec_f = pl.BlockSpec((1, D), lambda i: (0, 0))
    sh_f = [pl.BlockSpec((None, tf, FF_SH), lambda i, s=s: (s, i, 0)) for s in range(4)]
    sh_spec = [pl.BlockSpec((None, tm, FF_SH), lambda j, i, k, s=s: (s, i, 0)) for s in range(4)]
    out_f32, out_bf16 = (SDS((T, D), f32), rows_f), (SDS((T, D), bf16), rows_f)

    def resid_epilogue(with_pre):
        def ep(part, xs, os, accs):
            h = xs[0][...] + _rms(part, xs[1][...])
            os[0][...] = part
            os[1][...] = h
            if with_pre:
                os[2][...] = _rms(h, xs[2][...]).astype(bf16)
        return ep

    def bwd_epilogue(df_dtype):
        def ep(part, xs, os, accs):
            dx, dgp = _rms_bwd(xs[0][...], xs[3][...], part)
            dh = xs[2][...] + dx
            df, dgq = _rms_bwd(xs[1][...], xs[4][...], dh)
            os[0][...] = dh
            os[1][...] = df.astype(df_dtype)
            accs[0][...] += dgp
            accs[1][...] += dgq
        return ep

    def loss_epilogue(part, xs, os, accs):
        g_post = xs[2][...]
        e = xs[0][...] + _rms(part, g_post) - xs[1][...]
        accs[0][...] += jnp.sum(jnp.sum(e * e, axis=-1, keepdims=True) * (0.5 / D), axis=0, keepdims=True)
        dh = e * (1.0 / D)
        df, dg = _rms_bwd(part, g_post, dh)
        os[0][...] = dh
        os[1][...] = df.astype(bf16)
        accs[1][...] += dg

    def ffn_fwd(tag, n_bf, l, resid=None, loss=None):
        gate4, up4, act4 = ffn_up(f"ffn{tag}_up", T, min(T, 2 * tm), n_bf, W["wg4"], W["wu4"], l)
        wd_f = [pl.BlockSpec((None, FF_SH, D), lambda i, s=s: (s, l, 0)) for s in range(4)]
        pairs = [(act4, sh_f[s], W["wd4"], wd_f[s]) for s in range(4)]
        if loss is not None:
            return (gate4, up4, act4) + tuple(mm_fused(f"ffn{tag}_down", T // tf, pairs, NN, [(loss[0], rows_f), (loss[1], rows_f), (loss[2], vec_f)],
                                                       [out_f32, out_bf16], [(1, 1), (1, D)], loss_epilogue))
        f, h_out = mm_fused(f"ffn{tag}_down", T // tf, pairs, NN, [(resid[0], rows_f), (resid[1], vec_f)], [out_f32, out_f32], [],
                            resid_epilogue(False))
        return gate4, up4, act4, f, h_out

    def ffn_bwd(tag, l, n_bf, gate4, up4, act4, d_f, h_out, f_pre, d_res, g_pre, g_post, df_dtype):
        d_gate4, d_up4 = ffn_dgu(f"ffn{tag}_dgu", T, tm, d_f, W["wd4"], gate4, up4, l)
        w_f = [pl.BlockSpec((None, D, FF_SH), lambda i, s=s: (s, l, 0)) for s in range(4)]
        d_h, d_fp, dgp, dgq = mm_fused(
            f"ffn{tag}_dn", T // tf, [(d_gate4, sh_f[s], W["wg4"], w_f[s]) for s in range(4)] + [(d_up4, sh_f[s], W["wu4"], w_f[s]) for s in range(4)],
            NT, [(h_out, rows_f), (f_pre, rows_f), (d_res, rows_f), (g_pre, vec_f), (g_post, vec_f)],
            [out_f32, (SDS((T, D), df_dtype), rows_f)], [(1, D), (1, D)], bwd_epilogue(df_dtype))

        def wgrad(nm, a4, b):
            return mm(nm, (4, 1, T // TKW),
                      [(a4, pl.BlockSpec((None, TKW, FF_SH), lambda s, j, k: (s, k, 0)), b, pl.BlockSpec((TKW, D), lambda s, j, k: (k, 0)))],
                      TN, pl.BlockSpec((None, FF_SH, D), lambda s, j, k: (s, 0, 0)), SDS((4, FF_SH, D), f32))
        return d_h, d_fp, dgp, dgq, wgrad(f"ffn{tag}_dwg", d_gate4, n_bf), wgrad(f"ffn{tag}_dwu", d_up4, n_bf), wgrad(f"ffn{tag}_dwd", act4, d_f)

    y0 = rms_to_bf16("l0_prenorm", T, tm, x, g(0, 0))
    uvz = matmul("in_uvz", [(y0, W["w_uvz"])], "nn", f32, tm, 1024)
    xbc = matmul("in_xbc", [(y0, W["w_xbc"])], "nn", f32, tm, 1024)
    dtr = matmul("in_dt", [(y0, W["w_dt"])], "nn", f32, tm, DT_PAD)
    y_a = gmlp_fwd("gmlp_fwd", T, tm, uvz, W["ln_g"], W["ln_b"], W["wm"], W["bs"])
    xc = conv_fwd("conv_fwd", T, tm, xbc, W["conv_w"], W["conv_b"])
    y_ssd, sprev = ssd_fwd("ssd_fwd", T, xc, dtr, W["dtb"], W["alog"], W["dsk"])
    y_b = gate_fwd("gate_fwd", T, tm, y_ssd, uvz, W["gn"])
    half = D // 2
    wo4 = W["wo4"]
    ycol = [pl.BlockSpec((tf, half), lambda i, cb=cb: (i, cb)) for cb in range(2)]
    wo_s = [pl.BlockSpec((None, half, D), lambda i, s=s: (s, 0, 0)) for s in range(4)]
    mixo, h1, n1 = mm_fused("out_proj", T // tf, [(y_a, ycol[0], wo4, wo_s[0]), (y_a, ycol[1], wo4, wo_s[1]),
                                                  (y_b, ycol[0], wo4, wo_s[2]), (y_b, ycol[1], wo4, wo_s[3])], NN,
                            [(x, rows_f), (g(0, 1), vec_f), (g(0, 2), vec_f)], [out_f32, out_f32, out_bf16], [], resid_epilogue(True))
    W = dict(W)
    W["wg4"], W["wu4"], W["wd4"] = ffn_weights(h1)
    gate0, up0, act0, f1, h2 = ffn_fwd("0", n1, 0, resid=(h1, g(0, 3)))
    pm, h3, n3 = pool_fwd("pool_fwd", T, tm, h2, g(1, 0), W["pool_w"], W["pool_b"], W["pool_scale"], g(1, 1), g(1, 2))
    gate1, up1, act1, dh4, d_f2, loss_acc, dg13 = ffn_fwd("1", n3, 1, loss=(h3, tgt, g(1, 3)))
    d_h3, d_pm, dg12, dg11, dwg1, dwu1, dwd1 = ffn_bwd("1", 1, n3, gate1, up1, act1, d_f2, h3, pm, dh4, g(1, 2), g(1, 1), f32)
    d_h2, d_f1, G["pool_w"], G["pool_b"], G["pool_scale"], dg10, dg03 = pool_bwd("pool_bwd", T, tm, h2, d_pm, d_h3, g(1, 0), W["pool_w"], W["pool_b"],
                                                                                 W["pool_scale"], f1, g(0, 3))
    d_h1, d_mixo, dg02, dg01, dwg0, dwu0, dwd0 = ffn_bwd("0", 0, n1, gate0, up0, act0, d_f1, h1, mixo, d_h2, g(0, 2), g(0, 1), bf16)
    def d_ycat(nm, s0):
        return mm(nm, (2, T // tm, 1), [(d_mixo, row_spec, wo4, pl.BlockSpec((None, half, D), lambda j, i, k: (s0 + j, 0, 0)))], NT,
                  pl.BlockSpec((tm, half), lambda j, i, k: (i, j)), SDS((T, D), f32))

    def d_wo(nm, y):
        return mm(nm, (2, 1, T // TKW), [(y, pl.BlockSpec((TKW, half), lambda s, j, k: (k, s)), d_mixo, pl.BlockSpec((TKW, D), lambda s, j, k: (k, 0)))],
                  TN, pl.BlockSpec((None, half, D), lambda s, j, k: (s, 0, 0)), SDS((2, half, D), f32))
    d_ya, d_yb = d_ycat("out_proj_dya", 0), d_ycat("out_proj_dyb", 2)
    dwo_a, dwo_b = d_wo("out_proj_dwa", y_a), d_wo("out_proj_dwb", y_b)
    G["wo4"] = [dwo_a[0], dwo_a[1], dwo_b[0], dwo_b[1]]
    G["wgT4"], G["wuT4"], G["wd4"] = [dwg0, dwg1], [dwu0, dwu1], [dwd0, dwd1]
    token = early_grads(G)
    d_yssd, d_z, G["gn"] = gate_bwd("gate_bwd", T, tm, y_ssd, uvz, d_yb, W["gn"] + token[0, 0])
    d_xc, d_dtr, G["dtb"], G["alog"], G["dsk"] = ssd_bwd("ssd_bwd", T, xc, dtr, sprev, d_yssd, W["dtb"], W["alog"], W["dsk"])
    d_pre, G["conv_w"], G["conv_b"] = conv_bwd_pre("conv_bwd_pre", T, tm, xbc, d_xc, W["conv_w"], W["conv_b"])
    d_xbc = conv_bwd_x("conv_bwd_x", T, tm, d_pre, W["conv_w"])
    d_u, d_v, G["ln_g"], G["ln_b"], G["wm"], G["bs"] = gmlp_bwd("gmlp_bwd", T, tm, uvz, d_ya, W["ln_g"], W["ln_b"], W["wm"], W["bs"])
    w_u, w_v, w_z = W["w_uvz"][:, :D], W["w_uvz"][:, D:2 * D], W["w_uvz"][:, 2 * D:]
    def pre_epilogue(part, xs, os, accs):
        dx, dg = _rms_bwd(xs[0][...], xs[2][...], part)
        os[0][...] = xs[1][...] + dx
        accs[0][...] += dg
    blk = lambda w: pl.BlockSpec((tf, w), lambda i: (i, 0))
    whole = lambda a: pl.BlockSpec(a.shape, lambda i: (0, 0))
    grad_x, dg00 = mm_fused("in_dy0", T // tf, [(d_u, blk(D), w_u, whole(w_u)), (d_v, blk(D), w_v, whole(w_v)), (d_z, blk(D), w_z, whole(w_z)),
                                                (d_xbc, blk(CONV_DIM), W["w_xbc"], whole(W["w_xbc"])), (d_dtr, blk(DT_PAD), W["w_dt"], whole(W["w_dt"]))],
                            NT, [(x, rows_f), (d_h1, rows_f), (g(0, 0), vec_f)], [out_f32], [(1, D)], pre_epilogue)
    G["w_inT"] = [matmul("in_dwu", [(d_u, y0)], "tn", f32, 1024, 1024, TKW), matmul("in_dwv", [(d_v, y0)], "tn", f32, 1024, 1024, TKW),
                  matmul("in_dwz", [(d_z, y0)], "tn", f32, 1024, 1024, TKW), matmul("in_dwxbc", [(d_xbc, y0)], "tn", f32, 1024, 1024, TKW),
                  matmul("in_dwdt", [(d_dtr, y0)], "tn", f32, DT_PAD, 1024, TKW)[:N_HEADS]]
    G["norm_g"] = jnp.stack([jnp.concatenate([dg00, dg01, dg02, dg03], 0), jnp.concatenate([dg10, dg11, dg12, dg13], 0)])
    return loss_acc, grad_x, G


def build_weights(Wf):
    causal = jnp.tril(jnp.ones((CHUNK, CHUNK), bool))
    w_in = Wf["w_in"].astype(bf16)
    pad16 = lambda v: jnp.pad(v.reshape(1, N_HEADS).astype(f32), ((0, 0), (0, DT_PAD - N_HEADS)))
    return {
        "norm_g": Wf["norm_g"],
        "w_uvz": w_in[:, :3 * D], "w_xbc": w_in[:, 3 * D:3 * D + CONV_DIM],
        "w_dt": jnp.pad(w_in[:, 3 * D + CONV_DIM:], ((0, 0), (0, DT_PAD - N_HEADS))),
        "ln_g": Wf["gm_ln_g"].reshape(1, D), "ln_b": Wf["gm_ln_b"].reshape(1, D),
        "wm": jnp.where(causal[None], Wf["gm_ws"], 0).astype(bf16), "bs": Wf["gm_bs"].reshape(GM_HEADS, CHUNK, 1),
        "conv_w": Wf["conv_w"], "conv_b": Wf["conv_b"].reshape(1, CONV_DIM),
        "dtb": pad16(Wf["dt_bias"]), "alog": pad16(Wf["a_log"]), "dsk": pad16(Wf["d_skip"]),
        "gn": Wf["ssm_norm_g"].reshape(1, D),
        "wo4": Wf["wo4"].astype(bf16),
        "pool_w": Wf["pool_w"].astype(bf16), "pool_b": Wf["pool_b"].reshape(1, D), "pool_scale": Wf["pool_scale"].reshape(1, D),
    }


def small_grads(G):
    return {
        "norm_g": G["norm_g"],
        "gm_ln_g": G["ln_g"].reshape(D), "gm_ln_b": G["ln_b"].reshape(D),
        "gm_ws": G["wm"], "gm_bs": G["bs"].reshape(GM_HEADS, CHUNK),
        "conv_w": G["conv_w"], "conv_b": G["conv_b"].reshape(CONV_DIM),
        "dt_bias": G["dtb"][0, :N_HEADS], "a_log": G["alog"][0, :N_HEADS], "d_skip": G["dsk"][0, :N_HEADS],
        "ssm_norm_g": G["gn"].reshape(D),
        "pool_b": G["pool_b"].reshape(4, POOL_GD), "pool_scale": G["pool_scale"].reshape(D),
    }


MESH_ID = pl.DeviceIdType.MESH
ANY = pl.BlockSpec(memory_space=pl.ANY)


DMA_CHUNK_BYTES = 2 << 20
DMA_MAX_CHUNKS = 32


def _pieces(view, axis, align):
    shape = view.shape
    nbytes = math.prod(shape) * jnp.dtype(view.dtype).itemsize
    n = max(1, min(DMA_MAX_CHUNKS, -(-nbytes // DMA_CHUNK_BYTES)))
    rows = shape[axis]
    size = -(-rows // n)
    size = -(-size // align) * align
    out = []
    for s in range(0, rows, size):
        idx = [slice(None)] * len(shape)
        idx[axis] = pl.ds(s, min(size, rows - s))
        out.append(tuple(idx))
    return out


def comm_call(name, operands, out_shapes, plan):
    n_in = len(operands)
    n_out = len(out_shapes)
    n_remote, n_local = plan((0, 0, 0), [None] * n_in, [None] * n_out, True)

    def body(*refs):
        in_refs, out_refs = refs[:n_in], refs[n_in:n_in + n_out]
        send_sems, recv_sems, local_sems = refs[n_in + n_out:]
        me = (lax.axis_index("x"), lax.axis_index("y"), lax.axis_index("c"))
        remote, local = plan(me, in_refs, out_refs, False)
        align = lambda v: 16 if v.dtype == bf16 else 8
        for j, (s, d, axis) in enumerate(local):
            for ix in _pieces(s, axis, align(s)):
                pltpu.make_async_copy(s.at[ix], d.at[ix], local_sems.at[j]).start()
        peers = [tuple((1 - m) if f else m for m, f in zip(me, flip)) for flip, *_ in remote]
        for k, (flip, src, dst, _, axis) in enumerate(remote):
            for ix in _pieces(src, axis, align(src)):
                pltpu.make_async_remote_copy(src_ref=src.at[ix], dst_ref=dst.at[ix], send_sem=send_sems.at[k], recv_sem=recv_sems.at[k],
                                             device_id=peers[k], device_id_type=MESH_ID).start()
        for k, (flip, src, dst, landing, axis) in enumerate(remote):
            pltpu.make_async_remote_copy(src_ref=landing, dst_ref=landing, send_sem=send_sems.at[k], recv_sem=recv_sems.at[k],
                                         device_id=peers[k], device_id_type=MESH_ID).wait_recv()
        for k, (flip, src, dst, landing, axis) in enumerate(remote):
            pltpu.make_async_remote_copy(src_ref=src, dst_ref=dst, send_sem=send_sems.at[k], recv_sem=recv_sems.at[k],
                                         device_id=peers[k], device_id_type=MESH_ID).wait_send()
        for j, (s, d, axis) in enumerate(local):
            pltpu.make_async_copy(s, d, local_sems.at[j]).wait()

    return pl.pallas_call(
        body, name=name, out_shape=list(out_shapes), in_specs=[ANY] * n_in, out_specs=[ANY] * n_out,
        scratch_shapes=[pltpu.SemaphoreType.DMA((n_remote,)), pltpu.SemaphoreType.DMA((n_remote,)), pltpu.SemaphoreType.DMA((max(n_local, 1),))],
    )(*operands)


CHIP_FLIPS = ((1, 0, 0), (0, 1, 0), (1, 1, 0))
PAIR_FLIP = (0, 0, 1)


def gather_two_level(name, halved, whole):
    nh, nw = len(halved), len(whole)
    nf = len(CHIP_FLIPS)

    def body(*refs):
        srcs, outs = refs[:nh + nw], refs[nh + nw:2 * (nh + nw)]
        send_sems, recv_sems, fwd_send, fwd_recv = refs[2 * (nh + nw):]
        me = (lax.axis_index("x"), lax.axis_index("y"), lax.axis_index("c"))
        k, c = 2 * me[0] + me[1], me[2]
        sibling = (me[0], me[1], 1 - c)
        peers = [tuple((1 - m) if fl else m for m, fl in zip(me, flip)) for flip in CHIP_FLIPS]

        def half(ref, which):
            rh = ref.shape[0] // 2
            return ref.at[pl.ds(pl.multiple_of(which * rh, 16), rh), :]

        def ici(a, f):
            src = half(srcs[a], c) if a < nh else srcs[a]
            dst = half(outs[a].at[k], c) if a < nh else outs[a].at[k]
            return pltpu.make_async_remote_copy(src_ref=src, dst_ref=dst, send_sem=send_sems.at[a * nf + f], recv_sem=recv_sems.at[a * nf + f],
                                                device_id=peers[f], device_id_type=MESH_ID)

        def landed(a, f):
            slot = outs[a].at[_chip_of(me, CHIP_FLIPS[f])]
            return half(slot, c) if a < nh else slot

        def forward(a, f, which):
            v = half(outs[a].at[_chip_of(me, CHIP_FLIPS[f])], which)
            return pltpu.make_async_remote_copy(src_ref=v, dst_ref=v, send_sem=fwd_send.at[a * nf + f], recv_sem=fwd_recv.at[a * nf + f],
                                                device_id=sibling, device_id_type=MESH_ID)

        copies = [ici(a, f) for a in range(nh + nw) for f in range(nf)]
        for cp in copies:
            cp.start()
        fwds = []
        for a in range(nh):
            for f in range(nf):
                lv = landed(a, f)
                pltpu.make_async_remote_copy(src_ref=lv, dst_ref=lv, send_sem=send_sems.at[a * nf + f], recv_sem=recv_sems.at[a * nf + f],
                                             device_id=peers[f], device_id_type=MESH_ID).wait_recv()
                fw = forward(a, f, c)
                fw.start()
                fwds.append(fw)
        for a in range(nh, nh + nw):
            for f in range(nf):
                lv = landed(a, f)
                pltpu.make_async_remote_copy(src_ref=lv, dst_ref=lv, send_sem=send_sems.at[a * nf + f], recv_sem=recv_sems.at[a * nf + f],
                                             device_id=peers[f], device_id_type=MESH_ID).wait_recv()
        for a in range(nh):
            for f in range(nf):
                forward(a, f, 1 - c).wait_recv()
        for fw in fwds:
            fw.wait_send()
        for cp in copies:
            cp.wait_send()

    arrs = list(halved) + list(whole)
    n_ici = (nh + nw) * nf
    return pl.pallas_call(
        body, name=name, out_shape=[SDS((N_CHIPS,) + a.shape, a.dtype) for a in arrs], in_specs=[ANY] * len(arrs), out_specs=[ANY] * len(arrs),
        scratch_shapes=[pltpu.SemaphoreType.DMA((n_ici,)), pltpu.SemaphoreType.DMA((n_ici,)),
                        pltpu.SemaphoreType.DMA((nh * nf,)), pltpu.SemaphoreType.DMA((nh * nf,))],
    )(*arrs)


def pair_split_exchange(name, p, rh):
    def plan(me, ins, outs, count):
        if count:
            return 1, 0
        theirs = ins[0].at[:, pl.ds(pl.multiple_of((1 - me[2]) * rh, 8), rh), :]
        return [(PAIR_FLIP, theirs, outs[0], outs[0], 1)], []
    return comm_call(name, [p], [SDS((4, rh, p.shape[2]), p.dtype)], plan)[0]


def scatter_over_chips(name, cs):
    def plan(me, ins, outs, count):
        if count:
            return len(CHIP_FLIPS), 0
        k = 2 * me[0] + me[1]
        remote = []
        for flip in CHIP_FLIPS:
            kp = 2 * ((1 - me[0]) if flip[0] else me[0]) + ((1 - me[1]) if flip[1] else me[1])
            remote.append((flip, ins[0].at[kp], outs[0].at[k], outs[0].at[kp], 0))
        return remote, []
    return comm_call(name, [cs], [SDS(cs.shape, cs.dtype)], plan)[0]


def pair_swap(name, half):
    def plan(me, ins, outs, count):
        if count:
            return 1, 0
        return [(PAIR_FLIP, ins[0], outs[0], outs[0], 0)], []
    return comm_call(name, [half], [SDS(half.shape, half.dtype)], plan)[0]


def _row_tile(rows, cap=512):
    if rows <= cap:
        return rows
    t = cap - cap % 8
    while rows % t:
        t -= 8
    return t


def pair_sum(name, packs, got, c_arr, tile):
    rh = got.shape[1]
    nb = rh // tile

    def kern(c_ref, a_ref, b_ref, o16_ref):
        o16_ref[...] = (a_ref[...] + b_ref[...]).astype(bf16)
    blk = (None, tile, D)
    grid_spec = pltpu.PrefetchScalarGridSpec(
        num_scalar_prefetch=1, grid=(4, nb),
        in_specs=[pl.BlockSpec(blk, lambda s, i, c: (s, c[0] * nb + i, 0)), pl.BlockSpec(blk, lambda s, i, c: (s, i, 0))],
        out_specs=pl.BlockSpec(blk, lambda s, i, c: (s, i, 0)))
    return pl.pallas_call(kern, name=name, grid_spec=grid_spec, out_shape=SDS(got.shape, bf16),
                          compiler_params=pltpu.CompilerParams(dimension_semantics=("parallel", "parallel")))(c_arr, packs, got)


def chip_sum(name, own16, landed16, k_arr, tile):
    rh = own16.shape[1]
    nb = rh // tile

    def kern(k_ref, own_ref, l0, l1, l2, l3, o_ref):
        k = k_ref[0]
        s = None
        for j, lref in enumerate((l0, l1, l2, l3)):
            t = jnp.where(k == j, own_ref[...], lref[...]).astype(f32)
            s = t if s is None else s + t
        o_ref[...] = s
    blk = (None, tile, D)
    land = [pl.BlockSpec(blk, lambda i, k, j=j: (jnp.where(k[0] == j, (j + 1) % N_CHIPS, j), i, 0)) for j in range(N_CHIPS)]
    grid_spec = pltpu.PrefetchScalarGridSpec(
        num_scalar_prefetch=1, grid=(nb,),
        in_specs=[pl.BlockSpec(blk, lambda i, k: (k[0], i, 0))] + land,
        out_specs=pl.BlockSpec((tile, D), lambda i, k: (i, 0)))
    return pl.pallas_call(kern, name=name, grid_spec=grid_spec, out_shape=SDS((rh, D), f32),
                          compiler_params=pltpu.CompilerParams(dimension_semantics=("parallel",)))(k_arr, own16, landed16, landed16, landed16, landed16)


def adamw(name, w, g, m, v):
    R, C = w.shape
    tr = _row_tile(R, 256)

    def kern(w_ref, g_ref, m_ref, v_ref, d_ref, mo_ref, vo_ref):
        gg = g_ref[...]
        mn = ADAM_B1 * m_ref[...] + (1.0 - ADAM_B1) * gg
        vn = ADAM_B2 * v_ref[...] + (1.0 - ADAM_B2) * jnp.square(gg)
        m_hat = mn / (1.0 - ADAM_B1 ** ADAM_STEP)
        v_hat = vn / (1.0 - ADAM_B2 ** ADAM_STEP)
        d_ref[...] = -ADAM_LR * (m_hat / (jnp.sqrt(v_hat) + ADAM_EPS) + ADAM_WD * w_ref[...])
        mo_ref[...] = mn
        vo_ref[...] = vn
    spec = pl.BlockSpec((tr, C), lambda i: (i, 0))
    s = SDS((R, C), f32)
    return pl.pallas_call(kern, name=name, grid=(R // tr,), in_specs=[spec] * 4, out_specs=[spec] * 3, out_shape=[s, s, s],
                          compiler_params=pltpu.CompilerParams(dimension_semantics=("parallel",)))(w, g, m, v)


WEIGHT_NAMES = ("norm_g", "w_in", "gm_ln_g", "gm_ln_b", "gm_ws", "gm_bs", "conv_w", "conv_b", "dt_bias", "a_log", "d_skip",
                "ssm_norm_g", "w_out", "pool_w", "pool_b", "pool_scale", "ffn_w_gate", "ffn_w_up", "ffn_w_down")
SMALL = ("norm_g", "conv_w", "pool_b", "pool_scale")
REPL = ("gm_ln_g", "gm_ln_b", "gm_ws", "gm_bs", "conv_b", "dt_bias", "a_log", "d_skip", "ssm_norm_g")
SMALL_AXIS = {"norm_g": 2, "conv_w": 1, "pool_b": 1, "pool_scale": 0}
N_CHIPS = 4
IN_SH = IN_DIM // N_CHIPS
SMALL_ROWS = 8
REPL_ROWS = 72
E_OUT, E_GATE, E_UP, E_DOWN = 0, 512, 512 + 2 * FF_SH, 512 + 4 * FF_SH
E_POOL = E_DOWN + 2 * FF_SH
E_ROWS, E_TILE = E_POOL + 64, 400
L_SMALL, L_REPL, L_IN = 0, SMALL_ROWS, SMALL_ROWS + REPL_ROWS
L_END = L_IN + IN_SH
L_ROWS, L_TILE = 1408, 352


def _flat_rows(pieces, rows):
    v = jnp.concatenate([p.reshape(-1) for p in pieces])
    return jnp.pad(v, (0, rows * D - v.shape[0])).reshape(rows, D)


def _shard_small(name, full, k):
    ax = SMALL_AXIS[name]
    n = full.shape[ax] // N_CHIPS
    return lax.slice_in_dim(full, k * n, (k + 1) * n, axis=ax)


def _drop1(name, a):
    return a if name == "norm_g" else a[0]


HBM_SPEC = pl.BlockSpec(memory_space=pltpu.HBM)
SEM_SPEC = pl.BlockSpec(memory_space=pltpu.SEMAPHORE)
SPLIT_EFFECT = pltpu.SideEffectType.DATAFLOW_SIDE_EFFECTING


def _chip_of(me, flip):
    return 2 * ((1 - me[0]) if flip[0] else me[0]) + ((1 - me[1]) if flip[1] else me[1])


def gather_start(name, arrs, after, slotted=False):
    n = len(arrs)
    ncp = n * len(CHIP_FLIPS)

    def body(*refs):
        srcs, lands = refs[:n], refs[n:2 * n]
        send_sems, recv_sems, token = refs[2 * n + 1], refs[2 * n + 2], refs[-1]
        me = (lax.axis_index("x"), lax.axis_index("y"), lax.axis_index("c"))
        k = 2 * me[0] + me[1]
        for a in range(n):
            for f, flip in enumerate(CHIP_FLIPS):
                peer = tuple((1 - m) if fl else m for m, fl in zip(me, flip))
                src = srcs[a].at[_chip_of(me, flip)] if slotted else srcs[a]
                for ix in _pieces(src, 0, 16):
                    pltpu.make_async_remote_copy(src_ref=src.at[ix], dst_ref=lands[a].at[k].at[ix],
                                                 send_sem=send_sems.at[a * len(CHIP_FLIPS) + f], recv_sem=recv_sems.at[a * len(CHIP_FLIPS) + f],
                                                 device_id=peer, device_id_type=MESH_ID).start()
        token[...] = jnp.zeros_like(token)

    land_shapes = [a.shape if slotted else (N_CHIPS,) + a.shape for a in arrs]
    operands = [pltpu.with_memory_space_constraint(a, pltpu.HBM) for a in arrs]
    operands += [pltpu.with_memory_space_constraint(lax.empty(s, a.dtype), pltpu.HBM) for s, a in zip(land_shapes, arrs)]
    out = pl.pallas_call(
        body, name=name,
        out_shape=(pltpu.SemaphoreType.DMA((ncp,)), pltpu.SemaphoreType.DMA((ncp,)), *[pltpu.HBM(a.shape, a.dtype) for a in arrs],
                   *[pltpu.HBM(s, a.dtype) for s, a in zip(land_shapes, arrs)], SDS((8, 128), f32)),
        in_specs=[HBM_SPEC] * (2 * n) + [ANY], out_specs=(SEM_SPEC, SEM_SPEC, *[HBM_SPEC] * (2 * n), pl.BlockSpec(memory_space=pltpu.VMEM)),
        input_output_aliases={i: 2 + i for i in range(2 * n)},
        compiler_params=pltpu.CompilerParams(has_side_effects=SPLIT_EFFECT),
    )(*operands, after)
    return out[0], out[1], out[2:2 + n], out[2 + n:2 + 2 * n], out[-1]


def gather_wait(name, send_sems, recv_sems, thru, lands, after, slotted=False):
    n = len(thru)

    def body(*refs):
        srcs, lands_r = refs[:n], refs[n:2 * n]
        s_sems, r_sems = refs[2 * n], refs[2 * n + 1]
        me = (lax.axis_index("x"), lax.axis_index("y"), lax.axis_index("c"))
        k = 2 * me[0] + me[1]
        for a in range(n):
            for f, flip in enumerate(CHIP_FLIPS):
                peer = tuple((1 - m) if fl else m for m, fl in zip(me, flip))
                idx = a * len(CHIP_FLIPS) + f
                src = srcs[a].at[_chip_of(me, flip)] if slotted else srcs[a]
                pltpu.make_async_remote_copy(src_ref=src, dst_ref=lands_r[a].at[k], send_sem=s_sems.at[idx], recv_sem=r_sems.at[idx],
                                             device_id=peer, device_id_type=MESH_ID).wait_send()
                pltpu.make_async_remote_copy(src_ref=src, dst_ref=lands_r[a].at[_chip_of(me, flip)], send_sem=s_sems.at[idx],
                                             recv_sem=r_sems.at[idx], device_id=peer, device_id_type=MESH_ID).wait_recv()

    out = pl.pallas_call(
        body, name=name, out_shape=tuple(pltpu.HBM(t.shape, t.dtype) for t in (*thru, *lands)),
        in_specs=[HBM_SPEC] * (2 * n) + [SEM_SPEC, SEM_SPEC, ANY], out_specs=tuple([HBM_SPEC] * (2 * n)),
        input_output_aliases={i: i for i in range(2 * n)},
        compiler_params=pltpu.CompilerParams(has_side_effects=SPLIT_EFFECT),
    )(*thru, *lands, send_sems, recv_sems, after)
    return out[:n], out[n:]


def gather_weights(w_sh):
    big = [w_sh["w_in"][0], w_sh["w_out"][0], w_sh["pool_w"][0].reshape(4 * 64, POOL_GD)]
    small_pack = _flat_rows([w_sh[n] for n in SMALL], SMALL_ROWS)
    own = [b.astype(bf16) for b in big] + [small_pack]
    my_k = 2 * lax.axis_index("x") + lax.axis_index("y")
    s_in, s_out, s_pool, s_small = [lax.dynamic_update_slice(s, o[None], (my_k, 0, 0))
                                    for s, o in zip(gather_two_level("gather_weights", own[:3], own[3:]), own)]
    Wf = {n: w_sh[n][0] for n in REPL}
    Wf["w_in"] = s_in.transpose(1, 0, 2).reshape(D, IN_DIM)
    Wf["pool_w"] = s_pool.reshape(N_CHIPS, 4, 64, POOL_GD).transpose(1, 0, 2, 3).reshape(4, POOL_GD, POOL_GD)
    Wf["wo4"] = s_out
    small_shapes = [_drop1(n, w_sh[n]).shape for n in SMALL]
    parts = [_split_rows(s_small[k], small_shapes) for k in range(N_CHIPS)]
    for j, n in enumerate(SMALL):
        Wf[n] = jnp.concatenate([parts[k][j] for k in range(N_CHIPS)], axis=SMALL_AXIS[n])
    return Wf


def pack_early(G):
    slots = [jnp.concatenate([G["wo4"][k], G["wgT4"][0][k], G["wgT4"][1][k], G["wuT4"][0][k], G["wuT4"][1][k], G["wd4"][0][k], G["wd4"][1][k],
                              G["pool_w"][:, k * 64:(k + 1) * 64, :].reshape(64, D)], axis=0) for k in range(N_CHIPS)]
    return jnp.stack(slots)


def pack_late(G):
    sg = small_grads(G)
    repl = _flat_rows([sg[n] for n in REPL], REPL_ROWS)
    w_in_t = jnp.concatenate(G["w_inT"], axis=0)
    slots = [jnp.concatenate([_flat_rows([_shard_small(n, sg[n], k) for n in SMALL], SMALL_ROWS), repl,
                              jnp.pad(w_in_t[k * IN_SH:(k + 1) * IN_SH], ((0, L_ROWS - L_END), (0, 0)))], axis=0)
             for k in range(N_CHIPS)]
    return jnp.stack(slots)


def unpack_grads(early, late, w_sh):
    g = {"w_out": early[E_OUT:E_GATE], "ffn_w_down": early[E_DOWN:E_POOL], "pool_w": early[E_POOL:E_ROWS],
         "ffn_w_gate": jnp.stack([early[E_GATE + l * FF_SH:E_GATE + (l + 1) * FF_SH].T for l in range(2)]),
         "ffn_w_up": jnp.stack([early[E_UP + l * FF_SH:E_UP + (l + 1) * FF_SH].T for l in range(2)]),
         "w_in": late[L_IN:L_END].T}
    small = _split_rows(late[L_SMALL:L_REPL], [_drop1(n, w_sh[n]).shape for n in SMALL])
    repl = _split_rows(late[L_REPL:L_IN], [w_sh[n][0].shape for n in REPL])
    g.update(zip(SMALL, small))
    g.update(zip(REPL, repl))
    return {n: g[n].reshape(w_sh[n].shape) for n in WEIGHT_NAMES}


def _split_rows(flat2d, shapes):
    v = flat2d.reshape(-1)
    out, off = [], 0
    for s in shapes:
        n = math.prod(s)
        out.append(v[off:off + n].reshape(s))
        off += n
    return out


def kernel(x, norm_g, w_in, gm_ln_g, gm_ln_b, gm_ws, gm_bs, conv_w, conv_b, dt_bias, a_log, d_skip, ssm_norm_g, w_out, pool_w, pool_b, pool_scale, ffn_w_gate, ffn_w_up, ffn_w_down, loss_target, m_norm_g, m_w_in, m_gm_ln_g, m_gm_ln_b, m_gm_ws, m_gm_bs, m_conv_w, m_conv_b, m_dt_bias, m_a_log, m_d_skip, m_ssm_norm_g, m_w_out, m_pool_w, m_pool_b, m_pool_scale, m_ffn_w_gate, m_ffn_w_up, m_ffn_w_down, v_norm_g, v_w_in, v_gm_ln_g, v_gm_ln_b, v_gm_ws, v_gm_bs, v_conv_w, v_conv_b, v_dt_bias, v_a_log, v_d_skip, v_ssm_norm_g, v_w_out, v_pool_w, v_pool_b, v_pool_scale, v_ffn_w_gate, v_ffn_w_up, v_ffn_w_down):
    T = x.shape[1]
    w_sh = dict(zip(WEIGHT_NAMES, (norm_g, w_in, gm_ln_g, gm_ln_b, gm_ws, gm_bs, conv_w, conv_b, dt_bias, a_log, d_skip, ssm_norm_g, w_out,
                                   pool_w, pool_b, pool_scale, ffn_w_gate, ffn_w_up, ffn_w_down)))
    m_sh = dict(zip(WEIGHT_NAMES, (m_norm_g, m_w_in, m_gm_ln_g, m_gm_ln_b, m_gm_ws, m_gm_bs, m_conv_w, m_conv_b, m_dt_bias, m_a_log, m_d_skip,
                                   m_ssm_norm_g, m_w_out, m_pool_w, m_pool_b, m_pool_scale, m_ffn_w_gate, m_ffn_w_up, m_ffn_w_down)))
    v_sh = dict(zip(WEIGHT_NAMES, (v_norm_g, v_w_in, v_gm_ln_g, v_gm_ln_b, v_gm_ws, v_gm_bs, v_conv_w, v_conv_b, v_dt_bias, v_a_log, v_d_skip,
                                   v_ssm_norm_g, v_w_out, v_pool_w, v_pool_b, v_pool_scale, v_ffn_w_gate, v_ffn_w_up, v_ffn_w_down)))

    my_k = 2 * lax.axis_index("x") + lax.axis_index("y")
    ffn_own = [w_sh["ffn_w_gate"].reshape(2 * D, FF_SH).astype(bf16), w_sh["ffn_w_up"].reshape(2 * D, FF_SH).astype(bf16),
               w_sh["ffn_w_down"].reshape(2 * FF_SH, D).astype(bf16)]
    Wf = gather_weights(w_sh)
    send_sems, recv_sems, thru, lands, token = gather_start("gather_ffn_start", ffn_own, Wf["wo4"])
    Wf["norm_g"] = Wf["norm_g"] + token[0, 0]
    W = build_weights(Wf)

    def ffn_weights(after):
        _, landed = gather_wait("gather_ffn_wait", send_sems, recv_sems, thru, lands, after)
        return tuple(lax.dynamic_update_slice(l, o[None], (my_k, 0, 0)) for l, o in zip(landed, ffn_own))

    my_c = lax.axis_index("c")
    c_arr = my_c.astype(jnp.int32).reshape(1)
    k_arr = my_k.astype(jnp.int32).reshape(1)

    def pair_stage(tag, packs, tile):
        got = pair_split_exchange(f"grads{tag}_pair_split", packs, packs.shape[1] // 2)
        return pair_sum(f"grads{tag}_pair_sum", packs, got, c_arr, tile)

    def chip_stage(tag, pair16, landed, tile):
        half = chip_sum(f"grads{tag}_chip_sum", pair16, landed, k_arr, tile)
        other = pair_swap(f"grads{tag}_pair_swap", half)
        return jnp.concatenate([jnp.where(my_c == 0, half, other), jnp.where(my_c == 0, other, half)], axis=0)

    early = {}

    def early_grads(Ge):
        pair16 = pair_stage("E", pack_early(Ge), E_TILE)
        s_sems, r_sems, thru, lands, tok = gather_start("gradsE_scatter_start", [pair16], jnp.zeros((8, 128), f32), slotted=True)
        early.update(s_sems=s_sems, r_sems=r_sems, thru=thru, lands=lands)
        return tok

    loss_acc, grad_x, G = local_step(T, x[0], loss_target[0], W, ffn_weights, early_grads)
    pair_l = pair_stage("L", pack_late(G), L_TILE)
    total_l = chip_stage("L", pair_l, scatter_over_chips("gradsL_scatter", pair_l), L_TILE)
    (pair_e,), (landed_e,) = gather_wait("gradsE_scatter_wait", early["s_sems"], early["r_sems"], early["thru"], early["lands"], total_l,
                                         slotted=True)
    total_e = chip_stage("E", pair_e, landed_e, E_TILE)
    grads = unpack_grads(total_e, total_l, w_sh)

    delta, new_m, new_v = {}, {}, {}
    for n in WEIGHT_NAMES:
        shp = w_sh[n].shape
        two_d = (-1, shp[-1])
        d_, m_, v_ = adamw("adamw_" + n, w_sh[n].reshape(two_d), grads[n].reshape(two_d), m_sh[n].reshape(two_d), v_sh[n].reshape(two_d))
        delta[n], new_m[n], new_v[n] = d_.reshape(shp), m_.reshape(shp), v_.reshape(shp)

    loss = lax.psum(loss_acc[0, 0], ("x", "y", "c"))
    return (loss, grad_x[None], *[grads[n] for n in WEIGHT_NAMES], *[delta[n] for n in WEIGHT_NAMES],
            *[new_m[n] for n in WEIGHT_NAMES], *[new_v[n] for n in WEIGHT_NAMES])
```

```python
import math

import jax
import jax.numpy as jnp
from jax import lax
from jax.experimental import pallas as pl
from jax.experimental.pallas import tpu as pltpu

f32, bf16 = jnp.float32, jnp.bfloat16
SDS = jax.ShapeDtypeStruct

D = 1024
EPS = 1e-6
CHUNK = 128
GM_HEADS, GM_HD = 4, 256
SSM_GROUPS, SSM_HPG, SSM_P, SSM_N = 4, 4, 64, 128
N_HEADS = SSM_GROUPS * SSM_HPG
CONV_K = 4
CONV_DIM = 2048
POOL_WINDOWS = (2, 4, 8, 16)
POOL_GD = 256
POOL_HALO = 32
CONV_HALO = 8
D_FF = 2816
DT_PAD = 128
IN_DIM = 5136

ADAM_LR, ADAM_B1, ADAM_B2, ADAM_EPS, ADAM_WD, ADAM_STEP = 0.001, 0.9, 0.999, 1e-08, 0.01, 10

NT = (((1,), (1,)), ((), ()))
TN = (((0,), (0,)), ((), ()))
NN = (((1,), (0,)), ((), ()))
HI = lax.Precision.HIGHEST
MM_SUB = 256


def _silu(x):
    return x * jax.nn.sigmoid(x)


def _softplus(x):
    return jnp.maximum(x, 0.0) + jnp.log1p(jnp.exp(-jnp.abs(x)))


def _rms(x, g):
    return x * lax.rsqrt(jnp.mean(x * x, axis=-1, keepdims=True) + EPS) * g


def _rms_bwd(x, g, dy):
    r = lax.rsqrt(jnp.mean(x * x, axis=-1, keepdims=True) + EPS)
    xh = x * r
    dxh = dy * g
    dx = r * (dxh - xh * jnp.mean(dxh * xh, axis=-1, keepdims=True))
    return dx, jnp.sum(dy * xh, axis=0, keepdims=True)


def _bdot(a, b, dims=NN):
    return lax.dot_general(a.astype(bf16), b.astype(bf16), dims, preferred_element_type=f32)


def matmul(name, pairs, mode, out_dtype, tm, tn, tk=None):
    a0, b0 = pairs[0]
    if mode == "tn":
        M, N, K = a0.shape[1], b0.shape[1], a0.shape[0]
    else:
        M, K = a0.shape
        N = b0.shape[1] if mode == "nn" else b0.shape[0]
    tm, tn = min(tm, M), min(tn, N)
    assert M % tm == 0 and N % tn == 0, (name, M, N, tm, tn)
    if tk is None:
        nk = 1
    else:
        assert len(pairs) == 1 and K % tk == 0
        nk = K // tk
    dims = {"nn": NN, "nt": NT, "tn": TN}[mode]
    in_specs, args = [], []
    for a, b in pairs:
        kk = (a.shape[0] if mode == "tn" else a.shape[1]) if tk is None else tk
        if mode == "tn":
            in_specs.append(pl.BlockSpec((kk, tm), lambda j, i, k: (k, i)))
            in_specs.append(pl.BlockSpec((kk, tn), lambda j, i, k: (k, j)))
        elif mode == "nn":
            in_specs.append(pl.BlockSpec((tm, kk), lambda j, i, k: (i, k)))
            in_specs.append(pl.BlockSpec((kk, tn), lambda j, i, k: (k, j)))
        else:
            in_specs.append(pl.BlockSpec((tm, kk), lambda j, i, k: (i, k)))
            in_specs.append(pl.BlockSpec((tn, kk), lambda j, i, k: (j, k)))
        args += [a, b]
    npairs = len(pairs)

    def kern(*refs):
        o = refs[2 * npairs]
        part = None
        for p in range(npairs):
            d = _bdot(refs[2 * p][...], refs[2 * p + 1][...], dims)
            part = d if part is None else part + d
        if nk == 1:
            o[...] = part.astype(out_dtype)
        else:
            acc = refs[2 * npairs + 1]
            k = pl.program_id(2)

            @pl.when(k == 0)
            def _():
                acc[...] = part

            @pl.when(k > 0)
            def _():
                acc[...] += part

            @pl.when(k == nk - 1)
            def _():
                o[...] = acc[...].astype(out_dtype)

    return pl.pallas_call(
        kern, name=name, grid=(N // tn, M // tm, nk),
        in_specs=in_specs, out_specs=pl.BlockSpec((tm, tn), lambda j, i, k: (i, j)),
        out_shape=SDS((M, N), out_dtype),
        scratch_shapes=[pltpu.VMEM((tm, tn), f32)] if nk > 1 else [],
        compiler_params=pltpu.CompilerParams(dimension_semantics=("parallel", "parallel", "arbitrary")),
    )(*args)


def mm(name, grid, pairs, dims, o_spec, out_shape):
    nk = grid[2]
    npairs = len(pairs)
    in_specs, args = [], []
    for a, a_spec, b, b_spec in pairs:
        in_specs += [a_spec, b_spec]
        args += [a, b]
    blk = tuple(d for d in o_spec.block_shape if d is not None)

    def kern(*refs):
        o = refs[2 * npairs]
        part = None
        for p in range(npairs):
            d = _bdot(refs[2 * p][...], refs[2 * p + 1][...], dims)
            part = d if part is None else part + d
        if nk == 1:
            o[...] = part.astype(o.dtype)
        else:
            acc = refs[2 * npairs + 1]
            k = pl.program_id(2)

            @pl.when(k == 0)
            def _():
                acc[...] = part

            @pl.when(k > 0)
            def _():
                acc[...] += part

            @pl.when(k == nk - 1)
            def _():
                o[...] = acc[...].astype(o.dtype)

    return pl.pallas_call(
        kern, name=name, grid=grid, in_specs=in_specs, out_specs=o_spec, out_shape=out_shape,
        scratch_shapes=[pltpu.VMEM(blk, f32)] if nk > 1 else [],
        compiler_params=pltpu.CompilerParams(dimension_semantics=("parallel", "parallel", "arbitrary")),
    )(*args)


def mm_fused(name, n_row_blocks, pairs, dims, extra_ins, outs, accs, epilogue):
    npairs, nx, no, na = len(pairs), len(extra_ins), len(outs), len(accs)
    in_specs, args = [], []
    for a, a_spec, b, b_spec in pairs:
        in_specs += [a_spec, b_spec]
        args += [a, b]
    for arr, spec in extra_ins:
        in_specs.append(spec)
        args.append(arr)

    rows_blk = outs[0][1].block_shape[0]
    sub = min(rows_blk, MM_SUB)

    def kern(*refs):
        x_refs = refs[2 * npairs:2 * npairs + nx]
        o_refs = refs[2 * npairs + nx:2 * npairs + nx + no]
        a_refs = refs[2 * npairs + nx + no:]
        if na:
            @pl.when(pl.program_id(0) == 0)
            def _():
                for a in a_refs:
                    a[...] = jnp.zeros(a.shape, f32)
        for r0 in range(0, rows_blk, sub):
            rows = pl.ds(r0, sub)
            part = None
            for p in range(npairs):
                d = _bdot(refs[2 * p][rows, :], refs[2 * p + 1][...], dims)
                part = d if part is None else part + d
            epilogue(part, [x.at[rows, :] if x.shape[0] == rows_blk else x for x in x_refs], [o.at[rows, :] for o in o_refs], a_refs)

    return pl.pallas_call(
        kern, name=name, grid=(n_row_blocks,), in_specs=in_specs,
        out_specs=[spec for _, spec in outs] + [pl.BlockSpec(tuple(s), lambda i, nd=len(s): (0,) * nd) for s in accs],
        out_shape=[s for s, _ in outs] + [SDS(tuple(s), f32) for s in accs],
        compiler_params=pltpu.CompilerParams(dimension_semantics=("arbitrary",)),
    )(*args)


FF_SH = D_FF // 4


def ffn_up(name, T, tm, n_bf, wg4, wu4, l):
    sub = min(tm, MM_SUB)

    def kern(n_ref, wg_ref, wu_ref, g_ref, u_ref, a_ref):
        for r0 in range(0, tm, sub):
            rows = pl.ds(r0, sub)
            n = n_ref[rows, :]
            g = jnp.dot(n, wg_ref[...], preferred_element_type=f32)
            u = jnp.dot(n, wu_ref[...], preferred_element_type=f32)
            g_ref[rows, :] = g.astype(bf16)
            u_ref[rows, :] = u.astype(bf16)
            a_ref[rows, :] = (_silu(g) * u).astype(bf16)
    w_spec = pl.BlockSpec((None, D, FF_SH), lambda k, i: (k, l, 0))
    o_spec = pl.BlockSpec((None, tm, FF_SH), lambda k, i: (k, i, 0))
    s = SDS((4, T, FF_SH), bf16)
    return pl.pallas_call(kern, name=name, grid=(4, T // tm), in_specs=[pl.BlockSpec((tm, D), lambda k, i: (i, 0)), w_spec, w_spec],
                          out_specs=[o_spec] * 3, out_shape=[s, s, s],
                          compiler_params=pltpu.CompilerParams(dimension_semantics=("parallel", "parallel")))(n_bf, wg4, wu4)


def ffn_dgu(name, T, tm, d_f, wd4, gate4, up4, l):
    rc = 16

    sub = min(tm, MM_SUB)

    def kern(df_ref, wd_ref, g_ref, u_ref, dg_ref, du_ref, dact_ref):
        for s0 in range(0, tm, sub):
            dact_ref[pl.ds(s0, sub), :] = _bdot(df_ref[pl.ds(s0, sub), :], wd_ref[...], NT)
            for r0 in range(s0, s0 + sub, rc):
                rows = pl.ds(r0, rc)
                _, vjp = jax.vjp(lambda a, b: _silu(a) * b, g_ref[rows, :].astype(f32), u_ref[rows, :].astype(f32))
                dg, du = vjp(dact_ref[rows, :])
                dg_ref[rows, :] = dg.astype(bf16)
                du_ref[rows, :] = du.astype(bf16)
    a_spec = pl.BlockSpec((None, tm, FF_SH), lambda k, i: (k, i, 0))
    s = SDS((4, T, FF_SH), bf16)
    return pl.pallas_call(kern, name=name, grid=(4, T // tm),
                          in_specs=[pl.BlockSpec((tm, D), lambda k, i: (i, 0)), pl.BlockSpec((None, FF_SH, D), lambda k, i: (k, l, 0)), a_spec, a_spec],
                          out_specs=[a_spec] * 2, out_shape=[s, s], scratch_shapes=[pltpu.VMEM((tm, FF_SH), f32)],
                          compiler_params=pltpu.CompilerParams(dimension_semantics=("parallel", "parallel")))(d_f, wd4, gate4, up4)


def rowcall(name, body, T, tm, ins, outs, accs=(), scratch=(), reverse=False, sub=None):
    n = T // tm
    assert T % tm == 0

    def blk(i):
        return (n - 1 - i) if reverse else i

    in_specs, args = [], []
    for spec in ins:
        kind, arr = spec[0], spec[1]
        if kind == "row":
            _, _, w, cb = spec
            in_specs.append(pl.BlockSpec((tm, w), lambda i, cb=cb: (blk(i), cb)))
        elif kind == "prev":
            _, _, w, cb, h = spec
            r = tm // h
            in_specs.append(pl.BlockSpec((h, w), lambda i, cb=cb, r=r: (jnp.maximum(blk(i) * r - 1, 0), cb)))
        elif kind == "next":
            _, _, w, cb, h = spec
            r = tm // h
            in_specs.append(pl.BlockSpec((h, w), lambda i, cb=cb, r=r, h=h: (jnp.minimum((blk(i) + 1) * r, T // h - 1), cb)))
        else:
            nd = arr.ndim
            in_specs.append(pl.BlockSpec(arr.shape, lambda i, nd=nd: (0,) * nd))
        args.append(arr)
    out_shape = [SDS((T, w), dt) for w, dt in outs] + [SDS(tuple(s), f32) for s in accs]
    out_specs = [pl.BlockSpec((tm, w), lambda i: (blk(i), 0)) for w, _ in outs]
    out_specs += [pl.BlockSpec(tuple(s), lambda i, nd=len(s): (0,) * nd) for s in accs]
    ni, no, na = len(ins), len(outs), len(accs)

    def kern(*refs):
        i = pl.program_id(0)
        in_refs, out_refs = refs[:ni], refs[ni:ni + no]
        acc_refs, scr = refs[ni + no:ni + no + na], refs[ni + no + na:]
        if na:
            @pl.when(i == 0)
            def _():
                for a in acc_refs:
                    a[...] = jnp.zeros(a.shape, f32)
        if sub is None or sub >= tm:
            body(blk(i), n, in_refs, out_refs, acc_refs, scr)
        else:
            for r0 in range(0, tm, sub):
                rows = pl.ds(r0, sub)
                body(blk(i), n, [r.at[rows, :] if spec[0] == "row" else r for r, spec in zip(in_refs, ins)],
                     [o.at[rows, :] for o in out_refs], acc_refs, [s.at[rows, :] for s in scr])

    res = pl.pallas_call(
        kern, name=name, grid=(n,), in_specs=in_specs, out_specs=out_specs, out_shape=out_shape,
        scratch_shapes=list(scratch),
        compiler_params=pltpu.CompilerParams(dimension_semantics=("arbitrary",)),
    )(*args)
    return res


def rms_to_bf16(name, T, tm, x, g):
    def body(i, n, ins, outs, accs, scr):
        outs[0][...] = _rms(ins[0][...], ins[1][...]).astype(bf16)
    return rowcall(name, body, T, tm, [("row", x, D, 0), ("const", g)], [(D, bf16)], sub=64)[0]


def _layer_norm_parts(x):
    mu = jnp.mean(x, axis=-1, keepdims=True)
    xc = x - mu
    r = lax.rsqrt(jnp.mean(xc * xc, axis=-1, keepdims=True) + EPS)
    return xc * r, r


def gmlp_fwd(name, T, tm, uvz, ln_g, ln_b, wm, bs):
    def body(i, n, ins, outs, accs, scr):
        gu = jax.nn.gelu(ins[0][...])
        xh, _ = _layer_norm_parts(jax.nn.gelu(ins[1][...]))
        vln = (xh * ins[2][...] + ins[3][...]).astype(bf16)
        for c in range(ins[0].shape[0] // CHUNK):
            rows = slice(c * CHUNK, (c + 1) * CHUNK)
            for h in range(GM_HEADS):
                cols = slice(h * GM_HD, (h + 1) * GM_HD)
                mixed = jnp.dot(ins[4][h], vln[rows, cols], preferred_element_type=f32) + ins[5][h]
                outs[0][rows, cols] = (gu[rows, cols] * mixed).astype(bf16)
    return rowcall(name, body, T, tm, [("row", uvz, D, 0), ("row", uvz, D, 1), ("const", ln_g), ("const", ln_b), ("const", wm), ("const", bs)],
                   [(D, bf16)], sub=CHUNK)[0]


def gmlp_bwd(name, T, tm, uvz, d_ya, d_cb, ln_g, ln_b, wm, bs):
    def body(i, n, ins, outs, accs, scr):
        u, v, dya = ins[0][...], ins[1][...], ins[2][...]
        gu, gelu_u_vjp = jax.vjp(jax.nn.gelu, u)
        gv, gelu_v_vjp = jax.vjp(jax.nn.gelu, v)
        xh, r = _layer_norm_parts(gv)
        lng = ins[3][...]
        vln = (xh * lng + ins[4][...]).astype(bf16)
        rr = lax.broadcasted_iota(jnp.int32, (CHUNK, CHUNK), 0)
        cc = lax.broadcasted_iota(jnp.int32, (CHUNK, CHUNK), 1)
        causal = (rr >= cc).astype(f32)
        dvln_ref = scr[0]
        dgu_ref = scr[1]
        for c in range(ins[0].shape[0] // CHUNK):
            rows = slice(c * CHUNK, (c + 1) * CHUNK)
            for h in range(GM_HEADS):
                cols = slice(h * GM_HD, (h + 1) * GM_HD)
                w = ins[5][h]
                blk = vln[rows, cols]
                mixed = jnp.dot(w, blk, preferred_element_type=f32) + ins[6][h]
                dy = dya[rows, cols]
                dgu_ref[rows, cols] = dy * mixed
                dm = dy * gu[rows, cols]
                accs[3][h] += jnp.sum(dm, axis=1, keepdims=True)
                accs[2][h] += _bdot(dm, blk, NT) * causal
                dvln_ref[rows, cols] = _bdot(w, dm, TN)
        dvln = dvln_ref[...]
        accs[0][...] += jnp.sum(dvln * xh, axis=0, keepdims=True)
        accs[1][...] += jnp.sum(dvln, axis=0, keepdims=True)
        dxh = dvln * lng
        dgv = r * (dxh - jnp.mean(dxh, axis=-1, keepdims=True) - xh * jnp.mean(dxh * xh, axis=-1, keepdims=True))
        outs[0][...] = gelu_u_vjp(dgu_ref[...])[0].astype(bf16)
        outs[1][...] = gelu_v_vjp(dgv)[0].astype(bf16)
    return rowcall(name, body, T, tm,
                   [("row", uvz, D, 0), ("row", uvz, D, 1), ("row", d_ya, D, d_cb), ("const", ln_g), ("const", ln_b), ("const", wm), ("const", bs)],
                   [(D, bf16), (D, bf16)], accs=[(1, D), (1, D), (GM_HEADS, CHUNK, CHUNK), (GM_HEADS, CHUNK, 1)],
                   scratch=[pltpu.VMEM((tm, D), f32), pltpu.VMEM((tm, D), f32)], sub=CHUNK)


CONV_RC, CONV_LB = 32, 512


def _conv_fill(i, x_ref, halo_ref, scr, tm):
    scr[pl.ds(0, CONV_HALO), :] = jnp.where(i > 0, halo_ref[...], 0.0)
    scr[pl.ds(CONV_HALO, tm), :] = x_ref[...]


def _conv_taps(scr, r0, lanes):
    return [scr[pl.ds(r0 + CONV_HALO - (CONV_K - 1) + k, CONV_RC), lanes] for k in range(CONV_K)]


def conv_fwd(name, T, tm, xbc, conv_w, conv_b):
    def body(i, n, ins, outs, accs, scr):
        s = scr[0]
        _conv_fill(i, ins[0], ins[1], s, tm)
        for lb in range(CONV_DIM // CONV_LB):
            lanes = slice(lb * CONV_LB, (lb + 1) * CONV_LB)
            w, b = ins[2][:, lanes], ins[3][:, lanes]

            for r0 in range(0, tm, CONV_RC):
                taps = _conv_taps(s, r0, lanes)
                pre = b + sum(w[k:k + 1] * taps[k] for k in range(CONV_K))
                outs[0][pl.ds(r0, CONV_RC), lanes] = _silu(pre)
    return rowcall(name, body, T, tm, [("row", xbc, CONV_DIM, 0), ("prev", xbc, CONV_DIM, 0, CONV_HALO), ("const", conv_w), ("const", conv_b)],
                   [(CONV_DIM, f32)], scratch=[pltpu.VMEM((tm + CONV_HALO, CONV_DIM), f32)])[0]


def conv_bwd_pre(name, T, tm, xbc, d_xc, conv_w, conv_b):
    def body(i, n, ins, outs, accs, scr):
        s = scr[0]
        _conv_fill(i, ins[0], ins[1], s, tm)
        fold = lambda v: jnp.sum(v.reshape(CONV_RC // 8, 8, CONV_LB), axis=0)
        for lb in range(CONV_DIM // CONV_LB):
            lanes = slice(lb * CONV_LB, (lb + 1) * CONV_LB)
            w, b = ins[3][:, lanes], ins[4][:, lanes]

            sums = [jnp.zeros((8, CONV_LB), f32)] * (CONV_K + 1)
            for r0 in range(0, tm, CONV_RC):
                taps = _conv_taps(s, r0, lanes)
                pre = b + sum(w[k:k + 1] * taps[k] for k in range(CONV_K))
                _, vjp = jax.vjp(_silu, pre)
                dpre = vjp(ins[2][pl.ds(r0, CONV_RC), lanes])[0]
                outs[0][pl.ds(r0, CONV_RC), lanes] = dpre
                sums = [sums[k] + fold(dpre * taps[k]) for k in range(CONV_K)] + [sums[CONV_K] + fold(dpre)]
            for k in range(CONV_K):
                accs[0][pl.ds(k, 1), lanes] += jnp.sum(sums[k], axis=0, keepdims=True)
            accs[1][:, lanes] += jnp.sum(sums[CONV_K], axis=0, keepdims=True)
    return rowcall(name, body, T, tm,
                   [("row", xbc, CONV_DIM, 0), ("prev", xbc, CONV_DIM, 0, CONV_HALO), ("row", d_xc, CONV_DIM, 0), ("const", conv_w), ("const", conv_b)],
                   [(CONV_DIM, f32)], accs=[(CONV_K, CONV_DIM), (1, CONV_DIM)], scratch=[pltpu.VMEM((tm + CONV_HALO, CONV_DIM), f32)])


def conv_bwd_x(name, T, tm, d_pre, conv_w):
    def body(i, n, ins, outs, accs, scr):
        s = scr[0]
        s[pl.ds(0, tm), :] = ins[0][...]
        s[pl.ds(tm, CONV_HALO), :] = jnp.where(i < n - 1, ins[1][...], 0.0)
        for lb in range(CONV_DIM // CONV_LB):
            lanes = slice(lb * CONV_LB, (lb + 1) * CONV_LB)
            w = ins[2][:, lanes]

            for r0 in range(0, tm, CONV_RC):
                dx = sum(w[k:k + 1] * s[pl.ds(r0 + CONV_K - 1 - k, CONV_RC), lanes] for k in range(CONV_K))
                outs[0][pl.ds(r0, CONV_RC), lanes] = dx.astype(bf16)
    return rowcall(name, body, T, tm, [("row", d_pre, CONV_DIM, 0), ("next", d_pre, CONV_DIM, 0, CONV_HALO), ("const", conv_w)],
                   [(CONV_DIM, bf16)], scratch=[pltpu.VMEM((tm + CONV_HALO, CONV_DIM), f32)])[0]


def _ssd_prep(dtr, dtb, alog):
    rr = lax.broadcasted_iota(jnp.int32, (CHUNK, CHUNK), 0)
    cc = lax.broadcasted_iota(jnp.int32, (CHUNK, CHUNK), 1)
    dt = _softplus(dtr + dtb)
    dA = dt * -jnp.exp(alog)
    acum = jnp.dot((rr >= cc).astype(f32), dA, precision=HI, preferred_element_type=f32)
    return dt, acum, acum.T, jnp.sum(dA, axis=0, keepdims=True)


def _ssd_group(g, x, Bm, Cm, S, dt, acum, acumT, tot, dsk):
    rr = lax.broadcasted_iota(jnp.int32, (CHUNK, CHUNK), 0)
    cc = lax.broadcasted_iota(jnp.int32, (CHUNK, CHUNK), 1)
    tril = rr >= cc
    lane = lax.broadcasted_iota(jnp.int32, (1, DT_PAD), 1)
    sub = lax.broadcasted_iota(jnp.int32, (DT_PAD, 1), 0)
    glane = lax.broadcasted_iota(jnp.int32, (1, SSM_HPG * SSM_P), 1) // SSM_P
    hm = [(glane == r).astype(f32) for r in range(SSM_HPG)]
    pick = lambda v, r: jnp.sum(v * (lane == SSM_HPG * g + r).astype(f32), axis=1, keepdims=True)
    cols = [pick(acum, r) for r in range(SSM_HPG)]
    tots = [pick(tot, r) for r in range(SSM_HPG)]
    spread = lambda vals: sum(vals[r] * hm[r] for r in range(SSM_HPG))
    xdt = x * spread([pick(dt, r) for r in range(SSM_HPG)])
    cb = _bdot(Cm, Bm, NT)
    y = x * spread([pick(dsk, r) for r in range(SSM_HPG)])
    for r in range(SSM_HPG):
        row = jnp.sum(acumT * (sub == SSM_HPG * g + r).astype(f32), axis=0, keepdims=True)
        dec = jnp.exp(jnp.where(tril, cols[r] - row, -jnp.inf))
        y = y + _bdot(cb * dec, xdt * hm[r])
    y = y + _bdot(Cm, S) * spread([jnp.exp(c) for c in cols])
    dte = spread([jnp.exp(tots[r] - cols[r]) for r in range(SSM_HPG)])
    s_new = S * spread([jnp.exp(t) for t in tots]) + _bdot(Bm, xdt * dte, TN)
    return y, s_new


def _ssd_ins(xc, dtr):
    gw = SSM_HPG * SSM_P
    ins = [("row", xc, gw, g) for g in range(SSM_GROUPS)]
    ins += [("row", xc, SSM_N, D // SSM_N + g) for g in range(SSM_GROUPS)]
    ins += [("row", xc, SSM_N, D // SSM_N + SSM_GROUPS + g) for g in range(SSM_GROUPS)]
    ins += [("row", dtr, DT_PAD, 0)]
    return ins


SSD_CPS = 2


def ssd_fwd(name, T, xc, dtr, dtb, alog, dsk):
    gw = SSM_HPG * SSM_P

    def body(i, n, ins, outs, accs, scr):
        S = scr[0]

        @pl.when(i == 0)
        def _():
            S[...] = jnp.zeros(S.shape, f32)
        S4 = tuple(S[:, g * gw:(g + 1) * gw] for g in range(4))
        for c in range(SSD_CPS):
            rows = pl.ds(c * CHUNK, CHUNK)
            X4 = tuple(ins[g][rows, :] for g in range(4))
            B4 = tuple(ins[4 + g][rows, :] for g in range(4))
            C4 = tuple(ins[8 + g][rows, :] for g in range(4))
            prep = _ssd_prep(ins[12][rows, :], ins[13][...], ins[14][...])
            nxt = []
            for g in range(4):
                outs[1][rows, g * gw:(g + 1) * gw] = S4[g]
                y, s_new = _ssd_group(g, X4[g], B4[g], C4[g], S4[g], *prep, ins[15][...])
                outs[0][rows, g * gw:(g + 1) * gw] = y
                nxt.append(s_new)
            S4 = tuple(nxt)
        for g in range(4):
            S[:, g * gw:(g + 1) * gw] = S4[g]
    ins = _ssd_ins(xc, dtr) + [("const", dtb), ("const", alog), ("const", dsk)]
    return rowcall(name, body, T, SSD_CPS * CHUNK, ins, [(D, f32), (D, f32)], scratch=[pltpu.VMEM((SSM_N, D), f32)])


def ssd_bwd(name, T, xc, dtr, sprev, d_y, dtb, alog, dsk):
    gw = SSM_HPG * SSM_P

    def body(i, n, ins, outs, accs, scr):
        dS = scr[0]

        @pl.when(i == n - 1)
        def _():
            dS[...] = jnp.zeros(dS.shape, f32)
        dS4 = tuple(dS[:, g * gw:(g + 1) * gw] for g in range(4))
        def chunk(X4, dtr_c, B4, C4, S4, dtb_c, alog_c, dsk_c):
            prep = _ssd_prep(dtr_c, dtb_c, alog_c)
            res = [_ssd_group(g, X4[g], B4[g], C4[g], S4[g], *prep, dsk_c) for g in range(4)]
            return tuple(r[0] for r in res), tuple(r[1] for r in res)
        X4 = tuple(ins[g][...] for g in range(4))
        B4 = tuple(ins[4 + g][...] for g in range(4))
        C4 = tuple(ins[8 + g][...] for g in range(4))
        S4 = tuple(ins[13 + g][...] for g in range(4))
        dY4 = tuple(ins[17 + g][...] for g in range(4))
        _, vjp = jax.vjp(chunk, X4, ins[12][...], B4, C4, S4, ins[21][...], ins[22][...], ins[23][...])
        dX4, ddtr, dB4, dC4, dS4, ddtb, dalog, ddsk = vjp((dY4, dS4))
        for g in range(4):
            outs[0][:, g * gw:(g + 1) * gw] = dX4[g]
            outs[0][:, D + g * SSM_N:D + (g + 1) * SSM_N] = dB4[g]
            outs[0][:, D + (SSM_GROUPS + g) * SSM_N:D + (SSM_GROUPS + g + 1) * SSM_N] = dC4[g]
            dS[:, g * gw:(g + 1) * gw] = dS4[g]
        outs[1][...] = ddtr.astype(bf16)
        accs[0][...] += ddtb
        accs[1][...] += dalog
        accs[2][...] += ddsk
    ins = _ssd_ins(xc, dtr) + [("row", sprev, gw, g) for g in range(4)] + [("row", d_y, gw, g) for g in range(4)]
    ins += [("const", dtb), ("const", alog), ("const", dsk)]
    return rowcall(name, body, T, CHUNK, ins, [(CONV_DIM, f32), (DT_PAD, bf16)], accs=[(1, DT_PAD)] * 3,
                   scratch=[pltpu.VMEM((SSM_N, D), f32)], reverse=True)


def _gate_group(y, z, g):
    return _rms(y * _silu(z), g)


def gate_fwd(name, T, tm, y, uvz, gn):
    def body(i, n, ins, outs, accs, scr):
        for g in range(SSM_GROUPS):
            cols = slice(g * 256, (g + 1) * 256)
            outs[0][:, cols] = _gate_group(ins[0][:, cols], ins[1][:, cols], ins[2][:, cols]).astype(bf16)
    return rowcall(name, body, T, tm, [("row", y, D, 0), ("row", uvz, D, 2), ("const", gn)], [(D, bf16)], sub=64)[0]


def gate_bwd(name, T, tm, y, uvz, d_yb, d_cb, gn):
    def body(i, n, ins, outs, accs, scr):
        for g in range(SSM_GROUPS):
            cols = slice(g * 256, (g + 1) * 256)
            _, vjp = jax.vjp(_gate_group, ins[0][:, cols], ins[1][:, cols], ins[3][:, cols])
            dy, dz, dg = vjp(ins[2][:, cols])
            outs[0][:, cols] = dy
            outs[1][:, cols] = dz.astype(bf16)
            accs[0][:, cols] += dg
    return rowcall(name, body, T, tm, [("row", y, D, 0), ("row", uvz, D, 2), ("row", d_yb, D, d_cb), ("const", gn)],
                   [(D, f32), (D, bf16)], accs=[(1, D)], sub=64)


def _window_sum(src, cols, levels, tm, lv, trailing):
    cur, cur_cols = src, cols
    for l in range(1, levels + 1):
        shift = 2 ** (l - 1)
        last = l == levels
        if trailing:
            start = POOL_HALO if last else 8 * l
            rows = tm if last else tm + POOL_HALO - start
            new = cur[pl.ds(start, rows), cur_cols] + cur[pl.ds(start - shift, rows), cur_cols]
        else:
            start = 0
            rows = tm if last else tm + POOL_HALO - 8 * l
            new = cur[pl.ds(0, rows), cur_cols] + cur[pl.ds(shift, rows), cur_cols]
        if last:
            return new
        nxt = lv[l % 2]
        nxt[pl.ds(start, rows), :] = new
        cur, cur_cols = nxt, slice(None)


def _pool_diff(i, tm, h_ref, halo_ref, g_ref, scr, lv):
    g = g_ref[...]
    yn = _rms(h_ref[...], g)
    scr[pl.ds(0, POOL_HALO), :] = jnp.where(i > 0, _rms(halo_ref[...], g), 0.0)
    scr[pl.ds(POOL_HALO, tm), :] = yn
    pos = (i * tm + lax.broadcasted_iota(jnp.int32, (tm, 1), 0) + 1).astype(f32)
    parts = []
    for gi, win in enumerate(POOL_WINDOWS):
        cols = slice(gi * POOL_GD, (gi + 1) * POOL_GD)
        s = _window_sum(scr, cols, gi + 1, tm, lv, True)
        parts.append(s * (1.0 / jnp.minimum(pos, float(win))) - yn[:, cols])
    return parts


def pool_fwd(name, T, tm, h2, g_pre, pw, pb, psc, g_post, g_next):
    def body(i, n, ins, outs, accs, scr):
        parts = _pool_diff(i, tm, ins[0], ins[1], ins[2], scr[0], scr[1:3])
        for gi in range(len(POOL_WINDOWS)):
            cols = slice(gi * POOL_GD, (gi + 1) * POOL_GD)
            o = _bdot(parts[gi], ins[3][gi]) + ins[4][:, cols]
            outs[0][:, cols] = o * ins[5][:, cols]
        h = ins[0][...] + _rms(outs[0][...], ins[6][...])
        outs[1][...] = h
        outs[2][...] = _rms(h, ins[7][...]).astype(bf16)
    return rowcall(name, body, T, tm, [("row", h2, D, 0), ("prev", h2, D, 0, POOL_HALO), ("const", g_pre), ("const", pw), ("const", pb), ("const", psc),
                                       ("const", g_post), ("const", g_next)],
                   [(D, f32), (D, f32), (D, bf16)], scratch=[pltpu.VMEM((tm + POOL_HALO, D), f32)] + [pltpu.VMEM((tm + POOL_HALO, POOL_GD), f32)] * 2)


def pool_bwd(name, T, tm, h2, d_pm, d_res, g_pre, pw, pb, psc, f_prev, g_prev):
    def body(i, n, ins, outs, accs, scr):
        parts = _pool_diff(i, tm, ins[0], ins[1], ins[5], scr[0], scr[3:5])
        dpm = ins[2][...]
        psc_v = ins[8][...]
        dps = dpm * psc_v
        dps_halo = jnp.where(i < n - 1, ins[3][...] * psc_v, 0.0)
        accs[1][...] += jnp.sum(dps, axis=0, keepdims=True)
        pos = (i * tm + lax.broadcasted_iota(jnp.int32, (tm, 1), 0) + 1).astype(f32)
        pos_h = ((i + 1) * tm + lax.broadcasted_iota(jnp.int32, (POOL_HALO, 1), 0) + 1).astype(f32)
        r_scr = scr[1]
        dyn_scr = scr[2]
        for gi, win in enumerate(POOL_WINDOWS):
            cols = slice(gi * POOL_GD, (gi + 1) * POOL_GD)
            w = ins[6][gi]
            o = _bdot(parts[gi], w) + ins[7][:, cols]
            accs[2][:, cols] += jnp.sum(dpm[:, cols] * o, axis=0, keepdims=True)
            accs[0][gi] += _bdot(parts[gi], dps[:, cols], TN)
            q = _bdot(dps[:, cols], w, NT)
            qh = _bdot(dps_halo[:, cols], w, NT)
            r_scr[pl.ds(0, tm), cols] = q * (1.0 / jnp.minimum(pos, float(win)))
            r_scr[pl.ds(tm, POOL_HALO), cols] = qh * (1.0 / jnp.minimum(pos_h, float(win)))
            dyn_scr[:, cols] = _window_sum(r_scr, cols, gi + 1, tm, scr[3:5], False) - q
        dx, dg = _rms_bwd(ins[0][...], ins[5][...], dyn_scr[...])
        dh = ins[4][...] + dx
        outs[0][...] = dh
        accs[3][...] += dg
        df, dgp = _rms_bwd(ins[9][...], ins[10][...], dh)
        outs[1][...] = df.astype(bf16)
        accs[4][...] += dgp
    ins = [("row", h2, D, 0), ("prev", h2, D, 0, POOL_HALO), ("row", d_pm, D, 0), ("next", d_pm, D, 0, POOL_HALO), ("row", d_res, D, 0),
           ("const", g_pre), ("const", pw), ("const", pb), ("const", psc), ("row", f_prev, D, 0), ("const", g_prev)]
    return rowcall(name, body, T, tm, ins, [(D, f32), (D, bf16)], accs=[(4, POOL_GD, POOL_GD), (1, D), (1, D), (1, D), (1, D)],
                   scratch=[pltpu.VMEM((tm + POOL_HALO, D), f32), pltpu.VMEM((tm + POOL_HALO, D), f32), pltpu.VMEM((tm, D), f32)]
                   + [pltpu.VMEM((tm + POOL_HALO, POOL_GD), f32)] * 2)


def local_step(T, x, tgt, W, ffn_weights, early_grads):
    tm = 512 if T >= 1024 else T // 2
    TKW = 4096 if T >= 4096 else T
    ng = W["norm_g"]
    g = lambda l, j: ng[l, j][None, :]
    G = {}

    tf = tm
    once = pl.Buffered(1)
    rows_f = pl.BlockSpec((tf, D), lambda i: (i, 0))
    vec_f = pl.BlockSpec((1, D), lambda i: (0, 0))
    sh_f = [pl.BlockSpec((None, tf, FF_SH), lambda i, s=s: (s, i, 0)) for s in range(4)]
    out_f32, out_bf16 = (SDS((T, D), f32), rows_f), (SDS((T, D), bf16), rows_f)

    def resid_epilogue(with_pre):
        def ep(part, xs, os, accs):
            h = xs[0][...] + _rms(part, xs[1][...])
            os[0][...] = part
            os[1][...] = h
            if with_pre:
                os[2][...] = _rms(h, xs[2][...]).astype(bf16)
        return ep

    def bwd_epilogue(df_dtype):
        def ep(part, xs, os, accs):
            dx, dgp = _rms_bwd(xs[0][...], xs[3][...], part)
            dh = xs[2][...] + dx
            df, dgq = _rms_bwd(xs[1][...], xs[4][...], dh)
            os[0][...] = dh
            os[1][...] = df.astype(df_dtype)
            accs[0][...] += dgp
            accs[1][...] += dgq
        return ep

    def loss_epilogue(part, xs, os, accs):
        g_post = xs[2][...]
        e = xs[0][...] + _rms(part, g_post) - xs[1][...]
        accs[0][...] += jnp.sum(jnp.sum(e * e, axis=-1, keepdims=True) * (0.5 / D), axis=0, keepdims=True)
        dh = e * (1.0 / D)
        df, dg = _rms_bwd(part, g_post, dh)
        os[0][...] = dh
        os[1][...] = df.astype(bf16)
        accs[1][...] += dg

    def ffn_fwd(tag, n_bf, l, resid=None, loss=None):
        gate4, up4, act4 = ffn_up(f"ffn{tag}_up", T, min(T, 2 * tm), n_bf, W["wg4"], W["wu4"], l)
        wd_f = [pl.BlockSpec((None, FF_SH, D), lambda i, s=s: (s, l, 0), pipeline_mode=once) for s in range(4)]
        pairs = [(act4, sh_f[s], W["wd4"], wd_f[s]) for s in range(4)]
        if loss is not None:
            return (gate4, up4, act4) + tuple(mm_fused(f"ffn{tag}_down", T // tf, pairs, NN, [(loss[0], rows_f), (loss[1], rows_f), (loss[2], vec_f)],
                                                       [out_f32, out_bf16], [(1, 1), (1, D)], loss_epilogue))
        f, h_out = mm_fused(f"ffn{tag}_down", T // tf, pairs, NN, [(resid[0], rows_f), (resid[1], vec_f)], [out_f32, out_f32], [],
                            resid_epilogue(False))
        return gate4, up4, act4, f, h_out

    def ffn_bwd(tag, l, n_bf, gate4, up4, act4, d_f, h_out, f_pre, d_res, g_pre, g_post, df_dtype):
        d_gate4, d_up4 = ffn_dgu(f"ffn{tag}_dgu", T, min(T, 2 * tm), d_f, W["wd4"], gate4, up4, l)
        w_f = [pl.BlockSpec((None, D, FF_SH), lambda i, s=s: (s, l, 0), pipeline_mode=once) for s in range(4)]
        d_h, d_fp, dgp, dgq = mm_fused(
            f"ffn{tag}_dn", T // tf, [(d_gate4, sh_f[s], W["wg4"], w_f[s]) for s in range(4)] + [(d_up4, sh_f[s], W["wu4"], w_f[s]) for s in range(4)],
            NT, [(h_out, rows_f), (f_pre, rows_f), (d_res, rows_f), (g_pre, vec_f), (g_post, vec_f)],
            [out_f32, (SDS((T, D), df_dtype), rows_f)], [(1, D), (1, D)], bwd_epilogue(df_dtype))

        def wgrad(nm, a4, b):
            return mm(nm, (4, 1, T // TKW),
                      [(a4, pl.BlockSpec((None, TKW, FF_SH), lambda s, j, k: (s, k, 0)), b, pl.BlockSpec((TKW, D), lambda s, j, k: (k, 0)))],
                      TN, pl.BlockSpec((None, FF_SH, D), lambda s, j, k: (s, 0, 0)), SDS((4, FF_SH, D), f32))
        return d_h, d_fp, dgp, dgq, wgrad(f"ffn{tag}_dwg", d_gate4, n_bf), wgrad(f"ffn{tag}_dwu", d_up4, n_bf), wgrad(f"ffn{tag}_dwd", act4, d_f)

    y0 = rms_to_bf16("l0_prenorm", T, tm, x, g(0, 0))
    uvz = matmul("in_uvz", [(y0, W["w_uvz"])], "nn", f32, 2 * tm, 1024)
    xbc = matmul("in_xbc", [(y0, W["w_xbc"])], "nn", f32, 2 * tm, 1024)
    dtr = matmul("in_dt", [(y0, W["w_dt"])], "nn", f32, 2 * tm, DT_PAD)
    y_a = gmlp_fwd("gmlp_fwd", T, tm, uvz, W["ln_g"], W["ln_b"], W["wm"], W["bs"])
    xc = conv_fwd("conv_fwd", T, tm, xbc, W["conv_w"], W["conv_b"])
    y_ssd, sprev = ssd_fwd("ssd_fwd", T, xc, dtr, W["dtb"], W["alog"], W["dsk"])
    y_b = gate_fwd("gate_fwd", T, tm, y_ssd, uvz, W["gn"])
    half = D // 2
    wo4 = W["wo4"]
    ycol = [pl.BlockSpec((tf, half), lambda i, cb=cb: (i, cb)) for cb in range(2)]
    wo_s = [pl.BlockSpec((None, half, D), lambda i, s=s: (s, 0, 0), pipeline_mode=once) for s in range(4)]
    mixo, h1, n1 = mm_fused("out_proj", T // tf, [(y_a, ycol[0], wo4, wo_s[0]), (y_a, ycol[1], wo4, wo_s[1]),
                                                  (y_b, ycol[0], wo4, wo_s[2]), (y_b, ycol[1], wo4, wo_s[3])], NN,
                            [(x, rows_f), (g(0, 1), vec_f), (g(0, 2), vec_f)], [out_f32, out_f32, out_bf16], [], resid_epilogue(True))
    W = dict(W)
    W["wg4"], W["wu4"], W["wd4"] = ffn_weights(h1)
    gate0, up0, act0, f1, h2 = ffn_fwd("0", n1, 0, resid=(h1, g(0, 3)))
    pm, h3, n3 = pool_fwd("pool_fwd", T, tm, h2, g(1, 0), W["pool_w"], W["pool_b"], W["pool_scale"], g(1, 1), g(1, 2))
    gate1, up1, act1, dh4, d_f2, loss_acc, dg13 = ffn_fwd("1", n3, 1, loss=(h3, tgt, g(1, 3)))
    d_h3, d_pm, dg12, dg11, dwg1, dwu1, dwd1 = ffn_bwd("1", 1, n3, gate1, up1, act1, d_f2, h3, pm, dh4, g(1, 2), g(1, 1), f32)
    d_h2, d_f1, G["pool_w"], G["pool_b"], G["pool_scale"], dg10, dg03 = pool_bwd("pool_bwd", T, tm, h2, d_pm, d_h3, g(1, 0), W["pool_w"], W["pool_b"],
                                                                                 W["pool_scale"], f1, g(0, 3))
    d_h1, d_mixo, dg02, dg01, dwg0, dwu0, dwd0 = ffn_bwd("0", 0, n1, gate0, up0, act0, d_f1, h1, mixo, d_h2, g(0, 2), g(0, 1), bf16)
    def d_wo(nm, y):
        return mm(nm, (2, 1, T // TKW), [(y, pl.BlockSpec((TKW, half), lambda s, j, k: (k, s)), d_mixo, pl.BlockSpec((TKW, D), lambda s, j, k: (k, 0)))],
                  TN, pl.BlockSpec((None, half, D), lambda s, j, k: (s, 0, 0)), SDS((2, half, D), f32))
    d_ycat = matmul("out_proj_dy", [(d_mixo, wo4.reshape(4 * half, D))], "nt", f32, 2 * tm, 1024)
    dwo_a, dwo_b = d_wo("out_proj_dwa", y_a), d_wo("out_proj_dwb", y_b)
    G["wo4"] = [dwo_a[0], dwo_a[1], dwo_b[0], dwo_b[1]]
    G["wgT4"], G["wuT4"], G["wd4"] = [dwg0, dwg1], [dwu0, dwu1], [dwd0, dwd1]
    token = early_grads(G)
    d_yssd, d_z, G["gn"] = gate_bwd("gate_bwd", T, tm, y_ssd, uvz, d_ycat, 1, W["gn"] + token[0, 0])
    d_xc, d_dtr, G["dtb"], G["alog"], G["dsk"] = ssd_bwd("ssd_bwd", T, xc, dtr, sprev, d_yssd, W["dtb"], W["alog"], W["dsk"])
    d_pre, G["conv_w"], G["conv_b"] = conv_bwd_pre("conv_bwd_pre", T, tm, xbc, d_xc, W["conv_w"], W["conv_b"])
    d_xbc = conv_bwd_x("conv_bwd_x", T, tm, d_pre, W["conv_w"])
    d_u, d_v, G["ln_g"], G["ln_b"], G["wm"], G["bs"] = gmlp_bwd("gmlp_bwd", T, tm, uvz, d_ycat, 0, W["ln_g"], W["ln_b"], W["wm"], W["bs"])
    w_u, w_v, w_z = W["w_uvz"][:, :D], W["w_uvz"][:, D:2 * D], W["w_uvz"][:, 2 * D:]
    def pre_epilogue(part, xs, os, accs):
        dx, dg = _rms_bwd(xs[0][...], xs[2][...], part)
        os[0][...] = xs[1][...] + dx
        accs[0][...] += dg
    blk = lambda w: pl.BlockSpec((tf, w), lambda i: (i, 0))
    whole = lambda a: pl.BlockSpec(a.shape, lambda i: (0, 0), pipeline_mode=once)
    grad_x, dg00 = mm_fused("in_dy0", T // tf, [(d_u, blk(D), w_u, whole(w_u)), (d_v, blk(D), w_v, whole(w_v)), (d_z, blk(D), w_z, whole(w_z)),
                                                (d_xbc, blk(CONV_DIM), W["w_xbc"], whole(W["w_xbc"])), (d_dtr, blk(DT_PAD), W["w_dt"], whole(W["w_dt"]))],
                            NT, [(x, rows_f), (d_h1, rows_f), (g(0, 0), vec_f)], [out_f32], [(1, D)], pre_epilogue)
    G["w_inT"] = [matmul("in_dwu", [(d_u, y0)], "tn", f32, 1024, 1024, TKW), matmul("in_dwv", [(d_v, y0)], "tn", f32, 1024, 1024, TKW),
                  matmul("in_dwz", [(d_z, y0)], "tn", f32, 1024, 1024, TKW), matmul("in_dwxbc", [(d_xbc, y0)], "tn", f32, 1024, 1024, TKW),
                  matmul("in_dwdt", [(d_dtr, y0)], "tn", f32, DT_PAD, 1024, TKW)[:N_HEADS]]
    G["norm_g"] = jnp.stack([jnp.concatenate([dg00, dg01, dg02, dg03], 0), jnp.concatenate([dg10, dg11, dg12, dg13], 0)])
    return loss_acc, grad_x, G


def build_weights(Wf):
    causal = jnp.tril(jnp.ones((CHUNK, CHUNK), bool))
    w_in = Wf["w_in"].astype(bf16)
    pad16 = lambda v: jnp.pad(v.reshape(1, N_HEADS).astype(f32), ((0, 0), (0, DT_PAD - N_HEADS)))
    return {
        "norm_g": Wf["norm_g"],
        "w_uvz": w_in[:, :3 * D], "w_xbc": w_in[:, 3 * D:3 * D + CONV_DIM],
        "w_dt": jnp.pad(w_in[:, 3 * D + CONV_DIM:], ((0, 0), (0, DT_PAD - N_HEADS))),
        "ln_g": Wf["gm_ln_g"].reshape(1, D), "ln_b": Wf["gm_ln_b"].reshape(1, D),
        "wm": jnp.where(causal[None], Wf["gm_ws"], 0).astype(bf16), "bs": Wf["gm_bs"].reshape(GM_HEADS, CHUNK, 1),
        "conv_w": Wf["conv_w"], "conv_b": Wf["conv_b"].reshape(1, CONV_DIM),
        "dtb": pad16(Wf["dt_bias"]), "alog": pad16(Wf["a_log"]), "dsk": pad16(Wf["d_skip"]),
        "gn": Wf["ssm_norm_g"].reshape(1, D),
        "wo4": Wf["wo4"].astype(bf16),
        "pool_w": Wf["pool_w"].astype(bf16), "pool_b": Wf["pool_b"].reshape(1, D), "pool_scale": Wf["pool_scale"].reshape(1, D),
    }


def small_grads(G):
    return {
        "norm_g": G["norm_g"],
        "gm_ln_g": G["ln_g"].reshape(D), "gm_ln_b": G["ln_b"].reshape(D),
        "gm_ws": G["wm"], "gm_bs": G["bs"].reshape(GM_HEADS, CHUNK),
        "conv_w": G["conv_w"], "conv_b": G["conv_b"].reshape(CONV_DIM),
        "dt_bias": G["dtb"][0, :N_HEADS], "a_log": G["alog"][0, :N_HEADS], "d_skip": G["dsk"][0, :N_HEADS],
        "ssm_norm_g": G["gn"].reshape(D),
        "pool_b": G["pool_b"].reshape(4, POOL_GD), "pool_scale": G["pool_scale"].reshape(D),
    }


MESH_ID = pl.DeviceIdType.MESH
ANY = pl.BlockSpec(memory_space=pl.ANY)


DMA_CHUNK_BYTES = 2 << 20
DMA_MAX_CHUNKS = 32


def _pieces(view, axis, align):
    shape = view.shape
    nbytes = math.prod(shape) * jnp.dtype(view.dtype).itemsize
    n = max(1, min(DMA_MAX_CHUNKS, -(-nbytes // DMA_CHUNK_BYTES)))
    rows = shape[axis]
    size = -(-rows // n)
    size = -(-size // align) * align
    out = []
    for s in range(0, rows, size):
        idx = [slice(None)] * len(shape)
        idx[axis] = pl.ds(s, min(size, rows - s))
        out.append(tuple(idx))
    return out


def comm_call(name, operands, out_shapes, plan):
    n_in = len(operands)
    n_out = len(out_shapes)
    n_remote, n_local = plan((0, 0, 0), [None] * n_in, [None] * n_out, True)

    def body(*refs):
        in_refs, out_refs = refs[:n_in], refs[n_in:n_in + n_out]
        send_sems, recv_sems, local_sems = refs[n_in + n_out:]
        me = (lax.axis_index("x"), lax.axis_index("y"), lax.axis_index("c"))
        remote, local = plan(me, in_refs, out_refs, False)
        align = lambda v: 16 if v.dtype == bf16 else 8
        for j, (s, d, axis) in enumerate(local):
            for ix in _pieces(s, axis, align(s)):
                pltpu.make_async_copy(s.at[ix], d.at[ix], local_sems.at[j]).start()
        peers = [tuple((1 - m) if f else m for m, f in zip(me, flip)) for flip, *_ in remote]
        for k, (flip, src, dst, _, axis) in enumerate(remote):
            for ix in _pieces(src, axis, align(src)):
                pltpu.make_async_remote_copy(src_ref=src.at[ix], dst_ref=dst.at[ix], send_sem=send_sems.at[k], recv_sem=recv_sems.at[k],
                                             device_id=peers[k], device_id_type=MESH_ID).start()
        for k, (flip, src, dst, landing, axis) in enumerate(remote):
            pltpu.make_async_remote_copy(src_ref=landing, dst_ref=landing, send_sem=send_sems.at[k], recv_sem=recv_sems.at[k],
                                         device_id=peers[k], device_id_type=MESH_ID).wait_recv()
        for k, (flip, src, dst, landing, axis) in enumerate(remote):
            pltpu.make_async_remote_copy(src_ref=src, dst_ref=dst, send_sem=send_sems.at[k], recv_sem=recv_sems.at[k],
                                         device_id=peers[k], device_id_type=MESH_ID).wait_send()
        for j, (s, d, axis) in enumerate(local):
            pltpu.make_async_copy(s, d, local_sems.at[j]).wait()

    return pl.pallas_call(
        body, name=name, out_shape=list(out_shapes), in_specs=[ANY] * n_in, out_specs=[ANY] * n_out,
        scratch_shapes=[pltpu.SemaphoreType.DMA((n_remote,)), pltpu.SemaphoreType.DMA((n_remote,)), pltpu.SemaphoreType.DMA((max(n_local, 1),))],
    )(*operands)


CHIP_FLIPS = ((1, 0, 0), (0, 1, 0), (1, 1, 0))
PAIR_FLIP = (0, 0, 1)


def gather_two_level(name, halved, whole):
    nh, nw = len(halved), len(whole)
    nf = len(CHIP_FLIPS)

    def body(*refs):
        srcs, outs = refs[:nh + nw], refs[nh + nw:2 * (nh + nw)]
        send_sems, recv_sems, fwd_send, fwd_recv = refs[2 * (nh + nw):]
        me = (lax.axis_index("x"), lax.axis_index("y"), lax.axis_index("c"))
        k, c = 2 * me[0] + me[1], me[2]
        sibling = (me[0], me[1], 1 - c)
        peers = [tuple((1 - m) if fl else m for m, fl in zip(me, flip)) for flip in CHIP_FLIPS]

        def half(ref, which):
            rh = ref.shape[0] // 2
            return ref.at[pl.ds(pl.multiple_of(which * rh, 16), rh), :]

        def ici(a, f):
            src = half(srcs[a], c) if a < nh else srcs[a]
            dst = half(outs[a].at[k], c) if a < nh else outs[a].at[k]
            return pltpu.make_async_remote_copy(src_ref=src, dst_ref=dst, send_sem=send_sems.at[a * nf + f], recv_sem=recv_sems.at[a * nf + f],
                                                device_id=peers[f], device_id_type=MESH_ID)

        def landed(a, f):
            slot = outs[a].at[_chip_of(me, CHIP_FLIPS[f])]
            return half(slot, c) if a < nh else slot

        def forward(a, f, which):
            v = half(outs[a].at[_chip_of(me, CHIP_FLIPS[f])], which)
            return pltpu.make_async_remote_copy(src_ref=v, dst_ref=v, send_sem=fwd_send.at[a * nf + f], recv_sem=fwd_recv.at[a * nf + f],
                                                device_id=sibling, device_id_type=MESH_ID)

        copies = [ici(a, f) for a in range(nh + nw) for f in range(nf)]
        for cp in copies:
            cp.start()
        fwds = []
        for a in range(nh):
            for f in range(nf):
                lv = landed(a, f)
                pltpu.make_async_remote_copy(src_ref=lv, dst_ref=lv, send_sem=send_sems.at[a * nf + f], recv_sem=recv_sems.at[a * nf + f],
                                             device_id=peers[f], device_id_type=MESH_ID).wait_recv()
                fw = forward(a, f, c)
                fw.start()
                fwds.append(fw)
        for a in range(nh, nh + nw):
            for f in range(nf):
                lv = landed(a, f)
                pltpu.make_async_remote_copy(src_ref=lv, dst_ref=lv, send_sem=send_sems.at[a * nf + f], recv_sem=recv_sems.at[a * nf + f],
                                             device_id=peers[f], device_id_type=MESH_ID).wait_recv()
        for a in range(nh):
            for f in range(nf):
                forward(a, f, 1 - c).wait_recv()
        for fw in fwds:
            fw.wait_send()
        for cp in copies:
            cp.wait_send()

    arrs = list(halved) + list(whole)
    n_ici = (nh + nw) * nf
    return pl.pallas_call(
        body, name=name, out_shape=[SDS((N_CHIPS,) + a.shape, a.dtype) for a in arrs], in_specs=[ANY] * len(arrs), out_specs=[ANY] * len(arrs),
        scratch_shapes=[pltpu.SemaphoreType.DMA((n_ici,)), pltpu.SemaphoreType.DMA((n_ici,)),
                        pltpu.SemaphoreType.DMA((nh * nf,)), pltpu.SemaphoreType.DMA((nh * nf,))],
    )(*arrs)


def pair_split_exchange(name, p, rh):
    def plan(me, ins, outs, count):
        if count:
            return 1, 0
        theirs = ins[0].at[:, pl.ds(pl.multiple_of((1 - me[2]) * rh, 8), rh), :]
        return [(PAIR_FLIP, theirs, outs[0], outs[0], 1)], []
    return comm_call(name, [p], [SDS((4, rh, p.shape[2]), p.dtype)], plan)[0]


def scatter_over_chips(name, cs):
    def plan(me, ins, outs, count):
        if count:
            return len(CHIP_FLIPS), 0
        k = 2 * me[0] + me[1]
        remote = []
        for flip in CHIP_FLIPS:
            kp = 2 * ((1 - me[0]) if flip[0] else me[0]) + ((1 - me[1]) if flip[1] else me[1])
            remote.append((flip, ins[0].at[kp], outs[0].at[k], outs[0].at[kp], 0))
        return remote, []
    return comm_call(name, [cs], [SDS(cs.shape, cs.dtype)], plan)[0]


def pair_swap(name, half):
    def plan(me, ins, outs, count):
        if count:
            return 1, 0
        return [(PAIR_FLIP, ins[0], outs[0], outs[0], 0)], []
    return comm_call(name, [half], [SDS(half.shape, half.dtype)], plan)[0]


def _row_tile(rows, cap=512):
    if rows <= cap:
        return rows
    t = cap - cap % 8
    while rows % t:
        t -= 8
    return t


def pair_sum(name, packs, got, c_arr, tile):
    rh = got.shape[1]
    nb = rh // tile

    def kern(c_ref, a_ref, b_ref, o16_ref):
        o16_ref[...] = (a_ref[...] + b_ref[...]).astype(bf16)
    blk = (None, tile, D)
    grid_spec = pltpu.PrefetchScalarGridSpec(
        num_scalar_prefetch=1, grid=(4, nb),
        in_specs=[pl.BlockSpec(blk, lambda s, i, c: (s, c[0] * nb + i, 0)), pl.BlockSpec(blk, lambda s, i, c: (s, i, 0))],
        out_specs=pl.BlockSpec(blk, lambda s, i, c: (s, i, 0)))
    return pl.pallas_call(kern, name=name, grid_spec=grid_spec, out_shape=SDS(got.shape, bf16),
                          compiler_params=pltpu.CompilerParams(dimension_semantics=("parallel", "parallel")))(c_arr, packs, got)


def chip_sum(name, own16, landed16, k_arr, tile):
    rh = own16.shape[1]
    nb = rh // tile

    def kern(k_ref, own_ref, l0, l1, l2, l3, o_ref):
        k = k_ref[0]
        s = None
        for j, lref in enumerate((l0, l1, l2, l3)):
            t = jnp.where(k == j, own_ref[...], lref[...]).astype(f32)
            s = t if s is None else s + t
        o_ref[...] = s
    blk = (None, tile, D)
    land = [pl.BlockSpec(blk, lambda i, k, j=j: (jnp.where(k[0] == j, (j + 1) % N_CHIPS, j), i, 0)) for j in range(N_CHIPS)]
    grid_spec = pltpu.PrefetchScalarGridSpec(
        num_scalar_prefetch=1, grid=(nb,),
        in_specs=[pl.BlockSpec(blk, lambda i, k: (k[0], i, 0))] + land,
        out_specs=pl.BlockSpec((tile, D), lambda i, k: (i, 0)))
    return pl.pallas_call(kern, name=name, grid_spec=grid_spec, out_shape=SDS((rh, D), f32),
                          compiler_params=pltpu.CompilerParams(dimension_semantics=("parallel",)))(k_arr, own16, landed16, landed16, landed16, landed16)


def adamw(name, w, g, m, v):
    R, C = w.shape
    tr = _row_tile(R, 256)

    def kern(w_ref, g_ref, m_ref, v_ref, d_ref, mo_ref, vo_ref):
        gg = g_ref[...]
        mn = ADAM_B1 * m_ref[...] + (1.0 - ADAM_B1) * gg
        vn = ADAM_B2 * v_ref[...] + (1.0 - ADAM_B2) * jnp.square(gg)
        m_hat = mn / (1.0 - ADAM_B1 ** ADAM_STEP)
        v_hat = vn / (1.0 - ADAM_B2 ** ADAM_STEP)
        d_ref[...] = -ADAM_LR * (m_hat / (jnp.sqrt(v_hat) + ADAM_EPS) + ADAM_WD * w_ref[...])
        mo_ref[...] = mn
        vo_ref[...] = vn
    spec = pl.BlockSpec((tr, C), lambda i: (i, 0))
    s = SDS((R, C), f32)
    return pl.pallas_call(kern, name=name, grid=(R // tr,), in_specs=[spec] * 4, out_specs=[spec] * 3, out_shape=[s, s, s],
                          compiler_params=pltpu.CompilerParams(dimension_semantics=("parallel",)))(w, g, m, v)


WEIGHT_NAMES = ("norm_g", "w_in", "gm_ln_g", "gm_ln_b", "gm_ws", "gm_bs", "conv_w", "conv_b", "dt_bias", "a_log", "d_skip",
                "ssm_norm_g", "w_out", "pool_w", "pool_b", "pool_scale", "ffn_w_gate", "ffn_w_up", "ffn_w_down")
SMALL = ("norm_g", "conv_w", "pool_b", "pool_scale")
REPL = ("gm_ln_g", "gm_ln_b", "gm_ws", "gm_bs", "conv_b", "dt_bias", "a_log", "d_skip", "ssm_norm_g")
SMALL_AXIS = {"norm_g": 2, "conv_w": 1, "pool_b": 1, "pool_scale": 0}
N_CHIPS = 4
IN_SH = IN_DIM // N_CHIPS
SMALL_ROWS = 8
REPL_ROWS = 72
E_OUT, E_GATE, E_UP, E_DOWN = 0, 512, 512 + 2 * FF_SH, 512 + 4 * FF_SH
E_POOL = E_DOWN + 2 * FF_SH
E_ROWS, E_TILE = E_POOL + 64, 400
L_SMALL, L_REPL, L_IN = 0, SMALL_ROWS, SMALL_ROWS + REPL_ROWS
L_END = L_IN + IN_SH
L_ROWS, L_TILE = 1408, 352


def _flat_rows(pieces, rows):
    v = jnp.concatenate([p.reshape(-1) for p in pieces])
    return jnp.pad(v, (0, rows * D - v.shape[0])).reshape(rows, D)


def _shard_small(name, full, k):
    ax = SMALL_AXIS[name]
    n = full.shape[ax] // N_CHIPS
    return lax.slice_in_dim(full, k * n, (k + 1) * n, axis=ax)


def _drop1(name, a):
    return a if name == "norm_g" else a[0]


HBM_SPEC = pl.BlockSpec(memory_space=pltpu.HBM)
SEM_SPEC = pl.BlockSpec(memory_space=pltpu.SEMAPHORE)
SPLIT_EFFECT = pltpu.SideEffectType.DATAFLOW_SIDE_EFFECTING


def _chip_of(me, flip):
    return 2 * ((1 - me[0]) if flip[0] else me[0]) + ((1 - me[1]) if flip[1] else me[1])


def gather_start(name, arrs, after, slotted=False):
    n = len(arrs)
    ncp = n * len(CHIP_FLIPS)

    def body(*refs):
        srcs, lands = refs[:n], refs[n:2 * n]
        send_sems, recv_sems, token = refs[2 * n + 1], refs[2 * n + 2], refs[-1]
        me = (lax.axis_index("x"), lax.axis_index("y"), lax.axis_index("c"))
        k = 2 * me[0] + me[1]
        for a in range(n):
            for f, flip in enumerate(CHIP_FLIPS):
                peer = tuple((1 - m) if fl else m for m, fl in zip(me, flip))
                src = srcs[a].at[_chip_of(me, flip)] if slotted else srcs[a]
                for ix in _pieces(src, 0, 16):
                    pltpu.make_async_remote_copy(src_ref=src.at[ix], dst_ref=lands[a].at[k].at[ix],
                                                 send_sem=send_sems.at[a * len(CHIP_FLIPS) + f], recv_sem=recv_sems.at[a * len(CHIP_FLIPS) + f],
                                                 device_id=peer, device_id_type=MESH_ID).start()
        token[...] = jnp.zeros_like(token)

    land_shapes = [a.shape if slotted else (N_CHIPS,) + a.shape for a in arrs]
    operands = [pltpu.with_memory_space_constraint(a, pltpu.HBM) for a in arrs]
    operands += [pltpu.with_memory_space_constraint(lax.empty(s, a.dtype), pltpu.HBM) for s, a in zip(land_shapes, arrs)]
    out = pl.pallas_call(
        body, name=name,
        out_shape=(pltpu.SemaphoreType.DMA((ncp,)), pltpu.SemaphoreType.DMA((ncp,)), *[pltpu.HBM(a.shape, a.dtype) for a in arrs],
                   *[pltpu.HBM(s, a.dtype) for s, a in zip(land_shapes, arrs)], SDS((8, 128), f32)),
        in_specs=[HBM_SPEC] * (2 * n) + [ANY], out_specs=(SEM_SPEC, SEM_SPEC, *[HBM_SPEC] * (2 * n), pl.BlockSpec(memory_space=pltpu.VMEM)),
        input_output_aliases={i: 2 + i for i in range(2 * n)},
        compiler_params=pltpu.CompilerParams(has_side_effects=SPLIT_EFFECT),
    )(*operands, after)
    return out[0], out[1], out[2:2 + n], out[2 + n:2 + 2 * n], out[-1]


def gather_wait(name, send_sems, recv_sems, thru, lands, after, slotted=False):
    n = len(thru)

    def body(*refs):
        srcs, lands_r = refs[:n], refs[n:2 * n]
        s_sems, r_sems = refs[2 * n], refs[2 * n + 1]
        me = (lax.axis_index("x"), lax.axis_index("y"), lax.axis_index("c"))
        k = 2 * me[0] + me[1]
        for a in range(n):
            for f, flip in enumerate(CHIP_FLIPS):
                peer = tuple((1 - m) if fl else m for m, fl in zip(me, flip))
                idx = a * len(CHIP_FLIPS) + f
                src = srcs[a].at[_chip_of(me, flip)] if slotted else srcs[a]
                pltpu.make_async_remote_copy(src_ref=src, dst_ref=lands_r[a].at[k], send_sem=s_sems.at[idx], recv_sem=r_sems.at[idx],
                                             device_id=peer, device_id_type=MESH_ID).wait_send()
                pltpu.make_async_remote_copy(src_ref=src, dst_ref=lands_r[a].at[_chip_of(me, flip)], send_sem=s_sems.at[idx],
                                             recv_sem=r_sems.at[idx], device_id=peer, device_id_type=MESH_ID).wait_recv()

    out = pl.pallas_call(
        body, name=name, out_shape=tuple(pltpu.HBM(t.shape, t.dtype) for t in (*thru, *lands)),
        in_specs=[HBM_SPEC] * (2 * n) + [SEM_SPEC, SEM_SPEC, ANY], out_specs=tuple([HBM_SPEC] * (2 * n)),
        input_output_aliases={i: i for i in range(2 * n)},
        compiler_params=pltpu.CompilerParams(has_side_effects=SPLIT_EFFECT),
    )(*thru, *lands, send_sems, recv_sems, after)
    return out[:n], out[n:]


def gather_weights(w_sh):
    big = [w_sh["w_in"][0], w_sh["w_out"][0], w_sh["pool_w"][0].reshape(4 * 64, POOL_GD)]
    small_pack = _flat_rows([w_sh[n] for n in SMALL], SMALL_ROWS)
    own = [b.astype(bf16) for b in big] + [small_pack]
    my_k = 2 * lax.axis_index("x") + lax.axis_index("y")
    s_in, s_out, s_pool, s_small = [lax.dynamic_update_slice(s, o[None], (my_k, 0, 0))
                                    for s, o in zip(gather_two_level("gather_weights", own[:3], own[3:]), own)]
    Wf = {n: w_sh[n][0] for n in REPL}
    Wf["w_in"] = s_in.transpose(1, 0, 2).reshape(D, IN_DIM)
    Wf["pool_w"] = s_pool.reshape(N_CHIPS, 4, 64, POOL_GD).transpose(1, 0, 2, 3).reshape(4, POOL_GD, POOL_GD)
    Wf["wo4"] = s_out
    small_shapes = [_drop1(n, w_sh[n]).shape for n in SMALL]
    parts = [_split_rows(s_small[k], small_shapes) for k in range(N_CHIPS)]
    for j, n in enumerate(SMALL):
        Wf[n] = jnp.concatenate([parts[k][j] for k in range(N_CHIPS)], axis=SMALL_AXIS[n])
    return Wf


def pack_early(G):
    slots = [jnp.concatenate([G["wo4"][k], G["wgT4"][0][k], G["wgT4"][1][k], G["wuT4"][0][k], G["wuT4"][1][k], G["wd4"][0][k], G["wd4"][1][k],
                              G["pool_w"][:, k * 64:(k + 1) * 64, :].reshape(64, D)], axis=0) for k in range(N_CHIPS)]
    return jnp.stack(slots)


def pack_late(G):
    sg = small_grads(G)
    repl = _flat_rows([sg[n] for n in REPL], REPL_ROWS)
    w_in_t = jnp.concatenate(G["w_inT"], axis=0)
    slots = [jnp.concatenate([_flat_rows([_shard_small(n, sg[n], k) for n in SMALL], SMALL_ROWS), repl,
                              jnp.pad(w_in_t[k * IN_SH:(k + 1) * IN_SH], ((0, L_ROWS - L_END), (0, 0)))], axis=0)
             for k in range(N_CHIPS)]
    return jnp.stack(slots)


def unpack_grads(early, late, w_sh):
    g = {"w_out": early[E_OUT:E_GATE], "ffn_w_down": early[E_DOWN:E_POOL], "pool_w": early[E_POOL:E_ROWS],
         "ffn_w_gate": jnp.stack([early[E_GATE + l * FF_SH:E_GATE + (l + 1) * FF_SH].T for l in range(2)]),
         "ffn_w_up": jnp.stack([early[E_UP + l * FF_SH:E_UP + (l + 1) * FF_SH].T for l in range(2)]),
         "w_in": late[L_IN:L_END].T}
    small = _split_rows(late[L_SMALL:L_REPL], [_drop1(n, w_sh[n]).shape for n in SMALL])
    repl = _split_rows(late[L_REPL:L_IN], [w_sh[n][0].shape for n in REPL])
    g.update(zip(SMALL, small))
    g.update(zip(REPL, repl))
    return {n: g[n].reshape(w_sh[n].shape) for n in WEIGHT_NAMES}


def _split_rows(flat2d, shapes):
    v = flat2d.reshape(-1)
    out, off = [], 0
    for s in shapes:
        n = math.prod(s)
        out.append(v[off:off + n].reshape(s))
        off += n
    return out


def kernel(x, norm_g, w_in, gm_ln_g, gm_ln_b, gm_ws, gm_bs, conv_w, conv_b, dt_bias, a_log, d_skip, ssm_norm_g, w_out, pool_w, pool_b, pool_scale, ffn_w_gate, ffn_w_up, ffn_w_down, loss_target, m_norm_g, m_w_in, m_gm_ln_g, m_gm_ln_b, m_gm_ws, m_gm_bs, m_conv_w, m_conv_b, m_dt_bias, m_a_log, m_d_skip, m_ssm_norm_g, m_w_out, m_pool_w, m_pool_b, m_pool_scale, m_ffn_w_gate, m_ffn_w_up, m_ffn_w_down, v_norm_g, v_w_in, v_gm_ln_g, v_gm_ln_b, v_gm_ws, v_gm_bs, v_conv_w, v_conv_b, v_dt_bias, v_a_log, v_d_skip, v_ssm_norm_g, v_w_out, v_pool_w, v_pool_b, v_pool_scale, v_ffn_w_gate, v_ffn_w_up, v_ffn_w_down):
    T = x.shape[1]
    w_sh = dict(zip(WEIGHT_NAMES, (norm_g, w_in, gm_ln_g, gm_ln_b, gm_ws, gm_bs, conv_w, conv_b, dt_bias, a_log, d_skip, ssm_norm_g, w_out,
                                   pool_w, pool_b, pool_scale, ffn_w_gate, ffn_w_up, ffn_w_down)))
    m_sh = dict(zip(WEIGHT_NAMES, (m_norm_g, m_w_in, m_gm_ln_g, m_gm_ln_b, m_gm_ws, m_gm_bs, m_conv_w, m_conv_b, m_dt_bias, m_a_log, m_d_skip,
                                   m_ssm_norm_g, m_w_out, m_pool_w, m_pool_b, m_pool_scale, m_ffn_w_gate, m_ffn_w_up, m_ffn_w_down)))
    v_sh = dict(zip(WEIGHT_NAMES, (v_norm_g, v_w_in, v_gm_ln_g, v_gm_ln_b, v_gm_ws, v_gm_bs, v_conv_w, v_conv_b, v_dt_bias, v_a_log, v_d_skip,
                                   v_ssm_norm_g, v_w_out, v_pool_w, v_pool_b, v_pool_scale, v_ffn_w_gate, v_ffn_w_up, v_ffn_w_down)))

    my_k = 2 * lax.axis_index("x") + lax.axis_index("y")
    ffn_own = [w_sh["ffn_w_gate"].reshape(2 * D, FF_SH).astype(bf16), w_sh["ffn_w_up"].reshape(2 * D, FF_SH).astype(bf16),
               w_sh["ffn_w_down"].reshape(2 * FF_SH, D).astype(bf16)]
    Wf = gather_weights(w_sh)
    send_sems, recv_sems, thru, lands, token = gather_start("gather_ffn_start", ffn_own, Wf["wo4"])
    Wf["norm_g"] = Wf["norm_g"] + token[0, 0]
    W = build_weights(Wf)

    def ffn_weights(after):
        _, landed = gather_wait("gather_ffn_wait", send_sems, recv_sems, thru, lands, after)
        return tuple(lax.dynamic_update_slice(l, o[None], (my_k, 0, 0)) for l, o in zip(landed, ffn_own))

    my_c = lax.axis_index("c")
    c_arr = my_c.astype(jnp.int32).reshape(1)
    k_arr = my_k.astype(jnp.int32).reshape(1)

    def pair_stage(tag, packs, tile):
        got = pair_split_exchange(f"grads{tag}_pair_split", packs, packs.shape[1] // 2)
        return pair_sum(f"grads{tag}_pair_sum", packs, got, c_arr, tile)

    def chip_stage(tag, pair16, landed, tile):
        half = chip_sum(f"grads{tag}_chip_sum", pair16, landed, k_arr, tile)
        other = pair_swap(f"grads{tag}_pair_swap", half)
        return jnp.concatenate([jnp.where(my_c == 0, half, other), jnp.where(my_c == 0, other, half)], axis=0)

    early = {}

    def early_grads(Ge):
        pair16 = pair_stage("E", pack_early(Ge), E_TILE)
        s_sems, r_sems, thru, lands, tok = gather_start("gradsE_scatter_start", [pair16], jnp.zeros((8, 128), f32), slotted=True)
        early.update(s_sems=s_sems, r_sems=r_sems, thru=thru, lands=lands)
        return tok

    loss_acc, grad_x, G = local_step(T, x[0], loss_target[0], W, ffn_weights, early_grads)
    pair_l = pair_stage("L", pack_late(G), L_TILE)
    total_l = chip_stage("L", pair_l, scatter_over_chips("gradsL_scatter", pair_l), L_TILE)
    (pair_e,), (landed_e,) = gather_wait("gradsE_scatter_wait", early["s_sems"], early["r_sems"], early["thru"], early["lands"], total_l,
                                         slotted=True)
    total_e = chip_stage("E", pair_e, landed_e, E_TILE)
    grads = unpack_grads(total_e, total_l, w_sh)

    delta, new_m, new_v = {}, {}, {}
    for n in WEIGHT_NAMES:
        shp = w_sh[n].shape
        two_d = (-1, shp[-1])
        d_, m_, v_ = adamw("adamw_" + n, w_sh[n].reshape(two_d), grads[n].reshape(two_d), m_sh[n].reshape(two_d), v_sh[n].reshape(two_d))
        delta[n], new_m[n], new_v[n] = d_.reshape(shp), m_.reshape(shp), v_.reshape(shp)

    loss = lax.psum(loss_acc[0, 0], ("x", "y", "c"))
    return (loss, grad_x[None], *[grads[n] for n in WEIGHT_NAMES], *[delta[n] for n in WEIGHT_NAMES],
            *[new_m[n] for n in WEIGHT_NAMES], *[new_v[n] for n in WEIGHT_NAMES])
```

```python
import math

import jax
import jax.numpy as jnp
from jax import lax
from jax.experimental import pallas as pl
from jax.experimental.pallas import tpu as pltpu

f32, bf16 = jnp.float32, jnp.bfloat16
SDS = jax.ShapeDtypeStruct

D = 1024
EPS = 1e-6
CHUNK = 128
GM_HEADS, GM_HD = 4, 256
SSM_GROUPS, SSM_HPG, SSM_P, SSM_N = 4, 4, 64, 128
N_HEADS = SSM_GROUPS * SSM_HPG
CONV_K = 4
CONV_DIM = 2048
POOL_WINDOWS = (2, 4, 8, 16)
POOL_GD = 256
POOL_HALO = 32
CONV_HALO = 8
D_FF = 2816
DT_PAD = 128
IN_DIM = 5136

ADAM_LR, ADAM_B1, ADAM_B2, ADAM_EPS, ADAM_WD, ADAM_STEP = 0.001, 0.9, 0.999, 1e-08, 0.01, 10

NT = (((1,), (1,)), ((), ()))
TN = (((0,), (0,)), ((), ()))
NN = (((1,), (0,)), ((), ()))
HI = lax.Precision.HIGHEST
MM_SUB = 256


def _silu(x):
    return x * jax.nn.sigmoid(x)


def _softplus(x):
    return jnp.maximum(x, 0.0) + jnp.log1p(jnp.exp(-jnp.abs(x)))


def _rms(x, g):
    return x * lax.rsqrt(jnp.mean(x * x, axis=-1, keepdims=True) + EPS) * g


def _rms_bwd(x, g, dy):
    r = lax.rsqrt(jnp.mean(x * x, axis=-1, keepdims=True) + EPS)
    xh = x * r
    dxh = dy * g
    dx = r * (dxh - xh * jnp.mean(dxh * xh, axis=-1, keepdims=True))
    return dx, jnp.sum(dy * xh, axis=0, keepdims=True)


def _bdot(a, b, dims=NN):
    return lax.dot_general(a.astype(bf16), b.astype(bf16), dims, preferred_element_type=f32)


def matmul(name, pairs, mode, out_dtype, tm, tn, tk=None):
    a0, b0 = pairs[0]
    if mode == "tn":
        M, N, K = a0.shape[1], b0.shape[1], a0.shape[0]
    else:
        M, K = a0.shape
        N = b0.shape[1] if mode == "nn" else b0.shape[0]
    tm, tn = min(tm, M), min(tn, N)
    assert M % tm == 0 and N % tn == 0, (name, M, N, tm, tn)
    if tk is None:
        nk = 1
    else:
        assert len(pairs) == 1 and K % tk == 0
        nk = K // tk
    dims = {"nn": NN, "nt": NT, "tn": TN}[mode]
    in_specs, args = [], []
    for a, b in pairs:
        kk = (a.shape[0] if mode == "tn" else a.shape[1]) if tk is None else tk
        if mode == "tn":
            in_specs.append(pl.BlockSpec((kk, tm), lambda j, i, k: (k, i)))
            in_specs.append(pl.BlockSpec((kk, tn), lambda j, i, k: (k, j)))
        elif mode == "nn":
            in_specs.append(pl.BlockSpec((tm, kk), lambda j, i, k: (i, k)))
            in_specs.append(pl.BlockSpec((kk, tn), lambda j, i, k: (k, j)))
        else:
            in_specs.append(pl.BlockSpec((tm, kk), lambda j, i, k: (i, k)))
            in_specs.append(pl.BlockSpec((tn, kk), lambda j, i, k: (j, k)))
        args += [a, b]
    npairs = len(pairs)

    def kern(*refs):
        o = refs[2 * npairs]
        part = None
        for p in range(npairs):
            d = _bdot(refs[2 * p][...], refs[2 * p + 1][...], dims)
            part = d if part is None else part + d
        if nk == 1:
            o[...] = part.astype(out_dtype)
        else:
            acc = refs[2 * npairs + 1]
            k = pl.program_id(2)

            @pl.when(k == 0)
            def _():
                acc[...] = part

            @pl.when(k > 0)
            def _():
                acc[...] += part

            @pl.when(k == nk - 1)
            def _():
                o[...] = acc[...].astype(out_dtype)

    return pl.pallas_call(
        kern, name=name, grid=(N // tn, M // tm, nk),
        in_specs=in_specs, out_specs=pl.BlockSpec((tm, tn), lambda j, i, k: (i, j)),
        out_shape=SDS((M, N), out_dtype),
        scratch_shapes=[pltpu.VMEM((tm, tn), f32)] if nk > 1 else [],
        compiler_params=pltpu.CompilerParams(dimension_semantics=("parallel", "parallel", "arbitrary")),
    )(*args)


def mm(name, grid, pairs, dims, o_spec, out_shape):
    nk = grid[2]
    npairs = len(pairs)
    in_specs, args = [], []
    for a, a_spec, b, b_spec in pairs:
        in_specs += [a_spec, b_spec]
        args += [a, b]
    blk = tuple(d for d in o_spec.block_shape if d is not None)

    def kern(*refs):
        o = refs[2 * npairs]
        part = None
        for p in range(npairs):
            d = _bdot(refs[2 * p][...], refs[2 * p + 1][...], dims)
            part = d if part is None else part + d
        if nk == 1:
            o[...] = part.astype(o.dtype)
        else:
            acc = refs[2 * npairs + 1]
            k = pl.program_id(2)

            @pl.when(k == 0)
            def _():
                acc[...] = part

            @pl.when(k > 0)
            def _():
                acc[...] += part

            @pl.when(k == nk - 1)
            def _():
                o[...] = acc[...].astype(o.dtype)

    return pl.pallas_call(
        kern, name=name, grid=grid, in_specs=in_specs, out_specs=o_spec, out_shape=out_shape,
        scratch_shapes=[pltpu.VMEM(blk, f32)] if nk > 1 else [],
        compiler_params=pltpu.CompilerParams(dimension_semantics=("parallel", "parallel", "arbitrary")),
    )(*args)


def mm_fused(name, n_row_blocks, pairs, dims, extra_ins, outs, accs, epilogue):
    npairs, nx, no, na = len(pairs), len(extra_ins), len(outs), len(accs)
    in_specs, args = [], []
    for a, a_spec, b, b_spec in pairs:
        in_specs += [a_spec, b_spec]
        args += [a, b]
    for arr, spec in extra_ins:
        in_specs.append(spec)
        args.append(arr)

    rows_blk = outs[0][1].block_shape[0]
    sub = min(rows_blk, MM_SUB)

    def kern(*refs):
        x_refs = refs[2 * npairs:2 * npairs + nx]
        o_refs = refs[2 * npairs + nx:2 * npairs + nx + no]
        a_refs = refs[2 * npairs + nx + no:]
        if na:
            @pl.when(pl.program_id(0) == 0)
            def _():
                for a in a_refs:
                    a[...] = jnp.zeros(a.shape, f32)
        for r0 in range(0, rows_blk, sub):
            rows = pl.ds(r0, sub)
            part = None
            for p in range(npairs):
                d = _bdot(refs[2 * p][rows, :], refs[2 * p + 1][...], dims)
                part = d if part is None else part + d
            epilogue(part, [x.at[rows, :] if x.shape[0] == rows_blk else x for x in x_refs], [o.at[rows, :] for o in o_refs], a_refs)

    return pl.pallas_call(
        kern, name=name, grid=(n_row_blocks,), in_specs=in_specs,
        out_specs=[spec for _, spec in outs] + [pl.BlockSpec(tuple(s), lambda i, nd=len(s): (0,) * nd) for s in accs],
        out_shape=[s for s, _ in outs] + [SDS(tuple(s), f32) for s in accs],
        compiler_params=pltpu.CompilerParams(dimension_semantics=("arbitrary",)),
    )(*args)


FF_SH = D_FF // 4


def ffn_up(name, T, tm, n_bf, wg4, wu4, l):
    sub = min(tm, MM_SUB)

    def kern(n_ref, wg_ref, wu_ref, g_ref, u_ref, a_ref):
        for r0 in range(0, tm, sub):
            rows = pl.ds(r0, sub)
            n = n_ref[rows, :]
            g = jnp.dot(n, wg_ref[...], preferred_element_type=f32)
            u = jnp.dot(n, wu_ref[...], preferred_element_type=f32)
            g_ref[rows, :] = g.astype(bf16)
            u_ref[rows, :] = u.astype(bf16)
            a_ref[rows, :] = (_silu(g) * u).astype(bf16)
    w_spec = pl.BlockSpec((None, D, FF_SH), lambda k, i: (k, l, 0))
    o_spec = pl.BlockSpec((None, tm, FF_SH), lambda k, i: (k, i, 0))
    s = SDS((4, T, FF_SH), bf16)
    return pl.pallas_call(kern, name=name, grid=(4, T // tm), in_specs=[pl.BlockSpec((tm, D), lambda k, i: (i, 0)), w_spec, w_spec],
                          out_specs=[o_spec] * 3, out_shape=[s, s, s],
                          compiler_params=pltpu.CompilerParams(dimension_semantics=("parallel", "parallel")))(n_bf, wg4, wu4)


def ffn_dgu(name, T, tm, d_f, wd4, gate4, up4, l):
    rc = 16

    sub = min(tm, MM_SUB)

    def kern(df_ref, wd_ref, g_ref, u_ref, dg_ref, du_ref, dact_ref):
        for s0 in range(0, tm, sub):
            dact_ref[pl.ds(s0, sub), :] = _bdot(df_ref[pl.ds(s0, sub), :], wd_ref[...], NT)
            for r0 in range(s0, s0 + sub, rc):
                rows = pl.ds(r0, rc)
                _, vjp = jax.vjp(lambda a, b: _silu(a) * b, g_ref[rows, :].astype(f32), u_ref[rows, :].astype(f32))
                dg, du = vjp(dact_ref[rows, :])
                dg_ref[rows, :] = dg.astype(bf16)
                du_ref[rows, :] = du.astype(bf16)
    a_spec = pl.BlockSpec((None, tm, FF_SH), lambda k, i: (k, i, 0))
    s = SDS((4, T, FF_SH), bf16)
    return pl.pallas_call(kern, name=name, grid=(4, T // tm),
                          in_specs=[pl.BlockSpec((tm, D), lambda k, i: (i, 0)), pl.BlockSpec((None, FF_SH, D), lambda k, i: (k, l, 0)), a_spec, a_spec],
                          out_specs=[a_spec] * 2, out_shape=[s, s], scratch_shapes=[pltpu.VMEM((tm, FF_SH), f32)],
                          compiler_params=pltpu.CompilerParams(dimension_semantics=("parallel", "parallel")))(d_f, wd4, gate4, up4)


def rowcall(name, body, T, tm, ins, outs, accs=(), scratch=(), reverse=False, sub=None):
    n = T // tm
    assert T % tm == 0

    def blk(i):
        return (n - 1 - i) if reverse else i

    in_specs, args = [], []
    for spec in ins:
        kind, arr = spec[0], spec[1]
        if kind == "row":
            _, _, w, cb = spec
            in_specs.append(pl.BlockSpec((tm, w), lambda i, cb=cb: (blk(i), cb)))
        elif kind == "prev":
            _, _, w, cb, h = spec
            r = tm // h
            in_specs.append(pl.BlockSpec((h, w), lambda i, cb=cb, r=r: (jnp.maximum(blk(i) * r - 1, 0), cb)))
        elif kind == "next":
            _, _, w, cb, h = spec
            r = tm // h
            in_specs.append(pl.BlockSpec((h, w), lambda i, cb=cb, r=r, h=h: (jnp.minimum((blk(i) + 1) * r, T // h - 1), cb)))
        else:
            nd = arr.ndim
            in_specs.append(pl.BlockSpec(arr.shape, lambda i, nd=nd: (0,) * nd))
        args.append(arr)
    out_shape = [SDS((T, w), dt) for w, dt in outs] + [SDS(tuple(s), f32) for s in accs]
    out_specs = [pl.BlockSpec((tm, w), lambda i: (blk(i), 0)) for w, _ in outs]
    out_specs += [pl.BlockSpec(tuple(s), lambda i, nd=len(s): (0,) * nd) for s in accs]
    ni, no, na = len(ins), len(outs), len(accs)

    def kern(*refs):
        i = pl.program_id(0)
        in_refs, out_refs = refs[:ni], refs[ni:ni + no]
        acc_refs, scr = refs[ni + no:ni + no + na], refs[ni + no + na:]
        if na:
            @pl.when(i == 0)
            def _():
                for a in acc_refs:
                    a[...] = jnp.zeros(a.shape, f32)
        if sub is None or sub >= tm:
            body(blk(i), n, in_refs, out_refs, acc_refs, scr)
        else:
            for r0 in range(0, tm, sub):
                rows = pl.ds(r0, sub)
                body(blk(i), n, [r.at[rows, :] if spec[0] == "row" else r for r, spec in zip(in_refs, ins)],
                     [o.at[rows, :] for o in out_refs], acc_refs, [s.at[rows, :] for s in scr])

    res = pl.pallas_call(
        kern, name=name, grid=(n,), in_specs=in_specs, out_specs=out_specs, out_shape=out_shape,
        scratch_shapes=list(scratch),
        compiler_params=pltpu.CompilerParams(dimension_semantics=("arbitrary",)),
    )(*args)
    return res


def rms_to_bf16(name, T, tm, x, g):
    def body(i, n, ins, outs, accs, scr):
        outs[0][...] = _rms(ins[0][...], ins[1][...]).astype(bf16)
    return rowcall(name, body, T, tm, [("row", x, D, 0), ("const", g)], [(D, bf16)], sub=64)[0]


def _layer_norm_parts(x):
    mu = jnp.mean(x, axis=-1, keepdims=True)
    xc = x - mu
    r = lax.rsqrt(jnp.mean(xc * xc, axis=-1, keepdims=True) + EPS)
    return xc * r, r


def gmlp_fwd(name, T, tm, uvz, ln_g, ln_b, wm, bs):
    def body(i, n, ins, outs, accs, scr):
        gu = jax.nn.gelu(ins[0][...])
        xh, _ = _layer_norm_parts(jax.nn.gelu(ins[1][...]))
        vln = (xh * ins[2][...] + ins[3][...]).astype(bf16)
        for c in range(ins[0].shape[0] // CHUNK):
            rows = slice(c * CHUNK, (c + 1) * CHUNK)
            for h in range(GM_HEADS):
                cols = slice(h * GM_HD, (h + 1) * GM_HD)
                mixed = jnp.dot(ins[4][h], vln[rows, cols], preferred_element_type=f32) + ins[5][h]
                outs[0][rows, cols] = (gu[rows, cols] * mixed).astype(bf16)
    return rowcall(name, body, T, tm, [("row", uvz, D, 0), ("row", uvz, D, 1), ("const", ln_g), ("const", ln_b), ("const", wm), ("const", bs)],
                   [(D, bf16)], sub=CHUNK)[0]


def gmlp_bwd(name, T, tm, uvz, d_ya, d_cb, ln_g, ln_b, wm, bs):
    def body(i, n, ins, outs, accs, scr):
        u, v, dya = ins[0][...], ins[1][...], ins[2][...]
        gu, gelu_u_vjp = jax.vjp(jax.nn.gelu, u)
        gv, gelu_v_vjp = jax.vjp(jax.nn.gelu, v)
        xh, r = _layer_norm_parts(gv)
        lng = ins[3][...]
        vln = (xh * lng + ins[4][...]).astype(bf16)
        rr = lax.broadcasted_iota(jnp.int32, (CHUNK, CHUNK), 0)
        cc = lax.broadcasted_iota(jnp.int32, (CHUNK, CHUNK), 1)
        causal = (rr >= cc).astype(f32)
        dvln_ref = scr[0]
        dgu_ref = scr[1]
        for c in range(ins[0].shape[0] // CHUNK):
            rows = slice(c * CHUNK, (c + 1) * CHUNK)
            for h in range(GM_HEADS):
                cols = slice(h * GM_HD, (h + 1) * GM_HD)
                w = ins[5][h]
                blk = vln[rows, cols]
                mixed = jnp.dot(w, blk, preferred_element_type=f32) + ins[6][h]
                dy = dya[rows, cols]
                dgu_ref[rows, cols] = dy * mixed
                dm = dy * gu[rows, cols]
                accs[3][h] += jnp.sum(dm, axis=1, keepdims=True)
                accs[2][h] += _bdot(dm, blk, NT) * causal
                dvln_ref[rows, cols] = _bdot(w, dm, TN)
        dvln = dvln_ref[...]
        accs[0][...] += jnp.sum(dvln * xh, axis=0, keepdims=True)
        accs[1][...] += jnp.sum(dvln, axis=0, keepdims=True)
        dxh = dvln * lng
        dgv = r * (dxh - jnp.mean(dxh, axis=-1, keepdims=True) - xh * jnp.mean(dxh * xh, axis=-1, keepdims=True))
        outs[0][...] = gelu_u_vjp(dgu_ref[...])[0].astype(bf16)
        outs[1][...] = gelu_v_vjp(dgv)[0].astype(bf16)
    return rowcall(name, body, T, tm,
                   [("row", uvz, D, 0), ("row", uvz, D, 1), ("row", d_ya, D, d_cb), ("const", ln_g), ("const", ln_b), ("const", wm), ("const", bs)],
                   [(D, bf16), (D, bf16)], accs=[(1, D), (1, D), (GM_HEADS, CHUNK, CHUNK), (GM_HEADS, CHUNK, 1)],
                   scratch=[pltpu.VMEM((tm, D), f32), pltpu.VMEM((tm, D), f32)], sub=CHUNK)


CONV_RC, CONV_LB = 32, 512


def _conv_fill(i, x_ref, halo_ref, scr, tm):
    scr[pl.ds(0, CONV_HALO), :] = jnp.where(i > 0, halo_ref[...], 0.0)
    scr[pl.ds(CONV_HALO, tm), :] = x_ref[...]


def _conv_taps(scr, r0, lanes):
    return [scr[pl.ds(r0 + CONV_HALO - (CONV_K - 1) + k, CONV_RC), lanes] for k in range(CONV_K)]


def conv_fwd(name, T, tm, xbc, conv_w, conv_b):
    def body(i, n, ins, outs, accs, scr):
        s = scr[0]
        _conv_fill(i, ins[0], ins[1], s, tm)
        for lb in range(CONV_DIM // CONV_LB):
            lanes = slice(lb * CONV_LB, (lb + 1) * CONV_LB)
            w, b = ins[2][:, lanes], ins[3][:, lanes]

            for r0 in range(0, tm, CONV_RC):
                taps = _conv_taps(s, r0, lanes)
                pre = b + sum(w[k:k + 1] * taps[k] for k in range(CONV_K))
                outs[0][pl.ds(r0, CONV_RC), lanes] = _silu(pre)
    return rowcall(name, body, T, tm, [("row", xbc, CONV_DIM, 0), ("prev", xbc, CONV_DIM, 0, CONV_HALO), ("const", conv_w), ("const", conv_b)],
                   [(CONV_DIM, f32)], scratch=[pltpu.VMEM((tm + CONV_HALO, CONV_DIM), f32)])[0]


def conv_bwd_pre(name, T, tm, xbc, d_xc, conv_w, conv_b):
    def body(i, n, ins, outs, accs, scr):
        s = scr[0]
        _conv_fill(i, ins[0], ins[1], s, tm)
        fold = lambda v: jnp.sum(v.reshape(CONV_RC // 8, 8, CONV_LB), axis=0)
        for lb in range(CONV_DIM // CONV_LB):
            lanes = slice(lb * CONV_LB, (lb + 1) * CONV_LB)
            w, b = ins[3][:, lanes], ins[4][:, lanes]

            sums = [jnp.zeros((8, CONV_LB), f32)] * (CONV_K + 1)
            for r0 in range(0, tm, CONV_RC):
                taps = _conv_taps(s, r0, lanes)
                pre = b + sum(w[k:k + 1] * taps[k] for k in range(CONV_K))
                _, vjp = jax.vjp(_silu, pre)
                dpre = vjp(ins[2][pl.ds(r0, CONV_RC), lanes])[0]
                outs[0][pl.ds(r0, CONV_RC), lanes] = dpre
                sums = [sums[k] + fold(dpre * taps[k]) for k in range(CONV_K)] + [sums[CONV_K] + fold(dpre)]
            for k in range(CONV_K):
                accs[0][pl.ds(k, 1), lanes] += jnp.sum(sums[k], axis=0, keepdims=True)
            accs[1][:, lanes] += jnp.sum(sums[CONV_K], axis=0, keepdims=True)
    return rowcall(name, body, T, tm,
                   [("row", xbc, CONV_DIM, 0), ("prev", xbc, CONV_DIM, 0, CONV_HALO), ("row", d_xc, CONV_DIM, 0), ("const", conv_w), ("const", conv_b)],
                   [(CONV_DIM, f32)], accs=[(CONV_K, CONV_DIM), (1, CONV_DIM)], scratch=[pltpu.VMEM((tm + CONV_HALO, CONV_DIM), f32)])


def conv_bwd_x(name, T, tm, d_pre, conv_w):
    def body(i, n, ins, outs, accs, scr):
        s = scr[0]
        s[pl.ds(0, tm), :] = ins[0][...]
        s[pl.ds(tm, CONV_HALO), :] = jnp.where(i < n - 1, ins[1][...], 0.0)
        for lb in range(CONV_DIM // CONV_LB):
            lanes = slice(lb * CONV_LB, (lb + 1) * CONV_LB)
            w = ins[2][:, lanes]

            for r0 in range(0, tm, CONV_RC):
                dx = sum(w[k:k + 1] * s[pl.ds(r0 + CONV_K - 1 - k, CONV_RC), lanes] for k in range(CONV_K))
                outs[0][pl.ds(r0, CONV_RC), lanes] = dx.astype(bf16)
    return rowcall(name, body, T, tm, [("row", d_pre, CONV_DIM, 0), ("next", d_pre, CONV_DIM, 0, CONV_HALO), ("const", conv_w)],
                   [(CONV_DIM, bf16)], scratch=[pltpu.VMEM((tm + CONV_HALO, CONV_DIM), f32)])[0]


def _ssd_prep(dtr, dtb, alog):
    rr = lax.broadcasted_iota(jnp.int32, (CHUNK, CHUNK), 0)
    cc = lax.broadcasted_iota(jnp.int32, (CHUNK, CHUNK), 1)
    dt = _softplus(dtr + dtb)
    dA = dt * -jnp.exp(alog)
    acum = jnp.dot((rr >= cc).astype(f32), dA, precision=HI, preferred_element_type=f32)
    return dt, acum, acum.T, jnp.sum(dA, axis=0, keepdims=True)


def _ssd_group(g, x, Bm, Cm, S, dt, acum, acumT, tot, dsk):
    rr = lax.broadcasted_iota(jnp.int32, (CHUNK, CHUNK), 0)
    cc = lax.broadcasted_iota(jnp.int32, (CHUNK, CHUNK), 1)
    tril = rr >= cc
    lane = lax.broadcasted_iota(jnp.int32, (1, DT_PAD), 1)
    sub = lax.broadcasted_iota(jnp.int32, (DT_PAD, 1), 0)
    glane = lax.broadcasted_iota(jnp.int32, (1, SSM_HPG * SSM_P), 1) // SSM_P
    hm = [(glane == r).astype(f32) for r in range(SSM_HPG)]
    pick = lambda v, r: jnp.sum(v * (lane == SSM_HPG * g + r).astype(f32), axis=1, keepdims=True)
    cols = [pick(acum, r) for r in range(SSM_HPG)]
    tots = [pick(tot, r) for r in range(SSM_HPG)]
    spread = lambda vals: sum(vals[r] * hm[r] for r in range(SSM_HPG))
    xdt = x * spread([pick(dt, r) for r in range(SSM_HPG)])
    cb = _bdot(Cm, Bm, NT)
    y = x * spread([pick(dsk, r) for r in range(SSM_HPG)])
    for r in range(SSM_HPG):
        row = jnp.sum(acumT * (sub == SSM_HPG * g + r).astype(f32), axis=0, keepdims=True)
        dec = jnp.exp(jnp.where(tril, cols[r] - row, -jnp.inf))
        y = y + _bdot(cb * dec, xdt * hm[r])
    y = y + _bdot(Cm, S) * spread([jnp.exp(c) for c in cols])
    dte = spread([jnp.exp(tots[r] - cols[r]) for r in range(SSM_HPG)])
    s_new = S * spread([jnp.exp(t) for t in tots]) + _bdot(Bm, xdt * dte, TN)
    return y, s_new


def _ssd_ins(xc, dtr):
    gw = SSM_HPG * SSM_P
    ins = [("row", xc, gw, g) for g in range(SSM_GROUPS)]
    ins += [("row", xc, SSM_N, D // SSM_N + g) for g in range(SSM_GROUPS)]
    ins += [("row", xc, SSM_N, D // SSM_N + SSM_GROUPS + g) for g in range(SSM_GROUPS)]
    ins += [("row", dtr, DT_PAD, 0)]
    return ins


SSD_CPS = 2


def ssd_fwd(name, T, xc, dtr, dtb, alog, dsk):
    gw = SSM_HPG * SSM_P

    def body(i, n, ins, outs, accs, scr):
        S = scr[0]

        @pl.when(i == 0)
        def _():
            S[...] = jnp.zeros(S.shape, f32)
        S4 = tuple(S[:, g * gw:(g + 1) * gw] for g in range(4))
        for c in range(SSD_CPS):
            rows = pl.ds(c * CHUNK, CHUNK)
            X4 = tuple(ins[g][rows, :] for g in range(4))
            B4 = tuple(ins[4 + g][rows, :] for g in range(4))
            C4 = tuple(ins[8 + g][rows, :] for g in range(4))
            prep = _ssd_prep(ins[12][rows, :], ins[13][...], ins[14][...])
            nxt = []
            for g in range(4):
                outs[1][rows, g * gw:(g + 1) * gw] = S4[g]
                y, s_new = _ssd_group(g, X4[g], B4[g], C4[g], S4[g], *prep, ins[15][...])
                outs[0][rows, g * gw:(g + 1) * gw] = y
                nxt.append(s_new)
            S4 = tuple(nxt)
        for g in range(4):
            S[:, g * gw:(g + 1) * gw] = S4[g]
    ins = _ssd_ins(xc, dtr) + [("const", dtb), ("const", alog), ("const", dsk)]
    return rowcall(name, body, T, SSD_CPS * CHUNK, ins, [(D, f32), (D, f32)], scratch=[pltpu.VMEM((SSM_N, D), f32)])


def ssd_bwd(name, T, xc, dtr, sprev, d_y, dtb, alog, dsk):
    gw = SSM_HPG * SSM_P

    def body(i, n, ins, outs, accs, scr):
        dS = scr[0]

        @pl.when(i == n - 1)
        def _():
            dS[...] = jnp.zeros(dS.shape, f32)
        dS4 = tuple(dS[:, g * gw:(g + 1) * gw] for g in range(4))
        def chunk(X4, dtr_c, B4, C4, S4, dtb_c, alog_c, dsk_c):
            prep = _ssd_prep(dtr_c, dtb_c, alog_c)
            res = [_ssd_group(g, X4[g], B4[g], C4[g], S4[g], *prep, dsk_c) for g in range(4)]
            return tuple(r[0] for r in res), tuple(r[1] for r in res)
        X4 = tuple(ins[g][...] for g in range(4))
        B4 = tuple(ins[4 + g][...] for g in range(4))
        C4 = tuple(ins[8 + g][...] for g in range(4))
        S4 = tuple(ins[13 + g][...] for g in range(4))
        dY4 = tuple(ins[17 + g][...] for g in range(4))
        _, vjp = jax.vjp(chunk, X4, ins[12][...], B4, C4, S4, ins[21][...], ins[22][...], ins[23][...])
        dX4, ddtr, dB4, dC4, dS4, ddtb, dalog, ddsk = vjp((dY4, dS4))
        for g in range(4):
            outs[0][:, g * gw:(g + 1) * gw] = dX4[g]
            outs[0][:, D + g * SSM_N:D + (g + 1) * SSM_N] = dB4[g]
            outs[0][:, D + (SSM_GROUPS + g) * SSM_N:D + (SSM_GROUPS + g + 1) * SSM_N] = dC4[g]
            dS[:, g * gw:(g + 1) * gw] = dS4[g]
        outs[1][...] = ddtr.astype(bf16)
        accs[0][...] += ddtb
        accs[1][...] += dalog
        accs[2][...] += ddsk
    ins = _ssd_ins(xc, dtr) + [("row", sprev, gw, g) for g in range(4)] + [("row", d_y, gw, g) for g in range(4)]
    ins += [("const", dtb), ("const", alog), ("const", dsk)]
    return rowcall(name, body, T, CHUNK, ins, [(CONV_DIM, f32), (DT_PAD, bf16)], accs=[(1, DT_PAD)] * 3,
                   scratch=[pltpu.VMEM((SSM_N, D), f32)], reverse=True)


def _gate_group(y, z, g):
    return _rms(y * _silu(z), g)


def gate_fwd(name, T, tm, y, uvz, gn):
    def body(i, n, ins, outs, accs, scr):
        for g in range(SSM_GROUPS):
            cols = slice(g * 256, (g + 1) * 256)
            outs[0][:, cols] = _gate_group(ins[0][:, cols], ins[1][:, cols], ins[2][:, cols]).astype(bf16)
    return rowcall(name, body, T, tm, [("row", y, D, 0), ("row", uvz, D, 2), ("const", gn)], [(D, bf16)], sub=64)[0]


def gate_bwd(name, T, tm, y, uvz, d_yb, d_cb, gn):
    def body(i, n, ins, outs, accs, scr):
        for g in range(SSM_GROUPS):
            cols = slice(g * 256, (g + 1) * 256)
            _, vjp = jax.vjp(_gate_group, ins[0][:, cols], ins[1][:, cols], ins[3][:, cols])
            dy, dz, dg = vjp(ins[2][:, cols])
            outs[0][:, cols] = dy
            outs[1][:, cols] = dz.astype(bf16)
            accs[0][:, cols] += dg
    return rowcall(name, body, T, tm, [("row", y, D, 0), ("row", uvz, D, 2), ("row", d_yb, D, d_cb), ("const", gn)],
                   [(D, f32), (D, bf16)], accs=[(1, D)], sub=64)


def _window_sum(src, cols, levels, tm, lv, trailing):
    cur, cur_cols = src, cols
    for l in range(1, levels + 1):
        shift = 2 ** (l - 1)
        last = l == levels
        if trailing:
            start = POOL_HALO if last else 8 * l
            rows = tm if last else tm + POOL_HALO - start
            new = cur[pl.ds(start, rows), cur_cols] + cur[pl.ds(start - shift, rows), cur_cols]
        else:
            start = 0
            rows = tm if last else tm + POOL_HALO - 8 * l
            new = cur[pl.ds(0, rows), cur_cols] + cur[pl.ds(shift, rows), cur_cols]
        if last:
            return new
        nxt = lv[l % 2]
        nxt[pl.ds(start, rows), :] = new
        cur, cur_cols = nxt, slice(None)


def _pool_diff(i, tm, h_ref, halo_ref, g_ref, scr, lv):
    g = g_ref[...]
    yn = _rms(h_ref[...], g)
    scr[pl.ds(0, POOL_HALO), :] = jnp.where(i > 0, _rms(halo_ref[...], g), 0.0)
    scr[pl.ds(POOL_HALO, tm), :] = yn
    pos = (i * tm + lax.broadcasted_iota(jnp.int32, (tm, 1), 0) + 1).astype(f32)
    parts = []
    for gi, win in enumerate(POOL_WINDOWS):
        cols = slice(gi * POOL_GD, (gi + 1) * POOL_GD)
        s = _window_sum(scr, cols, gi + 1, tm, lv, True)
        parts.append(s * (1.0 / jnp.minimum(pos, float(win))) - yn[:, cols])
    return parts


def pool_fwd(name, T, tm, h2, g_pre, pw, pb, psc, g_post, g_next):
    def body(i, n, ins, outs, accs, scr):
        parts = _pool_diff(i, tm, ins[0], ins[1], ins[2], scr[0], scr[1:3])
        for gi in range(len(POOL_WINDOWS)):
            cols = slice(gi * POOL_GD, (gi + 1) * POOL_GD)
            o = _bdot(parts[gi], ins[3][gi]) + ins[4][:, cols]
            outs[0][:, cols] = o * ins[5][:, cols]
        h = ins[0][...] + _rms(outs[0][...], ins[6][...])
        outs[1][...] = h
        outs[2][...] = _rms(h, ins[7][...]).astype(bf16)
    return rowcall(name, body, T, tm, [("row", h2, D, 0), ("prev", h2, D, 0, POOL_HALO), ("const", g_pre), ("const", pw), ("const", pb), ("const", psc),
                                       ("const", g_post), ("const", g_next)],
                   [(D, f32), (D, f32), (D, bf16)], scratch=[pltpu.VMEM((tm + POOL_HALO, D), f32)] + [pltpu.VMEM((tm + POOL_HALO, POOL_GD), f32)] * 2)


def pool_bwd(name, T, tm, h2, d_pm, d_res, g_pre, pw, pb, psc, f_prev, g_prev):
    def body(i, n, ins, outs, accs, scr):
        parts = _pool_diff(i, tm, ins[0], ins[1], ins[5], scr[0], scr[3:5])
        dpm = ins[2][...]
        psc_v = ins[8][...]
        dps = dpm * psc_v
        dps_halo = jnp.where(i < n - 1, ins[3][...] * psc_v, 0.0)
        accs[1][...] += jnp.sum(dps, axis=0, keepdims=True)
        pos = (i * tm + lax.broadcasted_iota(jnp.int32, (tm, 1), 0) + 1).astype(f32)
        pos_h = ((i + 1) * tm + lax.broadcasted_iota(jnp.int32, (POOL_HALO, 1), 0) + 1).astype(f32)
        r_scr = scr[1]
        dyn_scr = scr[2]
        for gi, win in enumerate(POOL_WINDOWS):
            cols = slice(gi * POOL_GD, (gi + 1) * POOL_GD)
            w = ins[6][gi]
            o = _bdot(parts[gi], w) + ins[7][:, cols]
            accs[2][:, cols] += jnp.sum(dpm[:, cols] * o, axis=0, keepdims=True)
            accs[0][gi] += _bdot(parts[gi], dps[:, cols], TN)
            q = _bdot(dps[:, cols], w, NT)
            qh = _bdot(dps_halo[:, cols], w, NT)
            r_scr[pl.ds(0, tm), cols] = q * (1.0 / jnp.minimum(pos, float(win)))
            r_scr[pl.ds(tm, POOL_HALO), cols] = qh * (1.0 / jnp.minimum(pos_h, float(win)))
            dyn_scr[:, cols] = _window_sum(r_scr, cols, gi + 1, tm, scr[3:5], False) - q
        dx, dg = _rms_bwd(ins[0][...], ins[5][...], dyn_scr[...])
        dh = ins[4][...] + dx
        outs[0][...] = dh
        accs[3][...] += dg
        df, dgp = _rms_bwd(ins[9][...], ins[10][...], dh)
        outs[1][...] = df.astype(bf16)
        accs[4][...] += dgp
    ins = [("row", h2, D, 0), ("prev", h2, D, 0, POOL_HALO), ("row", d_pm, D, 0), ("next", d_pm, D, 0, POOL_HALO), ("row", d_res, D, 0),
           ("const", g_pre), ("const", pw), ("const", pb), ("const", psc), ("row", f_prev, D, 0), ("const", g_prev)]
    return rowcall(name, body, T, tm, ins, [(D, f32), (D, bf16)], accs=[(4, POOL_GD, POOL_GD), (1, D), (1, D), (1, D), (1, D)],
                   scratch=[pltpu.VMEM((tm + POOL_HALO, D), f32), pltpu.VMEM((tm + POOL_HALO, D), f32), pltpu.VMEM((tm, D), f32)]
                   + [pltpu.VMEM((tm + POOL_HALO, POOL_GD), f32)] * 2)


def local_step(T, x, tgt, W, ffn_weights, early_grads):
    tm = 512 if T >= 1024 else T // 2
    TKW = 4096 if T >= 4096 else T
    ng = W["norm_g"]
    g = lambda l, j: ng[l, j][None, :]
    G = {}

    tf = tm
    once = pl.Buffered(1)
    vec_f = pl.BlockSpec((1, D), lambda i: (0, 0))

    def fused_specs(t):
        rows = pl.BlockSpec((t, D), lambda i: (i, 0))
        return rows, [pl.BlockSpec((None, t, FF_SH), lambda i, s=s: (s, i, 0)) for s in range(4)], (SDS((T, D), f32), rows), (SDS((T, D), bf16), rows)
    rows_f, sh_f, out_f32, out_bf16 = fused_specs(tf)
    tf2 = min(T, 2 * tm)
    rows_f2, sh_f2, out2_f32, out2_bf16 = fused_specs(tf2)

    def resid_epilogue(with_pre):
        def ep(part, xs, os, accs):
            h = xs[0][...] + _rms(part, xs[1][...])
            os[0][...] = part
            os[1][...] = h
            if with_pre:
                os[2][...] = _rms(h, xs[2][...]).astype(bf16)
        return ep

    def bwd_epilogue(df_dtype):
        def ep(part, xs, os, accs):
            dx, dgp = _rms_bwd(xs[0][...], xs[3][...], part)
            dh = xs[2][...] + dx
            df, dgq = _rms_bwd(xs[1][...], xs[4][...], dh)
            os[0][...] = dh
            os[1][...] = df.astype(df_dtype)
            accs[0][...] += dgp
            accs[1][...] += dgq
        return ep

    def loss_epilogue(part, xs, os, accs):
        g_post = xs[2][...]
        e = xs[0][...] + _rms(part, g_post) - xs[1][...]
        accs[0][...] += jnp.sum(jnp.sum(e * e, axis=-1, keepdims=True) * (0.5 / D), axis=0, keepdims=True)
        dh = e * (1.0 / D)
        df, dg = _rms_bwd(part, g_post, dh)
        os[0][...] = dh
        os[1][...] = df.astype(bf16)
        accs[1][...] += dg

    def ffn_fwd(tag, n_bf, l, resid=None, loss=None):
        gate4, up4, act4 = ffn_up(f"ffn{tag}_up", T, min(T, 4 * tm), n_bf, W["wg4"], W["wu4"], l)
        wd_f = [pl.BlockSpec((None, FF_SH, D), lambda i, s=s: (s, l, 0), pipeline_mode=once) for s in range(4)]
        pairs = [(act4, sh_f2[s], W["wd4"], wd_f[s]) for s in range(4)]
        if loss is not None:
            return (gate4, up4, act4) + tuple(mm_fused(f"ffn{tag}_down", T // tf2, pairs, NN, [(loss[0], rows_f2), (loss[1], rows_f2), (loss[2], vec_f)],
                                                       [out2_f32, out2_bf16], [(1, 1), (1, D)], loss_epilogue))
        f, h_out = mm_fused(f"ffn{tag}_down", T // tf2, pairs, NN, [(resid[0], rows_f2), (resid[1], vec_f)], [out2_f32, out2_f32], [],
                            resid_epilogue(False))
        return gate4, up4, act4, f, h_out

    def ffn_bwd(tag, l, n_bf, gate4, up4, act4, d_f, h_out, f_pre, d_res, g_pre, g_post, df_dtype):
        d_gate4, d_up4 = ffn_dgu(f"ffn{tag}_dgu", T, min(T, 4 * tm), d_f, W["wd4"], gate4, up4, l)
        w_f = [pl.BlockSpec((None, D, FF_SH), lambda i, s=s: (s, l, 0), pipeline_mode=once) for s in range(4)]
        d_h, d_fp, dgp, dgq = mm_fused(
            f"ffn{tag}_dn", T // tf, [(d_gate4, sh_f[s], W["wg4"], w_f[s]) for s in range(4)] + [(d_up4, sh_f[s], W["wu4"], w_f[s]) for s in range(4)],
            NT, [(h_out, rows_f), (f_pre, rows_f), (d_res, rows_f), (g_pre, vec_f), (g_post, vec_f)],
            [out_f32, (SDS((T, D), df_dtype), rows_f)], [(1, D), (1, D)], bwd_epilogue(df_dtype))

        def wgrad(nm, a4, b):
            return mm(nm, (4, 1, T // TKW),
                      [(a4, pl.BlockSpec((None, TKW, FF_SH), lambda s, j, k: (s, k, 0)), b, pl.BlockSpec((TKW, D), lambda s, j, k: (k, 0)))],
                      TN, pl.BlockSpec((None, FF_SH, D), lambda s, j, k: (s, 0, 0)), SDS((4, FF_SH, D), f32))
        return d_h, d_fp, dgp, dgq, wgrad(f"ffn{tag}_dwg", d_gate4, n_bf), wgrad(f"ffn{tag}_dwu", d_up4, n_bf), wgrad(f"ffn{tag}_dwd", act4, d_f)

    y0 = rms_to_bf16("l0_prenorm", T, tm, x, g(0, 0))
    uvz = matmul("in_uvz", [(y0, W["w_uvz"])], "nn", f32, 4 * tm, 1024)
    xbc = matmul("in_xbc", [(y0, W["w_xbc"])], "nn", f32, 4 * tm, 1024)
    dtr = matmul("in_dt", [(y0, W["w_dt"])], "nn", f32, 4 * tm, DT_PAD)
    y_a = gmlp_fwd("gmlp_fwd", T, tm, uvz, W["ln_g"], W["ln_b"], W["wm"], W["bs"])
    xc = conv_fwd("conv_fwd", T, tm, xbc, W["conv_w"], W["conv_b"])
    y_ssd, sprev = ssd_fwd("ssd_fwd", T, xc, dtr, W["dtb"], W["alog"], W["dsk"])
    y_b = gate_fwd("gate_fwd", T, tm, y_ssd, uvz, W["gn"])
    half = D // 2
    wo4 = W["wo4"]
    ycol = [pl.BlockSpec((tf2, half), lambda i, cb=cb: (i, cb)) for cb in range(2)]
    wo_s = [pl.BlockSpec((None, half, D), lambda i, s=s: (s, 0, 0), pipeline_mode=once) for s in range(4)]
    mixo, h1, n1 = mm_fused("out_proj", T // tf2, [(y_a, ycol[0], wo4, wo_s[0]), (y_a, ycol[1], wo4, wo_s[1]),
                                                  (y_b, ycol[0], wo4, wo_s[2]), (y_b, ycol[1], wo4, wo_s[3])], NN,
                            [(x, rows_f2), (g(0, 1), vec_f), (g(0, 2), vec_f)], [out2_f32, out2_f32, out2_bf16], [], resid_epilogue(True))
    W = dict(W)
    W["wg4"], W["wu4"], W["wd4"] = ffn_weights(h1)
    gate0, up0, act0, f1, h2 = ffn_fwd("0", n1, 0, resid=(h1, g(0, 3)))
    pm, h3, n3 = pool_fwd("pool_fwd", T, tm, h2, g(1, 0), W["pool_w"], W["pool_b"], W["pool_scale"], g(1, 1), g(1, 2))
    gate1, up1, act1, dh4, d_f2, loss_acc, dg13 = ffn_fwd("1", n3, 1, loss=(h3, tgt, g(1, 3)))
    d_h3, d_pm, dg12, dg11, dwg1, dwu1, dwd1 = ffn_bwd("1", 1, n3, gate1, up1, act1, d_f2, h3, pm, dh4, g(1, 2), g(1, 1), f32)
    d_h2, d_f1, G["pool_w"], G["pool_b"], G["pool_scale"], dg10, dg03 = pool_bwd("pool_bwd", T, tm, h2, d_pm, d_h3, g(1, 0), W["pool_w"], W["pool_b"],
                                                                                 W["pool_scale"], f1, g(0, 3))
    d_h1, d_mixo, dg02, dg01, dwg0, dwu0, dwd0 = ffn_bwd("0", 0, n1, gate0, up0, act0, d_f1, h1, mixo, d_h2, g(0, 2), g(0, 1), bf16)
    def d_wo(nm, y):
        return mm(nm, (2, 1, T // TKW), [(y, pl.BlockSpec((TKW, half), lambda s, j, k: (k, s)), d_mixo, pl.BlockSpec((TKW, D), lambda s, j, k: (k, 0)))],
                  TN, pl.BlockSpec((None, half, D), lambda s, j, k: (s, 0, 0)), SDS((2, half, D), f32))
    d_ycat = matmul("out_proj_dy", [(d_mixo, wo4.reshape(4 * half, D))], "nt", f32, 4 * tm, 1024)
    dwo_a, dwo_b = d_wo("out_proj_dwa", y_a), d_wo("out_proj_dwb", y_b)
    G["wo4"] = [dwo_a[0], dwo_a[1], dwo_b[0], dwo_b[1]]
    G["wgT4"], G["wuT4"], G["wd4"] = [dwg0, dwg1], [dwu0, dwu1], [dwd0, dwd1]
    token = early_grads(G)
    d_yssd, d_z, G["gn"] = gate_bwd("gate_bwd", T, tm, y_ssd, uvz, d_ycat, 1, W["gn"] + token[0, 0])
    d_xc, d_dtr, G["dtb"], G["alog"], G["dsk"] = ssd_bwd("ssd_bwd", T, xc, dtr, sprev, d_yssd, W["dtb"], W["alog"], W["dsk"])
    d_pre, G["conv_w"], G["conv_b"] = conv_bwd_pre("conv_bwd_pre", T, tm, xbc, d_xc, W["conv_w"], W["conv_b"])
    d_xbc = conv_bwd_x("conv_bwd_x", T, tm, d_pre, W["conv_w"])
    d_u, d_v, G["ln_g"], G["ln_b"], G["wm"], G["bs"] = gmlp_bwd("gmlp_bwd", T, tm, uvz, d_ycat, 0, W["ln_g"], W["ln_b"], W["wm"], W["bs"])
    w_u, w_v, w_z = W["w_uvz"][:, :D], W["w_uvz"][:, D:2 * D], W["w_uvz"][:, 2 * D:]
    def pre_epilogue(part, xs, os, accs):
        dx, dg = _rms_bwd(xs[0][...], xs[2][...], part)
        os[0][...] = xs[1][...] + dx
        accs[0][...] += dg
    blk = lambda w: pl.BlockSpec((tf, w), lambda i: (i, 0))
    whole = lambda a: pl.BlockSpec(a.shape, lambda i: (0, 0), pipeline_mode=once)
    grad_x, dg00 = mm_fused("in_dy0", T // tf, [(d_u, blk(D), w_u, whole(w_u)), (d_v, blk(D), w_v, whole(w_v)), (d_z, blk(D), w_z, whole(w_z)),
                                                (d_xbc, blk(CONV_DIM), W["w_xbc"], whole(W["w_xbc"])), (d_dtr, blk(DT_PAD), W["w_dt"], whole(W["w_dt"]))],
                            NT, [(x, rows_f), (d_h1, rows_f), (g(0, 0), vec_f)], [out_f32], [(1, D)], pre_epilogue)
    G["w_inT"] = [matmul("in_dwu", [(d_u, y0)], "tn", f32, 1024, 1024, TKW), matmul("in_dwv", [(d_v, y0)], "tn", f32, 1024, 1024, TKW),
                  matmul("in_dwz", [(d_z, y0)], "tn", f32, 1024, 1024, TKW), matmul("in_dwxbc", [(d_xbc, y0)], "tn", f32, 1024, 1024, TKW),
                  matmul("in_dwdt", [(d_dtr, y0)], "tn", f32, DT_PAD, 1024, TKW)[:N_HEADS]]
    G["norm_g"] = jnp.stack([jnp.concatenate([dg00, dg01, dg02, dg03], 0), jnp.concatenate([dg10, dg11, dg12, dg13], 0)])
    return loss_acc, grad_x, G


def build_weights(Wf):
    causal = jnp.tril(jnp.ones((CHUNK, CHUNK), bool))
    w_in = Wf["w_in"].astype(bf16)
    pad16 = lambda v: jnp.pad(v.reshape(1, N_HEADS).astype(f32), ((0, 0), (0, DT_PAD - N_HEADS)))
    return {
        "norm_g": Wf["norm_g"],
        "w_uvz": w_in[:, :3 * D], "w_xbc": w_in[:, 3 * D:3 * D + CONV_DIM],
        "w_dt": jnp.pad(w_in[:, 3 * D + CONV_DIM:], ((0, 0), (0, DT_PAD - N_HEADS))),
        "ln_g": Wf["gm_ln_g"].reshape(1, D), "ln_b": Wf["gm_ln_b"].reshape(1, D),
        "wm": jnp.where(causal[None], Wf["gm_ws"], 0).astype(bf16), "bs": Wf["gm_bs"].reshape(GM_HEADS, CHUNK, 1),
        "conv_w": Wf["conv_w"], "conv_b": Wf["conv_b"].reshape(1, CONV_DIM),
        "dtb": pad16(Wf["dt_bias"]), "alog": pad16(Wf["a_log"]), "dsk": pad16(Wf["d_skip"]),
        "gn": Wf["ssm_norm_g"].reshape(1, D),
        "wo4": Wf["wo4"].astype(bf16),
        "pool_w": Wf["pool_w"].astype(bf16), "pool_b": Wf["pool_b"].reshape(1, D), "pool_scale": Wf["pool_scale"].reshape(1, D),
    }


def small_grads(G):
    return {
        "norm_g": G["norm_g"],
        "gm_ln_g": G["ln_g"].reshape(D), "gm_ln_b": G["ln_b"].reshape(D),
        "gm_ws": G["wm"], "gm_bs": G["bs"].reshape(GM_HEADS, CHUNK),
        "conv_w": G["conv_w"], "conv_b": G["conv_b"].reshape(CONV_DIM),
        "dt_bias": G["dtb"][0, :N_HEADS], "a_log": G["alog"][0, :N_HEADS], "d_skip": G["dsk"][0, :N_HEADS],
        "ssm_norm_g": G["gn"].reshape(D),
        "pool_b": G["pool_b"].reshape(4, POOL_GD), "pool_scale": G["pool_scale"].reshape(D),
    }


MESH_ID = pl.DeviceIdType.MESH
ANY = pl.BlockSpec(memory_space=pl.ANY)


DMA_CHUNK_BYTES = 2 << 20
DMA_MAX_CHUNKS = 32


def _pieces(view, axis, align):
    shape = view.shape
    nbytes = math.prod(shape) * jnp.dtype(view.dtype).itemsize
    n = max(1, min(DMA_MAX_CHUNKS, -(-nbytes // DMA_CHUNK_BYTES)))
    rows = shape[axis]
    size = -(-rows // n)
    size = -(-size // align) * align
    out = []
    for s in range(0, rows, size):
        idx = [slice(None)] * len(shape)
        idx[axis] = pl.ds(s, min(size, rows - s))
        out.append(tuple(idx))
    return out


def comm_call(name, operands, out_shapes, plan):
    n_in = len(operands)
    n_out = len(out_shapes)
    n_remote, n_local = plan((0, 0, 0), [None] * n_in, [None] * n_out, True)

    def body(*refs):
        in_refs, out_refs = refs[:n_in], refs[n_in:n_in + n_out]
        send_sems, recv_sems, local_sems = refs[n_in + n_out:]
        me = (lax.axis_index("x"), lax.axis_index("y"), lax.axis_index("c"))
        remote, local = plan(me, in_refs, out_refs, False)
        align = lambda v: 16 if v.dtype == bf16 else 8
        for j, (s, d, axis) in enumerate(local):
            for ix in _pieces(s, axis, align(s)):
                pltpu.make_async_copy(s.at[ix], d.at[ix], local_sems.at[j]).start()
        peers = [tuple((1 - m) if f else m for m, f in zip(me, flip)) for flip, *_ in remote]
        for k, (flip, src, dst, _, axis) in enumerate(remote):
            for ix in _pieces(src, axis, align(src)):
                pltpu.make_async_remote_copy(src_ref=src.at[ix], dst_ref=dst.at[ix], send_sem=send_sems.at[k], recv_sem=recv_sems.at[k],
                                             device_id=peers[k], device_id_type=MESH_ID).start()
        for k, (flip, src, dst, landing, axis) in enumerate(remote):
            pltpu.make_async_remote_copy(src_ref=landing, dst_ref=landing, send_sem=send_sems.at[k], recv_sem=recv_sems.at[k],
                                         device_id=peers[k], device_id_type=MESH_ID).wait_recv()
        for k, (flip, src, dst, landing, axis) in enumerate(remote):
            pltpu.make_async_remote_copy(src_ref=src, dst_ref=dst, send_sem=send_sems.at[k], recv_sem=recv_sems.at[k],
                                         device_id=peers[k], device_id_type=MESH_ID).wait_send()
        for j, (s, d, axis) in enumerate(local):
            pltpu.make_async_copy(s, d, local_sems.at[j]).wait()

    return pl.pallas_call(
        body, name=name, out_shape=list(out_shapes), in_specs=[ANY] * n_in, out_specs=[ANY] * n_out,
        scratch_shapes=[pltpu.SemaphoreType.DMA((n_remote,)), pltpu.SemaphoreType.DMA((n_remote,)), pltpu.SemaphoreType.DMA((max(n_local, 1),))],
    )(*operands)


CHIP_FLIPS = ((1, 0, 0), (0, 1, 0), (1, 1, 0))
PAIR_FLIP = (0, 0, 1)


def gather_two_level(name, halved, whole):
    nh, nw = len(halved), len(whole)
    nf = len(CHIP_FLIPS)

    def body(*refs):
        srcs, outs = refs[:nh + nw], refs[nh + nw:2 * (nh + nw)]
        send_sems, recv_sems, fwd_send, fwd_recv = refs[2 * (nh + nw):]
        me = (lax.axis_index("x"), lax.axis_index("y"), lax.axis_index("c"))
        k, c = 2 * me[0] + me[1], me[2]
        sibling = (me[0], me[1], 1 - c)
        peers = [tuple((1 - m) if fl else m for m, fl in zip(me, flip)) for flip in CHIP_FLIPS]

        def half(ref, which):
            rh = ref.shape[0] // 2
            return ref.at[pl.ds(pl.multiple_of(which * rh, 16), rh), :]

        def ici(a, f):
            src = half(srcs[a], c) if a < nh else srcs[a]
            dst = half(outs[a].at[k], c) if a < nh else outs[a].at[k]
            return pltpu.make_async_remote_copy(src_ref=src, dst_ref=dst, send_sem=send_sems.at[a * nf + f], recv_sem=recv_sems.at[a * nf + f],
                                                device_id=peers[f], device_id_type=MESH_ID)

        def landed(a, f):
            slot = outs[a].at[_chip_of(me, CHIP_FLIPS[f])]
            return half(slot, c) if a < nh else slot

        def forward(a, f, which):
            v = half(outs[a].at[_chip_of(me, CHIP_FLIPS[f])], which)
            return pltpu.make_async_remote_copy(src_ref=v, dst_ref=v, send_sem=fwd_send.at[a * nf + f], recv_sem=fwd_recv.at[a * nf + f],
                                                device_id=sibling, device_id_type=MESH_ID)

        copies = [ici(a, f) for a in range(nh + nw) for f in range(nf)]
        for cp in copies:
            cp.start()
        fwds = []
        for a in range(nh):
            for f in range(nf):
                lv = landed(a, f)
                pltpu.make_async_remote_copy(src_ref=lv, dst_ref=lv, send_sem=send_sems.at[a * nf + f], recv_sem=recv_sems.at[a * nf + f],
                                             device_id=peers[f], device_id_type=MESH_ID).wait_recv()
                fw = forward(a, f, c)
                fw.start()
                fwds.append(fw)
        for a in range(nh, nh + nw):
            for f in range(nf):
                lv = landed(a, f)
                pltpu.make_async_remote_copy(src_ref=lv, dst_ref=lv, send_sem=send_sems.at[a * nf + f], recv_sem=recv_sems.at[a * nf + f],
                                             device_id=peers[f], device_id_type=MESH_ID).wait_recv()
        for a in range(nh):
            for f in range(nf):
                forward(a, f, 1 - c).wait_recv()
        for fw in fwds:
            fw.wait_send()
        for cp in copies:
            cp.wait_send()

    arrs = list(halved) + list(whole)
    n_ici = (nh + nw) * nf
    return pl.pallas_call(
        body, name=name, out_shape=[SDS((N_CHIPS,) + a.shape, a.dtype) for a in arrs], in_specs=[ANY] * len(arrs), out_specs=[ANY] * len(arrs),
        scratch_shapes=[pltpu.SemaphoreType.DMA((n_ici,)), pltpu.SemaphoreType.DMA((n_ici,)),
                        pltpu.SemaphoreType.DMA((nh * nf,)), pltpu.SemaphoreType.DMA((nh * nf,))],
    )(*arrs)


def pair_split_exchange(name, p, rh):
    def plan(me, ins, outs, count):
        if count:
            return 1, 0
        theirs = ins[0].at[:, pl.ds(pl.multiple_of((1 - me[2]) * rh, 8), rh), :]
        return [(PAIR_FLIP, theirs, outs[0], outs[0], 1)], []
    return comm_call(name, [p], [SDS((4, rh, p.shape[2]), p.dtype)], plan)[0]


def scatter_over_chips(name, cs):
    def plan(me, ins, outs, count):
        if count:
            return len(CHIP_FLIPS), 0
        k = 2 * me[0] + me[1]
        remote = []
        for flip in CHIP_FLIPS:
            kp = 2 * ((1 - me[0]) if flip[0] else me[0]) + ((1 - me[1]) if flip[1] else me[1])
            remote.append((flip, ins[0].at[kp], outs[0].at[k], outs[0].at[kp], 0))
        return remote, []
    return comm_call(name, [cs], [SDS(cs.shape, cs.dtype)], plan)[0]


def pair_swap(name, half):
    def plan(me, ins, outs, count):
        if count:
            return 1, 0
        return [(PAIR_FLIP, ins[0], outs[0], outs[0], 0)], []
    return comm_call(name, [half], [SDS(half.shape, half.dtype)], plan)[0]


def _row_tile(rows, cap=512):
    if rows <= cap:
        return rows
    t = cap - cap % 8
    while rows % t:
        t -= 8
    return t


def pair_sum(name, packs, got, c_arr, tile):
    rh = got.shape[1]
    nb = rh // tile

    def kern(c_ref, a_ref, b_ref, o16_ref):
        o16_ref[...] = (a_ref[...] + b_ref[...]).astype(bf16)
    blk = (None, tile, D)
    grid_spec = pltpu.PrefetchScalarGridSpec(
        num_scalar_prefetch=1, grid=(4, nb),
        in_specs=[pl.BlockSpec(blk, lambda s, i, c: (s, c[0] * nb + i, 0)), pl.BlockSpec(blk, lambda s, i, c: (s, i, 0))],
        out_specs=pl.BlockSpec(blk, lambda s, i, c: (s, i, 0)))
    return pl.pallas_call(kern, name=name, grid_spec=grid_spec, out_shape=SDS(got.shape, bf16),
                          compiler_params=pltpu.CompilerParams(dimension_semantics=("parallel", "parallel")))(c_arr, packs, got)


def chip_sum(name, own16, landed16, k_arr, tile):
    rh = own16.shape[1]
    nb = rh // tile

    def kern(k_ref, own_ref, l0, l1, l2, l3, o_ref):
        k = k_ref[0]
        s = None
        for j, lref in enumerate((l0, l1, l2, l3)):
            t = jnp.where(k == j, own_ref[...], lref[...]).astype(f32)
            s = t if s is None else s + t
        o_ref[...] = s
    blk = (None, tile, D)
    land = [pl.BlockSpec(blk, lambda i, k, j=j: (jnp.where(k[0] == j, (j + 1) % N_CHIPS, j), i, 0)) for j in range(N_CHIPS)]
    grid_spec = pltpu.PrefetchScalarGridSpec(
        num_scalar_prefetch=1, grid=(nb,),
        in_specs=[pl.BlockSpec(blk, lambda i, k: (k[0], i, 0))] + land,
        out_specs=pl.BlockSpec((tile, D), lambda i, k: (i, 0)))
    return pl.pallas_call(kern, name=name, grid_spec=grid_spec, out_shape=SDS((rh, D), f32),
                          compiler_params=pltpu.CompilerParams(dimension_semantics=("parallel",)))(k_arr, own16, landed16, landed16, landed16, landed16)


def adamw(name, w, g, m, v):
    R, C = w.shape
    tr = _row_tile(R, 256)

    def kern(w_ref, g_ref, m_ref, v_ref, d_ref, mo_ref, vo_ref):
        gg = g_ref[...]
        mn = ADAM_B1 * m_ref[...] + (1.0 - ADAM_B1) * gg
        vn = ADAM_B2 * v_ref[...] + (1.0 - ADAM_B2) * jnp.square(gg)
        m_hat = mn / (1.0 - ADAM_B1 ** ADAM_STEP)
        v_hat = vn / (1.0 - ADAM_B2 ** ADAM_STEP)
        d_ref[...] = -ADAM_LR * (m_hat / (jnp.sqrt(v_hat) + ADAM_EPS) + ADAM_WD * w_ref[...])
        mo_ref[...] = mn
        vo_ref[...] = vn
    spec = pl.BlockSpec((tr, C), lambda i: (i, 0))
    s = SDS((R, C), f32)
    return pl.pallas_call(kern, name=name, grid=(R // tr,), in_specs=[spec] * 4, out_specs=[spec] * 3, out_shape=[s, s, s],
                          compiler_params=pltpu.CompilerParams(dimension_semantics=("parallel",)))(w, g, m, v)


WEIGHT_NAMES = ("norm_g", "w_in", "gm_ln_g", "gm_ln_b", "gm_ws", "gm_bs", "conv_w", "conv_b", "dt_bias", "a_log", "d_skip",
                "ssm_norm_g", "w_out", "pool_w", "pool_b", "pool_scale", "ffn_w_gate", "ffn_w_up", "ffn_w_down")
SMALL = ("norm_g", "conv_w", "pool_b", "pool_scale")
REPL = ("gm_ln_g", "gm_ln_b", "gm_ws", "gm_bs", "conv_b", "dt_bias", "a_log", "d_skip", "ssm_norm_g")
SMALL_AXIS = {"norm_g": 2, "conv_w": 1, "pool_b": 1, "pool_scale": 0}
N_CHIPS = 4
IN_SH = IN_DIM // N_CHIPS
SMALL_ROWS = 8
REPL_ROWS = 72
E_OUT, E_GATE, E_UP, E_DOWN = 0, 512, 512 + 2 * FF_SH, 512 + 4 * FF_SH
E_POOL = E_DOWN + 2 * FF_SH
E_ROWS, E_TILE = E_POOL + 64, 400
L_SMALL, L_REPL, L_IN = 0, SMALL_ROWS, SMALL_ROWS + REPL_ROWS
L_END = L_IN + IN_SH
L_ROWS, L_TILE = 1408, 352


def _flat_rows(pieces, rows):
    v = jnp.concatenate([p.reshape(-1) for p in pieces])
    return jnp.pad(v, (0, rows * D - v.shape[0])).reshape(rows, D)


def _shard_small(name, full, k):
    ax = SMALL_AXIS[name]
    n = full.shape[ax] // N_CHIPS
    return lax.slice_in_dim(full, k * n, (k + 1) * n, axis=ax)


def _drop1(name, a):
    return a if name == "norm_g" else a[0]


HBM_SPEC = pl.BlockSpec(memory_space=pltpu.HBM)
SEM_SPEC = pl.BlockSpec(memory_space=pltpu.SEMAPHORE)
SPLIT_EFFECT = pltpu.SideEffectType.DATAFLOW_SIDE_EFFECTING


def _chip_of(me, flip):
    return 2 * ((1 - me[0]) if flip[0] else me[0]) + ((1 - me[1]) if flip[1] else me[1])


def gather_start(name, arrs, after, slotted=False):
    n = len(arrs)
    ncp = n * len(CHIP_FLIPS)

    def body(*refs):
        srcs, lands = refs[:n], refs[n:2 * n]
        send_sems, recv_sems, token = refs[2 * n + 1], refs[2 * n + 2], refs[-1]
        me = (lax.axis_index("x"), lax.axis_index("y"), lax.axis_index("c"))
        k = 2 * me[0] + me[1]
        for a in range(n):
            for f, flip in enumerate(CHIP_FLIPS):
                peer = tuple((1 - m) if fl else m for m, fl in zip(me, flip))
                src = srcs[a].at[_chip_of(me, flip)] if slotted else srcs[a]
                for ix in _pieces(src, 0, 16):
                    pltpu.make_async_remote_copy(src_ref=src.at[ix], dst_ref=lands[a].at[k].at[ix],
                                                 send_sem=send_sems.at[a * len(CHIP_FLIPS) + f], recv_sem=recv_sems.at[a * len(CHIP_FLIPS) + f],
                                                 device_id=peer, device_id_type=MESH_ID).start()
        token[...] = jnp.zeros_like(token)

    land_shapes = [a.shape if slotted else (N_CHIPS,) + a.shape for a in arrs]
    operands = [pltpu.with_memory_space_constraint(a, pltpu.HBM) for a in arrs]
    operands += [pltpu.with_memory_space_constraint(lax.empty(s, a.dtype), pltpu.HBM) for s, a in zip(land_shapes, arrs)]
    out = pl.pallas_call(
        body, name=name,
        out_shape=(pltpu.SemaphoreType.DMA((ncp,)), pltpu.SemaphoreType.DMA((ncp,)), *[pltpu.HBM(a.shape, a.dtype) for a in arrs],
                   *[pltpu.HBM(s, a.dtype) for s, a in zip(land_shapes, arrs)], SDS((8, 128), f32)),
        in_specs=[HBM_SPEC] * (2 * n) + [ANY], out_specs=(SEM_SPEC, SEM_SPEC, *[HBM_SPEC] * (2 * n), pl.BlockSpec(memory_space=pltpu.VMEM)),
        input_output_aliases={i: 2 + i for i in range(2 * n)},
        compiler_params=pltpu.CompilerParams(has_side_effects=SPLIT_EFFECT),
    )(*operands, after)
    return out[0], out[1], out[2:2 + n], out[2 + n:2 + 2 * n], out[-1]


def gather_wait(name, send_sems, recv_sems, thru, lands, after, slotted=False):
    n = len(thru)

    def body(*refs):
        srcs, lands_r = refs[:n], refs[n:2 * n]
        s_sems, r_sems = refs[2 * n], refs[2 * n + 1]
        me = (lax.axis_index("x"), lax.axis_index("y"), lax.axis_index("c"))
        k = 2 * me[0] + me[1]
        for a in range(n):
            for f, flip in enumerate(CHIP_FLIPS):
                peer = tuple((1 - m) if fl else m for m, fl in zip(me, flip))
                idx = a * len(CHIP_FLIPS) + f
                src = srcs[a].at[_chip_of(me, flip)] if slotted else srcs[a]
                pltpu.make_async_remote_copy(src_ref=src, dst_ref=lands_r[a].at[k], send_sem=s_sems.at[idx], recv_sem=r_sems.at[idx],
                                             device_id=peer, device_id_type=MESH_ID).wait_send()
                pltpu.make_async_remote_copy(src_ref=src, dst_ref=lands_r[a].at[_chip_of(me, flip)], send_sem=s_sems.at[idx],
                                             recv_sem=r_sems.at[idx], device_id=peer, device_id_type=MESH_ID).wait_recv()

    out = pl.pallas_call(
        body, name=name, out_shape=tuple(pltpu.HBM(t.shape, t.dtype) for t in (*thru, *lands)),
        in_specs=[HBM_SPEC] * (2 * n) + [SEM_SPEC, SEM_SPEC, ANY], out_specs=tuple([HBM_SPEC] * (2 * n)),
        input_output_aliases={i: i for i in range(2 * n)},
        compiler_params=pltpu.CompilerParams(has_side_effects=SPLIT_EFFECT),
    )(*thru, *lands, send_sems, recv_sems, after)
    return out[:n], out[n:]


def gather_weights(w_sh):
    big = [w_sh["w_in"][0], w_sh["w_out"][0], w_sh["pool_w"][0].reshape(4 * 64, POOL_GD)]
    small_pack = _flat_rows([w_sh[n] for n in SMALL], SMALL_ROWS)
    own = [b.astype(bf16) for b in big] + [small_pack]
    my_k = 2 * lax.axis_index("x") + lax.axis_index("y")
    s_in, s_out, s_pool, s_small = [lax.dynamic_update_slice(s, o[None], (my_k, 0, 0))
                                    for s, o in zip(gather_two_level("gather_weights", own[:3], own[3:]), own)]
    Wf = {n: w_sh[n][0] for n in REPL}
    Wf["w_in"] = s_in.transpose(1, 0, 2).reshape(D, IN_DIM)
    Wf["pool_w"] = s_pool.reshape(N_CHIPS, 4, 64, POOL_GD).transpose(1, 0, 2, 3).reshape(4, POOL_GD, POOL_GD)
    Wf["wo4"] = s_out
    small_shapes = [_drop1(n, w_sh[n]).shape for n in SMALL]
    parts = [_split_rows(s_small[k], small_shapes) for k in range(N_CHIPS)]
    for j, n in enumerate(SMALL):
        Wf[n] = jnp.concatenate([parts[k][j] for k in range(N_CHIPS)], axis=SMALL_AXIS[n])
    return Wf


def pack_early(G):
    slots = [jnp.concatenate([G["wo4"][k], G["wgT4"][0][k], G["wgT4"][1][k], G["wuT4"][0][k], G["wuT4"][1][k], G["wd4"][0][k], G["wd4"][1][k],
                              G["pool_w"][:, k * 64:(k + 1) * 64, :].reshape(64, D)], axis=0) for k in range(N_CHIPS)]
    return jnp.stack(slots)


def pack_late(G):
    sg = small_grads(G)
    repl = _flat_rows([sg[n] for n in REPL], REPL_ROWS)
    w_in_t = jnp.concatenate(G["w_inT"], axis=0)
    slots = [jnp.concatenate([_flat_rows([_shard_small(n, sg[n], k) for n in SMALL], SMALL_ROWS), repl,
                              jnp.pad(w_in_t[k * IN_SH:(k + 1) * IN_SH], ((0, L_ROWS - L_END), (0, 0)))], axis=0)
             for k in range(N_CHIPS)]
    return jnp.stack(slots)


def unpack_grads(early, late, w_sh):
    g = {"w_out": early[E_OUT:E_GATE], "ffn_w_down": early[E_DOWN:E_POOL], "pool_w": early[E_POOL:E_ROWS],
         "ffn_w_gate": jnp.stack([early[E_GATE + l * FF_SH:E_GATE + (l + 1) * FF_SH].T for l in range(2)]),
         "ffn_w_up": jnp.stack([early[E_UP + l * FF_SH:E_UP + (l + 1) * FF_SH].T for l in range(2)]),
         "w_in": late[L_IN:L_END].T}
    small = _split_rows(late[L_SMALL:L_REPL], [_drop1(n, w_sh[n]).shape for n in SMALL])
    repl = _split_rows(late[L_REPL:L_IN], [w_sh[n][0].shape for n in REPL])
    g.update(zip(SMALL, small))
    g.update(zip(REPL, repl))
    return {n: g[n].reshape(w_sh[n].shape) for n in WEIGHT_NAMES}


def _split_rows(flat2d, shapes):
    v = flat2d.reshape(-1)
    out, off = [], 0
    for s in shapes:
        n = math.prod(s)
        out.append(v[off:off + n].reshape(s))
        off += n
    return out


def kernel(x, norm_g, w_in, gm_ln_g, gm_ln_b, gm_ws, gm_bs, conv_w, conv_b, dt_bias, a_log, d_skip, ssm_norm_g, w_out, pool_w, pool_b, pool_scale, ffn_w_gate, ffn_w_up, ffn_w_down, loss_target, m_norm_g, m_w_in, m_gm_ln_g, m_gm_ln_b, m_gm_ws, m_gm_bs, m_conv_w, m_conv_b, m_dt_bias, m_a_log, m_d_skip, m_ssm_norm_g, m_w_out, m_pool_w, m_pool_b, m_pool_scale, m_ffn_w_gate, m_ffn_w_up, m_ffn_w_down, v_norm_g, v_w_in, v_gm_ln_g, v_gm_ln_b, v_gm_ws, v_gm_bs, v_conv_w, v_conv_b, v_dt_bias, v_a_log, v_d_skip, v_ssm_norm_g, v_w_out, v_pool_w, v_pool_b, v_pool_scale, v_ffn_w_gate, v_ffn_w_up, v_ffn_w_down):
    T = x.shape[1]
    w_sh = dict(zip(WEIGHT_NAMES, (norm_g, w_in, gm_ln_g, gm_ln_b, gm_ws, gm_bs, conv_w, conv_b, dt_bias, a_log, d_skip, ssm_norm_g, w_out,
                                   pool_w, pool_b, pool_scale, ffn_w_gate, ffn_w_up, ffn_w_down)))
    m_sh = dict(zip(WEIGHT_NAMES, (m_norm_g, m_w_in, m_gm_ln_g, m_gm_ln_b, m_gm_ws, m_gm_bs, m_conv_w, m_conv_b, m_dt_bias, m_a_log, m_d_skip,
                                   m_ssm_norm_g, m_w_out, m_pool_w, m_pool_b, m_pool_scale, m_ffn_w_gate, m_ffn_w_up, m_ffn_w_down)))
    v_sh = dict(zip(WEIGHT_NAMES, (v_norm_g, v_w_in, v_gm_ln_g, v_gm_ln_b, v_gm_ws, v_gm_bs, v_conv_w, v_conv_b, v_dt_bias, v_a_log, v_d_skip,
                                   v_ssm_norm_g, v_w_out, v_pool_w, v_pool_b, v_pool_scale, v_ffn_w_gate, v_ffn_w_up, v_ffn_w_down)))

    my_k = 2 * lax.axis_index("x") + lax.axis_index("y")
    ffn_own = [w_sh["ffn_w_gate"].reshape(2 * D, FF_SH).astype(bf16), w_sh["ffn_w_up"].reshape(2 * D, FF_SH).astype(bf16),
               w_sh["ffn_w_down"].reshape(2 * FF_SH, D).astype(bf16)]
    Wf = gather_weights(w_sh)
    send_sems, recv_sems, thru, lands, token = gather_start("gather_ffn_start", ffn_own, Wf["wo4"])
    Wf["norm_g"] = Wf["norm_g"] + token[0, 0]
    W = build_weights(Wf)

    def ffn_weights(after):
        _, landed = gather_wait("gather_ffn_wait", send_sems, recv_sems, thru, lands, after)
        return tuple(lax.dynamic_update_slice(l, o[None], (my_k, 0, 0)) for l, o in zip(landed, ffn_own))

    my_c = lax.axis_index("c")
    c_arr = my_c.astype(jnp.int32).reshape(1)
    k_arr = my_k.astype(jnp.int32).reshape(1)

    def pair_stage(tag, packs, tile):
        got = pair_split_exchange(f"grads{tag}_pair_split", packs, packs.shape[1] // 2)
        return pair_sum(f"grads{tag}_pair_sum", packs, got, c_arr, tile)

    def chip_stage(tag, pair16, landed, tile):
        half = chip_sum(f"grads{tag}_chip_sum", pair16, landed, k_arr, tile)
        other = pair_swap(f"grads{tag}_pair_swap", half)
        return jnp.concatenate([jnp.where(my_c == 0, half, other), jnp.where(my_c == 0, other, half)], axis=0)

    early = {}

    def early_grads(Ge):
        pair16 = pair_stage("E", pack_early(Ge), E_TILE)
        s_sems, r_sems, thru, lands, tok = gather_start("gradsE_scatter_start", [pair16], jnp.zeros((8, 128), f32), slotted=True)
        early.update(s_sems=s_sems, r_sems=r_sems, thru=thru, lands=lands)
        return tok

    loss_acc, grad_x, G = local_step(T, x[0], loss_target[0], W, ffn_weights, early_grads)
    pair_l = pair_stage("L", pack_late(G), L_TILE)
    total_l = chip_stage("L", pair_l, scatter_over_chips("gradsL_scatter", pair_l), L_TILE)
    (pair_e,), (landed_e,) = gather_wait("gradsE_scatter_wait", early["s_sems"], early["r_sems"], early["thru"], early["lands"], total_l,
                                         slotted=True)
    total_e = chip_stage("E", pair_e, landed_e, E_TILE)
    grads = unpack_grads(total_e, total_l, w_sh)

    delta, new_m, new_v = {}, {}, {}
    for n in WEIGHT_NAMES:
        shp = w_sh[n].shape
        two_d = (-1, shp[-1])
        d_, m_, v_ = adamw("adamw_" + n, w_sh[n].reshape(two_d), grads[n].reshape(two_d), m_sh[n].reshape(two_d), v_sh[n].reshape(two_d))
        delta[n], new_m[n], new_v[n] = d_.reshape(shp), m_.reshape(shp), v_.reshape(shp)

    loss = lax.psum(loss_acc[0, 0], ("x", "y", "c"))
    return (loss, grad_x[None], *[grads[n] for n in WEIGHT_NAMES], *[delta[n] for n in WEIGHT_NAMES],
            *[new_m[n] for n in WEIGHT_NAMES], *[new_v[n] for n in WEIGHT_NAMES])
```

```python
import math

import jax
import jax.numpy as jnp
from jax import lax
from jax.experimental import pallas as pl
from jax.experimental.pallas import tpu as pltpu

f32, bf16 = jnp.float32, jnp.bfloat16
SDS = jax.ShapeDtypeStruct

D = 1024
EPS = 1e-6
CHUNK = 128
GM_HEADS, GM_HD = 4, 256
SSM_GROUPS, SSM_HPG, SSM_P, SSM_N = 4, 4, 64, 128
N_HEADS = SSM_GROUPS * SSM_HPG
CONV_K = 4
CONV_DIM = 2048
POOL_WINDOWS = (2, 4, 8, 16)
POOL_GD = 256
POOL_HALO = 32
CONV_HALO = 8
D_FF = 2816
DT_PAD = 128
IN_DIM = 5136

ADAM_LR, ADAM_B1, ADAM_B2, ADAM_EPS, ADAM_WD, ADAM_STEP = 0.001, 0.9, 0.999, 1e-08, 0.01, 10

NT = (((1,), (1,)), ((), ()))
TN = (((0,), (0,)), ((), ()))
NN = (((1,), (0,)), ((), ()))
HI = lax.Precision.HIGHEST
MM_SUB = 256


def _silu(x):
    return x * jax.nn.sigmoid(x)


def _softplus(x):
    return jnp.maximum(x, 0.0) + jnp.log1p(jnp.exp(-jnp.abs(x)))


def _rms(x, g):
    return x * lax.rsqrt(jnp.mean(x * x, axis=-1, keepdims=True) + EPS) * g


def _rms_bwd(x, g, dy):
    r = lax.rsqrt(jnp.mean(x * x, axis=-1, keepdims=True) + EPS)
    xh = x * r
    dxh = dy * g
    dx = r * (dxh - xh * jnp.mean(dxh * xh, axis=-1, keepdims=True))
    return dx, jnp.sum(dy * xh, axis=0, keepdims=True)


def _bdot(a, b, dims=NN):
    return lax.dot_general(a.astype(bf16), b.astype(bf16), dims, preferred_element_type=f32)


def matmul(name, pairs, mode, out_dtype, tm, tn, tk=None):
    a0, b0 = pairs[0]
    if mode == "tn":
        M, N, K = a0.shape[1], b0.shape[1], a0.shape[0]
    else:
        M, K = a0.shape
        N = b0.shape[1] if mode == "nn" else b0.shape[0]
    tm, tn = min(tm, M), min(tn, N)
    assert M % tm == 0 and N % tn == 0, (name, M, N, tm, tn)
    if tk is None:
        nk = 1
    else:
        assert len(pairs) == 1 and K % tk == 0
        nk = K // tk
    dims = {"nn": NN, "nt": NT, "tn": TN}[mode]
    in_specs, args = [], []
    for a, b in pairs:
        kk = (a.shape[0] if mode == "tn" else a.shape[1]) if tk is None else tk
        if mode == "tn":
            in_specs.append(pl.BlockSpec((kk, tm), lambda j, i, k: (k, i)))
            in_specs.append(pl.BlockSpec((kk, tn), lambda j, i, k: (k, j)))
        elif mode == "nn":
            in_specs.append(pl.BlockSpec((tm, kk), lambda j, i, k: (i, k)))
            in_specs.append(pl.BlockSpec((kk, tn), lambda j, i, k: (k, j)))
        else:
            in_specs.append(pl.BlockSpec((tm, kk), lambda j, i, k: (i, k)))
            in_specs.append(pl.BlockSpec((tn, kk), lambda j, i, k: (j, k)))
        args += [a, b]
    npairs = len(pairs)

    def kern(*refs):
        o = refs[2 * npairs]
        part = None
        for p in range(npairs):
            d = _bdot(refs[2 * p][...], refs[2 * p + 1][...], dims)
            part = d if part is None else part + d
        if nk == 1:
            o[...] = part.astype(out_dtype)
        else:
            acc = refs[2 * npairs + 1]
            k = pl.program_id(2)

            @pl.when(k == 0)
            def _():
                acc[...] = part

            @pl.when(k > 0)
            def _():
                acc[...] += part

            @pl.when(k == nk - 1)
            def _():
                o[...] = acc[...].astype(out_dtype)

    return pl.pallas_call(
        kern, name=name, grid=(N // tn, M // tm, nk),
        in_specs=in_specs, out_specs=pl.BlockSpec((tm, tn), lambda j, i, k: (i, j)),
        out_shape=SDS((M, N), out_dtype),
        scratch_shapes=[pltpu.VMEM((tm, tn), f32)] if nk > 1 else [],
        compiler_params=pltpu.CompilerParams(dimension_semantics=("parallel", "parallel", "arbitrary")),
    )(*args)


def mm(name, grid, pairs, dims, o_spec, out_shape):
    nk = grid[2]
    npairs = len(pairs)
    in_specs, args = [], []
    for a, a_spec, b, b_spec in pairs:
        in_specs += [a_spec, b_spec]
        args += [a, b]
    blk = tuple(d for d in o_spec.block_shape if d is not None)

    def kern(*refs):
        o = refs[2 * npairs]
        part = None
        for p in range(npairs):
            d = _bdot(refs[2 * p][...], refs[2 * p + 1][...], dims)
            part = d if part is None else part + d
        if nk == 1:
            o[...] = part.astype(o.dtype)
        else:
            acc = refs[2 * npairs + 1]
            k = pl.program_id(2)

            @pl.when(k == 0)
            def _():
                acc[...] = part

            @pl.when(k > 0)
            def _():
                acc[...] += part

            @pl.when(k == nk - 1)
            def _():
                o[...] = acc[...].astype(o.dtype)

    return pl.pallas_call(
        kern, name=name, grid=grid, in_specs=in_specs, out_specs=o_spec, out_shape=out_shape,
        scratch_shapes=[pltpu.VMEM(blk, f32)] if nk > 1 else [],
        compiler_params=pltpu.CompilerParams(dimension_semantics=("parallel", "parallel", "arbitrary")),
    )(*args)


def mm_fused(name, n_row_blocks, pairs, dims, extra_ins, outs, accs, epilogue):
    npairs, nx, no, na = len(pairs), len(extra_ins), len(outs), len(accs)
    in_specs, args = [], []
    for a, a_spec, b, b_spec in pairs:
        in_specs += [a_spec, b_spec]
        args += [a, b]
    for arr, spec in extra_ins:
        in_specs.append(spec)
        args.append(arr)

    rows_blk = outs[0][1].block_shape[0]
    sub = min(rows_blk, MM_SUB)

    def kern(*refs):
        x_refs = refs[2 * npairs:2 * npairs + nx]
        o_refs = refs[2 * npairs + nx:2 * npairs + nx + no]
        a_refs = refs[2 * npairs + nx + no:]
        if na:
            @pl.when(pl.program_id(0) == 0)
            def _():
                for a in a_refs:
                    a[...] = jnp.zeros(a.shape, f32)
        for r0 in range(0, rows_blk, sub):
            rows = pl.ds(r0, sub)
            part = None
            for p in range(npairs):
                d = _bdot(refs[2 * p][rows, :], refs[2 * p + 1][...], dims)
                part = d if part is None else part + d
            epilogue(part, [x.at[rows, :] if x.shape[0] == rows_blk else x for x in x_refs], [o.at[rows, :] for o in o_refs], a_refs)

    return pl.pallas_call(
        kern, name=name, grid=(n_row_blocks,), in_specs=in_specs,
        out_specs=[spec for _, spec in outs] + [pl.BlockSpec(tuple(s), lambda i, nd=len(s): (0,) * nd) for s in accs],
        out_shape=[s for s, _ in outs] + [SDS(tuple(s), f32) for s in accs],
        compiler_params=pltpu.CompilerParams(dimension_semantics=("arbitrary",)),
    )(*args)


FF_SH = D_FF // 4


def ffn_up(name, T, tm, n_bf, wg4, wu4, l):
    sub = min(tm, MM_SUB)

    def kern(n_ref, wg_ref, wu_ref, g_ref, u_ref, a_ref):
        for r0 in range(0, tm, sub):
            rows = pl.ds(r0, sub)
            n = n_ref[rows, :]
            g = jnp.dot(n, wg_ref[...], preferred_element_type=f32)
            u = jnp.dot(n, wu_ref[...], preferred_element_type=f32)
            g_ref[rows, :] = g.astype(bf16)
            u_ref[rows, :] = u.astype(bf16)
            a_ref[rows, :] = (_silu(g) * u).astype(bf16)
    w_spec = pl.BlockSpec((None, D, FF_SH), lambda k, i: (k, l, 0))
    o_spec = pl.BlockSpec((None, tm, FF_SH), lambda k, i: (k, i, 0))
    s = SDS((4, T, FF_SH), bf16)
    return pl.pallas_call(kern, name=name, grid=(4, T // tm), in_specs=[pl.BlockSpec((tm, D), lambda k, i: (i, 0)), w_spec, w_spec],
                          out_specs=[o_spec] * 3, out_shape=[s, s, s],
                          compiler_params=pltpu.CompilerParams(dimension_semantics=("parallel", "parallel")))(n_bf, wg4, wu4)


def ffn_dgu(name, T, tm, d_f, wd4, gate4, up4, l):
    rc = 16

    sub = min(tm, MM_SUB)

    def kern(df_ref, wd_ref, g_ref, u_ref, dg_ref, du_ref, dact_ref):
        for s0 in range(0, tm, sub):
            dact_ref[pl.ds(s0, sub), :] = _bdot(df_ref[pl.ds(s0, sub), :], wd_ref[...], NT)
            for r0 in range(s0, s0 + sub, rc):
                rows = pl.ds(r0, rc)
                _, vjp = jax.vjp(lambda a, b: _silu(a) * b, g_ref[rows, :].astype(f32), u_ref[rows, :].astype(f32))
                dg, du = vjp(dact_ref[rows, :])
                dg_ref[rows, :] = dg.astype(bf16)
                du_ref[rows, :] = du.astype(bf16)
    a_spec = pl.BlockSpec((None, tm, FF_SH), lambda k, i: (k, i, 0))
    s = SDS((4, T, FF_SH), bf16)
    return pl.pallas_call(kern, name=name, grid=(4, T // tm),
                          in_specs=[pl.BlockSpec((tm, D), lambda k, i: (i, 0)), pl.BlockSpec((None, FF_SH, D), lambda k, i: (k, l, 0)), a_spec, a_spec],
                          out_specs=[a_spec] * 2, out_shape=[s, s], scratch_shapes=[pltpu.VMEM((tm, FF_SH), f32)],
                          compiler_params=pltpu.CompilerParams(dimension_semantics=("parallel", "parallel")))(d_f, wd4, gate4, up4)


def rowcall(name, body, T, tm, ins, outs, accs=(), scratch=(), reverse=False, sub=None):
    n = T // tm
    assert T % tm == 0

    def blk(i):
        return (n - 1 - i) if reverse else i

    in_specs, args = [], []
    for spec in ins:
        kind, arr = spec[0], spec[1]
        if kind == "row":
            _, _, w, cb = spec
            in_specs.append(pl.BlockSpec((tm, w), lambda i, cb=cb: (blk(i), cb)))
        elif kind == "prev":
            _, _, w, cb, h = spec
            r = tm // h
            in_specs.append(pl.BlockSpec((h, w), lambda i, cb=cb, r=r: (jnp.maximum(blk(i) * r - 1, 0), cb)))
        elif kind == "next":
            _, _, w, cb, h = spec
            r = tm // h
            in_specs.append(pl.BlockSpec((h, w), lambda i, cb=cb, r=r, h=h: (jnp.minimum((blk(i) + 1) * r, T // h - 1), cb)))
        else:
            nd = arr.ndim
            in_specs.append(pl.BlockSpec(arr.shape, lambda i, nd=nd: (0,) * nd))
        args.append(arr)
    out_shape = [SDS((T, w), dt) for w, dt in outs] + [SDS(tuple(s), f32) for s in accs]
    out_specs = [pl.BlockSpec((tm, w), lambda i: (blk(i), 0)) for w, _ in outs]
    out_specs += [pl.BlockSpec(tuple(s), lambda i, nd=len(s): (0,) * nd) for s in accs]
    ni, no, na = len(ins), len(outs), len(accs)

    def kern(*refs):
        i = pl.program_id(0)
        in_refs, out_refs = refs[:ni], refs[ni:ni + no]
        acc_refs, scr = refs[ni + no:ni + no + na], refs[ni + no + na:]
        if na:
            @pl.when(i == 0)
            def _():
                for a in acc_refs:
                    a[...] = jnp.zeros(a.shape, f32)
        if sub is None or sub >= tm:
            body(blk(i), n, in_refs, out_refs, acc_refs, scr)
        else:
            for r0 in range(0, tm, sub):
                rows = pl.ds(r0, sub)
                body(blk(i), n, [r.at[rows, :] if spec[0] == "row" else r for r, spec in zip(in_refs, ins)],
                     [o.at[rows, :] for o in out_refs], acc_refs, [s.at[rows, :] for s in scr])

    res = pl.pallas_call(
        kern, name=name, grid=(n,), in_specs=in_specs, out_specs=out_specs, out_shape=out_shape,
        scratch_shapes=list(scratch),
        compiler_params=pltpu.CompilerParams(dimension_semantics=("arbitrary",)),
    )(*args)
    return res


def rms_to_bf16(name, T, tm, x, g):
    def body(i, n, ins, outs, accs, scr):
        outs[0][...] = _rms(ins[0][...], ins[1][...]).astype(bf16)
    return rowcall(name, body, T, tm, [("row", x, D, 0), ("const", g)], [(D, bf16)], sub=64)[0]


def _layer_norm_parts(x):
    mu = jnp.mean(x, axis=-1, keepdims=True)
    xc = x - mu
    r = lax.rsqrt(jnp.mean(xc * xc, axis=-1, keepdims=True) + EPS)
    return xc * r, r


def gmlp_fwd(name, T, tm, uvz, ln_g, ln_b, wm, bs):
    def body(i, n, ins, outs, accs, scr):
        gu = jax.nn.gelu(ins[0][...])
        xh, _ = _layer_norm_parts(jax.nn.gelu(ins[1][...]))
        vln = (xh * ins[2][...] + ins[3][...]).astype(bf16)
        for c in range(ins[0].shape[0] // CHUNK):
            rows = slice(c * CHUNK, (c + 1) * CHUNK)
            for h in range(GM_HEADS):
                cols = slice(h * GM_HD, (h + 1) * GM_HD)
                mixed = jnp.dot(ins[4][h], vln[rows, cols], preferred_element_type=f32) + ins[5][h]
                outs[0][rows, cols] = (gu[rows, cols] * mixed).astype(bf16)
    return rowcall(name, body, T, tm, [("row", uvz, D, 0), ("row", uvz, D, 1), ("const", ln_g), ("const", ln_b), ("const", wm), ("const", bs)],
                   [(D, bf16)], sub=CHUNK)[0]


def gmlp_bwd(name, T, tm, uvz, d_ya, d_cb, ln_g, ln_b, wm, bs):
    def body(i, n, ins, outs, accs, scr):
        u, v, dya = ins[0][...], ins[1][...], ins[2][...]
        gu, gelu_u_vjp = jax.vjp(jax.nn.gelu, u)
        gv, gelu_v_vjp = jax.vjp(jax.nn.gelu, v)
        xh, r = _layer_norm_parts(gv)
        lng = ins[3][...]
        vln = (xh * lng + ins[4][...]).astype(bf16)
        rr = lax.broadcasted_iota(jnp.int32, (CHUNK, CHUNK), 0)
        cc = lax.broadcasted_iota(jnp.int32, (CHUNK, CHUNK), 1)
        causal = (rr >= cc).astype(f32)
        dvln_ref = scr[0]
        dgu_ref = scr[1]
        for c in range(ins[0].shape[0] // CHUNK):
            rows = slice(c * CHUNK, (c + 1) * CHUNK)
            for h in range(GM_HEADS):
                cols = slice(h * GM_HD, (h + 1) * GM_HD)
                w = ins[5][h]
                blk = vln[rows, cols]
                mixed = jnp.dot(w, blk, preferred_element_type=f32) + ins[6][h]
                dy = dya[rows, cols]
                dgu_ref[rows, cols] = dy * mixed
                dm = dy * gu[rows, cols]
                accs[3][h] += jnp.sum(dm, axis=1, keepdims=True)
                accs[2][h] += _bdot(dm, blk, NT) * causal
                dvln_ref[rows, cols] = _bdot(w, dm, TN)
        dvln = dvln_ref[...]
        accs[0][...] += jnp.sum(dvln * xh, axis=0, keepdims=True)
        accs[1][...] += jnp.sum(dvln, axis=0, keepdims=True)
        dxh = dvln * lng
        dgv = r * (dxh - jnp.mean(dxh, axis=-1, keepdims=True) - xh * jnp.mean(dxh * xh, axis=-1, keepdims=True))
        outs[0][...] = gelu_u_vjp(dgu_ref[...])[0].astype(bf16)
        outs[1][...] = gelu_v_vjp(dgv)[0].astype(bf16)
    return rowcall(name, body, T, tm,
                   [("row", uvz, D, 0), ("row", uvz, D, 1), ("row", d_ya, D, d_cb), ("const", ln_g), ("const", ln_b), ("const", wm), ("const", bs)],
                   [(D, bf16), (D, bf16)], accs=[(1, D), (1, D), (GM_HEADS, CHUNK, CHUNK), (GM_HEADS, CHUNK, 1)],
                   scratch=[pltpu.VMEM((tm, D), f32), pltpu.VMEM((tm, D), f32)], sub=CHUNK)


CONV_RC, CONV_LB = 32, 512


def _conv_fill(i, x_ref, halo_ref, scr, tm):
    scr[pl.ds(0, CONV_HALO), :] = jnp.where(i > 0, halo_ref[...], 0.0)
    scr[pl.ds(CONV_HALO, tm), :] = x_ref[...]


def _conv_taps(scr, r0, lanes):
    return [scr[pl.ds(r0 + CONV_HALO - (CONV_K - 1) + k, CONV_RC), lanes] for k in range(CONV_K)]


def conv_fwd(name, T, tm, xbc, conv_w, conv_b):
    def body(i, n, ins, outs, accs, scr):
        s = scr[0]
        _conv_fill(i, ins[0], ins[1], s, tm)
        for lb in range(CONV_DIM // CONV_LB):
            lanes = slice(lb * CONV_LB, (lb + 1) * CONV_LB)
            w, b = ins[2][:, lanes], ins[3][:, lanes]

            for r0 in range(0, tm, CONV_RC):
                taps = _conv_taps(s, r0, lanes)
                pre = b + sum(w[k:k + 1] * taps[k] for k in range(CONV_K))
                outs[0][pl.ds(r0, CONV_RC), lanes] = _silu(pre)
    return rowcall(name, body, T, tm, [("row", xbc, CONV_DIM, 0), ("prev", xbc, CONV_DIM, 0, CONV_HALO), ("const", conv_w), ("const", conv_b)],
                   [(CONV_DIM, f32)], scratch=[pltpu.VMEM((tm + CONV_HALO, CONV_DIM), f32)])[0]


def conv_bwd_pre(name, T, tm, xbc, d_xc, conv_w, conv_b):
    def body(i, n, ins, outs, accs, scr):
        s = scr[0]
        _conv_fill(i, ins[0], ins[1], s, tm)
        fold = lambda v: jnp.sum(v.reshape(CONV_RC // 8, 8, CONV_LB), axis=0)
        for lb in range(CONV_DIM // CONV_LB):
            lanes = slice(lb * CONV_LB, (lb + 1) * CONV_LB)
            w, b = ins[3][:, lanes], ins[4][:, lanes]

            sums = [jnp.zeros((8, CONV_LB), f32)] * (CONV_K + 1)
            for r0 in range(0, tm, CONV_RC):
                taps = _conv_taps(s, r0, lanes)
                pre = b + sum(w[k:k + 1] * taps[k] for k in range(CONV_K))
                _, vjp = jax.vjp(_silu, pre)
                dpre = vjp(ins[2][pl.ds(r0, CONV_RC), lanes])[0]
                outs[0][pl.ds(r0, CONV_RC), lanes] = dpre
                sums = [sums[k] + fold(dpre * taps[k]) for k in range(CONV_K)] + [sums[CONV_K] + fold(dpre)]
            for k in range(CONV_K):
                accs[0][pl.ds(k, 1), lanes] += jnp.sum(sums[k], axis=0, keepdims=True)
            accs[1][:, lanes] += jnp.sum(sums[CONV_K], axis=0, keepdims=True)
    return rowcall(name, body, T, tm,
                   [("row", xbc, CONV_DIM, 0), ("prev", xbc, CONV_DIM, 0, CONV_HALO), ("row", d_xc, CONV_DIM, 0), ("const", conv_w), ("const", conv_b)],
                   [(CONV_DIM, f32)], accs=[(CONV_K, CONV_DIM), (1, CONV_DIM)], scratch=[pltpu.VMEM((tm + CONV_HALO, CONV_DIM), f32)])


def conv_bwd_x(name, T, tm, d_pre, conv_w):
    def body(i, n, ins, outs, accs, scr):
        s = scr[0]
        s[pl.ds(0, tm), :] = ins[0][...]
        s[pl.ds(tm, CONV_HALO), :] = jnp.where(i < n - 1, ins[1][...], 0.0)
        for lb in range(CONV_DIM // CONV_LB):
            lanes = slice(lb * CONV_LB, (lb + 1) * CONV_LB)
            w = ins[2][:, lanes]

            for r0 in range(0, tm, CONV_RC):
                dx = sum(w[k:k + 1] * s[pl.ds(r0 + CONV_K - 1 - k, CONV_RC), lanes] for k in range(CONV_K))
                outs[0][pl.ds(r0, CONV_RC), lanes] = dx.astype(bf16)
    return rowcall(name, body, T, tm, [("row", d_pre, CONV_DIM, 0), ("next", d_pre, CONV_DIM, 0, CONV_HALO), ("const", conv_w)],
                   [(CONV_DIM, bf16)], scratch=[pltpu.VMEM((tm + CONV_HALO, CONV_DIM), f32)])[0]


def _ssd_prep(dtr, dtb, alog):
    rr = lax.broadcasted_iota(jnp.int32, (CHUNK, CHUNK), 0)
    cc = lax.broadcasted_iota(jnp.int32, (CHUNK, CHUNK), 1)
    dt = _softplus(dtr + dtb)
    dA = dt * -jnp.exp(alog)
    acum = jnp.dot((rr >= cc).astype(f32), dA, precision=HI, preferred_element_type=f32)
    return dt, acum, acum.T, jnp.sum(dA, axis=0, keepdims=True)


def _ssd_group(g, x, Bm, Cm, S, dt, acum, acumT, tot, dsk):
    rr = lax.broadcasted_iota(jnp.int32, (CHUNK, CHUNK), 0)
    cc = lax.broadcasted_iota(jnp.int32, (CHUNK, CHUNK), 1)
    tril = rr >= cc
    lane = lax.broadcasted_iota(jnp.int32, (1, DT_PAD), 1)
    sub = lax.broadcasted_iota(jnp.int32, (DT_PAD, 1), 0)
    glane = lax.broadcasted_iota(jnp.int32, (1, SSM_HPG * SSM_P), 1) // SSM_P
    hm = [(glane == r).astype(f32) for r in range(SSM_HPG)]
    pick = lambda v, r: jnp.sum(v * (lane == SSM_HPG * g + r).astype(f32), axis=1, keepdims=True)
    cols = [pick(acum, r) for r in range(SSM_HPG)]
    tots = [pick(tot, r) for r in range(SSM_HPG)]
    spread = lambda vals: sum(vals[r] * hm[r] for r in range(SSM_HPG))
    xdt = x * spread([pick(dt, r) for r in range(SSM_HPG)])
    cb = _bdot(Cm, Bm, NT)
    y = x * spread([pick(dsk, r) for r in range(SSM_HPG)])
    for r in range(SSM_HPG):
        row = jnp.sum(acumT * (sub == SSM_HPG * g + r).astype(f32), axis=0, keepdims=True)
        dec = jnp.exp(jnp.where(tril, cols[r] - row, -jnp.inf))
        y = y + _bdot(cb * dec, xdt * hm[r])
    y = y + _bdot(Cm, S) * spread([jnp.exp(c) for c in cols])
    dte = spread([jnp.exp(tots[r] - cols[r]) for r in range(SSM_HPG)])
    s_new = S * spread([jnp.exp(t) for t in tots]) + _bdot(Bm, xdt * dte, TN)
    return y, s_new


def _ssd_ins(xc, dtr):
    gw = SSM_HPG * SSM_P
    ins = [("row", xc, gw, g) for g in range(SSM_GROUPS)]
    ins += [("row", xc, SSM_N, D // SSM_N + g) for g in range(SSM_GROUPS)]
    ins += [("row", xc, SSM_N, D // SSM_N + SSM_GROUPS + g) for g in range(SSM_GROUPS)]
    ins += [("row", dtr, DT_PAD, 0)]
    return ins


SSD_CPS = 2


def ssd_fwd(name, T, xc, dtr, dtb, alog, dsk):
    gw = SSM_HPG * SSM_P

    def body(i, n, ins, outs, accs, scr):
        S = scr[0]

        @pl.when(i == 0)
        def _():
            S[...] = jnp.zeros(S.shape, f32)
        S4 = tuple(S[:, g * gw:(g + 1) * gw] for g in range(4))
        for c in range(SSD_CPS):
            rows = pl.ds(c * CHUNK, CHUNK)
            X4 = tuple(ins[g][rows, :] for g in range(4))
            B4 = tuple(ins[4 + g][rows, :] for g in range(4))
            C4 = tuple(ins[8 + g][rows, :] for g in range(4))
            prep = _ssd_prep(ins[12][rows, :], ins[13][...], ins[14][...])
            nxt = []
            for g in range(4):
                outs[1][rows, g * gw:(g + 1) * gw] = S4[g]
                y, s_new = _ssd_group(g, X4[g], B4[g], C4[g], S4[g], *prep, ins[15][...])
                outs[0][rows, g * gw:(g + 1) * gw] = y
                nxt.append(s_new)
            S4 = tuple(nxt)
        for g in range(4):
            S[:, g * gw:(g + 1) * gw] = S4[g]
    ins = _ssd_ins(xc, dtr) + [("const", dtb), ("const", alog), ("const", dsk)]
    return rowcall(name, body, T, SSD_CPS * CHUNK, ins, [(D, f32), (D, f32)], scratch=[pltpu.VMEM((SSM_N, D), f32)])


def ssd_bwd(name, T, xc, dtr, sprev, d_y, dtb, alog, dsk):
    gw = SSM_HPG * SSM_P

    def body(i, n, ins, outs, accs, scr):
        dS = scr[0]

        @pl.when(i == n - 1)
        def _():
            dS[...] = jnp.zeros(dS.shape, f32)
        dS4 = tuple(dS[:, g * gw:(g + 1) * gw] for g in range(4))
        def chunk(X4, dtr_c, B4, C4, S4, dtb_c, alog_c, dsk_c):
            prep = _ssd_prep(dtr_c, dtb_c, alog_c)
            res = [_ssd_group(g, X4[g], B4[g], C4[g], S4[g], *prep, dsk_c) for g in range(4)]
            return tuple(r[0] for r in res), tuple(r[1] for r in res)
        X4 = tuple(ins[g][...] for g in range(4))
        B4 = tuple(ins[4 + g][...] for g in range(4))
        C4 = tuple(ins[8 + g][...] for g in range(4))
        S4 = tuple(ins[13 + g][...] for g in range(4))
        dY4 = tuple(ins[17 + g][...] for g in range(4))
        _, vjp = jax.vjp(chunk, X4, ins[12][...], B4, C4, S4, ins[21][...], ins[22][...], ins[23][...])
        dX4, ddtr, dB4, dC4, dS4, ddtb, dalog, ddsk = vjp((dY4, dS4))
        for g in range(4):
            outs[0][:, g * gw:(g + 1) * gw] = dX4[g]
            outs[0][:, D + g * SSM_N:D + (g + 1) * SSM_N] = dB4[g]
            outs[0][:, D + (SSM_GROUPS + g) * SSM_N:D + (SSM_GROUPS + g + 1) * SSM_N] = dC4[g]
            dS[:, g * gw:(g + 1) * gw] = dS4[g]
        outs[1][...] = ddtr.astype(bf16)
        accs[0][...] += ddtb
        accs[1][...] += dalog
        accs[2][...] += ddsk
    ins = _ssd_ins(xc, dtr) + [("row", sprev, gw, g) for g in range(4)] + [("row", d_y, gw, g) for g in range(4)]
    ins += [("const", dtb), ("const", alog), ("const", dsk)]
    return rowcall(name, body, T, CHUNK, ins, [(CONV_DIM, f32), (DT_PAD, bf16)], accs=[(1, DT_PAD)] * 3,
                   scratch=[pltpu.VMEM((SSM_N, D), f32)], reverse=True)


def _gate_group(y, z, g):
    return _rms(y * _silu(z), g)


def gate_fwd(name, T, tm, y, uvz, gn):
    def body(i, n, ins, outs, accs, scr):
        for g in range(SSM_GROUPS):
            cols = slice(g * 256, (g + 1) * 256)
            outs[0][:, cols] = _gate_group(ins[0][:, cols], ins[1][:, cols], ins[2][:, cols]).astype(bf16)
    return rowcall(name, body, T, tm, [("row", y, D, 0), ("row", uvz, D, 2), ("const", gn)], [(D, bf16)], sub=64)[0]


def gate_bwd(name, T, tm, y, uvz, d_yb, d_cb, gn):
    def body(i, n, ins, outs, accs, scr):
        for g in range(SSM_GROUPS):
            cols = slice(g * 256, (g + 1) * 256)
            _, vjp = jax.vjp(_gate_group, ins[0][:, cols], ins[1][:, cols], ins[3][:, cols])
            dy, dz, dg = vjp(ins[2][:, cols])
            outs[0][:, cols] = dy
            outs[1][:, cols] = dz.astype(bf16)
            accs[0][:, cols] += dg
    return rowcall(name, body, T, tm, [("row", y, D, 0), ("row", uvz, D, 2), ("row", d_yb, D, d_cb), ("const", gn)],
                   [(D, f32), (D, bf16)], accs=[(1, D)], sub=64)


def _window_sum(src, cols, levels, tm, lv, trailing):
    cur, cur_cols = src, cols
    for l in range(1, levels + 1):
        shift = 2 ** (l - 1)
        last = l == levels
        if trailing:
            start = POOL_HALO if last else 8 * l
            rows = tm if last else tm + POOL_HALO - start
            new = cur[pl.ds(start, rows), cur_cols] + cur[pl.ds(start - shift, rows), cur_cols]
        else:
            start = 0
            rows = tm if last else tm + POOL_HALO - 8 * l
            new = cur[pl.ds(0, rows), cur_cols] + cur[pl.ds(shift, rows), cur_cols]
        if last:
            return new
        nxt = lv[l % 2]
        nxt[pl.ds(start, rows), :] = new
        cur, cur_cols = nxt, slice(None)


def _pool_diff(i, tm, h_ref, halo_ref, g_ref, scr, lv):
    g = g_ref[...]
    yn = _rms(h_ref[...], g)
    scr[pl.ds(0, POOL_HALO), :] = jnp.where(i > 0, _rms(halo_ref[...], g), 0.0)
    scr[pl.ds(POOL_HALO, tm), :] = yn
    pos = (i * tm + lax.broadcasted_iota(jnp.int32, (tm, 1), 0) + 1).astype(f32)
    parts = []
    for gi, win in enumerate(POOL_WINDOWS):
        cols = slice(gi * POOL_GD, (gi + 1) * POOL_GD)
        s = _window_sum(scr, cols, gi + 1, tm, lv, True)
        parts.append(s * (1.0 / jnp.minimum(pos, float(win))) - yn[:, cols])
    return parts


def pool_fwd(name, T, tm, h2, g_pre, pw, pb, psc, g_post, g_next):
    def body(i, n, ins, outs, accs, scr):
        parts = _pool_diff(i, tm, ins[0], ins[1], ins[2], scr[0], scr[1:3])
        for gi in range(len(POOL_WINDOWS)):
            cols = slice(gi * POOL_GD, (gi + 1) * POOL_GD)
            o = _bdot(parts[gi], ins[3][gi]) + ins[4][:, cols]
            outs[0][:, cols] = o * ins[5][:, cols]
        h = ins[0][...] + _rms(outs[0][...], ins[6][...])
        outs[1][...] = h
        outs[2][...] = _rms(h, ins[7][...]).astype(bf16)
    return rowcall(name, body, T, tm, [("row", h2, D, 0), ("prev", h2, D, 0, POOL_HALO), ("const", g_pre), ("const", pw), ("const", pb), ("const", psc),
                                       ("const", g_post), ("const", g_next)],
                   [(D, f32), (D, f32), (D, bf16)], scratch=[pltpu.VMEM((tm + POOL_HALO, D), f32)] + [pltpu.VMEM((tm + POOL_HALO, POOL_GD), f32)] * 2)


def pool_bwd(name, T, tm, h2, d_pm, d_res, g_pre, pw, pb, psc, f_prev, g_prev):
    def body(i, n, ins, outs, accs, scr):
        parts = _pool_diff(i, tm, ins[0], ins[1], ins[5], scr[0], scr[3:5])
        dpm = ins[2][...]
        psc_v = ins[8][...]
        dps = dpm * psc_v
        dps_halo = jnp.where(i < n - 1, ins[3][...] * psc_v, 0.0)
        accs[1][...] += jnp.sum(dps, axis=0, keepdims=True)
        pos = (i * tm + lax.broadcasted_iota(jnp.int32, (tm, 1), 0) + 1).astype(f32)
        pos_h = ((i + 1) * tm + lax.broadcasted_iota(jnp.int32, (POOL_HALO, 1), 0) + 1).astype(f32)
        r_scr = scr[1]
        dyn_scr = scr[2]
        for gi, win in enumerate(POOL_WINDOWS):
            cols = slice(gi * POOL_GD, (gi + 1) * POOL_GD)
            w = ins[6][gi]
            o = _bdot(parts[gi], w) + ins[7][:, cols]
            accs[2][:, cols] += jnp.sum(dpm[:, cols] * o, axis=0, keepdims=True)
            accs[0][gi] += _bdot(parts[gi], dps[:, cols], TN)
            q = _bdot(dps[:, cols], w, NT)
            qh = _bdot(dps_halo[:, cols], w, NT)
            r_scr[pl.ds(0, tm), cols] = q * (1.0 / jnp.minimum(pos, float(win)))
            r_scr[pl.ds(tm, POOL_HALO), cols] = qh * (1.0 / jnp.minimum(pos_h, float(win)))
            dyn_scr[:, cols] = _window_sum(r_scr, cols, gi + 1, tm, scr[3:5], False) - q
        dx, dg = _rms_bwd(ins[0][...], ins[5][...], dyn_scr[...])
        dh = ins[4][...] + dx
        outs[0][...] = dh
        accs[3][...] += dg
        df, dgp = _rms_bwd(ins[9][...], ins[10][...], dh)
        outs[1][...] = df.astype(bf16)
        accs[4][...] += dgp
    ins = [("row", h2, D, 0), ("prev", h2, D, 0, POOL_HALO), ("row", d_pm, D, 0), ("next", d_pm, D, 0, POOL_HALO), ("row", d_res, D, 0),
           ("const", g_pre), ("const", pw), ("const", pb), ("const", psc), ("row", f_prev, D, 0), ("const", g_prev)]
    return rowcall(name, body, T, tm, ins, [(D, f32), (D, bf16)], accs=[(4, POOL_GD, POOL_GD), (1, D), (1, D), (1, D), (1, D)],
                   scratch=[pltpu.VMEM((tm + POOL_HALO, D), f32), pltpu.VMEM((tm + POOL_HALO, D), f32), pltpu.VMEM((tm, D), f32)]
                   + [pltpu.VMEM((tm + POOL_HALO, POOL_GD), f32)] * 2)


def local_step(T, x, tgt, W, ffn_weights, early_grads):
    tm = 512 if T >= 1024 else T // 2
    TKW = 4096 if T >= 4096 else T
    ng = W["norm_g"]
    g = lambda l, j: ng[l, j][None, :]
    G = {}

    tf = tm
    once = pl.Buffered(1)
    vec_f = pl.BlockSpec((1, D), lambda i: (0, 0))

    def fused_specs(t):
        rows = pl.BlockSpec((t, D), lambda i: (i, 0))
        return rows, [pl.BlockSpec((None, t, FF_SH), lambda i, s=s: (s, i, 0)) for s in range(4)], (SDS((T, D), f32), rows), (SDS((T, D), bf16), rows)
    rows_f, sh_f, out_f32, out_bf16 = fused_specs(tf)
    tf2 = min(T, 2 * tm)
    rows_f2, sh_f2, out2_f32, out2_bf16 = fused_specs(tf2)

    def resid_epilogue(with_pre):
        def ep(part, xs, os, accs):
            h = xs[0][...] + _rms(part, xs[1][...])
            os[0][...] = part
            os[1][...] = h
            if with_pre:
                os[2][...] = _rms(h, xs[2][...]).astype(bf16)
        return ep

    def bwd_epilogue(df_dtype):
        def ep(part, xs, os, accs):
            dx, dgp = _rms_bwd(xs[0][...], xs[3][...], part)
            dh = xs[2][...] + dx
            df, dgq = _rms_bwd(xs[1][...], xs[4][...], dh)
            os[0][...] = dh
            os[1][...] = df.astype(df_dtype)
            accs[0][...] += dgp
            accs[1][...] += dgq
        return ep

    def loss_epilogue(part, xs, os, accs):
        g_post = xs[2][...]
        e = xs[0][...] + _rms(part, g_post) - xs[1][...]
        accs[0][...] += jnp.sum(jnp.sum(e * e, axis=-1, keepdims=True) * (0.5 / D), axis=0, keepdims=True)
        dh = e * (1.0 / D)
        df, dg = _rms_bwd(part, g_post, dh)
        os[0][...] = dh
        os[1][...] = df.astype(bf16)
        accs[1][...] += dg

    def ffn_fwd(tag, n_bf, l, resid=None, loss=None):
        gate4, up4, act4 = ffn_up(f"ffn{tag}_up", T, min(T, 4 * tm), n_bf, W["wg4"], W["wu4"], l)
        wd_f = [pl.BlockSpec((None, FF_SH, D), lambda i, s=s: (s, l, 0), pipeline_mode=once) for s in range(4)]
        pairs = [(act4, sh_f2[s], W["wd4"], wd_f[s]) for s in range(4)]
        if loss is not None:
            return (gate4, up4, act4) + tuple(mm_fused(f"ffn{tag}_down", T // tf2, pairs, NN, [(loss[0], rows_f2), (loss[1], rows_f2), (loss[2], vec_f)],
                                                       [out2_f32, out2_bf16], [(1, 1), (1, D)], loss_epilogue))
        f, h_out = mm_fused(f"ffn{tag}_down", T // tf2, pairs, NN, [(resid[0], rows_f2), (resid[1], vec_f)], [out2_f32, out2_f32], [],
                            resid_epilogue(False))
        return gate4, up4, act4, f, h_out

    def ffn_bwd(tag, l, n_bf, gate4, up4, act4, d_f, h_out, f_pre, d_res, g_pre, g_post, df_dtype):
        d_gate4, d_up4 = ffn_dgu(f"ffn{tag}_dgu", T, min(T, 4 * tm), d_f, W["wd4"], gate4, up4, l)
        w_f = [pl.BlockSpec((None, D, FF_SH), lambda i, s=s: (s, l, 0), pipeline_mode=once) for s in range(4)]
        d_h, d_fp, dgp, dgq = mm_fused(
            f"ffn{tag}_dn", T // tf, [(d_gate4, sh_f[s], W["wg4"], w_f[s]) for s in range(4)] + [(d_up4, sh_f[s], W["wu4"], w_f[s]) for s in range(4)],
            NT, [(h_out, rows_f), (f_pre, rows_f), (d_res, rows_f), (g_pre, vec_f), (g_post, vec_f)],
            [out_f32, (SDS((T, D), df_dtype), rows_f)], [(1, D), (1, D)], bwd_epilogue(df_dtype))

        def wgrad(nm, a4, b):
            return mm(nm, (4, 1, T // TKW),
                      [(a4, pl.BlockSpec((None, TKW, FF_SH), lambda s, j, k: (s, k, 0)), b, pl.BlockSpec((TKW, D), lambda s, j, k: (k, 0)))],
                      TN, pl.BlockSpec((None, FF_SH, D), lambda s, j, k: (s, 0, 0)), SDS((4, FF_SH, D), f32))
        return d_h, d_fp, dgp, dgq, wgrad(f"ffn{tag}_dwg", d_gate4, n_bf), wgrad(f"ffn{tag}_dwu", d_up4, n_bf), wgrad(f"ffn{tag}_dwd", act4, d_f)

    y0 = rms_to_bf16("l0_prenorm", T, tf2, x, g(0, 0))
    uvz = matmul("in_uvz", [(y0, W["w_uvz"])], "nn", f32, 4 * tm, 1024)
    xbc = matmul("in_xbc", [(y0, W["w_xbc"])], "nn", f32, 4 * tm, 1024)
    dtr = matmul("in_dt", [(y0, W["w_dt"])], "nn", f32, 4 * tm, DT_PAD)
    y_a = gmlp_fwd("gmlp_fwd", T, tf2, uvz, W["ln_g"], W["ln_b"], W["wm"], W["bs"])
    xc = conv_fwd("conv_fwd", T, tm, xbc, W["conv_w"], W["conv_b"])
    y_ssd, sprev = ssd_fwd("ssd_fwd", T, xc, dtr, W["dtb"], W["alog"], W["dsk"])
    y_b = gate_fwd("gate_fwd", T, tf2, y_ssd, uvz, W["gn"])
    half = D // 2
    wo4 = W["wo4"]
    ycol = [pl.BlockSpec((tf2, half), lambda i, cb=cb: (i, cb)) for cb in range(2)]
    wo_s = [pl.BlockSpec((None, half, D), lambda i, s=s: (s, 0, 0), pipeline_mode=once) for s in range(4)]
    mixo, h1, n1 = mm_fused("out_proj", T // tf2, [(y_a, ycol[0], wo4, wo_s[0]), (y_a, ycol[1], wo4, wo_s[1]),
                                                  (y_b, ycol[0], wo4, wo_s[2]), (y_b, ycol[1], wo4, wo_s[3])], NN,
                            [(x, rows_f2), (g(0, 1), vec_f), (g(0, 2), vec_f)], [out2_f32, out2_f32, out2_bf16], [], resid_epilogue(True))
    W = dict(W)
    W["wg4"], W["wu4"], W["wd4"] = ffn_weights(h1)
    gate0, up0, act0, f1, h2 = ffn_fwd("0", n1, 0, resid=(h1, g(0, 3)))
    pm, h3, n3 = pool_fwd("pool_fwd", T, tm, h2, g(1, 0), W["pool_w"], W["pool_b"], W["pool_scale"], g(1, 1), g(1, 2))
    gate1, up1, act1, dh4, d_f2, loss_acc, dg13 = ffn_fwd("1", n3, 1, loss=(h3, tgt, g(1, 3)))
    d_h3, d_pm, dg12, dg11, dwg1, dwu1, dwd1 = ffn_bwd("1", 1, n3, gate1, up1, act1, d_f2, h3, pm, dh4, g(1, 2), g(1, 1), f32)
    d_h2, d_f1, G["pool_w"], G["pool_b"], G["pool_scale"], dg10, dg03 = pool_bwd("pool_bwd", T, tm, h2, d_pm, d_h3, g(1, 0), W["pool_w"], W["pool_b"],
                                                                                 W["pool_scale"], f1, g(0, 3))
    d_h1, d_mixo, dg02, dg01, dwg0, dwu0, dwd0 = ffn_bwd("0", 0, n1, gate0, up0, act0, d_f1, h1, mixo, d_h2, g(0, 2), g(0, 1), bf16)
    def d_wo(nm, y):
        return mm(nm, (2, 1, T // TKW), [(y, pl.BlockSpec((TKW, half), lambda s, j, k: (k, s)), d_mixo, pl.BlockSpec((TKW, D), lambda s, j, k: (k, 0)))],
                  TN, pl.BlockSpec((None, half, D), lambda s, j, k: (s, 0, 0)), SDS((2, half, D), f32))
    d_ycat = matmul("out_proj_dy", [(d_mixo, wo4.reshape(4 * half, D))], "nt", f32, 4 * tm, 1024)
    dwo_a, dwo_b = d_wo("out_proj_dwa", y_a), d_wo("out_proj_dwb", y_b)
    G["wo4"] = [dwo_a[0], dwo_a[1], dwo_b[0], dwo_b[1]]
    G["wgT4"], G["wuT4"], G["wd4"] = [dwg0, dwg1], [dwu0, dwu1], [dwd0, dwd1]
    token = early_grads(G)
    d_yssd, d_z, G["gn"] = gate_bwd("gate_bwd", T, tf2, y_ssd, uvz, d_ycat, 1, W["gn"] + token[0, 0])
    d_xc, d_dtr, G["dtb"], G["alog"], G["dsk"] = ssd_bwd("ssd_bwd", T, xc, dtr, sprev, d_yssd, W["dtb"], W["alog"], W["dsk"])
    d_pre, G["conv_w"], G["conv_b"] = conv_bwd_pre("conv_bwd_pre", T, tm, xbc, d_xc, W["conv_w"], W["conv_b"])
    d_xbc = conv_bwd_x("conv_bwd_x", T, tm, d_pre, W["conv_w"])
    d_u, d_v, G["ln_g"], G["ln_b"], G["wm"], G["bs"] = gmlp_bwd("gmlp_bwd", T, tf2, uvz, d_ycat, 0, W["ln_g"], W["ln_b"], W["wm"], W["bs"])
    w_u, w_v, w_z = W["w_uvz"][:, :D], W["w_uvz"][:, D:2 * D], W["w_uvz"][:, 2 * D:]
    def pre_epilogue(part, xs, os, accs):
        dx, dg = _rms_bwd(xs[0][...], xs[2][...], part)
        os[0][...] = xs[1][...] + dx
        accs[0][...] += dg
    blk = lambda w: pl.BlockSpec((tf, w), lambda i: (i, 0))
    whole = lambda a: pl.BlockSpec(a.shape, lambda i: (0, 0), pipeline_mode=once)
    grad_x, dg00 = mm_fused("in_dy0", T // tf, [(d_u, blk(D), w_u, whole(w_u)), (d_v, blk(D), w_v, whole(w_v)), (d_z, blk(D), w_z, whole(w_z)),
                                                (d_xbc, blk(CONV_DIM), W["w_xbc"], whole(W["w_xbc"])), (d_dtr, blk(DT_PAD), W["w_dt"], whole(W["w_dt"]))],
                            NT, [(x, rows_f), (d_h1, rows_f), (g(0, 0), vec_f)], [out_f32], [(1, D)], pre_epilogue)
    G["w_inT"] = [matmul("in_dwu", [(d_u, y0)], "tn", f32, 1024, 1024, TKW), matmul("in_dwv", [(d_v, y0)], "tn", f32, 1024, 1024, TKW),
                  matmul("in_dwz", [(d_z, y0)], "tn", f32, 1024, 1024, TKW), matmul("in_dwxbc", [(d_xbc, y0)], "tn", f32, 1024, 1024, TKW),
                  matmul("in_dwdt", [(d_dtr, y0)], "tn", f32, DT_PAD, 1024, TKW)[:N_HEADS]]
    G["norm_g"] = jnp.stack([jnp.concatenate([dg00, dg01, dg02, dg03], 0), jnp.concatenate([dg10, dg11, dg12, dg13], 0)])
    return loss_acc, grad_x, G


def build_weights(Wf):
    causal = jnp.tril(jnp.ones((CHUNK, CHUNK), bool))
    w_in = Wf["w_in"].astype(bf16)
    pad16 = lambda v: jnp.pad(v.reshape(1, N_HEADS).astype(f32), ((0, 0), (0, DT_PAD - N_HEADS)))
    return {
        "norm_g": Wf["norm_g"],
        "w_uvz": w_in[:, :3 * D], "w_xbc": w_in[:, 3 * D:3 * D + CONV_DIM],
        "w_dt": jnp.pad(w_in[:, 3 * D + CONV_DIM:], ((0, 0), (0, DT_PAD - N_HEADS))),
        "ln_g": Wf["gm_ln_g"].reshape(1, D), "ln_b": Wf["gm_ln_b"].reshape(1, D),
        "wm": jnp.where(causal[None], Wf["gm_ws"], 0).astype(bf16), "bs": Wf["gm_bs"].reshape(GM_HEADS, CHUNK, 1),
        "conv_w": Wf["conv_w"], "conv_b": Wf["conv_b"].reshape(1, CONV_DIM),
        "dtb": pad16(Wf["dt_bias"]), "alog": pad16(Wf["a_log"]), "dsk": pad16(Wf["d_skip"]),
        "gn": Wf["ssm_norm_g"].reshape(1, D),
        "wo4": Wf["wo4"].astype(bf16),
        "pool_w": Wf["pool_w"].astype(bf16), "pool_b": Wf["pool_b"].reshape(1, D), "pool_scale": Wf["pool_scale"].reshape(1, D),
    }


def small_grads(G):
    return {
        "norm_g": G["norm_g"],
        "gm_ln_g": G["ln_g"].reshape(D), "gm_ln_b": G["ln_b"].reshape(D),
        "gm_ws": G["wm"], "gm_bs": G["bs"].reshape(GM_HEADS, CHUNK),
        "conv_w": G["conv_w"], "conv_b": G["conv_b"].reshape(CONV_DIM),
        "dt_bias": G["dtb"][0, :N_HEADS], "a_log": G["alog"][0, :N_HEADS], "d_skip": G["dsk"][0, :N_HEADS],
        "ssm_norm_g": G["gn"].reshape(D),
        "pool_b": G["pool_b"].reshape(4, POOL_GD), "pool_scale": G["pool_scale"].reshape(D),
    }


MESH_ID = pl.DeviceIdType.MESH
ANY = pl.BlockSpec(memory_space=pl.ANY)


DMA_CHUNK_BYTES = 2 << 20
DMA_MAX_CHUNKS = 32


def _pieces(view, axis, align):
    shape = view.shape
    nbytes = math.prod(shape) * jnp.dtype(view.dtype).itemsize
    n = max(1, min(DMA_MAX_CHUNKS, -(-nbytes // DMA_CHUNK_BYTES)))
    rows = shape[axis]
    size = -(-rows // n)
    size = -(-size // align) * align
    out = []
    for s in range(0, rows, size):
        idx = [slice(None)] * len(shape)
        idx[axis] = pl.ds(s, min(size, rows - s))
        out.append(tuple(idx))
    return out


def comm_call(name, operands, out_shapes, plan):
    n_in = len(operands)
    n_out = len(out_shapes)
    n_remote, n_local = plan((0, 0, 0), [None] * n_in, [None] * n_out, True)

    def body(*refs):
        in_refs, out_refs = refs[:n_in], refs[n_in:n_in + n_out]
        send_sems, recv_sems, local_sems = refs[n_in + n_out:]
        me = (lax.axis_index("x"), lax.axis_index("y"), lax.axis_index("c"))
        remote, local = plan(me, in_refs, out_refs, False)
        align = lambda v: 16 if v.dtype == bf16 else 8
        for j, (s, d, axis) in enumerate(local):
            for ix in _pieces(s, axis, align(s)):
                pltpu.make_async_copy(s.at[ix], d.at[ix], local_sems.at[j]).start()
        peers = [tuple((1 - m) if f else m for m, f in zip(me, flip)) for flip, *_ in remote]
        for k, (flip, src, dst, _, axis) in enumerate(remote):
            for ix in _pieces(src, axis, align(src)):
                pltpu.make_async_remote_copy(src_ref=src.at[ix], dst_ref=dst.at[ix], send_sem=send_sems.at[k], recv_sem=recv_sems.at[k],
                                             device_id=peers[k], device_id_type=MESH_ID).start()
        for k, (flip, src, dst, landing, axis) in enumerate(remote):
            pltpu.make_async_remote_copy(src_ref=landing, dst_ref=landing, send_sem=send_sems.at[k], recv_sem=recv_sems.at[k],
                                         device_id=peers[k], device_id_type=MESH_ID).wait_recv()
        for k, (flip, src, dst, landing, axis) in enumerate(remote):
            pltpu.make_async_remote_copy(src_ref=src, dst_ref=dst, send_sem=send_sems.at[k], recv_sem=recv_sems.at[k],
                                         device_id=peers[k], device_id_type=MESH_ID).wait_send()
        for j, (s, d, axis) in enumerate(local):
            pltpu.make_async_copy(s, d, local_sems.at[j]).wait()

    return pl.pallas_call(
        body, name=name, out_shape=list(out_shapes), in_specs=[ANY] * n_in, out_specs=[ANY] * n_out,
        scratch_shapes=[pltpu.SemaphoreType.DMA((n_remote,)), pltpu.SemaphoreType.DMA((n_remote,)), pltpu.SemaphoreType.DMA((max(n_local, 1),))],
    )(*operands)


CHIP_FLIPS = ((1, 0, 0), (0, 1, 0), (1, 1, 0))
PAIR_FLIP = (0, 0, 1)


def gather_two_level(name, halved, whole):
    nh, nw = len(halved), len(whole)
    nf = len(CHIP_FLIPS)

    def body(*refs):
        srcs, outs = refs[:nh + nw], refs[nh + nw:2 * (nh + nw)]
        send_sems, recv_sems, fwd_send, fwd_recv = refs[2 * (nh + nw):]
        me = (lax.axis_index("x"), lax.axis_index("y"), lax.axis_index("c"))
        k, c = 2 * me[0] + me[1], me[2]
        sibling = (me[0], me[1], 1 - c)
        peers = [tuple((1 - m) if fl else m for m, fl in zip(me, flip)) for flip in CHIP_FLIPS]

        def half(ref, which):
            rh = ref.shape[0] // 2
            return ref.at[pl.ds(pl.multiple_of(which * rh, 16), rh), :]

        def ici(a, f):
            src = half(srcs[a], c) if a < nh else srcs[a]
            dst = half(outs[a].at[k], c) if a < nh else outs[a].at[k]
            return pltpu.make_async_remote_copy(src_ref=src, dst_ref=dst, send_sem=send_sems.at[a * nf + f], recv_sem=recv_sems.at[a * nf + f],
                                                device_id=peers[f], device_id_type=MESH_ID)

        def landed(a, f):
            slot = outs[a].at[_chip_of(me, CHIP_FLIPS[f])]
            return half(slot, c) if a < nh else slot

        def forward(a, f, which):
            v = half(outs[a].at[_chip_of(me, CHIP_FLIPS[f])], which)
            return pltpu.make_async_remote_copy(src_ref=v, dst_ref=v, send_sem=fwd_send.at[a * nf + f], recv_sem=fwd_recv.at[a * nf + f],
                                                device_id=sibling, device_id_type=MESH_ID)

        copies = [ici(a, f) for a in range(nh + nw) for f in range(nf)]
        for cp in copies:
            cp.start()
        fwds = []
        for a in range(nh):
            for f in range(nf):
                lv = landed(a, f)
                pltpu.make_async_remote_copy(src_ref=lv, dst_ref=lv, send_sem=send_sems.at[a * nf + f], recv_sem=recv_sems.at[a * nf + f],
                                             device_id=peers[f], device_id_type=MESH_ID).wait_recv()
                fw = forward(a, f, c)
                fw.start()
                fwds.append(fw)
        for a in range(nh, nh + nw):
            for f in range(nf):
                lv = landed(a, f)
                pltpu.make_async_remote_copy(src_ref=lv, dst_ref=lv, send_sem=send_sems.at[a * nf + f], recv_sem=recv_sems.at[a * nf + f],
                                             device_id=peers[f], device_id_type=MESH_ID).wait_recv()
        for a in range(nh):
            for f in range(nf):
                forward(a, f, 1 - c).wait_recv()
        for fw in fwds:
            fw.wait_send()
        for cp in copies:
            cp.wait_send()

    arrs = list(halved) + list(whole)
    n_ici = (nh + nw) * nf
    return pl.pallas_call(
        body, name=name, out_shape=[SDS((N_CHIPS,) + a.shape, a.dtype) for a in arrs], in_specs=[ANY] * len(arrs), out_specs=[ANY] * len(arrs),
        scratch_shapes=[pltpu.SemaphoreType.DMA((n_ici,)), pltpu.SemaphoreType.DMA((n_ici,)),
                        pltpu.SemaphoreType.DMA((nh * nf,)), pltpu.SemaphoreType.DMA((nh * nf,))],
    )(*arrs)


def pair_split_exchange(name, p, rh):
    def plan(me, ins, outs, count):
        if count:
            return 1, 0
        theirs = ins[0].at[:, pl.ds(pl.multiple_of((1 - me[2]) * rh, 8), rh), :]
        return [(PAIR_FLIP, theirs, outs[0], outs[0], 1)], []
    return comm_call(name, [p], [SDS((4, rh, p.shape[2]), p.dtype)], plan)[0]


def scatter_over_chips(name, cs):
    def plan(me, ins, outs, count):
        if count:
            return len(CHIP_FLIPS), 0
        k = 2 * me[0] + me[1]
        remote = []
        for flip in CHIP_FLIPS:
            kp = 2 * ((1 - me[0]) if flip[0] else me[0]) + ((1 - me[1]) if flip[1] else me[1])
            remote.append((flip, ins[0].at[kp], outs[0].at[k], outs[0].at[kp], 0))
        return remote, []
    return comm_call(name, [cs], [SDS(cs.shape, cs.dtype)], plan)[0]


def pair_swap(name, half):
    def plan(me, ins, outs, count):
        if count:
            return 1, 0
        return [(PAIR_FLIP, ins[0], outs[0], outs[0], 0)], []
    return comm_call(name, [half], [SDS(half.shape, half.dtype)], plan)[0]


def _row_tile(rows, cap=512):
    if rows <= cap:
        return rows
    t = cap - cap % 8
    while rows % t:
        t -= 8
    return t


def pair_sum(name, packs, got, c_arr, tile):
    rh = got.shape[1]
    nb = rh // tile

    def kern(c_ref, a_ref, b_ref, o16_ref):
        o16_ref[...] = (a_ref[...] + b_ref[...]).astype(bf16)
    blk = (None, tile, D)
    grid_spec = pltpu.PrefetchScalarGridSpec(
        num_scalar_prefetch=1, grid=(4, nb),
        in_specs=[pl.BlockSpec(blk, lambda s, i, c: (s, c[0] * nb + i, 0)), pl.BlockSpec(blk, lambda s, i, c: (s, i, 0))],
        out_specs=pl.BlockSpec(blk, lambda s, i, c: (s, i, 0)))
    return pl.pallas_call(kern, name=name, grid_spec=grid_spec, out_shape=SDS(got.shape, bf16),
                          compiler_params=pltpu.CompilerParams(dimension_semantics=("parallel", "parallel")))(c_arr, packs, got)


def chip_sum(name, own16, landed16, k_arr, tile):
    rh = own16.shape[1]
    nb = rh // tile

    def kern(k_ref, own_ref, l0, l1, l2, l3, o_ref):
        k = k_ref[0]
        s = None
        for j, lref in enumerate((l0, l1, l2, l3)):
            t = jnp.where(k == j, own_ref[...], lref[...]).astype(f32)
            s = t if s is None else s + t
        o_ref[...] = s
    blk = (None, tile, D)
    land = [pl.BlockSpec(blk, lambda i, k, j=j: (jnp.where(k[0] == j, (j + 1) % N_CHIPS, j), i, 0)) for j in range(N_CHIPS)]
    grid_spec = pltpu.PrefetchScalarGridSpec(
        num_scalar_prefetch=1, grid=(nb,),
        in_specs=[pl.BlockSpec(blk, lambda i, k: (k[0], i, 0))] + land,
        out_specs=pl.BlockSpec((tile, D), lambda i, k: (i, 0)))
    return pl.pallas_call(kern, name=name, grid_spec=grid_spec, out_shape=SDS((rh, D), f32),
                          compiler_params=pltpu.CompilerParams(dimension_semantics=("parallel",)))(k_arr, own16, landed16, landed16, landed16, landed16)


def adamw(name, w, g, m, v):
    R, C = w.shape
    tr = _row_tile(R, 256)

    def kern(w_ref, g_ref, m_ref, v_ref, d_ref, mo_ref, vo_ref):
        gg = g_ref[...]
        mn = ADAM_B1 * m_ref[...] + (1.0 - ADAM_B1) * gg
        vn = ADAM_B2 * v_ref[...] + (1.0 - ADAM_B2) * jnp.square(gg)
        m_hat = mn / (1.0 - ADAM_B1 ** ADAM_STEP)
        v_hat = vn / (1.0 - ADAM_B2 ** ADAM_STEP)
        d_ref[...] = -ADAM_LR * (m_hat / (jnp.sqrt(v_hat) + ADAM_EPS) + ADAM_WD * w_ref[...])
        mo_ref[...] = mn
        vo_ref[...] = vn
    spec = pl.BlockSpec((tr, C), lambda i: (i, 0))
    s = SDS((R, C), f32)
    return pl.pallas_call(kern, name=name, grid=(R // tr,), in_specs=[spec] * 4, out_specs=[spec] * 3, out_shape=[s, s, s],
                          compiler_params=pltpu.CompilerParams(dimension_semantics=("parallel",)))(w, g, m, v)


WEIGHT_NAMES = ("norm_g", "w_in", "gm_ln_g", "gm_ln_b", "gm_ws", "gm_bs", "conv_w", "conv_b", "dt_bias", "a_log", "d_skip",
                "ssm_norm_g", "w_out", "pool_w", "pool_b", "pool_scale", "ffn_w_gate", "ffn_w_up", "ffn_w_down")
SMALL = ("norm_g", "conv_w", "pool_b", "pool_scale")
REPL = ("gm_ln_g", "gm_ln_b", "gm_ws", "gm_bs", "conv_b", "dt_bias", "a_log", "d_skip", "ssm_norm_g")
SMALL_AXIS = {"norm_g": 2, "conv_w": 1, "pool_b": 1, "pool_scale": 0}
N_CHIPS = 4
IN_SH = IN_DIM // N_CHIPS
SMALL_ROWS = 8
REPL_ROWS = 72
E_OUT, E_GATE, E_UP, E_DOWN = 0, 512, 512 + 2 * FF_SH, 512 + 4 * FF_SH
E_POOL = E_DOWN + 2 * FF_SH
E_ROWS, E_TILE = E_POOL + 64, 400
L_SMALL, L_REPL, L_IN = 0, SMALL_ROWS, SMALL_ROWS + REPL_ROWS
L_END = L_IN + IN_SH
L_ROWS, L_TILE = 1408, 352


def _flat_rows(pieces, rows):
    v = jnp.concatenate([p.reshape(-1) for p in pieces])
    return jnp.pad(v, (0, rows * D - v.shape[0])).reshape(rows, D)


def _shard_small(name, full, k):
    ax = SMALL_AXIS[name]
    n = full.shape[ax] // N_CHIPS
    return lax.slice_in_dim(full, k * n, (k + 1) * n, axis=ax)


def _drop1(name, a):
    return a if name == "norm_g" else a[0]


HBM_SPEC = pl.BlockSpec(memory_space=pltpu.HBM)
SEM_SPEC = pl.BlockSpec(memory_space=pltpu.SEMAPHORE)
SPLIT_EFFECT = pltpu.SideEffectType.DATAFLOW_SIDE_EFFECTING


def _chip_of(me, flip):
    return 2 * ((1 - me[0]) if flip[0] else me[0]) + ((1 - me[1]) if flip[1] else me[1])


def gather_start(name, arrs, after, slotted=False):
    n = len(arrs)
    ncp = n * len(CHIP_FLIPS)

    def body(*refs):
        srcs, lands = refs[:n], refs[n:2 * n]
        send_sems, recv_sems, token = refs[2 * n + 1], refs[2 * n + 2], refs[-1]
        me = (lax.axis_index("x"), lax.axis_index("y"), lax.axis_index("c"))
        k = 2 * me[0] + me[1]
        for a in range(n):
            for f, flip in enumerate(CHIP_FLIPS):
                peer = tuple((1 - m) if fl else m for m, fl in zip(me, flip))
                src = srcs[a].at[_chip_of(me, flip)] if slotted else srcs[a]
                for ix in _pieces(src, 0, 16):
                    pltpu.make_async_remote_copy(src_ref=src.at[ix], dst_ref=lands[a].at[k].at[ix],
                                                 send_sem=send_sems.at[a * len(CHIP_FLIPS) + f], recv_sem=recv_sems.at[a * len(CHIP_FLIPS) + f],
                                                 device_id=peer, device_id_type=MESH_ID).start()
        token[...] = jnp.zeros_like(token)

    land_shapes = [a.shape if slotted else (N_CHIPS,) + a.shape for a in arrs]
    operands = [pltpu.with_memory_space_constraint(a, pltpu.HBM) for a in arrs]
    operands += [pltpu.with_memory_space_constraint(lax.empty(s, a.dtype), pltpu.HBM) for s, a in zip(land_shapes, arrs)]
    out = pl.pallas_call(
        body, name=name,
        out_shape=(pltpu.SemaphoreType.DMA((ncp,)), pltpu.SemaphoreType.DMA((ncp,)), *[pltpu.HBM(a.shape, a.dtype) for a in arrs],
                   *[pltpu.HBM(s, a.dtype) for s, a in zip(land_shapes, arrs)], SDS((8, 128), f32)),
        in_specs=[HBM_SPEC] * (2 * n) + [ANY], out_specs=(SEM_SPEC, SEM_SPEC, *[HBM_SPEC] * (2 * n), pl.BlockSpec(memory_space=pltpu.VMEM)),
        input_output_aliases={i: 2 + i for i in range(2 * n)},
        compiler_params=pltpu.CompilerParams(has_side_effects=SPLIT_EFFECT),
    )(*operands, after)
    return out[0], out[1], out[2:2 + n], out[2 + n:2 + 2 * n], out[-1]


def gather_wait(name, send_sems, recv_sems, thru, lands, after, slotted=False):
    n = len(thru)

    def body(*refs):
        srcs, lands_r = refs[:n], refs[n:2 * n]
        s_sems, r_sems = refs[2 * n], refs[2 * n + 1]
        me = (lax.axis_index("x"), lax.axis_index("y"), lax.axis_index("c"))
        k = 2 * me[0] + me[1]
        for a in range(n):
            for f, flip in enumerate(CHIP_FLIPS):
                peer = tuple((1 - m) if fl else m for m, fl in zip(me, flip))
                idx = a * len(CHIP_FLIPS) + f
                src = srcs[a].at[_chip_of(me, flip)] if slotted else srcs[a]
                pltpu.make_async_remote_copy(src_ref=src, dst_ref=lands_r[a].at[k], send_sem=s_sems.at[idx], recv_sem=r_sems.at[idx],
                                             device_id=peer, device_id_type=MESH_ID).wait_send()
                pltpu.make_async_remote_copy(src_ref=src, dst_ref=lands_r[a].at[_chip_of(me, flip)], send_sem=s_sems.at[idx],
                                             recv_sem=r_sems.at[idx], device_id=peer, device_id_type=MESH_ID).wait_recv()

    out = pl.pallas_call(
        body, name=name, out_shape=tuple(pltpu.HBM(t.shape, t.dtype) for t in (*thru, *lands)),
        in_specs=[HBM_SPEC] * (2 * n) + [SEM_SPEC, SEM_SPEC, ANY], out_specs=tuple([HBM_SPEC] * (2 * n)),
        input_output_aliases={i: i for i in range(2 * n)},
        compiler_params=pltpu.CompilerParams(has_side_effects=SPLIT_EFFECT),
    )(*thru, *lands, send_sems, recv_sems, after)
    return out[:n], out[n:]


def gather_weights(w_sh):
    big = [w_sh["w_in"][0], w_sh["w_out"][0], w_sh["pool_w"][0].reshape(4 * 64, POOL_GD)]
    small_pack = _flat_rows([w_sh[n] for n in SMALL], SMALL_ROWS)
    own = [b.astype(bf16) for b in big] + [small_pack]
    my_k = 2 * lax.axis_index("x") + lax.axis_index("y")
    s_in, s_out, s_pool, s_small = [lax.dynamic_update_slice(s, o[None], (my_k, 0, 0))
                                    for s, o in zip(gather_two_level("gather_weights", own[:3], own[3:]), own)]
    Wf = {n: w_sh[n][0] for n in REPL}
    Wf["w_in"] = s_in.transpose(1, 0, 2).reshape(D, IN_DIM)
    Wf["pool_w"] = s_pool.reshape(N_CHIPS, 4, 64, POOL_GD).transpose(1, 0, 2, 3).reshape(4, POOL_GD, POOL_GD)
    Wf["wo4"] = s_out
    small_shapes = [_drop1(n, w_sh[n]).shape for n in SMALL]
    parts = [_split_rows(s_small[k], small_shapes) for k in range(N_CHIPS)]
    for j, n in enumerate(SMALL):
        Wf[n] = jnp.concatenate([parts[k][j] for k in range(N_CHIPS)], axis=SMALL_AXIS[n])
    return Wf


def pack_early(G):
    slots = [jnp.concatenate([G["wo4"][k], G["wgT4"][0][k], G["wgT4"][1][k], G["wuT4"][0][k], G["wuT4"][1][k], G["wd4"][0][k], G["wd4"][1][k],
                              G["pool_w"][:, k * 64:(k + 1) * 64, :].reshape(64, D)], axis=0) for k in range(N_CHIPS)]
    return jnp.stack(slots)


def pack_late(G):
    sg = small_grads(G)
    repl = _flat_rows([sg[n] for n in REPL], REPL_ROWS)
    w_in_t = jnp.concatenate(G["w_inT"], axis=0)
    slots = [jnp.concatenate([_flat_rows([_shard_small(n, sg[n], k) for n in SMALL], SMALL_ROWS), repl,
                              jnp.pad(w_in_t[k * IN_SH:(k + 1) * IN_SH], ((0, L_ROWS - L_END), (0, 0)))], axis=0)
             for k in range(N_CHIPS)]
    return jnp.stack(slots)


def unpack_grads(early, late, w_sh):
    g = {"w_out": early[E_OUT:E_GATE], "ffn_w_down": early[E_DOWN:E_POOL], "pool_w": early[E_POOL:E_ROWS],
         "ffn_w_gate": jnp.stack([early[E_GATE + l * FF_SH:E_GATE + (l + 1) * FF_SH].T for l in range(2)]),
         "ffn_w_up": jnp.stack([early[E_UP + l * FF_SH:E_UP + (l + 1) * FF_SH].T for l in range(2)]),
         "w_in": late[L_IN:L_END].T}
    small = _split_rows(late[L_SMALL:L_REPL], [_drop1(n, w_sh[n]).shape for n in SMALL])
    repl = _split_rows(late[L_REPL:L_IN], [w_sh[n][0].shape for n in REPL])
    g.update(zip(SMALL, small))
    g.update(zip(REPL, repl))
    return {n: g[n].reshape(w_sh[n].shape) for n in WEIGHT_NAMES}


def _split_rows(flat2d, shapes):
    v = flat2d.reshape(-1)
    out, off = [], 0
    for s in shapes:
        n = math.prod(s)
        out.append(v[off:off + n].reshape(s))
        off += n
    return out


def kernel(x, norm_g, w_in, gm_ln_g, gm_ln_b, gm_ws, gm_bs, conv_w, conv_b, dt_bias, a_log, d_skip, ssm_norm_g, w_out, pool_w, pool_b, pool_scale, ffn_w_gate, ffn_w_up, ffn_w_down, loss_target, m_norm_g, m_w_in, m_gm_ln_g, m_gm_ln_b, m_gm_ws, m_gm_bs, m_conv_w, m_conv_b, m_dt_bias, m_a_log, m_d_skip, m_ssm_norm_g, m_w_out, m_pool_w, m_pool_b, m_pool_scale, m_ffn_w_gate, m_ffn_w_up, m_ffn_w_down, v_norm_g, v_w_in, v_gm_ln_g, v_gm_ln_b, v_gm_ws, v_gm_bs, v_conv_w, v_conv_b, v_dt_bias, v_a_log, v_d_skip, v_ssm_norm_g, v_w_out, v_pool_w, v_pool_b, v_pool_scale, v_ffn_w_gate, v_ffn_w_up, v_ffn_w_down):
    T = x.shape[1]
    w_sh = dict(zip(WEIGHT_NAMES, (norm_g, w_in, gm_ln_g, gm_ln_b, gm_ws, gm_bs, conv_w, conv_b, dt_bias, a_log, d_skip, ssm_norm_g, w_out,
                                   pool_w, pool_b, pool_scale, ffn_w_gate, ffn_w_up, ffn_w_down)))
    m_sh = dict(zip(WEIGHT_NAMES, (m_norm_g, m_w_in, m_gm_ln_g, m_gm_ln_b, m_gm_ws, m_gm_bs, m_conv_w, m_conv_b, m_dt_bias, m_a_log, m_d_skip,
                                   m_ssm_norm_g, m_w_out, m_pool_w, m_pool_b, m_pool_scale, m_ffn_w_gate, m_ffn_w_up, m_ffn_w_down)))
    v_sh = dict(zip(WEIGHT_NAMES, (v_norm_g, v_w_in, v_gm_ln_g, v_gm_ln_b, v_gm_ws, v_gm_bs, v_conv_w, v_conv_b, v_dt_bias, v_a_log, v_d_skip,
                                   v_ssm_norm_g, v_w_out, v_pool_w, v_pool_b, v_pool_scale, v_ffn_w_gate, v_ffn_w_up, v_ffn_w_down)))

    my_k = 2 * lax.axis_index("x") + lax.axis_index("y")
    ffn_own = [w_sh["ffn_w_gate"].reshape(2 * D, FF_SH).astype(bf16), w_sh["ffn_w_up"].reshape(2 * D, FF_SH).astype(bf16),
               w_sh["ffn_w_down"].reshape(2 * FF_SH, D).astype(bf16)]
    Wf = gather_weights(w_sh)
    send_sems, recv_sems, thru, lands, token = gather_start("gather_ffn_start", ffn_own, Wf["wo4"])
    Wf["norm_g"] = Wf["norm_g"] + token[0, 0]
    W = build_weights(Wf)

    def ffn_weights(after):
        _, landed = gather_wait("gather_ffn_wait", send_sems, recv_sems, thru, lands, after)
        return tuple(lax.dynamic_update_slice(l, o[None], (my_k, 0, 0)) for l, o in zip(landed, ffn_own))

    my_c = lax.axis_index("c")
    c_arr = my_c.astype(jnp.int32).reshape(1)
    k_arr = my_k.astype(jnp.int32).reshape(1)

    def pair_stage(tag, packs, tile):
        got = pair_split_exchange(f"grads{tag}_pair_split", packs, packs.shape[1] // 2)
        return pair_sum(f"grads{tag}_pair_sum", packs, got, c_arr, tile)

    def chip_stage(tag, pair16, landed, tile):
        half = chip_sum(f"grads{tag}_chip_sum", pair16, landed, k_arr, tile)
        other = pair_swap(f"grads{tag}_pair_swap", half)
        return jnp.concatenate([jnp.where(my_c == 0, half, other), jnp.where(my_c == 0, other, half)], axis=0)

    early = {}

    def early_grads(Ge):
        pair16 = pair_stage("E", pack_early(Ge), E_TILE)
        s_sems, r_sems, thru, lands, tok = gather_start("gradsE_scatter_start", [pair16], jnp.zeros((8, 128), f32), slotted=True)
        early.update(s_sems=s_sems, r_sems=r_sems, thru=thru, lands=lands)
        return tok

    loss_acc, grad_x, G = local_step(T, x[0], loss_target[0], W, ffn_weights, early_grads)
    pair_l = pair_stage("L", pack_late(G), L_TILE)
    total_l = chip_stage("L", pair_l, scatter_over_chips("gradsL_scatter", pair_l), L_TILE)
    (pair_e,), (landed_e,) = gather_wait("gradsE_scatter_wait", early["s_sems"], early["r_sems"], early["thru"], early["lands"], total_l,
                                         slotted=True)
    total_e = chip_stage("E", pair_e, landed_e, E_TILE)
    grads = unpack_grads(total_e, total_l, w_sh)

    delta, new_m, new_v = {}, {}, {}
    for n in WEIGHT_NAMES:
        shp = w_sh[n].shape
        two_d = (-1, shp[-1])
        d_, m_, v_ = adamw("adamw_" + n, w_sh[n].reshape(two_d), grads[n].reshape(two_d), m_sh[n].reshape(two_d), v_sh[n].reshape(two_d))
        delta[n], new_m[n], new_v[n] = d_.reshape(shp), m_.reshape(shp), v_.reshape(shp)

    loss = lax.psum(loss_acc[0, 0], ("x", "y", "c"))
    return (loss, grad_x[None], *[grads[n] for n in WEIGHT_NAMES], *[delta[n] for n in WEIGHT_NAMES],
            *[new_m[n] for n in WEIGHT_NAMES], *[new_v[n] for n in WEIGHT_NAMES])
```

```python
import math

import jax
import jax.numpy as jnp
from jax import lax
from jax.experimental import pallas as pl
from jax.experimental.pallas import tpu as pltpu

f32, bf16 = jnp.float32, jnp.bfloat16
SDS = jax.ShapeDtypeStruct

D = 1024
EPS = 1e-6
CHUNK = 128
GM_HEADS, GM_HD = 4, 256
SSM_GROUPS, SSM_HPG, SSM_P, SSM_N = 4, 4, 64, 128
N_HEADS = SSM_GROUPS * SSM_HPG
CONV_K = 4
CONV_DIM = 2048
POOL_WINDOWS = (2, 4, 8, 16)
POOL_GD = 256
POOL_HALO = 32
CONV_HALO = 8
D_FF = 2816
DT_PAD = 128
IN_DIM = 5136

ADAM_LR, ADAM_B1, ADAM_B2, ADAM_EPS, ADAM_WD, ADAM_STEP = 0.001, 0.9, 0.999, 1e-08, 0.01, 10

NT = (((1,), (1,)), ((), ()))
TN = (((0,), (0,)), ((), ()))
NN = (((1,), (0,)), ((), ()))
HI = lax.Precision.HIGHEST
MM_SUB = 256


def _silu(x):
    return x * jax.nn.sigmoid(x)


def _softplus(x):
    return jnp.maximum(x, 0.0) + jnp.log1p(jnp.exp(-jnp.abs(x)))


def _rms(x, g):
    return x * lax.rsqrt(jnp.mean(x * x, axis=-1, keepdims=True) + EPS) * g


def _rms_bwd(x, g, dy):
    r = lax.rsqrt(jnp.mean(x * x, axis=-1, keepdims=True) + EPS)
    xh = x * r
    dxh = dy * g
    dx = r * (dxh - xh * jnp.mean(dxh * xh, axis=-1, keepdims=True))
    return dx, jnp.sum(dy * xh, axis=0, keepdims=True)


def _bdot(a, b, dims=NN):
    return lax.dot_general(a.astype(bf16), b.astype(bf16), dims, preferred_element_type=f32)


def matmul(name, pairs, mode, out_dtype, tm, tn, tk=None):
    a0, b0 = pairs[0]
    if mode == "tn":
        M, N, K = a0.shape[1], b0.shape[1], a0.shape[0]
    else:
        M, K = a0.shape
        N = b0.shape[1] if mode == "nn" else b0.shape[0]
    tm, tn = min(tm, M), min(tn, N)
    assert M % tm == 0 and N % tn == 0, (name, M, N, tm, tn)
    if tk is None:
        nk = 1
    else:
        assert len(pairs) == 1 and K % tk == 0
        nk = K // tk
    dims = {"nn": NN, "nt": NT, "tn": TN}[mode]
    in_specs, args = [], []
    for a, b in pairs:
        kk = (a.shape[0] if mode == "tn" else a.shape[1]) if tk is None else tk
        if mode == "tn":
            in_specs.append(pl.BlockSpec((kk, tm), lambda j, i, k: (k, i)))
            in_specs.append(pl.BlockSpec((kk, tn), lambda j, i, k: (k, j)))
        elif mode == "nn":
            in_specs.append(pl.BlockSpec((tm, kk), lambda j, i, k: (i, k)))
            in_specs.append(pl.BlockSpec((kk, tn), lambda j, i, k: (k, j)))
        else:
            in_specs.append(pl.BlockSpec((tm, kk), lambda j, i, k: (i, k)))
            in_specs.append(pl.BlockSpec((tn, kk), lambda j, i, k: (j, k)))
        args += [a, b]
    npairs = len(pairs)

    def kern(*refs):
        o = refs[2 * npairs]
        part = None
        for p in range(npairs):
            d = _bdot(refs[2 * p][...], refs[2 * p + 1][...], dims)
            part = d if part is None else part + d
        if nk == 1:
            o[...] = part.astype(out_dtype)
        else:
            acc = refs[2 * npairs + 1]
            k = pl.program_id(2)

            @pl.when(k == 0)
            def _():
                acc[...] = part

            @pl.when(k > 0)
            def _():
                acc[...] += part

            @pl.when(k == nk - 1)
            def _():
                o[...] = acc[...].astype(out_dtype)

    return pl.pallas_call(
        kern, name=name, grid=(N // tn, M // tm, nk),
        in_specs=in_specs, out_specs=pl.BlockSpec((tm, tn), lambda j, i, k: (i, j)),
        out_shape=SDS((M, N), out_dtype),
        scratch_shapes=[pltpu.VMEM((tm, tn), f32)] if nk > 1 else [],
        compiler_params=pltpu.CompilerParams(dimension_semantics=("parallel", "parallel", "arbitrary")),
    )(*args)


def mm(name, grid, pairs, dims, o_spec, out_shape):
    nk = grid[2]
    npairs = len(pairs)
    in_specs, args = [], []
    for a, a_spec, b, b_spec in pairs:
        in_specs += [a_spec, b_spec]
        args += [a, b]
    blk = tuple(d for d in o_spec.block_shape if d is not None)

    def kern(*refs):
        o = refs[2 * npairs]
        part = None
        for p in range(npairs):
            d = _bdot(refs[2 * p][...], refs[2 * p + 1][...], dims)
            part = d if part is None else part + d
        if nk == 1:
            o[...] = part.astype(o.dtype)
        else:
            acc = refs[2 * npairs + 1]
            k = pl.program_id(2)

            @pl.when(k == 0)
            def _():
                acc[...] = part

            @pl.when(k > 0)
            def _():
                acc[...] += part

            @pl.when(k == nk - 1)
            def _():
                o[...] = acc[...].astype(o.dtype)

    return pl.pallas_call(
        kern, name=name, grid=grid, in_specs=in_specs, out_specs=o_spec, out_shape=out_shape,
        scratch_shapes=[pltpu.VMEM(blk, f32)] if nk > 1 else [],
        compiler_params=pltpu.CompilerParams(dimension_semantics=("parallel", "parallel", "arbitrary")),
    )(*args)


def mm_fused(name, n_row_blocks, pairs, dims, extra_ins, outs, accs, epilogue):
    npairs, nx, no, na = len(pairs), len(extra_ins), len(outs), len(accs)
    in_specs, args = [], []
    for a, a_spec, b, b_spec in pairs:
        in_specs += [a_spec, b_spec]
        args += [a, b]
    for arr, spec in extra_ins:
        in_specs.append(spec)
        args.append(arr)

    rows_blk = outs[0][1].block_shape[0]
    sub = min(rows_blk, MM_SUB)

    def kern(*refs):
        x_refs = refs[2 * npairs:2 * npairs + nx]
        o_refs = refs[2 * npairs + nx:2 * npairs + nx + no]
        a_refs = refs[2 * npairs + nx + no:]
        if na:
            @pl.when(pl.program_id(0) == 0)
            def _():
                for a in a_refs:
                    a[...] = jnp.zeros(a.shape, f32)
        for r0 in range(0, rows_blk, sub):
            rows = pl.ds(r0, sub)
            part = None
            for p in range(npairs):
                d = _bdot(refs[2 * p][rows, :], refs[2 * p + 1][...], dims)
                part = d if part is None else part + d
            epilogue(part, [x.at[rows, :] if x.shape[0] == rows_blk else x for x in x_refs], [o.at[rows, :] for o in o_refs], a_refs)

    return pl.pallas_call(
        kern, name=name, grid=(n_row_blocks,), in_specs=in_specs,
        out_specs=[spec for _, spec in outs] + [pl.BlockSpec(tuple(s), lambda i, nd=len(s): (0,) * nd) for s in accs],
        out_shape=[s for s, _ in outs] + [SDS(tuple(s), f32) for s in accs],
        compiler_params=pltpu.CompilerParams(dimension_semantics=("arbitrary",)),
    )(*args)


FF_SH = D_FF // 4


def ffn_up(name, T, tm, n_bf, wg4, wu4, l):
    sub = min(tm, MM_SUB)

    def kern(n_ref, wg_ref, wu_ref, g_ref, u_ref, a_ref):
        for r0 in range(0, tm, sub):
            rows = pl.ds(r0, sub)
            n = n_ref[rows, :]
            g = jnp.dot(n, wg_ref[...], preferred_element_type=f32)
            u = jnp.dot(n, wu_ref[...], preferred_element_type=f32)
            g_ref[rows, :] = g.astype(bf16)
            u_ref[rows, :] = u.astype(bf16)
            a_ref[rows, :] = (_silu(g) * u).astype(bf16)
    w_spec = pl.BlockSpec((None, D, FF_SH), lambda k, i: (k, l, 0))
    o_spec = pl.BlockSpec((None, tm, FF_SH), lambda k, i: (k, i, 0))
    s = SDS((4, T, FF_SH), bf16)
    return pl.pallas_call(kern, name=name, grid=(4, T // tm), in_specs=[pl.BlockSpec((tm, D), lambda k, i: (i, 0)), w_spec, w_spec],
                          out_specs=[o_spec] * 3, out_shape=[s, s, s],
                          compiler_params=pltpu.CompilerParams(dimension_semantics=("parallel", "parallel")))(n_bf, wg4, wu4)


def ffn_dgu(name, T, tm, d_f, wd4, gate4, up4, l):
    rc = 16

    sub = min(tm, MM_SUB)

    def kern(df_ref, wd_ref, g_ref, u_ref, dg_ref, du_ref, dact_ref):
        for s0 in range(0, tm, sub):
            dact_ref[pl.ds(s0, sub), :] = _bdot(df_ref[pl.ds(s0, sub), :], wd_ref[...], NT)
            for r0 in range(s0, s0 + sub, rc):
                rows = pl.ds(r0, rc)
                _, vjp = jax.vjp(lambda a, b: _silu(a) * b, g_ref[rows, :].astype(f32), u_ref[rows, :].astype(f32))
                dg, du = vjp(dact_ref[rows, :])
                dg_ref[rows, :] = dg.astype(bf16)
                du_ref[rows, :] = du.astype(bf16)
    a_spec = pl.BlockSpec((None, tm, FF_SH), lambda k, i: (k, i, 0))
    s = SDS((4, T, FF_SH), bf16)
    return pl.pallas_call(kern, name=name, grid=(4, T // tm),
                          in_specs=[pl.BlockSpec((tm, D), lambda k, i: (i, 0)), pl.BlockSpec((None, FF_SH, D), lambda k, i: (k, l, 0)), a_spec, a_spec],
                          out_specs=[a_spec] * 2, out_shape=[s, s], scratch_shapes=[pltpu.VMEM((tm, FF_SH), f32)],
                          compiler_params=pltpu.CompilerParams(dimension_semantics=("parallel", "parallel")))(d_f, wd4, gate4, up4)


def rowcall(name, body, T, tm, ins, outs, accs=(), scratch=(), reverse=False, sub=None):
    n = T // tm
    assert T % tm == 0

    def blk(i):
        return (n - 1 - i) if reverse else i

    in_specs, args = [], []
    for spec in ins:
        kind, arr = spec[0], spec[1]
        if kind == "row":
            _, _, w, cb = spec
            in_specs.append(pl.BlockSpec((tm, w), lambda i, cb=cb: (blk(i), cb)))
        elif kind == "prev":
            _, _, w, cb, h = spec
            r = tm // h
            in_specs.append(pl.BlockSpec((h, w), lambda i, cb=cb, r=r: (jnp.maximum(blk(i) * r - 1, 0), cb)))
        elif kind == "next":
            _, _, w, cb, h = spec
            r = tm // h
            in_specs.append(pl.BlockSpec((h, w), lambda i, cb=cb, r=r, h=h: (jnp.minimum((blk(i) + 1) * r, T // h - 1), cb)))
        else:
            nd = arr.ndim
            in_specs.append(pl.BlockSpec(arr.shape, lambda i, nd=nd: (0,) * nd))
        args.append(arr)
    out_shape = [SDS((T, w), dt) for w, dt in outs] + [SDS(tuple(s), f32) for s in accs]
    out_specs = [pl.BlockSpec((tm, w), lambda i: (blk(i), 0)) for w, _ in outs]
    out_specs += [pl.BlockSpec(tuple(s), lambda i, nd=len(s): (0,) * nd) for s in accs]
    ni, no, na = len(ins), len(outs), len(accs)

    def kern(*refs):
        i = pl.program_id(0)
        in_refs, out_refs = refs[:ni], refs[ni:ni + no]
        acc_refs, scr = refs[ni + no:ni + no + na], refs[ni + no + na:]
        if na:
            @pl.when(i == 0)
            def _():
                for a in acc_refs:
                    a[...] = jnp.zeros(a.shape, f32)
        if sub is None or sub >= tm:
            body(blk(i), n, in_refs, out_refs, acc_refs, scr)
        else:
            for r0 in range(0, tm, sub):
                rows = pl.ds(r0, sub)
                body(blk(i), n, [r.at[rows, :] if spec[0] == "row" else r for r, spec in zip(in_refs, ins)],
                     [o.at[rows, :] for o in out_refs], acc_refs, [s.at[rows, :] for s in scr])

    res = pl.pallas_call(
        kern, name=name, grid=(n,), in_specs=in_specs, out_specs=out_specs, out_shape=out_shape,
        scratch_shapes=list(scratch),
        compiler_params=pltpu.CompilerParams(dimension_semantics=("arbitrary",)),
    )(*args)
    return res


def rms_to_bf16(name, T, tm, x, g):
    def body(i, n, ins, outs, accs, scr):
        outs[0][...] = _rms(ins[0][...], ins[1][...]).astype(bf16)
    return rowcall(name, body, T, tm, [("row", x, D, 0), ("const", g)], [(D, bf16)], sub=64)[0]


def _layer_norm_parts(x):
    mu = jnp.mean(x, axis=-1, keepdims=True)
    xc = x - mu
    r = lax.rsqrt(jnp.mean(xc * xc, axis=-1, keepdims=True) + EPS)
    return xc * r, r


def gmlp_fwd(name, T, tm, uvz, ln_g, ln_b, wm, bs):
    def body(i, n, ins, outs, accs, scr):
        gu = jax.nn.gelu(ins[0][...])
        xh, _ = _layer_norm_parts(jax.nn.gelu(ins[1][...]))
        vln = (xh * ins[2][...] + ins[3][...]).astype(bf16)
        for c in range(ins[0].shape[0] // CHUNK):
            rows = slice(c * CHUNK, (c + 1) * CHUNK)
            for h in range(GM_HEADS):
                cols = slice(h * GM_HD, (h + 1) * GM_HD)
                mixed = jnp.dot(ins[4][h], vln[rows, cols], preferred_element_type=f32) + ins[5][h]
                outs[0][rows, cols] = (gu[rows, cols] * mixed).astype(bf16)
    return rowcall(name, body, T, tm, [("row", uvz, D, 0), ("row", uvz, D, 1), ("const", ln_g), ("const", ln_b), ("const", wm), ("const", bs)],
                   [(D, bf16)], sub=CHUNK)[0]


def gmlp_bwd(name, T, tm, uvz, d_ya, d_cb, ln_g, ln_b, wm, bs):
    def body(i, n, ins, outs, accs, scr):
        u, v, dya = ins[0][...], ins[1][...], ins[2][...]
        gu, gelu_u_vjp = jax.vjp(jax.nn.gelu, u)
        gv, gelu_v_vjp = jax.vjp(jax.nn.gelu, v)
        xh, r = _layer_norm_parts(gv)
        lng = ins[3][...]
        vln = (xh * lng + ins[4][...]).astype(bf16)
        rr = lax.broadcasted_iota(jnp.int32, (CHUNK, CHUNK), 0)
        cc = lax.broadcasted_iota(jnp.int32, (CHUNK, CHUNK), 1)
        causal = (rr >= cc).astype(f32)
        dvln_ref = scr[0]
        dgu_ref = scr[1]
        for c in range(ins[0].shape[0] // CHUNK):
            rows = slice(c * CHUNK, (c + 1) * CHUNK)
            for h in range(GM_HEADS):
                cols = slice(h * GM_HD, (h + 1) * GM_HD)
                w = ins[5][h]
                blk = vln[rows, cols]
                mixed = jnp.dot(w, blk, preferred_element_type=f32) + ins[6][h]
                dy = dya[rows, cols]
                dgu_ref[rows, cols] = dy * mixed
                dm = dy * gu[rows, cols]
                accs[3][h] += jnp.sum(dm, axis=1, keepdims=True)
                accs[2][h] += _bdot(dm, blk, NT) * causal
                dvln_ref[rows, cols] = _bdot(w, dm, TN)
        dvln = dvln_ref[...]
        accs[0][...] += jnp.sum(dvln * xh, axis=0, keepdims=True)
        accs[1][...] += jnp.sum(dvln, axis=0, keepdims=True)
        dxh = dvln * lng
        dgv = r * (dxh - jnp.mean(dxh, axis=-1, keepdims=True) - xh * jnp.mean(dxh * xh, axis=-1, keepdims=True))
        outs[0][...] = gelu_u_vjp(dgu_ref[...])[0].astype(bf16)
        outs[1][...] = gelu_v_vjp(dgv)[0].astype(bf16)
    return rowcall(name, body, T, tm,
                   [("row", uvz, D, 0), ("row", uvz, D, 1), ("row", d_ya, D, d_cb), ("const", ln_g), ("const", ln_b), ("const", wm), ("const", bs)],
                   [(D, bf16), (D, bf16)], accs=[(1, D), (1, D), (GM_HEADS, CHUNK, CHUNK), (GM_HEADS, CHUNK, 1)],
                   scratch=[pltpu.VMEM((tm, D), f32), pltpu.VMEM((tm, D), f32)], sub=CHUNK)


CONV_RC, CONV_LB = 64, 256


def _conv_fill(i, x_ref, halo_ref, scr, tm):
    scr[pl.ds(0, CONV_HALO), :] = jnp.where(i > 0, halo_ref[...], 0.0)
    scr[pl.ds(CONV_HALO, tm), :] = x_ref[...]


def _conv_taps(scr, r0, lanes):
    return [scr[pl.ds(r0 + CONV_HALO - (CONV_K - 1) + k, CONV_RC), lanes] for k in range(CONV_K)]


def conv_fwd(name, T, tm, xbc, conv_w, conv_b):
    def body(i, n, ins, outs, accs, scr):
        s = scr[0]
        _conv_fill(i, ins[0], ins[1], s, tm)
        for lb in range(CONV_DIM // CONV_LB):
            lanes = slice(lb * CONV_LB, (lb + 1) * CONV_LB)
            w, b = ins[2][:, lanes], ins[3][:, lanes]

            for r0 in range(0, tm, CONV_RC):
                taps = _conv_taps(s, r0, lanes)
                pre = b + sum(w[k:k + 1] * taps[k] for k in range(CONV_K))
                outs[0][pl.ds(r0, CONV_RC), lanes] = _silu(pre)
    return rowcall(name, body, T, tm, [("row", xbc, CONV_DIM, 0), ("prev", xbc, CONV_DIM, 0, CONV_HALO), ("const", conv_w), ("const", conv_b)],
                   [(CONV_DIM, f32)], scratch=[pltpu.VMEM((tm + CONV_HALO, CONV_DIM), f32)])[0]


def conv_bwd_pre(name, T, tm, xbc, d_xc, conv_w, conv_b):
    def body(i, n, ins, outs, accs, scr):
        s = scr[0]
        _conv_fill(i, ins[0], ins[1], s, tm)
        fold = lambda v: jnp.sum(v.reshape(CONV_RC // 8, 8, CONV_LB), axis=0)
        for lb in range(CONV_DIM // CONV_LB):
            lanes = slice(lb * CONV_LB, (lb + 1) * CONV_LB)
            w, b = ins[3][:, lanes], ins[4][:, lanes]

            sums = [jnp.zeros((8, CONV_LB), f32)] * (CONV_K + 1)
            for r0 in range(0, tm, CONV_RC):
                taps = _conv_taps(s, r0, lanes)
                pre = b + sum(w[k:k + 1] * taps[k] for k in range(CONV_K))
                _, vjp = jax.vjp(_silu, pre)
                dpre = vjp(ins[2][pl.ds(r0, CONV_RC), lanes])[0]
                outs[0][pl.ds(r0, CONV_RC), lanes] = dpre
                sums = [sums[k] + fold(dpre * taps[k]) for k in range(CONV_K)] + [sums[CONV_K] + fold(dpre)]
            for k in range(CONV_K):
                accs[0][pl.ds(k, 1), lanes] += jnp.sum(sums[k], axis=0, keepdims=True)
            accs[1][:, lanes] += jnp.sum(sums[CONV_K], axis=0, keepdims=True)
    return rowcall(name, body, T, tm,
                   [("row", xbc, CONV_DIM, 0), ("prev", xbc, CONV_DIM, 0, CONV_HALO), ("row", d_xc, CONV_DIM, 0), ("const", conv_w), ("const", conv_b)],
                   [(CONV_DIM, f32)], accs=[(CONV_K, CONV_DIM), (1, CONV_DIM)], scratch=[pltpu.VMEM((tm + CONV_HALO, CONV_DIM), f32)])


def conv_bwd_x(name, T, tm, d_pre, conv_w):
    def body(i, n, ins, outs, accs, scr):
        s = scr[0]
        s[pl.ds(0, tm), :] = ins[0][...]
        s[pl.ds(tm, CONV_HALO), :] = jnp.where(i < n - 1, ins[1][...], 0.0)
        for lb in range(CONV_DIM // CONV_LB):
            lanes = slice(lb * CONV_LB, (lb + 1) * CONV_LB)
            w = ins[2][:, lanes]

            for r0 in range(0, tm, CONV_RC):
                dx = sum(w[k:k + 1] * s[pl.ds(r0 + CONV_K - 1 - k, CONV_RC), lanes] for k in range(CONV_K))
                outs[0][pl.ds(r0, CONV_RC), lanes] = dx.astype(bf16)
    return rowcall(name, body, T, tm, [("row", d_pre, CONV_DIM, 0), ("next", d_pre, CONV_DIM, 0, CONV_HALO), ("const", conv_w)],
                   [(CONV_DIM, bf16)], scratch=[pltpu.VMEM((tm + CONV_HALO, CONV_DIM), f32)])[0]


def _ssd_prep(dtr, dtb, alog):
    rr = lax.broadcasted_iota(jnp.int32, (CHUNK, CHUNK), 0)
    cc = lax.broadcasted_iota(jnp.int32, (CHUNK, CHUNK), 1)
    dt = _softplus(dtr + dtb)
    dA = dt * -jnp.exp(alog)
    acum = jnp.dot((rr >= cc).astype(f32), dA, precision=HI, preferred_element_type=f32)
    return dt, acum, acum.T, jnp.sum(dA, axis=0, keepdims=True)


def _ssd_group(g, x, Bm, Cm, S, dt, acum, acumT, tot, dsk):
    rr = lax.broadcasted_iota(jnp.int32, (CHUNK, CHUNK), 0)
    cc = lax.broadcasted_iota(jnp.int32, (CHUNK, CHUNK), 1)
    tril = rr >= cc
    lane = lax.broadcasted_iota(jnp.int32, (1, DT_PAD), 1)
    sub = lax.broadcasted_iota(jnp.int32, (DT_PAD, 1), 0)
    glane = lax.broadcasted_iota(jnp.int32, (1, SSM_HPG * SSM_P), 1) // SSM_P
    hm = [(glane == r).astype(f32) for r in range(SSM_HPG)]
    pick = lambda v, r: jnp.sum(v * (lane == SSM_HPG * g + r).astype(f32), axis=1, keepdims=True)
    cols = [pick(acum, r) for r in range(SSM_HPG)]
    tots = [pick(tot, r) for r in range(SSM_HPG)]
    spread = lambda vals: sum(vals[r] * hm[r] for r in range(SSM_HPG))
    xdt = x * spread([pick(dt, r) for r in range(SSM_HPG)])
    cb = _bdot(Cm, Bm, NT)
    y = x * spread([pick(dsk, r) for r in range(SSM_HPG)])
    for r in range(SSM_HPG):
        row = jnp.sum(acumT * (sub == SSM_HPG * g + r).astype(f32), axis=0, keepdims=True)
        dec = jnp.exp(jnp.where(tril, cols[r] - row, -jnp.inf))
        y = y + _bdot(cb * dec, xdt * hm[r])
    y = y + _bdot(Cm, S) * spread([jnp.exp(c) for c in cols])
    dte = spread([jnp.exp(tots[r] - cols[r]) for r in range(SSM_HPG)])
    s_new = S * spread([jnp.exp(t) for t in tots]) + _bdot(Bm, xdt * dte, TN)
    return y, s_new


def _ssd_ins(xc, dtr):
    gw = SSM_HPG * SSM_P
    ins = [("row", xc, gw, g) for g in range(SSM_GROUPS)]
    ins += [("row", xc, SSM_N, D // SSM_N + g) for g in range(SSM_GROUPS)]
    ins += [("row", xc, SSM_N, D // SSM_N + SSM_GROUPS + g) for g in range(SSM_GROUPS)]
    ins += [("row", dtr, DT_PAD, 0)]
    return ins


SSD_CPS = 4


def ssd_fwd(name, T, xc, dtr, dtb, alog, dsk):
    gw = SSM_HPG * SSM_P
    cps = min(SSD_CPS, T // CHUNK)

    def body(i, n, ins, outs, accs, scr):
        S = scr[0]

        @pl.when(i == 0)
        def _():
            S[...] = jnp.zeros(S.shape, f32)
        S4 = tuple(S[:, g * gw:(g + 1) * gw] for g in range(4))
        for c in range(cps):
            rows = pl.ds(c * CHUNK, CHUNK)
            X4 = tuple(ins[g][rows, :] for g in range(4))
            B4 = tuple(ins[4 + g][rows, :] for g in range(4))
            C4 = tuple(ins[8 + g][rows, :] for g in range(4))
            prep = _ssd_prep(ins[12][rows, :], ins[13][...], ins[14][...])
            nxt = []
            for g in range(4):
                outs[1][rows, g * gw:(g + 1) * gw] = S4[g]
                y, s_new = _ssd_group(g, X4[g], B4[g], C4[g], S4[g], *prep, ins[15][...])
                outs[0][rows, g * gw:(g + 1) * gw] = y
                nxt.append(s_new)
            S4 = tuple(nxt)
        for g in range(4):
            S[:, g * gw:(g + 1) * gw] = S4[g]
    ins = _ssd_ins(xc, dtr) + [("const", dtb), ("const", alog), ("const", dsk)]
    return rowcall(name, body, T, cps * CHUNK, ins, [(D, f32), (D, f32)], scratch=[pltpu.VMEM((SSM_N, D), f32)])


def ssd_bwd(name, T, xc, dtr, sprev, d_y, dtb, alog, dsk):
    gw = SSM_HPG * SSM_P

    def body(i, n, ins, outs, accs, scr):
        dS = scr[0]

        @pl.when(i == n - 1)
        def _():
            dS[...] = jnp.zeros(dS.shape, f32)
        dS4 = tuple(dS[:, g * gw:(g + 1) * gw] for g in range(4))
        def chunk(X4, dtr_c, B4, C4, S4, dtb_c, alog_c, dsk_c):
            prep = _ssd_prep(dtr_c, dtb_c, alog_c)
            res = [_ssd_group(g, X4[g], B4[g], C4[g], S4[g], *prep, dsk_c) for g in range(4)]
            return tuple(r[0] for r in res), tuple(r[1] for r in res)
        X4 = tuple(ins[g][...] for g in range(4))
        B4 = tuple(ins[4 + g][...] for g in range(4))
        C4 = tuple(ins[8 + g][...] for g in range(4))
        S4 = tuple(ins[13 + g][...] for g in range(4))
        dY4 = tuple(ins[17 + g][...] for g in range(4))
        _, vjp = jax.vjp(chunk, X4, ins[12][...], B4, C4, S4, ins[21][...], ins[22][...], ins[23][...])
        dX4, ddtr, dB4, dC4, dS4, ddtb, dalog, ddsk = vjp((dY4, dS4))
        for g in range(4):
            outs[0][:, g * gw:(g + 1) * gw] = dX4[g]
            outs[0][:, D + g * SSM_N:D + (g + 1) * SSM_N] = dB4[g]
            outs[0][:, D + (SSM_GROUPS + g) * SSM_N:D + (SSM_GROUPS + g + 1) * SSM_N] = dC4[g]
            dS[:, g * gw:(g + 1) * gw] = dS4[g]
        outs[1][...] = ddtr.astype(bf16)
        accs[0][...] += ddtb
        accs[1][...] += dalog
        accs[2][...] += ddsk
    ins = _ssd_ins(xc, dtr) + [("row", sprev, gw, g) for g in range(4)] + [("row", d_y, gw, g) for g in range(4)]
    ins += [("const", dtb), ("const", alog), ("const", dsk)]
    return rowcall(name, body, T, CHUNK, ins, [(CONV_DIM, f32), (DT_PAD, bf16)], accs=[(1, DT_PAD)] * 3,
                   scratch=[pltpu.VMEM((SSM_N, D), f32)], reverse=True)


def _gate_group(y, z, g):
    return _rms(y * _silu(z), g)


def gate_fwd(name, T, tm, y, uvz, gn):
    def body(i, n, ins, outs, accs, scr):
        for g in range(SSM_GROUPS):
            cols = slice(g * 256, (g + 1) * 256)
            outs[0][:, cols] = _gate_group(ins[0][:, cols], ins[1][:, cols], ins[2][:, cols]).astype(bf16)
    return rowcall(name, body, T, tm, [("row", y, D, 0), ("row", uvz, D, 2), ("const", gn)], [(D, bf16)], sub=64)[0]


def gate_bwd(name, T, tm, y, uvz, d_yb, d_cb, gn):
    def body(i, n, ins, outs, accs, scr):
        for g in range(SSM_GROUPS):
            cols = slice(g * 256, (g + 1) * 256)
            _, vjp = jax.vjp(_gate_group, ins[0][:, cols], ins[1][:, cols], ins[3][:, cols])
            dy, dz, dg = vjp(ins[2][:, cols])
            outs[0][:, cols] = dy
            outs[1][:, cols] = dz.astype(bf16)
            accs[0][:, cols] += dg
    return rowcall(name, body, T, tm, [("row", y, D, 0), ("row", uvz, D, 2), ("row", d_yb, D, d_cb), ("const", gn)],
                   [(D, f32), (D, bf16)], accs=[(1, D)], sub=64)


def _window_sum(src, cols, levels, tm, lv, trailing):
    cur, cur_cols = src, cols
    for l in range(1, levels + 1):
        shift = 2 ** (l - 1)
        last = l == levels
        if trailing:
            start = POOL_HALO if last else 8 * l
            rows = tm if last else tm + POOL_HALO - start
            new = cur[pl.ds(start, rows), cur_cols] + cur[pl.ds(start - shift, rows), cur_cols]
        else:
            start = 0
            rows = tm if last else tm + POOL_HALO - 8 * l
            new = cur[pl.ds(0, rows), cur_cols] + cur[pl.ds(shift, rows), cur_cols]
        if last:
            return new
        nxt = lv[l % 2]
        nxt[pl.ds(start, rows), :] = new
        cur, cur_cols = nxt, slice(None)


def _pool_diff(i, tm, h_ref, halo_ref, g_ref, scr, lv):
    g = g_ref[...]
    yn = _rms(h_ref[...], g)
    scr[pl.ds(0, POOL_HALO), :] = jnp.where(i > 0, _rms(halo_ref[...], g), 0.0)
    scr[pl.ds(POOL_HALO, tm), :] = yn
    pos = (i * tm + lax.broadcasted_iota(jnp.int32, (tm, 1), 0) + 1).astype(f32)
    parts = []
    for gi, win in enumerate(POOL_WINDOWS):
        cols = slice(gi * POOL_GD, (gi + 1) * POOL_GD)
        s = _window_sum(scr, cols, gi + 1, tm, lv, True)
        parts.append(s * (1.0 / jnp.minimum(pos, float(win))) - yn[:, cols])
    return parts


def pool_fwd(name, T, tm, h2, g_pre, pw, pb, psc, g_post, g_next):
    def body(i, n, ins, outs, accs, scr):
        parts = _pool_diff(i, tm, ins[0], ins[1], ins[2], scr[0], scr[1:3])
        for gi in range(len(POOL_WINDOWS)):
            cols = slice(gi * POOL_GD, (gi + 1) * POOL_GD)
            o = _bdot(parts[gi], ins[3][gi]) + ins[4][:, cols]
            outs[0][:, cols] = o * ins[5][:, cols]
        h = ins[0][...] + _rms(outs[0][...], ins[6][...])
        outs[1][...] = h
        outs[2][...] = _rms(h, ins[7][...]).astype(bf16)
    return rowcall(name, body, T, tm, [("row", h2, D, 0), ("prev", h2, D, 0, POOL_HALO), ("const", g_pre), ("const", pw), ("const", pb), ("const", psc),
                                       ("const", g_post), ("const", g_next)],
                   [(D, f32), (D, f32), (D, bf16)], scratch=[pltpu.VMEM((tm + POOL_HALO, D), f32)] + [pltpu.VMEM((tm + POOL_HALO, POOL_GD), f32)] * 2)


def pool_bwd(name, T, tm, h2, d_pm, d_res, g_pre, pw, pb, psc, f_prev, g_prev):
    def body(i, n, ins, outs, accs, scr):
        parts = _pool_diff(i, tm, ins[0], ins[1], ins[5], scr[0], scr[3:5])
        dpm = ins[2][...]
        psc_v = ins[8][...]
        dps = dpm * psc_v
        dps_halo = jnp.where(i < n - 1, ins[3][...] * psc_v, 0.0)
        accs[1][...] += jnp.sum(dps, axis=0, keepdims=True)
        pos = (i * tm + lax.broadcasted_iota(jnp.int32, (tm, 1), 0) + 1).astype(f32)
        pos_h = ((i + 1) * tm + lax.broadcasted_iota(jnp.int32, (POOL_HALO, 1), 0) + 1).astype(f32)
        r_scr = scr[1]
        dyn_scr = scr[2]
        for gi, win in enumerate(POOL_WINDOWS):
            cols = slice(gi * POOL_GD, (gi + 1) * POOL_GD)
            w = ins[6][gi]
            o = _bdot(parts[gi], w) + ins[7][:, cols]
            accs[2][:, cols] += jnp.sum(dpm[:, cols] * o, axis=0, keepdims=True)
            accs[0][gi] += _bdot(parts[gi], dps[:, cols], TN)
            q = _bdot(dps[:, cols], w, NT)
            qh = _bdot(dps_halo[:, cols], w, NT)
            r_scr[pl.ds(0, tm), cols] = q * (1.0 / jnp.minimum(pos, float(win)))
            r_scr[pl.ds(tm, POOL_HALO), cols] = qh * (1.0 / jnp.minimum(pos_h, float(win)))
            dyn_scr[:, cols] = _window_sum(r_scr, cols, gi + 1, tm, scr[3:5], False) - q
        dx, dg = _rms_bwd(ins[0][...], ins[5][...], dyn_scr[...])
        dh = ins[4][...] + dx
        outs[0][...] = dh
        accs[3][...] += dg
        df, dgp = _rms_bwd(ins[9][...], ins[10][...], dh)
        outs[1][...] = df.astype(bf16)
        accs[4][...] += dgp
    ins = [("row", h2, D, 0), ("prev", h2, D, 0, POOL_HALO), ("row", d_pm, D, 0), ("next", d_pm, D, 0, POOL_HALO), ("row", d_res, D, 0),
           ("const", g_pre), ("const", pw), ("const", pb), ("const", psc), ("row", f_prev, D, 0), ("const", g_prev)]
    return rowcall(name, body, T, tm, ins, [(D, f32), (D, bf16)], accs=[(4, POOL_GD, POOL_GD), (1, D), (1, D), (1, D), (1, D)],
                   scratch=[pltpu.VMEM((tm + POOL_HALO, D), f32), pltpu.VMEM((tm + POOL_HALO, D), f32), pltpu.VMEM((tm, D), f32)]
                   + [pltpu.VMEM((tm + POOL_HALO, POOL_GD), f32)] * 2)


def local_step(T, x, tgt, W, ffn_weights, early_grads):
    tm = 512 if T >= 1024 else T // 2
    TKW = 4096 if T >= 4096 else T
    ng = W["norm_g"]
    g = lambda l, j: ng[l, j][None, :]
    G = {}

    tf = tm
    once = pl.Buffered(1)
    vec_f = pl.BlockSpec((1, D), lambda i: (0, 0))

    def fused_specs(t):
        rows = pl.BlockSpec((t, D), lambda i: (i, 0))
        return rows, [pl.BlockSpec((None, t, FF_SH), lambda i, s=s: (s, i, 0)) for s in range(4)], (SDS((T, D), f32), rows), (SDS((T, D), bf16), rows)
    rows_f, sh_f, out_f32, out_bf16 = fused_specs(tf)
    tf2 = min(T, 2 * tm)
    rows_f2, sh_f2, out2_f32, out2_bf16 = fused_specs(tf2)

    def resid_epilogue(with_pre):
        def ep(part, xs, os, accs):
            h = xs[0][...] + _rms(part, xs[1][...])
            os[0][...] = part
            os[1][...] = h
            if with_pre:
                os[2][...] = _rms(h, xs[2][...]).astype(bf16)
        return ep

    def bwd_epilogue(df_dtype):
        def ep(part, xs, os, accs):
            dx, dgp = _rms_bwd(xs[0][...], xs[3][...], part)
            dh = xs[2][...] + dx
            df, dgq = _rms_bwd(xs[1][...], xs[4][...], dh)
            os[0][...] = dh
            os[1][...] = df.astype(df_dtype)
            accs[0][...] += dgp
            accs[1][...] += dgq
        return ep

    def loss_epilogue(part, xs, os, accs):
        g_post = xs[2][...]
        e = xs[0][...] + _rms(part, g_post) - xs[1][...]
        accs[0][...] += jnp.sum(jnp.sum(e * e, axis=-1, keepdims=True) * (0.5 / D), axis=0, keepdims=True)
        dh = e * (1.0 / D)
        df, dg = _rms_bwd(part, g_post, dh)
        os[0][...] = dh
        os[1][...] = df.astype(bf16)
        accs[1][...] += dg

    def ffn_fwd(tag, n_bf, l, resid=None, loss=None):
        gate4, up4, act4 = ffn_up(f"ffn{tag}_up", T, min(T, 4 * tm), n_bf, W["wg4"], W["wu4"], l)
        wd_f = [pl.BlockSpec((None, FF_SH, D), lambda i, s=s: (s, l, 0), pipeline_mode=once) for s in range(4)]
        pairs = [(act4, sh_f2[s], W["wd4"], wd_f[s]) for s in range(4)]
        if loss is not None:
            return (gate4, up4, act4) + tuple(mm_fused(f"ffn{tag}_down", T // tf2, pairs, NN, [(loss[0], rows_f2), (loss[1], rows_f2), (loss[2], vec_f)],
                                                       [out2_f32, out2_bf16], [(1, 1), (1, D)], loss_epilogue))
        f, h_out = mm_fused(f"ffn{tag}_down", T // tf2, pairs, NN, [(resid[0], rows_f2), (resid[1], vec_f)], [out2_f32, out2_f32], [],
                            resid_epilogue(False))
        return gate4, up4, act4, f, h_out

    def ffn_bwd(tag, l, n_bf, gate4, up4, act4, d_f, h_out, f_pre, d_res, g_pre, g_post, df_dtype):
        d_gate4, d_up4 = ffn_dgu(f"ffn{tag}_dgu", T, min(T, 4 * tm), d_f, W["wd4"], gate4, up4, l)
        w_f = [pl.BlockSpec((None, D, FF_SH), lambda i, s=s: (s, l, 0), pipeline_mode=once) for s in range(4)]
        d_h, d_fp, dgp, dgq = mm_fused(
            f"ffn{tag}_dn", T // tf, [(d_gate4, sh_f[s], W["wg4"], w_f[s]) for s in range(4)] + [(d_up4, sh_f[s], W["wu4"], w_f[s]) for s in range(4)],
            NT, [(h_out, rows_f), (f_pre, rows_f), (d_res, rows_f), (g_pre, vec_f), (g_post, vec_f)],
            [out_f32, (SDS((T, D), df_dtype), rows_f)], [(1, D), (1, D)], bwd_epilogue(df_dtype))

        def wgrad(nm, a4, b):
            return mm(nm, (4, 1, T // TKW),
                      [(a4, pl.BlockSpec((None, TKW, FF_SH), lambda s, j, k: (s, k, 0)), b, pl.BlockSpec((TKW, D), lambda s, j, k: (k, 0)))],
                      TN, pl.BlockSpec((None, FF_SH, D), lambda s, j, k: (s, 0, 0)), SDS((4, FF_SH, D), f32))
        return d_h, d_fp, dgp, dgq, wgrad(f"ffn{tag}_dwg", d_gate4, n_bf), wgrad(f"ffn{tag}_dwu", d_up4, n_bf), wgrad(f"ffn{tag}_dwd", act4, d_f)

    y0 = rms_to_bf16("l0_prenorm", T, tf2, x, g(0, 0))
    uvz = matmul("in_uvz", [(y0, W["w_uvz"])], "nn", f32, 4 * tm, 1024)
    xbc = matmul("in_xbc", [(y0, W["w_xbc"])], "nn", f32, 4 * tm, 1024)
    dtr = matmul("in_dt", [(y0, W["w_dt"])], "nn", f32, 4 * tm, DT_PAD)
    y_a = gmlp_fwd("gmlp_fwd", T, tf2, uvz, W["ln_g"], W["ln_b"], W["wm"], W["bs"])
    xc = conv_fwd("conv_fwd", T, tm, xbc, W["conv_w"], W["conv_b"])
    y_ssd, sprev = ssd_fwd("ssd_fwd", T, xc, dtr, W["dtb"], W["alog"], W["dsk"])
    y_b = gate_fwd("gate_fwd", T, tf2, y_ssd, uvz, W["gn"])
    half = D // 2
    wo4 = W["wo4"]
    ycol = [pl.BlockSpec((tf2, half), lambda i, cb=cb: (i, cb)) for cb in range(2)]
    wo_s = [pl.BlockSpec((None, half, D), lambda i, s=s: (s, 0, 0), pipeline_mode=once) for s in range(4)]
    mixo, h1, n1 = mm_fused("out_proj", T // tf2, [(y_a, ycol[0], wo4, wo_s[0]), (y_a, ycol[1], wo4, wo_s[1]),
                                                  (y_b, ycol[0], wo4, wo_s[2]), (y_b, ycol[1], wo4, wo_s[3])], NN,
                            [(x, rows_f2), (g(0, 1), vec_f), (g(0, 2), vec_f)], [out2_f32, out2_f32, out2_bf16], [], resid_epilogue(True))
    W = dict(W)
    W["wg4"], W["wu4"], W["wd4"] = ffn_weights(h1)
    gate0, up0, act0, f1, h2 = ffn_fwd("0", n1, 0, resid=(h1, g(0, 3)))
    pm, h3, n3 = pool_fwd("pool_fwd", T, tm, h2, g(1, 0), W["pool_w"], W["pool_b"], W["pool_scale"], g(1, 1), g(1, 2))
    gate1, up1, act1, dh4, d_f2, loss_acc, dg13 = ffn_fwd("1", n3, 1, loss=(h3, tgt, g(1, 3)))
    d_h3, d_pm, dg12, dg11, dwg1, dwu1, dwd1 = ffn_bwd("1", 1, n3, gate1, up1, act1, d_f2, h3, pm, dh4, g(1, 2), g(1, 1), f32)
    d_h2, d_f1, G["pool_w"], G["pool_b"], G["pool_scale"], dg10, dg03 = pool_bwd("pool_bwd", T, tm, h2, d_pm, d_h3, g(1, 0), W["pool_w"], W["pool_b"],
                                                                                 W["pool_scale"], f1, g(0, 3))
    d_h1, d_mixo, dg02, dg01, dwg0, dwu0, dwd0 = ffn_bwd("0", 0, n1, gate0, up0, act0, d_f1, h1, mixo, d_h2, g(0, 2), g(0, 1), bf16)
    def d_wo(nm, y):
        return mm(nm, (2, 1, T // TKW), [(y, pl.BlockSpec((TKW, half), lambda s, j, k: (k, s)), d_mixo, pl.BlockSpec((TKW, D), lambda s, j, k: (k, 0)))],
                  TN, pl.BlockSpec((None, half, D), lambda s, j, k: (s, 0, 0)), SDS((2, half, D), f32))
    d_ycat = matmul("out_proj_dy", [(d_mixo, wo4.reshape(4 * half, D))], "nt", f32, 4 * tm, 1024)
    dwo_a, dwo_b = d_wo("out_proj_dwa", y_a), d_wo("out_proj_dwb", y_b)
    G["wo4"] = [dwo_a[0], dwo_a[1], dwo_b[0], dwo_b[1]]
    G["wgT4"], G["wuT4"], G["wd4"] = [dwg0, dwg1], [dwu0, dwu1], [dwd0, dwd1]
    token = early_grads(G)
    d_yssd, d_z, G["gn"] = gate_bwd("gate_bwd", T, tf2, y_ssd, uvz, d_ycat, 1, W["gn"] + token[0, 0])
    d_xc, d_dtr, G["dtb"], G["alog"], G["dsk"] = ssd_bwd("ssd_bwd", T, xc, dtr, sprev, d_yssd, W["dtb"], W["alog"], W["dsk"])
    d_pre, G["conv_w"], G["conv_b"] = conv_bwd_pre("conv_bwd_pre", T, tm, xbc, d_xc, W["conv_w"], W["conv_b"])
    d_xbc = conv_bwd_x("conv_bwd_x", T, tm, d_pre, W["conv_w"])
    d_u, d_v, G["ln_g"], G["ln_b"], G["wm"], G["bs"] = gmlp_bwd("gmlp_bwd", T, tf2, uvz, d_ycat, 0, W["ln_g"], W["ln_b"], W["wm"], W["bs"])
    w_u, w_v, w_z = W["w_uvz"][:, :D], W["w_uvz"][:, D:2 * D], W["w_uvz"][:, 2 * D:]
    def pre_epilogue(part, xs, os, accs):
        dx, dg = _rms_bwd(xs[0][...], xs[2][...], part)
        os[0][...] = xs[1][...] + dx
        accs[0][...] += dg
    blk = lambda w: pl.BlockSpec((tf, w), lambda i: (i, 0))
    whole = lambda a: pl.BlockSpec(a.shape, lambda i: (0, 0), pipeline_mode=once)
    grad_x, dg00 = mm_fused("in_dy0", T // tf, [(d_u, blk(D), w_u, whole(w_u)), (d_v, blk(D), w_v, whole(w_v)), (d_z, blk(D), w_z, whole(w_z)),
                                                (d_xbc, blk(CONV_DIM), W["w_xbc"], whole(W["w_xbc"])), (d_dtr, blk(DT_PAD), W["w_dt"], whole(W["w_dt"]))],
                            NT, [(x, rows_f), (d_h1, rows_f), (g(0, 0), vec_f)], [out_f32], [(1, D)], pre_epilogue)
    G["w_inT"] = [matmul("in_dwu", [(d_u, y0)], "tn", f32, 1024, 1024, TKW), matmul("in_dwv", [(d_v, y0)], "tn", f32, 1024, 1024, TKW),
                  matmul("in_dwz", [(d_z, y0)], "tn", f32, 1024, 1024, TKW), matmul("in_dwxbc", [(d_xbc, y0)], "tn", f32, 1024, 1024, TKW),
                  matmul("in_dwdt", [(d_dtr, y0)], "tn", f32, DT_PAD, 1024, TKW)[:N_HEADS]]
    G["norm_g"] = jnp.stack([jnp.concatenate([dg00, dg01, dg02, dg03], 0), jnp.concatenate([dg10, dg11, dg12, dg13], 0)])
    return loss_acc, grad_x, G


def build_weights(Wf):
    causal = jnp.tril(jnp.ones((CHUNK, CHUNK), bool))
    w_in = Wf["w_in"].astype(bf16)
    pad16 = lambda v: jnp.pad(v.reshape(1, N_HEADS).astype(f32), ((0, 0), (0, DT_PAD - N_HEADS)))
    return {
        "norm_g": Wf["norm_g"],
        "w_uvz": w_in[:, :3 * D], "w_xbc": w_in[:, 3 * D:3 * D + CONV_DIM],
        "w_dt": jnp.pad(w_in[:, 3 * D + CONV_DIM:], ((0, 0), (0, DT_PAD - N_HEADS))),
        "ln_g": Wf["gm_ln_g"].reshape(1, D), "ln_b": Wf["gm_ln_b"].reshape(1, D),
        "wm": jnp.where(causal[None], Wf["gm_ws"], 0).astype(bf16), "bs": Wf["gm_bs"].reshape(GM_HEADS, CHUNK, 1),
        "conv_w": Wf["conv_w"], "conv_b": Wf["conv_b"].reshape(1, CONV_DIM),
        "dtb": pad16(Wf["dt_bias"]), "alog": pad16(Wf["a_log"]), "dsk": pad16(Wf["d_skip"]),
        "gn": Wf["ssm_norm_g"].reshape(1, D),
        "wo4": Wf["wo4"].astype(bf16),
        "pool_w": Wf["pool_w"].astype(bf16), "pool_b": Wf["pool_b"].reshape(1, D), "pool_scale": Wf["pool_scale"].reshape(1, D),
    }


def small_grads(G):
    return {
        "norm_g": G["norm_g"],
        "gm_ln_g": G["ln_g"].reshape(D), "gm_ln_b": G["ln_b"].reshape(D),
        "gm_ws": G["wm"], "gm_bs": G["bs"].reshape(GM_HEADS, CHUNK),
        "conv_w": G["conv_w"], "conv_b": G["conv_b"].reshape(CONV_DIM),
        "dt_bias": G["dtb"][0, :N_HEADS], "a_log": G["alog"][0, :N_HEADS], "d_skip": G["dsk"][0, :N_HEADS],
        "ssm_norm_g": G["gn"].reshape(D),
        "pool_b": G["pool_b"].reshape(4, POOL_GD), "pool_scale": G["pool_scale"].reshape(D),
    }


MESH_ID = pl.DeviceIdType.MESH
ANY = pl.BlockSpec(memory_space=pl.ANY)


DMA_CHUNK_BYTES = 2 << 20
DMA_MAX_CHUNKS = 32


def _pieces(view, axis, align):
    shape = view.shape
    nbytes = math.prod(shape) * jnp.dtype(view.dtype).itemsize
    n = max(1, min(DMA_MAX_CHUNKS, -(-nbytes // DMA_CHUNK_BYTES)))
    rows = shape[axis]
    size = -(-rows // n)
    size = -(-size // align) * align
    out = []
    for s in range(0, rows, size):
        idx = [slice(None)] * len(shape)
        idx[axis] = pl.ds(s, min(size, rows - s))
        out.append(tuple(idx))
    return out


def comm_call(name, operands, out_shapes, plan):
    n_in = len(operands)
    n_out = len(out_shapes)
    n_remote, n_local = plan((0, 0, 0), [None] * n_in, [None] * n_out, True)

    def body(*refs):
        in_refs, out_refs = refs[:n_in], refs[n_in:n_in + n_out]
        send_sems, recv_sems, local_sems = refs[n_in + n_out:]
        me = (lax.axis_index("x"), lax.axis_index("y"), lax.axis_index("c"))
        remote, local = plan(me, in_refs, out_refs, False)
        align = lambda v: 16 if v.dtype == bf16 else 8
        for j, (s, d, axis) in enumerate(local):
            for ix in _pieces(s, axis, align(s)):
                pltpu.make_async_copy(s.at[ix], d.at[ix], local_sems.at[j]).start()
        peers = [tuple((1 - m) if f else m for m, f in zip(me, flip)) for flip, *_ in remote]
        for k, (flip, src, dst, _, axis) in enumerate(remote):
            for ix in _pieces(src, axis, align(src)):
                pltpu.make_async_remote_copy(src_ref=src.at[ix], dst_ref=dst.at[ix], send_sem=send_sems.at[k], recv_sem=recv_sems.at[k],
                                             device_id=peers[k], device_id_type=MESH_ID).start()
        for k, (flip, src, dst, landing, axis) in enumerate(remote):
            pltpu.make_async_remote_copy(src_ref=landing, dst_ref=landing, send_sem=send_sems.at[k], recv_sem=recv_sems.at[k],
                                         device_id=peers[k], device_id_type=MESH_ID).wait_recv()
        for k, (flip, src, dst, landing, axis) in enumerate(remote):
            pltpu.make_async_remote_copy(src_ref=src, dst_ref=dst, send_sem=send_sems.at[k], recv_sem=recv_sems.at[k],
                                         device_id=peers[k], device_id_type=MESH_ID).wait_send()
        for j, (s, d, axis) in enumerate(local):
            pltpu.make_async_copy(s, d, local_sems.at[j]).wait()

    return pl.pallas_call(
        body, name=name, out_shape=list(out_shapes), in_specs=[ANY] * n_in, out_specs=[ANY] * n_out,
        scratch_shapes=[pltpu.SemaphoreType.DMA((n_remote,)), pltpu.SemaphoreType.DMA((n_remote,)), pltpu.SemaphoreType.DMA((max(n_local, 1),))],
    )(*operands)


CHIP_FLIPS = ((1, 0, 0), (0, 1, 0), (1, 1, 0))
PAIR_FLIP = (0, 0, 1)


def gather_two_level(name, halved, whole):
    nh, nw = len(halved), len(whole)
    nf = len(CHIP_FLIPS)

    def body(*refs):
        srcs, outs = refs[:nh + nw], refs[nh + nw:2 * (nh + nw)]
        send_sems, recv_sems, fwd_send, fwd_recv = refs[2 * (nh + nw):]
        me = (lax.axis_index("x"), lax.axis_index("y"), lax.axis_index("c"))
        k, c = 2 * me[0] + me[1], me[2]
        sibling = (me[0], me[1], 1 - c)
        peers = [tuple((1 - m) if fl else m for m, fl in zip(me, flip)) for flip in CHIP_FLIPS]

        def half(ref, which):
            rh = ref.shape[0] // 2
            return ref.at[pl.ds(pl.multiple_of(which * rh, 16), rh), :]

        def ici(a, f):
            src = half(srcs[a], c) if a < nh else srcs[a]
            dst = half(outs[a].at[k], c) if a < nh else outs[a].at[k]
            return pltpu.make_async_remote_copy(src_ref=src, dst_ref=dst, send_sem=send_sems.at[a * nf + f], recv_sem=recv_sems.at[a * nf + f],
                                                device_id=peers[f], device_id_type=MESH_ID)

        def landed(a, f):
            slot = outs[a].at[_chip_of(me, CHIP_FLIPS[f])]
            return half(slot, c) if a < nh else slot

        def forward(a, f, which):
            v = half(outs[a].at[_chip_of(me, CHIP_FLIPS[f])], which)
            return pltpu.make_async_remote_copy(src_ref=v, dst_ref=v, send_sem=fwd_send.at[a * nf + f], recv_sem=fwd_recv.at[a * nf + f],
                                                device_id=sibling, device_id_type=MESH_ID)

        copies = [ici(a, f) for a in range(nh + nw) for f in range(nf)]
        for cp in copies:
            cp.start()
        fwds = []
        for a in range(nh):
            for f in range(nf):
                lv = landed(a, f)
                pltpu.make_async_remote_copy(src_ref=lv, dst_ref=lv, send_sem=send_sems.at[a * nf + f], recv_sem=recv_sems.at[a * nf + f],
                                             device_id=peers[f], device_id_type=MESH_ID).wait_recv()
                fw = forward(a, f, c)
                fw.start()
                fwds.append(fw)
        for a in range(nh, nh + nw):
            for f in range(nf):
                lv = landed(a, f)
                pltpu.make_async_remote_copy(src_ref=lv, dst_ref=lv, send_sem=send_sems.at[a * nf + f], recv_sem=recv_sems.at[a * nf + f],
                                             device_id=peers[f], device_id_type=MESH_ID).wait_recv()
        for a in range(nh):
            for f in range(nf):
                forward(a, f, 1 - c).wait_recv()
        for fw in fwds:
            fw.wait_send()
        for cp in copies:
            cp.wait_send()

    arrs = list(halved) + list(whole)
    n_ici = (nh + nw) * nf
    return pl.pallas_call(
        body, name=name, out_shape=[SDS((N_CHIPS,) + a.shape, a.dtype) for a in arrs], in_specs=[ANY] * len(arrs), out_specs=[ANY] * len(arrs),
        scratch_shapes=[pltpu.SemaphoreType.DMA((n_ici,)), pltpu.SemaphoreType.DMA((n_ici,)),
                        pltpu.SemaphoreType.DMA((nh * nf,)), pltpu.SemaphoreType.DMA((nh * nf,))],
    )(*arrs)


def pair_split_exchange(name, p, rh):
    def plan(me, ins, outs, count):
        if count:
            return 1, 0
        theirs = ins[0].at[:, pl.ds(pl.multiple_of((1 - me[2]) * rh, 8), rh), :]
        return [(PAIR_FLIP, theirs, outs[0], outs[0], 1)], []
    return comm_call(name, [p], [SDS((4, rh, p.shape[2]), p.dtype)], plan)[0]


def scatter_over_chips(name, cs):
    def plan(me, ins, outs, count):
        if count:
            return len(CHIP_FLIPS), 0
        k = 2 * me[0] + me[1]
        remote = []
        for flip in CHIP_FLIPS:
            kp = 2 * ((1 - me[0]) if flip[0] else me[0]) + ((1 - me[1]) if flip[1] else me[1])
            remote.append((flip, ins[0].at[kp], outs[0].at[k], outs[0].at[kp], 0))
        return remote, []
    return comm_call(name, [cs], [SDS(cs.shape, cs.dtype)], plan)[0]


def pair_swap(name, half):
    def plan(me, ins, outs, count):
        if count:
            return 1, 0
        return [(PAIR_FLIP, ins[0], outs[0], outs[0], 0)], []
    return comm_call(name, [half], [SDS(half.shape, half.dtype)], plan)[0]


def _row_tile(rows, cap=512):
    if rows <= cap:
        return rows
    t = cap - cap % 8
    while rows % t:
        t -= 8
    return t


def pair_sum(name, packs, got, c_arr, tile):
    rh = got.shape[1]
    nb = rh // tile

    def kern(c_ref, a_ref, b_ref, o16_ref):
        o16_ref[...] = (a_ref[...] + b_ref[...]).astype(bf16)
    blk = (None, tile, D)
    grid_spec = pltpu.PrefetchScalarGridSpec(
        num_scalar_prefetch=1, grid=(4, nb),
        in_specs=[pl.BlockSpec(blk, lambda s, i, c: (s, c[0] * nb + i, 0)), pl.BlockSpec(blk, lambda s, i, c: (s, i, 0))],
        out_specs=pl.BlockSpec(blk, lambda s, i, c: (s, i, 0)))
    return pl.pallas_call(kern, name=name, grid_spec=grid_spec, out_shape=SDS(got.shape, bf16),
                          compiler_params=pltpu.CompilerParams(dimension_semantics=("parallel", "parallel")))(c_arr, packs, got)


def chip_sum(name, own16, landed16, k_arr, tile):
    rh = own16.shape[1]
    nb = rh // tile

    def kern(k_ref, own_ref, l0, l1, l2, l3, o_ref):
        k = k_ref[0]
        s = None
        for j, lref in enumerate((l0, l1, l2, l3)):
            t = jnp.where(k == j, own_ref[...], lref[...]).astype(f32)
            s = t if s is None else s + t
        o_ref[...] = s
    blk = (None, tile, D)
    land = [pl.BlockSpec(blk, lambda i, k, j=j: (jnp.where(k[0] == j, (j + 1) % N_CHIPS, j), i, 0)) for j in range(N_CHIPS)]
    grid_spec = pltpu.PrefetchScalarGridSpec(
        num_scalar_prefetch=1, grid=(nb,),
        in_specs=[pl.BlockSpec(blk, lambda i, k: (k[0], i, 0))] + land,
        out_specs=pl.BlockSpec((tile, D), lambda i, k: (i, 0)))
    return pl.pallas_call(kern, name=name, grid_spec=grid_spec, out_shape=SDS((rh, D), f32),
                          compiler_params=pltpu.CompilerParams(dimension_semantics=("parallel",)))(k_arr, own16, landed16, landed16, landed16, landed16)


def adamw(name, w, g, m, v):
    R, C = w.shape
    tr = _row_tile(R, 256)

    def kern(w_ref, g_ref, m_ref, v_ref, d_ref, mo_ref, vo_ref):
        gg = g_ref[...]
        mn = ADAM_B1 * m_ref[...] + (1.0 - ADAM_B1) * gg
        vn = ADAM_B2 * v_ref[...] + (1.0 - ADAM_B2) * jnp.square(gg)
        m_hat = mn / (1.0 - ADAM_B1 ** ADAM_STEP)
        v_hat = vn / (1.0 - ADAM_B2 ** ADAM_STEP)
        d_ref[...] = -ADAM_LR * (m_hat / (jnp.sqrt(v_hat) + ADAM_EPS) + ADAM_WD * w_ref[...])
        mo_ref[...] = mn
        vo_ref[...] = vn
    spec = pl.BlockSpec((tr, C), lambda i: (i, 0))
    s = SDS((R, C), f32)
    return pl.pallas_call(kern, name=name, grid=(R // tr,), in_specs=[spec] * 4, out_specs=[spec] * 3, out_shape=[s, s, s],
                          compiler_params=pltpu.CompilerParams(dimension_semantics=("parallel",)))(w, g, m, v)


WEIGHT_NAMES = ("norm_g", "w_in", "gm_ln_g", "gm_ln_b", "gm_ws", "gm_bs", "conv_w", "conv_b", "dt_bias", "a_log", "d_skip",
                "ssm_norm_g", "w_out", "pool_w", "pool_b", "pool_scale", "ffn_w_gate", "ffn_w_up", "ffn_w_down")
SMALL = ("norm_g", "conv_w", "pool_b", "pool_scale")
REPL = ("gm_ln_g", "gm_ln_b", "gm_ws", "gm_bs", "conv_b", "dt_bias", "a_log", "d_skip", "ssm_norm_g")
SMALL_AXIS = {"norm_g": 2, "conv_w": 1, "pool_b": 1, "pool_scale": 0}
N_CHIPS = 4
IN_SH = IN_DIM // N_CHIPS
SMALL_ROWS = 8
REPL_ROWS = 72
E_OUT, E_GATE, E_UP, E_DOWN = 0, 512, 512 + 2 * FF_SH, 512 + 4 * FF_SH
E_POOL = E_DOWN + 2 * FF_SH
E_ROWS, E_TILE = E_POOL + 64, 400
L_SMALL, L_REPL, L_IN = 0, SMALL_ROWS, SMALL_ROWS + REPL_ROWS
L_END = L_IN + IN_SH
L_ROWS, L_TILE = 1408, 352


def _flat_rows(pieces, rows):
    v = jnp.concatenate([p.reshape(-1) for p in pieces])
    return jnp.pad(v, (0, rows * D - v.shape[0])).reshape(rows, D)


def _shard_small(name, full, k):
    ax = SMALL_AXIS[name]
    n = full.shape[ax] // N_CHIPS
    return lax.slice_in_dim(full, k * n, (k + 1) * n, axis=ax)


def _drop1(name, a):
    return a if name == "norm_g" else a[0]


HBM_SPEC = pl.BlockSpec(memory_space=pltpu.HBM)
SEM_SPEC = pl.BlockSpec(memory_space=pltpu.SEMAPHORE)
SPLIT_EFFECT = pltpu.SideEffectType.DATAFLOW_SIDE_EFFECTING


def _chip_of(me, flip):
    return 2 * ((1 - me[0]) if flip[0] else me[0]) + ((1 - me[1]) if flip[1] else me[1])


def gather_start(name, arrs, after, slotted=False):
    n = len(arrs)
    ncp = n * len(CHIP_FLIPS)

    def body(*refs):
        srcs, lands = refs[:n], refs[n:2 * n]
        send_sems, recv_sems, token = refs[2 * n + 1], refs[2 * n + 2], refs[-1]
        me = (lax.axis_index("x"), lax.axis_index("y"), lax.axis_index("c"))
        k = 2 * me[0] + me[1]
        for a in range(n):
            for f, flip in enumerate(CHIP_FLIPS):
                peer = tuple((1 - m) if fl else m for m, fl in zip(me, flip))
                src = srcs[a].at[_chip_of(me, flip)] if slotted else srcs[a]
                for ix in _pieces(src, 0, 16):
                    pltpu.make_async_remote_copy(src_ref=src.at[ix], dst_ref=lands[a].at[k].at[ix],
                                                 send_sem=send_sems.at[a * len(CHIP_FLIPS) + f], recv_sem=recv_sems.at[a * len(CHIP_FLIPS) + f],
                                                 device_id=peer, device_id_type=MESH_ID).start()
        token[...] = jnp.zeros_like(token)

    land_shapes = [a.shape if slotted else (N_CHIPS,) + a.shape for a in arrs]
    operands = [pltpu.with_memory_space_constraint(a, pltpu.HBM) for a in arrs]
    operands += [pltpu.with_memory_space_constraint(lax.empty(s, a.dtype), pltpu.HBM) for s, a in zip(land_shapes, arrs)]
    out = pl.pallas_call(
        body, name=name,
        out_shape=(pltpu.SemaphoreType.DMA((ncp,)), pltpu.SemaphoreType.DMA((ncp,)), *[pltpu.HBM(a.shape, a.dtype) for a in arrs],
                   *[pltpu.HBM(s, a.dtype) for s, a in zip(land_shapes, arrs)], SDS((8, 128), f32)),
        in_specs=[HBM_SPEC] * (2 * n) + [ANY], out_specs=(SEM_SPEC, SEM_SPEC, *[HBM_SPEC] * (2 * n), pl.BlockSpec(memory_space=pltpu.VMEM)),
        input_output_aliases={i: 2 + i for i in range(2 * n)},
        compiler_params=pltpu.CompilerParams(has_side_effects=SPLIT_EFFECT),
    )(*operands, after)
    return out[0], out[1], out[2:2 + n], out[2 + n:2 + 2 * n], out[-1]


def gather_wait(name, send_sems, recv_sems, thru, lands, after, slotted=False):
    n = len(thru)

    def body(*refs):
        srcs, lands_r = refs[:n], refs[n:2 * n]
        s_sems, r_sems = refs[2 * n], refs[2 * n + 1]
        me = (lax.axis_index("x"), lax.axis_index("y"), lax.axis_index("c"))
        k = 2 * me[0] + me[1]
        for a in range(n):
            for f, flip in enumerate(CHIP_FLIPS):
                peer = tuple((1 - m) if fl else m for m, fl in zip(me, flip))
                idx = a * len(CHIP_FLIPS) + f
                src = srcs[a].at[_chip_of(me, flip)] if slotted else srcs[a]
                pltpu.make_async_remote_copy(src_ref=src, dst_ref=lands_r[a].at[k], send_sem=s_sems.at[idx], recv_sem=r_sems.at[idx],
                                             device_id=peer, device_id_type=MESH_ID).wait_send()
                pltpu.make_async_remote_copy(src_ref=src, dst_ref=lands_r[a].at[_chip_of(me, flip)], send_sem=s_sems.at[idx],
                                             recv_sem=r_sems.at[idx], device_id=peer, device_id_type=MESH_ID).wait_recv()

    out = pl.pallas_call(
        body, name=name, out_shape=tuple(pltpu.HBM(t.shape, t.dtype) for t in (*thru, *lands)),
        in_specs=[HBM_SPEC] * (2 * n) + [SEM_SPEC, SEM_SPEC, ANY], out_specs=tuple([HBM_SPEC] * (2 * n)),
        input_output_aliases={i: i for i in range(2 * n)},
        compiler_params=pltpu.CompilerParams(has_side_effects=SPLIT_EFFECT),
    )(*thru, *lands, send_sems, recv_sems, after)
    return out[:n], out[n:]


def gather_weights(w_sh):
    big = [w_sh["w_in"][0], w_sh["w_out"][0], w_sh["pool_w"][0].reshape(4 * 64, POOL_GD)]
    small_pack = _flat_rows([w_sh[n] for n in SMALL], SMALL_ROWS)
    own = [b.astype(bf16) for b in big] + [small_pack]
    my_k = 2 * lax.axis_index("x") + lax.axis_index("y")
    s_in, s_out, s_pool, s_small = [lax.dynamic_update_slice(s, o[None], (my_k, 0, 0))
                                    for s, o in zip(gather_two_level("gather_weights", own[:3], own[3:]), own)]
    Wf = {n: w_sh[n][0] for n in REPL}
    Wf["w_in"] = s_in.transpose(1, 0, 2).reshape(D, IN_DIM)
    Wf["pool_w"] = s_pool.reshape(N_CHIPS, 4, 64, POOL_GD).transpose(1, 0, 2, 3).reshape(4, POOL_GD, POOL_GD)
    Wf["wo4"] = s_out
    small_shapes = [_drop1(n, w_sh[n]).shape for n in SMALL]
    parts = [_split_rows(s_small[k], small_shapes) for k in range(N_CHIPS)]
    for j, n in enumerate(SMALL):
        Wf[n] = jnp.concatenate([parts[k][j] for k in range(N_CHIPS)], axis=SMALL_AXIS[n])
    return Wf


def pack_early(G):
    slots = [jnp.concatenate([G["wo4"][k], G["wgT4"][0][k], G["wgT4"][1][k], G["wuT4"][0][k], G["wuT4"][1][k], G["wd4"][0][k], G["wd4"][1][k],
                              G["pool_w"][:, k * 64:(k + 1) * 64, :].reshape(64, D)], axis=0) for k in range(N_CHIPS)]
    return jnp.stack(slots)


def pack_late(G):
    sg = small_grads(G)
    repl = _flat_rows([sg[n] for n in REPL], REPL_ROWS)
    w_in_t = jnp.concatenate(G["w_inT"], axis=0)
    slots = [jnp.concatenate([_flat_rows([_shard_small(n, sg[n], k) for n in SMALL], SMALL_ROWS), repl,
                              jnp.pad(w_in_t[k * IN_SH:(k + 1) * IN_SH], ((0, L_ROWS - L_END), (0, 0)))], axis=0)
             for k in range(N_CHIPS)]
    return jnp.stack(slots)


def unpack_grads(early, late, w_sh):
    g = {"w_out": early[E_OUT:E_GATE], "ffn_w_down": early[E_DOWN:E_POOL], "pool_w": early[E_POOL:E_ROWS],
         "ffn_w_gate": jnp.stack([early[E_GATE + l * FF_SH:E_GATE + (l + 1) * FF_SH].T for l in range(2)]),
         "ffn_w_up": jnp.stack([early[E_UP + l * FF_SH:E_UP + (l + 1) * FF_SH].T for l in range(2)]),
         "w_in": late[L_IN:L_END].T}
    small = _split_rows(late[L_SMALL:L_REPL], [_drop1(n, w_sh[n]).shape for n in SMALL])
    repl = _split_rows(late[L_REPL:L_IN], [w_sh[n][0].shape for n in REPL])
    g.update(zip(SMALL, small))
    g.update(zip(REPL, repl))
    return {n: g[n].reshape(w_sh[n].shape) for n in WEIGHT_NAMES}


def _split_rows(flat2d, shapes):
    v = flat2d.reshape(-1)
    out, off = [], 0
    for s in shapes:
        n = math.prod(s)
        out.append(v[off:off + n].reshape(s))
        off += n
    return out


def kernel(x, norm_g, w_in, gm_ln_g, gm_ln_b, gm_ws, gm_bs, conv_w, conv_b, dt_bias, a_log, d_skip, ssm_norm_g, w_out, pool_w, pool_b, pool_scale, ffn_w_gate, ffn_w_up, ffn_w_down, loss_target, m_norm_g, m_w_in, m_gm_ln_g, m_gm_ln_b, m_gm_ws, m_gm_bs, m_conv_w, m_conv_b, m_dt_bias, m_a_log, m_d_skip, m_ssm_norm_g, m_w_out, m_pool_w, m_pool_b, m_pool_scale, m_ffn_w_gate, m_ffn_w_up, m_ffn_w_down, v_norm_g, v_w_in, v_gm_ln_g, v_gm_ln_b, v_gm_ws, v_gm_bs, v_conv_w, v_conv_b, v_dt_bias, v_a_log, v_d_skip, v_ssm_norm_g, v_w_out, v_pool_w, v_pool_b, v_pool_scale, v_ffn_w_gate, v_ffn_w_up, v_ffn_w_down):
    T = x.shape[1]
    w_sh = dict(zip(WEIGHT_NAMES, (norm_g, w_in, gm_ln_g, gm_ln_b, gm_ws, gm_bs, conv_w, conv_b, dt_bias, a_log, d_skip, ssm_norm_g, w_out,
                                   pool_w, pool_b, pool_scale, ffn_w_gate, ffn_w_up, ffn_w_down)))
    m_sh = dict(zip(WEIGHT_NAMES, (m_norm_g, m_w_in, m_gm_ln_g, m_gm_ln_b, m_gm_ws, m_gm_bs, m_conv_w, m_conv_b, m_dt_bias, m_a_log, m_d_skip,
                                   m_ssm_norm_g, m_w_out, m_pool_w, m_pool_b, m_pool_scale, m_ffn_w_gate, m_ffn_w_up, m_ffn_w_down)))
    v_sh = dict(zip(WEIGHT_NAMES, (v_norm_g, v_w_in, v_gm_ln_g, v_gm_ln_b, v_gm_ws, v_gm_bs, v_conv_w, v_conv_b, v_dt_bias, v_a_log, v_d_skip,
                                   v_ssm_norm_g, v_w_out, v_pool_w, v_pool_b, v_pool_scale, v_ffn_w_gate, v_ffn_w_up, v_ffn_w_down)))

    my_k = 2 * lax.axis_index("x") + lax.axis_index("y")
    ffn_own = [w_sh["ffn_w_gate"].reshape(2 * D, FF_SH).astype(bf16), w_sh["ffn_w_up"].reshape(2 * D, FF_SH).astype(bf16),
               w_sh["ffn_w_down"].reshape(2 * FF_SH, D).astype(bf16)]
    Wf = gather_weights(w_sh)
    send_sems, recv_sems, thru, lands, token = gather_start("gather_ffn_start", ffn_own, Wf["wo4"])
    Wf["norm_g"] = Wf["norm_g"] + token[0, 0]
    W = build_weights(Wf)

    def ffn_weights(after):
        _, landed = gather_wait("gather_ffn_wait", send_sems, recv_sems, thru, lands, after)
        return tuple(lax.dynamic_update_slice(l, o[None], (my_k, 0, 0)) for l, o in zip(landed, ffn_own))

    my_c = lax.axis_index("c")
    c_arr = my_c.astype(jnp.int32).reshape(1)
    k_arr = my_k.astype(jnp.int32).reshape(1)

    def pair_stage(tag, packs, tile):
        got = pair_split_exchange(f"grads{tag}_pair_split", packs, packs.shape[1] // 2)
        return pair_sum(f"grads{tag}_pair_sum", packs, got, c_arr, tile)

    def chip_stage(tag, pair16, landed, tile):
        half = chip_sum(f"grads{tag}_chip_sum", pair16, landed, k_arr, tile)
        other = pair_swap(f"grads{tag}_pair_swap", half)
        return jnp.concatenate([jnp.where(my_c == 0, half, other), jnp.where(my_c == 0, other, half)], axis=0)

    early = {}

    def early_grads(Ge):
        pair16 = pair_stage("E", pack_early(Ge), E_TILE)
        s_sems, r_sems, thru, lands, tok = gather_start("gradsE_scatter_start", [pair16], jnp.zeros((8, 128), f32), slotted=True)
        early.update(s_sems=s_sems, r_sems=r_sems, thru=thru, lands=lands)
        return tok

    loss_acc, grad_x, G = local_step(T, x[0], loss_target[0], W, ffn_weights, early_grads)
    pair_l = pair_stage("L", pack_late(G), L_TILE)
    total_l = chip_stage("L", pair_l, scatter_over_chips("gradsL_scatter", pair_l), L_TILE)
    (pair_e,), (landed_e,) = gather_wait("gradsE_scatter_wait", early["s_sems"], early["r_sems"], early["thru"], early["lands"], total_l,
                                         slotted=True)
    total_e = chip_stage("E", pair_e, landed_e, E_TILE)
    grads = unpack_grads(total_e, total_l, w_sh)

    delta, new_m, new_v = {}, {}, {}
    for n in WEIGHT_NAMES:
        shp = w_sh[n].shape
        two_d = (-1, shp[-1])
        d_, m_, v_ = adamw("adamw_" + n, w_sh[n].reshape(two_d), grads[n].reshape(two_d), m_sh[n].reshape(two_d), v_sh[n].reshape(two_d))
        delta[n], new_m[n], new_v[n] = d_.reshape(shp), m_.reshape(shp), v_.reshape(shp)

    loss = lax.psum(loss_acc[0, 0], ("x", "y", "c"))
    return (loss, grad_x[None], *[grads[n] for n in WEIGHT_NAMES], *[delta[n] for n in WEIGHT_NAMES],
            *[new_m[n] for n in WEIGHT_NAMES], *[new_v[n] for n in WEIGHT_NAMES])
```

```python
import math

import jax
import jax.numpy as jnp
from jax import lax
from jax.experimental import pallas as pl
from jax.experimental.pallas import tpu as pltpu

f32, bf16 = jnp.float32, jnp.bfloat16
SDS = jax.ShapeDtypeStruct

D = 1024
EPS = 1e-6
CHUNK = 128
GM_HEADS, GM_HD = 4, 256
SSM_GROUPS, SSM_HPG, SSM_P, SSM_N = 4, 4, 64, 128
N_HEADS = SSM_GROUPS * SSM_HPG
CONV_K = 4
CONV_DIM = 2048
POOL_WINDOWS = (2, 4, 8, 16)
POOL_GD = 256
POOL_HALO = 32
CONV_HALO = 8
D_FF = 2816
DT_PAD = 128
IN_DIM = 5136

ADAM_LR, ADAM_B1, ADAM_B2, ADAM_EPS, ADAM_WD, ADAM_STEP = 0.001, 0.9, 0.999, 1e-08, 0.01, 10

NT = (((1,), (1,)), ((), ()))
TN = (((0,), (0,)), ((), ()))
NN = (((1,), (0,)), ((), ()))
HI = lax.Precision.HIGHEST
MM_SUB = 256


def _silu(x):
    return x * jax.nn.sigmoid(x)


def _softplus(x):
    return jnp.maximum(x, 0.0) + jnp.log1p(jnp.exp(-jnp.abs(x)))


def _rms(x, g):
    return x * lax.rsqrt(jnp.mean(x * x, axis=-1, keepdims=True) + EPS) * g


def _rms_bwd(x, g, dy):
    r = lax.rsqrt(jnp.mean(x * x, axis=-1, keepdims=True) + EPS)
    xh = x * r
    dxh = dy * g
    dx = r * (dxh - xh * jnp.mean(dxh * xh, axis=-1, keepdims=True))
    return dx, jnp.sum(dy * xh, axis=0, keepdims=True)


def _bdot(a, b, dims=NN):
    return lax.dot_general(a.astype(bf16), b.astype(bf16), dims, preferred_element_type=f32)


def matmul(name, pairs, mode, out_dtype, tm, tn, tk=None):
    a0, b0 = pairs[0]
    if mode == "tn":
        M, N, K = a0.shape[1], b0.shape[1], a0.shape[0]
    else:
        M, K = a0.shape
        N = b0.shape[1] if mode == "nn" else b0.shape[0]
    tm, tn = min(tm, M), min(tn, N)
    assert M % tm == 0 and N % tn == 0, (name, M, N, tm, tn)
    if tk is None:
        nk = 1
    else:
        assert len(pairs) == 1 and K % tk == 0
        nk = K // tk
    dims = {"nn": NN, "nt": NT, "tn": TN}[mode]
    in_specs, args = [], []
    for a, b in pairs:
        kk = (a.shape[0] if mode == "tn" else a.shape[1]) if tk is None else tk
        if mode == "tn":
            in_specs.append(pl.BlockSpec((kk, tm), lambda j, i, k: (k, i)))
            in_specs.append(pl.BlockSpec((kk, tn), lambda j, i, k: (k, j)))
        elif mode == "nn":
            in_specs.append(pl.BlockSpec((tm, kk), lambda j, i, k: (i, k)))
            in_specs.append(pl.BlockSpec((kk, tn), lambda j, i, k: (k, j)))
        else:
            in_specs.append(pl.BlockSpec((tm, kk), lambda j, i, k: (i, k)))
            in_specs.append(pl.BlockSpec((tn, kk), lambda j, i, k: (j, k)))
        args += [a, b]
    npairs = len(pairs)

    def kern(*refs):
        o = refs[2 * npairs]
        part = None
        for p in range(npairs):
            d = _bdot(refs[2 * p][...], refs[2 * p + 1][...], dims)
            part = d if part is None else part + d
        if nk == 1:
            o[...] = part.astype(out_dtype)
        else:
            acc = refs[2 * npairs + 1]
            k = pl.program_id(2)

            @pl.when(k == 0)
            def _():
                acc[...] = part

            @pl.when(k > 0)
            def _():
                acc[...] += part

            @pl.when(k == nk - 1)
            def _():
                o[...] = acc[...].astype(out_dtype)

    return pl.pallas_call(
        kern, name=name, grid=(N // tn, M // tm, nk),
        in_specs=in_specs, out_specs=pl.BlockSpec((tm, tn), lambda j, i, k: (i, j)),
        out_shape=SDS((M, N), out_dtype),
        scratch_shapes=[pltpu.VMEM((tm, tn), f32)] if nk > 1 else [],
        compiler_params=pltpu.CompilerParams(dimension_semantics=("parallel", "parallel", "arbitrary")),
    )(*args)


def mm(name, grid, pairs, dims, o_spec, out_shape):
    nk = grid[2]
    npairs = len(pairs)
    in_specs, args = [], []
    for a, a_spec, b, b_spec in pairs:
        in_specs += [a_spec, b_spec]
        args += [a, b]
    blk = tuple(d for d in o_spec.block_shape if d is not None)

    def kern(*refs):
        o = refs[2 * npairs]
        part = None
        for p in range(npairs):
            d = _bdot(refs[2 * p][...], refs[2 * p + 1][...], dims)
            part = d if part is None else part + d
        if nk == 1:
            o[...] = part.astype(o.dtype)
        else:
            acc = refs[2 * npairs + 1]
            k = pl.program_id(2)

            @pl.when(k == 0)
            def _():
                acc[...] = part

            @pl.when(k > 0)
            def _():
                acc[...] += part

            @pl.when(k == nk - 1)
            def _():
                o[...] = acc[...].astype(o.dtype)

    return pl.pallas_call(
        kern, name=name, grid=grid, in_specs=in_specs, out_specs=o_spec, out_shape=out_shape,
        scratch_shapes=[pltpu.VMEM(blk, f32)] if nk > 1 else [],
        compiler_params=pltpu.CompilerParams(dimension_semantics=("parallel", "parallel", "arbitrary")),
    )(*args)


def mm_fused(name, n_row_blocks, pairs, dims, extra_ins, outs, accs, epilogue):
    npairs, nx, no, na = len(pairs), len(extra_ins), len(outs), len(accs)
    in_specs, args = [], []
    for a, a_spec, b, b_spec in pairs:
        in_specs += [a_spec, b_spec]
        args += [a, b]
    for arr, spec in extra_ins:
        in_specs.append(spec)
        args.append(arr)

    rows_blk = outs[0][1].block_shape[0]
    sub = min(rows_blk, MM_SUB)

    def kern(*refs):
        x_refs = refs[2 * npairs:2 * npairs + nx]
        o_refs = refs[2 * npairs + nx:2 * npairs + nx + no]
        a_refs = refs[2 * npairs + nx + no:]
        if na:
            @pl.when(pl.program_id(0) == 0)
            def _():
                for a in a_refs:
                    a[...] = jnp.zeros(a.shape, f32)
        for r0 in range(0, rows_blk, sub):
            rows = pl.ds(r0, sub)
            part = None
            for p in range(npairs):
                d = _bdot(refs[2 * p][rows, :], refs[2 * p + 1][...], dims)
                part = d if part is None else part + d
            epilogue(part, [x.at[rows, :] if x.shape[0] == rows_blk else x for x in x_refs], [o.at[rows, :] for o in o_refs], a_refs)

    return pl.pallas_call(
        kern, name=name, grid=(n_row_blocks,), in_specs=in_specs,
        out_specs=[spec for _, spec in outs] + [pl.BlockSpec(tuple(s), lambda i, nd=len(s): (0,) * nd) for s in accs],
        out_shape=[s for s, _ in outs] + [SDS(tuple(s), f32) for s in accs],
        compiler_params=pltpu.CompilerParams(dimension_semantics=("arbitrary",)),
    )(*args)


FF_SH = D_FF // 4


def ffn_up(name, T, tm, n_bf, wg4, wu4, l):
    sub = min(tm, MM_SUB)

    def kern(n_ref, wg_ref, wu_ref, g_ref, u_ref, a_ref):
        for r0 in range(0, tm, sub):
            rows = pl.ds(r0, sub)
            n = n_ref[rows, :]
            g = jnp.dot(n, wg_ref[...], preferred_element_type=f32)
            u = jnp.dot(n, wu_ref[...], preferred_element_type=f32)
            g_ref[rows, :] = g.astype(bf16)
            u_ref[rows, :] = u.astype(bf16)
            a_ref[rows, :] = (_silu(g) * u).astype(bf16)
    w_spec = pl.BlockSpec((None, D, FF_SH), lambda k, i: (k, l, 0))
    o_spec = pl.BlockSpec((None, tm, FF_SH), lambda k, i: (k, i, 0))
    s = SDS((4, T, FF_SH), bf16)
    return pl.pallas_call(kern, name=name, grid=(4, T // tm), in_specs=[pl.BlockSpec((tm, D), lambda k, i: (i, 0)), w_spec, w_spec],
                          out_specs=[o_spec] * 3, out_shape=[s, s, s],
                          compiler_params=pltpu.CompilerParams(dimension_semantics=("parallel", "parallel")))(n_bf, wg4, wu4)


def ffn_dgu(name, T, tm, d_f, wd4, gate4, up4, l):
    rc = 16

    sub = min(tm, MM_SUB)

    def kern(df_ref, wd_ref, g_ref, u_ref, dg_ref, du_ref, dact_ref):
        for s0 in range(0, tm, sub):
            dact_ref[pl.ds(s0, sub), :] = _bdot(df_ref[pl.ds(s0, sub), :], wd_ref[...], NT)
            for r0 in range(s0, s0 + sub, rc):
                rows = pl.ds(r0, rc)
                _, vjp = jax.vjp(lambda a, b: _silu(a) * b, g_ref[rows, :].astype(f32), u_ref[rows, :].astype(f32))
                dg, du = vjp(dact_ref[rows, :])
                dg_ref[rows, :] = dg.astype(bf16)
                du_ref[rows, :] = du.astype(bf16)
    a_spec = pl.BlockSpec((None, tm, FF_SH), lambda k, i: (k, i, 0))
    s = SDS((4, T, FF_SH), bf16)
    return pl.pallas_call(kern, name=name, grid=(4, T // tm),
                          in_specs=[pl.BlockSpec((tm, D), lambda k, i: (i, 0)), pl.BlockSpec((None, FF_SH, D), lambda k, i: (k, l, 0)), a_spec, a_spec],
                          out_specs=[a_spec] * 2, out_shape=[s, s], scratch_shapes=[pltpu.VMEM((tm, FF_SH), f32)],
                          compiler_params=pltpu.CompilerParams(dimension_semantics=("parallel", "parallel")))(d_f, wd4, gate4, up4)


def rowcall(name, body, T, tm, ins, outs, accs=(), scratch=(), reverse=False, sub=None):
    n = T // tm
    assert T % tm == 0

    def blk(i):
        return (n - 1 - i) if reverse else i

    in_specs, args = [], []
    for spec in ins:
        kind, arr = spec[0], spec[1]
        if kind == "row":
            _, _, w, cb = spec
            in_specs.append(pl.BlockSpec((tm, w), lambda i, cb=cb: (blk(i), cb)))
        elif kind == "prev":
            _, _, w, cb, h = spec
            r = tm // h
            in_specs.append(pl.BlockSpec((h, w), lambda i, cb=cb, r=r: (jnp.maximum(blk(i) * r - 1, 0), cb)))
        elif kind == "next":
            _, _, w, cb, h = spec
            r = tm // h
            in_specs.append(pl.BlockSpec((h, w), lambda i, cb=cb, r=r, h=h: (jnp.minimum((blk(i) + 1) * r, T // h - 1), cb)))
        else:
            nd = arr.ndim
            in_specs.append(pl.BlockSpec(arr.shape, lambda i, nd=nd: (0,) * nd))
        args.append(arr)
    out_shape = [SDS((T, w), dt) for w, dt in outs] + [SDS(tuple(s), f32) for s in accs]
    out_specs = [pl.BlockSpec((tm, w), lambda i: (blk(i), 0)) for w, _ in outs]
    out_specs += [pl.BlockSpec(tuple(s), lambda i, nd=len(s): (0,) * nd) for s in accs]
    ni, no, na = len(ins), len(outs), len(accs)

    def kern(*refs):
        i = pl.program_id(0)
        in_refs, out_refs = refs[:ni], refs[ni:ni + no]
        acc_refs, scr = refs[ni + no:ni + no + na], refs[ni + no + na:]
        if na:
            @pl.when(i == 0)
            def _():
                for a in acc_refs:
                    a[...] = jnp.zeros(a.shape, f32)
        if sub is None or sub >= tm:
            body(blk(i), n, in_refs, out_refs, acc_refs, scr)
        else:
            for r0 in range(0, tm, sub):
                rows = pl.ds(r0, sub)
                body(blk(i), n, [r.at[rows, :] if spec[0] == "row" else r for r, spec in zip(in_refs, ins)],
                     [o.at[rows, :] for o in out_refs], acc_refs, [s.at[rows, :] for s in scr])

    res = pl.pallas_call(
        kern, name=name, grid=(n,), in_specs=in_specs, out_specs=out_specs, out_shape=out_shape,
        scratch_shapes=list(scratch),
        compiler_params=pltpu.CompilerParams(dimension_semantics=("arbitrary",)),
    )(*args)
    return res


def rms_to_bf16(name, T, tm, x, g):
    def body(i, n, ins, outs, accs, scr):
        outs[0][...] = _rms(ins[0][...], ins[1][...]).astype(bf16)
    return rowcall(name, body, T, tm, [("row", x, D, 0), ("const", g)], [(D, bf16)], sub=64)[0]


def _layer_norm_parts(x):
    mu = jnp.mean(x, axis=-1, keepdims=True)
    xc = x - mu
    r = lax.rsqrt(jnp.mean(xc * xc, axis=-1, keepdims=True) + EPS)
    return xc * r, r


def _gelu_and_slope(x):
    c, a = math.sqrt(2.0 / math.pi), 0.044715
    x2 = x * x
    t = jnp.tanh(c * (x + a * (x * x2)))
    cdf = 0.5 * (1.0 + t)
    slope = cdf + (0.5 * x) * (1.0 - t * t) * (c * (1.0 + (3.0 * a) * x2))
    return x * cdf, slope


def gmlp_fwd(name, T, tm, uvz, ln_g, ln_b, wm, bs):
    def body(i, n, ins, outs, accs, scr):
        gu = _gelu_and_slope(ins[0][...])[0]
        xh, _ = _layer_norm_parts(_gelu_and_slope(ins[1][...])[0])
        vln = (xh * ins[2][...] + ins[3][...]).astype(bf16)
        for c in range(ins[0].shape[0] // CHUNK):
            rows = slice(c * CHUNK, (c + 1) * CHUNK)
            for h in range(GM_HEADS):
                cols = slice(h * GM_HD, (h + 1) * GM_HD)
                mixed = jnp.dot(ins[4][h], vln[rows, cols], preferred_element_type=f32) + ins[5][h]
                outs[0][rows, cols] = (gu[rows, cols] * mixed).astype(bf16)
    return rowcall(name, body, T, tm, [("row", uvz, D, 0), ("row", uvz, D, 1), ("const", ln_g), ("const", ln_b), ("const", wm), ("const", bs)],
                   [(D, bf16)], sub=CHUNK)[0]


def gmlp_bwd(name, T, tm, uvz, d_ya, d_cb, ln_g, ln_b, wm, bs):
    def body(i, n, ins, outs, accs, scr):
        u, v, dya = ins[0][...], ins[1][...], ins[2][...]
        gu, slope_u = _gelu_and_slope(u)
        gv, slope_v = _gelu_and_slope(v)
        xh, r = _layer_norm_parts(gv)
        lng = ins[3][...]
        vln = (xh * lng + ins[4][...]).astype(bf16)
        rr = lax.broadcasted_iota(jnp.int32, (CHUNK, CHUNK), 0)
        cc = lax.broadcasted_iota(jnp.int32, (CHUNK, CHUNK), 1)
        causal = (rr >= cc).astype(f32)
        dvln_ref = scr[0]
        dgu_ref = scr[1]
        for c in range(ins[0].shape[0] // CHUNK):
            rows = slice(c * CHUNK, (c + 1) * CHUNK)
            for h in range(GM_HEADS):
                cols = slice(h * GM_HD, (h + 1) * GM_HD)
                w = ins[5][h]
                blk = vln[rows, cols]
                mixed = jnp.dot(w, blk, preferred_element_type=f32) + ins[6][h]
                dy = dya[rows, cols]
                dgu_ref[rows, cols] = dy * mixed
                dm = dy * gu[rows, cols]
                accs[3][h] += jnp.sum(dm, axis=1, keepdims=True)
                accs[2][h] += _bdot(dm, blk, NT) * causal
                dvln_ref[rows, cols] = _bdot(w, dm, TN)
        dvln = dvln_ref[...]
        accs[0][...] += jnp.sum(dvln * xh, axis=0, keepdims=True)
        accs[1][...] += jnp.sum(dvln, axis=0, keepdims=True)
        dxh = dvln * lng
        dgv = r * (dxh - jnp.mean(dxh, axis=-1, keepdims=True) - xh * jnp.mean(dxh * xh, axis=-1, keepdims=True))
        outs[0][...] = (dgu_ref[...] * slope_u).astype(bf16)
        outs[1][...] = (dgv * slope_v).astype(bf16)
    return rowcall(name, body, T, tm,
                   [("row", uvz, D, 0), ("row", uvz, D, 1), ("row", d_ya, D, d_cb), ("const", ln_g), ("const", ln_b), ("const", wm), ("const", bs)],
                   [(D, bf16), (D, bf16)], accs=[(1, D), (1, D), (GM_HEADS, CHUNK, CHUNK), (GM_HEADS, CHUNK, 1)],
                   scratch=[pltpu.VMEM((tm, D), f32), pltpu.VMEM((tm, D), f32)], sub=CHUNK)


CONV_RC, CONV_LB = 64, 256


def _conv_fill(i, x_ref, halo_ref, scr, tm):
    scr[pl.ds(0, CONV_HALO), :] = jnp.where(i > 0, halo_ref[...], 0.0)
    scr[pl.ds(CONV_HALO, tm), :] = x_ref[...]


def _conv_taps(scr, r0, lanes):
    return [scr[pl.ds(r0 + CONV_HALO - (CONV_K - 1) + k, CONV_RC), lanes] for k in range(CONV_K)]


def conv_fwd(name, T, tm, xbc, conv_w, conv_b):
    def body(i, n, ins, outs, accs, scr):
        s = scr[0]
        _conv_fill(i, ins[0], ins[1], s, tm)
        for lb in range(CONV_DIM // CONV_LB):
            lanes = slice(lb * CONV_LB, (lb + 1) * CONV_LB)
            w, b = ins[2][:, lanes], ins[3][:, lanes]

            for r0 in range(0, tm, CONV_RC):
                taps = _conv_taps(s, r0, lanes)
                pre = b + sum(w[k:k + 1] * taps[k] for k in range(CONV_K))
                outs[0][pl.ds(r0, CONV_RC), lanes] = _silu(pre)
    return rowcall(name, body, T, tm, [("row", xbc, CONV_DIM, 0), ("prev", xbc, CONV_DIM, 0, CONV_HALO), ("const", conv_w), ("const", conv_b)],
                   [(CONV_DIM, f32)], scratch=[pltpu.VMEM((tm + CONV_HALO, CONV_DIM), f32)])[0]


def conv_bwd_pre(name, T, tm, xbc, d_xc, conv_w, conv_b):
    def body(i, n, ins, outs, accs, scr):
        s = scr[0]
        _conv_fill(i, ins[0], ins[1], s, tm)
        fold = lambda v: jnp.sum(v.reshape(CONV_RC // 8, 8, CONV_LB), axis=0)
        for lb in range(CONV_DIM // CONV_LB):
            lanes = slice(lb * CONV_LB, (lb + 1) * CONV_LB)
            w, b = ins[3][:, lanes], ins[4][:, lanes]

            sums = [jnp.zeros((8, CONV_LB), f32)] * (CONV_K + 1)
            for r0 in range(0, tm, CONV_RC):
                taps = _conv_taps(s, r0, lanes)
                pre = b + sum(w[k:k + 1] * taps[k] for k in range(CONV_K))
                _, vjp = jax.vjp(_silu, pre)
                dpre = vjp(ins[2][pl.ds(r0, CONV_RC), lanes])[0]
                outs[0][pl.ds(r0, CONV_RC), lanes] = dpre
                sums = [sums[k] + fold(dpre * taps[k]) for k in range(CONV_K)] + [sums[CONV_K] + fold(dpre)]
            for k in range(CONV_K):
                accs[0][pl.ds(k, 1), lanes] += jnp.sum(sums[k], axis=0, keepdims=True)
            accs[1][:, lanes] += jnp.sum(sums[CONV_K], axis=0, keepdims=True)
    return rowcall(name, body, T, tm,
                   [("row", xbc, CONV_DIM, 0), ("prev", xbc, CONV_DIM, 0, CONV_HALO), ("row", d_xc, CONV_DIM, 0), ("const", conv_w), ("const", conv_b)],
                   [(CONV_DIM, f32)], accs=[(CONV_K, CONV_DIM), (1, CONV_DIM)], scratch=[pltpu.VMEM((tm + CONV_HALO, CONV_DIM), f32)])


def conv_bwd_x(name, T, tm, d_pre, conv_w):
    def body(i, n, ins, outs, accs, scr):
        s = scr[0]
        s[pl.ds(0, tm), :] = ins[0][...]
        s[pl.ds(tm, CONV_HALO), :] = jnp.where(i < n - 1, ins[1][...], 0.0)
        for lb in range(CONV_DIM // CONV_LB):
            lanes = slice(lb * CONV_LB, (lb + 1) * CONV_LB)
            w = ins[2][:, lanes]

            for r0 in range(0, tm, CONV_RC):
                dx = sum(w[k:k + 1] * s[pl.ds(r0 + CONV_K - 1 - k, CONV_RC), lanes] for k in range(CONV_K))
                outs[0][pl.ds(r0, CONV_RC), lanes] = dx.astype(bf16)
    return rowcall(name, body, T, tm, [("row", d_pre, CONV_DIM, 0), ("next", d_pre, CONV_DIM, 0, CONV_HALO), ("const", conv_w)],
                   [(CONV_DIM, bf16)], scratch=[pltpu.VMEM((tm + CONV_HALO, CONV_DIM), f32)])[0]


def _ssd_prep(dtr, dtb, alog):
    rr = lax.broadcasted_iota(jnp.int32, (CHUNK, CHUNK), 0)
    cc = lax.broadcasted_iota(jnp.int32, (CHUNK, CHUNK), 1)
    dt = _softplus(dtr + dtb)
    dA = dt * -jnp.exp(alog)
    acum = jnp.dot((rr >= cc).astype(f32), dA, precision=HI, preferred_element_type=f32)
    return dt, acum, acum.T, jnp.sum(dA, axis=0, keepdims=True)


def _ssd_group(g, x, Bm, Cm, S, dt, acum, acumT, tot, dsk):
    rr = lax.broadcasted_iota(jnp.int32, (CHUNK, CHUNK), 0)
    cc = lax.broadcasted_iota(jnp.int32, (CHUNK, CHUNK), 1)
    tril = rr >= cc
    lane = lax.broadcasted_iota(jnp.int32, (1, DT_PAD), 1)
    sub = lax.broadcasted_iota(jnp.int32, (DT_PAD, 1), 0)
    glane = lax.broadcasted_iota(jnp.int32, (1, SSM_HPG * SSM_P), 1) // SSM_P
    hm = [(glane == r).astype(f32) for r in range(SSM_HPG)]
    pick = lambda v, r: jnp.sum(v * (lane == SSM_HPG * g + r).astype(f32), axis=1, keepdims=True)
    cols = [pick(acum, r) for r in range(SSM_HPG)]
    tots = [pick(tot, r) for r in range(SSM_HPG)]
    spread = lambda vals: sum(vals[r] * hm[r] for r in range(SSM_HPG))
    xdt = x * spread([pick(dt, r) for r in range(SSM_HPG)])
    cb = _bdot(Cm, Bm, NT)
    y = x * spread([pick(dsk, r) for r in range(SSM_HPG)])
    for r in range(SSM_HPG):
        row = jnp.sum(acumT * (sub == SSM_HPG * g + r).astype(f32), axis=0, keepdims=True)
        dec = jnp.exp(jnp.where(tril, cols[r] - row, -jnp.inf))
        y = y + _bdot(cb * dec, xdt * hm[r])
    y = y + _bdot(Cm, S) * spread([jnp.exp(c) for c in cols])
    dte = spread([jnp.exp(tots[r] - cols[r]) for r in range(SSM_HPG)])
    s_new = S * spread([jnp.exp(t) for t in tots]) + _bdot(Bm, xdt * dte, TN)
    return y, s_new


def _ssd_ins(xc, dtr):
    gw = SSM_HPG * SSM_P
    ins = [("row", xc, gw, g) for g in range(SSM_GROUPS)]
    ins += [("row", xc, SSM_N, D // SSM_N + g) for g in range(SSM_GROUPS)]
    ins += [("row", xc, SSM_N, D // SSM_N + SSM_GROUPS + g) for g in range(SSM_GROUPS)]
    ins += [("row", dtr, DT_PAD, 0)]
    return ins


SSD_CPS = 4


def ssd_fwd(name, T, xc, dtr, dtb, alog, dsk):
    gw = SSM_HPG * SSM_P
    cps = min(SSD_CPS, T // CHUNK)

    def body(i, n, ins, outs, accs, scr):
        S = scr[0]

        @pl.when(i == 0)
        def _():
            S[...] = jnp.zeros(S.shape, f32)
        S4 = tuple(S[:, g * gw:(g + 1) * gw] for g in range(4))
        for c in range(cps):
            rows = pl.ds(c * CHUNK, CHUNK)
            X4 = tuple(ins[g][rows, :] for g in range(4))
            B4 = tuple(ins[4 + g][rows, :] for g in range(4))
            C4 = tuple(ins[8 + g][rows, :] for g in range(4))
            prep = _ssd_prep(ins[12][rows, :], ins[13][...], ins[14][...])
            nxt = []
            for g in range(4):
                outs[1][rows, g * gw:(g + 1) * gw] = S4[g]
                y, s_new = _ssd_group(g, X4[g], B4[g], C4[g], S4[g], *prep, ins[15][...])
                outs[0][rows, g * gw:(g + 1) * gw] = y
                nxt.append(s_new)
            S4 = tuple(nxt)
        for g in range(4):
            S[:, g * gw:(g + 1) * gw] = S4[g]
    ins = _ssd_ins(xc, dtr) + [("const", dtb), ("const", alog), ("const", dsk)]
    return rowcall(name, body, T, cps * CHUNK, ins, [(D, f32), (D, f32)], scratch=[pltpu.VMEM((SSM_N, D), f32)])


def ssd_bwd(name, T, xc, dtr, sprev, d_y, dtb, alog, dsk):
    gw = SSM_HPG * SSM_P

    def body(i, n, ins, outs, accs, scr):
        dS = scr[0]

        @pl.when(i == n - 1)
        def _():
            dS[...] = jnp.zeros(dS.shape, f32)
        dS4 = tuple(dS[:, g * gw:(g + 1) * gw] for g in range(4))
        def chunk(X4, dtr_c, B4, C4, S4, dtb_c, alog_c, dsk_c):
            prep = _ssd_prep(dtr_c, dtb_c, alog_c)
            res = [_ssd_group(g, X4[g], B4[g], C4[g], S4[g], *prep, dsk_c) for g in range(4)]
            return tuple(r[0] for r in res), tuple(r[1] for r in res)
        X4 = tuple(ins[g][...] for g in range(4))
        B4 = tuple(ins[4 + g][...] for g in range(4))
        C4 = tuple(ins[8 + g][...] for g in range(4))
        S4 = tuple(ins[13 + g][...] for g in range(4))
        dY4 = tuple(ins[17 + g][...] for g in range(4))
        _, vjp = jax.vjp(chunk, X4, ins[12][...], B4, C4, S4, ins[21][...], ins[22][...], ins[23][...])
        dX4, ddtr, dB4, dC4, dS4, ddtb, dalog, ddsk = vjp((dY4, dS4))
        for g in range(4):
            outs[0][:, g * gw:(g + 1) * gw] = dX4[g]
            outs[0][:, D + g * SSM_N:D + (g + 1) * SSM_N] = dB4[g]
            outs[0][:, D + (SSM_GROUPS + g) * SSM_N:D + (SSM_GROUPS + g + 1) * SSM_N] = dC4[g]
            dS[:, g * gw:(g + 1) * gw] = dS4[g]
        outs[1][...] = ddtr.astype(bf16)
        accs[0][...] += ddtb
        accs[1][...] += dalog
        accs[2][...] += ddsk
    ins = _ssd_ins(xc, dtr) + [("row", sprev, gw, g) for g in range(4)] + [("row", d_y, gw, g) for g in range(4)]
    ins += [("const", dtb), ("const", alog), ("const", dsk)]
    return rowcall(name, body, T, CHUNK, ins, [(CONV_DIM, f32), (DT_PAD, bf16)], accs=[(1, DT_PAD)] * 3,
                   scratch=[pltpu.VMEM((SSM_N, D), f32)], reverse=True)


def _gate_group(y, z, g):
    return _rms(y * _silu(z), g)


def gate_fwd(name, T, tm, y, uvz, gn):
    def body(i, n, ins, outs, accs, scr):
        for g in range(SSM_GROUPS):
            cols = slice(g * 256, (g + 1) * 256)
            outs[0][:, cols] = _gate_group(ins[0][:, cols], ins[1][:, cols], ins[2][:, cols]).astype(bf16)
    return rowcall(name, body, T, tm, [("row", y, D, 0), ("row", uvz, D, 2), ("const", gn)], [(D, bf16)], sub=64)[0]


def gate_bwd(name, T, tm, y, uvz, d_yb, d_cb, gn):
    def body(i, n, ins, outs, accs, scr):
        for g in range(SSM_GROUPS):
            cols = slice(g * 256, (g + 1) * 256)
            _, vjp = jax.vjp(_gate_group, ins[0][:, cols], ins[1][:, cols], ins[3][:, cols])
            dy, dz, dg = vjp(ins[2][:, cols])
            outs[0][:, cols] = dy
            outs[1][:, cols] = dz.astype(bf16)
            accs[0][:, cols] += dg
    return rowcall(name, body, T, tm, [("row", y, D, 0), ("row", uvz, D, 2), ("row", d_yb, D, d_cb), ("const", gn)],
                   [(D, f32), (D, bf16)], accs=[(1, D)], sub=64)


def _window_sum(src, cols, levels, tm, lv, trailing):
    cur, cur_cols = src, cols
    for l in range(1, levels + 1):
        shift = 2 ** (l - 1)
        last = l == levels
        if trailing:
            start = POOL_HALO if last else 8 * l
            rows = tm if last else tm + POOL_HALO - start
            new = cur[pl.ds(start, rows), cur_cols] + cur[pl.ds(start - shift, rows), cur_cols]
        else:
            start = 0
            rows = tm if last else tm + POOL_HALO - 8 * l
            new = cur[pl.ds(0, rows), cur_cols] + cur[pl.ds(shift, rows), cur_cols]
        if last:
            return new
        nxt = lv[l % 2]
        nxt[pl.ds(start, rows), :] = new
        cur, cur_cols = nxt, slice(None)


def _pool_diff(i, tm, h_ref, halo_ref, g_ref, scr, lv):
    g = g_ref[...]
    yn = _rms(h_ref[...], g)
    scr[pl.ds(0, POOL_HALO), :] = jnp.where(i > 0, _rms(halo_ref[...], g), 0.0)
    scr[pl.ds(POOL_HALO, tm), :] = yn
    pos = (i * tm + lax.broadcasted_iota(jnp.int32, (tm, 1), 0) + 1).astype(f32)
    parts = []
    for gi, win in enumerate(POOL_WINDOWS):
        cols = slice(gi * POOL_GD, (gi + 1) * POOL_GD)
        s = _window_sum(scr, cols, gi + 1, tm, lv, True)
        parts.append(s * (1.0 / jnp.minimum(pos, float(win))) - yn[:, cols])
    return parts


def pool_fwd(name, T, tm, h2, g_pre, pw, pb, psc, g_post, g_next):
    def body(i, n, ins, outs, accs, scr):
        parts = _pool_diff(i, tm, ins[0], ins[1], ins[2], scr[0], scr[1:3])
        for gi in range(len(POOL_WINDOWS)):
            cols = slice(gi * POOL_GD, (gi + 1) * POOL_GD)
            o = _bdot(parts[gi], ins[3][gi]) + ins[4][:, cols]
            outs[0][:, cols] = o * ins[5][:, cols]
        h = ins[0][...] + _rms(outs[0][...], ins[6][...])
        outs[1][...] = h
        outs[2][...] = _rms(h, ins[7][...]).astype(bf16)
    return rowcall(name, body, T, tm, [("row", h2, D, 0), ("prev", h2, D, 0, POOL_HALO), ("const", g_pre), ("const", pw), ("const", pb), ("const", psc),
                                       ("const", g_post), ("const", g_next)],
                   [(D, f32), (D, f32), (D, bf16)], scratch=[pltpu.VMEM((tm + POOL_HALO, D), f32)] + [pltpu.VMEM((tm + POOL_HALO, POOL_GD), f32)] * 2)


def pool_bwd(name, T, tm, h2, d_pm, d_res, g_pre, pw, pb, psc, f_prev, g_prev):
    def body(i, n, ins, outs, accs, scr):
        parts = _pool_diff(i, tm, ins[0], ins[1], ins[5], scr[0], scr[3:5])
        dpm = ins[2][...]
        psc_v = ins[8][...]
        dps = dpm * psc_v
        dps_halo = jnp.where(i < n - 1, ins[3][...] * psc_v, 0.0)
        accs[1][...] += jnp.sum(dps, axis=0, keepdims=True)
        pos = (i * tm + lax.broadcasted_iota(jnp.int32, (tm, 1), 0) + 1).astype(f32)
        pos_h = ((i + 1) * tm + lax.broadcasted_iota(jnp.int32, (POOL_HALO, 1), 0) + 1).astype(f32)
        r_scr = scr[1]
        dyn_scr = scr[2]
        for gi, win in enumerate(POOL_WINDOWS):
            cols = slice(gi * POOL_GD, (gi + 1) * POOL_GD)
            w = ins[6][gi]
            o = _bdot(parts[gi], w) + ins[7][:, cols]
            accs[2][:, cols] += jnp.sum(dpm[:, cols] * o, axis=0, keepdims=True)
            accs[0][gi] += _bdot(parts[gi], dps[:, cols], TN)
            q = _bdot(dps[:, cols], w, NT)
            qh = _bdot(dps_halo[:, cols], w, NT)
            r_scr[pl.ds(0, tm), cols] = q * (1.0 / jnp.minimum(pos, float(win)))
            r_scr[pl.ds(tm, POOL_HALO), cols] = qh * (1.0 / jnp.minimum(pos_h, float(win)))
            dyn_scr[:, cols] = _window_sum(r_scr, cols, gi + 1, tm, scr[3:5], False) - q
        dx, dg = _rms_bwd(ins[0][...], ins[5][...], dyn_scr[...])
        dh = ins[4][...] + dx
        outs[0][...] = dh
        accs[3][...] += dg
        df, dgp = _rms_bwd(ins[9][...], ins[10][...], dh)
        outs[1][...] = df.astype(bf16)
        accs[4][...] += dgp
    ins = [("row", h2, D, 0), ("prev", h2, D, 0, POOL_HALO), ("row", d_pm, D, 0), ("next", d_pm, D, 0, POOL_HALO), ("row", d_res, D, 0),
           ("const", g_pre), ("const", pw), ("const", pb), ("const", psc), ("row", f_prev, D, 0), ("const", g_prev)]
    return rowcall(name, body, T, tm, ins, [(D, f32), (D, bf16)], accs=[(4, POOL_GD, POOL_GD), (1, D), (1, D), (1, D), (1, D)],
                   scratch=[pltpu.VMEM((tm + POOL_HALO, D), f32), pltpu.VMEM((tm + POOL_HALO, D), f32), pltpu.VMEM((tm, D), f32)]
                   + [pltpu.VMEM((tm + POOL_HALO, POOL_GD), f32)] * 2)


def local_step(T, x, tgt, W, ffn_weights, early_grads, early_grads_next):
    tm = 512 if T >= 1024 else T // 2
    TKW = 4096 if T >= 4096 else T
    ng = W["norm_g"]
    g = lambda l, j: ng[l, j][None, :]
    G = {}

    tf = tm
    once = pl.Buffered(1)
    vec_f = pl.BlockSpec((1, D), lambda i: (0, 0))

    def fused_specs(t):
        rows = pl.BlockSpec((t, D), lambda i: (i, 0))
        return rows, [pl.BlockSpec((None, t, FF_SH), lambda i, s=s: (s, i, 0)) for s in range(4)], (SDS((T, D), f32), rows), (SDS((T, D), bf16), rows)
    rows_f, sh_f, out_f32, out_bf16 = fused_specs(tf)
    tf2 = min(T, 2 * tm)
    rows_f2, sh_f2, out2_f32, out2_bf16 = fused_specs(tf2)

    def resid_epilogue(with_pre):
        def ep(part, xs, os, accs):
            h = xs[0][...] + _rms(part, xs[1][...])
            os[0][...] = part
            os[1][...] = h
            if with_pre:
                os[2][...] = _rms(h, xs[2][...]).astype(bf16)
        return ep

    def bwd_epilogue(df_dtype):
        def ep(part, xs, os, accs):
            dx, dgp = _rms_bwd(xs[0][...], xs[3][...], part)
            dh = xs[2][...] + dx
            df, dgq = _rms_bwd(xs[1][...], xs[4][...], dh)
            os[0][...] = dh
            os[1][...] = df.astype(df_dtype)
            accs[0][...] += dgp
            accs[1][...] += dgq
        return ep

    def loss_epilogue(part, xs, os, accs):
        g_post = xs[2][...]
        e = xs[0][...] + _rms(part, g_post) - xs[1][...]
        accs[0][...] += jnp.sum(jnp.sum(e * e, axis=-1, keepdims=True) * (0.5 / D), axis=0, keepdims=True)
        dh = e * (1.0 / D)
        df, dg = _rms_bwd(part, g_post, dh)
        os[0][...] = dh
        os[1][...] = df.astype(bf16)
        accs[1][...] += dg

    def ffn_fwd(tag, n_bf, l, resid=None, loss=None):
        gate4, up4, act4 = ffn_up(f"ffn{tag}_up", T, min(T, 4 * tm), n_bf, W["wg4"], W["wu4"], l)
        wd_f = [pl.BlockSpec((None, FF_SH, D), lambda i, s=s: (s, l, 0), pipeline_mode=once) for s in range(4)]
        pairs = [(act4, sh_f2[s], W["wd4"], wd_f[s]) for s in range(4)]
        if loss is not None:
            return (gate4, up4, act4) + tuple(mm_fused(f"ffn{tag}_down", T // tf2, pairs, NN, [(loss[0], rows_f2), (loss[1], rows_f2), (loss[2], vec_f)],
                                                       [out2_f32, out2_bf16], [(1, 1), (1, D)], loss_epilogue))
        f, h_out = mm_fused(f"ffn{tag}_down", T // tf2, pairs, NN, [(resid[0], rows_f2), (resid[1], vec_f)], [out2_f32, out2_f32], [],
                            resid_epilogue(False))
        return gate4, up4, act4, f, h_out

    def ffn_bwd(tag, l, n_bf, gate4, up4, act4, d_f, h_out, f_pre, d_res, g_pre, g_post, df_dtype):
        d_gate4, d_up4 = ffn_dgu(f"ffn{tag}_dgu", T, min(T, 4 * tm), d_f, W["wd4"], gate4, up4, l)
        w_f = [pl.BlockSpec((None, D, FF_SH), lambda i, s=s: (s, l, 0), pipeline_mode=once) for s in range(4)]
        d_h, d_fp, dgp, dgq = mm_fused(
            f"ffn{tag}_dn", T // tf, [(d_gate4, sh_f[s], W["wg4"], w_f[s]) for s in range(4)] + [(d_up4, sh_f[s], W["wu4"], w_f[s]) for s in range(4)],
            NT, [(h_out, rows_f), (f_pre, rows_f), (d_res, rows_f), (g_pre, vec_f), (g_post, vec_f)],
            [out_f32, (SDS((T, D), df_dtype), rows_f)], [(1, D), (1, D)], bwd_epilogue(df_dtype))

        def wgrad(nm, a4, b):
            return mm(nm, (4, 1, T // TKW),
                      [(a4, pl.BlockSpec((None, TKW, FF_SH), lambda s, j, k: (s, k, 0)), b, pl.BlockSpec((TKW, D), lambda s, j, k: (k, 0)))],
                      TN, pl.BlockSpec((None, FF_SH, D), lambda s, j, k: (s, 0, 0)), SDS((4, FF_SH, D), f32))
        return d_h, d_fp, dgp, dgq, wgrad(f"ffn{tag}_dwg", d_gate4, n_bf), wgrad(f"ffn{tag}_dwu", d_up4, n_bf), wgrad(f"ffn{tag}_dwd", act4, d_f)

    y0 = rms_to_bf16("l0_prenorm", T, tf2, x, g(0, 0))
    uvz = matmul("in_uvz", [(y0, W["w_uvz"])], "nn", f32, 4 * tm, 1024)
    xbc = matmul("in_xbc", [(y0, W["w_xbc"])], "nn", f32, 4 * tm, 1024)
    dtr = matmul("in_dt", [(y0, W["w_dt"])], "nn", f32, 4 * tm, DT_PAD)
    y_a = gmlp_fwd("gmlp_fwd", T, tf2, uvz, W["ln_g"], W["ln_b"], W["wm"], W["bs"])
    xc = conv_fwd("conv_fwd", T, tm, xbc, W["conv_w"], W["conv_b"])
    y_ssd, sprev = ssd_fwd("ssd_fwd", T, xc, dtr, W["dtb"], W["alog"], W["dsk"])
    y_b = gate_fwd("gate_fwd", T, tf2, y_ssd, uvz, W["gn"])
    half = D // 2
    wo4 = W["wo4"]
    ycol = [pl.BlockSpec((tf2, half), lambda i, cb=cb: (i, cb)) for cb in range(2)]
    wo_s = [pl.BlockSpec((None, half, D), lambda i, s=s: (s, 0, 0), pipeline_mode=once) for s in range(4)]
    mixo, h1, n1 = mm_fused("out_proj", T // tf2, [(y_a, ycol[0], wo4, wo_s[0]), (y_a, ycol[1], wo4, wo_s[1]),
                                                  (y_b, ycol[0], wo4, wo_s[2]), (y_b, ycol[1], wo4, wo_s[3])], NN,
                            [(x, rows_f2), (g(0, 1), vec_f), (g(0, 2), vec_f)], [out2_f32, out2_f32, out2_bf16], [], resid_epilogue(True))
    W = dict(W)
    W["wg4"], W["wu4"], W["wd4"] = ffn_weights(h1)
    gate0, up0, act0, f1, h2 = ffn_fwd("0", n1, 0, resid=(h1, g(0, 3)))
    pm, h3, n3 = pool_fwd("pool_fwd", T, tm, h2, g(1, 0), W["pool_w"], W["pool_b"], W["pool_scale"], g(1, 1), g(1, 2))
    gate1, up1, act1, dh4, d_f2, loss_acc, dg13 = ffn_fwd("1", n3, 1, loss=(h3, tgt, g(1, 3)))
    d_h3, d_pm, dg12, dg11, dwg1, dwu1, dwd1 = ffn_bwd("1", 1, n3, gate1, up1, act1, d_f2, h3, pm, dh4, g(1, 2), g(1, 1), f32)
    d_h2, d_f1, G["pool_w"], G["pool_b"], G["pool_scale"], dg10, dg03 = pool_bwd("pool_bwd", T, tm, h2, d_pm, d_h3, g(1, 0), W["pool_w"], W["pool_b"],
                                                                                 W["pool_scale"], f1, g(0, 3))
    d_h1, d_mixo, dg02, dg01, dwg0, dwu0, dwd0 = ffn_bwd("0", 0, n1, gate0, up0, act0, d_f1, h1, mixo, d_h2, g(0, 2), g(0, 1), bf16)
    def d_wo(nm, y):
        return mm(nm, (2, 1, T // TKW), [(y, pl.BlockSpec((TKW, half), lambda s, j, k: (k, s)), d_mixo, pl.BlockSpec((TKW, D), lambda s, j, k: (k, 0)))],
                  TN, pl.BlockSpec((None, half, D), lambda s, j, k: (s, 0, 0)), SDS((2, half, D), f32))
    d_ycat = matmul("out_proj_dy", [(d_mixo, wo4.reshape(4 * half, D))], "nt", f32, 4 * tm, 1024)
    dwo_a, dwo_b = d_wo("out_proj_dwa", y_a), d_wo("out_proj_dwb", y_b)
    G["wo4"] = [dwo_a[0], dwo_a[1], dwo_b[0], dwo_b[1]]
    G["wgT4"], G["wuT4"], G["wd4"] = [dwg0, dwg1], [dwu0, dwu1], [dwd0, dwd1]
    token = early_grads(G)
    d_yssd, d_z, G["gn"] = gate_bwd("gate_bwd", T, tf2, y_ssd, uvz, d_ycat, 1, W["gn"] + token[0, 0])
    d_xc, d_dtr, G["dtb"], G["alog"], G["dsk"] = ssd_bwd("ssd_bwd", T, xc, dtr, sprev, d_yssd, W["dtb"], W["alog"], W["dsk"])
    token = early_grads_next(d_dtr)
    d_pre, G["conv_w"], G["conv_b"] = conv_bwd_pre("conv_bwd_pre", T, tm, xbc, d_xc, W["conv_w"], W["conv_b"] + token[0, 0])
    d_xbc = conv_bwd_x("conv_bwd_x", T, tm, d_pre, W["conv_w"])
    d_u, d_v, G["ln_g"], G["ln_b"], G["wm"], G["bs"] = gmlp_bwd("gmlp_bwd", T, tf2, uvz, d_ycat, 0, W["ln_g"], W["ln_b"], W["wm"], W["bs"])
    w_u, w_v, w_z = W["w_uvz"][:, :D], W["w_uvz"][:, D:2 * D], W["w_uvz"][:, 2 * D:]
    def pre_epilogue(part, xs, os, accs):
        dx, dg = _rms_bwd(xs[0][...], xs[2][...], part)
        os[0][...] = xs[1][...] + dx
        accs[0][...] += dg
    blk = lambda w: pl.BlockSpec((tf, w), lambda i: (i, 0))
    whole = lambda a: pl.BlockSpec(a.shape, lambda i: (0, 0), pipeline_mode=once)
    grad_x, dg00 = mm_fused("in_dy0", T // tf, [(d_u, blk(D), w_u, whole(w_u)), (d_v, blk(D), w_v, whole(w_v)), (d_z, blk(D), w_z, whole(w_z)),
                                                (d_xbc, blk(CONV_DIM), W["w_xbc"], whole(W["w_xbc"])), (d_dtr, blk(DT_PAD), W["w_dt"], whole(W["w_dt"]))],
                            NT, [(x, rows_f), (d_h1, rows_f), (g(0, 0), vec_f)], [out_f32], [(1, D)], pre_epilogue)
    G["w_inT"] = [matmul("in_dwu", [(d_u, y0)], "tn", f32, 1024, 1024, TKW), matmul("in_dwv", [(d_v, y0)], "tn", f32, 1024, 1024, TKW),
                  matmul("in_dwz", [(d_z, y0)], "tn", f32, 1024, 1024, TKW), matmul("in_dwxbc", [(d_xbc, y0)], "tn", f32, 1024, 1024, TKW),
                  matmul("in_dwdt", [(d_dtr, y0)], "tn", f32, DT_PAD, 1024, TKW)[:N_HEADS]]
    G["norm_g"] = jnp.stack([jnp.concatenate([dg00, dg01, dg02, dg03], 0), jnp.concatenate([dg10, dg11, dg12, dg13], 0)])
    return loss_acc, grad_x, G


def build_weights(Wf):
    causal = jnp.tril(jnp.ones((CHUNK, CHUNK), bool))
    w_in = Wf["w_in"].astype(bf16)
    pad16 = lambda v: jnp.pad(v.reshape(1, N_HEADS).astype(f32), ((0, 0), (0, DT_PAD - N_HEADS)))
    return {
        "norm_g": Wf["norm_g"],
        "w_uvz": w_in[:, :3 * D], "w_xbc": w_in[:, 3 * D:3 * D + CONV_DIM],
        "w_dt": jnp.pad(w_in[:, 3 * D + CONV_DIM:], ((0, 0), (0, DT_PAD - N_HEADS))),
        "ln_g": Wf["gm_ln_g"].reshape(1, D), "ln_b": Wf["gm_ln_b"].reshape(1, D),
        "wm": jnp.where(causal[None], Wf["gm_ws"], 0).astype(bf16), "bs": Wf["gm_bs"].reshape(GM_HEADS, CHUNK, 1),
        "conv_w": Wf["conv_w"], "conv_b": Wf["conv_b"].reshape(1, CONV_DIM),
        "dtb": pad16(Wf["dt_bias"]), "alog": pad16(Wf["a_log"]), "dsk": pad16(Wf["d_skip"]),
        "gn": Wf["ssm_norm_g"].reshape(1, D),
        "wo4": Wf["wo4"].astype(bf16),
        "pool_w": Wf["pool_w"].astype(bf16), "pool_b": Wf["pool_b"].reshape(1, D), "pool_scale": Wf["pool_scale"].reshape(1, D),
    }


def small_grads(G):
    return {
        "norm_g": G["norm_g"],
        "gm_ln_g": G["ln_g"].reshape(D), "gm_ln_b": G["ln_b"].reshape(D),
        "gm_ws": G["wm"], "gm_bs": G["bs"].reshape(GM_HEADS, CHUNK),
        "conv_w": G["conv_w"], "conv_b": G["conv_b"].reshape(CONV_DIM),
        "dt_bias": G["dtb"][0, :N_HEADS], "a_log": G["alog"][0, :N_HEADS], "d_skip": G["dsk"][0, :N_HEADS],
        "ssm_norm_g": G["gn"].reshape(D),
        "pool_b": G["pool_b"].reshape(4, POOL_GD), "pool_scale": G["pool_scale"].reshape(D),
    }


MESH_ID = pl.DeviceIdType.MESH
ANY = pl.BlockSpec(memory_space=pl.ANY)


DMA_CHUNK_BYTES = 2 << 20
DMA_MAX_CHUNKS = 32


def _pieces(view, axis, align):
    shape = view.shape
    nbytes = math.prod(shape) * jnp.dtype(view.dtype).itemsize
    n = max(1, min(DMA_MAX_CHUNKS, -(-nbytes // DMA_CHUNK_BYTES)))
    rows = shape[axis]
    size = -(-rows // n)
    size = -(-size // align) * align
    out = []
    for s in range(0, rows, size):
        idx = [slice(None)] * len(shape)
        idx[axis] = pl.ds(s, min(size, rows - s))
        out.append(tuple(idx))
    return out


def comm_call(name, operands, out_shapes, plan):
    n_in = len(operands)
    n_out = len(out_shapes)
    n_remote, n_local = plan((0, 0, 0), [None] * n_in, [None] * n_out, True)

    def body(*refs):
        in_refs, out_refs = refs[:n_in], refs[n_in:n_in + n_out]
        send_sems, recv_sems, local_sems = refs[n_in + n_out:]
        me = (lax.axis_index("x"), lax.axis_index("y"), lax.axis_index("c"))
        remote, local = plan(me, in_refs, out_refs, False)
        align = lambda v: 16 if v.dtype == bf16 else 8
        for j, (s, d, axis) in enumerate(local):
            for ix in _pieces(s, axis, align(s)):
                pltpu.make_async_copy(s.at[ix], d.at[ix], local_sems.at[j]).start()
        peers = [tuple((1 - m) if f else m for m, f in zip(me, flip)) for flip, *_ in remote]
        for k, (flip, src, dst, _, axis) in enumerate(remote):
            for ix in _pieces(src, axis, align(src)):
                pltpu.make_async_remote_copy(src_ref=src.at[ix], dst_ref=dst.at[ix], send_sem=send_sems.at[k], recv_sem=recv_sems.at[k],
                                             device_id=peers[k], device_id_type=MESH_ID).start()
        for k, (flip, src, dst, landing, axis) in enumerate(remote):
            pltpu.make_async_remote_copy(src_ref=landing, dst_ref=landing, send_sem=send_sems.at[k], recv_sem=recv_sems.at[k],
                                         device_id=peers[k], device_id_type=MESH_ID).wait_recv()
        for k, (flip, src, dst, landing, axis) in enumerate(remote):
            pltpu.make_async_remote_copy(src_ref=src, dst_ref=dst, send_sem=send_sems.at[k], recv_sem=recv_sems.at[k],
                                         device_id=peers[k], device_id_type=MESH_ID).wait_send()
        for j, (s, d, axis) in enumerate(local):
            pltpu.make_async_copy(s, d, local_sems.at[j]).wait()

    return pl.pallas_call(
        body, name=name, out_shape=list(out_shapes), in_specs=[ANY] * n_in, out_specs=[ANY] * n_out,
        scratch_shapes=[pltpu.SemaphoreType.DMA((n_remote,)), pltpu.SemaphoreType.DMA((n_remote,)), pltpu.SemaphoreType.DMA((max(n_local, 1),))],
    )(*operands)


CHIP_FLIPS = ((1, 0, 0), (0, 1, 0), (1, 1, 0))
PAIR_FLIP = (0, 0, 1)


def gather_two_level(name, halved, whole):
    nh, nw = len(halved), len(whole)
    nf = len(CHIP_FLIPS)

    def body(*refs):
        srcs, outs = refs[:nh + nw], refs[nh + nw:2 * (nh + nw)]
        send_sems, recv_sems, fwd_send, fwd_recv = refs[2 * (nh + nw):]
        me = (lax.axis_index("x"), lax.axis_index("y"), lax.axis_index("c"))
        k, c = 2 * me[0] + me[1], me[2]
        sibling = (me[0], me[1], 1 - c)
        peers = [tuple((1 - m) if fl else m for m, fl in zip(me, flip)) for flip in CHIP_FLIPS]

        def half(ref, which):
            rh = ref.shape[0] // 2
            return ref.at[pl.ds(pl.multiple_of(which * rh, 16), rh), :]

        def ici(a, f):
            src = half(srcs[a], c) if a < nh else srcs[a]
            dst = half(outs[a].at[k], c) if a < nh else outs[a].at[k]
            return pltpu.make_async_remote_copy(src_ref=src, dst_ref=dst, send_sem=send_sems.at[a * nf + f], recv_sem=recv_sems.at[a * nf + f],
                                                device_id=peers[f], device_id_type=MESH_ID)

        def landed(a, f):
            slot = outs[a].at[_chip_of(me, CHIP_FLIPS[f])]
            return half(slot, c) if a < nh else slot

        def forward(a, f, which):
            v = half(outs[a].at[_chip_of(me, CHIP_FLIPS[f])], which)
            return pltpu.make_async_remote_copy(src_ref=v, dst_ref=v, send_sem=fwd_send.at[a * nf + f], recv_sem=fwd_recv.at[a * nf + f],
                                                device_id=sibling, device_id_type=MESH_ID)

        copies = [ici(a, f) for a in range(nh + nw) for f in range(nf)]
        for cp in copies:
            cp.start()
        fwds = []
        for a in range(nh):
            for f in range(nf):
                lv = landed(a, f)
                pltpu.make_async_remote_copy(src_ref=lv, dst_ref=lv, send_sem=send_sems.at[a * nf + f], recv_sem=recv_sems.at[a * nf + f],
                                             device_id=peers[f], device_id_type=MESH_ID).wait_recv()
                fw = forward(a, f, c)
                fw.start()
                fwds.append(fw)
        for a in range(nh, nh + nw):
            for f in range(nf):
                lv = landed(a, f)
                pltpu.make_async_remote_copy(src_ref=lv, dst_ref=lv, send_sem=send_sems.at[a * nf + f], recv_sem=recv_sems.at[a * nf + f],
                                             device_id=peers[f], device_id_type=MESH_ID).wait_recv()
        for a in range(nh):
            for f in range(nf):
                forward(a, f, 1 - c).wait_recv()
        for fw in fwds:
            fw.wait_send()
        for cp in copies:
            cp.wait_send()

    arrs = list(halved) + list(whole)
    n_ici = (nh + nw) * nf
    return pl.pallas_call(
        body, name=name, out_shape=[SDS((N_CHIPS,) + a.shape, a.dtype) for a in arrs], in_specs=[ANY] * len(arrs), out_specs=[ANY] * len(arrs),
        scratch_shapes=[pltpu.SemaphoreType.DMA((n_ici,)), pltpu.SemaphoreType.DMA((n_ici,)),
                        pltpu.SemaphoreType.DMA((nh * nf,)), pltpu.SemaphoreType.DMA((nh * nf,))],
    )(*arrs)


def pair_split_exchange(name, p, rh):
    def plan(me, ins, outs, count):
        if count:
            return 1, 0
        theirs = ins[0].at[:, pl.ds(pl.multiple_of((1 - me[2]) * rh, 8), rh), :]
        return [(PAIR_FLIP, theirs, outs[0], outs[0], 1)], []
    return comm_call(name, [p], [SDS((4, rh, p.shape[2]), p.dtype)], plan)[0]


def scatter_over_chips(name, cs):
    def plan(me, ins, outs, count):
        if count:
            return len(CHIP_FLIPS), 0
        k = 2 * me[0] + me[1]
        remote = []
        for flip in CHIP_FLIPS:
            kp = 2 * ((1 - me[0]) if flip[0] else me[0]) + ((1 - me[1]) if flip[1] else me[1])
            remote.append((flip, ins[0].at[kp], outs[0].at[k], outs[0].at[kp], 0))
        return remote, []
    return comm_call(name, [cs], [SDS(cs.shape, cs.dtype)], plan)[0]


def pair_swap(name, half):
    def plan(me, ins, outs, count):
        if count:
            return 1, 0
        return [(PAIR_FLIP, ins[0], outs[0], outs[0], 0)], []
    return comm_call(name, [half], [SDS(half.shape, half.dtype)], plan)[0]


def _row_tile(rows, cap=512):
    if rows <= cap:
        return rows
    t = cap - cap % 8
    while rows % t:
        t -= 8
    return t


def pair_sum(name, packs, got, c_arr, tile):
    rh = got.shape[1]
    nb = rh // tile

    def kern(c_ref, a_ref, b_ref, o16_ref):
        o16_ref[...] = (a_ref[...] + b_ref[...]).astype(bf16)
    blk = (None, tile, D)
    grid_spec = pltpu.PrefetchScalarGridSpec(
        num_scalar_prefetch=1, grid=(4, nb),
        in_specs=[pl.BlockSpec(blk, lambda s, i, c: (s, c[0] * nb + i, 0)), pl.BlockSpec(blk, lambda s, i, c: (s, i, 0))],
        out_specs=pl.BlockSpec(blk, lambda s, i, c: (s, i, 0)))
    return pl.pallas_call(kern, name=name, grid_spec=grid_spec, out_shape=SDS(got.shape, bf16),
                          compiler_params=pltpu.CompilerParams(dimension_semantics=("parallel", "parallel")))(c_arr, packs, got)


def chip_sum(name, own16, landed16, k_arr, tile):
    rh = own16.shape[1]
    nb = rh // tile

    def kern(k_ref, own_ref, l0, l1, l2, l3, o_ref):
        k = k_ref[0]
        s = None
        for j, lref in enumerate((l0, l1, l2, l3)):
            t = jnp.where(k == j, own_ref[...], lref[...]).astype(f32)
            s = t if s is None else s + t
        o_ref[...] = s
    blk = (None, tile, D)
    land = [pl.BlockSpec(blk, lambda i, k, j=j: (jnp.where(k[0] == j, (j + 1) % N_CHIPS, j), i, 0)) for j in range(N_CHIPS)]
    grid_spec = pltpu.PrefetchScalarGridSpec(
        num_scalar_prefetch=1, grid=(nb,),
        in_specs=[pl.BlockSpec(blk, lambda i, k: (k[0], i, 0))] + land,
        out_specs=pl.BlockSpec((tile, D), lambda i, k: (i, 0)))
    return pl.pallas_call(kern, name=name, grid_spec=grid_spec, out_shape=SDS((rh, D), f32),
                          compiler_params=pltpu.CompilerParams(dimension_semantics=("parallel",)))(k_arr, own16, landed16, landed16, landed16, landed16)


def adamw(name, w, g, m, v):
    R, C = w.shape
    tr = _row_tile(R, 256)

    def kern(w_ref, g_ref, m_ref, v_ref, d_ref, mo_ref, vo_ref):
        gg = g_ref[...]
        mn = ADAM_B1 * m_ref[...] + (1.0 - ADAM_B1) * gg
        vn = ADAM_B2 * v_ref[...] + (1.0 - ADAM_B2) * jnp.square(gg)
        m_hat = mn / (1.0 - ADAM_B1 ** ADAM_STEP)
        v_hat = vn / (1.0 - ADAM_B2 ** ADAM_STEP)
        d_ref[...] = -ADAM_LR * (m_hat / (jnp.sqrt(v_hat) + ADAM_EPS) + ADAM_WD * w_ref[...])
        mo_ref[...] = mn
        vo_ref[...] = vn
    spec = pl.BlockSpec((tr, C), lambda i: (i, 0))
    s = SDS((R, C), f32)
    return pl.pallas_call(kern, name=name, grid=(R // tr,), in_specs=[spec] * 4, out_specs=[spec] * 3, out_shape=[s, s, s],
                          compiler_params=pltpu.CompilerParams(dimension_semantics=("parallel",)))(w, g, m, v)


WEIGHT_NAMES = ("norm_g", "w_in", "gm_ln_g", "gm_ln_b", "gm_ws", "gm_bs", "conv_w", "conv_b", "dt_bias", "a_log", "d_skip",
                "ssm_norm_g", "w_out", "pool_w", "pool_b", "pool_scale", "ffn_w_gate", "ffn_w_up", "ffn_w_down")
SMALL = ("norm_g", "conv_w", "pool_b", "pool_scale")
REPL = ("gm_ln_g", "gm_ln_b", "gm_ws", "gm_bs", "conv_b", "dt_bias", "a_log", "d_skip", "ssm_norm_g")
SMALL_AXIS = {"norm_g": 2, "conv_w": 1, "pool_b": 1, "pool_scale": 0}
N_CHIPS = 4
IN_SH = IN_DIM // N_CHIPS
SMALL_ROWS = 8
REPL_ROWS = 72
E_OUT, E_GATE, E_UP, E_DOWN = 0, 512, 512 + 2 * FF_SH, 512 + 4 * FF_SH
E_POOL = E_DOWN + 2 * FF_SH
E_ROWS, E_TILE = E_POOL + 64, 400
L_SMALL, L_REPL, L_IN = 0, SMALL_ROWS, SMALL_ROWS + REPL_ROWS
L_END = L_IN + IN_SH
L_ROWS, L_TILE = 1408, 352


def _flat_rows(pieces, rows):
    v = jnp.concatenate([p.reshape(-1) for p in pieces])
    return jnp.pad(v, (0, rows * D - v.shape[0])).reshape(rows, D)


def _shard_small(name, full, k):
    ax = SMALL_AXIS[name]
    n = full.shape[ax] // N_CHIPS
    return lax.slice_in_dim(full, k * n, (k + 1) * n, axis=ax)


def _drop1(name, a):
    return a if name == "norm_g" else a[0]


HBM_SPEC = pl.BlockSpec(memory_space=pltpu.HBM)
SEM_SPEC = pl.BlockSpec(memory_space=pltpu.SEMAPHORE)
SPLIT_EFFECT = pltpu.SideEffectType.DATAFLOW_SIDE_EFFECTING


def _chip_of(me, flip):
    return 2 * ((1 - me[0]) if flip[0] else me[0]) + ((1 - me[1]) if flip[1] else me[1])


def gather_start(name, arrs, after, slotted=False):
    n = len(arrs)
    ncp = n * len(CHIP_FLIPS)

    def body(*refs):
        srcs, lands = refs[:n], refs[n:2 * n]
        send_sems, recv_sems, token = refs[2 * n + 1], refs[2 * n + 2], refs[-1]
        me = (lax.axis_index("x"), lax.axis_index("y"), lax.axis_index("c"))
        k = 2 * me[0] + me[1]
        for a in range(n):
            for f, flip in enumerate(CHIP_FLIPS):
                peer = tuple((1 - m) if fl else m for m, fl in zip(me, flip))
                src = srcs[a].at[_chip_of(me, flip)] if slotted else srcs[a]
                for ix in _pieces(src, 0, 16):
                    pltpu.make_async_remote_copy(src_ref=src.at[ix], dst_ref=lands[a].at[k].at[ix],
                                                 send_sem=send_sems.at[a * len(CHIP_FLIPS) + f], recv_sem=recv_sems.at[a * len(CHIP_FLIPS) + f],
                                                 device_id=peer, device_id_type=MESH_ID).start()
        token[...] = jnp.zeros_like(token)

    land_shapes = [a.shape if slotted else (N_CHIPS,) + a.shape for a in arrs]
    operands = [pltpu.with_memory_space_constraint(a, pltpu.HBM) for a in arrs]
    operands += [pltpu.with_memory_space_constraint(lax.empty(s, a.dtype), pltpu.HBM) for s, a in zip(land_shapes, arrs)]
    out = pl.pallas_call(
        body, name=name,
        out_shape=(pltpu.SemaphoreType.DMA((ncp,)), pltpu.SemaphoreType.DMA((ncp,)), *[pltpu.HBM(a.shape, a.dtype) for a in arrs],
                   *[pltpu.HBM(s, a.dtype) for s, a in zip(land_shapes, arrs)], SDS((8, 128), f32)),
        in_specs=[HBM_SPEC] * (2 * n) + [ANY], out_specs=(SEM_SPEC, SEM_SPEC, *[HBM_SPEC] * (2 * n), pl.BlockSpec(memory_space=pltpu.VMEM)),
        input_output_aliases={i: 2 + i for i in range(2 * n)},
        compiler_params=pltpu.CompilerParams(has_side_effects=SPLIT_EFFECT),
    )(*operands, after)
    return out[0], out[1], out[2:2 + n], out[2 + n:2 + 2 * n], out[-1]


def gather_wait(name, send_sems, recv_sems, thru, lands, after, slotted=False):
    n = len(thru)

    def body(*refs):
        srcs, lands_r = refs[:n], refs[n:2 * n]
        s_sems, r_sems = refs[2 * n], refs[2 * n + 1]
        me = (lax.axis_index("x"), lax.axis_index("y"), lax.axis_index("c"))
        k = 2 * me[0] + me[1]
        for a in range(n):
            for f, flip in enumerate(CHIP_FLIPS):
                peer = tuple((1 - m) if fl else m for m, fl in zip(me, flip))
                idx = a * len(CHIP_FLIPS) + f
                src = srcs[a].at[_chip_of(me, flip)] if slotted else srcs[a]
                pltpu.make_async_remote_copy(src_ref=src, dst_ref=lands_r[a].at[k], send_sem=s_sems.at[idx], recv_sem=r_sems.at[idx],
                                             device_id=peer, device_id_type=MESH_ID).wait_send()
                pltpu.make_async_remote_copy(src_ref=src, dst_ref=lands_r[a].at[_chip_of(me, flip)], send_sem=s_sems.at[idx],
                                             recv_sem=r_sems.at[idx], device_id=peer, device_id_type=MESH_ID).wait_recv()

    out = pl.pallas_call(
        body, name=name, out_shape=tuple(pltpu.HBM(t.shape, t.dtype) for t in (*thru, *lands)),
        in_specs=[HBM_SPEC] * (2 * n) + [SEM_SPEC, SEM_SPEC, ANY], out_specs=tuple([HBM_SPEC] * (2 * n)),
        input_output_aliases={i: i for i in range(2 * n)},
        compiler_params=pltpu.CompilerParams(has_side_effects=SPLIT_EFFECT),
    )(*thru, *lands, send_sems, recv_sems, after)
    return out[:n], out[n:]


def pair_start(name, packs, rh):
    land_shape = (packs.shape[0], rh, packs.shape[2])

    def body(p_ref, land_ref, send_sem, recv_sem, p_thru, land_thru, token):
        me = (lax.axis_index("x"), lax.axis_index("y"), lax.axis_index("c"))
        theirs = p_ref.at[:, pl.ds(pl.multiple_of((1 - me[2]) * rh, 8), rh), :]
        for ix in _pieces(theirs, 1, 8):
            pltpu.make_async_remote_copy(src_ref=theirs.at[ix], dst_ref=land_ref.at[ix], send_sem=send_sem, recv_sem=recv_sem,
                                         device_id=(me[0], me[1], 1 - me[2]), device_id_type=MESH_ID).start()
        token[...] = jnp.zeros_like(token)

    return pl.pallas_call(
        body, name=name,
        out_shape=(pltpu.SemaphoreType.DMA(()), pltpu.SemaphoreType.DMA(()), pltpu.HBM(packs.shape, packs.dtype), pltpu.HBM(land_shape, packs.dtype),
                   SDS((8, 128), f32)),
        in_specs=[HBM_SPEC, HBM_SPEC], out_specs=(SEM_SPEC, SEM_SPEC, HBM_SPEC, HBM_SPEC, pl.BlockSpec(memory_space=pltpu.VMEM)),
        input_output_aliases={0: 2, 1: 3}, compiler_params=pltpu.CompilerParams(has_side_effects=SPLIT_EFFECT),
    )(pltpu.with_memory_space_constraint(packs, pltpu.HBM), pltpu.with_memory_space_constraint(lax.empty(land_shape, packs.dtype), pltpu.HBM))


def pair_wait(name, send_sem, recv_sem, packs, land, after):
    rh = land.shape[1]

    def body(p_ref, land_ref, s_sem, r_sem, after_ref, p_out, land_out):
        me = (lax.axis_index("x"), lax.axis_index("y"), lax.axis_index("c"))
        theirs = p_ref.at[:, pl.ds(pl.multiple_of((1 - me[2]) * rh, 8), rh), :]
        cp = pltpu.make_async_remote_copy(src_ref=theirs, dst_ref=land_ref, send_sem=s_sem, recv_sem=r_sem,
                                          device_id=(me[0], me[1], 1 - me[2]), device_id_type=MESH_ID)
        cp.wait_send()
        cp.wait_recv()

    return pl.pallas_call(
        body, name=name, out_shape=(pltpu.HBM(packs.shape, packs.dtype), pltpu.HBM(land.shape, land.dtype)),
        in_specs=[HBM_SPEC, HBM_SPEC, SEM_SPEC, SEM_SPEC, ANY], out_specs=(HBM_SPEC, HBM_SPEC), input_output_aliases={0: 0, 1: 1},
        compiler_params=pltpu.CompilerParams(has_side_effects=SPLIT_EFFECT),
    )(packs, land, send_sem, recv_sem, after)


def gather_weights(w_sh):
    big = [w_sh["w_in"][0], w_sh["w_out"][0], w_sh["pool_w"][0].reshape(4 * 64, POOL_GD)]
    small_pack = _flat_rows([w_sh[n] for n in SMALL], SMALL_ROWS)
    own = [b.astype(bf16) for b in big] + [small_pack]
    my_k = 2 * lax.axis_index("x") + lax.axis_index("y")
    s_in, s_out, s_pool, s_small = [lax.dynamic_update_slice(s, o[None], (my_k, 0, 0))
                                    for s, o in zip(gather_two_level("gather_weights", own[:3], own[3:]), own)]
    Wf = {n: w_sh[n][0] for n in REPL}
    Wf["w_in"] = s_in.transpose(1, 0, 2).reshape(D, IN_DIM)
    Wf["pool_w"] = s_pool.reshape(N_CHIPS, 4, 64, POOL_GD).transpose(1, 0, 2, 3).reshape(4, POOL_GD, POOL_GD)
    Wf["wo4"] = s_out
    small_shapes = [_drop1(n, w_sh[n]).shape for n in SMALL]
    parts = [_split_rows(s_small[k], small_shapes) for k in range(N_CHIPS)]
    for j, n in enumerate(SMALL):
        Wf[n] = jnp.concatenate([parts[k][j] for k in range(N_CHIPS)], axis=SMALL_AXIS[n])
    return Wf


def pack_early(G):
    slots = [jnp.concatenate([G["wo4"][k], G["wgT4"][0][k], G["wgT4"][1][k], G["wuT4"][0][k], G["wuT4"][1][k], G["wd4"][0][k], G["wd4"][1][k],
                              G["pool_w"][:, k * 64:(k + 1) * 64, :].reshape(64, D)], axis=0) for k in range(N_CHIPS)]
    return jnp.stack(slots)


def pack_late(G):
    sg = small_grads(G)
    repl = _flat_rows([sg[n] for n in REPL], REPL_ROWS)
    w_in_t = jnp.concatenate(G["w_inT"], axis=0)
    slots = [jnp.concatenate([_flat_rows([_shard_small(n, sg[n], k) for n in SMALL], SMALL_ROWS), repl,
                              jnp.pad(w_in_t[k * IN_SH:(k + 1) * IN_SH], ((0, L_ROWS - L_END), (0, 0)))], axis=0)
             for k in range(N_CHIPS)]
    return jnp.stack(slots)


def unpack_grads(early, late, w_sh):
    g = {"w_out": early[E_OUT:E_GATE], "ffn_w_down": early[E_DOWN:E_POOL], "pool_w": early[E_POOL:E_ROWS],
         "ffn_w_gate": jnp.stack([early[E_GATE + l * FF_SH:E_GATE + (l + 1) * FF_SH].T for l in range(2)]),
         "ffn_w_up": jnp.stack([early[E_UP + l * FF_SH:E_UP + (l + 1) * FF_SH].T for l in range(2)]),
         "w_in": late[L_IN:L_END].T}
    small = _split_rows(late[L_SMALL:L_REPL], [_drop1(n, w_sh[n]).shape for n in SMALL])
    repl = _split_rows(late[L_REPL:L_IN], [w_sh[n][0].shape for n in REPL])
    g.update(zip(SMALL, small))
    g.update(zip(REPL, repl))
    return {n: g[n].reshape(w_sh[n].shape) for n in WEIGHT_NAMES}


def _split_rows(flat2d, shapes):
    v = flat2d.reshape(-1)
    out, off = [], 0
    for s in shapes:
        n = math.prod(s)
        out.append(v[off:off + n].reshape(s))
        off += n
    return out


def kernel(x, norm_g, w_in, gm_ln_g, gm_ln_b, gm_ws, gm_bs, conv_w, conv_b, dt_bias, a_log, d_skip, ssm_norm_g, w_out, pool_w, pool_b, pool_scale, ffn_w_gate, ffn_w_up, ffn_w_down, loss_target, m_norm_g, m_w_in, m_gm_ln_g, m_gm_ln_b, m_gm_ws, m_gm_bs, m_conv_w, m_conv_b, m_dt_bias, m_a_log, m_d_skip, m_ssm_norm_g, m_w_out, m_pool_w, m_pool_b, m_pool_scale, m_ffn_w_gate, m_ffn_w_up, m_ffn_w_down, v_norm_g, v_w_in, v_gm_ln_g, v_gm_ln_b, v_gm_ws, v_gm_bs, v_conv_w, v_conv_b, v_dt_bias, v_a_log, v_d_skip, v_ssm_norm_g, v_w_out, v_pool_w, v_pool_b, v_pool_scale, v_ffn_w_gate, v_ffn_w_up, v_ffn_w_down):
    T = x.shape[1]
    w_sh = dict(zip(WEIGHT_NAMES, (norm_g, w_in, gm_ln_g, gm_ln_b, gm_ws, gm_bs, conv_w, conv_b, dt_bias, a_log, d_skip, ssm_norm_g, w_out,
                                   pool_w, pool_b, pool_scale, ffn_w_gate, ffn_w_up, ffn_w_down)))
    m_sh = dict(zip(WEIGHT_NAMES, (m_norm_g, m_w_in, m_gm_ln_g, m_gm_ln_b, m_gm_ws, m_gm_bs, m_conv_w, m_conv_b, m_dt_bias, m_a_log, m_d_skip,
                                   m_ssm_norm_g, m_w_out, m_pool_w, m_pool_b, m_pool_scale, m_ffn_w_gate, m_ffn_w_up, m_ffn_w_down)))
    v_sh = dict(zip(WEIGHT_NAMES, (v_norm_g, v_w_in, v_gm_ln_g, v_gm_ln_b, v_gm_ws, v_gm_bs, v_conv_w, v_conv_b, v_dt_bias, v_a_log, v_d_skip,
                                   v_ssm_norm_g, v_w_out, v_pool_w, v_pool_b, v_pool_scale, v_ffn_w_gate, v_ffn_w_up, v_ffn_w_down)))

    my_k = 2 * lax.axis_index("x") + lax.axis_index("y")
    ffn_own = [w_sh["ffn_w_gate"].reshape(2 * D, FF_SH).astype(bf16), w_sh["ffn_w_up"].reshape(2 * D, FF_SH).astype(bf16),
               w_sh["ffn_w_down"].reshape(2 * FF_SH, D).astype(bf16)]
    Wf = gather_weights(w_sh)
    send_sems, recv_sems, thru, lands, token = gather_start("gather_ffn_start", ffn_own, Wf["wo4"])
    Wf["norm_g"] = Wf["norm_g"] + token[0, 0]
    W = build_weights(Wf)

    def ffn_weights(after):
        _, landed = gather_wait("gather_ffn_wait", send_sems, recv_sems, thru, lands, after)
        return tuple(lax.dynamic_update_slice(l, o[None], (my_k, 0, 0)) for l, o in zip(landed, ffn_own))

    my_c = lax.axis_index("c")
    c_arr = my_c.astype(jnp.int32).reshape(1)
    k_arr = my_k.astype(jnp.int32).reshape(1)

    def pair_stage(tag, packs, tile):
        got = pair_split_exchange(f"grads{tag}_pair_split", packs, packs.shape[1] // 2)
        return pair_sum(f"grads{tag}_pair_sum", packs, got, c_arr, tile)

    def chip_stage(tag, pair16, landed, tile):
        half = chip_sum(f"grads{tag}_chip_sum", pair16, landed, k_arr, tile)
        other = pair_swap(f"grads{tag}_pair_swap", half)
        return jnp.concatenate([jnp.where(my_c == 0, half, other), jnp.where(my_c == 0, other, half)], axis=0)

    early = {}

    def early_grads(Ge):
        *pair, tok = pair_start("gradsE_pair_start", pack_early(Ge), E_ROWS // 2)
        early.update(pair=pair)
        return tok

    def early_grads_next(after):
        packs, got = pair_wait("gradsE_pair_wait", *early["pair"], after)
        pair16 = pair_sum("gradsE_pair_sum", packs, got, c_arr, E_TILE)
        s_sems, r_sems, thru, lands, tok = gather_start("gradsE_scatter_start", [pair16], jnp.zeros((8, 128), f32), slotted=True)
        early.update(s_sems=s_sems, r_sems=r_sems, thru=thru, lands=lands)
        return tok

    loss_acc, grad_x, G = local_step(T, x[0], loss_target[0], W, ffn_weights, early_grads, early_grads_next)
    pair_l = pair_stage("L", pack_late(G), L_TILE)
    total_l = chip_stage("L", pair_l, scatter_over_chips("gradsL_scatter", pair_l), L_TILE)
    (pair_e,), (landed_e,) = gather_wait("gradsE_scatter_wait", early["s_sems"], early["r_sems"], early["thru"], early["lands"], total_l,
                                         slotted=True)
    total_e = chip_stage("E", pair_e, landed_e, E_TILE)
    grads = unpack_grads(total_e, total_l, w_sh)

    delta, new_m, new_v = {}, {}, {}
    for n in WEIGHT_NAMES:
        shp = w_sh[n].shape
        two_d = (-1, shp[-1])
        d_, m_, v_ = adamw("adamw_" + n, w_sh[n].reshape(two_d), grads[n].reshape(two_d), m_sh[n].reshape(two_d), v_sh[n].reshape(two_d))
        delta[n], new_m[n], new_v[n] = d_.reshape(shp), m_.reshape(shp), v_.reshape(shp)

    loss = lax.psum(loss_acc[0, 0], ("x", "y", "c"))
    return (loss, grad_x[None], *[grads[n] for n in WEIGHT_NAMES], *[delta[n] for n in WEIGHT_NAMES],
            *[new_m[n] for n in WEIGHT_NAMES], *[new_v[n] for n in WEIGHT_NAMES])
```

```python
import math

import jax
import jax.numpy as jnp
from jax import lax
from jax.experimental import pallas as pl
from jax.experimental.pallas import tpu as pltpu

f32, bf16 = jnp.float32, jnp.bfloat16
SDS = jax.ShapeDtypeStruct

D = 1024
EPS = 1e-6
CHUNK = 128
GM_HEADS, GM_HD = 4, 256
SSM_GROUPS, SSM_HPG, SSM_P, SSM_N = 4, 4, 64, 128
N_HEADS = SSM_GROUPS * SSM_HPG
CONV_K = 4
CONV_DIM = 2048
POOL_WINDOWS = (2, 4, 8, 16)
POOL_GD = 256
POOL_HALO = 32
CONV_HALO = 16
D_FF = 2816
DT_PAD = 128
IN_DIM = 5136

ADAM_LR, ADAM_B1, ADAM_B2, ADAM_EPS, ADAM_WD, ADAM_STEP = 0.001, 0.9, 0.999, 1e-08, 0.01, 10

NT = (((1,), (1,)), ((), ()))
TN = (((0,), (0,)), ((), ()))
NN = (((1,), (0,)), ((), ()))
HI = lax.Precision.HIGHEST
MM_SUB = 256


def _silu(x):
    return x * jax.nn.sigmoid(x)


def _softplus(x):
    return jnp.maximum(x, 0.0) + jnp.log1p(jnp.exp(-jnp.abs(x)))


def _rms(x, g):
    return x * lax.rsqrt(jnp.mean(x * x, axis=-1, keepdims=True) + EPS) * g


def _rms_bwd(x, g, dy):
    r = lax.rsqrt(jnp.mean(x * x, axis=-1, keepdims=True) + EPS)
    xh = x * r
    dxh = dy * g
    dx = r * (dxh - xh * jnp.mean(dxh * xh, axis=-1, keepdims=True))
    return dx, jnp.sum(dy * xh, axis=0, keepdims=True)


def _bdot(a, b, dims=NN):
    return lax.dot_general(a.astype(bf16), b.astype(bf16), dims, preferred_element_type=f32)


def matmul(name, pairs, mode, out_dtype, tm, tn, tk=None):
    a0, b0 = pairs[0]
    if mode == "tn":
        M, N, K = a0.shape[1], b0.shape[1], a0.shape[0]
    else:
        M, K = a0.shape
        N = b0.shape[1] if mode == "nn" else b0.shape[0]
    tm, tn = min(tm, M), min(tn, N)
    assert M % tm == 0 and N % tn == 0, (name, M, N, tm, tn)
    if tk is None:
        nk = 1
    else:
        assert len(pairs) == 1 and K % tk == 0
        nk = K // tk
    dims = {"nn": NN, "nt": NT, "tn": TN}[mode]
    in_specs, args = [], []
    for a, b in pairs:
        kk = (a.shape[0] if mode == "tn" else a.shape[1]) if tk is None else tk
        if mode == "tn":
            in_specs.append(pl.BlockSpec((kk, tm), lambda j, i, k: (k, i)))
            in_specs.append(pl.BlockSpec((kk, tn), lambda j, i, k: (k, j)))
        elif mode == "nn":
            in_specs.append(pl.BlockSpec((tm, kk), lambda j, i, k: (i, k)))
            in_specs.append(pl.BlockSpec((kk, tn), lambda j, i, k: (k, j)))
        else:
            in_specs.append(pl.BlockSpec((tm, kk), lambda j, i, k: (i, k)))
            in_specs.append(pl.BlockSpec((tn, kk), lambda j, i, k: (j, k)))
        args += [a, b]
    npairs = len(pairs)

    def kern(*refs):
        o = refs[2 * npairs]
        part = None
        for p in range(npairs):
            d = _bdot(refs[2 * p][...], refs[2 * p + 1][...], dims)
            part = d if part is None else part + d
        if nk == 1:
            o[...] = part.astype(out_dtype)
        else:
            acc = refs[2 * npairs + 1]
            k = pl.program_id(2)

            @pl.when(k == 0)
            def _():
                acc[...] = part

            @pl.when(k > 0)
            def _():
                acc[...] += part

            @pl.when(k == nk - 1)
            def _():
                o[...] = acc[...].astype(out_dtype)

    return pl.pallas_call(
        kern, name=name, grid=(N // tn, M // tm, nk),
        in_specs=in_specs, out_specs=pl.BlockSpec((tm, tn), lambda j, i, k: (i, j)),
        out_shape=SDS((M, N), out_dtype),
        scratch_shapes=[pltpu.VMEM((tm, tn), f32)] if nk > 1 else [],
        compiler_params=pltpu.CompilerParams(dimension_semantics=("parallel", "parallel", "arbitrary")),
    )(*args)


def mm(name, grid, pairs, dims, o_spec, out_shape):
    nk = grid[2]
    npairs = len(pairs)
    in_specs, args = [], []
    for a, a_spec, b, b_spec in pairs:
        in_specs += [a_spec, b_spec]
        args += [a, b]
    blk = tuple(d for d in o_spec.block_shape if d is not None)

    def kern(*refs):
        o = refs[2 * npairs]
        part = None
        for p in range(npairs):
            d = _bdot(refs[2 * p][...], refs[2 * p + 1][...], dims)
            part = d if part is None else part + d
        if nk == 1:
            o[...] = part.astype(o.dtype)
        else:
            acc = refs[2 * npairs + 1]
            k = pl.program_id(2)

            @pl.when(k == 0)
            def _():
                acc[...] = part

            @pl.when(k > 0)
            def _():
                acc[...] += part

            @pl.when(k == nk - 1)
            def _():
                o[...] = acc[...].astype(o.dtype)

    return pl.pallas_call(
        kern, name=name, grid=grid, in_specs=in_specs, out_specs=o_spec, out_shape=out_shape,
        scratch_shapes=[pltpu.VMEM(blk, f32)] if nk > 1 else [],
        compiler_params=pltpu.CompilerParams(dimension_semantics=("parallel", "parallel", "arbitrary")),
    )(*args)


def mm_fused(name, n_row_blocks, pairs, dims, extra_ins, outs, accs, epilogue):
    npairs, nx, no, na = len(pairs), len(extra_ins), len(outs), len(accs)
    in_specs, args = [], []
    for a, a_spec, b, b_spec in pairs:
        in_specs += [a_spec, b_spec]
        args += [a, b]
    for arr, spec in extra_ins:
        in_specs.append(spec)
        args.append(arr)

    rows_blk = outs[0][1].block_shape[0]
    sub = min(rows_blk, MM_SUB)

    def kern(*refs):
        x_refs = refs[2 * npairs:2 * npairs + nx]
        o_refs = refs[2 * npairs + nx:2 * npairs + nx + no]
        a_refs = refs[2 * npairs + nx + no:]
        if na:
            @pl.when(pl.program_id(0) == 0)
            def _():
                for a in a_refs:
                    a[...] = jnp.zeros(a.shape, f32)
        for r0 in range(0, rows_blk, sub):
            rows = pl.ds(r0, sub)
            part = None
            for p in range(npairs):
                d = _bdot(refs[2 * p][rows, :], refs[2 * p + 1][...], dims)
                part = d if part is None else part + d
            epilogue(part, [x.at[rows, :] if x.shape[0] == rows_blk else x for x in x_refs], [o.at[rows, :] for o in o_refs], a_refs)

    return pl.pallas_call(
        kern, name=name, grid=(n_row_blocks,), in_specs=in_specs,
        out_specs=[spec for _, spec in outs] + [pl.BlockSpec(tuple(s), lambda i, nd=len(s): (0,) * nd) for s in accs],
        out_shape=[s for s, _ in outs] + [SDS(tuple(s), f32) for s in accs],
        compiler_params=pltpu.CompilerParams(dimension_semantics=("arbitrary",)),
    )(*args)


FF_SH = D_FF // 4


def ffn_up(name, T, tm, n_bf, wg4, wu4, l):
    sub = min(tm, MM_SUB)

    def kern(n_ref, wg_ref, wu_ref, g_ref, u_ref, a_ref):
        for r0 in range(0, tm, sub):
            rows = pl.ds(r0, sub)
            n = n_ref[rows, :]
            g = jnp.dot(n, wg_ref[...], preferred_element_type=f32)
            u = jnp.dot(n, wu_ref[...], preferred_element_type=f32)
            g_ref[rows, :] = g.astype(bf16)
            u_ref[rows, :] = u.astype(bf16)
            a_ref[rows, :] = (_silu(g) * u).astype(bf16)
    w_spec = pl.BlockSpec((None, D, FF_SH), lambda k, i: (k, l, 0))
    o_spec = pl.BlockSpec((None, tm, FF_SH), lambda k, i: (k, i, 0))
    s = SDS((4, T, FF_SH), bf16)
    return pl.pallas_call(kern, name=name, grid=(4, T // tm), in_specs=[pl.BlockSpec((tm, D), lambda k, i: (i, 0)), w_spec, w_spec],
                          out_specs=[o_spec] * 3, out_shape=[s, s, s],
                          compiler_params=pltpu.CompilerParams(dimension_semantics=("parallel", "parallel")))(n_bf, wg4, wu4)


def ffn_dgu(name, T, tm, d_f, wd4, gate4, up4, l):
    rc = 16

    sub = min(tm, MM_SUB)

    def kern(df_ref, wd_ref, g_ref, u_ref, dg_ref, du_ref, dact_ref):
        for s0 in range(0, tm, sub):
            dact_ref[pl.ds(s0, sub), :] = _bdot(df_ref[pl.ds(s0, sub), :], wd_ref[...], NT)
            for r0 in range(s0, s0 + sub, rc):
                rows = pl.ds(r0, rc)
                _, vjp = jax.vjp(lambda a, b: _silu(a) * b, g_ref[rows, :].astype(f32), u_ref[rows, :].astype(f32))
                dg, du = vjp(dact_ref[rows, :])
                dg_ref[rows, :] = dg.astype(bf16)
                du_ref[rows, :] = du.astype(bf16)
    a_spec = pl.BlockSpec((None, tm, FF_SH), lambda k, i: (k, i, 0))
    s = SDS((4, T, FF_SH), bf16)
    return pl.pallas_call(kern, name=name, grid=(4, T // tm),
                          in_specs=[pl.BlockSpec((tm, D), lambda k, i: (i, 0)), pl.BlockSpec((None, FF_SH, D), lambda k, i: (k, l, 0)), a_spec, a_spec],
                          out_specs=[a_spec] * 2, out_shape=[s, s], scratch_shapes=[pltpu.VMEM((tm, FF_SH), f32)],
                          compiler_params=pltpu.CompilerParams(dimension_semantics=("parallel", "parallel")))(d_f, wd4, gate4, up4)


def rowcall(name, body, T, tm, ins, outs, accs=(), scratch=(), reverse=False, sub=None):
    n = T // tm
    assert T % tm == 0

    def blk(i):
        return (n - 1 - i) if reverse else i

    in_specs, args = [], []
    for spec in ins:
        kind, arr = spec[0], spec[1]
        if kind == "row":
            _, _, w, cb = spec
            in_specs.append(pl.BlockSpec((tm, w), lambda i, cb=cb: (blk(i), cb)))
        elif kind == "prev":
            _, _, w, cb, h = spec
            r = tm // h
            in_specs.append(pl.BlockSpec((h, w), lambda i, cb=cb, r=r: (jnp.maximum(blk(i) * r - 1, 0), cb)))
        elif kind == "next":
            _, _, w, cb, h = spec
            r = tm // h
            in_specs.append(pl.BlockSpec((h, w), lambda i, cb=cb, r=r, h=h: (jnp.minimum((blk(i) + 1) * r, T // h - 1), cb)))
        else:
            nd = arr.ndim
            in_specs.append(pl.BlockSpec(arr.shape, lambda i, nd=nd: (0,) * nd))
        args.append(arr)
    out_shape = [SDS((T, w), dt) for w, dt in outs] + [SDS(tuple(s), f32) for s in accs]
    out_specs = [pl.BlockSpec((tm, w), lambda i: (blk(i), 0)) for w, _ in outs]
    out_specs += [pl.BlockSpec(tuple(s), lambda i, nd=len(s): (0,) * nd) for s in accs]
    ni, no, na = len(ins), len(outs), len(accs)

    def kern(*refs):
        i = pl.program_id(0)
        in_refs, out_refs = refs[:ni], refs[ni:ni + no]
        acc_refs, scr = refs[ni + no:ni + no + na], refs[ni + no + na:]
        if na:
            @pl.when(i == 0)
            def _():
                for a in acc_refs:
                    a[...] = jnp.zeros(a.shape, f32)
        if sub is None or sub >= tm:
            body(blk(i), n, in_refs, out_refs, acc_refs, scr)
        else:
            for r0 in range(0, tm, sub):
                rows = pl.ds(r0, sub)
                body(blk(i), n, [r.at[rows, :] if spec[0] == "row" else r for r, spec in zip(in_refs, ins)],
                     [o.at[rows, :] for o in out_refs], acc_refs, [s.at[rows, :] for s in scr])

    res = pl.pallas_call(
        kern, name=name, grid=(n,), in_specs=in_specs, out_specs=out_specs, out_shape=out_shape,
        scratch_shapes=list(scratch),
        compiler_params=pltpu.CompilerParams(dimension_semantics=("arbitrary",)),
    )(*args)
    return res


def rms_to_bf16(name, T, tm, x, g):
    def body(i, n, ins, outs, accs, scr):
        outs[0][...] = _rms(ins[0][...], ins[1][...]).astype(bf16)
    return rowcall(name, body, T, tm, [("row", x, D, 0), ("const", g)], [(D, bf16)], sub=64)[0]


def _layer_norm_parts(x):
    mu = jnp.mean(x, axis=-1, keepdims=True)
    xc = x - mu
    r = lax.rsqrt(jnp.mean(xc * xc, axis=-1, keepdims=True) + EPS)
    return xc * r, r


def _gelu_and_slope(x):
    c, a = math.sqrt(2.0 / math.pi), 0.044715
    x2 = x * x
    t = jnp.tanh(c * (x + a * (x * x2)))
    cdf = 0.5 * (1.0 + t)
    slope = cdf + (0.5 * x) * (1.0 - t * t) * (c * (1.0 + (3.0 * a) * x2))
    return x * cdf, slope


def gmlp_fwd(name, T, tm, uvz, ln_g, ln_b, wm, bs):
    def body(i, n, ins, outs, accs, scr):
        gu = _gelu_and_slope(ins[0][...].astype(f32))[0]
        xh, _ = _layer_norm_parts(_gelu_and_slope(ins[1][...].astype(f32))[0])
        vln = (xh * ins[2][...] + ins[3][...]).astype(bf16)
        for c in range(ins[0].shape[0] // CHUNK):
            rows = slice(c * CHUNK, (c + 1) * CHUNK)
            for h in range(GM_HEADS):
                cols = slice(h * GM_HD, (h + 1) * GM_HD)
                mixed = jnp.dot(ins[4][h], vln[rows, cols], preferred_element_type=f32) + ins[5][h]
                outs[0][rows, cols] = (gu[rows, cols] * mixed).astype(bf16)
    return rowcall(name, body, T, tm, [("row", uvz, D, 0), ("row", uvz, D, 1), ("const", ln_g), ("const", ln_b), ("const", wm), ("const", bs)],
                   [(D, bf16)], sub=CHUNK)[0]


def gmlp_bwd(name, T, tm, uvz, d_ya, d_cb, ln_g, ln_b, wm, bs):
    def body(i, n, ins, outs, accs, scr):
        u, v, dya = ins[0][...].astype(f32), ins[1][...].astype(f32), ins[2][...]
        gu, slope_u = _gelu_and_slope(u)
        gv, slope_v = _gelu_and_slope(v)
        xh, r = _layer_norm_parts(gv)
        lng = ins[3][...]
        vln = (xh * lng + ins[4][...]).astype(bf16)
        rr = lax.broadcasted_iota(jnp.int32, (CHUNK, CHUNK), 0)
        cc = lax.broadcasted_iota(jnp.int32, (CHUNK, CHUNK), 1)
        causal = (rr >= cc).astype(f32)
        dvln_ref = scr[0]
        dgu_ref = scr[1]
        for c in range(ins[0].shape[0] // CHUNK):
            rows = slice(c * CHUNK, (c + 1) * CHUNK)
            for h in range(GM_HEADS):
                cols = slice(h * GM_HD, (h + 1) * GM_HD)
                w = ins[5][h]
                blk = vln[rows, cols]
                mixed = jnp.dot(w, blk, preferred_element_type=f32) + ins[6][h]
                dy = dya[rows, cols]
                dgu_ref[rows, cols] = dy * mixed
                dm = dy * gu[rows, cols]
                accs[3][h] += jnp.sum(dm, axis=1, keepdims=True)
                accs[2][h] += _bdot(dm, blk, NT) * causal
                dvln_ref[rows, cols] = _bdot(w, dm, TN)
        dvln = dvln_ref[...]
        accs[0][...] += jnp.sum(dvln * xh, axis=0, keepdims=True)
        accs[1][...] += jnp.sum(dvln, axis=0, keepdims=True)
        dxh = dvln * lng
        dgv = r * (dxh - jnp.mean(dxh, axis=-1, keepdims=True) - xh * jnp.mean(dxh * xh, axis=-1, keepdims=True))
        outs[0][...] = (dgu_ref[...] * slope_u).astype(bf16)
        outs[1][...] = (dgv * slope_v).astype(bf16)
    return rowcall(name, body, T, tm,
                   [("row", uvz, D, 0), ("row", uvz, D, 1), ("row", d_ya, D, d_cb), ("const", ln_g), ("const", ln_b), ("const", wm), ("const", bs)],
                   [(D, bf16), (D, bf16)], accs=[(1, D), (1, D), (GM_HEADS, CHUNK, CHUNK), (GM_HEADS, CHUNK, 1)],
                   scratch=[pltpu.VMEM((tm, D), f32), pltpu.VMEM((tm, D), f32)], sub=CHUNK)


CONV_RC, CONV_LB = 64, 256


def _conv_fill(i, x_ref, halo_ref, scr, tm):
    scr[pl.ds(0, CONV_HALO), :] = jnp.where(i > 0, halo_ref[...].astype(f32), 0.0)
    scr[pl.ds(CONV_HALO, tm), :] = x_ref[...].astype(f32)


def _conv_taps(scr, r0, lanes):
    return [scr[pl.ds(r0 + CONV_HALO - (CONV_K - 1) + k, CONV_RC), lanes] for k in range(CONV_K)]


def conv_fwd(name, T, tm, xbc, conv_w, conv_b):
    def body(i, n, ins, outs, accs, scr):
        s = scr[0]
        _conv_fill(i, ins[0], ins[1], s, tm)
        for lb in range(CONV_DIM // CONV_LB):
            lanes = slice(lb * CONV_LB, (lb + 1) * CONV_LB)
            w, b = ins[2][:, lanes], ins[3][:, lanes]

            for r0 in range(0, tm, CONV_RC):
                taps = _conv_taps(s, r0, lanes)
                pre = b + sum(w[k:k + 1] * taps[k] for k in range(CONV_K))
                outs[0][pl.ds(r0, CONV_RC), lanes] = _silu(pre)
    return rowcall(name, body, T, tm, [("row", xbc, CONV_DIM, 0), ("prev", xbc, CONV_DIM, 0, CONV_HALO), ("const", conv_w), ("const", conv_b)],
                   [(CONV_DIM, f32)], scratch=[pltpu.VMEM((tm + CONV_HALO, CONV_DIM), f32)])[0]


def conv_bwd_pre(name, T, tm, xbc, d_xc, conv_w, conv_b):
    def body(i, n, ins, outs, accs, scr):
        s = scr[0]
        _conv_fill(i, ins[0], ins[1], s, tm)
        fold = lambda v: jnp.sum(v.reshape(CONV_RC // 8, 8, CONV_LB), axis=0)
        for lb in range(CONV_DIM // CONV_LB):
            lanes = slice(lb * CONV_LB, (lb + 1) * CONV_LB)
            w, b = ins[3][:, lanes], ins[4][:, lanes]

            sums = [jnp.zeros((8, CONV_LB), f32)] * (CONV_K + 1)
            for r0 in range(0, tm, CONV_RC):
                taps = _conv_taps(s, r0, lanes)
                pre = b + sum(w[k:k + 1] * taps[k] for k in range(CONV_K))
                _, vjp = jax.vjp(_silu, pre)
                dpre = vjp(ins[2][pl.ds(r0, CONV_RC), lanes])[0]
                outs[0][pl.ds(r0, CONV_RC), lanes] = dpre
                sums = [sums[k] + fold(dpre * taps[k]) for k in range(CONV_K)] + [sums[CONV_K] + fold(dpre)]
            for k in range(CONV_K):
                accs[0][pl.ds(k, 1), lanes] += jnp.sum(sums[k], axis=0, keepdims=True)
            accs[1][:, lanes] += jnp.sum(sums[CONV_K], axis=0, keepdims=True)
    return rowcall(name, body, T, tm,
                   [("row", xbc, CONV_DIM, 0), ("prev", xbc, CONV_DIM, 0, CONV_HALO), ("row", d_xc, CONV_DIM, 0), ("const", conv_w), ("const", conv_b)],
                   [(CONV_DIM, f32)], accs=[(CONV_K, CONV_DIM), (1, CONV_DIM)], scratch=[pltpu.VMEM((tm + CONV_HALO, CONV_DIM), f32)])


def conv_bwd_x(name, T, tm, d_pre, conv_w):
    def body(i, n, ins, outs, accs, scr):
        s = scr[0]
        s[pl.ds(0, tm), :] = ins[0][...]
        s[pl.ds(tm, CONV_HALO), :] = jnp.where(i < n - 1, ins[1][...], 0.0)
        for lb in range(CONV_DIM // CONV_LB):
            lanes = slice(lb * CONV_LB, (lb + 1) * CONV_LB)
            w = ins[2][:, lanes]

            for r0 in range(0, tm, CONV_RC):
                dx = sum(w[k:k + 1] * s[pl.ds(r0 + CONV_K - 1 - k, CONV_RC), lanes] for k in range(CONV_K))
                outs[0][pl.ds(r0, CONV_RC), lanes] = dx.astype(bf16)
    return rowcall(name, body, T, tm, [("row", d_pre, CONV_DIM, 0), ("next", d_pre, CONV_DIM, 0, CONV_HALO), ("const", conv_w)],
                   [(CONV_DIM, bf16)], scratch=[pltpu.VMEM((tm + CONV_HALO, CONV_DIM), f32)])[0]


def _ssd_prep(dtr, dtb, alog):
    rr = lax.broadcasted_iota(jnp.int32, (CHUNK, CHUNK), 0)
    cc = lax.broadcasted_iota(jnp.int32, (CHUNK, CHUNK), 1)
    dt = _softplus(dtr + dtb)
    dA = dt * -jnp.exp(alog)
    acum = jnp.dot((rr >= cc).astype(f32), dA, precision=HI, preferred_element_type=f32)
    return dt, acum, acum.T, jnp.sum(dA, axis=0, keepdims=True)


def _ssd_group(g, x, Bm, Cm, S, dt, acum, acumT, tot, dsk):
    rr = lax.broadcasted_iota(jnp.int32, (CHUNK, CHUNK), 0)
    cc = lax.broadcasted_iota(jnp.int32, (CHUNK, CHUNK), 1)
    tril = rr >= cc
    lane = lax.broadcasted_iota(jnp.int32, (1, DT_PAD), 1)
    sub = lax.broadcasted_iota(jnp.int32, (DT_PAD, 1), 0)
    glane = lax.broadcasted_iota(jnp.int32, (1, SSM_HPG * SSM_P), 1) // SSM_P
    hm = [(glane == r).astype(f32) for r in range(SSM_HPG)]
    pick = lambda v, r: jnp.sum(v * (lane == SSM_HPG * g + r).astype(f32), axis=1, keepdims=True)
    cols = [pick(acum, r) for r in range(SSM_HPG)]
    tots = [pick(tot, r) for r in range(SSM_HPG)]
    spread = lambda vals: sum(vals[r] * hm[r] for r in range(SSM_HPG))
    xdt = x * spread([pick(dt, r) for r in range(SSM_HPG)])
    cb = _bdot(Cm, Bm, NT)
    y = x * spread([pick(dsk, r) for r in range(SSM_HPG)])
    for r in range(SSM_HPG):
        row = jnp.sum(acumT * (sub == SSM_HPG * g + r).astype(f32), axis=0, keepdims=True)
        dec = jnp.exp(jnp.where(tril, cols[r] - row, -jnp.inf))
        y = y + _bdot(cb * dec, xdt * hm[r])
    y = y + _bdot(Cm, S) * spread([jnp.exp(c) for c in cols])
    dte = spread([jnp.exp(tots[r] - cols[r]) for r in range(SSM_HPG)])
    s_new = S * spread([jnp.exp(t) for t in tots]) + _bdot(Bm, xdt * dte, TN)
    return y, s_new


def _ssd_ins(xc, dtr):
    gw = SSM_HPG * SSM_P
    ins = [("row", xc, gw, g) for g in range(SSM_GROUPS)]
    ins += [("row", xc, SSM_N, D // SSM_N + g) for g in range(SSM_GROUPS)]
    ins += [("row", xc, SSM_N, D // SSM_N + SSM_GROUPS + g) for g in range(SSM_GROUPS)]
    ins += [("row", dtr, DT_PAD, 0)]
    return ins


SSD_CPS = 4


def ssd_fwd(name, T, xc, dtr, dtb, alog, dsk):
    gw = SSM_HPG * SSM_P
    cps = min(SSD_CPS, T // CHUNK)

    def body(i, n, ins, outs, accs, scr):
        S = scr[0]

        @pl.when(i == 0)
        def _():
            S[...] = jnp.zeros(S.shape, f32)
        S4 = tuple(S[:, g * gw:(g + 1) * gw] for g in range(4))
        for c in range(cps):
            rows = pl.ds(c * CHUNK, CHUNK)
            X4 = tuple(ins[g][rows, :] for g in range(4))
            B4 = tuple(ins[4 + g][rows, :] for g in range(4))
            C4 = tuple(ins[8 + g][rows, :] for g in range(4))
            prep = _ssd_prep(ins[12][rows, :], ins[13][...], ins[14][...])
            nxt = []
            for g in range(4):
                outs[1][rows, g * gw:(g + 1) * gw] = S4[g]
                y, s_new = _ssd_group(g, X4[g], B4[g], C4[g], S4[g], *prep, ins[15][...])
                outs[0][rows, g * gw:(g + 1) * gw] = y
                nxt.append(s_new)
            S4 = tuple(nxt)
        for g in range(4):
            S[:, g * gw:(g + 1) * gw] = S4[g]
    ins = _ssd_ins(xc, dtr) + [("const", dtb), ("const", alog), ("const", dsk)]
    return rowcall(name, body, T, cps * CHUNK, ins, [(D, f32), (D, f32)], scratch=[pltpu.VMEM((SSM_N, D), f32)])


def ssd_bwd(name, T, xc, dtr, sprev, d_y, dtb, alog, dsk):
    gw = SSM_HPG * SSM_P

    def body(i, n, ins, outs, accs, scr):
        dS = scr[0]

        @pl.when(i == n - 1)
        def _():
            dS[...] = jnp.zeros(dS.shape, f32)
        dS4 = tuple(dS[:, g * gw:(g + 1) * gw] for g in range(4))
        def chunk(X4, dtr_c, B4, C4, S4, dtb_c, alog_c, dsk_c):
            prep = _ssd_prep(dtr_c, dtb_c, alog_c)
            res = [_ssd_group(g, X4[g], B4[g], C4[g], S4[g], *prep, dsk_c) for g in range(4)]
            return tuple(r[0] for r in res), tuple(r[1] for r in res)
        X4 = tuple(ins[g][...] for g in range(4))
        B4 = tuple(ins[4 + g][...] for g in range(4))
        C4 = tuple(ins[8 + g][...] for g in range(4))
        S4 = tuple(ins[13 + g][...] for g in range(4))
        dY4 = tuple(ins[17 + g][...] for g in range(4))
        _, vjp = jax.vjp(chunk, X4, ins[12][...], B4, C4, S4, ins[21][...], ins[22][...], ins[23][...])
        dX4, ddtr, dB4, dC4, dS4, ddtb, dalog, ddsk = vjp((dY4, dS4))
        for g in range(4):
            outs[0][:, g * gw:(g + 1) * gw] = dX4[g]
            outs[0][:, D + g * SSM_N:D + (g + 1) * SSM_N] = dB4[g]
            outs[0][:, D + (SSM_GROUPS + g) * SSM_N:D + (SSM_GROUPS + g + 1) * SSM_N] = dC4[g]
            dS[:, g * gw:(g + 1) * gw] = dS4[g]
        outs[1][...] = ddtr.astype(bf16)
        accs[0][...] += ddtb
        accs[1][...] += dalog
        accs[2][...] += ddsk
    ins = _ssd_ins(xc, dtr) + [("row", sprev, gw, g) for g in range(4)] + [("row", d_y, gw, g) for g in range(4)]
    ins += [("const", dtb), ("const", alog), ("const", dsk)]
    return rowcall(name, body, T, CHUNK, ins, [(CONV_DIM, f32), (DT_PAD, bf16)], accs=[(1, DT_PAD)] * 3,
                   scratch=[pltpu.VMEM((SSM_N, D), f32)], reverse=True)


def _gate_group(y, z, g):
    return _rms(y * _silu(z), g)


def gate_fwd(name, T, tm, y, uvz, gn):
    def body(i, n, ins, outs, accs, scr):
        for g in range(SSM_GROUPS):
            cols = slice(g * 256, (g + 1) * 256)
            outs[0][:, cols] = _gate_group(ins[0][:, cols], ins[1][:, cols].astype(f32), ins[2][:, cols]).astype(bf16)
    return rowcall(name, body, T, tm, [("row", y, D, 0), ("row", uvz, D, 2), ("const", gn)], [(D, bf16)], sub=64)[0]


def gate_bwd(name, T, tm, y, uvz, d_yb, d_cb, gn):
    def body(i, n, ins, outs, accs, scr):
        for g in range(SSM_GROUPS):
            cols = slice(g * 256, (g + 1) * 256)
            _, vjp = jax.vjp(_gate_group, ins[0][:, cols], ins[1][:, cols].astype(f32), ins[3][:, cols])
            dy, dz, dg = vjp(ins[2][:, cols])
            outs[0][:, cols] = dy
            outs[1][:, cols] = dz.astype(bf16)
            accs[0][:, cols] += dg
    return rowcall(name, body, T, tm, [("row", y, D, 0), ("row", uvz, D, 2), ("row", d_yb, D, d_cb), ("const", gn)],
                   [(D, f32), (D, bf16)], accs=[(1, D)], sub=64)


def _window_sum(src, cols, levels, tm, lv, trailing):
    cur, cur_cols = src, cols
    for l in range(1, levels + 1):
        shift = 2 ** (l - 1)
        last = l == levels
        if trailing:
            start = POOL_HALO if last else 8 * l
            rows = tm if last else tm + POOL_HALO - start
            new = cur[pl.ds(start, rows), cur_cols] + cur[pl.ds(start - shift, rows), cur_cols]
        else:
            start = 0
            rows = tm if last else tm + POOL_HALO - 8 * l
            new = cur[pl.ds(0, rows), cur_cols] + cur[pl.ds(shift, rows), cur_cols]
        if last:
            return new
        nxt = lv[l % 2]
        nxt[pl.ds(start, rows), :] = new
        cur, cur_cols = nxt, slice(None)


def _pool_diff(i, tm, h_ref, halo_ref, g_ref, scr, lv):
    g = g_ref[...]
    yn = _rms(h_ref[...], g)
    scr[pl.ds(0, POOL_HALO), :] = jnp.where(i > 0, _rms(halo_ref[...], g), 0.0)
    scr[pl.ds(POOL_HALO, tm), :] = yn
    pos = (i * tm + lax.broadcasted_iota(jnp.int32, (tm, 1), 0) + 1).astype(f32)
    parts = []
    for gi, win in enumerate(POOL_WINDOWS):
        cols = slice(gi * POOL_GD, (gi + 1) * POOL_GD)
        s = _window_sum(scr, cols, gi + 1, tm, lv, True)
        parts.append(s * (1.0 / jnp.minimum(pos, float(win))) - yn[:, cols])
    return parts


def pool_fwd(name, T, tm, h2, g_pre, pw, pb, psc, g_post, g_next):
    def body(i, n, ins, outs, accs, scr):
        parts = _pool_diff(i, tm, ins[0], ins[1], ins[2], scr[0], scr[1:3])
        for gi in range(len(POOL_WINDOWS)):
            cols = slice(gi * POOL_GD, (gi + 1) * POOL_GD)
            o = _bdot(parts[gi], ins[3][gi]) + ins[4][:, cols]
            outs[0][:, cols] = o * ins[5][:, cols]
        h = ins[0][...] + _rms(outs[0][...], ins[6][...])
        outs[1][...] = h
        outs[2][...] = _rms(h, ins[7][...]).astype(bf16)
    return rowcall(name, body, T, tm, [("row", h2, D, 0), ("prev", h2, D, 0, POOL_HALO), ("const", g_pre), ("const", pw), ("const", pb), ("const", psc),
                                       ("const", g_post), ("const", g_next)],
                   [(D, f32), (D, f32), (D, bf16)], scratch=[pltpu.VMEM((tm + POOL_HALO, D), f32)] + [pltpu.VMEM((tm + POOL_HALO, POOL_GD), f32)] * 2)


def pool_bwd(name, T, tm, h2, d_pm, d_res, g_pre, pw, pb, psc, f_prev, g_prev):
    def body(i, n, ins, outs, accs, scr):
        parts = _pool_diff(i, tm, ins[0], ins[1], ins[5], scr[0], scr[3:5])
        dpm = ins[2][...]
        psc_v = ins[8][...]
        dps = dpm * psc_v
        dps_halo = jnp.where(i < n - 1, ins[3][...] * psc_v, 0.0)
        accs[1][...] += jnp.sum(dps, axis=0, keepdims=True)
        pos = (i * tm + lax.broadcasted_iota(jnp.int32, (tm, 1), 0) + 1).astype(f32)
        pos_h = ((i + 1) * tm + lax.broadcasted_iota(jnp.int32, (POOL_HALO, 1), 0) + 1).astype(f32)
        r_scr = scr[1]
        dyn_scr = scr[2]
        for gi, win in enumerate(POOL_WINDOWS):
            cols = slice(gi * POOL_GD, (gi + 1) * POOL_GD)
            w = ins[6][gi]
            o = _bdot(parts[gi], w) + ins[7][:, cols]
            accs[2][:, cols] += jnp.sum(dpm[:, cols] * o, axis=0, keepdims=True)
            accs[0][gi] += _bdot(parts[gi], dps[:, cols], TN)
            q = _bdot(dps[:, cols], w, NT)
            qh = _bdot(dps_halo[:, cols], w, NT)
            r_scr[pl.ds(0, tm), cols] = q * (1.0 / jnp.minimum(pos, float(win)))
            r_scr[pl.ds(tm, POOL_HALO), cols] = qh * (1.0 / jnp.minimum(pos_h, float(win)))
            dyn_scr[:, cols] = _window_sum(r_scr, cols, gi + 1, tm, scr[3:5], False) - q
        dx, dg = _rms_bwd(ins[0][...], ins[5][...], dyn_scr[...])
        dh = ins[4][...] + dx
        outs[0][...] = dh
        accs[3][...] += dg
        df, dgp = _rms_bwd(ins[9][...], ins[10][...], dh)
        outs[1][...] = df.astype(bf16)
        accs[4][...] += dgp
    ins = [("row", h2, D, 0), ("prev", h2, D, 0, POOL_HALO), ("row", d_pm, D, 0), ("next", d_pm, D, 0, POOL_HALO), ("row", d_res, D, 0),
           ("const", g_pre), ("const", pw), ("const", pb), ("const", psc), ("row", f_prev, D, 0), ("const", g_prev)]
    return rowcall(name, body, T, tm, ins, [(D, f32), (D, bf16)], accs=[(4, POOL_GD, POOL_GD), (1, D), (1, D), (1, D), (1, D)],
                   scratch=[pltpu.VMEM((tm + POOL_HALO, D), f32), pltpu.VMEM((tm + POOL_HALO, D), f32), pltpu.VMEM((tm, D), f32)]
                   + [pltpu.VMEM((tm + POOL_HALO, POOL_GD), f32)] * 2)


def local_step(T, x, tgt, W, ffn_weights, early_grads, early_grads_next):
    tm = 512 if T >= 1024 else T // 2
    TKW = 4096 if T >= 4096 else T
    ng = W["norm_g"]
    g = lambda l, j: ng[l, j][None, :]
    G = {}

    tf = tm
    once = pl.Buffered(1)
    vec_f = pl.BlockSpec((1, D), lambda i: (0, 0))

    def fused_specs(t):
        rows = pl.BlockSpec((t, D), lambda i: (i, 0))
        return rows, [pl.BlockSpec((None, t, FF_SH), lambda i, s=s: (s, i, 0)) for s in range(4)], (SDS((T, D), f32), rows), (SDS((T, D), bf16), rows)
    rows_f, sh_f, out_f32, out_bf16 = fused_specs(tf)
    tf2 = min(T, 2 * tm)
    rows_f2, sh_f2, out2_f32, out2_bf16 = fused_specs(tf2)

    def resid_epilogue(with_pre):
        def ep(part, xs, os, accs):
            h = xs[0][...] + _rms(part, xs[1][...])
            os[0][...] = part
            os[1][...] = h
            if with_pre:
                os[2][...] = _rms(h, xs[2][...]).astype(bf16)
        return ep

    def bwd_epilogue(df_dtype):
        def ep(part, xs, os, accs):
            dx, dgp = _rms_bwd(xs[0][...], xs[3][...], part)
            dh = xs[2][...] + dx
            df, dgq = _rms_bwd(xs[1][...], xs[4][...], dh)
            os[0][...] = dh
            os[1][...] = df.astype(df_dtype)
            accs[0][...] += dgp
            accs[1][...] += dgq
        return ep

    def loss_epilogue(part, xs, os, accs):
        g_post = xs[2][...]
        e = xs[0][...] + _rms(part, g_post) - xs[1][...]
        accs[0][...] += jnp.sum(jnp.sum(e * e, axis=-1, keepdims=True) * (0.5 / D), axis=0, keepdims=True)
        dh = e * (1.0 / D)
        df, dg = _rms_bwd(part, g_post, dh)
        os[0][...] = dh
        os[1][...] = df.astype(bf16)
        accs[1][...] += dg

    def ffn_fwd(tag, n_bf, l, resid=None, loss=None):
        gate4, up4, act4 = ffn_up(f"ffn{tag}_up", T, min(T, 4 * tm), n_bf, W["wg4"], W["wu4"], l)
        wd_f = [pl.BlockSpec((None, FF_SH, D), lambda i, s=s: (s, l, 0), pipeline_mode=once) for s in range(4)]
        pairs = [(act4, sh_f2[s], W["wd4"], wd_f[s]) for s in range(4)]
        if loss is not None:
            return (gate4, up4, act4) + tuple(mm_fused(f"ffn{tag}_down", T // tf2, pairs, NN, [(loss[0], rows_f2), (loss[1], rows_f2), (loss[2], vec_f)],
                                                       [out2_f32, out2_bf16], [(1, 1), (1, D)], loss_epilogue))
        f, h_out = mm_fused(f"ffn{tag}_down", T // tf2, pairs, NN, [(resid[0], rows_f2), (resid[1], vec_f)], [out2_f32, out2_f32], [],
                            resid_epilogue(False))
        return gate4, up4, act4, f, h_out

    def ffn_bwd(tag, l, n_bf, gate4, up4, act4, d_f, h_out, f_pre, d_res, g_pre, g_post, df_dtype):
        d_gate4, d_up4 = ffn_dgu(f"ffn{tag}_dgu", T, min(T, 4 * tm), d_f, W["wd4"], gate4, up4, l)
        w_f = [pl.BlockSpec((None, D, FF_SH), lambda i, s=s: (s, l, 0), pipeline_mode=once) for s in range(4)]
        d_h, d_fp, dgp, dgq = mm_fused(
            f"ffn{tag}_dn", T // tf, [(d_gate4, sh_f[s], W["wg4"], w_f[s]) for s in range(4)] + [(d_up4, sh_f[s], W["wu4"], w_f[s]) for s in range(4)],
            NT, [(h_out, rows_f), (f_pre, rows_f), (d_res, rows_f), (g_pre, vec_f), (g_post, vec_f)],
            [out_f32, (SDS((T, D), df_dtype), rows_f)], [(1, D), (1, D)], bwd_epilogue(df_dtype))

        def wgrad(nm, a4, b):
            return mm(nm, (4, 1, T // TKW),
                      [(a4, pl.BlockSpec((None, TKW, FF_SH), lambda s, j, k: (s, k, 0)), b, pl.BlockSpec((TKW, D), lambda s, j, k: (k, 0)))],
                      TN, pl.BlockSpec((None, FF_SH, D), lambda s, j, k: (s, 0, 0)), SDS((4, FF_SH, D), f32))
        return d_h, d_fp, dgp, dgq, wgrad(f"ffn{tag}_dwg", d_gate4, n_bf), wgrad(f"ffn{tag}_dwu", d_up4, n_bf), wgrad(f"ffn{tag}_dwd", act4, d_f)

    y0 = rms_to_bf16("l0_prenorm", T, tf2, x, g(0, 0))
    uvz = matmul("in_uvz", [(y0, W["w_uvz"])], "nn", bf16, 4 * tm, 1024)
    xbc = matmul("in_xbc", [(y0, W["w_xbc"])], "nn", bf16, 4 * tm, 1024)
    dtr = matmul("in_dt", [(y0, W["w_dt"])], "nn", f32, 4 * tm, DT_PAD)
    y_a = gmlp_fwd("gmlp_fwd", T, tf2, uvz, W["ln_g"], W["ln_b"], W["wm"], W["bs"])
    xc = conv_fwd("conv_fwd", T, tm, xbc, W["conv_w"], W["conv_b"])
    y_ssd, sprev = ssd_fwd("ssd_fwd", T, xc, dtr, W["dtb"], W["alog"], W["dsk"])
    y_b = gate_fwd("gate_fwd", T, tf2, y_ssd, uvz, W["gn"])
    half = D // 2
    wo4 = W["wo4"]
    ycol = [pl.BlockSpec((tf2, half), lambda i, cb=cb: (i, cb)) for cb in range(2)]
    wo_s = [pl.BlockSpec((None, half, D), lambda i, s=s: (s, 0, 0), pipeline_mode=once) for s in range(4)]
    mixo, h1, n1 = mm_fused("out_proj", T // tf2, [(y_a, ycol[0], wo4, wo_s[0]), (y_a, ycol[1], wo4, wo_s[1]),
                                                  (y_b, ycol[0], wo4, wo_s[2]), (y_b, ycol[1], wo4, wo_s[3])], NN,
                            [(x, rows_f2), (g(0, 1), vec_f), (g(0, 2), vec_f)], [out2_f32, out2_f32, out2_bf16], [], resid_epilogue(True))
    W = dict(W)
    W["wg4"], W["wu4"], W["wd4"] = ffn_weights(h1)
    gate0, up0, act0, f1, h2 = ffn_fwd("0", n1, 0, resid=(h1, g(0, 3)))
    pm, h3, n3 = pool_fwd("pool_fwd", T, tm, h2, g(1, 0), W["pool_w"], W["pool_b"], W["pool_scale"], g(1, 1), g(1, 2))
    gate1, up1, act1, dh4, d_f2, loss_acc, dg13 = ffn_fwd("1", n3, 1, loss=(h3, tgt, g(1, 3)))
    d_h3, d_pm, dg12, dg11, dwg1, dwu1, dwd1 = ffn_bwd("1", 1, n3, gate1, up1, act1, d_f2, h3, pm, dh4, g(1, 2), g(1, 1), f32)
    d_h2, d_f1, G["pool_w"], G["pool_b"], G["pool_scale"], dg10, dg03 = pool_bwd("pool_bwd", T, tm, h2, d_pm, d_h3, g(1, 0), W["pool_w"], W["pool_b"],
                                                                                 W["pool_scale"], f1, g(0, 3))
    d_h1, d_mixo, dg02, dg01, dwg0, dwu0, dwd0 = ffn_bwd("0", 0, n1, gate0, up0, act0, d_f1, h1, mixo, d_h2, g(0, 2), g(0, 1), bf16)
    def d_wo(nm, y):
        return mm(nm, (2, 1, T // TKW), [(y, pl.BlockSpec((TKW, half), lambda s, j, k: (k, s)), d_mixo, pl.BlockSpec((TKW, D), lambda s, j, k: (k, 0)))],
                  TN, pl.BlockSpec((None, half, D), lambda s, j, k: (s, 0, 0)), SDS((2, half, D), f32))
    d_ycat = matmul("out_proj_dy", [(d_mixo, wo4.reshape(4 * half, D))], "nt", f32, 4 * tm, 1024)
    dwo_a, dwo_b = d_wo("out_proj_dwa", y_a), d_wo("out_proj_dwb", y_b)
    G["wo4"] = [dwo_a[0], dwo_a[1], dwo_b[0], dwo_b[1]]
    G["wgT4"], G["wuT4"], G["wd4"] = [dwg0, dwg1], [dwu0, dwu1], [dwd0, dwd1]
    token = early_grads(G)
    d_yssd, d_z, G["gn"] = gate_bwd("gate_bwd", T, tf2, y_ssd, uvz, d_ycat, 1, W["gn"] + token[0, 0])
    d_xc, d_dtr, G["dtb"], G["alog"], G["dsk"] = ssd_bwd("ssd_bwd", T, xc, dtr, sprev, d_yssd, W["dtb"], W["alog"], W["dsk"])
    token = early_grads_next(d_dtr)
    d_pre, G["conv_w"], G["conv_b"] = conv_bwd_pre("conv_bwd_pre", T, tm, xbc, d_xc, W["conv_w"], W["conv_b"] + token[0, 0])
    d_xbc = conv_bwd_x("conv_bwd_x", T, tm, d_pre, W["conv_w"])
    d_u, d_v, G["ln_g"], G["ln_b"], G["wm"], G["bs"] = gmlp_bwd("gmlp_bwd", T, tf2, uvz, d_ycat, 0, W["ln_g"], W["ln_b"], W["wm"], W["bs"])
    w_u, w_v, w_z = W["w_uvz"][:, :D], W["w_uvz"][:, D:2 * D], W["w_uvz"][:, 2 * D:]
    def pre_epilogue(part, xs, os, accs):
        dx, dg = _rms_bwd(xs[0][...], xs[2][...], part)
        os[0][...] = xs[1][...] + dx
        accs[0][...] += dg
    blk = lambda w: pl.BlockSpec((tf, w), lambda i: (i, 0))
    whole = lambda a: pl.BlockSpec(a.shape, lambda i: (0, 0), pipeline_mode=once)
    grad_x, dg00 = mm_fused("in_dy0", T // tf, [(d_u, blk(D), w_u, whole(w_u)), (d_v, blk(D), w_v, whole(w_v)), (d_z, blk(D), w_z, whole(w_z)),
                                                (d_xbc, blk(CONV_DIM), W["w_xbc"], whole(W["w_xbc"])), (d_dtr, blk(DT_PAD), W["w_dt"], whole(W["w_dt"]))],
                            NT, [(x, rows_f), (d_h1, rows_f), (g(0, 0), vec_f)], [out_f32], [(1, D)], pre_epilogue)
    G["w_inT"] = [matmul("in_dwu", [(d_u, y0)], "tn", f32, 1024, 1024, TKW), matmul("in_dwv", [(d_v, y0)], "tn", f32, 1024, 1024, TKW),
                  matmul("in_dwz", [(d_z, y0)], "tn", f32, 1024, 1024, TKW), matmul("in_dwxbc", [(d_xbc, y0)], "tn", f32, 1024, 1024, TKW),
                  matmul("in_dwdt", [(d_dtr, y0)], "tn", f32, DT_PAD, 1024, TKW)[:N_HEADS]]
    G["norm_g"] = jnp.stack([jnp.concatenate([dg00, dg01, dg02, dg03], 0), jnp.concatenate([dg10, dg11, dg12, dg13], 0)])
    return loss_acc, grad_x, G


def build_weights(Wf):
    causal = jnp.tril(jnp.ones((CHUNK, CHUNK), bool))
    w_in = Wf["w_in"].astype(bf16)
    pad16 = lambda v: jnp.pad(v.reshape(1, N_HEADS).astype(f32), ((0, 0), (0, DT_PAD - N_HEADS)))
    return {
        "norm_g": Wf["norm_g"],
        "w_uvz": w_in[:, :3 * D], "w_xbc": w_in[:, 3 * D:3 * D + CONV_DIM],
        "w_dt": jnp.pad(w_in[:, 3 * D + CONV_DIM:], ((0, 0), (0, DT_PAD - N_HEADS))),
        "ln_g": Wf["gm_ln_g"].reshape(1, D), "ln_b": Wf["gm_ln_b"].reshape(1, D),
        "wm": jnp.where(causal[None], Wf["gm_ws"], 0).astype(bf16), "bs": Wf["gm_bs"].reshape(GM_HEADS, CHUNK, 1),
        "conv_w": Wf["conv_w"], "conv_b": Wf["conv_b"].reshape(1, CONV_DIM),
        "dtb": pad16(Wf["dt_bias"]), "alog": pad16(Wf["a_log"]), "dsk": pad16(Wf["d_skip"]),
        "gn": Wf["ssm_norm_g"].reshape(1, D),
        "wo4": Wf["wo4"].astype(bf16),
        "pool_w": Wf["pool_w"].astype(bf16), "pool_b": Wf["pool_b"].reshape(1, D), "pool_scale": Wf["pool_scale"].reshape(1, D),
    }


def small_grads(G):
    return {
        "norm_g": G["norm_g"],
        "gm_ln_g": G["ln_g"].reshape(D), "gm_ln_b": G["ln_b"].reshape(D),
        "gm_ws": G["wm"], "gm_bs": G["bs"].reshape(GM_HEADS, CHUNK),
        "conv_w": G["conv_w"], "conv_b": G["conv_b"].reshape(CONV_DIM),
        "dt_bias": G["dtb"][0, :N_HEADS], "a_log": G["alog"][0, :N_HEADS], "d_skip": G["dsk"][0, :N_HEADS],
        "ssm_norm_g": G["gn"].reshape(D),
        "pool_b": G["pool_b"].reshape(4, POOL_GD), "pool_scale": G["pool_scale"].reshape(D),
    }


MESH_ID = pl.DeviceIdType.MESH
ANY = pl.BlockSpec(memory_space=pl.ANY)


DMA_CHUNK_BYTES = 2 << 20
DMA_MAX_CHUNKS = 32


def _pieces(view, axis, align):
    shape = view.shape
    nbytes = math.prod(shape) * jnp.dtype(view.dtype).itemsize
    n = max(1, min(DMA_MAX_CHUNKS, -(-nbytes // DMA_CHUNK_BYTES)))
    rows = shape[axis]
    size = -(-rows // n)
    size = -(-size // align) * align
    out = []
    for s in range(0, rows, size):
        idx = [slice(None)] * len(shape)
        idx[axis] = pl.ds(s, min(size, rows - s))
        out.append(tuple(idx))
    return out


def comm_call(name, operands, out_shapes, plan):
    n_in = len(operands)
    n_out = len(out_shapes)
    n_remote, n_local = plan((0, 0, 0), [None] * n_in, [None] * n_out, True)

    def body(*refs):
        in_refs, out_refs = refs[:n_in], refs[n_in:n_in + n_out]
        send_sems, recv_sems, local_sems = refs[n_in + n_out:]
        me = (lax.axis_index("x"), lax.axis_index("y"), lax.axis_index("c"))
        remote, local = plan(me, in_refs, out_refs, False)
        align = lambda v: 16 if v.dtype == bf16 else 8
        for j, (s, d, axis) in enumerate(local):
            for ix in _pieces(s, axis, align(s)):
                pltpu.make_async_copy(s.at[ix], d.at[ix], local_sems.at[j]).start()
        peers = [tuple((1 - m) if f else m for m, f in zip(me, flip)) for flip, *_ in remote]
        for k, (flip, src, dst, _, axis) in enumerate(remote):
            for ix in _pieces(src, axis, align(src)):
                pltpu.make_async_remote_copy(src_ref=src.at[ix], dst_ref=dst.at[ix], send_sem=send_sems.at[k], recv_sem=recv_sems.at[k],
                                             device_id=peers[k], device_id_type=MESH_ID).start()
        for k, (flip, src, dst, landing, axis) in enumerate(remote):
            pltpu.make_async_remote_copy(src_ref=landing, dst_ref=landing, send_sem=send_sems.at[k], recv_sem=recv_sems.at[k],
                                         device_id=peers[k], device_id_type=MESH_ID).wait_recv()
        for k, (flip, src, dst, landing, axis) in enumerate(remote):
            pltpu.make_async_remote_copy(src_ref=src, dst_ref=dst, send_sem=send_sems.at[k], recv_sem=recv_sems.at[k],
                                         device_id=peers[k], device_id_type=MESH_ID).wait_send()
        for j, (s, d, axis) in enumerate(local):
            pltpu.make_async_copy(s, d, local_sems.at[j]).wait()

    return pl.pallas_call(
        body, name=name, out_shape=list(out_shapes), in_specs=[ANY] * n_in, out_specs=[ANY] * n_out,
        scratch_shapes=[pltpu.SemaphoreType.DMA((n_remote,)), pltpu.SemaphoreType.DMA((n_remote,)), pltpu.SemaphoreType.DMA((max(n_local, 1),))],
    )(*operands)


CHIP_FLIPS = ((1, 0, 0), (0, 1, 0), (1, 1, 0))
PAIR_FLIP = (0, 0, 1)


def gather_two_level(name, halved, whole):
    nh, nw = len(halved), len(whole)
    nf = len(CHIP_FLIPS)

    def body(*refs):
        srcs, outs = refs[:nh + nw], refs[nh + nw:2 * (nh + nw)]
        send_sems, recv_sems, fwd_send, fwd_recv = refs[2 * (nh + nw):]
        me = (lax.axis_index("x"), lax.axis_index("y"), lax.axis_index("c"))
        k, c = 2 * me[0] + me[1], me[2]
        sibling = (me[0], me[1], 1 - c)
        peers = [tuple((1 - m) if fl else m for m, fl in zip(me, flip)) for flip in CHIP_FLIPS]

        def half(ref, which):
            rh = ref.shape[0] // 2
            return ref.at[pl.ds(pl.multiple_of(which * rh, 16), rh), :]

        def ici(a, f):
            src = half(srcs[a], c) if a < nh else srcs[a]
            dst = half(outs[a].at[k], c) if a < nh else outs[a].at[k]
            return pltpu.make_async_remote_copy(src_ref=src, dst_ref=dst, send_sem=send_sems.at[a * nf + f], recv_sem=recv_sems.at[a * nf + f],
                                                device_id=peers[f], device_id_type=MESH_ID)

        def landed(a, f):
            slot = outs[a].at[_chip_of(me, CHIP_FLIPS[f])]
            return half(slot, c) if a < nh else slot

        def forward(a, f, which):
            v = half(outs[a].at[_chip_of(me, CHIP_FLIPS[f])], which)
            return pltpu.make_async_remote_copy(src_ref=v, dst_ref=v, send_sem=fwd_send.at[a * nf + f], recv_sem=fwd_recv.at[a * nf + f],
                                                device_id=sibling, device_id_type=MESH_ID)

        copies = [ici(a, f) for a in range(nh + nw) for f in range(nf)]
        for cp in copies:
            cp.start()
        fwds = []
        for a in range(nh):
            for f in range(nf):
                lv = landed(a, f)
                pltpu.make_async_remote_copy(src_ref=lv, dst_ref=lv, send_sem=send_sems.at[a * nf + f], recv_sem=recv_sems.at[a * nf + f],
                                             device_id=peers[f], device_id_type=MESH_ID).wait_recv()
                fw = forward(a, f, c)
                fw.start()
                fwds.append(fw)
        for a in range(nh, nh + nw):
            for f in range(nf):
                lv = landed(a, f)
                pltpu.make_async_remote_copy(src_ref=lv, dst_ref=lv, send_sem=send_sems.at[a * nf + f], recv_sem=recv_sems.at[a * nf + f],
                                             device_id=peers[f], device_id_type=MESH_ID).wait_recv()
        for a in range(nh):
            for f in range(nf):
                forward(a, f, 1 - c).wait_recv()
        for fw in fwds:
            fw.wait_send()
        for cp in copies:
            cp.wait_send()

    arrs = list(halved) + list(whole)
    n_ici = (nh + nw) * nf
    return pl.pallas_call(
        body, name=name, out_shape=[SDS((N_CHIPS,) + a.shape, a.dtype) for a in arrs], in_specs=[ANY] * len(arrs), out_specs=[ANY] * len(arrs),
        scratch_shapes=[pltpu.SemaphoreType.DMA((n_ici,)), pltpu.SemaphoreType.DMA((n_ici,)),
                        pltpu.SemaphoreType.DMA((nh * nf,)), pltpu.SemaphoreType.DMA((nh * nf,))],
    )(*arrs)


def pair_split_exchange(name, p, rh):
    def plan(me, ins, outs, count):
        if count:
            return 1, 0
        theirs = ins[0].at[:, pl.ds(pl.multiple_of((1 - me[2]) * rh, 8), rh), :]
        return [(PAIR_FLIP, theirs, outs[0], outs[0], 1)], []
    return comm_call(name, [p], [SDS((4, rh, p.shape[2]), p.dtype)], plan)[0]


def scatter_over_chips(name, cs):
    def plan(me, ins, outs, count):
        if count:
            return len(CHIP_FLIPS), 0
        k = 2 * me[0] + me[1]
        remote = []
        for flip in CHIP_FLIPS:
            kp = 2 * ((1 - me[0]) if flip[0] else me[0]) + ((1 - me[1]) if flip[1] else me[1])
            remote.append((flip, ins[0].at[kp], outs[0].at[k], outs[0].at[kp], 0))
        return remote, []
    return comm_call(name, [cs], [SDS(cs.shape, cs.dtype)], plan)[0]


def pair_swap(name, half):
    def plan(me, ins, outs, count):
        if count:
            return 1, 0
        return [(PAIR_FLIP, ins[0], outs[0], outs[0], 0)], []
    return comm_call(name, [half], [SDS(half.shape, half.dtype)], plan)[0]


def _row_tile(rows, cap=512):
    if rows <= cap:
        return rows
    t = cap - cap % 8
    while rows % t:
        t -= 8
    return t


def pair_sum(name, packs, got, c_arr, tile):
    rh = got.shape[1]
    nb = rh // tile

    def kern(c_ref, a_ref, b_ref, o16_ref):
        o16_ref[...] = (a_ref[...] + b_ref[...]).astype(bf16)
    blk = (None, tile, D)
    grid_spec = pltpu.PrefetchScalarGridSpec(
        num_scalar_prefetch=1, grid=(4, nb),
        in_specs=[pl.BlockSpec(blk, lambda s, i, c: (s, c[0] * nb + i, 0)), pl.BlockSpec(blk, lambda s, i, c: (s, i, 0))],
        out_specs=pl.BlockSpec(blk, lambda s, i, c: (s, i, 0)))
    return pl.pallas_call(kern, name=name, grid_spec=grid_spec, out_shape=SDS(got.shape, bf16),
                          compiler_params=pltpu.CompilerParams(dimension_semantics=("parallel", "parallel")))(c_arr, packs, got)


def chip_sum(name, own16, landed16, k_arr, tile):
    rh = own16.shape[1]
    nb = rh // tile

    def kern(k_ref, own_ref, l0, l1, l2, l3, o_ref):
        k = k_ref[0]
        s = None
        for j, lref in enumerate((l0, l1, l2, l3)):
            t = jnp.where(k == j, own_ref[...], lref[...]).astype(f32)
            s = t if s is None else s + t
        o_ref[...] = s
    blk = (None, tile, D)
    land = [pl.BlockSpec(blk, lambda i, k, j=j: (jnp.where(k[0] == j, (j + 1) % N_CHIPS, j), i, 0)) for j in range(N_CHIPS)]
    grid_spec = pltpu.PrefetchScalarGridSpec(
        num_scalar_prefetch=1, grid=(nb,),
        in_specs=[pl.BlockSpec(blk, lambda i, k: (k[0], i, 0))] + land,
        out_specs=pl.BlockSpec((tile, D), lambda i, k: (i, 0)))
    return pl.pallas_call(kern, name=name, grid_spec=grid_spec, out_shape=SDS((rh, D), f32),
                          compiler_params=pltpu.CompilerParams(dimension_semantics=("parallel",)))(k_arr, own16, landed16, landed16, landed16, landed16)


def adamw(name, w, g, m, v):
    R, C = w.shape
    tr = _row_tile(R, 256)

    def kern(w_ref, g_ref, m_ref, v_ref, d_ref, mo_ref, vo_ref):
        gg = g_ref[...]
        mn = ADAM_B1 * m_ref[...] + (1.0 - ADAM_B1) * gg
        vn = ADAM_B2 * v_ref[...] + (1.0 - ADAM_B2) * jnp.square(gg)
        m_hat = mn / (1.0 - ADAM_B1 ** ADAM_STEP)
        v_hat = vn / (1.0 - ADAM_B2 ** ADAM_STEP)
        d_ref[...] = -ADAM_LR * (m_hat / (jnp.sqrt(v_hat) + ADAM_EPS) + ADAM_WD * w_ref[...])
        mo_ref[...] = mn
        vo_ref[...] = vn
    spec = pl.BlockSpec((tr, C), lambda i: (i, 0))
    s = SDS((R, C), f32)
    return pl.pallas_call(kern, name=name, grid=(R // tr,), in_specs=[spec] * 4, out_specs=[spec] * 3, out_shape=[s, s, s],
                          compiler_params=pltpu.CompilerParams(dimension_semantics=("parallel",)))(w, g, m, v)


WEIGHT_NAMES = ("norm_g", "w_in", "gm_ln_g", "gm_ln_b", "gm_ws", "gm_bs", "conv_w", "conv_b", "dt_bias", "a_log", "d_skip",
                "ssm_norm_g", "w_out", "pool_w", "pool_b", "pool_scale", "ffn_w_gate", "ffn_w_up", "ffn_w_down")
SMALL = ("norm_g", "conv_w", "pool_b", "pool_scale")
REPL = ("gm_ln_g", "gm_ln_b", "gm_ws", "gm_bs", "conv_b", "dt_bias", "a_log", "d_skip", "ssm_norm_g")
SMALL_AXIS = {"norm_g": 2, "conv_w": 1, "pool_b": 1, "pool_scale": 0}
N_CHIPS = 4
IN_SH = IN_DIM // N_CHIPS
SMALL_ROWS = 8
REPL_ROWS = 72
E_OUT, E_GATE, E_UP, E_DOWN = 0, 512, 512 + 2 * FF_SH, 512 + 4 * FF_SH
E_POOL = E_DOWN + 2 * FF_SH
E_ROWS, E_TILE = E_POOL + 64, 400
L_SMALL, L_REPL, L_IN = 0, SMALL_ROWS, SMALL_ROWS + REPL_ROWS
L_END = L_IN + IN_SH
L_ROWS, L_TILE = 1408, 352


def _flat_rows(pieces, rows):
    v = jnp.concatenate([p.reshape(-1) for p in pieces])
    return jnp.pad(v, (0, rows * D - v.shape[0])).reshape(rows, D)


def _shard_small(name, full, k):
    ax = SMALL_AXIS[name]
    n = full.shape[ax] // N_CHIPS
    return lax.slice_in_dim(full, k * n, (k + 1) * n, axis=ax)


def _drop1(name, a):
    return a if name == "norm_g" else a[0]


HBM_SPEC = pl.BlockSpec(memory_space=pltpu.HBM)
SEM_SPEC = pl.BlockSpec(memory_space=pltpu.SEMAPHORE)
SPLIT_EFFECT = pltpu.SideEffectType.DATAFLOW_SIDE_EFFECTING


def _chip_of(me, flip):
    return 2 * ((1 - me[0]) if flip[0] else me[0]) + ((1 - me[1]) if flip[1] else me[1])


def gather_start(name, arrs, after, slotted=False):
    n = len(arrs)
    ncp = n * len(CHIP_FLIPS)

    def body(*refs):
        srcs, lands = refs[:n], refs[n:2 * n]
        send_sems, recv_sems, token = refs[2 * n + 1], refs[2 * n + 2], refs[-1]
        me = (lax.axis_index("x"), lax.axis_index("y"), lax.axis_index("c"))
        k = 2 * me[0] + me[1]
        for a in range(n):
            for f, flip in enumerate(CHIP_FLIPS):
                peer = tuple((1 - m) if fl else m for m, fl in zip(me, flip))
                src = srcs[a].at[_chip_of(me, flip)] if slotted else srcs[a]
                for ix in _pieces(src, 0, 16):
                    pltpu.make_async_remote_copy(src_ref=src.at[ix], dst_ref=lands[a].at[k].at[ix],
                                                 send_sem=send_sems.at[a * len(CHIP_FLIPS) + f], recv_sem=recv_sems.at[a * len(CHIP_FLIPS) + f],
                                                 device_id=peer, device_id_type=MESH_ID).start()
        token[...] = jnp.zeros_like(token)

    land_shapes = [a.shape if slotted else (N_CHIPS,) + a.shape for a in arrs]
    operands = [pltpu.with_memory_space_constraint(a, pltpu.HBM) for a in arrs]
    operands += [pltpu.with_memory_space_constraint(lax.empty(s, a.dtype), pltpu.HBM) for s, a in zip(land_shapes, arrs)]
    out = pl.pallas_call(
        body, name=name,
        out_shape=(pltpu.SemaphoreType.DMA((ncp,)), pltpu.SemaphoreType.DMA((ncp,)), *[pltpu.HBM(a.shape, a.dtype) for a in arrs],
                   *[pltpu.HBM(s, a.dtype) for s, a in zip(land_shapes, arrs)], SDS((8, 128), f32)),
        in_specs=[HBM_SPEC] * (2 * n) + [ANY], out_specs=(SEM_SPEC, SEM_SPEC, *[HBM_SPEC] * (2 * n), pl.BlockSpec(memory_space=pltpu.VMEM)),
        input_output_aliases={i: 2 + i for i in range(2 * n)},
        compiler_params=pltpu.CompilerParams(has_side_effects=SPLIT_EFFECT),
    )(*operands, after)
    return out[0], out[1], out[2:2 + n], out[2 + n:2 + 2 * n], out[-1]


def gather_wait(name, send_sems, recv_sems, thru, lands, after, slotted=False):
    n = len(thru)

    def body(*refs):
        srcs, lands_r = refs[:n], refs[n:2 * n]
        s_sems, r_sems = refs[2 * n], refs[2 * n + 1]
        me = (lax.axis_index("x"), lax.axis_index("y"), lax.axis_index("c"))
        k = 2 * me[0] + me[1]
        for a in range(n):
            for f, flip in enumerate(CHIP_FLIPS):
                peer = tuple((1 - m) if fl else m for m, fl in zip(me, flip))
                idx = a * len(CHIP_FLIPS) + f
                src = srcs[a].at[_chip_of(me, flip)] if slotted else srcs[a]
                pltpu.make_async_remote_copy(src_ref=src, dst_ref=lands_r[a].at[k], send_sem=s_sems.at[idx], recv_sem=r_sems.at[idx],
                                             device_id=peer, device_id_type=MESH_ID).wait_send()
                pltpu.make_async_remote_copy(src_ref=src, dst_ref=lands_r[a].at[_chip_of(me, flip)], send_sem=s_sems.at[idx],
                                             recv_sem=r_sems.at[idx], device_id=peer, device_id_type=MESH_ID).wait_recv()

    out = pl.pallas_call(
        body, name=name, out_shape=tuple(pltpu.HBM(t.shape, t.dtype) for t in (*thru, *lands)),
        in_specs=[HBM_SPEC] * (2 * n) + [SEM_SPEC, SEM_SPEC, ANY], out_specs=tuple([HBM_SPEC] * (2 * n)),
        input_output_aliases={i: i for i in range(2 * n)},
        compiler_params=pltpu.CompilerParams(has_side_effects=SPLIT_EFFECT),
    )(*thru, *lands, send_sems, recv_sems, after)
    return out[:n], out[n:]


def pair_start(name, packs, rh):
    land_shape = (packs.shape[0], rh, packs.shape[2])

    def body(p_ref, land_ref, send_sem, recv_sem, p_thru, land_thru, token):
        me = (lax.axis_index("x"), lax.axis_index("y"), lax.axis_index("c"))
        theirs = p_ref.at[:, pl.ds(pl.multiple_of((1 - me[2]) * rh, 8), rh), :]
        for ix in _pieces(theirs, 1, 8):
            pltpu.make_async_remote_copy(src_ref=theirs.at[ix], dst_ref=land_ref.at[ix], send_sem=send_sem, recv_sem=recv_sem,
                                         device_id=(me[0], me[1], 1 - me[2]), device_id_type=MESH_ID).start()
        token[...] = jnp.zeros_like(token)

    return pl.pallas_call(
        body, name=name,
        out_shape=(pltpu.SemaphoreType.DMA(()), pltpu.SemaphoreType.DMA(()), pltpu.HBM(packs.shape, packs.dtype), pltpu.HBM(land_shape, packs.dtype),
                   SDS((8, 128), f32)),
        in_specs=[HBM_SPEC, HBM_SPEC], out_specs=(SEM_SPEC, SEM_SPEC, HBM_SPEC, HBM_SPEC, pl.BlockSpec(memory_space=pltpu.VMEM)),
        input_output_aliases={0: 2, 1: 3}, compiler_params=pltpu.CompilerParams(has_side_effects=SPLIT_EFFECT),
    )(pltpu.with_memory_space_constraint(packs, pltpu.HBM), pltpu.with_memory_space_constraint(lax.empty(land_shape, packs.dtype), pltpu.HBM))


def pair_wait(name, send_sem, recv_sem, packs, land, after):
    rh = land.shape[1]

    def body(p_ref, land_ref, s_sem, r_sem, after_ref, p_out, land_out):
        me = (lax.axis_index("x"), lax.axis_index("y"), lax.axis_index("c"))
        theirs = p_ref.at[:, pl.ds(pl.multiple_of((1 - me[2]) * rh, 8), rh), :]
        cp = pltpu.make_async_remote_copy(src_ref=theirs, dst_ref=land_ref, send_sem=s_sem, recv_sem=r_sem,
                                          device_id=(me[0], me[1], 1 - me[2]), device_id_type=MESH_ID)
        cp.wait_send()
        cp.wait_recv()

    return pl.pallas_call(
        body, name=name, out_shape=(pltpu.HBM(packs.shape, packs.dtype), pltpu.HBM(land.shape, land.dtype)),
        in_specs=[HBM_SPEC, HBM_SPEC, SEM_SPEC, SEM_SPEC, ANY], out_specs=(HBM_SPEC, HBM_SPEC), input_output_aliases={0: 0, 1: 1},
        compiler_params=pltpu.CompilerParams(has_side_effects=SPLIT_EFFECT),
    )(packs, land, send_sem, recv_sem, after)


def gather_weights(w_sh):
    big = [w_sh["w_in"][0], w_sh["w_out"][0], w_sh["pool_w"][0].reshape(4 * 64, POOL_GD)]
    small_pack = _flat_rows([w_sh[n] for n in SMALL], SMALL_ROWS)
    own = [b.astype(bf16) for b in big] + [small_pack]
    my_k = 2 * lax.axis_index("x") + lax.axis_index("y")
    s_in, s_out, s_pool, s_small = [lax.dynamic_update_slice(s, o[None], (my_k, 0, 0))
                                    for s, o in zip(gather_two_level("gather_weights", own[:3], own[3:]), own)]
    Wf = {n: w_sh[n][0] for n in REPL}
    Wf["w_in"] = s_in.transpose(1, 0, 2).reshape(D, IN_DIM)
    Wf["pool_w"] = s_pool.reshape(N_CHIPS, 4, 64, POOL_GD).transpose(1, 0, 2, 3).reshape(4, POOL_GD, POOL_GD)
    Wf["wo4"] = s_out
    small_shapes = [_drop1(n, w_sh[n]).shape for n in SMALL]
    parts = [_split_rows(s_small[k], small_shapes) for k in range(N_CHIPS)]
    for j, n in enumerate(SMALL):
        Wf[n] = jnp.concatenate([parts[k][j] for k in range(N_CHIPS)], axis=SMALL_AXIS[n])
    return Wf


def pack_early(G):
    slots = [jnp.concatenate([G["wo4"][k], G["wgT4"][0][k], G["wgT4"][1][k], G["wuT4"][0][k], G["wuT4"][1][k], G["wd4"][0][k], G["wd4"][1][k],
                              G["pool_w"][:, k * 64:(k + 1) * 64, :].reshape(64, D)], axis=0) for k in range(N_CHIPS)]
    return jnp.stack(slots)


def pack_late(G):
    sg = small_grads(G)
    repl = _flat_rows([sg[n] for n in REPL], REPL_ROWS)
    w_in_t = jnp.concatenate(G["w_inT"], axis=0)
    slots = [jnp.concatenate([_flat_rows([_shard_small(n, sg[n], k) for n in SMALL], SMALL_ROWS), repl,
                              jnp.pad(w_in_t[k * IN_SH:(k + 1) * IN_SH], ((0, L_ROWS - L_END), (0, 0)))], axis=0)
             for k in range(N_CHIPS)]
    return jnp.stack(slots)


def unpack_grads(early, late, w_sh):
    g = {"w_out": early[E_OUT:E_GATE], "ffn_w_down": early[E_DOWN:E_POOL], "pool_w": early[E_POOL:E_ROWS],
         "ffn_w_gate": jnp.stack([early[E_GATE + l * FF_SH:E_GATE + (l + 1) * FF_SH].T for l in range(2)]),
         "ffn_w_up": jnp.stack([early[E_UP + l * FF_SH:E_UP + (l + 1) * FF_SH].T for l in range(2)]),
         "w_in": late[L_IN:L_END].T}
    small = _split_rows(late[L_SMALL:L_REPL], [_drop1(n, w_sh[n]).shape for n in SMALL])
    repl = _split_rows(late[L_REPL:L_IN], [w_sh[n][0].shape for n in REPL])
    g.update(zip(SMALL, small))
    g.update(zip(REPL, repl))
    return {n: g[n].reshape(w_sh[n].shape) for n in WEIGHT_NAMES}


def _split_rows(flat2d, shapes):
    v = flat2d.reshape(-1)
    out, off = [], 0
    for s in shapes:
        n = math.prod(s)
        out.append(v[off:off + n].reshape(s))
        off += n
    return out


def kernel(x, norm_g, w_in, gm_ln_g, gm_ln_b, gm_ws, gm_bs, conv_w, conv_b, dt_bias, a_log, d_skip, ssm_norm_g, w_out, pool_w, pool_b, pool_scale, ffn_w_gate, ffn_w_up, ffn_w_down, loss_target, m_norm_g, m_w_in, m_gm_ln_g, m_gm_ln_b, m_gm_ws, m_gm_bs, m_conv_w, m_conv_b, m_dt_bias, m_a_log, m_d_skip, m_ssm_norm_g, m_w_out, m_pool_w, m_pool_b, m_pool_scale, m_ffn_w_gate, m_ffn_w_up, m_ffn_w_down, v_norm_g, v_w_in, v_gm_ln_g, v_gm_ln_b, v_gm_ws, v_gm_bs, v_conv_w, v_conv_b, v_dt_bias, v_a_log, v_d_skip, v_ssm_norm_g, v_w_out, v_pool_w, v_pool_b, v_pool_scale, v_ffn_w_gate, v_ffn_w_up, v_ffn_w_down):
    T = x.shape[1]
    w_sh = dict(zip(WEIGHT_NAMES, (norm_g, w_in, gm_ln_g, gm_ln_b, gm_ws, gm_bs, conv_w, conv_b, dt_bias, a_log, d_skip, ssm_norm_g, w_out,
                                   pool_w, pool_b, pool_scale, ffn_w_gate, ffn_w_up, ffn_w_down)))
    m_sh = dict(zip(WEIGHT_NAMES, (m_norm_g, m_w_in, m_gm_ln_g, m_gm_ln_b, m_gm_ws, m_gm_bs, m_conv_w, m_conv_b, m_dt_bias, m_a_log, m_d_skip,
                                   m_ssm_norm_g, m_w_out, m_pool_w, m_pool_b, m_pool_scale, m_ffn_w_gate, m_ffn_w_up, m_ffn_w_down)))
    v_sh = dict(zip(WEIGHT_NAMES, (v_norm_g, v_w_in, v_gm_ln_g, v_gm_ln_b, v_gm_ws, v_gm_bs, v_conv_w, v_conv_b, v_dt_bias, v_a_log, v_d_skip,
                                   v_ssm_norm_g, v_w_out, v_pool_w, v_pool_b, v_pool_scale, v_ffn_w_gate, v_ffn_w_up, v_ffn_w_down)))

    my_k = 2 * lax.axis_index("x") + lax.axis_index("y")
    ffn_own = [w_sh["ffn_w_gate"].reshape(2 * D, FF_SH).astype(bf16), w_sh["ffn_w_up"].reshape(2 * D, FF_SH).astype(bf16),
               w_sh["ffn_w_down"].reshape(2 * FF_SH, D).astype(bf16)]
    Wf = gather_weights(w_sh)
    send_sems, recv_sems, thru, lands, token = gather_start("gather_ffn_start", ffn_own, Wf["wo4"])
    Wf["norm_g"] = Wf["norm_g"] + token[0, 0]
    W = build_weights(Wf)

    def ffn_weights(after):
        _, landed = gather_wait("gather_ffn_wait", send_sems, recv_sems, thru, lands, after)
        return tuple(lax.dynamic_update_slice(l, o[None], (my_k, 0, 0)) for l, o in zip(landed, ffn_own))

    my_c = lax.axis_index("c")
    c_arr = my_c.astype(jnp.int32).reshape(1)
    k_arr = my_k.astype(jnp.int32).reshape(1)

    def pair_stage(tag, packs, tile):
        got = pair_split_exchange(f"grads{tag}_pair_split", packs, packs.shape[1] // 2)
        return pair_sum(f"grads{tag}_pair_sum", packs, got, c_arr, tile)

    def chip_stage(tag, pair16, landed, tile):
        half = chip_sum(f"grads{tag}_chip_sum", pair16, landed, k_arr, tile)
        other = pair_swap(f"grads{tag}_pair_swap", half)
        return jnp.concatenate([jnp.where(my_c == 0, half, other), jnp.where(my_c == 0, other, half)], axis=0)

    early = {}

    def early_grads(Ge):
        *pair, tok = pair_start("gradsE_pair_start", pack_early(Ge), E_ROWS // 2)
        early.update(pair=pair)
        return tok

    def early_grads_next(after):
        packs, got = pair_wait("gradsE_pair_wait", *early["pair"], after)
        pair16 = pair_sum("gradsE_pair_sum", packs, got, c_arr, E_TILE)
        s_sems, r_sems, thru, lands, tok = gather_start("gradsE_scatter_start", [pair16], jnp.zeros((8, 128), f32), slotted=True)
        early.update(s_sems=s_sems, r_sems=r_sems, thru=thru, lands=lands)
        return tok

    loss_acc, grad_x, G = local_step(T, x[0], loss_target[0], W, ffn_weights, early_grads, early_grads_next)
    pair_l = pair_stage("L", pack_late(G), L_TILE)
    total_l = chip_stage("L", pair_l, scatter_over_chips("gradsL_scatter", pair_l), L_TILE)
    (pair_e,), (landed_e,) = gather_wait("gradsE_scatter_wait", early["s_sems"], early["r_sems"], early["thru"], early["lands"], total_l,
                                         slotted=True)
    total_e = chip_stage("E", pair_e, landed_e, E_TILE)
    grads = unpack_grads(total_e, total_l, w_sh)

    delta, new_m, new_v = {}, {}, {}
    for n in WEIGHT_NAMES:
        shp = w_sh[n].shape
        two_d = (-1, shp[-1])
        d_, m_, v_ = adamw("adamw_" + n, w_sh[n].reshape(two_d), grads[n].reshape(two_d), m_sh[n].reshape(two_d), v_sh[n].reshape(two_d))
        delta[n], new_m[n], new_v[n] = d_.reshape(shp), m_.reshape(shp), v_.reshape(shp)

    loss = lax.psum(loss_acc[0, 0], ("x", "y", "c"))
    return (loss, grad_x[None], *[grads[n] for n in WEIGHT_NAMES], *[delta[n] for n in WEIGHT_NAMES],
            *[new_m[n] for n in WEIGHT_NAMES], *[new_v[n] for n in WEIGHT_NAMES])
```

```python
import math

import jax
import jax.numpy as jnp
from jax import lax
from jax.experimental import pallas as pl
from jax.experimental.pallas import tpu as pltpu

f32, bf16 = jnp.float32, jnp.bfloat16
SDS = jax.ShapeDtypeStruct

D = 1024
EPS = 1e-6
CHUNK = 128
GM_HEADS, GM_HD = 4, 256
SSM_GROUPS, SSM_HPG, SSM_P, SSM_N = 4, 4, 64, 128
N_HEADS = SSM_GROUPS * SSM_HPG
CONV_K = 4
CONV_DIM = 2048
POOL_WINDOWS = (2, 4, 8, 16)
POOL_GD = 256
POOL_HALO = 32
CONV_HALO = 16
D_FF = 2816
DT_PAD = 128
IN_DIM = 5136

ADAM_LR, ADAM_B1, ADAM_B2, ADAM_EPS, ADAM_WD, ADAM_STEP = 0.001, 0.9, 0.999, 1e-08, 0.01, 10

NT = (((1,), (1,)), ((), ()))
TN = (((0,), (0,)), ((), ()))
NN = (((1,), (0,)), ((), ()))
HI = lax.Precision.HIGHEST
MM_SUB = 256


def _silu(x):
    return x * jax.nn.sigmoid(x)


def _softplus(x):
    return jnp.maximum(x, 0.0) + jnp.log1p(jnp.exp(-jnp.abs(x)))


def _rms(x, g):
    return x * lax.rsqrt(jnp.mean(x * x, axis=-1, keepdims=True) + EPS) * g


def _rms_bwd(x, g, dy):
    r = lax.rsqrt(jnp.mean(x * x, axis=-1, keepdims=True) + EPS)
    xh = x * r
    dxh = dy * g
    dx = r * (dxh - xh * jnp.mean(dxh * xh, axis=-1, keepdims=True))
    return dx, jnp.sum(dy * xh, axis=0, keepdims=True)


def _bdot(a, b, dims=NN):
    return lax.dot_general(a.astype(bf16), b.astype(bf16), dims, preferred_element_type=f32)


def matmul(name, pairs, mode, out_dtype, tm, tn, tk=None):
    a0, b0 = pairs[0]
    if mode == "tn":
        M, N, K = a0.shape[1], b0.shape[1], a0.shape[0]
    else:
        M, K = a0.shape
        N = b0.shape[1] if mode == "nn" else b0.shape[0]
    tm, tn = min(tm, M), min(tn, N)
    assert M % tm == 0 and N % tn == 0, (name, M, N, tm, tn)
    if tk is None:
        nk = 1
    else:
        assert len(pairs) == 1 and K % tk == 0
        nk = K // tk
    dims = {"nn": NN, "nt": NT, "tn": TN}[mode]
    in_specs, args = [], []
    for a, b in pairs:
        kk = (a.shape[0] if mode == "tn" else a.shape[1]) if tk is None else tk
        if mode == "tn":
            in_specs.append(pl.BlockSpec((kk, tm), lambda j, i, k: (k, i)))
            in_specs.append(pl.BlockSpec((kk, tn), lambda j, i, k: (k, j)))
        elif mode == "nn":
            in_specs.append(pl.BlockSpec((tm, kk), lambda j, i, k: (i, k)))
            in_specs.append(pl.BlockSpec((kk, tn), lambda j, i, k: (k, j)))
        else:
            in_specs.append(pl.BlockSpec((tm, kk), lambda j, i, k: (i, k)))
            in_specs.append(pl.BlockSpec((tn, kk), lambda j, i, k: (j, k)))
        args += [a, b]
    npairs = len(pairs)

    def kern(*refs):
        o = refs[2 * npairs]
        part = None
        for p in range(npairs):
            d = _bdot(refs[2 * p][...], refs[2 * p + 1][...], dims)
            part = d if part is None else part + d
        if nk == 1:
            o[...] = part.astype(out_dtype)
        else:
            acc = refs[2 * npairs + 1]
            k = pl.program_id(2)

            @pl.when(k == 0)
            def _():
                acc[...] = part

            @pl.when(k > 0)
            def _():
                acc[...] += part

            @pl.when(k == nk - 1)
            def _():
                o[...] = acc[...].astype(out_dtype)

    return pl.pallas_call(
        kern, name=name, grid=(N // tn, M // tm, nk),
        in_specs=in_specs, out_specs=pl.BlockSpec((tm, tn), lambda j, i, k: (i, j)),
        out_shape=SDS((M, N), out_dtype),
        scratch_shapes=[pltpu.VMEM((tm, tn), f32)] if nk > 1 else [],
        compiler_params=pltpu.CompilerParams(dimension_semantics=("parallel", "parallel", "arbitrary")),
    )(*args)


def mm(name, grid, pairs, dims, o_spec, out_shape):
    nk = grid[2]
    npairs = len(pairs)
    in_specs, args = [], []
    for a, a_spec, b, b_spec in pairs:
        in_specs += [a_spec, b_spec]
        args += [a, b]
    blk = tuple(d for d in o_spec.block_shape if d is not None)

    def kern(*refs):
        o = refs[2 * npairs]
        part = None
        for p in range(npairs):
            d = _bdot(refs[2 * p][...], refs[2 * p + 1][...], dims)
            part = d if part is None else part + d
        if nk == 1:
            o[...] = part.astype(o.dtype)
        else:
            acc = refs[2 * npairs + 1]
            k = pl.program_id(2)

            @pl.when(k == 0)
            def _():
                acc[...] = part

            @pl.when(k > 0)
            def _():
                acc[...] += part

            @pl.when(k == nk - 1)
            def _():
                o[...] = acc[...].astype(o.dtype)

    return pl.pallas_call(
        kern, name=name, grid=grid, in_specs=in_specs, out_specs=o_spec, out_shape=out_shape,
        scratch_shapes=[pltpu.VMEM(blk, f32)] if nk > 1 else [],
        compiler_params=pltpu.CompilerParams(dimension_semantics=("parallel", "parallel", "arbitrary")),
    )(*args)


def mm_fused(name, n_row_blocks, pairs, dims, extra_ins, outs, accs, epilogue):
    npairs, nx, no, na = len(pairs), len(extra_ins), len(outs), len(accs)
    in_specs, args = [], []
    for a, a_spec, b, b_spec in pairs:
        in_specs += [a_spec, b_spec]
        args += [a, b]
    for arr, spec in extra_ins:
        in_specs.append(spec)
        args.append(arr)

    rows_blk = outs[0][1].block_shape[0]
    sub = min(rows_blk, MM_SUB)

    def kern(*refs):
        x_refs = refs[2 * npairs:2 * npairs + nx]
        o_refs = refs[2 * npairs + nx:2 * npairs + nx + no]
        a_refs = refs[2 * npairs + nx + no:]
        if na:
            @pl.when(pl.program_id(0) == 0)
            def _():
                for a in a_refs:
                    a[...] = jnp.zeros(a.shape, f32)
        for r0 in range(0, rows_blk, sub):
            rows = pl.ds(r0, sub)
            part = None
            for p in range(npairs):
                d = _bdot(refs[2 * p][rows, :], refs[2 * p + 1][...], dims)
                part = d if part is None else part + d
            epilogue(part, [x.at[rows, :] if x.shape[0] == rows_blk else x for x in x_refs], [o.at[rows, :] for o in o_refs], a_refs)

    return pl.pallas_call(
        kern, name=name, grid=(n_row_blocks,), in_specs=in_specs,
        out_specs=[spec for _, spec in outs] + [pl.BlockSpec(tuple(s), lambda i, nd=len(s): (0,) * nd) for s in accs],
        out_shape=[s for s, _ in outs] + [SDS(tuple(s), f32) for s in accs],
        compiler_params=pltpu.CompilerParams(dimension_semantics=("arbitrary",)),
    )(*args)


FF_SH = D_FF // 4


def ffn_up(name, T, tm, n_bf, wg4, wu4, l):
    sub = min(tm, MM_SUB)

    def kern(n_ref, wg_ref, wu_ref, g_ref, u_ref, a_ref):
        for r0 in range(0, tm, sub):
            rows = pl.ds(r0, sub)
            n = n_ref[rows, :]
            g = jnp.dot(n, wg_ref[...], preferred_element_type=f32)
            u = jnp.dot(n, wu_ref[...], preferred_element_type=f32)
            g_ref[rows, :] = g.astype(bf16)
            u_ref[rows, :] = u.astype(bf16)
            a_ref[rows, :] = (_silu(g) * u).astype(bf16)
    w_spec = pl.BlockSpec((None, D, FF_SH), lambda k, i: (k, l, 0))
    o_spec = pl.BlockSpec((None, tm, FF_SH), lambda k, i: (k, i, 0))
    s = SDS((4, T, FF_SH), bf16)
    return pl.pallas_call(kern, name=name, grid=(4, T // tm), in_specs=[pl.BlockSpec((tm, D), lambda k, i: (i, 0)), w_spec, w_spec],
                          out_specs=[o_spec] * 3, out_shape=[s, s, s],
                          compiler_params=pltpu.CompilerParams(dimension_semantics=("parallel", "parallel")))(n_bf, wg4, wu4)


def ffn_dgu(name, T, tm, d_f, wd4, gate4, up4, l):
    rc = 16

    sub = min(tm, MM_SUB)

    def kern(df_ref, wd_ref, g_ref, u_ref, dg_ref, du_ref, dact_ref):
        for s0 in range(0, tm, sub):
            dact_ref[pl.ds(s0, sub), :] = _bdot(df_ref[pl.ds(s0, sub), :], wd_ref[...], NT)
            for r0 in range(s0, s0 + sub, rc):
                rows = pl.ds(r0, rc)
                _, vjp = jax.vjp(lambda a, b: _silu(a) * b, g_ref[rows, :].astype(f32), u_ref[rows, :].astype(f32))
                dg, du = vjp(dact_ref[rows, :])
                dg_ref[rows, :] = dg.astype(bf16)
                du_ref[rows, :] = du.astype(bf16)
    a_spec = pl.BlockSpec((None, tm, FF_SH), lambda k, i: (k, i, 0))
    s = SDS((4, T, FF_SH), bf16)
    return pl.pallas_call(kern, name=name, grid=(4, T // tm),
                          in_specs=[pl.BlockSpec((tm, D), lambda k, i: (i, 0)), pl.BlockSpec((None, FF_SH, D), lambda k, i: (k, l, 0)), a_spec, a_spec],
                          out_specs=[a_spec] * 2, out_shape=[s, s], scratch_shapes=[pltpu.VMEM((tm, FF_SH), f32)],
                          compiler_params=pltpu.CompilerParams(dimension_semantics=("parallel", "parallel")))(d_f, wd4, gate4, up4)


def rowcall(name, body, T, tm, ins, outs, accs=(), scratch=(), reverse=False, sub=None):
    n = T // tm
    assert T % tm == 0

    def blk(i):
        return (n - 1 - i) if reverse else i

    in_specs, args = [], []
    for spec in ins:
        kind, arr = spec[0], spec[1]
        if kind == "row":
            _, _, w, cb = spec
            in_specs.append(pl.BlockSpec((tm, w), lambda i, cb=cb: (blk(i), cb)))
        elif kind == "prev":
            _, _, w, cb, h = spec
            r = tm // h
            in_specs.append(pl.BlockSpec((h, w), lambda i, cb=cb, r=r: (jnp.maximum(blk(i) * r - 1, 0), cb)))
        elif kind == "next":
            _, _, w, cb, h = spec
            r = tm // h
            in_specs.append(pl.BlockSpec((h, w), lambda i, cb=cb, r=r, h=h: (jnp.minimum((blk(i) + 1) * r, T // h - 1), cb)))
        else:
            nd = arr.ndim
            in_specs.append(pl.BlockSpec(arr.shape, lambda i, nd=nd: (0,) * nd))
        args.append(arr)
    out_shape = [SDS((T, w), dt) for w, dt in outs] + [SDS(tuple(s), f32) for s in accs]
    out_specs = [pl.BlockSpec((tm, w), lambda i: (blk(i), 0)) for w, _ in outs]
    out_specs += [pl.BlockSpec(tuple(s), lambda i, nd=len(s): (0,) * nd) for s in accs]
    ni, no, na = len(ins), len(outs), len(accs)

    def kern(*refs):
        i = pl.program_id(0)
        in_refs, out_refs = refs[:ni], refs[ni:ni + no]
        acc_refs, scr = refs[ni + no:ni + no + na], refs[ni + no + na:]
        if na:
            @pl.when(i == 0)
            def _():
                for a in acc_refs:
                    a[...] = jnp.zeros(a.shape, f32)
        if sub is None or sub >= tm:
            body(blk(i), n, in_refs, out_refs, acc_refs, scr)
        else:
            for r0 in range(0, tm, sub):
                rows = pl.ds(r0, sub)
                body(blk(i), n, [r.at[rows, :] if spec[0] == "row" else r for r, spec in zip(in_refs, ins)],
                     [o.at[rows, :] for o in out_refs], acc_refs, [s.at[rows, :] for s in scr])

    res = pl.pallas_call(
        kern, name=name, grid=(n,), in_specs=in_specs, out_specs=out_specs, out_shape=out_shape,
        scratch_shapes=list(scratch),
        compiler_params=pltpu.CompilerParams(dimension_semantics=("arbitrary",)),
    )(*args)
    return res


def rms_to_bf16(name, T, tm, x, g):
    def body(i, n, ins, outs, accs, scr):
        outs[0][...] = _rms(ins[0][...], ins[1][...]).astype(bf16)
    return rowcall(name, body, T, tm, [("row", x, D, 0), ("const", g)], [(D, bf16)], sub=64)[0]


def _layer_norm_parts(x):
    mu = jnp.mean(x, axis=-1, keepdims=True)
    xc = x - mu
    r = lax.rsqrt(jnp.mean(xc * xc, axis=-1, keepdims=True) + EPS)
    return xc * r, r


def _gelu_and_slope(x):
    c, a = math.sqrt(2.0 / math.pi), 0.044715
    x2 = x * x
    t = jnp.tanh(c * (x + a * (x * x2)))
    cdf = 0.5 * (1.0 + t)
    slope = cdf + (0.5 * x) * (1.0 - t * t) * (c * (1.0 + (3.0 * a) * x2))
    return x * cdf, slope


def gmlp_fwd(name, T, tm, uvz, ln_g, ln_b, wm, bs):
    def body(i, n, ins, outs, accs, scr):
        gu = _gelu_and_slope(ins[0][...].astype(f32))[0]
        xh, _ = _layer_norm_parts(_gelu_and_slope(ins[1][...].astype(f32))[0])
        vln = (xh * ins[2][...] + ins[3][...]).astype(bf16)
        for c in range(ins[0].shape[0] // CHUNK):
            rows = slice(c * CHUNK, (c + 1) * CHUNK)
            for h in range(GM_HEADS):
                cols = slice(h * GM_HD, (h + 1) * GM_HD)
                mixed = jnp.dot(ins[4][h], vln[rows, cols], preferred_element_type=f32) + ins[5][h]
                outs[0][rows, cols] = (gu[rows, cols] * mixed).astype(bf16)
    return rowcall(name, body, T, tm, [("row", uvz, D, 0), ("row", uvz, D, 1), ("const", ln_g), ("const", ln_b), ("const", wm), ("const", bs)],
                   [(D, bf16)], sub=CHUNK)[0]


def gmlp_bwd(name, T, tm, uvz, d_ya, d_cb, ln_g, ln_b, wm, bs):
    def body(i, n, ins, outs, accs, scr):
        u, v, dya = ins[0][...].astype(f32), ins[1][...].astype(f32), ins[2][...]
        gu, slope_u = _gelu_and_slope(u)
        gv, slope_v = _gelu_and_slope(v)
        xh, r = _layer_norm_parts(gv)
        lng = ins[3][...]
        vln = (xh * lng + ins[4][...]).astype(bf16)
        rr = lax.broadcasted_iota(jnp.int32, (CHUNK, CHUNK), 0)
        cc = lax.broadcasted_iota(jnp.int32, (CHUNK, CHUNK), 1)
        causal = (rr >= cc).astype(f32)
        dvln_ref = scr[0]
        dgu_ref = scr[1]
        for c in range(ins[0].shape[0] // CHUNK):
            rows = slice(c * CHUNK, (c + 1) * CHUNK)
            for h in range(GM_HEADS):
                cols = slice(h * GM_HD, (h + 1) * GM_HD)
                w = ins[5][h]
                blk = vln[rows, cols]
                mixed = jnp.dot(w, blk, preferred_element_type=f32) + ins[6][h]
                dy = dya[rows, cols]
                dgu_ref[rows, cols] = dy * mixed
                dm = dy * gu[rows, cols]
                accs[3][h] += jnp.sum(dm, axis=1, keepdims=True)
                accs[2][h] += _bdot(dm, blk, NT) * causal
                dvln_ref[rows, cols] = _bdot(w, dm, TN)
        dvln = dvln_ref[...]
        accs[0][...] += jnp.sum(dvln * xh, axis=0, keepdims=True)
        accs[1][...] += jnp.sum(dvln, axis=0, keepdims=True)
        dxh = dvln * lng
        dgv = r * (dxh - jnp.mean(dxh, axis=-1, keepdims=True) - xh * jnp.mean(dxh * xh, axis=-1, keepdims=True))
        outs[0][...] = (dgu_ref[...] * slope_u).astype(bf16)
        outs[1][...] = (dgv * slope_v).astype(bf16)
    return rowcall(name, body, T, tm,
                   [("row", uvz, D, 0), ("row", uvz, D, 1), ("row", d_ya, D, d_cb), ("const", ln_g), ("const", ln_b), ("const", wm), ("const", bs)],
                   [(D, bf16), (D, bf16)], accs=[(1, D), (1, D), (GM_HEADS, CHUNK, CHUNK), (GM_HEADS, CHUNK, 1)],
                   scratch=[pltpu.VMEM((tm, D), f32), pltpu.VMEM((tm, D), f32)], sub=CHUNK)


CONV_RC, CONV_LB = 64, 256


def _conv_fill(i, x_ref, halo_ref, scr, tm):
    scr[pl.ds(0, CONV_HALO), :] = jnp.where(i > 0, halo_ref[...].astype(f32), 0.0)
    scr[pl.ds(CONV_HALO, tm), :] = x_ref[...].astype(f32)


def _conv_taps(scr, r0, lanes):
    return [scr[pl.ds(r0 + CONV_HALO - (CONV_K - 1) + k, CONV_RC), lanes] for k in range(CONV_K)]


def conv_fwd(name, T, tm, xbc, conv_w, conv_b):
    def body(i, n, ins, outs, accs, scr):
        s = scr[0]
        _conv_fill(i, ins[0], ins[1], s, tm)
        for lb in range(CONV_DIM // CONV_LB):
            lanes = slice(lb * CONV_LB, (lb + 1) * CONV_LB)
            w, b = ins[2][:, lanes], ins[3][:, lanes]

            for r0 in range(0, tm, CONV_RC):
                taps = _conv_taps(s, r0, lanes)
                pre = b + sum(w[k:k + 1] * taps[k] for k in range(CONV_K))
                outs[0][pl.ds(r0, CONV_RC), lanes] = _silu(pre)
    return rowcall(name, body, T, tm, [("row", xbc, CONV_DIM, 0), ("prev", xbc, CONV_DIM, 0, CONV_HALO), ("const", conv_w), ("const", conv_b)],
                   [(CONV_DIM, f32)], scratch=[pltpu.VMEM((tm + CONV_HALO, CONV_DIM), f32)])[0]


def conv_bwd_pre(name, T, tm, xbc, d_xc, conv_w, conv_b):
    def body(i, n, ins, outs, accs, scr):
        s = scr[0]
        _conv_fill(i, ins[0], ins[1], s, tm)
        fold = lambda v: jnp.sum(v.reshape(CONV_RC // 8, 8, CONV_LB), axis=0)
        for lb in range(CONV_DIM // CONV_LB):
            lanes = slice(lb * CONV_LB, (lb + 1) * CONV_LB)
            w, b = ins[3][:, lanes], ins[4][:, lanes]

            sums = [jnp.zeros((8, CONV_LB), f32)] * (CONV_K + 1)
            for r0 in range(0, tm, CONV_RC):
                taps = _conv_taps(s, r0, lanes)
                pre = b + sum(w[k:k + 1] * taps[k] for k in range(CONV_K))
                _, vjp = jax.vjp(_silu, pre)
                dpre = vjp(ins[2][pl.ds(r0, CONV_RC), lanes])[0]
                outs[0][pl.ds(r0, CONV_RC), lanes] = dpre
                sums = [sums[k] + fold(dpre * taps[k]) for k in range(CONV_K)] + [sums[CONV_K] + fold(dpre)]
            for k in range(CONV_K):
                accs[0][pl.ds(k, 1), lanes] += jnp.sum(sums[k], axis=0, keepdims=True)
            accs[1][:, lanes] += jnp.sum(sums[CONV_K], axis=0, keepdims=True)
    return rowcall(name, body, T, tm,
                   [("row", xbc, CONV_DIM, 0), ("prev", xbc, CONV_DIM, 0, CONV_HALO), ("row", d_xc, CONV_DIM, 0), ("const", conv_w), ("const", conv_b)],
                   [(CONV_DIM, f32)], accs=[(CONV_K, CONV_DIM), (1, CONV_DIM)], scratch=[pltpu.VMEM((tm + CONV_HALO, CONV_DIM), f32)])


def conv_bwd_x(name, T, tm, d_pre, conv_w):
    def body(i, n, ins, outs, accs, scr):
        s = scr[0]
        s[pl.ds(0, tm), :] = ins[0][...]
        s[pl.ds(tm, CONV_HALO), :] = jnp.where(i < n - 1, ins[1][...], 0.0)
        for lb in range(CONV_DIM // CONV_LB):
            lanes = slice(lb * CONV_LB, (lb + 1) * CONV_LB)
            w = ins[2][:, lanes]

            for r0 in range(0, tm, CONV_RC):
                dx = sum(w[k:k + 1] * s[pl.ds(r0 + CONV_K - 1 - k, CONV_RC), lanes] for k in range(CONV_K))
                outs[0][pl.ds(r0, CONV_RC), lanes] = dx.astype(bf16)
    return rowcall(name, body, T, tm, [("row", d_pre, CONV_DIM, 0), ("next", d_pre, CONV_DIM, 0, CONV_HALO), ("const", conv_w)],
                   [(CONV_DIM, bf16)], scratch=[pltpu.VMEM((tm + CONV_HALO, CONV_DIM), f32)])[0]


def _ssd_prep(dtr, dtb, alog):
    rr = lax.broadcasted_iota(jnp.int32, (CHUNK, CHUNK), 0)
    cc = lax.broadcasted_iota(jnp.int32, (CHUNK, CHUNK), 1)
    dt = _softplus(dtr + dtb)
    dA = dt * -jnp.exp(alog)
    acum = jnp.dot((rr >= cc).astype(f32), dA, precision=HI, preferred_element_type=f32)
    return dt, acum, acum.T, jnp.sum(dA, axis=0, keepdims=True)


def _ssd_group(g, x, Bm, Cm, S, dt, acum, acumT, tot, dsk):
    rr = lax.broadcasted_iota(jnp.int32, (CHUNK, CHUNK), 0)
    cc = lax.broadcasted_iota(jnp.int32, (CHUNK, CHUNK), 1)
    tril = rr >= cc
    lane = lax.broadcasted_iota(jnp.int32, (1, DT_PAD), 1)
    sub = lax.broadcasted_iota(jnp.int32, (DT_PAD, 1), 0)
    glane = lax.broadcasted_iota(jnp.int32, (1, SSM_HPG * SSM_P), 1) // SSM_P
    hm = [(glane == r).astype(f32) for r in range(SSM_HPG)]
    pick = lambda v, r: jnp.sum(v * (lane == SSM_HPG * g + r).astype(f32), axis=1, keepdims=True)
    cols = [pick(acum, r) for r in range(SSM_HPG)]
    tots = [pick(tot, r) for r in range(SSM_HPG)]
    spread = lambda vals: sum(vals[r] * hm[r] for r in range(SSM_HPG))
    xdt = x * spread([pick(dt, r) for r in range(SSM_HPG)])
    cb = _bdot(Cm, Bm, NT)
    y = x * spread([pick(dsk, r) for r in range(SSM_HPG)])
    for r in range(SSM_HPG):
        row = jnp.sum(acumT * (sub == SSM_HPG * g + r).astype(f32), axis=0, keepdims=True)
        dec = jnp.exp(jnp.where(tril, cols[r] - row, -jnp.inf))
        y = y + _bdot(cb * dec, xdt * hm[r])
    y = y + _bdot(Cm, S) * spread([jnp.exp(c) for c in cols])
    dte = spread([jnp.exp(tots[r] - cols[r]) for r in range(SSM_HPG)])
    s_new = S * spread([jnp.exp(t) for t in tots]) + _bdot(Bm, xdt * dte, TN)
    return y, s_new


def _ssd_ins(xc, dtr):
    gw = SSM_HPG * SSM_P
    ins = [("row", xc, gw, g) for g in range(SSM_GROUPS)]
    ins += [("row", xc, SSM_N, D // SSM_N + g) for g in range(SSM_GROUPS)]
    ins += [("row", xc, SSM_N, D // SSM_N + SSM_GROUPS + g) for g in range(SSM_GROUPS)]
    ins += [("row", dtr, DT_PAD, 0)]
    return ins


SSD_CPS = 4


def ssd_fwd(name, T, xc, dtr, dtb, alog, dsk):
    gw = SSM_HPG * SSM_P
    cps = min(SSD_CPS, T // CHUNK)

    def body(i, n, ins, outs, accs, scr):
        S = scr[0]

        @pl.when(i == 0)
        def _():
            S[...] = jnp.zeros(S.shape, f32)
        S4 = tuple(S[:, g * gw:(g + 1) * gw] for g in range(4))
        for c in range(cps):
            rows = pl.ds(c * CHUNK, CHUNK)
            X4 = tuple(ins[g][rows, :] for g in range(4))
            B4 = tuple(ins[4 + g][rows, :] for g in range(4))
            C4 = tuple(ins[8 + g][rows, :] for g in range(4))
            prep = _ssd_prep(ins[12][rows, :], ins[13][...], ins[14][...])
            nxt = []
            for g in range(4):
                outs[1][rows, g * gw:(g + 1) * gw] = S4[g]
                y, s_new = _ssd_group(g, X4[g], B4[g], C4[g], S4[g], *prep, ins[15][...])
                outs[0][rows, g * gw:(g + 1) * gw] = y
                nxt.append(s_new)
            S4 = tuple(nxt)
        for g in range(4):
            S[:, g * gw:(g + 1) * gw] = S4[g]
    ins = _ssd_ins(xc, dtr) + [("const", dtb), ("const", alog), ("const", dsk)]
    return rowcall(name, body, T, cps * CHUNK, ins, [(D, f32), (D, f32)], scratch=[pltpu.VMEM((SSM_N, D), f32)])


def ssd_bwd(name, T, xc, dtr, sprev, d_y, dtb, alog, dsk):
    gw = SSM_HPG * SSM_P

    def body(i, n, ins, outs, accs, scr):
        dS = scr[0]

        @pl.when(i == n - 1)
        def _():
            dS[...] = jnp.zeros(dS.shape, f32)
        dS4 = tuple(dS[:, g * gw:(g + 1) * gw] for g in range(4))
        def chunk(X4, dtr_c, B4, C4, S4, dtb_c, alog_c, dsk_c):
            prep = _ssd_prep(dtr_c, dtb_c, alog_c)
            res = [_ssd_group(g, X4[g], B4[g], C4[g], S4[g], *prep, dsk_c) for g in range(4)]
            return tuple(r[0] for r in res), tuple(r[1] for r in res)
        X4 = tuple(ins[g][...] for g in range(4))
        B4 = tuple(ins[4 + g][...] for g in range(4))
        C4 = tuple(ins[8 + g][...] for g in range(4))
        S4 = tuple(ins[13 + g][...] for g in range(4))
        dY4 = tuple(ins[17 + g][...] for g in range(4))
        _, vjp = jax.vjp(chunk, X4, ins[12][...], B4, C4, S4, ins[21][...], ins[22][...], ins[23][...])
        dX4, ddtr, dB4, dC4, dS4, ddtb, dalog, ddsk = vjp((dY4, dS4))
        for g in range(4):
            outs[0][:, g * gw:(g + 1) * gw] = dX4[g]
            outs[0][:, D + g * SSM_N:D + (g + 1) * SSM_N] = dB4[g]
            outs[0][:, D + (SSM_GROUPS + g) * SSM_N:D + (SSM_GROUPS + g + 1) * SSM_N] = dC4[g]
            dS[:, g * gw:(g + 1) * gw] = dS4[g]
        outs[1][...] = ddtr.astype(bf16)
        accs[0][...] += ddtb
        accs[1][...] += dalog
        accs[2][...] += ddsk
    ins = _ssd_ins(xc, dtr) + [("row", sprev, gw, g) for g in range(4)] + [("row", d_y, gw, g) for g in range(4)]
    ins += [("const", dtb), ("const", alog), ("const", dsk)]
    return rowcall(name, body, T, CHUNK, ins, [(CONV_DIM, f32), (DT_PAD, bf16)], accs=[(1, DT_PAD)] * 3,
                   scratch=[pltpu.VMEM((SSM_N, D), f32)], reverse=True)


def _gate_group(y, z, g):
    return _rms(y * _silu(z), g)


def gate_fwd(name, T, tm, y, uvz, gn):
    def body(i, n, ins, outs, accs, scr):
        for g in range(SSM_GROUPS):
            cols = slice(g * 256, (g + 1) * 256)
            outs[0][:, cols] = _gate_group(ins[0][:, cols], ins[1][:, cols].astype(f32), ins[2][:, cols]).astype(bf16)
    return rowcall(name, body, T, tm, [("row", y, D, 0), ("row", uvz, D, 2), ("const", gn)], [(D, bf16)], sub=64)[0]


def gate_bwd(name, T, tm, y, uvz, d_yb, d_cb, gn):
    def body(i, n, ins, outs, accs, scr):
        for g in range(SSM_GROUPS):
            cols = slice(g * 256, (g + 1) * 256)
            _, vjp = jax.vjp(_gate_group, ins[0][:, cols], ins[1][:, cols].astype(f32), ins[3][:, cols])
            dy, dz, dg = vjp(ins[2][:, cols])
            outs[0][:, cols] = dy
            outs[1][:, cols] = dz.astype(bf16)
            accs[0][:, cols] += dg
    return rowcall(name, body, T, tm, [("row", y, D, 0), ("row", uvz, D, 2), ("row", d_yb, D, d_cb), ("const", gn)],
                   [(D, f32), (D, bf16)], accs=[(1, D)], sub=64)


def _window_sum(src, cols, levels, tm, lv, trailing):
    cur, cur_cols = src, cols
    for l in range(1, levels + 1):
        shift = 2 ** (l - 1)
        last = l == levels
        if trailing:
            start = POOL_HALO if last else 8 * l
            rows = tm if last else tm + POOL_HALO - start
            new = cur[pl.ds(start, rows), cur_cols] + cur[pl.ds(start - shift, rows), cur_cols]
        else:
            start = 0
            rows = tm if last else tm + POOL_HALO - 8 * l
            new = cur[pl.ds(0, rows), cur_cols] + cur[pl.ds(shift, rows), cur_cols]
        if last:
            return new
        nxt = lv[l % 2]
        nxt[pl.ds(start, rows), :] = new
        cur, cur_cols = nxt, slice(None)


def _pool_diff(i, tm, h_ref, halo_ref, g_ref, scr, lv):
    g = g_ref[...]
    yn = _rms(h_ref[...], g)
    scr[pl.ds(0, POOL_HALO), :] = jnp.where(i > 0, _rms(halo_ref[...], g), 0.0)
    scr[pl.ds(POOL_HALO, tm), :] = yn
    pos = (i * tm + lax.broadcasted_iota(jnp.int32, (tm, 1), 0) + 1).astype(f32)
    parts = []
    for gi, win in enumerate(POOL_WINDOWS):
        cols = slice(gi * POOL_GD, (gi + 1) * POOL_GD)
        s = _window_sum(scr, cols, gi + 1, tm, lv, True)
        parts.append(s * (1.0 / jnp.minimum(pos, float(win))) - yn[:, cols])
    return parts


def pool_fwd(name, T, tm, h2, g_pre, pw, pb, psc, g_post, g_next):
    def body(i, n, ins, outs, accs, scr):
        parts = _pool_diff(i, tm, ins[0], ins[1], ins[2], scr[0], scr[1:3])
        for gi in range(len(POOL_WINDOWS)):
            cols = slice(gi * POOL_GD, (gi + 1) * POOL_GD)
            o = _bdot(parts[gi], ins[3][gi]) + ins[4][:, cols]
            outs[0][:, cols] = o * ins[5][:, cols]
        h = ins[0][...] + _rms(outs[0][...], ins[6][...])
        outs[1][...] = h
        outs[2][...] = _rms(h, ins[7][...]).astype(bf16)
    return rowcall(name, body, T, tm, [("row", h2, D, 0), ("prev", h2, D, 0, POOL_HALO), ("const", g_pre), ("const", pw), ("const", pb), ("const", psc),
                                       ("const", g_post), ("const", g_next)],
                   [(D, f32), (D, f32), (D, bf16)], scratch=[pltpu.VMEM((tm + POOL_HALO, D), f32)] + [pltpu.VMEM((tm + POOL_HALO, POOL_GD), f32)] * 2)


def pool_bwd(name, T, tm, h2, d_pm, d_res, g_pre, pw, pb, psc, f_prev, g_prev):
    def body(i, n, ins, outs, accs, scr):
        parts = _pool_diff(i, tm, ins[0], ins[1], ins[5], scr[0], scr[3:5])
        dpm = ins[2][...]
        psc_v = ins[8][...]
        dps = dpm * psc_v
        dps_halo = jnp.where(i < n - 1, ins[3][...] * psc_v, 0.0)
        accs[1][...] += jnp.sum(dps, axis=0, keepdims=True)
        pos = (i * tm + lax.broadcasted_iota(jnp.int32, (tm, 1), 0) + 1).astype(f32)
        pos_h = ((i + 1) * tm + lax.broadcasted_iota(jnp.int32, (POOL_HALO, 1), 0) + 1).astype(f32)
        r_scr = scr[1]
        dyn_scr = scr[2]
        for gi, win in enumerate(POOL_WINDOWS):
            cols = slice(gi * POOL_GD, (gi + 1) * POOL_GD)
            w = ins[6][gi]
            o = _bdot(parts[gi], w) + ins[7][:, cols]
            accs[2][:, cols] += jnp.sum(dpm[:, cols] * o, axis=0, keepdims=True)
            accs[0][gi] += _bdot(parts[gi], dps[:, cols], TN)
            q = _bdot(dps[:, cols], w, NT)
            qh = _bdot(dps_halo[:, cols], w, NT)
            r_scr[pl.ds(0, tm), cols] = q * (1.0 / jnp.minimum(pos, float(win)))
            r_scr[pl.ds(tm, POOL_HALO), cols] = qh * (1.0 / jnp.minimum(pos_h, float(win)))
            dyn_scr[:, cols] = _window_sum(r_scr, cols, gi + 1, tm, scr[3:5], False) - q
        dx, dg = _rms_bwd(ins[0][...], ins[5][...], dyn_scr[...])
        dh = ins[4][...] + dx
        outs[0][...] = dh
        accs[3][...] += dg
        df, dgp = _rms_bwd(ins[9][...], ins[10][...], dh)
        outs[1][...] = df.astype(bf16)
        accs[4][...] += dgp
    ins = [("row", h2, D, 0), ("prev", h2, D, 0, POOL_HALO), ("row", d_pm, D, 0), ("next", d_pm, D, 0, POOL_HALO), ("row", d_res, D, 0),
           ("const", g_pre), ("const", pw), ("const", pb), ("const", psc), ("row", f_prev, D, 0), ("const", g_prev)]
    return rowcall(name, body, T, tm, ins, [(D, f32), (D, bf16)], accs=[(4, POOL_GD, POOL_GD), (1, D), (1, D), (1, D), (1, D)],
                   scratch=[pltpu.VMEM((tm + POOL_HALO, D), f32), pltpu.VMEM((tm + POOL_HALO, D), f32), pltpu.VMEM((tm, D), f32)]
                   + [pltpu.VMEM((tm + POOL_HALO, POOL_GD), f32)] * 2)


def local_step(T, x, tgt, W, ffn_weights, early_grads, early_grads_next, w_in_grads):
    tm = 512 if T >= 1024 else T // 2
    TKW = 4096 if T >= 4096 else T
    ng = W["norm_g"]
    g = lambda l, j: ng[l, j][None, :]
    G = {}

    tf = tm
    once = pl.Buffered(1)
    vec_f = pl.BlockSpec((1, D), lambda i: (0, 0))

    def fused_specs(t):
        rows = pl.BlockSpec((t, D), lambda i: (i, 0))
        return rows, [pl.BlockSpec((None, t, FF_SH), lambda i, s=s: (s, i, 0)) for s in range(4)], (SDS((T, D), f32), rows), (SDS((T, D), bf16), rows)
    rows_f, sh_f, out_f32, out_bf16 = fused_specs(tf)
    tf2 = min(T, 2 * tm)
    rows_f2, sh_f2, out2_f32, out2_bf16 = fused_specs(tf2)

    def resid_epilogue(with_pre):
        def ep(part, xs, os, accs):
            h = xs[0][...] + _rms(part, xs[1][...])
            os[0][...] = part
            os[1][...] = h
            if with_pre:
                os[2][...] = _rms(h, xs[2][...]).astype(bf16)
        return ep

    def bwd_epilogue(df_dtype):
        def ep(part, xs, os, accs):
            dx, dgp = _rms_bwd(xs[0][...], xs[3][...], part)
            dh = xs[2][...] + dx
            df, dgq = _rms_bwd(xs[1][...], xs[4][...], dh)
            os[0][...] = dh
            os[1][...] = df.astype(df_dtype)
            accs[0][...] += dgp
            accs[1][...] += dgq
        return ep

    def loss_epilogue(part, xs, os, accs):
        g_post = xs[2][...]
        e = xs[0][...] + _rms(part, g_post) - xs[1][...]
        accs[0][...] += jnp.sum(jnp.sum(e * e, axis=-1, keepdims=True) * (0.5 / D), axis=0, keepdims=True)
        dh = e * (1.0 / D)
        df, dg = _rms_bwd(part, g_post, dh)
        os[0][...] = dh
        os[1][...] = df.astype(bf16)
        accs[1][...] += dg

    def ffn_fwd(tag, n_bf, l, resid=None, loss=None):
        gate4, up4, act4 = ffn_up(f"ffn{tag}_up", T, min(T, 4 * tm), n_bf, W["wg4"], W["wu4"], l)
        wd_f = [pl.BlockSpec((None, FF_SH, D), lambda i, s=s: (s, l, 0), pipeline_mode=once) for s in range(4)]
        pairs = [(act4, sh_f2[s], W["wd4"], wd_f[s]) for s in range(4)]
        if loss is not None:
            return (gate4, up4, act4) + tuple(mm_fused(f"ffn{tag}_down", T // tf2, pairs, NN, [(loss[0], rows_f2), (loss[1], rows_f2), (loss[2], vec_f)],
                                                       [out2_f32, out2_bf16], [(1, 1), (1, D)], loss_epilogue))
        f, h_out = mm_fused(f"ffn{tag}_down", T // tf2, pairs, NN, [(resid[0], rows_f2), (resid[1], vec_f)], [out2_f32, out2_f32], [],
                            resid_epilogue(False))
        return gate4, up4, act4, f, h_out

    def ffn_bwd(tag, l, n_bf, gate4, up4, act4, d_f, h_out, f_pre, d_res, g_pre, g_post, df_dtype):
        d_gate4, d_up4 = ffn_dgu(f"ffn{tag}_dgu", T, min(T, 4 * tm), d_f, W["wd4"], gate4, up4, l)
        w_f = [pl.BlockSpec((None, D, FF_SH), lambda i, s=s: (s, l, 0), pipeline_mode=once) for s in range(4)]
        d_h, d_fp, dgp, dgq = mm_fused(
            f"ffn{tag}_dn", T // tf, [(d_gate4, sh_f[s], W["wg4"], w_f[s]) for s in range(4)] + [(d_up4, sh_f[s], W["wu4"], w_f[s]) for s in range(4)],
            NT, [(h_out, rows_f), (f_pre, rows_f), (d_res, rows_f), (g_pre, vec_f), (g_post, vec_f)],
            [out_f32, (SDS((T, D), df_dtype), rows_f)], [(1, D), (1, D)], bwd_epilogue(df_dtype))

        def wgrad(nm, a4, b):
            return mm(nm, (4, 1, T // TKW),
                      [(a4, pl.BlockSpec((None, TKW, FF_SH), lambda s, j, k: (s, k, 0)), b, pl.BlockSpec((TKW, D), lambda s, j, k: (k, 0)))],
                      TN, pl.BlockSpec((None, FF_SH, D), lambda s, j, k: (s, 0, 0)), SDS((4, FF_SH, D), f32))
        return d_h, d_fp, dgp, dgq, wgrad(f"ffn{tag}_dwg", d_gate4, n_bf), wgrad(f"ffn{tag}_dwu", d_up4, n_bf), wgrad(f"ffn{tag}_dwd", act4, d_f)

    y0 = rms_to_bf16("l0_prenorm", T, tf2, x, g(0, 0))
    uvz = matmul("in_uvz", [(y0, W["w_uvz"])], "nn", bf16, 4 * tm, 1024)
    xbc = matmul("in_xbc", [(y0, W["w_xbc"])], "nn", bf16, 4 * tm, 1024)
    dtr = matmul("in_dt", [(y0, W["w_dt"])], "nn", f32, 4 * tm, DT_PAD)
    y_a = gmlp_fwd("gmlp_fwd", T, tf2, uvz, W["ln_g"], W["ln_b"], W["wm"], W["bs"])
    xc = conv_fwd("conv_fwd", T, tm, xbc, W["conv_w"], W["conv_b"])
    y_ssd, sprev = ssd_fwd("ssd_fwd", T, xc, dtr, W["dtb"], W["alog"], W["dsk"])
    y_b = gate_fwd("gate_fwd", T, tf2, y_ssd, uvz, W["gn"])
    half = D // 2
    wo4 = W["wo4"]
    ycol = [pl.BlockSpec((tf2, half), lambda i, cb=cb: (i, cb)) for cb in range(2)]
    wo_s = [pl.BlockSpec((None, half, D), lambda i, s=s: (s, 0, 0), pipeline_mode=once) for s in range(4)]
    mixo, h1, n1 = mm_fused("out_proj", T // tf2, [(y_a, ycol[0], wo4, wo_s[0]), (y_a, ycol[1], wo4, wo_s[1]),
                                                  (y_b, ycol[0], wo4, wo_s[2]), (y_b, ycol[1], wo4, wo_s[3])], NN,
                            [(x, rows_f2), (g(0, 1), vec_f), (g(0, 2), vec_f)], [out2_f32, out2_f32, out2_bf16], [], resid_epilogue(True))
    W = dict(W)
    W["wg4"], W["wu4"], W["wd4"] = ffn_weights(h1)
    gate0, up0, act0, f1, h2 = ffn_fwd("0", n1, 0, resid=(h1, g(0, 3)))
    pm, h3, n3 = pool_fwd("pool_fwd", T, tm, h2, g(1, 0), W["pool_w"], W["pool_b"], W["pool_scale"], g(1, 1), g(1, 2))
    gate1, up1, act1, dh4, d_f2, loss_acc, dg13 = ffn_fwd("1", n3, 1, loss=(h3, tgt, g(1, 3)))
    d_h3, d_pm, dg12, dg11, dwg1, dwu1, dwd1 = ffn_bwd("1", 1, n3, gate1, up1, act1, d_f2, h3, pm, dh4, g(1, 2), g(1, 1), f32)
    d_h2, d_f1, G["pool_w"], G["pool_b"], G["pool_scale"], dg10, dg03 = pool_bwd("pool_bwd", T, tm, h2, d_pm, d_h3, g(1, 0), W["pool_w"], W["pool_b"],
                                                                                 W["pool_scale"], f1, g(0, 3))
    d_h1, d_mixo, dg02, dg01, dwg0, dwu0, dwd0 = ffn_bwd("0", 0, n1, gate0, up0, act0, d_f1, h1, mixo, d_h2, g(0, 2), g(0, 1), bf16)
    def d_wo(nm, y):
        return mm(nm, (2, 1, T // TKW), [(y, pl.BlockSpec((TKW, half), lambda s, j, k: (k, s)), d_mixo, pl.BlockSpec((TKW, D), lambda s, j, k: (k, 0)))],
                  TN, pl.BlockSpec((None, half, D), lambda s, j, k: (s, 0, 0)), SDS((2, half, D), f32))
    d_ycat = matmul("out_proj_dy", [(d_mixo, wo4.reshape(4 * half, D))], "nt", f32, 4 * tm, 1024)
    dwo_a, dwo_b = d_wo("out_proj_dwa", y_a), d_wo("out_proj_dwb", y_b)
    G["wo4"] = [dwo_a[0], dwo_a[1], dwo_b[0], dwo_b[1]]
    G["wgT4"], G["wuT4"], G["wd4"] = [dwg0, dwg1], [dwu0, dwu1], [dwd0, dwd1]
    token = early_grads(G)
    d_yssd, d_z, G["gn"] = gate_bwd("gate_bwd", T, tf2, y_ssd, uvz, d_ycat, 1, W["gn"] + token[0, 0])
    d_xc, d_dtr, G["dtb"], G["alog"], G["dsk"] = ssd_bwd("ssd_bwd", T, xc, dtr, sprev, d_yssd, W["dtb"], W["alog"], W["dsk"])
    token = early_grads_next(d_dtr)
    d_pre, G["conv_w"], G["conv_b"] = conv_bwd_pre("conv_bwd_pre", T, tm, xbc, d_xc, W["conv_w"], W["conv_b"] + token[0, 0])
    d_xbc = conv_bwd_x("conv_bwd_x", T, tm, d_pre, W["conv_w"])
    d_u, d_v, G["ln_g"], G["ln_b"], G["wm"], G["bs"] = gmlp_bwd("gmlp_bwd", T, tf2, uvz, d_ycat, 0, W["ln_g"], W["ln_b"], W["wm"], W["bs"])
    w_u, w_v, w_z = W["w_uvz"][:, :D], W["w_uvz"][:, D:2 * D], W["w_uvz"][:, 2 * D:]
    def pre_epilogue(part, xs, os, accs):
        dx, dg = _rms_bwd(xs[0][...], xs[2][...], part)
        os[0][...] = xs[1][...] + dx
        accs[0][...] += dg
    blk = lambda w: pl.BlockSpec((tf, w), lambda i: (i, 0))
    whole = lambda a: pl.BlockSpec(a.shape, lambda i: (0, 0), pipeline_mode=once)
    w_in_t = [matmul("in_dwu", [(d_u, y0)], "tn", f32, 1024, 1024, TKW), matmul("in_dwv", [(d_v, y0)], "tn", f32, 1024, 1024, TKW),
              matmul("in_dwz", [(d_z, y0)], "tn", f32, 1024, 1024, TKW), matmul("in_dwxbc", [(d_xbc, y0)], "tn", f32, 1024, 1024, TKW),
              matmul("in_dwdt", [(d_dtr, y0)], "tn", f32, DT_PAD, 1024, TKW)[:N_HEADS]]
    token = w_in_grads(w_in_t)
    grad_x, dg00 = mm_fused("in_dy0", T // tf, [(d_u, blk(D), w_u, whole(w_u)), (d_v, blk(D), w_v, whole(w_v)), (d_z, blk(D), w_z, whole(w_z)),
                                                (d_xbc, blk(CONV_DIM), W["w_xbc"], whole(W["w_xbc"])), (d_dtr, blk(DT_PAD), W["w_dt"], whole(W["w_dt"]))],
                            NT, [(x, rows_f), (d_h1, rows_f), (g(0, 0) + token[0, 0], vec_f)], [out_f32], [(1, D)], pre_epilogue)
    G["norm_g"] = jnp.stack([jnp.concatenate([dg00, dg01, dg02, dg03], 0), jnp.concatenate([dg10, dg11, dg12, dg13], 0)])
    return loss_acc, grad_x, G


def build_weights(Wf):
    causal = jnp.tril(jnp.ones((CHUNK, CHUNK), bool))
    w_in = Wf["w_in"].astype(bf16)
    pad16 = lambda v: jnp.pad(v.reshape(1, N_HEADS).astype(f32), ((0, 0), (0, DT_PAD - N_HEADS)))
    return {
        "norm_g": Wf["norm_g"],
        "w_uvz": w_in[:, :3 * D], "w_xbc": w_in[:, 3 * D:3 * D + CONV_DIM],
        "w_dt": jnp.pad(w_in[:, 3 * D + CONV_DIM:], ((0, 0), (0, DT_PAD - N_HEADS))),
        "ln_g": Wf["gm_ln_g"].reshape(1, D), "ln_b": Wf["gm_ln_b"].reshape(1, D),
        "wm": jnp.where(causal[None], Wf["gm_ws"], 0).astype(bf16), "bs": Wf["gm_bs"].reshape(GM_HEADS, CHUNK, 1),
        "conv_w": Wf["conv_w"], "conv_b": Wf["conv_b"].reshape(1, CONV_DIM),
        "dtb": pad16(Wf["dt_bias"]), "alog": pad16(Wf["a_log"]), "dsk": pad16(Wf["d_skip"]),
        "gn": Wf["ssm_norm_g"].reshape(1, D),
        "wo4": Wf["wo4"].astype(bf16),
        "pool_w": Wf["pool_w"].astype(bf16), "pool_b": Wf["pool_b"].reshape(1, D), "pool_scale": Wf["pool_scale"].reshape(1, D),
    }


def small_grads(G):
    return {
        "norm_g": G["norm_g"],
        "gm_ln_g": G["ln_g"].reshape(D), "gm_ln_b": G["ln_b"].reshape(D),
        "gm_ws": G["wm"], "gm_bs": G["bs"].reshape(GM_HEADS, CHUNK),
        "conv_w": G["conv_w"], "conv_b": G["conv_b"].reshape(CONV_DIM),
        "dt_bias": G["dtb"][0, :N_HEADS], "a_log": G["alog"][0, :N_HEADS], "d_skip": G["dsk"][0, :N_HEADS],
        "ssm_norm_g": G["gn"].reshape(D),
        "pool_b": G["pool_b"].reshape(4, POOL_GD), "pool_scale": G["pool_scale"].reshape(D),
    }


MESH_ID = pl.DeviceIdType.MESH
ANY = pl.BlockSpec(memory_space=pl.ANY)


DMA_CHUNK_BYTES = 2 << 20
DMA_MAX_CHUNKS = 32


def _pieces(view, axis, align):
    shape = view.shape
    nbytes = math.prod(shape) * jnp.dtype(view.dtype).itemsize
    n = max(1, min(DMA_MAX_CHUNKS, -(-nbytes // DMA_CHUNK_BYTES)))
    rows = shape[axis]
    size = -(-rows // n)
    size = -(-size // align) * align
    out = []
    for s in range(0, rows, size):
        idx = [slice(None)] * len(shape)
        idx[axis] = pl.ds(s, min(size, rows - s))
        out.append(tuple(idx))
    return out


def comm_call(name, operands, out_shapes, plan):
    n_in = len(operands)
    n_out = len(out_shapes)
    n_remote, n_local = plan((0, 0, 0), [None] * n_in, [None] * n_out, True)

    def body(*refs):
        in_refs, out_refs = refs[:n_in], refs[n_in:n_in + n_out]
        send_sems, recv_sems, local_sems = refs[n_in + n_out:]
        me = (lax.axis_index("x"), lax.axis_index("y"), lax.axis_index("c"))
        remote, local = plan(me, in_refs, out_refs, False)
        align = lambda v: 16 if v.dtype == bf16 else 8
        for j, (s, d, axis) in enumerate(local):
            for ix in _pieces(s, axis, align(s)):
                pltpu.make_async_copy(s.at[ix], d.at[ix], local_sems.at[j]).start()
        peers = [tuple((1 - m) if f else m for m, f in zip(me, flip)) for flip, *_ in remote]
        for k, (flip, src, dst, _, axis) in enumerate(remote):
            for ix in _pieces(src, axis, align(src)):
                pltpu.make_async_remote_copy(src_ref=src.at[ix], dst_ref=dst.at[ix], send_sem=send_sems.at[k], recv_sem=recv_sems.at[k],
                                             device_id=peers[k], device_id_type=MESH_ID).start()
        for k, (flip, src, dst, landing, axis) in enumerate(remote):
            pltpu.make_async_remote_copy(src_ref=landing, dst_ref=landing, send_sem=send_sems.at[k], recv_sem=recv_sems.at[k],
                                         device_id=peers[k], device_id_type=MESH_ID).wait_recv()
        for k, (flip, src, dst, landing, axis) in enumerate(remote):
            pltpu.make_async_remote_copy(src_ref=src, dst_ref=dst, send_sem=send_sems.at[k], recv_sem=recv_sems.at[k],
                                         device_id=peers[k], device_id_type=MESH_ID).wait_send()
        for j, (s, d, axis) in enumerate(local):
            pltpu.make_async_copy(s, d, local_sems.at[j]).wait()

    return pl.pallas_call(
        body, name=name, out_shape=list(out_shapes), in_specs=[ANY] * n_in, out_specs=[ANY] * n_out,
        scratch_shapes=[pltpu.SemaphoreType.DMA((n_remote,)), pltpu.SemaphoreType.DMA((n_remote,)), pltpu.SemaphoreType.DMA((max(n_local, 1),))],
    )(*operands)


CHIP_FLIPS = ((1, 0, 0), (0, 1, 0), (1, 1, 0))
PAIR_FLIP = (0, 0, 1)


def gather_two_level(name, halved, whole):
    nh, nw = len(halved), len(whole)
    nf = len(CHIP_FLIPS)

    def body(*refs):
        srcs, outs = refs[:nh + nw], refs[nh + nw:2 * (nh + nw)]
        send_sems, recv_sems, fwd_send, fwd_recv = refs[2 * (nh + nw):]
        me = (lax.axis_index("x"), lax.axis_index("y"), lax.axis_index("c"))
        k, c = 2 * me[0] + me[1], me[2]
        sibling = (me[0], me[1], 1 - c)
        peers = [tuple((1 - m) if fl else m for m, fl in zip(me, flip)) for flip in CHIP_FLIPS]

        def half(ref, which):
            rh = ref.shape[0] // 2
            return ref.at[pl.ds(pl.multiple_of(which * rh, 16), rh), :]

        def ici(a, f):
            src = half(srcs[a], c) if a < nh else srcs[a]
            dst = half(outs[a].at[k], c) if a < nh else outs[a].at[k]
            return pltpu.make_async_remote_copy(src_ref=src, dst_ref=dst, send_sem=send_sems.at[a * nf + f], recv_sem=recv_sems.at[a * nf + f],
                                                device_id=peers[f], device_id_type=MESH_ID)

        def landed(a, f):
            slot = outs[a].at[_chip_of(me, CHIP_FLIPS[f])]
            return half(slot, c) if a < nh else slot

        def forward(a, f, which):
            v = half(outs[a].at[_chip_of(me, CHIP_FLIPS[f])], which)
            return pltpu.make_async_remote_copy(src_ref=v, dst_ref=v, send_sem=fwd_send.at[a * nf + f], recv_sem=fwd_recv.at[a * nf + f],
                                                device_id=sibling, device_id_type=MESH_ID)

        copies = [ici(a, f) for a in range(nh + nw) for f in range(nf)]
        for cp in copies:
            cp.start()
        fwds = []
        for a in range(nh):
            for f in range(nf):
                lv = landed(a, f)
                pltpu.make_async_remote_copy(src_ref=lv, dst_ref=lv, send_sem=send_sems.at[a * nf + f], recv_sem=recv_sems.at[a * nf + f],
                                             device_id=peers[f], device_id_type=MESH_ID).wait_recv()
                fw = forward(a, f, c)
                fw.start()
                fwds.append(fw)
        for a in range(nh, nh + nw):
            for f in range(nf):
                lv = landed(a, f)
                pltpu.make_async_remote_copy(src_ref=lv, dst_ref=lv, send_sem=send_sems.at[a * nf + f], recv_sem=recv_sems.at[a * nf + f],
                                             device_id=peers[f], device_id_type=MESH_ID).wait_recv()
        for a in range(nh):
            for f in range(nf):
                forward(a, f, 1 - c).wait_recv()
        for fw in fwds:
            fw.wait_send()
        for cp in copies:
            cp.wait_send()

    arrs = list(halved) + list(whole)
    n_ici = (nh + nw) * nf
    return pl.pallas_call(
        body, name=name, out_shape=[SDS((N_CHIPS,) + a.shape, a.dtype) for a in arrs], in_specs=[ANY] * len(arrs), out_specs=[ANY] * len(arrs),
        scratch_shapes=[pltpu.SemaphoreType.DMA((n_ici,)), pltpu.SemaphoreType.DMA((n_ici,)),
                        pltpu.SemaphoreType.DMA((nh * nf,)), pltpu.SemaphoreType.DMA((nh * nf,))],
    )(*arrs)


def pair_split_exchange(name, p, rh):
    def plan(me, ins, outs, count):
        if count:
            return 1, 0
        theirs = ins[0].at[:, pl.ds(pl.multiple_of((1 - me[2]) * rh, 8), rh), :]
        return [(PAIR_FLIP, theirs, outs[0], outs[0], 1)], []
    return comm_call(name, [p], [SDS((4, rh, p.shape[2]), p.dtype)], plan)[0]


def scatter_over_chips(name, cs):
    def plan(me, ins, outs, count):
        if count:
            return len(CHIP_FLIPS), 0
        k = 2 * me[0] + me[1]
        remote = []
        for flip in CHIP_FLIPS:
            kp = 2 * ((1 - me[0]) if flip[0] else me[0]) + ((1 - me[1]) if flip[1] else me[1])
            remote.append((flip, ins[0].at[kp], outs[0].at[k], outs[0].at[kp], 0))
        return remote, []
    return comm_call(name, [cs], [SDS(cs.shape, cs.dtype)], plan)[0]


def pair_swap(name, half):
    def plan(me, ins, outs, count):
        if count:
            return 1, 0
        return [(PAIR_FLIP, ins[0], outs[0], outs[0], 0)], []
    return comm_call(name, [half], [SDS(half.shape, half.dtype)], plan)[0]


def _row_tile(rows, cap=512):
    if rows <= cap:
        return rows
    t = cap - cap % 8
    while rows % t:
        t -= 8
    return t


def pair_sum(name, packs, got, c_arr, tile):
    rh = got.shape[1]
    nb = rh // tile

    def kern(c_ref, a_ref, b_ref, o16_ref):
        o16_ref[...] = (a_ref[...] + b_ref[...]).astype(bf16)
    blk = (None, tile, D)
    grid_spec = pltpu.PrefetchScalarGridSpec(
        num_scalar_prefetch=1, grid=(4, nb),
        in_specs=[pl.BlockSpec(blk, lambda s, i, c: (s, c[0] * nb + i, 0)), pl.BlockSpec(blk, lambda s, i, c: (s, i, 0))],
        out_specs=pl.BlockSpec(blk, lambda s, i, c: (s, i, 0)))
    return pl.pallas_call(kern, name=name, grid_spec=grid_spec, out_shape=SDS(got.shape, bf16),
                          compiler_params=pltpu.CompilerParams(dimension_semantics=("parallel", "parallel")))(c_arr, packs, got)


def chip_sum(name, own16, landed16, k_arr, tile):
    rh = own16.shape[1]
    nb = rh // tile

    def kern(k_ref, own_ref, l0, l1, l2, l3, o_ref):
        k = k_ref[0]
        s = None
        for j, lref in enumerate((l0, l1, l2, l3)):
            t = jnp.where(k == j, own_ref[...], lref[...]).astype(f32)
            s = t if s is None else s + t
        o_ref[...] = s
    blk = (None, tile, D)
    land = [pl.BlockSpec(blk, lambda i, k, j=j: (jnp.where(k[0] == j, (j + 1) % N_CHIPS, j), i, 0)) for j in range(N_CHIPS)]
    grid_spec = pltpu.PrefetchScalarGridSpec(
        num_scalar_prefetch=1, grid=(nb,),
        in_specs=[pl.BlockSpec(blk, lambda i, k: (k[0], i, 0))] + land,
        out_specs=pl.BlockSpec((tile, D), lambda i, k: (i, 0)))
    return pl.pallas_call(kern, name=name, grid_spec=grid_spec, out_shape=SDS((rh, D), f32),
                          compiler_params=pltpu.CompilerParams(dimension_semantics=("parallel",)))(k_arr, own16, landed16, landed16, landed16, landed16)


def adamw(name, w, g, m, v):
    R, C = w.shape
    tr = _row_tile(R, 256)

    def kern(w_ref, g_ref, m_ref, v_ref, d_ref, mo_ref, vo_ref):
        gg = g_ref[...]
        mn = ADAM_B1 * m_ref[...] + (1.0 - ADAM_B1) * gg
        vn = ADAM_B2 * v_ref[...] + (1.0 - ADAM_B2) * jnp.square(gg)
        m_hat = mn / (1.0 - ADAM_B1 ** ADAM_STEP)
        v_hat = vn / (1.0 - ADAM_B2 ** ADAM_STEP)
        d_ref[...] = -ADAM_LR * (m_hat / (jnp.sqrt(v_hat) + ADAM_EPS) + ADAM_WD * w_ref[...])
        mo_ref[...] = mn
        vo_ref[...] = vn
    spec = pl.BlockSpec((tr, C), lambda i: (i, 0))
    s = SDS((R, C), f32)
    return pl.pallas_call(kern, name=name, grid=(R // tr,), in_specs=[spec] * 4, out_specs=[spec] * 3, out_shape=[s, s, s],
                          compiler_params=pltpu.CompilerParams(dimension_semantics=("parallel",)))(w, g, m, v)


WEIGHT_NAMES = ("norm_g", "w_in", "gm_ln_g", "gm_ln_b", "gm_ws", "gm_bs", "conv_w", "conv_b", "dt_bias", "a_log", "d_skip",
                "ssm_norm_g", "w_out", "pool_w", "pool_b", "pool_scale", "ffn_w_gate", "ffn_w_up", "ffn_w_down")
SMALL = ("norm_g", "conv_w", "pool_b", "pool_scale")
REPL = ("gm_ln_g", "gm_ln_b", "gm_ws", "gm_bs", "conv_b", "dt_bias", "a_log", "d_skip", "ssm_norm_g")
SMALL_AXIS = {"norm_g": 2, "conv_w": 1, "pool_b": 1, "pool_scale": 0}
N_CHIPS = 4
IN_SH = IN_DIM // N_CHIPS
SMALL_ROWS = 8
REPL_ROWS = 72
E_OUT, E_GATE, E_UP, E_DOWN = 0, 512, 512 + 2 * FF_SH, 512 + 4 * FF_SH
E_POOL = E_DOWN + 2 * FF_SH
E_ROWS, E_TILE = E_POOL + 64, 400
L_SMALL, L_REPL = 0, SMALL_ROWS
L_ROWS, L_TILE = 96, 48
W_ROWS, W_TILE = 1408, 352


def _flat_rows(pieces, rows):
    v = jnp.concatenate([p.reshape(-1) for p in pieces])
    return jnp.pad(v, (0, rows * D - v.shape[0])).reshape(rows, D)


def _shard_small(name, full, k):
    ax = SMALL_AXIS[name]
    n = full.shape[ax] // N_CHIPS
    return lax.slice_in_dim(full, k * n, (k + 1) * n, axis=ax)


def _drop1(name, a):
    return a if name == "norm_g" else a[0]


HBM_SPEC = pl.BlockSpec(memory_space=pltpu.HBM)
SEM_SPEC = pl.BlockSpec(memory_space=pltpu.SEMAPHORE)
SPLIT_EFFECT = pltpu.SideEffectType.DATAFLOW_SIDE_EFFECTING


def _chip_of(me, flip):
    return 2 * ((1 - me[0]) if flip[0] else me[0]) + ((1 - me[1]) if flip[1] else me[1])


def gather_start(name, arrs, after, slotted=False):
    n = len(arrs)
    ncp = n * len(CHIP_FLIPS)

    def body(*refs):
        srcs, lands = refs[:n], refs[n:2 * n]
        send_sems, recv_sems, token = refs[2 * n + 1], refs[2 * n + 2], refs[-1]
        me = (lax.axis_index("x"), lax.axis_index("y"), lax.axis_index("c"))
        k = 2 * me[0] + me[1]
        for a in range(n):
            for f, flip in enumerate(CHIP_FLIPS):
                peer = tuple((1 - m) if fl else m for m, fl in zip(me, flip))
                src = srcs[a].at[_chip_of(me, flip)] if slotted else srcs[a]
                for ix in _pieces(src, 0, 16):
                    pltpu.make_async_remote_copy(src_ref=src.at[ix], dst_ref=lands[a].at[k].at[ix],
                                                 send_sem=send_sems.at[a * len(CHIP_FLIPS) + f], recv_sem=recv_sems.at[a * len(CHIP_FLIPS) + f],
                                                 device_id=peer, device_id_type=MESH_ID).start()
        token[...] = jnp.zeros_like(token)

    land_shapes = [a.shape if slotted else (N_CHIPS,) + a.shape for a in arrs]
    operands = [pltpu.with_memory_space_constraint(a, pltpu.HBM) for a in arrs]
    operands += [pltpu.with_memory_space_constraint(lax.empty(s, a.dtype), pltpu.HBM) for s, a in zip(land_shapes, arrs)]
    out = pl.pallas_call(
        body, name=name,
        out_shape=(pltpu.SemaphoreType.DMA((ncp,)), pltpu.SemaphoreType.DMA((ncp,)), *[pltpu.HBM(a.shape, a.dtype) for a in arrs],
                   *[pltpu.HBM(s, a.dtype) for s, a in zip(land_shapes, arrs)], SDS((8, 128), f32)),
        in_specs=[HBM_SPEC] * (2 * n) + [ANY], out_specs=(SEM_SPEC, SEM_SPEC, *[HBM_SPEC] * (2 * n), pl.BlockSpec(memory_space=pltpu.VMEM)),
        input_output_aliases={i: 2 + i for i in range(2 * n)},
        compiler_params=pltpu.CompilerParams(has_side_effects=SPLIT_EFFECT),
    )(*operands, after)
    return out[0], out[1], out[2:2 + n], out[2 + n:2 + 2 * n], out[-1]


def gather_wait(name, send_sems, recv_sems, thru, lands, after, slotted=False):
    n = len(thru)

    def body(*refs):
        srcs, lands_r = refs[:n], refs[n:2 * n]
        s_sems, r_sems = refs[2 * n], refs[2 * n + 1]
        me = (lax.axis_index("x"), lax.axis_index("y"), lax.axis_index("c"))
        k = 2 * me[0] + me[1]
        for a in range(n):
            for f, flip in enumerate(CHIP_FLIPS):
                peer = tuple((1 - m) if fl else m for m, fl in zip(me, flip))
                idx = a * len(CHIP_FLIPS) + f
                src = srcs[a].at[_chip_of(me, flip)] if slotted else srcs[a]
                pltpu.make_async_remote_copy(src_ref=src, dst_ref=lands_r[a].at[k], send_sem=s_sems.at[idx], recv_sem=r_sems.at[idx],
                                             device_id=peer, device_id_type=MESH_ID).wait_send()
                pltpu.make_async_remote_copy(src_ref=src, dst_ref=lands_r[a].at[_chip_of(me, flip)], send_sem=s_sems.at[idx],
                                             recv_sem=r_sems.at[idx], device_id=peer, device_id_type=MESH_ID).wait_recv()

    out = pl.pallas_call(
        body, name=name, out_shape=tuple(pltpu.HBM(t.shape, t.dtype) for t in (*thru, *lands)),
        in_specs=[HBM_SPEC] * (2 * n) + [SEM_SPEC, SEM_SPEC, ANY], out_specs=tuple([HBM_SPEC] * (2 * n)),
        input_output_aliases={i: i for i in range(2 * n)},
        compiler_params=pltpu.CompilerParams(has_side_effects=SPLIT_EFFECT),
    )(*thru, *lands, send_sems, recv_sems, after)
    return out[:n], out[n:]


def pair_start(name, packs, rh):
    land_shape = (packs.shape[0], rh, packs.shape[2])

    def body(p_ref, land_ref, send_sem, recv_sem, p_thru, land_thru, token):
        me = (lax.axis_index("x"), lax.axis_index("y"), lax.axis_index("c"))
        theirs = p_ref.at[:, pl.ds(pl.multiple_of((1 - me[2]) * rh, 8), rh), :]
        for ix in _pieces(theirs, 1, 8):
            pltpu.make_async_remote_copy(src_ref=theirs.at[ix], dst_ref=land_ref.at[ix], send_sem=send_sem, recv_sem=recv_sem,
                                         device_id=(me[0], me[1], 1 - me[2]), device_id_type=MESH_ID).start()
        token[...] = jnp.zeros_like(token)

    return pl.pallas_call(
        body, name=name,
        out_shape=(pltpu.SemaphoreType.DMA(()), pltpu.SemaphoreType.DMA(()), pltpu.HBM(packs.shape, packs.dtype), pltpu.HBM(land_shape, packs.dtype),
                   SDS((8, 128), f32)),
        in_specs=[HBM_SPEC, HBM_SPEC], out_specs=(SEM_SPEC, SEM_SPEC, HBM_SPEC, HBM_SPEC, pl.BlockSpec(memory_space=pltpu.VMEM)),
        input_output_aliases={0: 2, 1: 3}, compiler_params=pltpu.CompilerParams(has_side_effects=SPLIT_EFFECT),
    )(pltpu.with_memory_space_constraint(packs, pltpu.HBM), pltpu.with_memory_space_constraint(lax.empty(land_shape, packs.dtype), pltpu.HBM))


def pair_wait(name, send_sem, recv_sem, packs, land, after):
    rh = land.shape[1]

    def body(p_ref, land_ref, s_sem, r_sem, after_ref, p_out, land_out):
        me = (lax.axis_index("x"), lax.axis_index("y"), lax.axis_index("c"))
        theirs = p_ref.at[:, pl.ds(pl.multiple_of((1 - me[2]) * rh, 8), rh), :]
        cp = pltpu.make_async_remote_copy(src_ref=theirs, dst_ref=land_ref, send_sem=s_sem, recv_sem=r_sem,
                                          device_id=(me[0], me[1], 1 - me[2]), device_id_type=MESH_ID)
        cp.wait_send()
        cp.wait_recv()

    return pl.pallas_call(
        body, name=name, out_shape=(pltpu.HBM(packs.shape, packs.dtype), pltpu.HBM(land.shape, land.dtype)),
        in_specs=[HBM_SPEC, HBM_SPEC, SEM_SPEC, SEM_SPEC, ANY], out_specs=(HBM_SPEC, HBM_SPEC), input_output_aliases={0: 0, 1: 1},
        compiler_params=pltpu.CompilerParams(has_side_effects=SPLIT_EFFECT),
    )(packs, land, send_sem, recv_sem, after)


def gather_weights(w_sh):
    big = [w_sh["w_in"][0], w_sh["w_out"][0], w_sh["pool_w"][0].reshape(4 * 64, POOL_GD)]
    small_pack = _flat_rows([w_sh[n] for n in SMALL], SMALL_ROWS)
    own = [b.astype(bf16) for b in big] + [small_pack]
    my_k = 2 * lax.axis_index("x") + lax.axis_index("y")
    s_in, s_out, s_pool, s_small = [lax.dynamic_update_slice(s, o[None], (my_k, 0, 0))
                                    for s, o in zip(gather_two_level("gather_weights", own[:3], own[3:]), own)]
    Wf = {n: w_sh[n][0] for n in REPL}
    Wf["w_in"] = s_in.transpose(1, 0, 2).reshape(D, IN_DIM)
    Wf["pool_w"] = s_pool.reshape(N_CHIPS, 4, 64, POOL_GD).transpose(1, 0, 2, 3).reshape(4, POOL_GD, POOL_GD)
    Wf["wo4"] = s_out
    small_shapes = [_drop1(n, w_sh[n]).shape for n in SMALL]
    parts = [_split_rows(s_small[k], small_shapes) for k in range(N_CHIPS)]
    for j, n in enumerate(SMALL):
        Wf[n] = jnp.concatenate([parts[k][j] for k in range(N_CHIPS)], axis=SMALL_AXIS[n])
    return Wf


def pack_early(G):
    slots = [jnp.concatenate([G["wo4"][k], G["wgT4"][0][k], G["wgT4"][1][k], G["wuT4"][0][k], G["wuT4"][1][k], G["wd4"][0][k], G["wd4"][1][k],
                              G["pool_w"][:, k * 64:(k + 1) * 64, :].reshape(64, D)], axis=0) for k in range(N_CHIPS)]
    return jnp.stack(slots)


def pack_w_in(pieces):
    w_in_t = jnp.concatenate(pieces, axis=0)
    return jnp.stack([jnp.pad(w_in_t[k * IN_SH:(k + 1) * IN_SH], ((0, W_ROWS - IN_SH), (0, 0))) for k in range(N_CHIPS)])


def pack_late(G):
    sg = small_grads(G)
    repl = _flat_rows([sg[n] for n in REPL], L_ROWS - SMALL_ROWS)
    return jnp.stack([jnp.concatenate([_flat_rows([_shard_small(n, sg[n], k) for n in SMALL], SMALL_ROWS), repl], axis=0)
                      for k in range(N_CHIPS)])


def unpack_grads(early, w_in_t, late, w_sh):
    g = {"w_out": early[E_OUT:E_GATE], "ffn_w_down": early[E_DOWN:E_POOL], "pool_w": early[E_POOL:E_ROWS],
         "ffn_w_gate": jnp.stack([early[E_GATE + l * FF_SH:E_GATE + (l + 1) * FF_SH].T for l in range(2)]),
         "ffn_w_up": jnp.stack([early[E_UP + l * FF_SH:E_UP + (l + 1) * FF_SH].T for l in range(2)]),
         "w_in": w_in_t[:IN_SH].T}
    small = _split_rows(late[L_SMALL:L_REPL], [_drop1(n, w_sh[n]).shape for n in SMALL])
    repl = _split_rows(late[L_REPL:L_REPL + REPL_ROWS], [w_sh[n][0].shape for n in REPL])
    g.update(zip(SMALL, small))
    g.update(zip(REPL, repl))
    return {n: g[n].reshape(w_sh[n].shape) for n in WEIGHT_NAMES}


def _split_rows(flat2d, shapes):
    v = flat2d.reshape(-1)
    out, off = [], 0
    for s in shapes:
        n = math.prod(s)
        out.append(v[off:off + n].reshape(s))
        off += n
    return out


def kernel(x, norm_g, w_in, gm_ln_g, gm_ln_b, gm_ws, gm_bs, conv_w, conv_b, dt_bias, a_log, d_skip, ssm_norm_g, w_out, pool_w, pool_b, pool_scale, ffn_w_gate, ffn_w_up, ffn_w_down, loss_target, m_norm_g, m_w_in, m_gm_ln_g, m_gm_ln_b, m_gm_ws, m_gm_bs, m_conv_w, m_conv_b, m_dt_bias, m_a_log, m_d_skip, m_ssm_norm_g, m_w_out, m_pool_w, m_pool_b, m_pool_scale, m_ffn_w_gate, m_ffn_w_up, m_ffn_w_down, v_norm_g, v_w_in, v_gm_ln_g, v_gm_ln_b, v_gm_ws, v_gm_bs, v_conv_w, v_conv_b, v_dt_bias, v_a_log, v_d_skip, v_ssm_norm_g, v_w_out, v_pool_w, v_pool_b, v_pool_scale, v_ffn_w_gate, v_ffn_w_up, v_ffn_w_down):
    T = x.shape[1]
    w_sh = dict(zip(WEIGHT_NAMES, (norm_g, w_in, gm_ln_g, gm_ln_b, gm_ws, gm_bs, conv_w, conv_b, dt_bias, a_log, d_skip, ssm_norm_g, w_out,
                                   pool_w, pool_b, pool_scale, ffn_w_gate, ffn_w_up, ffn_w_down)))
    m_sh = dict(zip(WEIGHT_NAMES, (m_norm_g, m_w_in, m_gm_ln_g, m_gm_ln_b, m_gm_ws, m_gm_bs, m_conv_w, m_conv_b, m_dt_bias, m_a_log, m_d_skip,
                                   m_ssm_norm_g, m_w_out, m_pool_w, m_pool_b, m_pool_scale, m_ffn_w_gate, m_ffn_w_up, m_ffn_w_down)))
    v_sh = dict(zip(WEIGHT_NAMES, (v_norm_g, v_w_in, v_gm_ln_g, v_gm_ln_b, v_gm_ws, v_gm_bs, v_conv_w, v_conv_b, v_dt_bias, v_a_log, v_d_skip,
                                   v_ssm_norm_g, v_w_out, v_pool_w, v_pool_b, v_pool_scale, v_ffn_w_gate, v_ffn_w_up, v_ffn_w_down)))

    my_k = 2 * lax.axis_index("x") + lax.axis_index("y")
    ffn_own = [w_sh["ffn_w_gate"].reshape(2 * D, FF_SH).astype(bf16), w_sh["ffn_w_up"].reshape(2 * D, FF_SH).astype(bf16),
               w_sh["ffn_w_down"].reshape(2 * FF_SH, D).astype(bf16)]
    Wf = gather_weights(w_sh)
    send_sems, recv_sems, thru, lands, token = gather_start("gather_ffn_start", ffn_own, Wf["wo4"])
    Wf["norm_g"] = Wf["norm_g"] + token[0, 0]
    W = build_weights(Wf)

    def ffn_weights(after):
        _, landed = gather_wait("gather_ffn_wait", send_sems, recv_sems, thru, lands, after)
        return tuple(lax.dynamic_update_slice(l, o[None], (my_k, 0, 0)) for l, o in zip(landed, ffn_own))

    my_c = lax.axis_index("c")
    c_arr = my_c.astype(jnp.int32).reshape(1)
    k_arr = my_k.astype(jnp.int32).reshape(1)

    def pair_stage(tag, packs, tile):
        got = pair_split_exchange(f"grads{tag}_pair_split", packs, packs.shape[1] // 2)
        return pair_sum(f"grads{tag}_pair_sum", packs, got, c_arr, tile)

    def chip_stage(tag, pair16, landed, tile):
        half = chip_sum(f"grads{tag}_chip_sum", pair16, landed, k_arr, tile)
        other = pair_swap(f"grads{tag}_pair_swap", half)
        return jnp.concatenate([jnp.where(my_c == 0, half, other), jnp.where(my_c == 0, other, half)], axis=0)

    early = {}

    def early_grads(Ge):
        *pair, tok = pair_start("gradsE_pair_start", pack_early(Ge), E_ROWS // 2)
        early.update(pair=pair)
        return tok

    def early_grads_next(after):
        packs, got = pair_wait("gradsE_pair_wait", *early["pair"], after)
        pair16 = pair_sum("gradsE_pair_sum", packs, got, c_arr, E_TILE)
        s_sems, r_sems, thru, lands, tok = gather_start("gradsE_scatter_start", [pair16], jnp.zeros((8, 128), f32), slotted=True)
        early.update(s_sems=s_sems, r_sems=r_sems, thru=thru, lands=lands)
        return tok

    w_group = {}

    def w_in_grads(pieces):
        pair16 = pair_stage("W", pack_w_in(pieces), W_TILE)
        s_sems, r_sems, thru, lands, tok = gather_start("gradsW_scatter_start", [pair16], jnp.zeros((8, 128), f32), slotted=True)
        w_group.update(s_sems=s_sems, r_sems=r_sems, thru=thru, lands=lands)
        return tok

    loss_acc, grad_x, G = local_step(T, x[0], loss_target[0], W, ffn_weights, early_grads, early_grads_next, w_in_grads)
    pair_l = pair_stage("L", pack_late(G), L_TILE)
    total_l = chip_stage("L", pair_l, scatter_over_chips("gradsL_scatter", pair_l), L_TILE)
    (pair_w,), (landed_w,) = gather_wait("gradsW_scatter_wait", w_group["s_sems"], w_group["r_sems"], w_group["thru"], w_group["lands"], total_l,
                                         slotted=True)
    total_w = chip_stage("W", pair_w, landed_w, W_TILE)
    (pair_e,), (landed_e,) = gather_wait("gradsE_scatter_wait", early["s_sems"], early["r_sems"], early["thru"], early["lands"], total_w,
                                         slotted=True)
    total_e = chip_stage("E", pair_e, landed_e, E_TILE)
    grads = unpack_grads(total_e, total_w, total_l, w_sh)

    delta, new_m, new_v = {}, {}, {}
    for n in WEIGHT_NAMES:
        shp = w_sh[n].shape
        two_d = (-1, shp[-1])
        d_, m_, v_ = adamw("adamw_" + n, w_sh[n].reshape(two_d), grads[n].reshape(two_d), m_sh[n].reshape(two_d), v_sh[n].reshape(two_d))
        delta[n], new_m[n], new_v[n] = d_.reshape(shp), m_.reshape(shp), v_.reshape(shp)

    loss = lax.psum(loss_acc[0, 0], ("x", "y", "c"))
    return (loss, grad_x[None], *[grads[n] for n in WEIGHT_NAMES], *[delta[n] for n in WEIGHT_NAMES],
            *[new_m[n] for n in WEIGHT_NAMES], *[new_v[n] for n in WEIGHT_NAMES])
```

```python
import math

import jax
import jax.numpy as jnp
from jax import lax
from jax.experimental import pallas as pl
from jax.experimental.pallas import tpu as pltpu

f32, bf16 = jnp.float32, jnp.bfloat16
SDS = jax.ShapeDtypeStruct

D = 1024
EPS = 1e-6
CHUNK = 128
GM_HEADS, GM_HD = 4, 256
SSM_GROUPS, SSM_HPG, SSM_P, SSM_N = 4, 4, 64, 128
N_HEADS = SSM_GROUPS * SSM_HPG
CONV_K = 4
CONV_DIM = 2048
POOL_WINDOWS = (2, 4, 8, 16)
POOL_GD = 256
POOL_HALO = 32
CONV_HALO = 16
D_FF = 2816
DT_PAD = 128
IN_DIM = 5136

ADAM_LR, ADAM_B1, ADAM_B2, ADAM_EPS, ADAM_WD, ADAM_STEP = 0.001, 0.9, 0.999, 1e-08, 0.01, 10

NT = (((1,), (1,)), ((), ()))
TN = (((0,), (0,)), ((), ()))
NN = (((1,), (0,)), ((), ()))
HI = lax.Precision.HIGHEST
MM_SUB = 256


def _silu(x):
    return x * jax.nn.sigmoid(x)


def _softplus(x):
    return jnp.maximum(x, 0.0) + jnp.log1p(jnp.exp(-jnp.abs(x)))


def _rms(x, g):
    return x * lax.rsqrt(jnp.mean(x * x, axis=-1, keepdims=True) + EPS) * g


def _rms_bwd(x, g, dy):
    r = lax.rsqrt(jnp.mean(x * x, axis=-1, keepdims=True) + EPS)
    xh = x * r
    dxh = dy * g
    dx = r * (dxh - xh * jnp.mean(dxh * xh, axis=-1, keepdims=True))
    return dx, jnp.sum(dy * xh, axis=0, keepdims=True)


def _bdot(a, b, dims=NN):
    return lax.dot_general(a.astype(bf16), b.astype(bf16), dims, preferred_element_type=f32)


def matmul(name, pairs, mode, out_dtype, tm, tn, tk=None):
    a0, b0 = pairs[0]
    if mode == "tn":
        M, N, K = a0.shape[1], b0.shape[1], a0.shape[0]
    else:
        M, K = a0.shape
        N = b0.shape[1] if mode == "nn" else b0.shape[0]
    tm, tn = min(tm, M), min(tn, N)
    assert M % tm == 0 and N % tn == 0, (name, M, N, tm, tn)
    if tk is None:
        nk = 1
    else:
        assert len(pairs) == 1 and K % tk == 0
        nk = K // tk
    dims = {"nn": NN, "nt": NT, "tn": TN}[mode]
    in_specs, args = [], []
    for a, b in pairs:
        kk = (a.shape[0] if mode == "tn" else a.shape[1]) if tk is None else tk
        if mode == "tn":
            in_specs.append(pl.BlockSpec((kk, tm), lambda j, i, k: (k, i)))
            in_specs.append(pl.BlockSpec((kk, tn), lambda j, i, k: (k, j)))
        elif mode == "nn":
            in_specs.append(pl.BlockSpec((tm, kk), lambda j, i, k: (i, k)))
            in_specs.append(pl.BlockSpec((kk, tn), lambda j, i, k: (k, j)))
        else:
            in_specs.append(pl.BlockSpec((tm, kk), lambda j, i, k: (i, k)))
            in_specs.append(pl.BlockSpec((tn, kk), lambda j, i, k: (j, k)))
        args += [a, b]
    npairs = len(pairs)

    def kern(*refs):
        o = refs[2 * npairs]
        part = None
        for p in range(npairs):
            d = _bdot(refs[2 * p][...], refs[2 * p + 1][...], dims)
            part = d if part is None else part + d
        if nk == 1:
            o[...] = part.astype(out_dtype)
        else:
            acc = refs[2 * npairs + 1]
            k = pl.program_id(2)

            @pl.when(k == 0)
            def _():
                acc[...] = part

            @pl.when(k > 0)
            def _():
                acc[...] += part

            @pl.when(k == nk - 1)
            def _():
                o[...] = acc[...].astype(out_dtype)

    return pl.pallas_call(
        kern, name=name, grid=(N // tn, M // tm, nk),
        in_specs=in_specs, out_specs=pl.BlockSpec((tm, tn), lambda j, i, k: (i, j)),
        out_shape=SDS((M, N), out_dtype),
        scratch_shapes=[pltpu.VMEM((tm, tn), f32)] if nk > 1 else [],
        compiler_params=pltpu.CompilerParams(dimension_semantics=("parallel", "parallel", "arbitrary")),
    )(*args)


def mm(name, grid, pairs, dims, o_spec, out_shape):
    nk = grid[2]
    npairs = len(pairs)
    in_specs, args = [], []
    for a, a_spec, b, b_spec in pairs:
        in_specs += [a_spec, b_spec]
        args += [a, b]
    blk = tuple(d for d in o_spec.block_shape if d is not None)

    def kern(*refs):
        o = refs[2 * npairs]
        part = None
        for p in range(npairs):
            d = _bdot(refs[2 * p][...], refs[2 * p + 1][...], dims)
            part = d if part is None else part + d
        if nk == 1:
            o[...] = part.astype(o.dtype)
        else:
            acc = refs[2 * npairs + 1]
            k = pl.program_id(2)

            @pl.when(k == 0)
            def _():
                acc[...] = part

            @pl.when(k > 0)
            def _():
                acc[...] += part

            @pl.when(k == nk - 1)
            def _():
                o[...] = acc[...].astype(o.dtype)

    return pl.pallas_call(
        kern, name=name, grid=grid, in_specs=in_specs, out_specs=o_spec, out_shape=out_shape,
        scratch_shapes=[pltpu.VMEM(blk, f32)] if nk > 1 else [],
        compiler_params=pltpu.CompilerParams(dimension_semantics=("parallel", "parallel", "arbitrary")),
    )(*args)


def mm_fused(name, n_row_blocks, pairs, dims, extra_ins, outs, accs, epilogue):
    npairs, nx, no, na = len(pairs), len(extra_ins), len(outs), len(accs)
    in_specs, args = [], []
    for a, a_spec, b, b_spec in pairs:
        in_specs += [a_spec, b_spec]
        args += [a, b]
    for arr, spec in extra_ins:
        in_specs.append(spec)
        args.append(arr)

    rows_blk = outs[0][1].block_shape[0]
    sub = min(rows_blk, MM_SUB)

    def kern(*refs):
        x_refs = refs[2 * npairs:2 * npairs + nx]
        o_refs = refs[2 * npairs + nx:2 * npairs + nx + no]
        a_refs = refs[2 * npairs + nx + no:]
        if na:
            @pl.when(pl.program_id(0) == 0)
            def _():
                for a in a_refs:
                    a[...] = jnp.zeros(a.shape, f32)
        for r0 in range(0, rows_blk, sub):
            rows = pl.ds(r0, sub)
            part = None
            for p in range(npairs):
                d = _bdot(refs[2 * p][rows, :], refs[2 * p + 1][...], dims)
                part = d if part is None else part + d
            epilogue(part, [x.at[rows, :] if x.shape[0] == rows_blk else x for x in x_refs], [o.at[rows, :] for o in o_refs], a_refs)

    return pl.pallas_call(
        kern, name=name, grid=(n_row_blocks,), in_specs=in_specs,
        out_specs=[spec for _, spec in outs] + [pl.BlockSpec(tuple(s), lambda i, nd=len(s): (0,) * nd) for s in accs],
        out_shape=[s for s, _ in outs] + [SDS(tuple(s), f32) for s in accs],
        compiler_params=pltpu.CompilerParams(dimension_semantics=("arbitrary",)),
    )(*args)


FF_SH = D_FF // 4


def ffn_up(name, T, tm, n_bf, wg4, wu4, l):
    sub = min(tm, MM_SUB)

    def kern(n_ref, wg_ref, wu_ref, g_ref, u_ref, a_ref):
        for r0 in range(0, tm, sub):
            rows = pl.ds(r0, sub)
            n = n_ref[rows, :]
            g = jnp.dot(n, wg_ref[...], preferred_element_type=f32)
            u = jnp.dot(n, wu_ref[...], preferred_element_type=f32)
            g_ref[rows, :] = g.astype(bf16)
            u_ref[rows, :] = u.astype(bf16)
            a_ref[rows, :] = (_silu(g) * u).astype(bf16)
    w_spec = pl.BlockSpec((None, D, FF_SH), lambda k, i: (k, l, 0))
    o_spec = pl.BlockSpec((None, tm, FF_SH), lambda k, i: (k, i, 0))
    s = SDS((4, T, FF_SH), bf16)
    return pl.pallas_call(kern, name=name, grid=(4, T // tm), in_specs=[pl.BlockSpec((tm, D), lambda k, i: (i, 0)), w_spec, w_spec],
                          out_specs=[o_spec] * 3, out_shape=[s, s, s],
                          compiler_params=pltpu.CompilerParams(dimension_semantics=("parallel", "parallel")))(n_bf, wg4, wu4)


def ffn_dgu(name, T, tm, d_f, wd4, gate4, up4, l):
    rc = 16

    sub = min(tm, MM_SUB)

    def kern(df_ref, wd_ref, g_ref, u_ref, dg_ref, du_ref, dact_ref):
        for s0 in range(0, tm, sub):
            dact_ref[pl.ds(s0, sub), :] = _bdot(df_ref[pl.ds(s0, sub), :], wd_ref[...], NT)
            for r0 in range(s0, s0 + sub, rc):
                rows = pl.ds(r0, rc)
                _, vjp = jax.vjp(lambda a, b: _silu(a) * b, g_ref[rows, :].astype(f32), u_ref[rows, :].astype(f32))
                dg, du = vjp(dact_ref[rows, :])
                dg_ref[rows, :] = dg.astype(bf16)
                du_ref[rows, :] = du.astype(bf16)
    a_spec = pl.BlockSpec((None, tm, FF_SH), lambda k, i: (k, i, 0))
    s = SDS((4, T, FF_SH), bf16)
    return pl.pallas_call(kern, name=name, grid=(4, T // tm),
                          in_specs=[pl.BlockSpec((tm, D), lambda k, i: (i, 0)), pl.BlockSpec((None, FF_SH, D), lambda k, i: (k, l, 0)), a_spec, a_spec],
                          out_specs=[a_spec] * 2, out_shape=[s, s], scratch_shapes=[pltpu.VMEM((tm, FF_SH), f32)],
                          compiler_params=pltpu.CompilerParams(dimension_semantics=("parallel", "parallel")))(d_f, wd4, gate4, up4)


def rowcall(name, body, T, tm, ins, outs, accs=(), scratch=(), reverse=False, sub=None):
    n = T // tm
    assert T % tm == 0

    def blk(i):
        return (n - 1 - i) if reverse else i

    in_specs, args = [], []
    for spec in ins:
        kind, arr = spec[0], spec[1]
        if kind == "row":
            _, _, w, cb = spec
            in_specs.append(pl.BlockSpec((tm, w), lambda i, cb=cb: (blk(i), cb)))
        elif kind == "prev":
            _, _, w, cb, h = spec
            r = tm // h
            in_specs.append(pl.BlockSpec((h, w), lambda i, cb=cb, r=r: (jnp.maximum(blk(i) * r - 1, 0), cb)))
        elif kind == "next":
            _, _, w, cb, h = spec
            r = tm // h
            in_specs.append(pl.BlockSpec((h, w), lambda i, cb=cb, r=r, h=h: (jnp.minimum((blk(i) + 1) * r, T // h - 1), cb)))
        else:
            nd = arr.ndim
            in_specs.append(pl.BlockSpec(arr.shape, lambda i, nd=nd: (0,) * nd))
        args.append(arr)
    out_shape = [SDS((T, w), dt) for w, dt in outs] + [SDS(tuple(s), f32) for s in accs]
    out_specs = [pl.BlockSpec((tm, w), lambda i: (blk(i), 0)) for w, _ in outs]
    out_specs += [pl.BlockSpec(tuple(s), lambda i, nd=len(s): (0,) * nd) for s in accs]
    ni, no, na = len(ins), len(outs), len(accs)

    def kern(*refs):
        i = pl.program_id(0)
        in_refs, out_refs = refs[:ni], refs[ni:ni + no]
        acc_refs, scr = refs[ni + no:ni + no + na], refs[ni + no + na:]
        if na:
            @pl.when(i == 0)
            def _():
                for a in acc_refs:
                    a[...] = jnp.zeros(a.shape, f32)
        if sub is None or sub >= tm:
            body(blk(i), n, in_refs, out_refs, acc_refs, scr)
        else:
            for r0 in range(0, tm, sub):
                rows = pl.ds(r0, sub)
                body(blk(i), n, [r.at[rows, :] if spec[0] == "row" else r for r, spec in zip(in_refs, ins)],
                     [o.at[rows, :] for o in out_refs], acc_refs, [s.at[rows, :] for s in scr])

    res = pl.pallas_call(
        kern, name=name, grid=(n,), in_specs=in_specs, out_specs=out_specs, out_shape=out_shape,
        scratch_shapes=list(scratch),
        compiler_params=pltpu.CompilerParams(dimension_semantics=("arbitrary",)),
    )(*args)
    return res


def rms_to_bf16(name, T, tm, x, g):
    def body(i, n, ins, outs, accs, scr):
        outs[0][...] = _rms(ins[0][...], ins[1][...]).astype(bf16)
    return rowcall(name, body, T, tm, [("row", x, D, 0), ("const", g)], [(D, bf16)], sub=64)[0]


def _layer_norm_parts(x):
    mu = jnp.mean(x, axis=-1, keepdims=True)
    xc = x - mu
    r = lax.rsqrt(jnp.mean(xc * xc, axis=-1, keepdims=True) + EPS)
    return xc * r, r


def _gelu_and_slope(x):
    c, a = math.sqrt(2.0 / math.pi), 0.044715
    x2 = x * x
    t = jnp.tanh(c * (x + a * (x * x2)))
    cdf = 0.5 * (1.0 + t)
    slope = cdf + (0.5 * x) * (1.0 - t * t) * (c * (1.0 + (3.0 * a) * x2))
    return x * cdf, slope


def gmlp_fwd(name, T, tm, uvz, ln_g, ln_b, wm, bs):
    def body(i, n, ins, outs, accs, scr):
        gu = _gelu_and_slope(ins[0][...].astype(f32))[0]
        xh, _ = _layer_norm_parts(_gelu_and_slope(ins[1][...].astype(f32))[0])
        vln = (xh * ins[2][...] + ins[3][...]).astype(bf16)
        for c in range(ins[0].shape[0] // CHUNK):
            rows = slice(c * CHUNK, (c + 1) * CHUNK)
            for h in range(GM_HEADS):
                cols = slice(h * GM_HD, (h + 1) * GM_HD)
                mixed = jnp.dot(ins[4][h], vln[rows, cols], preferred_element_type=f32) + ins[5][h]
                outs[0][rows, cols] = (gu[rows, cols] * mixed).astype(bf16)
    return rowcall(name, body, T, tm, [("row", uvz, D, 0), ("row", uvz, D, 1), ("const", ln_g), ("const", ln_b), ("const", wm), ("const", bs)],
                   [(D, bf16)], sub=CHUNK)[0]


def gmlp_bwd(name, T, tm, uvz, d_ya, d_cb, ln_g, ln_b, wm, bs):
    def body(i, n, ins, outs, accs, scr):
        u, v, dya = ins[0][...].astype(f32), ins[1][...].astype(f32), ins[2][...]
        gu, slope_u = _gelu_and_slope(u)
        gv, slope_v = _gelu_and_slope(v)
        xh, r = _layer_norm_parts(gv)
        lng = ins[3][...]
        vln = (xh * lng + ins[4][...]).astype(bf16)
        rr = lax.broadcasted_iota(jnp.int32, (CHUNK, CHUNK), 0)
        cc = lax.broadcasted_iota(jnp.int32, (CHUNK, CHUNK), 1)
        causal = (rr >= cc).astype(f32)
        dvln_ref = scr[0]
        dgu_ref = scr[1]
        for c in range(ins[0].shape[0] // CHUNK):
            rows = slice(c * CHUNK, (c + 1) * CHUNK)
            for h in range(GM_HEADS):
                cols = slice(h * GM_HD, (h + 1) * GM_HD)
                w = ins[5][h]
                blk = vln[rows, cols]
                mixed = jnp.dot(w, blk, preferred_element_type=f32) + ins[6][h]
                dy = dya[rows, cols]
                dgu_ref[rows, cols] = dy * mixed
                dm = dy * gu[rows, cols]
                accs[3][h] += jnp.sum(dm, axis=1, keepdims=True)
                accs[2][h] += _bdot(dm, blk, NT) * causal
                dvln_ref[rows, cols] = _bdot(w, dm, TN)
        dvln = dvln_ref[...]
        accs[0][...] += jnp.sum(dvln * xh, axis=0, keepdims=True)
        accs[1][...] += jnp.sum(dvln, axis=0, keepdims=True)
        dxh = dvln * lng
        dgv = r * (dxh - jnp.mean(dxh, axis=-1, keepdims=True) - xh * jnp.mean(dxh * xh, axis=-1, keepdims=True))
        outs[0][...] = (dgu_ref[...] * slope_u).astype(bf16)
        outs[1][...] = (dgv * slope_v).astype(bf16)
    return rowcall(name, body, T, tm,
                   [("row", uvz, D, 0), ("row", uvz, D, 1), ("row", d_ya, D, d_cb), ("const", ln_g), ("const", ln_b), ("const", wm), ("const", bs)],
                   [(D, bf16), (D, bf16)], accs=[(1, D), (1, D), (GM_HEADS, CHUNK, CHUNK), (GM_HEADS, CHUNK, 1)],
                   scratch=[pltpu.VMEM((tm, D), f32), pltpu.VMEM((tm, D), f32)], sub=CHUNK)


CONV_RC, CONV_LB = 64, 256


def _conv_fill(i, x_ref, halo_ref, scr, tm):
    scr[pl.ds(0, CONV_HALO), :] = jnp.where(i > 0, halo_ref[...].astype(f32), 0.0)
    scr[pl.ds(CONV_HALO, tm), :] = x_ref[...].astype(f32)


def _conv_taps(scr, r0, lanes):
    return [scr[pl.ds(r0 + CONV_HALO - (CONV_K - 1) + k, CONV_RC), lanes] for k in range(CONV_K)]


def conv_fwd(name, T, tm, xbc, conv_w, conv_b):
    def body(i, n, ins, outs, accs, scr):
        s = scr[0]
        _conv_fill(i, ins[0], ins[1], s, tm)
        for lb in range(CONV_DIM // CONV_LB):
            lanes = slice(lb * CONV_LB, (lb + 1) * CONV_LB)
            w, b = ins[2][:, lanes], ins[3][:, lanes]

            for r0 in range(0, tm, CONV_RC):
                taps = _conv_taps(s, r0, lanes)
                pre = b + sum(w[k:k + 1] * taps[k] for k in range(CONV_K))
                outs[0][pl.ds(r0, CONV_RC), lanes] = _silu(pre)
    return rowcall(name, body, T, tm, [("row", xbc, CONV_DIM, 0), ("prev", xbc, CONV_DIM, 0, CONV_HALO), ("const", conv_w), ("const", conv_b)],
                   [(CONV_DIM, f32)], scratch=[pltpu.VMEM((tm + CONV_HALO, CONV_DIM), f32)])[0]


def conv_bwd_pre(name, T, tm, xbc, d_xc, conv_w, conv_b):
    def body(i, n, ins, outs, accs, scr):
        s = scr[0]
        _conv_fill(i, ins[0], ins[1], s, tm)
        fold = lambda v: jnp.sum(v.reshape(CONV_RC // 8, 8, CONV_LB), axis=0)
        for lb in range(CONV_DIM // CONV_LB):
            lanes = slice(lb * CONV_LB, (lb + 1) * CONV_LB)
            w, b = ins[3][:, lanes], ins[4][:, lanes]

            sums = [jnp.zeros((8, CONV_LB), f32)] * (CONV_K + 1)
            for r0 in range(0, tm, CONV_RC):
                taps = _conv_taps(s, r0, lanes)
                pre = b + sum(w[k:k + 1] * taps[k] for k in range(CONV_K))
                _, vjp = jax.vjp(_silu, pre)
                dpre = vjp(ins[2][pl.ds(r0, CONV_RC), lanes])[0]
                outs[0][pl.ds(r0, CONV_RC), lanes] = dpre
                sums = [sums[k] + fold(dpre * taps[k]) for k in range(CONV_K)] + [sums[CONV_K] + fold(dpre)]
            for k in range(CONV_K):
                accs[0][pl.ds(k, 1), lanes] += jnp.sum(sums[k], axis=0, keepdims=True)
            accs[1][:, lanes] += jnp.sum(sums[CONV_K], axis=0, keepdims=True)
    return rowcall(name, body, T, tm,
                   [("row", xbc, CONV_DIM, 0), ("prev", xbc, CONV_DIM, 0, CONV_HALO), ("row", d_xc, CONV_DIM, 0), ("const", conv_w), ("const", conv_b)],
                   [(CONV_DIM, f32)], accs=[(CONV_K, CONV_DIM), (1, CONV_DIM)], scratch=[pltpu.VMEM((tm + CONV_HALO, CONV_DIM), f32)])


def conv_bwd_x(name, T, tm, d_pre, conv_w):
    def body(i, n, ins, outs, accs, scr):
        s = scr[0]
        s[pl.ds(0, tm), :] = ins[0][...]
        s[pl.ds(tm, CONV_HALO), :] = jnp.where(i < n - 1, ins[1][...], 0.0)
        for lb in range(CONV_DIM // CONV_LB):
            lanes = slice(lb * CONV_LB, (lb + 1) * CONV_LB)
            w = ins[2][:, lanes]

            for r0 in range(0, tm, CONV_RC):
                dx = sum(w[k:k + 1] * s[pl.ds(r0 + CONV_K - 1 - k, CONV_RC), lanes] for k in range(CONV_K))
                outs[0][pl.ds(r0, CONV_RC), lanes] = dx.astype(bf16)
    return rowcall(name, body, T, tm, [("row", d_pre, CONV_DIM, 0), ("next", d_pre, CONV_DIM, 0, CONV_HALO), ("const", conv_w)],
                   [(CONV_DIM, bf16)], scratch=[pltpu.VMEM((tm + CONV_HALO, CONV_DIM), f32)])[0]


def _ssd_prep(dtr, dtb, alog):
    rr = lax.broadcasted_iota(jnp.int32, (CHUNK, CHUNK), 0)
    cc = lax.broadcasted_iota(jnp.int32, (CHUNK, CHUNK), 1)
    dt = _softplus(dtr + dtb)
    dA = dt * -jnp.exp(alog)
    acum = jnp.dot((rr >= cc).astype(f32), dA, precision=HI, preferred_element_type=f32)
    return dt, acum, acum.T, jnp.sum(dA, axis=0, keepdims=True)


def _ssd_group(g, x, Bm, Cm, S, dt, acum, acumT, tot, dsk):
    rr = lax.broadcasted_iota(jnp.int32, (CHUNK, CHUNK), 0)
    cc = lax.broadcasted_iota(jnp.int32, (CHUNK, CHUNK), 1)
    tril = rr >= cc
    lane = lax.broadcasted_iota(jnp.int32, (1, DT_PAD), 1)
    sub = lax.broadcasted_iota(jnp.int32, (DT_PAD, 1), 0)
    glane = lax.broadcasted_iota(jnp.int32, (1, SSM_HPG * SSM_P), 1) // SSM_P
    hm = [(glane == r).astype(f32) for r in range(SSM_HPG)]
    pick = lambda v, r: jnp.sum(v * (lane == SSM_HPG * g + r).astype(f32), axis=1, keepdims=True)
    cols = [pick(acum, r) for r in range(SSM_HPG)]
    tots = [pick(tot, r) for r in range(SSM_HPG)]
    spread = lambda vals: sum(vals[r] * hm[r] for r in range(SSM_HPG))
    xdt = x * spread([pick(dt, r) for r in range(SSM_HPG)])
    cb = _bdot(Cm, Bm, NT)
    y = x * spread([pick(dsk, r) for r in range(SSM_HPG)])
    for r in range(SSM_HPG):
        row = jnp.sum(acumT * (sub == SSM_HPG * g + r).astype(f32), axis=0, keepdims=True)
        dec = jnp.exp(jnp.where(tril, cols[r] - row, -jnp.inf))
        y = y + _bdot(cb * dec, xdt * hm[r])
    y = y + _bdot(Cm, S) * spread([jnp.exp(c) for c in cols])
    dte = spread([jnp.exp(tots[r] - cols[r]) for r in range(SSM_HPG)])
    s_new = S * spread([jnp.exp(t) for t in tots]) + _bdot(Bm, xdt * dte, TN)
    return y, s_new


def _ssd_ins(xc, dtr):
    gw = SSM_HPG * SSM_P
    ins = [("row", xc, gw, g) for g in range(SSM_GROUPS)]
    ins += [("row", xc, SSM_N, D // SSM_N + g) for g in range(SSM_GROUPS)]
    ins += [("row", xc, SSM_N, D // SSM_N + SSM_GROUPS + g) for g in range(SSM_GROUPS)]
    ins += [("row", dtr, DT_PAD, 0)]
    return ins


SSD_CPS = 4


def ssd_fwd(name, T, xc, dtr, dtb, alog, dsk):
    gw = SSM_HPG * SSM_P
    cps = min(SSD_CPS, T // CHUNK)

    def body(i, n, ins, outs, accs, scr):
        S = scr[0]

        @pl.when(i == 0)
        def _():
            S[...] = jnp.zeros(S.shape, f32)
        S4 = tuple(S[:, g * gw:(g + 1) * gw] for g in range(4))
        for c in range(cps):
            rows = pl.ds(c * CHUNK, CHUNK)
            X4 = tuple(ins[g][rows, :] for g in range(4))
            B4 = tuple(ins[4 + g][rows, :] for g in range(4))
            C4 = tuple(ins[8 + g][rows, :] for g in range(4))
            prep = _ssd_prep(ins[12][rows, :], ins[13][...], ins[14][...])
            nxt = []
            for g in range(4):
                outs[1][rows, g * gw:(g + 1) * gw] = S4[g]
                y, s_new = _ssd_group(g, X4[g], B4[g], C4[g], S4[g], *prep, ins[15][...])
                outs[0][rows, g * gw:(g + 1) * gw] = y
                nxt.append(s_new)
            S4 = tuple(nxt)
        for g in range(4):
            S[:, g * gw:(g + 1) * gw] = S4[g]
    ins = _ssd_ins(xc, dtr) + [("const", dtb), ("const", alog), ("const", dsk)]
    return rowcall(name, body, T, cps * CHUNK, ins, [(D, f32), (D, f32)], scratch=[pltpu.VMEM((SSM_N, D), f32)])


def ssd_bwd(name, T, xc, dtr, sprev, d_y, dtb, alog, dsk):
    gw = SSM_HPG * SSM_P

    def body(i, n, ins, outs, accs, scr):
        dS = scr[0]

        @pl.when(i == n - 1)
        def _():
            dS[...] = jnp.zeros(dS.shape, f32)
        dS4 = tuple(dS[:, g * gw:(g + 1) * gw] for g in range(4))
        def chunk(X4, dtr_c, B4, C4, S4, dtb_c, alog_c, dsk_c):
            prep = _ssd_prep(dtr_c, dtb_c, alog_c)
            res = [_ssd_group(g, X4[g], B4[g], C4[g], S4[g], *prep, dsk_c) for g in range(4)]
            return tuple(r[0] for r in res), tuple(r[1] for r in res)
        X4 = tuple(ins[g][...] for g in range(4))
        B4 = tuple(ins[4 + g][...] for g in range(4))
        C4 = tuple(ins[8 + g][...] for g in range(4))
        S4 = tuple(ins[13 + g][...] for g in range(4))
        dY4 = tuple(ins[17 + g][...] for g in range(4))
        _, vjp = jax.vjp(chunk, X4, ins[12][...], B4, C4, S4, ins[21][...], ins[22][...], ins[23][...])
        dX4, ddtr, dB4, dC4, dS4, ddtb, dalog, ddsk = vjp((dY4, dS4))
        for g in range(4):
            outs[0][:, g * gw:(g + 1) * gw] = dX4[g]
            outs[0][:, D + g * SSM_N:D + (g + 1) * SSM_N] = dB4[g]
            outs[0][:, D + (SSM_GROUPS + g) * SSM_N:D + (SSM_GROUPS + g + 1) * SSM_N] = dC4[g]
            dS[:, g * gw:(g + 1) * gw] = dS4[g]
        outs[1][...] = ddtr.astype(bf16)
        accs[0][...] += ddtb
        accs[1][...] += dalog
        accs[2][...] += ddsk
    ins = _ssd_ins(xc, dtr) + [("row", sprev, gw, g) for g in range(4)] + [("row", d_y, gw, g) for g in range(4)]
    ins += [("const", dtb), ("const", alog), ("const", dsk)]
    return rowcall(name, body, T, CHUNK, ins, [(CONV_DIM, f32), (DT_PAD, bf16)], accs=[(1, DT_PAD)] * 3,
                   scratch=[pltpu.VMEM((SSM_N, D), f32)], reverse=True)


def _gate_group(y, z, g):
    return _rms(y * _silu(z), g)


def gate_fwd(name, T, tm, y, uvz, gn):
    def body(i, n, ins, outs, accs, scr):
        for g in range(SSM_GROUPS):
            cols = slice(g * 256, (g + 1) * 256)
            outs[0][:, cols] = _gate_group(ins[0][:, cols], ins[1][:, cols].astype(f32), ins[2][:, cols]).astype(bf16)
    return rowcall(name, body, T, tm, [("row", y, D, 0), ("row", uvz, D, 2), ("const", gn)], [(D, bf16)], sub=64)[0]


def gate_bwd(name, T, tm, y, uvz, d_yb, d_cb, gn):
    def body(i, n, ins, outs, accs, scr):
        for g in range(SSM_GROUPS):
            cols = slice(g * 256, (g + 1) * 256)
            _, vjp = jax.vjp(_gate_group, ins[0][:, cols], ins[1][:, cols].astype(f32), ins[3][:, cols])
            dy, dz, dg = vjp(ins[2][:, cols])
            outs[0][:, cols] = dy
            outs[1][:, cols] = dz.astype(bf16)
            accs[0][:, cols] += dg
    return rowcall(name, body, T, tm, [("row", y, D, 0), ("row", uvz, D, 2), ("row", d_yb, D, d_cb), ("const", gn)],
                   [(D, f32), (D, bf16)], accs=[(1, D)], sub=64)


def _window_sum(src, cols, levels, tm, lv, trailing):
    cur, cur_cols = src, cols
    for l in range(1, levels + 1):
        shift = 2 ** (l - 1)
        last = l == levels
        if trailing:
            start = POOL_HALO if last else 8 * l
            rows = tm if last else tm + POOL_HALO - start
            new = cur[pl.ds(start, rows), cur_cols] + cur[pl.ds(start - shift, rows), cur_cols]
        else:
            start = 0
            rows = tm if last else tm + POOL_HALO - 8 * l
            new = cur[pl.ds(0, rows), cur_cols] + cur[pl.ds(shift, rows), cur_cols]
        if last:
            return new
        nxt = lv[l % 2]
        nxt[pl.ds(start, rows), :] = new
        cur, cur_cols = nxt, slice(None)


def _pool_diff(i, tm, h_ref, halo_ref, g_ref, scr, lv):
    g = g_ref[...]
    yn = _rms(h_ref[...], g)
    scr[pl.ds(0, POOL_HALO), :] = jnp.where(i > 0, _rms(halo_ref[...], g), 0.0)
    scr[pl.ds(POOL_HALO, tm), :] = yn
    pos = (i * tm + lax.broadcasted_iota(jnp.int32, (tm, 1), 0) + 1).astype(f32)
    parts = []
    for gi, win in enumerate(POOL_WINDOWS):
        cols = slice(gi * POOL_GD, (gi + 1) * POOL_GD)
        s = _window_sum(scr, cols, gi + 1, tm, lv, True)
        parts.append(s * (1.0 / jnp.minimum(pos, float(win))) - yn[:, cols])
    return parts


def pool_fwd(name, T, tm, h2, g_pre, pw, pb, psc, g_post, g_next):
    def body(i, n, ins, outs, accs, scr):
        parts = _pool_diff(i, tm, ins[0], ins[1], ins[2], scr[0], scr[1:3])
        for gi in range(len(POOL_WINDOWS)):
            cols = slice(gi * POOL_GD, (gi + 1) * POOL_GD)
            o = _bdot(parts[gi], ins[3][gi]) + ins[4][:, cols]
            outs[0][:, cols] = o * ins[5][:, cols]
        h = ins[0][...] + _rms(outs[0][...], ins[6][...])
        outs[1][...] = h
        outs[2][...] = _rms(h, ins[7][...]).astype(bf16)
    return rowcall(name, body, T, tm, [("row", h2, D, 0), ("prev", h2, D, 0, POOL_HALO), ("const", g_pre), ("const", pw), ("const", pb), ("const", psc),
                                       ("const", g_post), ("const", g_next)],
                   [(D, f32), (D, f32), (D, bf16)], scratch=[pltpu.VMEM((tm + POOL_HALO, D), f32)] + [pltpu.VMEM((tm + POOL_HALO, POOL_GD), f32)] * 2)


def pool_bwd(name, T, tm, h2, d_pm, d_res, g_pre, pw, pb, psc, f_prev, g_prev):
    def body(i, n, ins, outs, accs, scr):
        parts = _pool_diff(i, tm, ins[0], ins[1], ins[5], scr[0], scr[3:5])
        dpm = ins[2][...]
        psc_v = ins[8][...]
        dps = dpm * psc_v
        dps_halo = jnp.where(i < n - 1, ins[3][...] * psc_v, 0.0)
        accs[1][...] += jnp.sum(dps, axis=0, keepdims=True)
        pos = (i * tm + lax.broadcasted_iota(jnp.int32, (tm, 1), 0) + 1).astype(f32)
        pos_h = ((i + 1) * tm + lax.broadcasted_iota(jnp.int32, (POOL_HALO, 1), 0) + 1).astype(f32)
        r_scr = scr[1]
        dyn_scr = scr[2]
        for gi, win in enumerate(POOL_WINDOWS):
            cols = slice(gi * POOL_GD, (gi + 1) * POOL_GD)
            w = ins[6][gi]
            o = _bdot(parts[gi], w) + ins[7][:, cols]
            accs[2][:, cols] += jnp.sum(dpm[:, cols] * o, axis=0, keepdims=True)
            accs[0][gi] += _bdot(parts[gi], dps[:, cols], TN)
            q = _bdot(dps[:, cols], w, NT)
            qh = _bdot(dps_halo[:, cols], w, NT)
            r_scr[pl.ds(0, tm), cols] = q * (1.0 / jnp.minimum(pos, float(win)))
            r_scr[pl.ds(tm, POOL_HALO), cols] = qh * (1.0 / jnp.minimum(pos_h, float(win)))
            dyn_scr[:, cols] = _window_sum(r_scr, cols, gi + 1, tm, scr[3:5], False) - q
        dx, dg = _rms_bwd(ins[0][...], ins[5][...], dyn_scr[...])
        dh = ins[4][...] + dx
        outs[0][...] = dh
        accs[3][...] += dg
        df, dgp = _rms_bwd(ins[9][...], ins[10][...], dh)
        outs[1][...] = df.astype(bf16)
        accs[4][...] += dgp
    ins = [("row", h2, D, 0), ("prev", h2, D, 0, POOL_HALO), ("row", d_pm, D, 0), ("next", d_pm, D, 0, POOL_HALO), ("row", d_res, D, 0),
           ("const", g_pre), ("const", pw), ("const", pb), ("const", psc), ("row", f_prev, D, 0), ("const", g_prev)]
    return rowcall(name, body, T, tm, ins, [(D, f32), (D, bf16)], accs=[(4, POOL_GD, POOL_GD), (1, D), (1, D), (1, D), (1, D)],
                   scratch=[pltpu.VMEM((tm + POOL_HALO, D), f32), pltpu.VMEM((tm + POOL_HALO, D), f32), pltpu.VMEM((tm, D), f32)]
                   + [pltpu.VMEM((tm + POOL_HALO, POOL_GD), f32)] * 2)


def local_step(T, x, tgt, W, ffn_weights, early_grads, early_grads_next, w_in_grads):
    tm = 512 if T >= 1024 else T // 2
    TKW = 4096 if T >= 4096 else T
    ng = W["norm_g"]
    g = lambda l, j: ng[l, j][None, :]
    G = {}

    tf = tm
    once = pl.Buffered(1)
    vec_f = pl.BlockSpec((1, D), lambda i: (0, 0))

    def fused_specs(t):
        rows = pl.BlockSpec((t, D), lambda i: (i, 0))
        return rows, [pl.BlockSpec((None, t, FF_SH), lambda i, s=s: (s, i, 0)) for s in range(4)], (SDS((T, D), f32), rows), (SDS((T, D), bf16), rows)
    rows_f, sh_f, out_f32, out_bf16 = fused_specs(tf)
    tf2 = min(T, 2 * tm)
    rows_f2, sh_f2, out2_f32, out2_bf16 = fused_specs(tf2)

    def resid_epilogue(with_pre):
        def ep(part, xs, os, accs):
            h = xs[0][...] + _rms(part, xs[1][...])
            os[0][...] = part
            os[1][...] = h
            if with_pre:
                os[2][...] = _rms(h, xs[2][...]).astype(bf16)
        return ep

    def bwd_epilogue(df_dtype):
        def ep(part, xs, os, accs):
            dx, dgp = _rms_bwd(xs[0][...], xs[3][...], part)
            dh = xs[2][...] + dx
            df, dgq = _rms_bwd(xs[1][...], xs[4][...], dh)
            os[0][...] = dh
            os[1][...] = df.astype(df_dtype)
            accs[0][...] += dgp
            accs[1][...] += dgq
        return ep

    def loss_epilogue(part, xs, os, accs):
        g_post = xs[2][...]
        e = xs[0][...] + _rms(part, g_post) - xs[1][...]
        accs[0][...] += jnp.sum(jnp.sum(e * e, axis=-1, keepdims=True) * (0.5 / D), axis=0, keepdims=True)
        dh = e * (1.0 / D)
        df, dg = _rms_bwd(part, g_post, dh)
        os[0][...] = dh
        os[1][...] = df.astype(bf16)
        accs[1][...] += dg

    def ffn_fwd(tag, n_bf, l, resid=None, loss=None):
        gate4, up4, act4 = ffn_up(f"ffn{tag}_up", T, min(T, 4 * tm), n_bf, W["wg4"], W["wu4"], l)
        wd_f = [pl.BlockSpec((None, FF_SH, D), lambda i, s=s: (s, l, 0), pipeline_mode=once) for s in range(4)]
        pairs = [(act4, sh_f2[s], W["wd4"], wd_f[s]) for s in range(4)]
        if loss is not None:
            return (gate4, up4, act4) + tuple(mm_fused(f"ffn{tag}_down", T // tf2, pairs, NN, [(loss[0], rows_f2), (loss[1], rows_f2), (loss[2], vec_f)],
                                                       [out2_f32, out2_bf16], [(1, 1), (1, D)], loss_epilogue))
        f, h_out = mm_fused(f"ffn{tag}_down", T // tf2, pairs, NN, [(resid[0], rows_f2), (resid[1], vec_f)], [out2_f32, out2_f32], [],
                            resid_epilogue(False))
        return gate4, up4, act4, f, h_out

    def ffn_bwd(tag, l, n_bf, gate4, up4, act4, d_f, h_out, f_pre, d_res, g_pre, g_post, df_dtype):
        d_gate4, d_up4 = ffn_dgu(f"ffn{tag}_dgu", T, min(T, 4 * tm), d_f, W["wd4"], gate4, up4, l)
        w_f = [pl.BlockSpec((None, D, FF_SH), lambda i, s=s: (s, l, 0), pipeline_mode=once) for s in range(4)]
        d_h, d_fp, dgp, dgq = mm_fused(
            f"ffn{tag}_dn", T // tf, [(d_gate4, sh_f[s], W["wg4"], w_f[s]) for s in range(4)] + [(d_up4, sh_f[s], W["wu4"], w_f[s]) for s in range(4)],
            NT, [(h_out, rows_f), (f_pre, rows_f), (d_res, rows_f), (g_pre, vec_f), (g_post, vec_f)],
            [out_f32, (SDS((T, D), df_dtype), rows_f)], [(1, D), (1, D)], bwd_epilogue(df_dtype))

        def wgrad(nm, a4, b):
            return mm(nm, (4, 1, T // TKW),
                      [(a4, pl.BlockSpec((None, TKW, FF_SH), lambda s, j, k: (s, k, 0)), b, pl.BlockSpec((TKW, D), lambda s, j, k: (k, 0)))],
                      TN, pl.BlockSpec((None, FF_SH, D), lambda s, j, k: (s, 0, 0)), SDS((4, FF_SH, D), f32))
        return d_h, d_fp, dgp, dgq, wgrad(f"ffn{tag}_dwg", d_gate4, n_bf), wgrad(f"ffn{tag}_dwu", d_up4, n_bf), wgrad(f"ffn{tag}_dwd", act4, d_f)

    y0 = rms_to_bf16("l0_prenorm", T, tf2, x, g(0, 0))
    uvz = matmul("in_uvz", [(y0, W["w_uvz"])], "nn", bf16, 4 * tm, 1024)
    xbc = matmul("in_xbc", [(y0, W["w_xbc"])], "nn", bf16, 4 * tm, 1024)
    dtr = matmul("in_dt", [(y0, W["w_dt"])], "nn", f32, 4 * tm, DT_PAD)
    y_a = gmlp_fwd("gmlp_fwd", T, tf2, uvz, W["ln_g"], W["ln_b"], W["wm"], W["bs"])
    xc = conv_fwd("conv_fwd", T, tm, xbc, W["conv_w"], W["conv_b"])
    y_ssd, sprev = ssd_fwd("ssd_fwd", T, xc, dtr, W["dtb"], W["alog"], W["dsk"])
    y_b = gate_fwd("gate_fwd", T, tf2, y_ssd, uvz, W["gn"])
    half = D // 2
    wo4 = W["wo4"]
    ycol = [pl.BlockSpec((tf2, half), lambda i, cb=cb: (i, cb)) for cb in range(2)]
    wo_s = [pl.BlockSpec((None, half, D), lambda i, s=s: (s, 0, 0), pipeline_mode=once) for s in range(4)]
    mixo, h1, n1 = mm_fused("out_proj", T // tf2, [(y_a, ycol[0], wo4, wo_s[0]), (y_a, ycol[1], wo4, wo_s[1]),
                                                  (y_b, ycol[0], wo4, wo_s[2]), (y_b, ycol[1], wo4, wo_s[3])], NN,
                            [(x, rows_f2), (g(0, 1), vec_f), (g(0, 2), vec_f)], [out2_f32, out2_f32, out2_bf16], [], resid_epilogue(True))
    W = dict(W)
    W["wg4"], W["wu4"], W["wd4"] = ffn_weights(h1)
    gate0, up0, act0, f1, h2 = ffn_fwd("0", n1, 0, resid=(h1, g(0, 3)))
    pm, h3, n3 = pool_fwd("pool_fwd", T, tm, h2, g(1, 0), W["pool_w"], W["pool_b"], W["pool_scale"], g(1, 1), g(1, 2))
    gate1, up1, act1, dh4, d_f2, loss_acc, dg13 = ffn_fwd("1", n3, 1, loss=(h3, tgt, g(1, 3)))
    d_h3, d_pm, dg12, dg11, dwg1, dwu1, dwd1 = ffn_bwd("1", 1, n3, gate1, up1, act1, d_f2, h3, pm, dh4, g(1, 2), g(1, 1), f32)
    d_h2, d_f1, G["pool_w"], G["pool_b"], G["pool_scale"], dg10, dg03 = pool_bwd("pool_bwd", T, tm, h2, d_pm, d_h3, g(1, 0), W["pool_w"], W["pool_b"],
                                                                                 W["pool_scale"], f1, g(0, 3))
    d_h1, d_mixo, dg02, dg01, dwg0, dwu0, dwd0 = ffn_bwd("0", 0, n1, gate0, up0, act0, d_f1, h1, mixo, d_h2, g(0, 2), g(0, 1), bf16)
    def d_wo(nm, y):
        return mm(nm, (2, 1, T // TKW), [(y, pl.BlockSpec((TKW, half), lambda s, j, k: (k, s)), d_mixo, pl.BlockSpec((TKW, D), lambda s, j, k: (k, 0)))],
                  TN, pl.BlockSpec((None, half, D), lambda s, j, k: (s, 0, 0)), SDS((2, half, D), f32))
    d_ycat = matmul("out_proj_dy", [(d_mixo, wo4.reshape(4 * half, D))], "nt", f32, 4 * tm, 1024)
    dwo_a, dwo_b = d_wo("out_proj_dwa", y_a), d_wo("out_proj_dwb", y_b)
    G["wo4"] = [dwo_a[0], dwo_a[1], dwo_b[0], dwo_b[1]]
    G["wgT4"], G["wuT4"], G["wd4"] = [dwg0, dwg1], [dwu0, dwu1], [dwd0, dwd1]
    token = early_grads(G)
    d_yssd, d_z, G["gn"] = gate_bwd("gate_bwd", T, tf2, y_ssd, uvz, d_ycat, 1, W["gn"] + token[0, 0])
    d_xc, d_dtr, G["dtb"], G["alog"], G["dsk"] = ssd_bwd("ssd_bwd", T, xc, dtr, sprev, d_yssd, W["dtb"], W["alog"], W["dsk"])
    token = early_grads_next(d_dtr)
    d_pre, G["conv_w"], G["conv_b"] = conv_bwd_pre("conv_bwd_pre", T, tm, xbc, d_xc, W["conv_w"], W["conv_b"] + token[0, 0])
    d_xbc = conv_bwd_x("conv_bwd_x", T, tm, d_pre, W["conv_w"])
    d_u, d_v, G["ln_g"], G["ln_b"], G["wm"], G["bs"] = gmlp_bwd("gmlp_bwd", T, tf2, uvz, d_ycat, 0, W["ln_g"], W["ln_b"], W["wm"], W["bs"])
    w_u, w_v, w_z = W["w_uvz"][:, :D], W["w_uvz"][:, D:2 * D], W["w_uvz"][:, 2 * D:]
    def pre_epilogue(part, xs, os, accs):
        dx, dg = _rms_bwd(xs[0][...], xs[2][...], part)
        os[0][...] = xs[1][...] + dx
        accs[0][...] += dg
    blk = lambda w: pl.BlockSpec((tf, w), lambda i: (i, 0))
    whole = lambda a: pl.BlockSpec(a.shape, lambda i: (0, 0), pipeline_mode=once)
    w_in_t = [matmul("in_dwu", [(d_u, y0)], "tn", f32, 1024, 1024, TKW), matmul("in_dwv", [(d_v, y0)], "tn", f32, 1024, 1024, TKW),
              matmul("in_dwz", [(d_z, y0)], "tn", f32, 1024, 1024, TKW), matmul("in_dwxbc", [(d_xbc, y0)], "tn", f32, 1024, 1024, TKW),
              matmul("in_dwdt", [(d_dtr, y0)], "tn", f32, DT_PAD, 1024, TKW)[:N_HEADS]]
    token = w_in_grads(w_in_t)
    grad_x, dg00 = mm_fused("in_dy0", T // tf, [(d_u, blk(D), w_u, whole(w_u)), (d_v, blk(D), w_v, whole(w_v)), (d_z, blk(D), w_z, whole(w_z)),
                                                (d_xbc, blk(CONV_DIM), W["w_xbc"], whole(W["w_xbc"])), (d_dtr, blk(DT_PAD), W["w_dt"], whole(W["w_dt"]))],
                            NT, [(x, rows_f), (d_h1, rows_f), (g(0, 0) + token[0, 0], vec_f)], [out_f32], [(1, D)], pre_epilogue)
    G["norm_g"] = jnp.stack([jnp.concatenate([dg00, dg01, dg02, dg03], 0), jnp.concatenate([dg10, dg11, dg12, dg13], 0)])
    return loss_acc, grad_x, G


def build_weights(Wf):
    causal = jnp.tril(jnp.ones((CHUNK, CHUNK), bool))
    w_in = Wf["w_in"].astype(bf16)
    pad16 = lambda v: jnp.pad(v.reshape(1, N_HEADS).astype(f32), ((0, 0), (0, DT_PAD - N_HEADS)))
    return {
        "norm_g": Wf["norm_g"],
        "w_uvz": w_in[:, :3 * D], "w_xbc": w_in[:, 3 * D:3 * D + CONV_DIM],
        "w_dt": jnp.pad(w_in[:, 3 * D + CONV_DIM:], ((0, 0), (0, DT_PAD - N_HEADS))),
        "ln_g": Wf["gm_ln_g"].reshape(1, D), "ln_b": Wf["gm_ln_b"].reshape(1, D),
        "wm": jnp.where(causal[None], Wf["gm_ws"], 0).astype(bf16), "bs": Wf["gm_bs"].reshape(GM_HEADS, CHUNK, 1),
        "conv_w": Wf["conv_w"], "conv_b": Wf["conv_b"].reshape(1, CONV_DIM),
        "dtb": pad16(Wf["dt_bias"]), "alog": pad16(Wf["a_log"]), "dsk": pad16(Wf["d_skip"]),
        "gn": Wf["ssm_norm_g"].reshape(1, D),
        "wo4": Wf["wo4"].astype(bf16),
        "pool_w": Wf["pool_w"].astype(bf16), "pool_b": Wf["pool_b"].reshape(1, D), "pool_scale": Wf["pool_scale"].reshape(1, D),
    }


def small_grads(G):
    return {
        "norm_g": G["norm_g"],
        "gm_ln_g": G["ln_g"].reshape(D), "gm_ln_b": G["ln_b"].reshape(D),
        "gm_ws": G["wm"], "gm_bs": G["bs"].reshape(GM_HEADS, CHUNK),
        "conv_w": G["conv_w"], "conv_b": G["conv_b"].reshape(CONV_DIM),
        "dt_bias": G["dtb"][0, :N_HEADS], "a_log": G["alog"][0, :N_HEADS], "d_skip": G["dsk"][0, :N_HEADS],
        "ssm_norm_g": G["gn"].reshape(D),
        "pool_b": G["pool_b"].reshape(4, POOL_GD), "pool_scale": G["pool_scale"].reshape(D),
    }


MESH_ID = pl.DeviceIdType.MESH
ANY = pl.BlockSpec(memory_space=pl.ANY)


DMA_CHUNK_BYTES = 2 << 20
DMA_MAX_CHUNKS = 32


def _pieces(view, axis, align):
    shape = view.shape
    nbytes = math.prod(shape) * jnp.dtype(view.dtype).itemsize
    n = max(1, min(DMA_MAX_CHUNKS, -(-nbytes // DMA_CHUNK_BYTES)))
    rows = shape[axis]
    size = -(-rows // n)
    size = -(-size // align) * align
    out = []
    for s in range(0, rows, size):
        idx = [slice(None)] * len(shape)
        idx[axis] = pl.ds(s, min(size, rows - s))
        out.append(tuple(idx))
    return out


def comm_call(name, operands, out_shapes, plan):
    n_in = len(operands)
    n_out = len(out_shapes)
    n_remote, n_local = plan((0, 0, 0), [None] * n_in, [None] * n_out, True)

    def body(*refs):
        in_refs, out_refs = refs[:n_in], refs[n_in:n_in + n_out]
        send_sems, recv_sems, local_sems = refs[n_in + n_out:]
        me = (lax.axis_index("x"), lax.axis_index("y"), lax.axis_index("c"))
        remote, local = plan(me, in_refs, out_refs, False)
        align = lambda v: 16 if v.dtype == bf16 else 8
        for j, (s, d, axis) in enumerate(local):
            for ix in _pieces(s, axis, align(s)):
                pltpu.make_async_copy(s.at[ix], d.at[ix], local_sems.at[j]).start()
        peers = [tuple((1 - m) if f else m for m, f in zip(me, flip)) for flip, *_ in remote]
        for k, (flip, src, dst, _, axis) in enumerate(remote):
            for ix in _pieces(src, axis, align(src)):
                pltpu.make_async_remote_copy(src_ref=src.at[ix], dst_ref=dst.at[ix], send_sem=send_sems.at[k], recv_sem=recv_sems.at[k],
                                             device_id=peers[k], device_id_type=MESH_ID).start()
        for k, (flip, src, dst, landing, axis) in enumerate(remote):
            pltpu.make_async_remote_copy(src_ref=landing, dst_ref=landing, send_sem=send_sems.at[k], recv_sem=recv_sems.at[k],
                                         device_id=peers[k], device_id_type=MESH_ID).wait_recv()
        for k, (flip, src, dst, landing, axis) in enumerate(remote):
            pltpu.make_async_remote_copy(src_ref=src, dst_ref=dst, send_sem=send_sems.at[k], recv_sem=recv_sems.at[k],
                                         device_id=peers[k], device_id_type=MESH_ID).wait_send()
        for j, (s, d, axis) in enumerate(local):
            pltpu.make_async_copy(s, d, local_sems.at[j]).wait()

    return pl.pallas_call(
        body, name=name, out_shape=list(out_shapes), in_specs=[ANY] * n_in, out_specs=[ANY] * n_out,
        scratch_shapes=[pltpu.SemaphoreType.DMA((n_remote,)), pltpu.SemaphoreType.DMA((n_remote,)), pltpu.SemaphoreType.DMA((max(n_local, 1),))],
    )(*operands)


CHIP_FLIPS = ((1, 0, 0), (0, 1, 0), (1, 1, 0))
PAIR_FLIP = (0, 0, 1)


def gather_two_level(name, halved, whole):
    nh, nw = len(halved), len(whole)
    nf = len(CHIP_FLIPS)

    def body(*refs):
        srcs, outs = refs[:nh + nw], refs[nh + nw:2 * (nh + nw)]
        send_sems, recv_sems, fwd_send, fwd_recv = refs[2 * (nh + nw):]
        me = (lax.axis_index("x"), lax.axis_index("y"), lax.axis_index("c"))
        k, c = 2 * me[0] + me[1], me[2]
        sibling = (me[0], me[1], 1 - c)
        peers = [tuple((1 - m) if fl else m for m, fl in zip(me, flip)) for flip in CHIP_FLIPS]

        def half(ref, which):
            rh = ref.shape[0] // 2
            return ref.at[pl.ds(pl.multiple_of(which * rh, 16), rh), :]

        def ici(a, f):
            src = half(srcs[a], c) if a < nh else srcs[a]
            dst = half(outs[a].at[k], c) if a < nh else outs[a].at[k]
            return pltpu.make_async_remote_copy(src_ref=src, dst_ref=dst, send_sem=send_sems.at[a * nf + f], recv_sem=recv_sems.at[a * nf + f],
                                                device_id=peers[f], device_id_type=MESH_ID)

        def landed(a, f):
            slot = outs[a].at[_chip_of(me, CHIP_FLIPS[f])]
            return half(slot, c) if a < nh else slot

        def forward(a, f, which):
            v = half(outs[a].at[_chip_of(me, CHIP_FLIPS[f])], which)
            return pltpu.make_async_remote_copy(src_ref=v, dst_ref=v, send_sem=fwd_send.at[a * nf + f], recv_sem=fwd_recv.at[a * nf + f],
                                                device_id=sibling, device_id_type=MESH_ID)

        copies = [ici(a, f) for a in range(nh + nw) for f in range(nf)]
        for cp in copies:
            cp.start()
        fwds = []
        for a in range(nh):
            for f in range(nf):
                lv = landed(a, f)
                pltpu.make_async_remote_copy(src_ref=lv, dst_ref=lv, send_sem=send_sems.at[a * nf + f], recv_sem=recv_sems.at[a * nf + f],
                                             device_id=peers[f], device_id_type=MESH_ID).wait_recv()
                fw = forward(a, f, c)
                fw.start()
                fwds.append(fw)
        for a in range(nh, nh + nw):
            for f in range(nf):
                lv = landed(a, f)
                pltpu.make_async_remote_copy(src_ref=lv, dst_ref=lv, send_sem=send_sems.at[a * nf + f], recv_sem=recv_sems.at[a * nf + f],
                                             device_id=peers[f], device_id_type=MESH_ID).wait_recv()
        for a in range(nh):
            for f in range(nf):
                forward(a, f, 1 - c).wait_recv()
        for fw in fwds:
            fw.wait_send()
        for cp in copies:
            cp.wait_send()

    arrs = list(halved) + list(whole)
    n_ici = (nh + nw) * nf
    return pl.pallas_call(
        body, name=name, out_shape=[SDS((N_CHIPS,) + a.shape, a.dtype) for a in arrs], in_specs=[ANY] * len(arrs), out_specs=[ANY] * len(arrs),
        scratch_shapes=[pltpu.SemaphoreType.DMA((n_ici,)), pltpu.SemaphoreType.DMA((n_ici,)),
                        pltpu.SemaphoreType.DMA((nh * nf,)), pltpu.SemaphoreType.DMA((nh * nf,))],
    )(*arrs)


def pair_split_exchange(name, p, rh):
    def plan(me, ins, outs, count):
        if count:
            return 1, 0
        theirs = ins[0].at[:, pl.ds(pl.multiple_of((1 - me[2]) * rh, 8), rh), :]
        return [(PAIR_FLIP, theirs, outs[0], outs[0], 1)], []
    return comm_call(name, [p], [SDS((4, rh, p.shape[2]), p.dtype)], plan)[0]


def scatter_over_chips(name, cs):
    def plan(me, ins, outs, count):
        if count:
            return len(CHIP_FLIPS), 0
        k = 2 * me[0] + me[1]
        remote = []
        for flip in CHIP_FLIPS:
            kp = 2 * ((1 - me[0]) if flip[0] else me[0]) + ((1 - me[1]) if flip[1] else me[1])
            remote.append((flip, ins[0].at[kp], outs[0].at[k], outs[0].at[kp], 0))
        return remote, []
    return comm_call(name, [cs], [SDS(cs.shape, cs.dtype)], plan)[0]


def pair_swap(name, half):
    def plan(me, ins, outs, count):
        if count:
            return 1, 0
        return [(PAIR_FLIP, ins[0], outs[0], outs[0], 0)], []
    return comm_call(name, [half], [SDS(half.shape, half.dtype)], plan)[0]


def _row_tile(rows, cap=512):
    if rows <= cap:
        return rows
    t = cap - cap % 8
    while rows % t:
        t -= 8
    return t


def pair_sum(name, packs, got, c_arr, tile):
    rh = got.shape[1]
    nb = rh // tile

    def kern(c_ref, a_ref, b_ref, o16_ref):
        o16_ref[...] = (a_ref[...] + b_ref[...]).astype(bf16)
    blk = (None, tile, D)
    grid_spec = pltpu.PrefetchScalarGridSpec(
        num_scalar_prefetch=1, grid=(4, nb),
        in_specs=[pl.BlockSpec(blk, lambda s, i, c: (s, c[0] * nb + i, 0)), pl.BlockSpec(blk, lambda s, i, c: (s, i, 0))],
        out_specs=pl.BlockSpec(blk, lambda s, i, c: (s, i, 0)))
    return pl.pallas_call(kern, name=name, grid_spec=grid_spec, out_shape=SDS(got.shape, bf16),
                          compiler_params=pltpu.CompilerParams(dimension_semantics=("parallel", "parallel")))(c_arr, packs, got)


def chip_sum(name, own16, landed16, k_arr, tile):
    rh = own16.shape[1]
    nb = rh // tile

    def kern(k_ref, own_ref, l0, l1, l2, l3, o_ref):
        k = k_ref[0]
        s = None
        for j, lref in enumerate((l0, l1, l2, l3)):
            t = jnp.where(k == j, own_ref[...], lref[...]).astype(f32)
            s = t if s is None else s + t
        o_ref[...] = s
    blk = (None, tile, D)
    land = [pl.BlockSpec(blk, lambda i, k, j=j: (jnp.where(k[0] == j, (j + 1) % N_CHIPS, j), i, 0)) for j in range(N_CHIPS)]
    grid_spec = pltpu.PrefetchScalarGridSpec(
        num_scalar_prefetch=1, grid=(nb,),
        in_specs=[pl.BlockSpec(blk, lambda i, k: (k[0], i, 0))] + land,
        out_specs=pl.BlockSpec((tile, D), lambda i, k: (i, 0)))
    return pl.pallas_call(kern, name=name, grid_spec=grid_spec, out_shape=SDS((rh, D), f32),
                          compiler_params=pltpu.CompilerParams(dimension_semantics=("parallel",)))(k_arr, own16, landed16, landed16, landed16, landed16)


def adamw(name, w, g, m, v):
    R, C = w.shape
    tr = _row_tile(R, 512)

    def kern(w_ref, g_ref, m_ref, v_ref, d_ref, mo_ref, vo_ref):
        gg = g_ref[...]
        mn = ADAM_B1 * m_ref[...] + (1.0 - ADAM_B1) * gg
        vn = ADAM_B2 * v_ref[...] + (1.0 - ADAM_B2) * jnp.square(gg)
        m_hat = mn / (1.0 - ADAM_B1 ** ADAM_STEP)
        v_hat = vn / (1.0 - ADAM_B2 ** ADAM_STEP)
        d_ref[...] = -ADAM_LR * (m_hat / (jnp.sqrt(v_hat) + ADAM_EPS) + ADAM_WD * w_ref[...])
        mo_ref[...] = mn
        vo_ref[...] = vn
    spec = pl.BlockSpec((tr, C), lambda i: (i, 0))
    s = SDS((R, C), f32)
    return pl.pallas_call(kern, name=name, grid=(R // tr,), in_specs=[spec] * 4, out_specs=[spec] * 3, out_shape=[s, s, s],
                          compiler_params=pltpu.CompilerParams(dimension_semantics=("parallel",)))(w, g, m, v)


WEIGHT_NAMES = ("norm_g", "w_in", "gm_ln_g", "gm_ln_b", "gm_ws", "gm_bs", "conv_w", "conv_b", "dt_bias", "a_log", "d_skip",
                "ssm_norm_g", "w_out", "pool_w", "pool_b", "pool_scale", "ffn_w_gate", "ffn_w_up", "ffn_w_down")
SMALL = ("norm_g", "conv_w", "pool_b", "pool_scale")
REPL = ("gm_ln_g", "gm_ln_b", "gm_ws", "gm_bs", "conv_b", "dt_bias", "a_log", "d_skip", "ssm_norm_g")
SMALL_AXIS = {"norm_g": 2, "conv_w": 1, "pool_b": 1, "pool_scale": 0}
N_CHIPS = 4
IN_SH = IN_DIM // N_CHIPS
SMALL_ROWS = 8
REPL_ROWS = 72
E_OUT, E_GATE, E_UP, E_DOWN = 0, 512, 512 + 2 * FF_SH, 512 + 4 * FF_SH
E_POOL = E_DOWN + 2 * FF_SH
E_ROWS, E_TILE = E_POOL + 64, 800
L_SMALL, L_REPL = 0, SMALL_ROWS
L_ROWS, L_TILE = 96, 48
W_ROWS, W_TILE = 1408, 704


def _flat_rows(pieces, rows):
    v = jnp.concatenate([p.reshape(-1) for p in pieces])
    return jnp.pad(v, (0, rows * D - v.shape[0])).reshape(rows, D)


def _shard_small(name, full, k):
    ax = SMALL_AXIS[name]
    n = full.shape[ax] // N_CHIPS
    return lax.slice_in_dim(full, k * n, (k + 1) * n, axis=ax)


def _drop1(name, a):
    return a if name == "norm_g" else a[0]


HBM_SPEC = pl.BlockSpec(memory_space=pltpu.HBM)
SEM_SPEC = pl.BlockSpec(memory_space=pltpu.SEMAPHORE)
SPLIT_EFFECT = pltpu.SideEffectType.DATAFLOW_SIDE_EFFECTING


def _chip_of(me, flip):
    return 2 * ((1 - me[0]) if flip[0] else me[0]) + ((1 - me[1]) if flip[1] else me[1])


def gather_start(name, arrs, after, slotted=False):
    n = len(arrs)
    ncp = n * len(CHIP_FLIPS)

    def body(*refs):
        srcs, lands = refs[:n], refs[n:2 * n]
        send_sems, recv_sems, token = refs[2 * n + 1], refs[2 * n + 2], refs[-1]
        me = (lax.axis_index("x"), lax.axis_index("y"), lax.axis_index("c"))
        k = 2 * me[0] + me[1]
        for a in range(n):
            for f, flip in enumerate(CHIP_FLIPS):
                peer = tuple((1 - m) if fl else m for m, fl in zip(me, flip))
                src = srcs[a].at[_chip_of(me, flip)] if slotted else srcs[a]
                for ix in _pieces(src, 0, 16):
                    pltpu.make_async_remote_copy(src_ref=src.at[ix], dst_ref=lands[a].at[k].at[ix],
                                                 send_sem=send_sems.at[a * len(CHIP_FLIPS) + f], recv_sem=recv_sems.at[a * len(CHIP_FLIPS) + f],
                                                 device_id=peer, device_id_type=MESH_ID).start()
        token[...] = jnp.zeros_like(token)

    land_shapes = [a.shape if slotted else (N_CHIPS,) + a.shape for a in arrs]
    operands = [pltpu.with_memory_space_constraint(a, pltpu.HBM) for a in arrs]
    operands += [pltpu.with_memory_space_constraint(lax.empty(s, a.dtype), pltpu.HBM) for s, a in zip(land_shapes, arrs)]
    out = pl.pallas_call(
        body, name=name,
        out_shape=(pltpu.SemaphoreType.DMA((ncp,)), pltpu.SemaphoreType.DMA((ncp,)), *[pltpu.HBM(a.shape, a.dtype) for a in arrs],
                   *[pltpu.HBM(s, a.dtype) for s, a in zip(land_shapes, arrs)], SDS((8, 128), f32)),
        in_specs=[HBM_SPEC] * (2 * n) + [ANY], out_specs=(SEM_SPEC, SEM_SPEC, *[HBM_SPEC] * (2 * n), pl.BlockSpec(memory_space=pltpu.VMEM)),
        input_output_aliases={i: 2 + i for i in range(2 * n)},
        compiler_params=pltpu.CompilerParams(has_side_effects=SPLIT_EFFECT),
    )(*operands, after)
    return out[0], out[1], out[2:2 + n], out[2 + n:2 + 2 * n], out[-1]


def gather_wait(name, send_sems, recv_sems, thru, lands, after, slotted=False):
    n = len(thru)

    def body(*refs):
        srcs, lands_r = refs[:n], refs[n:2 * n]
        s_sems, r_sems = refs[2 * n], refs[2 * n + 1]
        me = (lax.axis_index("x"), lax.axis_index("y"), lax.axis_index("c"))
        k = 2 * me[0] + me[1]
        for a in range(n):
            for f, flip in enumerate(CHIP_FLIPS):
                peer = tuple((1 - m) if fl else m for m, fl in zip(me, flip))
                idx = a * len(CHIP_FLIPS) + f
                src = srcs[a].at[_chip_of(me, flip)] if slotted else srcs[a]
                pltpu.make_async_remote_copy(src_ref=src, dst_ref=lands_r[a].at[k], send_sem=s_sems.at[idx], recv_sem=r_sems.at[idx],
                                             device_id=peer, device_id_type=MESH_ID).wait_send()
                pltpu.make_async_remote_copy(src_ref=src, dst_ref=lands_r[a].at[_chip_of(me, flip)], send_sem=s_sems.at[idx],
                                             recv_sem=r_sems.at[idx], device_id=peer, device_id_type=MESH_ID).wait_recv()

    out = pl.pallas_call(
        body, name=name, out_shape=tuple(pltpu.HBM(t.shape, t.dtype) for t in (*thru, *lands)),
        in_specs=[HBM_SPEC] * (2 * n) + [SEM_SPEC, SEM_SPEC, ANY], out_specs=tuple([HBM_SPEC] * (2 * n)),
        input_output_aliases={i: i for i in range(2 * n)},
        compiler_params=pltpu.CompilerParams(has_side_effects=SPLIT_EFFECT),
    )(*thru, *lands, send_sems, recv_sems, after)
    return out[:n], out[n:]


def pair_start(name, packs, rh):
    land_shape = (packs.shape[0], rh, packs.shape[2])

    def body(p_ref, land_ref, send_sem, recv_sem, p_thru, land_thru, token):
        me = (lax.axis_index("x"), lax.axis_index("y"), lax.axis_index("c"))
        theirs = p_ref.at[:, pl.ds(pl.multiple_of((1 - me[2]) * rh, 8), rh), :]
        for ix in _pieces(theirs, 1, 8):
            pltpu.make_async_remote_copy(src_ref=theirs.at[ix], dst_ref=land_ref.at[ix], send_sem=send_sem, recv_sem=recv_sem,
                                         device_id=(me[0], me[1], 1 - me[2]), device_id_type=MESH_ID).start()
        token[...] = jnp.zeros_like(token)

    return pl.pallas_call(
        body, name=name,
        out_shape=(pltpu.SemaphoreType.DMA(()), pltpu.SemaphoreType.DMA(()), pltpu.HBM(packs.shape, packs.dtype), pltpu.HBM(land_shape, packs.dtype),
                   SDS((8, 128), f32)),
        in_specs=[HBM_SPEC, HBM_SPEC], out_specs=(SEM_SPEC, SEM_SPEC, HBM_SPEC, HBM_SPEC, pl.BlockSpec(memory_space=pltpu.VMEM)),
        input_output_aliases={0: 2, 1: 3}, compiler_params=pltpu.CompilerParams(has_side_effects=SPLIT_EFFECT),
    )(pltpu.with_memory_space_constraint(packs, pltpu.HBM), pltpu.with_memory_space_constraint(lax.empty(land_shape, packs.dtype), pltpu.HBM))


def pair_wait(name, send_sem, recv_sem, packs, land, after):
    rh = land.shape[1]

    def body(p_ref, land_ref, s_sem, r_sem, after_ref, p_out, land_out):
        me = (lax.axis_index("x"), lax.axis_index("y"), lax.axis_index("c"))
        theirs = p_ref.at[:, pl.ds(pl.multiple_of((1 - me[2]) * rh, 8), rh), :]
        cp = pltpu.make_async_remote_copy(src_ref=theirs, dst_ref=land_ref, send_sem=s_sem, recv_sem=r_sem,
                                          device_id=(me[0], me[1], 1 - me[2]), device_id_type=MESH_ID)
        cp.wait_send()
        cp.wait_recv()

    return pl.pallas_call(
        body, name=name, out_shape=(pltpu.HBM(packs.shape, packs.dtype), pltpu.HBM(land.shape, land.dtype)),
        in_specs=[HBM_SPEC, HBM_SPEC, SEM_SPEC, SEM_SPEC, ANY], out_specs=(HBM_SPEC, HBM_SPEC), input_output_aliases={0: 0, 1: 1},
        compiler_params=pltpu.CompilerParams(has_side_effects=SPLIT_EFFECT),
    )(packs, land, send_sem, recv_sem, after)


def gather_weights(w_sh):
    big = [w_sh["w_in"][0], w_sh["w_out"][0], w_sh["pool_w"][0].reshape(4 * 64, POOL_GD)]
    small_pack = _flat_rows([w_sh[n] for n in SMALL], SMALL_ROWS)
    own = [b.astype(bf16) for b in big] + [small_pack]
    my_k = 2 * lax.axis_index("x") + lax.axis_index("y")
    s_in, s_out, s_pool, s_small = [lax.dynamic_update_slice(s, o[None], (my_k, 0, 0))
                                    for s, o in zip(gather_two_level("gather_weights", own[:3], own[3:]), own)]
    Wf = {n: w_sh[n][0] for n in REPL}
    Wf["w_in"] = s_in.transpose(1, 0, 2).reshape(D, IN_DIM)
    Wf["pool_w"] = s_pool.reshape(N_CHIPS, 4, 64, POOL_GD).transpose(1, 0, 2, 3).reshape(4, POOL_GD, POOL_GD)
    Wf["wo4"] = s_out
    small_shapes = [_drop1(n, w_sh[n]).shape for n in SMALL]
    parts = [_split_rows(s_small[k], small_shapes) for k in range(N_CHIPS)]
    for j, n in enumerate(SMALL):
        Wf[n] = jnp.concatenate([parts[k][j] for k in range(N_CHIPS)], axis=SMALL_AXIS[n])
    return Wf


def pack_early(G):
    slots = [jnp.concatenate([G["wo4"][k], G["wgT4"][0][k], G["wgT4"][1][k], G["wuT4"][0][k], G["wuT4"][1][k], G["wd4"][0][k], G["wd4"][1][k],
                              G["pool_w"][:, k * 64:(k + 1) * 64, :].reshape(64, D)], axis=0) for k in range(N_CHIPS)]
    return jnp.stack(slots)


def pack_w_in(pieces):
    w_in_t = jnp.concatenate(pieces, axis=0)
    return jnp.stack([jnp.pad(w_in_t[k * IN_SH:(k + 1) * IN_SH], ((0, W_ROWS - IN_SH), (0, 0))) for k in range(N_CHIPS)])


def pack_late(G):
    sg = small_grads(G)
    repl = _flat_rows([sg[n] for n in REPL], L_ROWS - SMALL_ROWS)
    return jnp.stack([jnp.concatenate([_flat_rows([_shard_small(n, sg[n], k) for n in SMALL], SMALL_ROWS), repl], axis=0)
                      for k in range(N_CHIPS)])


def unpack_grads(early, w_in_t, late, w_sh):
    g = {"w_out": early[E_OUT:E_GATE], "ffn_w_down": early[E_DOWN:E_POOL], "pool_w": early[E_POOL:E_ROWS],
         "ffn_w_gate": jnp.stack([early[E_GATE + l * FF_SH:E_GATE + (l + 1) * FF_SH].T for l in range(2)]),
         "ffn_w_up": jnp.stack([early[E_UP + l * FF_SH:E_UP + (l + 1) * FF_SH].T for l in range(2)]),
         "w_in": w_in_t[:IN_SH].T}
    small = _split_rows(late[L_SMALL:L_REPL], [_drop1(n, w_sh[n]).shape for n in SMALL])
    repl = _split_rows(late[L_REPL:L_REPL + REPL_ROWS], [w_sh[n][0].shape for n in REPL])
    g.update(zip(SMALL, small))
    g.update(zip(REPL, repl))
    return {n: g[n].reshape(w_sh[n].shape) for n in WEIGHT_NAMES}


def _split_rows(flat2d, shapes):
    v = flat2d.reshape(-1)
    out, off = [], 0
    for s in shapes:
        n = math.prod(s)
        out.append(v[off:off + n].reshape(s))
        off += n
    return out


def kernel(x, norm_g, w_in, gm_ln_g, gm_ln_b, gm_ws, gm_bs, conv_w, conv_b, dt_bias, a_log, d_skip, ssm_norm_g, w_out, pool_w, pool_b, pool_scale, ffn_w_gate, ffn_w_up, ffn_w_down, loss_target, m_norm_g, m_w_in, m_gm_ln_g, m_gm_ln_b, m_gm_ws, m_gm_bs, m_conv_w, m_conv_b, m_dt_bias, m_a_log, m_d_skip, m_ssm_norm_g, m_w_out, m_pool_w, m_pool_b, m_pool_scale, m_ffn_w_gate, m_ffn_w_up, m_ffn_w_down, v_norm_g, v_w_in, v_gm_ln_g, v_gm_ln_b, v_gm_ws, v_gm_bs, v_conv_w, v_conv_b, v_dt_bias, v_a_log, v_d_skip, v_ssm_norm_g, v_w_out, v_pool_w, v_pool_b, v_pool_scale, v_ffn_w_gate, v_ffn_w_up, v_ffn_w_down):
    T = x.shape[1]
    w_sh = dict(zip(WEIGHT_NAMES, (norm_g, w_in, gm_ln_g, gm_ln_b, gm_ws, gm_bs, conv_w, conv_b, dt_bias, a_log, d_skip, ssm_norm_g, w_out,
                                   pool_w, pool_b, pool_scale, ffn_w_gate, ffn_w_up, ffn_w_down)))
    m_sh = dict(zip(WEIGHT_NAMES, (m_norm_g, m_w_in, m_gm_ln_g, m_gm_ln_b, m_gm_ws, m_gm_bs, m_conv_w, m_conv_b, m_dt_bias, m_a_log, m_d_skip,
                                   m_ssm_norm_g, m_w_out, m_pool_w, m_pool_b, m_pool_scale, m_ffn_w_gate, m_ffn_w_up, m_ffn_w_down)))
    v_sh = dict(zip(WEIGHT_NAMES, (v_norm_g, v_w_in, v_gm_ln_g, v_gm_ln_b, v_gm_ws, v_gm_bs, v_conv_w, v_conv_b, v_dt_bias, v_a_log, v_d_skip,
                                   v_ssm_norm_g, v_w_out, v_pool_w, v_pool_b, v_pool_scale, v_ffn_w_gate, v_ffn_w_up, v_ffn_w_down)))

    my_k = 2 * lax.axis_index("x") + lax.axis_index("y")
    ffn_own = [w_sh["ffn_w_gate"].reshape(2 * D, FF_SH).astype(bf16), w_sh["ffn_w_up"].reshape(2 * D, FF_SH).astype(bf16),
               w_sh["ffn_w_down"].reshape(2 * FF_SH, D).astype(bf16)]
    Wf = gather_weights(w_sh)
    send_sems, recv_sems, thru, lands, token = gather_start("gather_ffn_start", ffn_own, Wf["wo4"])
    Wf["norm_g"] = Wf["norm_g"] + token[0, 0]
    W = build_weights(Wf)

    def ffn_weights(after):
        _, landed = gather_wait("gather_ffn_wait", send_sems, recv_sems, thru, lands, after)
        return tuple(lax.dynamic_update_slice(l, o[None], (my_k, 0, 0)) for l, o in zip(landed, ffn_own))

    my_c = lax.axis_index("c")
    c_arr = my_c.astype(jnp.int32).reshape(1)
    k_arr = my_k.astype(jnp.int32).reshape(1)

    def pair_stage(tag, packs, tile):
        got = pair_split_exchange(f"grads{tag}_pair_split", packs, packs.shape[1] // 2)
        return pair_sum(f"grads{tag}_pair_sum", packs, got, c_arr, tile)

    def chip_stage(tag, pair16, landed, tile):
        half = chip_sum(f"grads{tag}_chip_sum", pair16, landed, k_arr, tile)
        other = pair_swap(f"grads{tag}_pair_swap", half)
        return jnp.concatenate([jnp.where(my_c == 0, half, other), jnp.where(my_c == 0, other, half)], axis=0)

    early = {}

    def early_grads(Ge):
        *pair, tok = pair_start("gradsE_pair_start", pack_early(Ge), E_ROWS // 2)
        early.update(pair=pair)
        return tok

    def early_grads_next(after):
        packs, got = pair_wait("gradsE_pair_wait", *early["pair"], after)
        pair16 = pair_sum("gradsE_pair_sum", packs, got, c_arr, E_TILE)
        s_sems, r_sems, thru, lands, tok = gather_start("gradsE_scatter_start", [pair16], jnp.zeros((8, 128), f32), slotted=True)
        early.update(s_sems=s_sems, r_sems=r_sems, thru=thru, lands=lands)
        return tok

    w_group = {}

    def w_in_grads(pieces):
        pair16 = pair_stage("W", pack_w_in(pieces), W_TILE)
        s_sems, r_sems, thru, lands, tok = gather_start("gradsW_scatter_start", [pair16], jnp.zeros((8, 128), f32), slotted=True)
        w_group.update(s_sems=s_sems, r_sems=r_sems, thru=thru, lands=lands)
        return tok

    loss_acc, grad_x, G = local_step(T, x[0], loss_target[0], W, ffn_weights, early_grads, early_grads_next, w_in_grads)
    pair_l = pair_stage("L", pack_late(G), L_TILE)
    total_l = chip_stage("L", pair_l, scatter_over_chips("gradsL_scatter", pair_l), L_TILE)
    (pair_w,), (landed_w,) = gather_wait("gradsW_scatter_wait", w_group["s_sems"], w_group["r_sems"], w_group["thru"], w_group["lands"], total_l,
                                         slotted=True)
    total_w = chip_stage("W", pair_w, landed_w, W_TILE)
    (pair_e,), (landed_e,) = gather_wait("gradsE_scatter_wait", early["s_sems"], early["r_sems"], early["thru"], early["lands"], total_w,
                                         slotted=True)
    total_e = chip_stage("E", pair_e, landed_e, E_TILE)
    grads = unpack_grads(total_e, total_w, total_l, w_sh)

    delta, new_m, new_v = {}, {}, {}
    for n in WEIGHT_NAMES:
        shp = w_sh[n].shape
        two_d = (-1, shp[-1])
        d_, m_, v_ = adamw("adamw_" + n, w_sh[n].reshape(two_d), grads[n].reshape(two_d), m_sh[n].reshape(two_d), v_sh[n].reshape(two_d))
        delta[n], new_m[n], new_v[n] = d_.reshape(shp), m_.reshape(shp), v_.reshape(shp)

    loss = lax.psum(loss_acc[0, 0], ("x", "y", "c"))
    return (loss, grad_x[None], *[grads[n] for n in WEIGHT_NAMES], *[delta[n] for n in WEIGHT_NAMES],
            *[new_m[n] for n in WEIGHT_NAMES], *[new_v[n] for n in WEIGHT_NAMES])
```
